```python
import math
import jax, jax.numpy as jnp
from jax import lax
import numpy as np

D_MODEL = 1024
BATCH = 8
SEQ = 4096
DEPTH = 2

N_A_LAYERS = DEPTH // 2
N_B_LAYERS = DEPTH - N_A_LAYERS
CONV_WIDTH = 2 * D_MODEL
CONV_KERNEL = 31
N_HEADS = 16
QK_NOPE_DIM = 128
QK_ROPE_DIM = 64
V_HEAD_DIM = 128
KV_LORA_RANK = D_MODEL // 4
Q_LORA_RANK = D_MODEL // 2
ROPE_THETA = 10000.0
Q_BLOCK = 128
LN_EPS = 1e-5
RMS_EPS = 1e-6
MASK_VALUE = -1e30

kernel_name = "yoco_conformer_conv_mla_deepnorm"


def layer_norm(x, g, b):
    xf = x.astype(jnp.float32)
    mu = jnp.mean(xf, axis=-1, keepdims=True)
    var = jnp.mean(jnp.square(xf - mu), axis=-1, keepdims=True)
    return ((xf - mu) * lax.rsqrt(var + LN_EPS)).astype(x.dtype) * g + b


def rms_norm(x, g):
    xf = x.astype(jnp.float32)
    ms = jnp.mean(jnp.square(xf), axis=-1, keepdims=True)
    return (xf * lax.rsqrt(ms + RMS_EPS)).astype(x.dtype) * g


def rope_tables(positions, dtype):
    freqs = ROPE_THETA ** (-jnp.arange(0, QK_ROPE_DIM, 2, dtype=jnp.float32) / QK_ROPE_DIM)
    ang = positions.astype(jnp.float32)[..., None] * freqs
    return jnp.cos(ang).astype(dtype), jnp.sin(ang).astype(dtype)


def apply_rope(x, cos, sin):
    x1, x2 = jnp.split(x, 2, axis=-1)
    return jnp.concatenate([x1 * cos - x2 * sin, x2 * cos + x1 * sin], axis=-1)


def conformer_conv_branch(h, w_in, b_in, conv_w, conv_b, norm_g, norm_b, w_out, b_out):
    proj = h @ w_in + b_in
    val, glu_gate, z = jnp.split(proj, 3, axis=-1)
    u = val * jax.nn.sigmoid(glu_gate)
    u = lax.conv_general_dilated(
        u, conv_w[:, None, :], window_strides=(1,), padding=[(CONV_KERNEL - 1, 0)],
        dimension_numbers=('NWC', 'WIO', 'NWC'), feature_group_count=CONV_WIDTH) + conv_b
    u = jax.nn.silu(layer_norm(u, norm_g, norm_b))
    u = u * jax.nn.silu(z)
    return u @ w_out + b_out


def shared_latent_kv(h, w_down, kv_norm_g, w_uk, w_uv, cos, sin):
    ckv_kr = h @ w_down
    c_kv, k_rope = jnp.split(ckv_kr, [KV_LORA_RANK], axis=-1)
    c_kv = rms_norm(c_kv, kv_norm_g)
    k_nope = jnp.einsum('bsr,rhd->bshd', c_kv, w_uk)
    v = jnp.einsum('bsr,rhd->bshd', c_kv, w_uv)
    k_rope = apply_rope(k_rope, cos, sin)
    return k_nope, k_rope, v


def causal_mla_attention(q_nope, q_rope, k_nope, k_rope, v):
    scale = 1.0 / math.sqrt(QK_NOPE_DIM + QK_ROPE_DIM)
    seq = q_nope.shape[1]
    outs = []
    for blk in range(seq // Q_BLOCK):
        q0 = blk * Q_BLOCK
        kend = q0 + Q_BLOCK
        s = (jnp.einsum('bqhd,bkhd->bhqk', q_nope[:, q0:kend], k_nope[:, :kend])
             + jnp.einsum('bqhr,bkr->bhqk', q_rope[:, q0:kend], k_rope[:, :kend]))
        s = s.astype(jnp.float32) * scale
        mask = jnp.arange(kend)[None, :] <= (q0 + jnp.arange(Q_BLOCK))[:, None]
        s = jnp.where(mask, s, MASK_VALUE)
        p = jax.nn.softmax(s, axis=-1).astype(v.dtype)
        outs.append(jnp.einsum('bhqk,bkhd->bqhd', p, v[:, :kend]))
    return jnp.concatenate(outs, axis=1)


def mla_branch(h, k_nope, k_rope, v, w_in, q_norm_g, w_uq, w_out, cos, sin):
    bsz, seq, _ = h.shape
    proj = h @ w_in
    c_q, z = jnp.split(proj, [Q_LORA_RANK], axis=-1)
    q = jnp.einsum('bsr,rhd->bshd', rms_norm(c_q, q_norm_g), w_uq)
    q_nope, q_rope = jnp.split(q, [QK_NOPE_DIM], axis=-1)
    q_rope = apply_rope(q_rope, cos[:, :, None, :], sin[:, :, None, :])
    o = causal_mla_attention(q_nope, q_rope, k_nope, k_rope, v)
    o = o.reshape(bsz, seq, N_HEADS * V_HEAD_DIM) * jax.nn.silu(z)
    return o @ w_out


def _fwd_setup_inputs(seed: int = 0) -> dict:
    key = jax.random.key(seed)
    ks = jax.random.split(key, 20)
    f32 = jnp.float32
    beta = (8.0 * DEPTH) ** -0.25
    E = CONV_WIDTH
    HV = N_HEADS * V_HEAD_DIM

    def dense(k, shape, fan_in, scale=1.0):
        return jax.random.normal(k, shape, f32) * (scale * fan_in ** -0.5)

    def gain(k, shape):
        return 1.0 + 0.02 * jax.random.normal(k, shape, f32)

    def small(k, shape):
        return 0.02 * jax.random.normal(k, shape, f32)

    return {
        "x": jax.random.normal(ks[0], (BATCH, SEQ, D_MODEL), f32),
        "positions": jnp.broadcast_to(jnp.arange(SEQ, dtype=jnp.int32), (BATCH, SEQ)),
        "ln_g": gain(ks[1], (DEPTH, D_MODEL)),
        "ln_b": small(ks[2], (DEPTH, D_MODEL)),
        "a_w_in": dense(ks[3], (N_A_LAYERS, D_MODEL, 3 * E), D_MODEL),
        "a_b_in": small(ks[4], (N_A_LAYERS, 3 * E)),
        "a_conv_w": dense(ks[5], (N_A_LAYERS, CONV_KERNEL, E), CONV_KERNEL),
        "a_conv_b": small(ks[6], (N_A_LAYERS, E)),
        "a_norm_g": gain(ks[7], (N_A_LAYERS, E)),
        "a_norm_b": small(ks[8], (N_A_LAYERS, E)),
        "a_w_out": dense(ks[9], (N_A_LAYERS, E, D_MODEL), E, beta),
        "a_b_out": small(ks[10], (N_A_LAYERS, D_MODEL)),
        "kv_w_down": dense(ks[11], (D_MODEL, KV_LORA_RANK + QK_ROPE_DIM), D_MODEL),
        "kv_norm_g": gain(ks[12], (KV_LORA_RANK,)),
        "kv_w_uk": dense(ks[13], (KV_LORA_RANK, N_HEADS, QK_NOPE_DIM), KV_LORA_RANK),
        "kv_w_uv": dense(ks[14], (KV_LORA_RANK, N_HEADS, V_HEAD_DIM), KV_LORA_RANK),
        "b_w_in": dense(ks[15], (N_B_LAYERS, D_MODEL, Q_LORA_RANK + HV), D_MODEL),
        "b_q_norm_g": gain(ks[16], (N_B_LAYERS, Q_LORA_RANK)),
        "b_w_uq": dense(ks[17], (N_B_LAYERS, Q_LORA_RANK, N_HEADS, QK_NOPE_DIM + QK_ROPE_DIM), Q_LORA_RANK),
        "b_w_out": dense(ks[18], (N_B_LAYERS, HV, D_MODEL), HV, beta),
    }


def _fwd_reference(x, positions, ln_g, ln_b, a_w_in, a_b_in, a_conv_w, a_conv_b, a_norm_g, a_norm_b,
              a_w_out, a_b_out, kv_w_down, kv_norm_g, kv_w_uk, kv_w_uv, b_w_in, b_q_norm_g,
              b_w_uq, b_w_out):
    alpha = (2.0 * DEPTH) ** 0.25
    cos, sin = rope_tables(positions, x.dtype)
    h = x
    k_nope = k_rope = v = None
    for layer in range(DEPTH):
        if layer < N_A_LAYERS:
            i = layer
            y = conformer_conv_branch(h, a_w_in[i], a_b_in[i], a_conv_w[i], a_conv_b[i],
                                      a_norm_g[i], a_norm_b[i], a_w_out[i], a_b_out[i])
        else:
            if layer == N_A_LAYERS:
                k_nope, k_rope, v = shared_latent_kv(h, kv_w_down, kv_norm_g, kv_w_uk, kv_w_uv, cos, sin)
            j = layer - N_A_LAYERS
            y = mla_branch(h, k_nope, k_rope, v, b_w_in[j], b_q_norm_g[j], b_w_uq[j], b_w_out[j], cos, sin)
        h = layer_norm(alpha * h + y, ln_g[layer], ln_b[layer])
    return h


import jax as _jax
import jax.numpy as _jnp

TWIN_FORMAT = 'train_step'
FWD_PARAMS = ['x', 'positions', 'ln_g', 'ln_b', 'a_w_in', 'a_b_in', 'a_conv_w', 'a_conv_b', 'a_norm_g', 'a_norm_b', 'a_w_out', 'a_b_out', 'kv_w_down', 'kv_norm_g', 'kv_w_uk', 'kv_w_uv', 'b_w_in', 'b_q_norm_g', 'b_w_uq', 'b_w_out']
TWIN_WEIGHTS = ['ln_g', 'ln_b', 'a_w_in', 'a_b_in', 'a_conv_w', 'a_conv_b', 'a_norm_g', 'a_norm_b', 'a_w_out', 'a_b_out', 'kv_w_down', 'kv_norm_g', 'kv_w_uk', 'kv_w_uv', 'b_w_in', 'b_q_norm_g', 'b_w_uq', 'b_w_out']
TWIN_DIFF_INPUT = 'x'
TWIN_INPUTS = ['x', 'positions', 'ln_g', 'ln_b', 'a_w_in', 'a_b_in', 'a_conv_w', 'a_conv_b', 'a_norm_g', 'a_norm_b', 'a_w_out', 'a_b_out', 'kv_w_down', 'kv_norm_g', 'kv_w_uk', 'kv_w_uv', 'b_w_in', 'b_q_norm_g', 'b_w_uq', 'b_w_out', 'loss_target', 'm_ln_g', 'm_ln_b', 'm_a_w_in', 'm_a_b_in', 'm_a_conv_w', 'm_a_conv_b', 'm_a_norm_g', 'm_a_norm_b', 'm_a_w_out', 'm_a_b_out', 'm_kv_w_down', 'm_kv_norm_g', 'm_kv_w_uk', 'm_kv_w_uv', 'm_b_w_in', 'm_b_q_norm_g', 'm_b_w_uq', 'm_b_w_out', 'v_ln_g', 'v_ln_b', 'v_a_w_in', 'v_a_b_in', 'v_a_conv_w', 'v_a_conv_b', 'v_a_norm_g', 'v_a_norm_b', 'v_a_w_out', 'v_a_b_out', 'v_kv_w_down', 'v_kv_norm_g', 'v_kv_w_uk', 'v_kv_w_uv', 'v_b_w_in', 'v_b_q_norm_g', 'v_b_w_uq', 'v_b_w_out']
TWIN_OUTPUTS = ['loss', 'grad_x', 'grad_ln_g', 'grad_ln_b', 'grad_a_w_in', 'grad_a_b_in', 'grad_a_conv_w', 'grad_a_conv_b', 'grad_a_norm_g', 'grad_a_norm_b', 'grad_a_w_out', 'grad_a_b_out', 'grad_kv_w_down', 'grad_kv_norm_g', 'grad_kv_w_uk', 'grad_kv_w_uv', 'grad_b_w_in', 'grad_b_q_norm_g', 'grad_b_w_uq', 'grad_b_w_out', 'delta_ln_g', 'delta_ln_b', 'delta_a_w_in', 'delta_a_b_in', 'delta_a_conv_w', 'delta_a_conv_b', 'delta_a_norm_g', 'delta_a_norm_b', 'delta_a_w_out', 'delta_a_b_out', 'delta_kv_w_down', 'delta_kv_norm_g', 'delta_kv_w_uk', 'delta_kv_w_uv', 'delta_b_w_in', 'delta_b_q_norm_g', 'delta_b_w_uq', 'delta_b_w_out', 'new_m_ln_g', 'new_m_ln_b', 'new_m_a_w_in', 'new_m_a_b_in', 'new_m_a_conv_w', 'new_m_a_conv_b', 'new_m_a_norm_g', 'new_m_a_norm_b', 'new_m_a_w_out', 'new_m_a_b_out', 'new_m_kv_w_down', 'new_m_kv_norm_g', 'new_m_kv_w_uk', 'new_m_kv_w_uv', 'new_m_b_w_in', 'new_m_b_q_norm_g', 'new_m_b_w_uq', 'new_m_b_w_out', 'new_v_ln_g', 'new_v_ln_b', 'new_v_a_w_in', 'new_v_a_b_in', 'new_v_a_conv_w', 'new_v_a_conv_b', 'new_v_a_norm_g', 'new_v_a_norm_b', 'new_v_a_w_out', 'new_v_a_b_out', 'new_v_kv_w_down', 'new_v_kv_norm_g', 'new_v_kv_w_uk', 'new_v_kv_w_uv', 'new_v_b_w_in', 'new_v_b_q_norm_g', 'new_v_b_w_uq', 'new_v_b_w_out']
TWIN_LEAF_KINDS = {'loss': 'loss', 'grad_x': 'grad_x', 'grad_ln_g': 'grad_w', 'grad_ln_b': 'grad_w', 'grad_a_w_in': 'grad_w', 'grad_a_b_in': 'grad_w', 'grad_a_conv_w': 'grad_w', 'grad_a_conv_b': 'grad_w', 'grad_a_norm_g': 'grad_w', 'grad_a_norm_b': 'grad_w', 'grad_a_w_out': 'grad_w', 'grad_a_b_out': 'grad_w', 'grad_kv_w_down': 'grad_w', 'grad_kv_norm_g': 'grad_w', 'grad_kv_w_uk': 'grad_w', 'grad_kv_w_uv': 'grad_w', 'grad_b_w_in': 'grad_w', 'grad_b_q_norm_g': 'grad_w', 'grad_b_w_uq': 'grad_w', 'grad_b_w_out': 'grad_w', 'delta_ln_g': 'delta_w', 'delta_ln_b': 'delta_w', 'delta_a_w_in': 'delta_w', 'delta_a_b_in': 'delta_w', 'delta_a_conv_w': 'delta_w', 'delta_a_conv_b': 'delta_w', 'delta_a_norm_g': 'delta_w', 'delta_a_norm_b': 'delta_w', 'delta_a_w_out': 'delta_w', 'delta_a_b_out': 'delta_w', 'delta_kv_w_down': 'delta_w', 'delta_kv_norm_g': 'delta_w', 'delta_kv_w_uk': 'delta_w', 'delta_kv_w_uv': 'delta_w', 'delta_b_w_in': 'delta_w', 'delta_b_q_norm_g': 'delta_w', 'delta_b_w_uq': 'delta_w', 'delta_b_w_out': 'delta_w', 'new_m_ln_g': 'new_m', 'new_m_ln_b': 'new_m', 'new_m_a_w_in': 'new_m', 'new_m_a_b_in': 'new_m', 'new_m_a_conv_w': 'new_m', 'new_m_a_conv_b': 'new_m', 'new_m_a_norm_g': 'new_m', 'new_m_a_norm_b': 'new_m', 'new_m_a_w_out': 'new_m', 'new_m_a_b_out': 'new_m', 'new_m_kv_w_down': 'new_m', 'new_m_kv_norm_g': 'new_m', 'new_m_kv_w_uk': 'new_m', 'new_m_kv_w_uv': 'new_m', 'new_m_b_w_in': 'new_m', 'new_m_b_q_norm_g': 'new_m', 'new_m_b_w_uq': 'new_m', 'new_m_b_w_out': 'new_m', 'new_v_ln_g': 'new_v', 'new_v_ln_b': 'new_v', 'new_v_a_w_in': 'new_v', 'new_v_a_b_in': 'new_v', 'new_v_a_conv_w': 'new_v', 'new_v_a_conv_b': 'new_v', 'new_v_a_norm_g': 'new_v', 'new_v_a_norm_b': 'new_v', 'new_v_a_w_out': 'new_v', 'new_v_a_b_out': 'new_v', 'new_v_kv_w_down': 'new_v', 'new_v_kv_norm_g': 'new_v', 'new_v_kv_w_uk': 'new_v', 'new_v_kv_w_uv': 'new_v', 'new_v_b_w_in': 'new_v', 'new_v_b_q_norm_g': 'new_v', 'new_v_b_w_uq': 'new_v', 'new_v_b_w_out': 'new_v'}


def _forward(args):
    return _fwd_reference(*[args[k] for k in FWD_PARAMS])


def _output_shape():
    out = _jax.eval_shape(lambda: _forward(_fwd_setup_inputs(0)))
    return out.shape, out.dtype

N_MICROBATCH = 1
ADAM_LR = 0.001
ADAM_B1 = 0.9
ADAM_B2 = 0.999
ADAM_EPS = 1e-08
ADAM_WD = 0.01
ADAM_STEP = 10
PER_EXAMPLE_BATCH_AXIS = {'x': 0, 'positions': 0, 'loss_target': 0}
SHARED_INPUTS = []
_WEIGHT_DTYPES = {'ln_g': _jnp.float32, 'ln_b': _jnp.float32, 'a_w_in': _jnp.float32, 'a_b_in': _jnp.float32, 'a_conv_w': _jnp.float32, 'a_conv_b': _jnp.float32, 'a_norm_g': _jnp.float32, 'a_norm_b': _jnp.float32, 'a_w_out': _jnp.float32, 'a_b_out': _jnp.float32, 'kv_w_down': _jnp.float32, 'kv_norm_g': _jnp.float32, 'kv_w_uk': _jnp.float32, 'kv_w_uv': _jnp.float32, 'b_w_in': _jnp.float32, 'b_q_norm_g': _jnp.float32, 'b_w_uq': _jnp.float32, 'b_w_out': _jnp.float32}
MOMENT_SCALE = {'ln_g': 2.264780e+01, 'ln_b': 7.126686e-01, 'a_w_in': 1.383212e-02, 'a_b_in': 1.543519e-02, 'a_conv_w': 1.632803e-02, 'a_conv_b': 3.557395e-02, 'a_norm_g': 1.884033e-02, 'a_norm_b': 1.784247e-02, 'a_w_out': 4.551515e-02, 'a_b_out': 3.317916e-01, 'kv_w_down': 1.902294e-02, 'kv_norm_g': 1.868549e-02, 'kv_w_uk': 4.011393e-03, 'kv_w_uv': 5.889296e-03, 'b_w_in': 6.949710e-03, 'b_q_norm_g': 9.708793e-03, 'b_w_uq': 3.907421e-03, 'b_w_out': 1.670238e-02}


def _to_microbatches(a, axis):
    t = _jnp.moveaxis(a, axis, 0)
    t = t.reshape((N_MICROBATCH, t.shape[0] // N_MICROBATCH) + t.shape[1:])
    return _jnp.moveaxis(t, 1, axis + 1)


def setup_inputs(seed: int = 0) -> dict:
    inp = _fwd_setup_inputs(seed)
    key = _jax.random.fold_in(_jax.random.key(seed), 7919)
    shape, _ = _output_shape()
    out = dict(inp)
    out["loss_target"] = _jax.random.normal(_jax.random.fold_in(key, 0), shape, _jnp.float32)
    for i, name in enumerate(TWIN_WEIGHTS):
        w = inp[name].astype(_jnp.float32)
        if MOMENT_SCALE is None:
            s = _jnp.sqrt(_jnp.mean(_jnp.square(w)) + 1e-30)
        else:
            s = MOMENT_SCALE[name]
        km, kv = _jax.random.split(_jax.random.fold_in(key, i + 1))
        out[name] = w
        out["m_" + name] = s * _jax.random.normal(km, w.shape, _jnp.float32)
        out["v_" + name] = (s * s) * _jax.random.uniform(kv, w.shape, _jnp.float32, 0.5, 1.5)
    if N_MICROBATCH > 1:
        for name, axis in PER_EXAMPLE_BATCH_AXIS.items():
            out[name] = _to_microbatches(out[name], axis)
    return {'x': out['x'], 'positions': out['positions'], 'ln_g': out['ln_g'], 'ln_b': out['ln_b'], 'a_w_in': out['a_w_in'], 'a_b_in': out['a_b_in'], 'a_conv_w': out['a_conv_w'], 'a_conv_b': out['a_conv_b'], 'a_norm_g': out['a_norm_g'], 'a_norm_b': out['a_norm_b'], 'a_w_out': out['a_w_out'], 'a_b_out': out['a_b_out'], 'kv_w_down': out['kv_w_down'], 'kv_norm_g': out['kv_norm_g'], 'kv_w_uk': out['kv_w_uk'], 'kv_w_uv': out['kv_w_uv'], 'b_w_in': out['b_w_in'], 'b_q_norm_g': out['b_q_norm_g'], 'b_w_uq': out['b_w_uq'], 'b_w_out': out['b_w_out'], 'loss_target': out['loss_target'], 'm_ln_g': out['m_ln_g'], 'm_ln_b': out['m_ln_b'], 'm_a_w_in': out['m_a_w_in'], 'm_a_b_in': out['m_a_b_in'], 'm_a_conv_w': out['m_a_conv_w'], 'm_a_conv_b': out['m_a_conv_b'], 'm_a_norm_g': out['m_a_norm_g'], 'm_a_norm_b': out['m_a_norm_b'], 'm_a_w_out': out['m_a_w_out'], 'm_a_b_out': out['m_a_b_out'], 'm_kv_w_down': out['m_kv_w_down'], 'm_kv_norm_g': out['m_kv_norm_g'], 'm_kv_w_uk': out['m_kv_w_uk'], 'm_kv_w_uv': out['m_kv_w_uv'], 'm_b_w_in': out['m_b_w_in'], 'm_b_q_norm_g': out['m_b_q_norm_g'], 'm_b_w_uq': out['m_b_w_uq'], 'm_b_w_out': out['m_b_w_out'], 'v_ln_g': out['v_ln_g'], 'v_ln_b': out['v_ln_b'], 'v_a_w_in': out['v_a_w_in'], 'v_a_b_in': out['v_a_b_in'], 'v_a_conv_w': out['v_a_conv_w'], 'v_a_conv_b': out['v_a_conv_b'], 'v_a_norm_g': out['v_a_norm_g'], 'v_a_norm_b': out['v_a_norm_b'], 'v_a_w_out': out['v_a_w_out'], 'v_a_b_out': out['v_a_b_out'], 'v_kv_w_down': out['v_kv_w_down'], 'v_kv_norm_g': out['v_kv_norm_g'], 'v_kv_w_uk': out['v_kv_w_uk'], 'v_kv_w_uv': out['v_kv_w_uv'], 'v_b_w_in': out['v_b_w_in'], 'v_b_q_norm_g': out['v_b_q_norm_g'], 'v_b_w_uq': out['v_b_w_uq'], 'v_b_w_out': out['v_b_w_out']}


def _loss(weights, diff, rest, loss_target):
    with _jax.named_scope("forward"):
        args = {**rest, TWIN_DIFF_INPUT: diff, **{k: w.astype(_WEIGHT_DTYPES[k]) for k, w in weights.items()}}
        y = _forward(args)
    with _jax.named_scope("loss_head"):
        err = _jnp.square(y.astype(_jnp.float32) - loss_target)
        return 0.5 * _jnp.sum(_jnp.mean(err, axis=-1)) if err.ndim else 0.5 * err


def _adamw(w, g, m, v):
    m = ADAM_B1 * m + (1.0 - ADAM_B1) * g
    v = ADAM_B2 * v + (1.0 - ADAM_B2) * _jnp.square(g)
    m_hat = m / (1.0 - ADAM_B1 ** ADAM_STEP)
    v_hat = v / (1.0 - ADAM_B2 ** ADAM_STEP)
    delta = -ADAM_LR * (m_hat / (_jnp.sqrt(v_hat) + ADAM_EPS) + ADAM_WD * w)
    return delta, m, v


def reference(x, positions, ln_g, ln_b, a_w_in, a_b_in, a_conv_w, a_conv_b, a_norm_g, a_norm_b, a_w_out, a_b_out, kv_w_down, kv_norm_g, kv_w_uk, kv_w_uv, b_w_in, b_q_norm_g, b_w_uq, b_w_out, loss_target, m_ln_g, m_ln_b, m_a_w_in, m_a_b_in, m_a_conv_w, m_a_conv_b, m_a_norm_g, m_a_norm_b, m_a_w_out, m_a_b_out, m_kv_w_down, m_kv_norm_g, m_kv_w_uk, m_kv_w_uv, m_b_w_in, m_b_q_norm_g, m_b_w_uq, m_b_w_out, v_ln_g, v_ln_b, v_a_w_in, v_a_b_in, v_a_conv_w, v_a_conv_b, v_a_norm_g, v_a_norm_b, v_a_w_out, v_a_b_out, v_kv_w_down, v_kv_norm_g, v_kv_w_uk, v_kv_w_uv, v_b_w_in, v_b_q_norm_g, v_b_w_uq, v_b_w_out):
    given = dict(x=x, positions=positions, ln_g=ln_g, ln_b=ln_b, a_w_in=a_w_in, a_b_in=a_b_in, a_conv_w=a_conv_w, a_conv_b=a_conv_b, a_norm_g=a_norm_g, a_norm_b=a_norm_b, a_w_out=a_w_out, a_b_out=a_b_out, kv_w_down=kv_w_down, kv_norm_g=kv_norm_g, kv_w_uk=kv_w_uk, kv_w_uv=kv_w_uv, b_w_in=b_w_in, b_q_norm_g=b_q_norm_g, b_w_uq=b_w_uq, b_w_out=b_w_out, loss_target=loss_target, m_ln_g=m_ln_g, m_ln_b=m_ln_b, m_a_w_in=m_a_w_in, m_a_b_in=m_a_b_in, m_a_conv_w=m_a_conv_w, m_a_conv_b=m_a_conv_b, m_a_norm_g=m_a_norm_g, m_a_norm_b=m_a_norm_b, m_a_w_out=m_a_w_out, m_a_b_out=m_a_b_out, m_kv_w_down=m_kv_w_down, m_kv_norm_g=m_kv_norm_g, m_kv_w_uk=m_kv_w_uk, m_kv_w_uv=m_kv_w_uv, m_b_w_in=m_b_w_in, m_b_q_norm_g=m_b_q_norm_g, m_b_w_uq=m_b_w_uq, m_b_w_out=m_b_w_out, v_ln_g=v_ln_g, v_ln_b=v_ln_b, v_a_w_in=v_a_w_in, v_a_b_in=v_a_b_in, v_a_conv_w=v_a_conv_w, v_a_conv_b=v_a_conv_b, v_a_norm_g=v_a_norm_g, v_a_norm_b=v_a_norm_b, v_a_w_out=v_a_w_out, v_a_b_out=v_a_b_out, v_kv_w_down=v_kv_w_down, v_kv_norm_g=v_kv_norm_g, v_kv_w_uk=v_kv_w_uk, v_kv_w_uv=v_kv_w_uv, v_b_w_in=v_b_w_in, v_b_q_norm_g=v_b_q_norm_g, v_b_w_uq=v_b_w_uq, v_b_w_out=v_b_w_out)
    weights = {n: given[n] for n in TWIN_WEIGHTS}
    shared = {n: given[n] for n in SHARED_INPUTS}
    per_example = {n: given[n] for n in ['x', 'positions']}
    grad_fn = _jax.value_and_grad(_loss, argnums=(0, 1))

    def one_microbatch(ex, loss_target):
        ex = dict(ex)
        diff = ex.pop(TWIN_DIFF_INPUT)
        return grad_fn(weights, diff, {**shared, **ex}, loss_target)

    if N_MICROBATCH == 1:
        loss, (grad_w, grad_x) = one_microbatch(per_example, given["loss_target"])
    else:
        def body(carry, xs):
            loss_sum, grad_sum = carry
            l_k, (gw_k, gx_k) = one_microbatch(xs[0], xs[1])
            with _jax.named_scope("update"):
                return (loss_sum + l_k, _jax.tree.map(_jnp.add, grad_sum, gw_k)), gx_k

        init = (_jnp.zeros((), _jnp.float32), _jax.tree.map(_jnp.zeros_like, weights))
        (loss, grad_w), grad_x = _jax.lax.scan(body, init, (per_example, given["loss_target"]))
    with _jax.named_scope("update"):
        delta_w, new_m, new_v = {}, {}, {}
        for n in TWIN_WEIGHTS:
            delta_w[n], new_m[n], new_v[n] = _adamw(weights[n], grad_w[n], given["m_" + n], given["v_" + n])
    return (loss, grad_x, *[grad_w[n] for n in TWIN_WEIGHTS], *[delta_w[n] for n in TWIN_WEIGHTS],
            *[new_m[n] for n in TWIN_WEIGHTS], *[new_v[n] for n in TWIN_WEIGHTS])
```

```python
import functools
import math

import jax
import jax.numpy as jnp
from jax import lax
from jax.experimental import pallas as pl
from jax.experimental.pallas import tpu as pltpu

F32, BF16 = jnp.float32, jnp.bfloat16
NN = (((1,), (0,)), ((), ()))
NT = (((1,), (1,)), ((), ()))
TN = (((0,), (0,)), ((), ()))
MESH = pl.DeviceIdType.MESH
ANY = pl.BlockSpec(memory_space=pl.ANY)

LANES = 128
BF16_ROWS = 16
VMEM_LIMIT = 56 * 1024 * 1024
N_CHIPS = 4
LN_EPS = 1e-5
RMS_EPS = 1e-6
MASK_VALUE = -1e30
ROPE_THETA = 10000.0
ROPE_DIM = 64
ROPE_HALF = ROPE_DIM // 2
ADAM_LR, ADAM_B1, ADAM_B2, ADAM_EPS, ADAM_WD, ADAM_STEP = 0.001, 0.9, 0.999, 1e-08, 0.01, 10

TM, TN_, TK = 512, 512, 512
TM_WIDE = 256
TQ = 512
CONV_HALO = 32
CONV_LC = 512
CONV_RC = 32
ADAM_ROWS = 64


def _dot(a, b, dims):
    return lax.dot_general(a.astype(BF16), b.astype(BF16), dims, preferred_element_type=F32)


def _sig(x):
    return 1.0 / (1.0 + jnp.exp(-x))


def _params(n_axes):
    return pltpu.CompilerParams(dimension_semantics=("arbitrary",) * n_axes, vmem_limit_bytes=VMEM_LIMIT)


def _gcd(*v):
    return functools.reduce(math.gcd, v)


def _row_mm(name, a_segs, b, *, nt, tm, tn, tk, outs, epi, epi_ins=()):
    M = a_segs[0][0][0].shape[0]
    N = b.shape[0] if nt else b.shape[1]
    nkb = [arrs[0].shape[1] // tk for arrs, _ in a_segs]
    koff = [sum(nkb[:s]) for s in range(len(nkb))]
    ni, nj, nk = M // tm, N // tn, sum(nkb)
    assert M % tm == 0 and N % tn == 0 and all(arrs[0].shape[1] % tk == 0 for arrs, _ in a_segs), name
    assert (b.shape[1] if nt else b.shape[0]) == nk * tk, name

    def spec_of(shape, kind):
        if isinstance(kind, pl.BlockSpec):
            return kind
        if kind == 'tile':
            return pl.BlockSpec((tm, tn), lambda i, j, k: (i, j))
        if kind == 'row':
            return pl.BlockSpec((tm, shape[1]), lambda i, j, k: (i, 0))
        if kind == 'col':
            return pl.BlockSpec((1, tn), lambda i, j, k: (0, j))
        assert kind == 'acc' and nj == 1, name
        return pl.BlockSpec(shape, lambda i, j, k: (0,) * len(shape))

    in_specs, operands = [], []
    for s, (arrs, _) in enumerate(a_segs):
        for arr in arrs:
            in_specs.append(pl.BlockSpec(
                (tm, tk), lambda i, j, k, s=s: (i, jnp.clip(k - koff[s], 0, nkb[s] - 1))))
            operands.append(arr)
    in_specs.append(pl.BlockSpec((tn, tk), lambda i, j, k: (j, k)) if nt
                    else pl.BlockSpec((tk, tn), lambda i, j, k: (k, j)))
    operands.append(b)
    for arr, kind in epi_ins:
        in_specs.append(spec_of(arr.shape, kind))
        operands.append(arr)
    out_specs = [spec_of(shape, kind) for shape, _, kind in outs]
    out_shape = [jax.ShapeDtypeStruct(shape, dtype) for shape, dtype, _ in outs]
    n_seg_refs = [len(arrs) for arrs, _ in a_segs]

    def body(*refs):
        pos = 0
        seg_refs = []
        for n in n_seg_refs:
            seg_refs.append(refs[pos:pos + n])
            pos += n
        b_ref = refs[pos]
        e_refs = refs[pos + 1:pos + 1 + len(epi_ins)]
        o_refs = refs[pos + 1 + len(epi_ins):pos + 1 + len(epi_ins) + len(outs)]
        acc_ref = refs[-1]
        i, j, k = pl.program_id(0), pl.program_id(1), pl.program_id(2)

        @pl.when(k == 0)
        def _():
            acc_ref[...] = jnp.zeros_like(acc_ref)

        for s, ((_, fn), rs) in enumerate(zip(a_segs, seg_refs)):
            def accumulate(fn=fn, rs=rs):
                a = rs[0][...] if fn is None else fn(*[r[...] for r in rs])
                acc_ref[...] += _dot(a, b_ref[...], NT if nt else NN)
            if len(a_segs) == 1:
                accumulate()
            else:
                pl.when(jnp.logical_and(k >= koff[s], k < koff[s] + nkb[s]))(accumulate)

        @pl.when(k == nk - 1)
        def _():
            res = epi(acc_ref[...], [r[...] for r in e_refs], i, j)
            for o_ref, (_, _, kind), r in zip(o_refs, outs, res):
                if isinstance(kind, str) and kind == 'acc':
                    @pl.when(i == 0)
                    def _(o_ref=o_ref, r=r):
                        o_ref[...] = r

                    @pl.when(i > 0)
                    def _(o_ref=o_ref, r=r):
                        o_ref[...] += r
                else:
                    o_ref[...] = r.astype(o_ref.dtype)

    return pl.pallas_call(
        body, name=name, grid=(ni, nj, nk), in_specs=in_specs, out_specs=out_specs, out_shape=out_shape,
        scratch_shapes=[pltpu.VMEM((tm, tn), F32)], compiler_params=_params(3),
    )(*operands)


def _tn_mm(name, a_arrs, a_fn, b_segs, *, tn, tk, out_dtype, shard_major=False, colsum=False):
    T, M = a_arrs[0].shape
    nbj = [arrs[0].shape[1] // tn for arrs, _ in b_segs]
    joff = [sum(nbj[:s]) for s in range(len(nbj))]
    nj, nk = sum(nbj), T // tk
    N = nj * tn
    assert T % tk == 0 and all(arrs[0].shape[1] % tn == 0 for arrs, _ in b_segs), name

    in_specs = [pl.BlockSpec((tk, M), lambda j, k: (k, 0)) for _ in a_arrs]
    operands = list(a_arrs)
    for s, (arrs, _) in enumerate(b_segs):
        for arr in arrs:
            in_specs.append(pl.BlockSpec(
                (tk, tn), lambda j, k, s=s: (k, jnp.clip(j - joff[s], 0, nbj[s] - 1))))
            operands.append(arr)
    if shard_major:
        per = (N // N_CHIPS) // tn
        assert per * tn * N_CHIPS == N, name
        out_shape = [jax.ShapeDtypeStruct((N_CHIPS, M, N // N_CHIPS), out_dtype)]
        out_specs = [pl.BlockSpec((1, M, tn), lambda j, k: (j // per, 0, j % per))]
    else:
        out_shape = [jax.ShapeDtypeStruct((M, N), out_dtype)]
        out_specs = [pl.BlockSpec((M, tn), lambda j, k: (0, j))]
    if colsum:
        out_shape.append(jax.ShapeDtypeStruct((1, N), F32))
        out_specs.append(pl.BlockSpec((1, tn), lambda j, k: (0, j)))
    n_a = len(a_arrs)
    n_seg_refs = [len(arrs) for arrs, _ in b_segs]

    def body(*refs):
        a_refs = refs[:n_a]
        pos = n_a
        seg_refs = []
        for n in n_seg_refs:
            seg_refs.append(refs[pos:pos + n])
            pos += n
        o_ref = refs[pos]
        cs_ref = refs[pos + 1] if colsum else None
        acc_ref = refs[-1]
        j, k = pl.program_id(0), pl.program_id(1)

        @pl.when(k == 0)
        def _():
            acc_ref[...] = jnp.zeros_like(acc_ref)
            if colsum:
                cs_ref[...] = jnp.zeros_like(cs_ref)

        for s, ((_, fn), rs) in enumerate(zip(b_segs, seg_refs)):
            def accumulate(fn=fn, rs=rs):
                a = a_refs[0][...] if a_fn is None else a_fn(*[r[...] for r in a_refs])
                bt = rs[0][...] if fn is None else fn(*[r[...] for r in rs])
                acc_ref[...] += _dot(a, bt, TN)
                if colsum:
                    cs_ref[...] += jnp.sum(bt.astype(F32), axis=0, keepdims=True)
            if len(b_segs) == 1:
                accumulate()
            else:
                pl.when(jnp.logical_and(j >= joff[s], j < joff[s] + nbj[s]))(accumulate)

        @pl.when(k == nk - 1)
        def _():
            if shard_major:
                o_ref[0] = acc_ref[...].astype(o_ref.dtype)
            else:
                o_ref[...] = acc_ref[...].astype(o_ref.dtype)

    res = pl.pallas_call(
        body, name=name, grid=(nj, nk), in_specs=in_specs, out_specs=out_specs, out_shape=out_shape,
        scratch_shapes=[pltpu.VMEM((M, tn), F32)], compiler_params=_params(2),
    )(*operands)
    return res if colsum else res[0]


def _silu(z):
    return z * _sig(z)


def _silu_grad(z):
    s = _sig(z)
    return s * (1.0 + z * (1.0 - s))


def _gate(o, z):
    return o * _silu(z)


def _ln_stats(r):
    mu = jnp.mean(r, axis=1, keepdims=True)
    xc = r - mu
    var = jnp.mean(xc * xc, axis=1, keepdims=True)
    rstd = lax.rsqrt(var + LN_EPS)
    return xc * rstd, rstd


def _ln_bwd(dy, xhat, rstd, g):
    dxh = dy * g
    m1 = jnp.mean(dxh, axis=1, keepdims=True)
    m2 = jnp.mean(dxh * xhat, axis=1, keepdims=True)
    return (rstd * (dxh - m1 - xhat * m2), jnp.sum(dy * xhat, axis=0, keepdims=True),
            jnp.sum(dy, axis=0, keepdims=True))


def _rms_fwd(x, g):
    rstd = lax.rsqrt(jnp.mean(x * x, axis=1, keepdims=True) + RMS_EPS)
    return x * rstd * g


def _rms_bwd(dy, x, g):
    rstd = lax.rsqrt(jnp.mean(x * x, axis=1, keepdims=True) + RMS_EPS)
    xn = x * rstd
    dxn = dy * g
    return rstd * (dxn - xn * jnp.mean(dxn * xn, axis=1, keepdims=True)), jnp.sum(dy * xn, axis=0, keepdims=True)


def _rope(x, cos, sin, transpose=False):
    parts = []
    for g in range(x.shape[1] // LANES):
        xg = x[:, g * LANES:(g + 1) * LANES]
        if transpose:
            parts.append(xg * cos + pltpu.roll(xg * sin, LANES // 2, 1))
        else:
            parts.append(xg * cos + pltpu.roll(xg, LANES // 2, 1) * sin)
    return parts[0] if len(parts) == 1 else jnp.concatenate(parts, axis=1)


def _conv_fwd(proj, conv_w, conv_b, E, tm):
    T = proj.shape[0]
    kc = conv_w.shape[0]
    lc, hb, rc = min(CONV_LC, E), CONV_HALO, min(CONV_RC, tm)
    nl, ni, ratio = E // lc, T // tm, tm // hb
    gate_off = E // lc

    def body(val_ref, gate_ref, valh_ref, gateh_ref, w_ref, cb_ref, u1_ref, ubuf):
        i = pl.program_id(1)
        ubuf[hb:, :] = val_ref[...] * _sig(gate_ref[...])
        halo = valh_ref[...] * _sig(gateh_ref[...])
        ubuf[0:hb, :] = jnp.where(i > 0, halo, 0.0)
        for r0 in range(0, tm, rc):
            acc = jnp.zeros((rc, lc), F32) + cb_ref[...]
            for k in range(kc):
                acc += w_ref[k:k + 1, :] * ubuf[r0 + hb - (kc - 1) + k:r0 + hb - (kc - 1) + k + rc, :]
            u1_ref[r0:r0 + rc, :] = acc

    return pl.pallas_call(
        body, name="conv_fwd", grid=(nl, ni),
        in_specs=[
            pl.BlockSpec((tm, lc), lambda l, i: (i, l)),
            pl.BlockSpec((tm, lc), lambda l, i: (i, gate_off + l)),
            pl.BlockSpec((hb, lc), lambda l, i: (jnp.maximum(i * ratio - 1, 0), l)),
            pl.BlockSpec((hb, lc), lambda l, i: (jnp.maximum(i * ratio - 1, 0), gate_off + l)),
            pl.BlockSpec((kc, lc), lambda l, i: (0, l)),
            pl.BlockSpec((1, lc), lambda l, i: (0, l)),
        ],
        out_specs=pl.BlockSpec((tm, lc), lambda l, i: (i, l)),
        out_shape=jax.ShapeDtypeStruct((T, E), F32),
        scratch_shapes=[pltpu.VMEM((hb + tm, lc), F32)], compiler_params=_params(2),
    )(proj, proj, proj, proj, conv_w, conv_b)


def _conv_post(u1, proj, norm_g, norm_b, E, tm):
    T = u1.shape[0]

    def body(u1_ref, z_ref, g_ref, b_ref, u4_ref):
        xhat, _ = _ln_stats(u1_ref[...])
        u4_ref[...] = (_silu(xhat * g_ref[...] + b_ref[...]) * _silu(z_ref[...])).astype(BF16)

    return pl.pallas_call(
        body, name="conv_post", grid=(T // tm,),
        in_specs=[pl.BlockSpec((tm, E), lambda i: (i, 0)), pl.BlockSpec((tm, E), lambda i: (i, 2)),
                  pl.BlockSpec((1, E), lambda i: (0, 0)), pl.BlockSpec((1, E), lambda i: (0, 0))],
        out_specs=pl.BlockSpec((tm, E), lambda i: (i, 0)),
        out_shape=jax.ShapeDtypeStruct((T, E), BF16), compiler_params=_params(1),
    )(u1, proj, norm_g, norm_b)


def _conv_bwd(du1, proj, conv_w, E, tm):
    T = du1.shape[0]
    kc = conv_w.shape[0]
    lc, hb, rc = min(CONV_LC, E), CONV_HALO, min(CONV_RC, tm)
    nl, ni, ratio = E // lc, T // tm, tm // hb
    gate_off = E // lc
    last_halo = T // hb - 1

    def body(du_ref, dun_ref, val_ref, gate_ref, valh_ref, gateh_ref, w_ref,
             dval_ref, dgate_ref, dw_ref, db_ref, ubuf, dbuf):
        i = pl.program_id(1)
        val, sg = val_ref[...], _sig(gate_ref[...])
        ubuf[hb:, :] = val * sg
        halo = valh_ref[...] * _sig(gateh_ref[...])
        ubuf[0:hb, :] = jnp.where(i > 0, halo, 0.0)
        dbuf[0:tm, :] = du_ref[...]
        dbuf[tm:, :] = jnp.where(i < ni - 1, dun_ref[...], 0.0)

        @pl.when(i == 0)
        def _():
            dw_ref[...] = jnp.zeros_like(dw_ref)
            db_ref[...] = jnp.zeros_like(db_ref)

        db_ref[...] += jnp.sum(du_ref[...], axis=0, keepdims=True)
        for k in range(kc):
            acc = jnp.zeros((rc, lc), F32)
            for r0 in range(0, tm, rc):
                acc += ubuf[r0 + hb - (kc - 1) + k:r0 + hb - (kc - 1) + k + rc, :] * dbuf[r0:r0 + rc, :]
            dw_ref[k:k + 1, :] += jnp.sum(acc, axis=0, keepdims=True)
        for r0 in range(0, tm, rc):
            acc = jnp.zeros((rc, lc), F32)
            for k in range(kc):
                acc += w_ref[k:k + 1, :] * dbuf[r0 + (kc - 1) - k:r0 + (kc - 1) - k + rc, :]
            v, s = val[r0:r0 + rc, :], sg[r0:r0 + rc, :]
            dval_ref[r0:r0 + rc, :] = (acc * s).astype(BF16)
            dgate_ref[r0:r0 + rc, :] = (acc * v * s * (1.0 - s)).astype(BF16)

    return pl.pallas_call(
        body, name="conv_bwd", grid=(nl, ni),
        in_specs=[
            pl.BlockSpec((tm, lc), lambda l, i: (i, l)),
            pl.BlockSpec((hb, lc), lambda l, i: (jnp.minimum((i + 1) * ratio, last_halo), l)),
            pl.BlockSpec((tm, lc), lambda l, i: (i, l)),
            pl.BlockSpec((tm, lc), lambda l, i: (i, gate_off + l)),
            pl.BlockSpec((hb, lc), lambda l, i: (jnp.maximum(i * ratio - 1, 0), l)),
            pl.BlockSpec((hb, lc), lambda l, i: (jnp.maximum(i * ratio - 1, 0), gate_off + l)),
            pl.BlockSpec((kc, lc), lambda l, i: (0, l)),
        ],
        out_specs=[pl.BlockSpec((tm, lc), lambda l, i: (i, l)), pl.BlockSpec((tm, lc), lambda l, i: (i, l)),
                   pl.BlockSpec((kc, lc), lambda l, i: (0, l)), pl.BlockSpec((1, lc), lambda l, i: (0, l))],
        out_shape=[jax.ShapeDtypeStruct((T, E), BF16), jax.ShapeDtypeStruct((T, E), BF16),
                   jax.ShapeDtypeStruct((kc, E), F32), jax.ShapeDtypeStruct((1, E), F32)],
        scratch_shapes=[pltpu.VMEM((hb + tm, lc), F32), pltpu.VMEM((tm + hb, lc), F32)],
        compiler_params=_params(2),
    )(du1, du1, proj, proj, proj, proj, conv_w)


def _norm_prep(pb, kv_g, q_g, cos, sin, rkv, rq, wk, tm):
    T = pb.shape[0]

    def body(ckv_ref, cq_ref, kg_ref, qg_ref, cos_ref, sin_ref, c_ref, kr_ref, cqn_ref):
        blk = ckv_ref[...]
        c_ref[...] = _rms_fwd(blk[:, :rkv], kg_ref[...]).astype(BF16)
        kr_ref[...] = _rope(blk[:, rkv:rkv + LANES], cos_ref[...], sin_ref[...]).astype(BF16)
        cqn_ref[...] = _rms_fwd(cq_ref[...], qg_ref[...]).astype(BF16)

    return pl.pallas_call(
        body, name="norm_prep", grid=(T // tm,),
        in_specs=[pl.BlockSpec((tm, wk), lambda i: (i, 0)), pl.BlockSpec((tm, rq), lambda i: (i, wk // rq)),
                  pl.BlockSpec((1, rkv), lambda i: (0, 0)), pl.BlockSpec((1, rq), lambda i: (0, 0)),
                  pl.BlockSpec((tm, LANES), lambda i: (i, 0)), pl.BlockSpec((tm, LANES), lambda i: (i, 0))],
        out_specs=[pl.BlockSpec((tm, rkv), lambda i: (i, 0)), pl.BlockSpec((tm, LANES), lambda i: (i, 0)),
                   pl.BlockSpec((tm, rq), lambda i: (i, 0))],
        out_shape=[jax.ShapeDtypeStruct((T, rkv), BF16), jax.ShapeDtypeStruct((T, LANES), BF16),
                   jax.ShapeDtypeStruct((T, rq), BF16)],
        compiler_params=_params(1),
    )(pb, pb, kv_g, q_g, cos, sin)


def _attn_fwd(q_all, kv, kr, H, tq, scale):
    T = q_all.shape[0]
    nq = T // tq

    def body(qn_ref, qr_ref, kn_ref, kr_ref, v_ref, o_ref, lse_ref, m_sc, l_sc, acc_sc):
        qi, ki = pl.program_id(1), pl.program_id(2)

        @pl.when(ki == 0)
        def _():
            m_sc[...] = jnp.full_like(m_sc, MASK_VALUE)
            l_sc[...] = jnp.zeros_like(l_sc)
            acc_sc[...] = jnp.zeros_like(acc_sc)

        def step(masked):
            q = jnp.concatenate([qn_ref[...], qr_ref[...]], axis=1)
            k = jnp.concatenate([kn_ref[...], kr_ref[...]], axis=1)
            s = _dot(q, k, NT) * scale
            if masked:
                row = lax.broadcasted_iota(jnp.int32, s.shape, 0)
                col = lax.broadcasted_iota(jnp.int32, s.shape, 1)
                s = jnp.where(col <= row, s, MASK_VALUE)
            m_new = jnp.maximum(m_sc[...], jnp.max(s, axis=1, keepdims=True))
            p = jnp.exp(s - m_new)
            alpha = jnp.exp(m_sc[...] - m_new)
            l_sc[...] = alpha * l_sc[...] + jnp.sum(p, axis=1, keepdims=True)
            acc_sc[...] = alpha * acc_sc[...] + _dot(p, v_ref[...], NN)
            m_sc[...] = m_new

        pl.when(ki < qi)(functools.partial(step, False))

        @pl.when(ki == qi)
        def _():
            step(True)
            o_ref[...] = acc_sc[...] / l_sc[...]
            lse_ref[0] = m_sc[...] + jnp.log(l_sc[...])

    kmap = lambda h, qi, ki: (jnp.minimum(ki, qi), h)
    return pl.pallas_call(
        body, name="attn_fwd", grid=(H, nq, nq),
        in_specs=[pl.BlockSpec((tq, LANES), lambda h, qi, ki: (qi, h)),
                  pl.BlockSpec((tq, LANES), lambda h, qi, ki: (qi, H + h)),
                  pl.BlockSpec((tq, LANES), kmap),
                  pl.BlockSpec((tq, LANES), lambda h, qi, ki: (jnp.minimum(ki, qi), 0)),
                  pl.BlockSpec((tq, LANES), lambda h, qi, ki: (jnp.minimum(ki, qi), H + h))],
        out_specs=[pl.BlockSpec((tq, LANES), lambda h, qi, ki: (qi, h)),
                   pl.BlockSpec((1, tq, 1), lambda h, qi, ki: (h, qi, 0))],
        out_shape=[jax.ShapeDtypeStruct((T, H * LANES), F32), jax.ShapeDtypeStruct((H, T, 1), F32)],
        scratch_shapes=[pltpu.VMEM((tq, 1), F32), pltpu.VMEM((tq, 1), F32), pltpu.VMEM((tq, LANES), F32)],
        compiler_params=_params(3),
    )(q_all, q_all, kv, kr, kv)


def _attn_bwd(q_all, kv, kr, do, o, lse, H, tq, scale):
    T = q_all.shape[0]
    nq = T // tq
    HV = H * LANES

    def body(qn_ref, qr_ref, kn_ref, kr_ref, v_ref, do_ref, o_ref, lse_ref,
             dqn_ref, dqr_ref, dkn_ref, dkr_ref, dv_ref, dq_sc, dk_sc, dv_sc):
        ki, qi = pl.program_id(1), pl.program_id(2)

        @pl.when(jnp.logical_and(ki == 0, qi == 0))
        def _():
            dq_sc[...] = jnp.zeros_like(dq_sc)

        @pl.when(qi == ki)
        def _():
            dk_sc[...] = jnp.zeros_like(dk_sc)
            dv_sc[...] = jnp.zeros_like(dv_sc)

        def step(masked):
            q = jnp.concatenate([qn_ref[...], qr_ref[...]], axis=1)
            k = jnp.concatenate([kn_ref[...], kr_ref[...]], axis=1)
            s = _dot(q, k, NT) * scale
            if masked:
                row = lax.broadcasted_iota(jnp.int32, s.shape, 0)
                col = lax.broadcasted_iota(jnp.int32, s.shape, 1)
                s = jnp.where(col <= row, s, MASK_VALUE)
            p = jnp.exp(s - lse_ref[0])
            dov = do_ref[...]
            delta = jnp.sum(dov.astype(F32) * o_ref[...], axis=1, keepdims=True)
            dv_sc[...] += _dot(p, dov, TN)
            dp = _dot(dov, v_ref[...], NT)
            ds = (p * (dp - delta) * scale).astype(BF16)
            dk_sc[...] += _dot(ds, q, TN)
            rows = pl.ds(pl.multiple_of(qi * tq, tq), tq)
            dq_sc[rows, :] += _dot(ds, k, NN)

        pl.when(qi > ki)(functools.partial(step, False))
        pl.when(qi == ki)(functools.partial(step, True))

        @pl.when(qi == nq - 1)
        def _():
            dkn_ref[...] = dk_sc[:, :LANES].astype(BF16)
            dkr_ref[...] = dk_sc[:, LANES:]
            dv_ref[...] = dv_sc[...].astype(BF16)

        @pl.when(jnp.logical_and(ki == nq - 1, qi == nq - 1))
        def _():
            dqn_ref[...] = dq_sc[:, :LANES].astype(BF16)
            dqr_ref[...] = dq_sc[:, LANES:]

    qrow = lambda h, ki, qi: jnp.maximum(qi, ki)
    return pl.pallas_call(
        body, name="attn_bwd", grid=(H, nq, nq),
        in_specs=[pl.BlockSpec((tq, LANES), lambda h, ki, qi: (qrow(h, ki, qi), h)),
                  pl.BlockSpec((tq, LANES), lambda h, ki, qi: (qrow(h, ki, qi), H + h)),
                  pl.BlockSpec((tq, LANES), lambda h, ki, qi: (ki, h)),
                  pl.BlockSpec((tq, LANES), lambda h, ki, qi: (ki, 0)),
                  pl.BlockSpec((tq, LANES), lambda h, ki, qi: (ki, H + h)),
                  pl.BlockSpec((tq, LANES), lambda h, ki, qi: (qrow(h, ki, qi), h)),
                  pl.BlockSpec((tq, LANES), lambda h, ki, qi: (qrow(h, ki, qi), h)),
                  pl.BlockSpec((1, tq, 1), lambda h, ki, qi: (h, qrow(h, ki, qi), 0))],
        out_specs=[pl.BlockSpec((T, LANES), lambda h, ki, qi: (0, h)),
                   pl.BlockSpec((T, LANES), lambda h, ki, qi: (0, h)),
                   pl.BlockSpec((tq, LANES), lambda h, ki, qi: (ki, h)),
                   pl.BlockSpec((tq, LANES), lambda h, ki, qi: (ki, h)),
                   pl.BlockSpec((tq, LANES), lambda h, ki, qi: (ki, h))],
        out_shape=[jax.ShapeDtypeStruct((T, HV), BF16), jax.ShapeDtypeStruct((T, HV), F32),
                   jax.ShapeDtypeStruct((T, HV), BF16), jax.ShapeDtypeStruct((T, HV), F32),
                   jax.ShapeDtypeStruct((T, HV), BF16)],
        scratch_shapes=[pltpu.VMEM((T, 2 * LANES), F32), pltpu.VMEM((tq, 2 * LANES), F32),
                        pltpu.VMEM((tq, LANES), F32)],
        compiler_params=_params(3),
    )(q_all, q_all, kv, kr, kv, do, o, lse)


def _rope_bwd(dqr, dkr_heads, cos, sin, H, tm):
    T, HV = dqr.shape

    def body(dqr_ref, dkr_ref, cos_ref, sin_ref, dq_ref, dk_ref):
        c, s = cos_ref[...], sin_ref[...]
        dq_ref[...] = _rope(dqr_ref[...], c, s, transpose=True).astype(BF16)
        dk = dkr_ref[...]
        tot = dk[:, 0:LANES]
        for h in range(1, H):
            tot = tot + dk[:, h * LANES:(h + 1) * LANES]
        dk_ref[...] = _rope(tot, c, s, transpose=True)

    return pl.pallas_call(
        body, name="rope_bwd", grid=(T // tm,),
        in_specs=[pl.BlockSpec((tm, HV), lambda i: (i, 0)), pl.BlockSpec((tm, HV), lambda i: (i, 0)),
                  pl.BlockSpec((tm, LANES), lambda i: (i, 0)), pl.BlockSpec((tm, LANES), lambda i: (i, 0))],
        out_specs=[pl.BlockSpec((tm, HV), lambda i: (i, 0)), pl.BlockSpec((tm, LANES), lambda i: (i, 0))],
        out_shape=[jax.ShapeDtypeStruct((T, HV), BF16), jax.ShapeDtypeStruct((T, LANES), F32)],
        compiler_params=_params(1),
    )(dqr, dkr_heads, cos, sin)


def _place():
    x, y, c = lax.axis_index("x"), lax.axis_index("y"), lax.axis_index("c")
    chips = [(1 - x, y), (x, 1 - y), (1 - x, 1 - y)]
    return x, y, c, chips


def _all_gather_chips(pack):
    rows = pack.shape[0]
    half = rows // 2
    assert half * 2 == rows and half % BF16_ROWS == 0

    def body(w_ref, out_ref, send_sems, recv_sems, local_sem):
        x, y, c, chips = _place()
        sibling = (x, y, 1 - c)

        def region(px, py, pc):
            return out_ref.at[2 * px + py, pl.ds(pc * half, half), :]

        def copy(k, block, to, src=None):
            return pltpu.make_async_remote_copy(
                src_ref=region(*block) if src is None else src, dst_ref=region(*block),
                send_sem=send_sems.at[k], recv_sem=recv_sems.at[k], device_id=to, device_id_type=MESH)

        mine = pltpu.make_async_copy(w_ref, out_ref.at[2 * x + y], local_sem)
        mine.start()
        my_half = w_ref.at[pl.ds(c * half, half), :]
        first = [copy(j, (x, y, c), (*chip, c), src=my_half) for j, chip in enumerate(chips)]
        for cp in first:
            cp.start()
        passed = [copy(3 + j, (*chip, c), sibling) for j, chip in enumerate(chips)]
        for j, chip in enumerate(chips):
            copy(j, (*chip, c), (x, y, c)).wait_recv()
            passed[j].start()
        for j, chip in enumerate(chips):
            copy(3 + j, (*chip, 1 - c), (x, y, c)).wait_recv()
        for cp in first + passed:
            cp.wait_send()
        mine.wait()

    return pl.pallas_call(
        body, name="gather_weights", in_specs=[ANY], out_specs=ANY,
        out_shape=jax.ShapeDtypeStruct((N_CHIPS,) + pack.shape, pack.dtype),
        scratch_shapes=[pltpu.SemaphoreType.DMA((6,)), pltpu.SemaphoreType.DMA((6,)), pltpu.SemaphoreType.DMA],
    )(pack)


def _scatter_grads(grads):
    n = len(grads)

    def body(*refs):
        g_refs, r_refs = refs[:n], refs[n:2 * n]
        send_sems, recv_sems, local_sems = refs[2 * n:]
        x, y, c, chips = _place()
        me = 2 * x + y
        locals_, sends = [], []
        for w in range(n):
            cp = pltpu.make_async_copy(g_refs[w].at[me], r_refs[w].at[me], local_sems.at[w])
            cp.start()
            locals_.append(cp)
            for j, (px, py) in enumerate(chips):
                cp = pltpu.make_async_remote_copy(
                    src_ref=g_refs[w].at[2 * px + py], dst_ref=r_refs[w].at[me],
                    send_sem=send_sems.at[3 * w + j], recv_sem=recv_sems.at[3 * w + j],
                    device_id=(px, py, c), device_id_type=MESH)
                cp.start()
                sends.append(cp)
        for w in range(n):
            for j, (px, py) in enumerate(chips):
                pltpu.make_async_remote_copy(
                    src_ref=g_refs[w].at[me], dst_ref=r_refs[w].at[2 * px + py],
                    send_sem=send_sems.at[3 * w + j], recv_sem=recv_sems.at[3 * w + j],
                    device_id=(px, py, c), device_id_type=MESH).wait_recv()
        for cp in sends:
            cp.wait_send()
        for cp in locals_:
            cp.wait()

    return pl.pallas_call(
        body, name="scatter_grads", in_specs=[ANY] * n, out_specs=[ANY] * n,
        out_shape=[jax.ShapeDtypeStruct(g.shape, g.dtype) for g in grads],
        scratch_shapes=[pltpu.SemaphoreType.DMA((3 * n,)), pltpu.SemaphoreType.DMA((3 * n,)),
                        pltpu.SemaphoreType.DMA((n,))],
    )(*grads)


def _swap_cores(parts):
    n = len(parts)

    def body(*refs):
        p_refs, r_refs = refs[:n], refs[n:2 * n]
        send_sems, recv_sems = refs[2 * n:]
        x, y, c, _ = _place()
        copies = [pltpu.make_async_remote_copy(
            src_ref=p_refs[w], dst_ref=r_refs[w], send_sem=send_sems.at[w], recv_sem=recv_sems.at[w],
            device_id=(x, y, 1 - c), device_id_type=MESH) for w in range(n)]
        for cp in copies:
            cp.start()
        for cp in copies:
            cp.wait()

    return pl.pallas_call(
        body, name="swap_cores", in_specs=[ANY] * n, out_specs=[ANY] * n,
        out_shape=[jax.ShapeDtypeStruct(p.shape, p.dtype) for p in parts],
        scratch_shapes=[pltpu.SemaphoreType.DMA((n,)), pltpu.SemaphoreType.DMA((n,))],
    )(*parts)


def _all_reduce_small(part):
    n_dev = 8

    def body(p_ref, out_ref, buf, send_sems, recv_sems):
        x, y, c, _ = _place()
        me = 4 * x + 2 * y + c
        buf[me] = p_ref[...]
        copies = []
        for k in range(1, n_dev):
            kx, ky, kc = (k >> 2) & 1, (k >> 1) & 1, k & 1
            peer = (x ^ kx, y ^ ky, c ^ kc)
            cp = pltpu.make_async_remote_copy(
                src_ref=p_ref, dst_ref=buf.at[me], send_sem=send_sems.at[k - 1], recv_sem=recv_sems.at[k - 1],
                device_id=peer, device_id_type=MESH)
            cp.start()
            copies.append(cp)
        for k in range(1, n_dev):
            pltpu.make_async_remote_copy(
                src_ref=p_ref, dst_ref=buf.at[me ^ k], send_sem=send_sems.at[k - 1], recv_sem=recv_sems.at[k - 1],
                device_id=(x, y, c), device_id_type=MESH).wait_recv()
        for cp in copies:
            cp.wait_send()
        tot = buf[0]
        for d in range(1, n_dev):
            tot = tot + buf[d]
        out_ref[...] = tot

    vm = pl.BlockSpec(memory_space=pltpu.VMEM)
    return pl.pallas_call(
        body, name="all_reduce_small", in_specs=[vm], out_specs=vm,
        out_shape=jax.ShapeDtypeStruct(part.shape, F32),
        scratch_shapes=[pltpu.VMEM((n_dev,) + part.shape, F32), pltpu.SemaphoreType.DMA((n_dev - 1,)),
                        pltpu.SemaphoreType.DMA((n_dev - 1,))],
    )(part)


def _sum_chips(name, r):
    _, rows, cols = r.shape
    tr = _gcd(ADAM_ROWS, rows)

    def body(r_ref, o_ref):
        tot = r_ref[0].astype(F32)
        for q in range(1, N_CHIPS):
            tot = tot + r_ref[q].astype(F32)
        o_ref[...] = tot

    return pl.pallas_call(
        body, name=name, grid=(rows // tr,),
        in_specs=[pl.BlockSpec((N_CHIPS, tr, cols), lambda i: (0, i, 0))],
        out_specs=pl.BlockSpec((tr, cols), lambda i: (i, 0)),
        out_shape=jax.ShapeDtypeStruct((rows, cols), F32), compiler_params=_params(1),
    )(r)


def _adamw(name, g_parts, w, m, v):
    rows, cols = w.shape
    tr = _gcd(ADAM_ROWS, rows)
    n = len(g_parts)

    def body(*refs):
        g = refs[0][...]
        for r in refs[1:n]:
            g = g + r[...]
        w_ref, m_ref, v_ref, go_ref, d_ref, mo_ref, vo_ref = refs[n:]
        mn = ADAM_B1 * m_ref[...] + (1.0 - ADAM_B1) * g
        vn = ADAM_B2 * v_ref[...] + (1.0 - ADAM_B2) * jnp.square(g)
        m_hat = mn / (1.0 - ADAM_B1 ** ADAM_STEP)
        v_hat = vn / (1.0 - ADAM_B2 ** ADAM_STEP)
        go_ref[...] = g
        d_ref[...] = -ADAM_LR * (m_hat / (jnp.sqrt(v_hat) + ADAM_EPS) + ADAM_WD * w_ref[...])
        mo_ref[...] = mn
        vo_ref[...] = vn

    spec = pl.BlockSpec((tr, cols), lambda i: (i, 0))
    return pl.pallas_call(
        body, name=name, grid=(rows // tr,), in_specs=[spec] * (n + 3), out_specs=[spec] * 4,
        out_shape=[jax.ShapeDtypeStruct((rows, cols), F32)] * 4, compiler_params=_params(1),
    )(*g_parts, w, m, v)


def _pack_rows(flat, dtype, multiple):
    n = flat.shape[0]
    total = -(-n // multiple) * multiple
    return jnp.pad(flat, (0, total - n)).astype(dtype).reshape(total // LANES, LANES)


def kernel(x, positions, ln_g, ln_b, a_w_in, a_b_in, a_conv_w, a_conv_b, a_norm_g, a_norm_b, a_w_out, a_b_out, kv_w_down, kv_norm_g, kv_w_uk, kv_w_uv, b_w_in, b_q_norm_g, b_w_uq, b_w_out, loss_target, m_ln_g, m_ln_b, m_a_w_in, m_a_b_in, m_a_conv_w, m_a_conv_b, m_a_norm_g, m_a_norm_b, m_a_w_out, m_a_b_out, m_kv_w_down, m_kv_norm_g, m_kv_w_uk, m_kv_w_uv, m_b_w_in, m_b_q_norm_g, m_b_w_uq, m_b_w_out, v_ln_g, v_ln_b, v_a_w_in, v_a_b_in, v_a_conv_w, v_a_conv_b, v_a_norm_g, v_a_norm_b, v_a_w_out, v_a_b_out, v_kv_w_down, v_kv_norm_g, v_kv_w_uk, v_kv_w_uv, v_b_w_in, v_b_q_norm_g, v_b_w_uq, v_b_w_out):
    T, D = x.shape[1], x.shape[2]
    E = N_CHIPS * a_w_out.shape[1]
    KC = a_conv_w.shape[1]
    RKV = kv_norm_g.shape[0]
    H, DN = kv_w_uk.shape[1], kv_w_uk.shape[2]
    RQ = b_q_norm_g.shape[1]
    HV = N_CHIPS * b_w_out.shape[1]
    assert DN == LANES and kv_w_uv.shape[2] == LANES and HV == H * LANES
    assert kv_w_down.shape[1] == RKV + ROPE_DIM and b_w_uq.shape[3] == DN + ROPE_DIM
    assert ln_g.shape[0] == 2 and a_w_in.shape[0] == 1 and b_w_in.shape[0] == 1
    alpha = (2.0 * ln_g.shape[0]) ** 0.25
    scale = 1.0 / math.sqrt(DN + ROPE_DIM)
    WK = -(-(RKV + LANES) // 256) * 256
    assert WK % RQ == 0
    Z_OFF = WK + RQ
    tm, tmw, tq = min(TM, T), min(TM_WIDE, T), min(TQ, T)
    xs = x[0]
    tgt = loss_target[0]
    px, py = lax.axis_index("x"), lax.axis_index("y")
    chip = 2 * px + py

    mats = [a_w_in[0], a_w_out[0], kv_w_down, kv_w_uk, kv_w_uv, b_w_in[0], b_w_uq[0], b_w_out[0]]
    vecs = [a_b_in[0], a_conv_w[0], a_conv_b[0], a_norm_g[0], a_norm_b[0], a_b_out[0]]
    flat = jnp.concatenate(
        [w.astype(BF16).reshape(-1) for w in mats]
        + [lax.bitcast_convert_type(w.reshape(-1), BF16).reshape(-1) for w in vecs])
    pack = _pack_rows(flat, BF16, 2 * BF16_ROWS * LANES)
    gathered = _all_gather_chips(pack).reshape(N_CHIPS, -1)
    off = 0
    full = []
    for w in mats:
        full.append(gathered[:, off:off + w.size].reshape((N_CHIPS,) + w.shape))
        off += w.size
    fvec = []
    for w in vecs:
        bits = gathered[:, off:off + 2 * w.size].reshape((N_CHIPS,) + w.shape + (2,))
        fvec.append(lax.bitcast_convert_type(bits, F32))
        off += 2 * w.size
    g_win, g_wout, g_wd, g_uk, g_uv, g_wbin, g_wuq, g_wbout = full
    cols = lambda g: jnp.moveaxis(g, 0, -2).reshape(g.shape[1:-1] + (N_CHIPS * g.shape[-1],))
    w_in = cols(g_win)
    w_out = g_wout.reshape(E, D)
    wd = g_wd.reshape(D, RKV + ROPE_DIM)
    zpad = jnp.zeros((D, ROPE_HALF), BF16)
    wd_p = jnp.concatenate(
        [wd[:, :RKV], wd[:, RKV:RKV + ROPE_HALF], zpad, wd[:, RKV + ROPE_HALF:], zpad,
         jnp.zeros((D, WK - RKV - LANES), BF16)], axis=1)
    w_bin = cols(g_wbin)
    w_z = w_bin[:, RQ:]
    wb_small = jnp.concatenate([wd_p, w_bin[:, :RQ]], axis=1)
    wb_all = jnp.concatenate([wd_p, w_bin], axis=1)
    w_kv = jnp.concatenate([g_uk.reshape(RKV, HV), g_uv.reshape(RKV, HV)], axis=1)
    wuq = g_wuq.reshape(RQ, H, DN + ROPE_DIM)
    zq = jnp.zeros((RQ, H, ROPE_HALF), BF16)
    w_qr = jnp.concatenate([wuq[:, :, DN:DN + ROPE_HALF], zq, wuq[:, :, DN + ROPE_HALF:], zq], axis=2)
    w_q = jnp.concatenate([wuq[:, :, :DN].reshape(RQ, HV), w_qr.reshape(RQ, HV)], axis=1)
    w_bout = g_wbout.reshape(HV, D)
    b_in = cols(fvec[0][:, None, :])
    conv_w = cols(fvec[1])
    conv_b, norm_g, norm_b, b_out = (cols(f[:, None, :]) for f in fvec[2:])

    freqs = ROPE_THETA ** (-jnp.arange(0, ROPE_DIM, 2, dtype=F32) / ROPE_DIM)
    ang = positions[0].astype(F32)[:, None] * freqs
    cs, sn = jnp.cos(ang), jnp.sin(ang)
    ones, zeros = jnp.ones_like(cs), jnp.zeros_like(cs)
    cos_t = jnp.concatenate([cs, ones, cs, ones], axis=1)
    sin_t = jnp.concatenate([-sn, zeros, sn, zeros], axis=1)

    row = lambda a: a.reshape(1, -1)
    g0, b0, g1, b1 = row(ln_g[0]), row(ln_b[0]), row(ln_g[1]), row(ln_b[1])
    kv_g, q_g = row(kv_norm_g), row(b_q_norm_g[0])
    plain = lambda acc, ins, i, j: [acc]

    (proj,) = _row_mm("a_in", [((xs,), None)], w_in, nt=False, tm=tm, tn=min(TN_, E), tk=min(TK, D),
                      outs=[((T, 3 * E), F32, 'tile')], epi=lambda acc, ins, i, j: [acc + ins[0]],
                      epi_ins=[(b_in, 'col')])
    u1 = _conv_fwd(proj, conv_w, conv_b, E, tmw)
    u4 = _conv_post(u1, proj, norm_g, norm_b, E, tmw)

    def ln_epi(acc, ins, i, j):
        bias, res, g, b = ins
        xhat, rstd = _ln_stats(alpha * res + acc + bias)
        return [xhat * g + b, xhat, rstd]

    h1, xhat1, rstd1 = _row_mm(
        "a_out", [((u4,), None)], w_out, nt=False, tm=tmw, tn=D, tk=min(TK, E),
        outs=[((T, D), F32, 'tile'), ((T, D), F32, 'tile'), ((T, 1), F32, 'row')], epi=ln_epi,
        epi_ins=[(b_out, 'col'), (xs, 'tile'), (g0, 'col'), (b0, 'col')])

    tkb = _gcd(WK, RQ, HV, TK)
    (pb,) = _row_mm("b_in", [((h1,), None)], wb_small, nt=False, tm=tm, tn=tkb, tk=min(TK, D),
                    outs=[((T, Z_OFF), F32, 'tile')], epi=plain)
    (zb,) = _row_mm("b_in_gate", [((h1,), None)], w_z, nt=False, tm=tm, tn=min(TN_, HV), tk=min(TK, D),
                    outs=[((T, HV), F32, 'tile')], epi=plain)
    c_lat, kr, cqn = _norm_prep(pb, kv_g, q_g, cos_t, sin_t, RKV, RQ, WK, tmw)
    (kv,) = _row_mm("kv_up", [((c_lat,), None)], w_kv, nt=False, tm=tm, tn=min(TN_, HV), tk=min(TK, RKV),
                    outs=[((T, 2 * HV), BF16, 'tile')], epi=plain)
    tnq = min(TN_, HV)
    half_q = HV // tnq

    def q_epi(acc, ins, i, j):
        return [jnp.where(j >= half_q, _rope(acc, ins[0], ins[1]), acc)]

    (q_all,) = _row_mm("q_up", [((cqn,), None)], w_q, nt=False, tm=tm, tn=tnq, tk=min(TK, RQ),
                       outs=[((T, 2 * HV), BF16, 'tile')], epi=q_epi,
                       epi_ins=[(cos_t, 'row'), (sin_t, 'row')])
    o, lse = _attn_fwd(q_all, kv, kr, H, tq, scale)

    def loss_epi(acc, ins, i, j):
        res, g, b, target = ins
        xhat, rstd = _ln_stats(alpha * res + acc)
        diff = xhat * g + b - target
        dr, dg, db = _ln_bwd(diff / D, xhat, rstd, g)
        return [dr, 0.5 * jnp.sum(diff * diff, keepdims=True) / D, dg, db]

    zk = min(TK, HV)
    dr1, loss_part, dg1, db1 = _row_mm(
        "b_out", [((o, zb), _gate)], w_bout, nt=False, tm=tmw, tn=D, tk=zk,
        outs=[((T, D), F32, 'tile'), ((1, 1), F32, 'acc'), ((1, D), F32, 'acc'), ((1, D), F32, 'acc')],
        epi=loss_epi, epi_ins=[(h1, 'tile'), (g1, 'col'), (b1, 'col'), (tgt, 'tile')])

    def gate_bwd_epi(acc, ins, i, j):
        return [acc * _silu(ins[1]), acc * ins[0] * _silu_grad(ins[1])]

    tnz = min(TN_, HV)
    do, dz = _row_mm(
        "b_out_bwd", [((dr1,), None)], w_bout, nt=True, tm=tmw, tn=tnz, tk=min(TK, D),
        outs=[((T, HV), BF16, 'tile'), ((T, HV), BF16, 'tile')], epi=gate_bwd_epi,
        epi_ins=[(o, 'tile'), (zb, 'tile')])
    gw_bout = _tn_mm("dw_b_out", (o, zb), _gate, [((dr1,), None)], tn=min(TN_, D), tk=tm, out_dtype=BF16)
    dqn, dqr, dkn, dkr_h, dv = _attn_bwd(q_all, kv, kr, do, o, lse, H, tq, scale)
    dqr_pre, dkr_pre = _rope_bwd(dqr, dkr_h, cos_t, sin_t, H, tmw)

    def cq_bwd_epi(acc, ins, i, j):
        dx, dg = _rms_bwd(acc, ins[0], ins[1])
        return [dx, dg]

    dcq, dqg = _row_mm(
        "q_up_bwd", [((dqn,), None), ((dqr_pre,), None)], w_q, nt=True, tm=tmw, tn=RQ, tk=min(TK, HV),
        outs=[((T, RQ), BF16, 'tile'), ((1, RQ), F32, 'acc')], epi=cq_bwd_epi,
        epi_ins=[(pb, pl.BlockSpec((tmw, RQ), lambda i, j, k: (i, WK // RQ))), (q_g, 'col')])
    gw_q = _tn_mm("dw_q_up", (cqn,), None, [((dqn,), None), ((dqr_pre,), None)],
                  tn=min(TN_, HV), tk=tm, out_dtype=BF16)

    def ckv_bwd_epi(acc, ins, i, j):
        blk, dkr_t, g = ins
        dx, dg = _rms_bwd(acc, blk[:, :RKV], g)
        parts = [dx, dkr_t]
        if WK > RKV + LANES:
            parts.append(jnp.zeros((dx.shape[0], WK - RKV - LANES), F32))
        return [jnp.concatenate(parts, axis=1), dg]

    dckv, dkvg = _row_mm(
        "kv_up_bwd", [((dkn,), None), ((dv,), None)], w_kv, nt=True, tm=tmw, tn=RKV, tk=min(TK, HV),
        outs=[((T, WK), BF16, pl.BlockSpec((tmw, WK), lambda i, j, k: (i, 0))), ((1, RKV), F32, 'acc')],
        epi=ckv_bwd_epi,
        epi_ins=[(pb, pl.BlockSpec((tmw, WK), lambda i, j, k: (i, 0))), (dkr_pre, 'row'), (kv_g, 'col')])
    gw_kv = _tn_mm("dw_kv_up", (c_lat,), None, [((dkn,), None), ((dv,), None)],
                   tn=min(TN_, HV), tk=tm, out_dtype=BF16)

    def ln1_bwd_epi(acc, ins, i, j):
        dr_up, xhat, rstd, g = ins
        dr, dg, db = _ln_bwd(alpha * dr_up + acc, xhat, rstd, g)
        return [dr, dg, db]

    dp_segs = [((dckv,), None), ((dcq,), None), ((dz,), None)]
    dr0, dg0, db0 = _row_mm(
        "b_in_bwd", dp_segs, wb_all, nt=True, tm=tmw, tn=D, tk=tkb,
        outs=[((T, D), F32, 'tile'), ((1, D), F32, 'acc'), ((1, D), F32, 'acc')], epi=ln1_bwd_epi,
        epi_ins=[(dr1, 'tile'), (xhat1, 'tile'), (rstd1, 'row'), (g0, 'col')])
    gw_ball = _tn_mm("dw_b_in", (h1,), None, dp_segs, tn=tkb, tk=tm, out_dtype=BF16)

    def conv_branch_bwd_epi(acc, ins, i, j):
        u1_t, z, g, b = ins
        xhat, rstd = _ln_stats(u1_t)
        u2 = xhat * g + b
        du3 = acc * _silu(z)
        dz_a = acc * _silu(u2) * _silu_grad(z)
        du1, dg, db = _ln_bwd(du3 * _silu_grad(u2), xhat, rstd, g)
        return [du1, dz_a, dg, db]

    du1, dz_a, dng, dnb = _row_mm(
        "a_out_bwd", [((dr0,), None)], w_out, nt=True, tm=tmw, tn=E, tk=min(TK, D),
        outs=[((T, E), F32, 'tile'), ((T, E), BF16, 'tile'), ((1, E), F32, 'acc'), ((1, E), F32, 'acc')],
        epi=conv_branch_bwd_epi,
        epi_ins=[(u1, 'tile'), (proj, pl.BlockSpec((tmw, E), lambda i, j, k: (i, 2))), (norm_g, 'col'),
                 (norm_b, 'col')])
    gw_out, dbo = _tn_mm("dw_a_out", (u4,), None, [((dr0,), None)], tn=min(TN_, D), tk=tm, out_dtype=BF16,
                         colsum=True)
    dval, dgate, dcw, dcb = _conv_bwd(du1, proj, conv_w, E, tmw)
    dproj_segs = [((dval,), None), ((dgate,), None), ((dz_a,), None)]
    (grad_x,) = _row_mm(
        "a_in_bwd", dproj_segs, w_in, nt=True, tm=tm, tn=min(TN_, D), tk=min(TK, E),
        outs=[((T, D), F32, 'tile')], epi=lambda acc, ins, i, j: [alpha * ins[0] + acc],
        epi_ins=[(dr0, 'tile')])
    tn_in = _gcd(3 * E // N_CHIPS, TN_)
    gw_in, dbi = _tn_mm("dw_a_in", (xs,), None, dproj_segs, tn=tn_in, tk=tm, out_dtype=BF16,
                        shard_major=True, colsum=True)

    shard_cols = lambda g: jnp.moveaxis(g.reshape(g.shape[0], N_CHIPS, -1), 1, 0)
    shard_rows = lambda g: g.reshape(N_CHIPS, g.shape[0] // N_CHIPS, g.shape[1])
    gq = gw_q.reshape(RQ, 2, H, LANES)
    g_uq = jnp.concatenate(
        [gq[:, 0], gq[:, 1, :, :ROPE_HALF], gq[:, 1, :, 2 * ROPE_HALF:3 * ROPE_HALF]], axis=2)
    g_wd_full = jnp.concatenate(
        [gw_ball[:, :RKV], gw_ball[:, RKV:RKV + ROPE_HALF],
         gw_ball[:, RKV + 2 * ROPE_HALF:RKV + 3 * ROPE_HALF]], axis=1)
    big_grads = [
        gw_in,
        shard_rows(gw_out),
        shard_rows(g_wd_full),
        shard_rows(gw_kv[:, :HV]),
        shard_rows(gw_kv[:, HV:]),
        shard_cols(gw_ball[:, WK:]),
        shard_rows(g_uq.reshape(RQ, H * (DN + ROPE_DIM))),
        shard_rows(gw_bout),
    ]
    received = _scatter_grads(big_grads)
    names = ["a_w_in", "a_w_out", "kv_w_down", "kv_w_uk", "kv_w_uv", "b_w_in", "b_w_uq", "b_w_out"]
    core_sums = [_sum_chips("sum_" + n, r) for n, r in zip(names, received)]
    sibling_sums = _swap_cores(core_sums)
    big_w = [a_w_in, a_w_out, kv_w_down, kv_w_uk, kv_w_uv, b_w_in, b_w_uq, b_w_out]
    big_m = [m_a_w_in, m_a_w_out, m_kv_w_down, m_kv_w_uk, m_kv_w_uv, m_b_w_in, m_b_w_uq, m_b_w_out]
    big_v = [v_a_w_in, v_a_w_out, v_kv_w_down, v_kv_w_uk, v_kv_w_uv, v_b_w_in, v_b_w_uq, v_b_w_out]
    big_out = {}
    for n, mine, theirs, w, m, v in zip(names, core_sums, sibling_sums, big_w, big_m, big_v):
        res = _adamw("adamw_" + n, [mine, theirs], *(a.reshape(mine.shape) for a in (w, m, v)))
        big_out[n] = [r.reshape(w.shape) for r in res]

    small_full = [jnp.concatenate([dg0, dg1]), jnp.concatenate([db0, db1]), dbi, dcw, dcb, dng, dnb, dbo,
                  dkvg, dqg]
    sflat = jnp.concatenate([g.reshape(-1) for g in small_full])
    summed = _all_reduce_small(_pack_rows(sflat, F32, 8 * LANES)).reshape(-1)
    soff = 0
    sgrads = []
    for g in small_full:
        sgrads.append(summed[soff:soff + g.size].reshape(g.shape))
        soff += g.size
    local_cols = lambda g, n: lax.dynamic_slice_in_dim(g, chip * n, n, axis=g.ndim - 1)
    snames = ["ln_g", "ln_b", "a_b_in", "a_conv_w", "a_conv_b", "a_norm_g", "a_norm_b", "a_b_out",
              "kv_norm_g", "b_q_norm_g"]
    small_w = [ln_g, ln_b, a_b_in, a_conv_w, a_conv_b, a_norm_g, a_norm_b, a_b_out, kv_norm_g, b_q_norm_g]
    small_m = [m_ln_g, m_ln_b, m_a_b_in, m_a_conv_w, m_a_conv_b, m_a_norm_g, m_a_norm_b, m_a_b_out,
               m_kv_norm_g, m_b_q_norm_g]
    small_v = [v_ln_g, v_ln_b, v_a_b_in, v_a_conv_w, v_a_conv_b, v_a_norm_g, v_a_norm_b, v_a_b_out,
               v_kv_norm_g, v_b_q_norm_g]
    sharded = {"a_b_in", "a_conv_w", "a_conv_b", "a_norm_g", "a_norm_b", "a_b_out"}
    local_g = [(local_cols(g, w.shape[-1]) if n in sharded else g).reshape(w.shape)
               for n, g, w in zip(snames, sgrads, small_w)]
    pack_small = lambda arrs: _pack_rows(jnp.concatenate([a.reshape(-1) for a in arrs]), F32, 8 * LANES)
    sres = _adamw("adamw_vectors", [pack_small(local_g)], pack_small(small_w), pack_small(small_m),
                  pack_small(small_v))
    small_out = {}
    soff = 0
    for n, w in zip(snames, small_w):
        small_out[n] = [r.reshape(-1)[soff:soff + w.size].reshape(w.shape) for r in sres]
        soff += w.size

    loss = lax.psum(loss_part[0, 0], ("x", "y", "c"))
    order = ["ln_g", "ln_b", "a_w_in", "a_b_in", "a_conv_w", "a_conv_b", "a_norm_g", "a_norm_b", "a_w_out",
             "a_b_out", "kv_w_down", "kv_norm_g", "kv_w_uk", "kv_w_uv", "b_w_in", "b_q_norm_g", "b_w_uq",
             "b_w_out"]
    outs = {**big_out, **small_out}
    result = [loss, grad_x[None]]
    for part in range(4):
        result += [outs[n][part] for n in order]
    return tuple(result)
```

```python
import functools
import math

import jax
import jax.numpy as jnp
from jax import lax
from jax.experimental import pallas as pl
from jax.experimental.pallas import tpu as pltpu

F32, BF16 = jnp.float32, jnp.bfloat16
NN = (((1,), (0,)), ((), ()))
NT = (((1,), (1,)), ((), ()))
TN = (((0,), (0,)), ((), ()))
MESH = pl.DeviceIdType.MESH
ANY = pl.BlockSpec(memory_space=pl.ANY)

LANES = 128
BF16_ROWS = 16
VMEM_LIMIT = 56 * 1024 * 1024
N_CHIPS = 4
LN_EPS = 1e-5
RMS_EPS = 1e-6
MASK_VALUE = -1e30
ROPE_THETA = 10000.0
ROPE_DIM = 64
ROPE_HALF = ROPE_DIM // 2
ADAM_LR, ADAM_B1, ADAM_B2, ADAM_EPS, ADAM_WD, ADAM_STEP = 0.001, 0.9, 0.999, 1e-08, 0.01, 10

MAX_TILE = 2048
TM_WIDE = 256
TQ = 512
CONV_HALO = 32
CONV_LC = 512
CONV_RC = 32
ADAM_ROWS = 64


def _dot(a, b, dims):
    return lax.dot_general(a.astype(BF16), b.astype(BF16), dims, preferred_element_type=F32)


def _sig(x):
    return 1.0 / (1.0 + jnp.exp(-x))


def _params(n_axes):
    return pltpu.CompilerParams(dimension_semantics=("arbitrary",) * n_axes, vmem_limit_bytes=VMEM_LIMIT)


def _gcd(*v):
    return functools.reduce(math.gcd, v)


def _fit(want, dim):
    return math.gcd(min(want, MAX_TILE), dim)


def _row_mm(name, a_segs, b, *, nt, tm, tn, tk, outs, epi, epi_ins=()):
    M = a_segs[0][0][0].shape[0]
    N = b.shape[0] if nt else b.shape[1]
    nkb = [arrs[0].shape[1] // tk for arrs, _ in a_segs]
    koff = [sum(nkb[:s]) for s in range(len(nkb))]
    ni, nj, nk = M // tm, N // tn, sum(nkb)
    assert M % tm == 0 and N % tn == 0 and all(arrs[0].shape[1] % tk == 0 for arrs, _ in a_segs), name
    assert (b.shape[1] if nt else b.shape[0]) == nk * tk, name

    def spec_of(shape, kind):
        if isinstance(kind, pl.BlockSpec):
            return kind
        if kind == 'tile':
            return pl.BlockSpec((tm, tn), lambda i, j, k: (i, j))
        if kind == 'row':
            return pl.BlockSpec((tm, shape[1]), lambda i, j, k: (i, 0))
        if kind == 'col':
            return pl.BlockSpec((1, tn), lambda i, j, k: (0, j))
        assert kind == 'acc' and nj == 1, name
        return pl.BlockSpec(shape, lambda i, j, k: (0,) * len(shape))

    in_specs, operands = [], []
    for s, (arrs, _) in enumerate(a_segs):
        for arr in arrs:
            in_specs.append(pl.BlockSpec(
                (tm, tk), lambda i, j, k, s=s: (i, jnp.clip(k - koff[s], 0, nkb[s] - 1))))
            operands.append(arr)
    in_specs.append(pl.BlockSpec((tn, tk), lambda i, j, k: (j, k)) if nt
                    else pl.BlockSpec((tk, tn), lambda i, j, k: (k, j)))
    operands.append(b)
    for arr, kind in epi_ins:
        in_specs.append(spec_of(arr.shape, kind))
        operands.append(arr)
    out_specs = [spec_of(shape, kind) for shape, _, kind in outs]
    out_shape = [jax.ShapeDtypeStruct(shape, dtype) for shape, dtype, _ in outs]
    n_seg_refs = [len(arrs) for arrs, _ in a_segs]

    def body(*refs):
        pos = 0
        seg_refs = []
        for n in n_seg_refs:
            seg_refs.append(refs[pos:pos + n])
            pos += n
        b_ref = refs[pos]
        e_refs = refs[pos + 1:pos + 1 + len(epi_ins)]
        o_refs = refs[pos + 1 + len(epi_ins):pos + 1 + len(epi_ins) + len(outs)]
        i, j, k = pl.program_id(0), pl.program_id(1), pl.program_id(2)

        def product(fn, rs):
            a = rs[0][...] if fn is None else fn(*[r[...] for r in rs])
            return _dot(a, b_ref[...], NT if nt else NN)

        def finish(acc):
            res = epi(acc, [r[...] for r in e_refs], i, j)
            for o_ref, (_, _, kind), r in zip(o_refs, outs, res):
                if isinstance(kind, str) and kind == 'acc':
                    @pl.when(i == 0)
                    def _(o_ref=o_ref, r=r):
                        o_ref[...] = r

                    @pl.when(i > 0)
                    def _(o_ref=o_ref, r=r):
                        o_ref[...] += r
                else:
                    o_ref[...] = r.astype(o_ref.dtype)

        if nk == 1:
            finish(product(a_segs[0][1], seg_refs[0]))
            return
        acc_ref = refs[-1]

        @pl.when(k == 0)
        def _():
            acc_ref[...] = jnp.zeros_like(acc_ref)

        for s, ((_, fn), rs) in enumerate(zip(a_segs, seg_refs)):
            def accumulate(fn=fn, rs=rs):
                acc_ref[...] += product(fn, rs)
            if len(a_segs) == 1:
                accumulate()
            else:
                pl.when(jnp.logical_and(k >= koff[s], k < koff[s] + nkb[s]))(accumulate)

        @pl.when(k == nk - 1)
        def _():
            finish(acc_ref[...])

    return pl.pallas_call(
        body, name=name, grid=(ni, nj, nk), in_specs=in_specs, out_specs=out_specs, out_shape=out_shape,
        scratch_shapes=[] if nk == 1 else [pltpu.VMEM((tm, tn), F32)], compiler_params=_params(3),
    )(*operands)


def _tn_mm(name, a_arrs, a_fn, b_segs, *, tn, tk, out_dtype, shard_major=False, colsum=False):
    T, M = a_arrs[0].shape
    nbj = [arrs[0].shape[1] // tn for arrs, _ in b_segs]
    joff = [sum(nbj[:s]) for s in range(len(nbj))]
    nj, nk = sum(nbj), T // tk
    N = nj * tn
    assert T % tk == 0 and all(arrs[0].shape[1] % tn == 0 for arrs, _ in b_segs), name

    in_specs = [pl.BlockSpec((tk, M), lambda j, k: (k, 0)) for _ in a_arrs]
    operands = list(a_arrs)
    for s, (arrs, _) in enumerate(b_segs):
        for arr in arrs:
            in_specs.append(pl.BlockSpec(
                (tk, tn), lambda j, k, s=s: (k, jnp.clip(j - joff[s], 0, nbj[s] - 1))))
            operands.append(arr)
    if shard_major:
        per = (N // N_CHIPS) // tn
        assert per * tn * N_CHIPS == N, name
        out_shape = [jax.ShapeDtypeStruct((N_CHIPS, M, N // N_CHIPS), out_dtype)]
        out_specs = [pl.BlockSpec((1, M, tn), lambda j, k: (j // per, 0, j % per))]
    else:
        out_shape = [jax.ShapeDtypeStruct((M, N), out_dtype)]
        out_specs = [pl.BlockSpec((M, tn), lambda j, k: (0, j))]
    if colsum:
        out_shape.append(jax.ShapeDtypeStruct((1, N), F32))
        out_specs.append(pl.BlockSpec((1, tn), lambda j, k: (0, j)))
    n_a = len(a_arrs)
    n_seg_refs = [len(arrs) for arrs, _ in b_segs]

    def body(*refs):
        a_refs = refs[:n_a]
        pos = n_a
        seg_refs = []
        for n in n_seg_refs:
            seg_refs.append(refs[pos:pos + n])
            pos += n
        o_ref = refs[pos]
        cs_ref = refs[pos + 1] if colsum else None
        acc_ref = refs[-1]
        j, k = pl.program_id(0), pl.program_id(1)

        @pl.when(k == 0)
        def _():
            acc_ref[...] = jnp.zeros_like(acc_ref)
            if colsum:
                cs_ref[...] = jnp.zeros_like(cs_ref)

        for s, ((_, fn), rs) in enumerate(zip(b_segs, seg_refs)):
            def accumulate(fn=fn, rs=rs):
                a = a_refs[0][...] if a_fn is None else a_fn(*[r[...] for r in a_refs])
                bt = rs[0][...] if fn is None else fn(*[r[...] for r in rs])
                acc_ref[...] += _dot(a, bt, TN)
                if colsum:
                    cs_ref[...] += jnp.sum(bt.astype(F32), axis=0, keepdims=True)
            if len(b_segs) == 1:
                accumulate()
            else:
                pl.when(jnp.logical_and(j >= joff[s], j < joff[s] + nbj[s]))(accumulate)

        @pl.when(k == nk - 1)
        def _():
            if shard_major:
                o_ref[0] = acc_ref[...].astype(o_ref.dtype)
            else:
                o_ref[...] = acc_ref[...].astype(o_ref.dtype)

    res = pl.pallas_call(
        body, name=name, grid=(nj, nk), in_specs=in_specs, out_specs=out_specs, out_shape=out_shape,
        scratch_shapes=[pltpu.VMEM((M, tn), F32)], compiler_params=_params(2),
    )(*operands)
    return res if colsum else res[0]


def _silu(z):
    return z * _sig(z)


def _silu_grad(z):
    s = _sig(z)
    return s * (1.0 + z * (1.0 - s))


def _gate(o, z):
    return o * _silu(z)


def _ln_stats(r):
    mu = jnp.mean(r, axis=1, keepdims=True)
    xc = r - mu
    var = jnp.mean(xc * xc, axis=1, keepdims=True)
    rstd = lax.rsqrt(var + LN_EPS)
    return xc * rstd, rstd


def _ln_bwd(dy, xhat, rstd, g):
    dxh = dy * g
    m1 = jnp.mean(dxh, axis=1, keepdims=True)
    m2 = jnp.mean(dxh * xhat, axis=1, keepdims=True)
    return (rstd * (dxh - m1 - xhat * m2), jnp.sum(dy * xhat, axis=0, keepdims=True),
            jnp.sum(dy, axis=0, keepdims=True))


def _rms_fwd(x, g):
    rstd = lax.rsqrt(jnp.mean(x * x, axis=1, keepdims=True) + RMS_EPS)
    return x * rstd * g


def _rms_bwd(dy, x, g):
    rstd = lax.rsqrt(jnp.mean(x * x, axis=1, keepdims=True) + RMS_EPS)
    xn = x * rstd
    dxn = dy * g
    return rstd * (dxn - xn * jnp.mean(dxn * xn, axis=1, keepdims=True)), jnp.sum(dy * xn, axis=0, keepdims=True)


def _rope(x, cos, sin, transpose=False):
    parts = []
    for g in range(x.shape[1] // LANES):
        xg = x[:, g * LANES:(g + 1) * LANES]
        if transpose:
            parts.append(xg * cos + pltpu.roll(xg * sin, LANES // 2, 1))
        else:
            parts.append(xg * cos + pltpu.roll(xg, LANES // 2, 1) * sin)
    return parts[0] if len(parts) == 1 else jnp.concatenate(parts, axis=1)


def _conv_fwd(proj, conv_w, conv_b, E, tm):
    T = proj.shape[0]
    kc = conv_w.shape[0]
    lc, hb, rc = min(CONV_LC, E), CONV_HALO, min(CONV_RC, tm)
    nl, ni, ratio = E // lc, T // tm, tm // hb
    gate_off = E // lc

    def body(val_ref, gate_ref, valh_ref, gateh_ref, w_ref, cb_ref, u1_ref, ubuf):
        i = pl.program_id(1)
        ubuf[hb:, :] = val_ref[...] * _sig(gate_ref[...])
        halo = valh_ref[...] * _sig(gateh_ref[...])
        ubuf[0:hb, :] = jnp.where(i > 0, halo, 0.0)
        for r0 in range(0, tm, rc):
            acc = jnp.zeros((rc, lc), F32) + cb_ref[...]
            for k in range(kc):
                acc += w_ref[k:k + 1, :] * ubuf[r0 + hb - (kc - 1) + k:r0 + hb - (kc - 1) + k + rc, :]
            u1_ref[r0:r0 + rc, :] = acc

    return pl.pallas_call(
        body, name="conv_fwd", grid=(nl, ni),
        in_specs=[
            pl.BlockSpec((tm, lc), lambda l, i: (i, l)),
            pl.BlockSpec((tm, lc), lambda l, i: (i, gate_off + l)),
            pl.BlockSpec((hb, lc), lambda l, i: (jnp.maximum(i * ratio - 1, 0), l)),
            pl.BlockSpec((hb, lc), lambda l, i: (jnp.maximum(i * ratio - 1, 0), gate_off + l)),
            pl.BlockSpec((kc, lc), lambda l, i: (0, l)),
            pl.BlockSpec((1, lc), lambda l, i: (0, l)),
        ],
        out_specs=pl.BlockSpec((tm, lc), lambda l, i: (i, l)),
        out_shape=jax.ShapeDtypeStruct((T, E), F32),
        scratch_shapes=[pltpu.VMEM((hb + tm, lc), F32)], compiler_params=_params(2),
    )(proj, proj, proj, proj, conv_w, conv_b)


def _conv_post(u1, proj, norm_g, norm_b, E, tm):
    T = u1.shape[0]

    def body(u1_ref, z_ref, g_ref, b_ref, u4_ref):
        xhat, _ = _ln_stats(u1_ref[...])
        u4_ref[...] = (_silu(xhat * g_ref[...] + b_ref[...]) * _silu(z_ref[...])).astype(BF16)

    return pl.pallas_call(
        body, name="conv_post", grid=(T // tm,),
        in_specs=[pl.BlockSpec((tm, E), lambda i: (i, 0)), pl.BlockSpec((tm, E), lambda i: (i, 2)),
                  pl.BlockSpec((1, E), lambda i: (0, 0)), pl.BlockSpec((1, E), lambda i: (0, 0))],
        out_specs=pl.BlockSpec((tm, E), lambda i: (i, 0)),
        out_shape=jax.ShapeDtypeStruct((T, E), BF16), compiler_params=_params(1),
    )(u1, proj, norm_g, norm_b)


def _conv_bwd(du1, proj, conv_w, E, tm):
    T = du1.shape[0]
    kc = conv_w.shape[0]
    lc, hb, rc = min(CONV_LC, E), CONV_HALO, min(CONV_RC, tm)
    nl, ni, ratio = E // lc, T // tm, tm // hb
    gate_off = E // lc
    last_halo = T // hb - 1

    def body(du_ref, dun_ref, val_ref, gate_ref, valh_ref, gateh_ref, w_ref,
             dval_ref, dgate_ref, dw_ref, db_ref, ubuf, dbuf):
        i = pl.program_id(1)
        val, sg = val_ref[...], _sig(gate_ref[...])
        ubuf[hb:, :] = val * sg
        halo = valh_ref[...] * _sig(gateh_ref[...])
        ubuf[0:hb, :] = jnp.where(i > 0, halo, 0.0)
        dbuf[0:tm, :] = du_ref[...]
        dbuf[tm:, :] = jnp.where(i < ni - 1, dun_ref[...], 0.0)

        @pl.when(i == 0)
        def _():
            dw_ref[...] = jnp.zeros_like(dw_ref)
            db_ref[...] = jnp.zeros_like(db_ref)

        db_ref[...] += jnp.sum(du_ref[...], axis=0, keepdims=True)
        for k in range(kc):
            acc = jnp.zeros((rc, lc), F32)
            for r0 in range(0, tm, rc):
                acc += ubuf[r0 + hb - (kc - 1) + k:r0 + hb - (kc - 1) + k + rc, :] * dbuf[r0:r0 + rc, :]
            dw_ref[k:k + 1, :] += jnp.sum(acc, axis=0, keepdims=True)
        for r0 in range(0, tm, rc):
            acc = jnp.zeros((rc, lc), F32)
            for k in range(kc):
                acc += w_ref[k:k + 1, :] * dbuf[r0 + (kc - 1) - k:r0 + (kc - 1) - k + rc, :]
            v, s = val[r0:r0 + rc, :], sg[r0:r0 + rc, :]
            dval_ref[r0:r0 + rc, :] = (acc * s).astype(BF16)
            dgate_ref[r0:r0 + rc, :] = (acc * v * s * (1.0 - s)).astype(BF16)

    return pl.pallas_call(
        body, name="conv_bwd", grid=(nl, ni),
        in_specs=[
            pl.BlockSpec((tm, lc), lambda l, i: (i, l)),
            pl.BlockSpec((hb, lc), lambda l, i: (jnp.minimum((i + 1) * ratio, last_halo), l)),
            pl.BlockSpec((tm, lc), lambda l, i: (i, l)),
            pl.BlockSpec((tm, lc), lambda l, i: (i, gate_off + l)),
            pl.BlockSpec((hb, lc), lambda l, i: (jnp.maximum(i * ratio - 1, 0), l)),
            pl.BlockSpec((hb, lc), lambda l, i: (jnp.maximum(i * ratio - 1, 0), gate_off + l)),
            pl.BlockSpec((kc, lc), lambda l, i: (0, l)),
        ],
        out_specs=[pl.BlockSpec((tm, lc), lambda l, i: (i, l)), pl.BlockSpec((tm, lc), lambda l, i: (i, l)),
                   pl.BlockSpec((kc, lc), lambda l, i: (0, l)), pl.BlockSpec((1, lc), lambda l, i: (0, l))],
        out_shape=[jax.ShapeDtypeStruct((T, E), BF16), jax.ShapeDtypeStruct((T, E), BF16),
                   jax.ShapeDtypeStruct((kc, E), F32), jax.ShapeDtypeStruct((1, E), F32)],
        scratch_shapes=[pltpu.VMEM((hb + tm, lc), F32), pltpu.VMEM((tm + hb, lc), F32)],
        compiler_params=_params(2),
    )(du1, du1, proj, proj, proj, proj, conv_w)


def _norm_prep(pb, kv_g, q_g, cos, sin, rkv, rq, wk, tm):
    T = pb.shape[0]

    def body(ckv_ref, cq_ref, kg_ref, qg_ref, cos_ref, sin_ref, c_ref, kr_ref, cqn_ref):
        blk = ckv_ref[...]
        c_ref[...] = _rms_fwd(blk[:, :rkv], kg_ref[...]).astype(BF16)
        kr_ref[...] = _rope(blk[:, rkv:rkv + LANES], cos_ref[...], sin_ref[...]).astype(BF16)
        cqn_ref[...] = _rms_fwd(cq_ref[...], qg_ref[...]).astype(BF16)

    return pl.pallas_call(
        body, name="norm_prep", grid=(T // tm,),
        in_specs=[pl.BlockSpec((tm, wk), lambda i: (i, 0)), pl.BlockSpec((tm, rq), lambda i: (i, wk // rq)),
                  pl.BlockSpec((1, rkv), lambda i: (0, 0)), pl.BlockSpec((1, rq), lambda i: (0, 0)),
                  pl.BlockSpec((tm, LANES), lambda i: (i, 0)), pl.BlockSpec((tm, LANES), lambda i: (i, 0))],
        out_specs=[pl.BlockSpec((tm, rkv), lambda i: (i, 0)), pl.BlockSpec((tm, LANES), lambda i: (i, 0)),
                   pl.BlockSpec((tm, rq), lambda i: (i, 0))],
        out_shape=[jax.ShapeDtypeStruct((T, rkv), BF16), jax.ShapeDtypeStruct((T, LANES), BF16),
                   jax.ShapeDtypeStruct((T, rq), BF16)],
        compiler_params=_params(1),
    )(pb, pb, kv_g, q_g, cos, sin)


def _attn_fwd(q_all, kv, kr, H, tq, scale):
    T = q_all.shape[0]
    nq = T // tq
    pair = 2
    W = pair * LANES
    assert H % pair == 0
    hp_n = H // pair

    def body(qn_ref, qr_ref, kn_ref, kr_ref, v_ref, o_ref, lse_ref, *scratch):
        qi = pl.program_id(1)
        chains = [scratch[3 * a:3 * a + 3] for a in range(pair)]
        lanes = [slice(a * LANES, (a + 1) * LANES) for a in range(pair)]
        for m_sc, l_sc, acc_sc in chains:
            m_sc[...] = jnp.full_like(m_sc, MASK_VALUE)
            l_sc[...] = jnp.zeros_like(l_sc)
            acc_sc[...] = jnp.zeros_like(acc_sc)

        def block(j, masked):
            rows = pl.ds(pl.multiple_of(j * tq, tq), tq)
            krope = kr_ref[rows, :]
            for a, (m_sc, l_sc, acc_sc) in enumerate(chains):
                q = jnp.concatenate([qn_ref[:, lanes[a]], qr_ref[:, lanes[a]]], axis=1)
                k = jnp.concatenate([kn_ref[rows, lanes[a]], krope], axis=1)
                s = _dot(q, k, NT) * scale
                if masked:
                    row = lax.broadcasted_iota(jnp.int32, s.shape, 0)
                    col = lax.broadcasted_iota(jnp.int32, s.shape, 1)
                    s = jnp.where(col <= row, s, MASK_VALUE)
                m_old = m_sc[...]
                m_new = jnp.maximum(m_old, jnp.max(s, axis=1, keepdims=True))
                p = jnp.exp(s - m_new)
                alpha = jnp.exp(m_old - m_new)
                l_sc[...] = alpha * l_sc[...] + jnp.sum(p, axis=1, keepdims=True)
                acc_sc[...] = alpha * acc_sc[...] + _dot(p, v_ref[rows, lanes[a]], NN)
                m_sc[...] = m_new

        def unmasked(j, carry):
            block(j, False)
            return carry

        lax.fori_loop(0, qi, unmasked, 0)
        block(qi, True)
        for a, (m_sc, l_sc, acc_sc) in enumerate(chains):
            o_ref[:, lanes[a]] = acc_sc[...] / l_sc[...]
            lse_ref[a] = m_sc[...] + jnp.log(l_sc[...])

    chain_scratch = [pltpu.VMEM((tq, 1), F32), pltpu.VMEM((tq, 1), F32), pltpu.VMEM((tq, LANES), F32)]
    return pl.pallas_call(
        body, name="attn_fwd", grid=(hp_n, nq),
        in_specs=[pl.BlockSpec((tq, W), lambda hp, qi: (qi, hp)),
                  pl.BlockSpec((tq, W), lambda hp, qi: (qi, hp_n + hp)),
                  pl.BlockSpec((T, W), lambda hp, qi: (0, hp)),
                  pl.BlockSpec((T, LANES), lambda hp, qi: (0, 0)),
                  pl.BlockSpec((T, W), lambda hp, qi: (0, hp_n + hp))],
        out_specs=[pl.BlockSpec((tq, W), lambda hp, qi: (qi, hp)),
                   pl.BlockSpec((pair, tq, 1), lambda hp, qi: (hp, qi, 0))],
        out_shape=[jax.ShapeDtypeStruct((T, H * LANES), F32), jax.ShapeDtypeStruct((H, T, 1), F32)],
        scratch_shapes=chain_scratch * pair, compiler_params=_params(2),
    )(q_all, q_all, kv, kr, kv)


def _attn_bwd(q_all, kv, kr, do, o, lse, H, tq, scale):
    T = q_all.shape[0]
    nq = T // tq
    HV = H * LANES
    pair = 2
    tk2 = pair * tq
    ng = T // tk2
    assert ng * tk2 == T

    def body(qn_ref, qr_ref, kn_ref, kr_ref, v_ref, do_ref, o_ref, lse_ref,
             dqn_ref, dqr_ref, dkn_ref, dkr_ref, dv_ref, dq_sc, dk_sc, dv_sc):
        g = pl.program_id(1)

        @pl.when(g == 0)
        def _():
            dq_sc[...] = jnp.zeros_like(dq_sc)

        dk_sc[...] = jnp.zeros_like(dk_sc)
        dv_sc[...] = jnp.zeros_like(dv_sc)

        def block(qi, modes):
            rows = pl.ds(pl.multiple_of(qi * tq, tq), tq)
            q = jnp.concatenate([qn_ref[rows, :], qr_ref[rows, :]], axis=1)
            dov = do_ref[rows, :]
            delta = jnp.sum(dov.astype(F32) * o_ref[rows, :], axis=1, keepdims=True)
            lse_q = lse_ref[0, rows, :]
            dq = None
            for c, masked in enumerate(modes):
                if masked is None:
                    continue
                kr_ = slice(c * tq, (c + 1) * tq)
                k = jnp.concatenate([kn_ref[kr_, :], kr_ref[kr_, :]], axis=1)
                s = _dot(q, k, NT) * scale
                if masked:
                    row = lax.broadcasted_iota(jnp.int32, s.shape, 0)
                    col = lax.broadcasted_iota(jnp.int32, s.shape, 1)
                    s = jnp.where(col <= row, s, MASK_VALUE)
                p = jnp.exp(s - lse_q)
                dv_sc[kr_, :] += _dot(p, dov, TN)
                dp = _dot(dov, v_ref[kr_, :], NT)
                ds = (p * (dp - delta) * scale).astype(BF16)
                dk_sc[kr_, :] += _dot(ds, q, TN)
                part = _dot(ds, k, NN)
                dq = part if dq is None else dq + part
            dq_sc[rows, :] += dq

        block(pair * g, (True, None))
        block(pair * g + 1, (False, True))

        def below(qi, carry):
            block(qi, (False, False))
            return carry

        lax.fori_loop(pair * g + pair, nq, below, 0)
        dkn_ref[...] = dk_sc[:, :LANES].astype(BF16)
        dkr_ref[...] = dk_sc[:, LANES:]
        dv_ref[...] = dv_sc[...].astype(BF16)

        @pl.when(g == ng - 1)
        def _():
            dqn_ref[...] = dq_sc[:, :LANES].astype(BF16)
            dqr_ref[...] = dq_sc[:, LANES:]

    whole = lambda col: pl.BlockSpec((T, LANES), col)
    tile = lambda col: pl.BlockSpec((tk2, LANES), col)
    return pl.pallas_call(
        body, name="attn_bwd", grid=(H, ng),
        in_specs=[whole(lambda h, g: (0, h)), whole(lambda h, g: (0, H + h)),
                  tile(lambda h, g: (g, h)), tile(lambda h, g: (g, 0)), tile(lambda h, g: (g, H + h)),
                  whole(lambda h, g: (0, h)), whole(lambda h, g: (0, h)),
                  pl.BlockSpec((1, T, 1), lambda h, g: (h, 0, 0))],
        out_specs=[whole(lambda h, g: (0, h)), whole(lambda h, g: (0, h)),
                   tile(lambda h, g: (g, h)), tile(lambda h, g: (g, h)), tile(lambda h, g: (g, h))],
        out_shape=[jax.ShapeDtypeStruct((T, HV), BF16), jax.ShapeDtypeStruct((T, HV), F32),
                   jax.ShapeDtypeStruct((T, HV), BF16), jax.ShapeDtypeStruct((T, HV), F32),
                   jax.ShapeDtypeStruct((T, HV), BF16)],
        scratch_shapes=[pltpu.VMEM((T, 2 * LANES), F32), pltpu.VMEM((tk2, 2 * LANES), F32),
                        pltpu.VMEM((tk2, LANES), F32)],
        compiler_params=_params(2),
    )(q_all, q_all, kv, kr, kv, do, o, lse)


def _rope_bwd(dqr, dkr_heads, cos, sin, H, tm):
    T, HV = dqr.shape

    def body(dqr_ref, dkr_ref, cos_ref, sin_ref, dq_ref, dk_ref):
        c, s = cos_ref[...], sin_ref[...]
        dq_ref[...] = _rope(dqr_ref[...], c, s, transpose=True).astype(BF16)
        dk = dkr_ref[...]
        tot = dk[:, 0:LANES]
        for h in range(1, H):
            tot = tot + dk[:, h * LANES:(h + 1) * LANES]
        dk_ref[...] = _rope(tot, c, s, transpose=True)

    return pl.pallas_call(
        body, name="rope_bwd", grid=(T // tm,),
        in_specs=[pl.BlockSpec((tm, HV), lambda i: (i, 0)), pl.BlockSpec((tm, HV), lambda i: (i, 0)),
                  pl.BlockSpec((tm, LANES), lambda i: (i, 0)), pl.BlockSpec((tm, LANES), lambda i: (i, 0))],
        out_specs=[pl.BlockSpec((tm, HV), lambda i: (i, 0)), pl.BlockSpec((tm, LANES), lambda i: (i, 0))],
        out_shape=[jax.ShapeDtypeStruct((T, HV), BF16), jax.ShapeDtypeStruct((T, LANES), F32)],
        compiler_params=_params(1),
    )(dqr, dkr_heads, cos, sin)


def _place():
    x, y, c = lax.axis_index("x"), lax.axis_index("y"), lax.axis_index("c")
    chips = [(1 - x, y), (x, 1 - y), (1 - x, 1 - y)]
    return x, y, c, chips


def _all_gather_chips(pack):
    rows = pack.shape[0]
    half = rows // 2
    assert half * 2 == rows and half % BF16_ROWS == 0

    def body(w_ref, out_ref, send_sems, recv_sems, local_sem):
        x, y, c, chips = _place()
        sibling = (x, y, 1 - c)

        def region(px, py, pc):
            return out_ref.at[2 * px + py, pl.ds(pc * half, half), :]

        def copy(k, block, to, src=None):
            return pltpu.make_async_remote_copy(
                src_ref=region(*block) if src is None else src, dst_ref=region(*block),
                send_sem=send_sems.at[k], recv_sem=recv_sems.at[k], device_id=to, device_id_type=MESH)

        mine = pltpu.make_async_copy(w_ref, out_ref.at[2 * x + y], local_sem)
        mine.start()
        my_half = w_ref.at[pl.ds(c * half, half), :]
        first = [copy(j, (x, y, c), (*chip, c), src=my_half) for j, chip in enumerate(chips)]
        for cp in first:
            cp.start()
        passed = [copy(3 + j, (*chip, c), sibling) for j, chip in enumerate(chips)]
        for j, chip in enumerate(chips):
            copy(j, (*chip, c), (x, y, c)).wait_recv()
            passed[j].start()
        for j, chip in enumerate(chips):
            copy(3 + j, (*chip, 1 - c), (x, y, c)).wait_recv()
        for cp in first + passed:
            cp.wait_send()
        mine.wait()

    return pl.pallas_call(
        body, name="gather_weights", in_specs=[ANY], out_specs=ANY,
        out_shape=jax.ShapeDtypeStruct((N_CHIPS,) + pack.shape, pack.dtype),
        scratch_shapes=[pltpu.SemaphoreType.DMA((6,)), pltpu.SemaphoreType.DMA((6,)), pltpu.SemaphoreType.DMA],
    )(pack)


def _scatter_grads(grads):
    n = len(grads)

    def body(*refs):
        g_refs, r_refs = refs[:n], refs[n:2 * n]
        send_sems, recv_sems, local_sems = refs[2 * n:]
        x, y, c, chips = _place()
        me = 2 * x + y
        locals_, sends = [], []
        for w in range(n):
            cp = pltpu.make_async_copy(g_refs[w].at[me], r_refs[w].at[me], local_sems.at[w])
            cp.start()
            locals_.append(cp)
            for j, (px, py) in enumerate(chips):
                cp = pltpu.make_async_remote_copy(
                    src_ref=g_refs[w].at[2 * px + py], dst_ref=r_refs[w].at[me],
                    send_sem=send_sems.at[3 * w + j], recv_sem=recv_sems.at[3 * w + j],
                    device_id=(px, py, c), device_id_type=MESH)
                cp.start()
                sends.append(cp)
        for w in range(n):
            for j, (px, py) in enumerate(chips):
                pltpu.make_async_remote_copy(
                    src_ref=g_refs[w].at[me], dst_ref=r_refs[w].at[2 * px + py],
                    send_sem=send_sems.at[3 * w + j], recv_sem=recv_sems.at[3 * w + j],
                    device_id=(px, py, c), device_id_type=MESH).wait_recv()
        for cp in sends:
            cp.wait_send()
        for cp in locals_:
            cp.wait()

    return pl.pallas_call(
        body, name="scatter_grads", in_specs=[ANY] * n, out_specs=[ANY] * n,
        out_shape=[jax.ShapeDtypeStruct(g.shape, g.dtype) for g in grads],
        scratch_shapes=[pltpu.SemaphoreType.DMA((3 * n,)), pltpu.SemaphoreType.DMA((3 * n,)),
                        pltpu.SemaphoreType.DMA((n,))],
    )(*grads)


def _swap_cores(parts):
    n = len(parts)

    def body(*refs):
        p_refs, r_refs = refs[:n], refs[n:2 * n]
        send_sems, recv_sems = refs[2 * n:]
        x, y, c, _ = _place()
        copies = [pltpu.make_async_remote_copy(
            src_ref=p_refs[w], dst_ref=r_refs[w], send_sem=send_sems.at[w], recv_sem=recv_sems.at[w],
            device_id=(x, y, 1 - c), device_id_type=MESH) for w in range(n)]
        for cp in copies:
            cp.start()
        for cp in copies:
            cp.wait()

    return pl.pallas_call(
        body, name="swap_cores", in_specs=[ANY] * n, out_specs=[ANY] * n,
        out_shape=[jax.ShapeDtypeStruct(p.shape, p.dtype) for p in parts],
        scratch_shapes=[pltpu.SemaphoreType.DMA((n,)), pltpu.SemaphoreType.DMA((n,))],
    )(*parts)


def _all_reduce_small(part):
    n_dev = 8

    def body(p_ref, out_ref, buf, send_sems, recv_sems):
        x, y, c, _ = _place()
        me = 4 * x + 2 * y + c
        buf[me] = p_ref[...]
        copies = []
        for k in range(1, n_dev):
            kx, ky, kc = (k >> 2) & 1, (k >> 1) & 1, k & 1
            peer = (x ^ kx, y ^ ky, c ^ kc)
            cp = pltpu.make_async_remote_copy(
                src_ref=p_ref, dst_ref=buf.at[me], send_sem=send_sems.at[k - 1], recv_sem=recv_sems.at[k - 1],
                device_id=peer, device_id_type=MESH)
            cp.start()
            copies.append(cp)
        for k in range(1, n_dev):
            pltpu.make_async_remote_copy(
                src_ref=p_ref, dst_ref=buf.at[me ^ k], send_sem=send_sems.at[k - 1], recv_sem=recv_sems.at[k - 1],
                device_id=(x, y, c), device_id_type=MESH).wait_recv()
        for cp in copies:
            cp.wait_send()
        tot = buf[0]
        for d in range(1, n_dev):
            tot = tot + buf[d]
        out_ref[...] = tot

    vm = pl.BlockSpec(memory_space=pltpu.VMEM)
    return pl.pallas_call(
        body, name="all_reduce_small", in_specs=[vm], out_specs=vm,
        out_shape=jax.ShapeDtypeStruct(part.shape, F32),
        scratch_shapes=[pltpu.VMEM((n_dev,) + part.shape, F32), pltpu.SemaphoreType.DMA((n_dev - 1,)),
                        pltpu.SemaphoreType.DMA((n_dev - 1,))],
    )(part)


def _sum_chips(name, r):
    _, rows, cols = r.shape
    tr = _gcd(ADAM_ROWS, rows)

    def body(r_ref, o_ref):
        tot = r_ref[0].astype(F32)
        for q in range(1, N_CHIPS):
            tot = tot + r_ref[q].astype(F32)
        o_ref[...] = tot

    return pl.pallas_call(
        body, name=name, grid=(rows // tr,),
        in_specs=[pl.BlockSpec((N_CHIPS, tr, cols), lambda i: (0, i, 0))],
        out_specs=pl.BlockSpec((tr, cols), lambda i: (i, 0)),
        out_shape=jax.ShapeDtypeStruct((rows, cols), F32), compiler_params=_params(1),
    )(r)


def _adamw(name, g_parts, w, m, v):
    rows, cols = w.shape
    tr = _gcd(ADAM_ROWS, rows)
    n = len(g_parts)

    def body(*refs):
        g = refs[0][...]
        for r in refs[1:n]:
            g = g + r[...]
        w_ref, m_ref, v_ref, go_ref, d_ref, mo_ref, vo_ref = refs[n:]
        mn = ADAM_B1 * m_ref[...] + (1.0 - ADAM_B1) * g
        vn = ADAM_B2 * v_ref[...] + (1.0 - ADAM_B2) * jnp.square(g)
        m_hat = mn / (1.0 - ADAM_B1 ** ADAM_STEP)
        v_hat = vn / (1.0 - ADAM_B2 ** ADAM_STEP)
        go_ref[...] = g
        d_ref[...] = -ADAM_LR * (m_hat / (jnp.sqrt(v_hat) + ADAM_EPS) + ADAM_WD * w_ref[...])
        mo_ref[...] = mn
        vo_ref[...] = vn

    spec = pl.BlockSpec((tr, cols), lambda i: (i, 0))
    return pl.pallas_call(
        body, name=name, grid=(rows // tr,), in_specs=[spec] * (n + 3), out_specs=[spec] * 4,
        out_shape=[jax.ShapeDtypeStruct((rows, cols), F32)] * 4, compiler_params=_params(1),
    )(*g_parts, w, m, v)


def _pack_rows(flat, dtype, multiple):
    n = flat.shape[0]
    total = -(-n // multiple) * multiple
    return jnp.pad(flat, (0, total - n)).astype(dtype).reshape(total // LANES, LANES)


def kernel(x, positions, ln_g, ln_b, a_w_in, a_b_in, a_conv_w, a_conv_b, a_norm_g, a_norm_b, a_w_out, a_b_out, kv_w_down, kv_norm_g, kv_w_uk, kv_w_uv, b_w_in, b_q_norm_g, b_w_uq, b_w_out, loss_target, m_ln_g, m_ln_b, m_a_w_in, m_a_b_in, m_a_conv_w, m_a_conv_b, m_a_norm_g, m_a_norm_b, m_a_w_out, m_a_b_out, m_kv_w_down, m_kv_norm_g, m_kv_w_uk, m_kv_w_uv, m_b_w_in, m_b_q_norm_g, m_b_w_uq, m_b_w_out, v_ln_g, v_ln_b, v_a_w_in, v_a_b_in, v_a_conv_w, v_a_conv_b, v_a_norm_g, v_a_norm_b, v_a_w_out, v_a_b_out, v_kv_w_down, v_kv_norm_g, v_kv_w_uk, v_kv_w_uv, v_b_w_in, v_b_q_norm_g, v_b_w_uq, v_b_w_out):
    T, D = x.shape[1], x.shape[2]
    E = N_CHIPS * a_w_out.shape[1]
    KC = a_conv_w.shape[1]
    RKV = kv_norm_g.shape[0]
    H, DN = kv_w_uk.shape[1], kv_w_uk.shape[2]
    RQ = b_q_norm_g.shape[1]
    HV = N_CHIPS * b_w_out.shape[1]
    assert DN == LANES and kv_w_uv.shape[2] == LANES and HV == H * LANES
    assert kv_w_down.shape[1] == RKV + ROPE_DIM and b_w_uq.shape[3] == DN + ROPE_DIM
    assert ln_g.shape[0] == 2 and a_w_in.shape[0] == 1 and b_w_in.shape[0] == 1
    alpha = (2.0 * ln_g.shape[0]) ** 0.25
    scale = 1.0 / math.sqrt(DN + ROPE_DIM)
    WK = -(-(RKV + LANES) // 256) * 256
    assert WK % RQ == 0
    Z_OFF = WK + RQ
    tmw, tq = min(TM_WIDE, T), min(TQ, T)
    t512, t1024 = _fit(512, T), _fit(1024, T)
    xs = x[0]
    xb = xs.astype(BF16)
    tgt = loss_target[0]
    px, py = lax.axis_index("x"), lax.axis_index("y")
    chip = 2 * px + py

    mats = [a_w_in[0], a_w_out[0], kv_w_down, kv_w_uk, kv_w_uv, b_w_in[0], b_w_uq[0], b_w_out[0]]
    vecs = [a_b_in[0], a_conv_w[0], a_conv_b[0], a_norm_g[0], a_norm_b[0], a_b_out[0]]
    flat = jnp.concatenate(
        [w.astype(BF16).reshape(-1) for w in mats]
        + [lax.bitcast_convert_type(w.reshape(-1), BF16).reshape(-1) for w in vecs])
    pack = _pack_rows(flat, BF16, 2 * BF16_ROWS * LANES)
    gathered = _all_gather_chips(pack).reshape(N_CHIPS, -1)
    off = 0
    full = []
    for w in mats:
        full.append(gathered[:, off:off + w.size].reshape((N_CHIPS,) + w.shape))
        off += w.size
    fvec = []
    for w in vecs:
        bits = gathered[:, off:off + 2 * w.size].reshape((N_CHIPS,) + w.shape + (2,))
        fvec.append(lax.bitcast_convert_type(bits, F32))
        off += 2 * w.size
    g_win, g_wout, g_wd, g_uk, g_uv, g_wbin, g_wuq, g_wbout = full
    cols = lambda g: jnp.moveaxis(g, 0, -2).reshape(g.shape[1:-1] + (N_CHIPS * g.shape[-1],))
    w_in = cols(g_win)
    w_out = g_wout.reshape(E, D)
    wd = g_wd.reshape(D, RKV + ROPE_DIM)
    zpad = jnp.zeros((D, ROPE_HALF), BF16)
    wd_p = jnp.concatenate(
        [wd[:, :RKV], wd[:, RKV:RKV + ROPE_HALF], zpad, wd[:, RKV + ROPE_HALF:], zpad,
         jnp.zeros((D, WK - RKV - LANES), BF16)], axis=1)
    w_bin = cols(g_wbin)
    w_z = w_bin[:, RQ:]
    wb_small = jnp.concatenate([wd_p, w_bin[:, :RQ]], axis=1)
    wb_all = jnp.concatenate([wd_p, w_bin], axis=1)
    w_kv = jnp.concatenate([g_uk.reshape(RKV, HV), g_uv.reshape(RKV, HV)], axis=1)
    wuq = g_wuq.reshape(RQ, H, DN + ROPE_DIM)
    zq = jnp.zeros((RQ, H, ROPE_HALF), BF16)
    w_qr = jnp.concatenate([wuq[:, :, DN:DN + ROPE_HALF], zq, wuq[:, :, DN + ROPE_HALF:], zq], axis=2)
    w_q = jnp.concatenate([wuq[:, :, :DN].reshape(RQ, HV), w_qr.reshape(RQ, HV)], axis=1)
    w_bout = g_wbout.reshape(HV, D)
    b_in = cols(fvec[0][:, None, :])
    conv_w = cols(fvec[1])
    conv_b, norm_g, norm_b, b_out = (cols(f[:, None, :]) for f in fvec[2:])

    freqs = ROPE_THETA ** (-jnp.arange(0, ROPE_DIM, 2, dtype=F32) / ROPE_DIM)
    ang = positions[0].astype(F32)[:, None] * freqs
    cs, sn = jnp.cos(ang), jnp.sin(ang)
    ones, zeros = jnp.ones_like(cs), jnp.zeros_like(cs)
    cos_t = jnp.concatenate([cs, ones, cs, ones], axis=1)
    sin_t = jnp.concatenate([-sn, zeros, sn, zeros], axis=1)

    row = lambda a: a.reshape(1, -1)
    g0, b0, g1, b1 = row(ln_g[0]), row(ln_b[0]), row(ln_g[1]), row(ln_b[1])
    kv_g, q_g = row(kv_norm_g), row(b_q_norm_g[0])
    plain = lambda acc, ins, i, j: [acc]

    (proj,) = _row_mm("a_in", [((xb,), None)], w_in, nt=False, tm=t1024, tn=_fit(1536, 3 * E), tk=_fit(1024, D),
                      outs=[((T, 3 * E), F32, 'tile')], epi=lambda acc, ins, i, j: [acc + ins[0]],
                      epi_ins=[(b_in, 'col')])
    u1 = _conv_fwd(proj, conv_w, conv_b, E, tmw)
    u4 = _conv_post(u1, proj, norm_g, norm_b, E, tmw)

    def ln_epi(acc, ins, i, j):
        bias, res, g, b = ins
        xhat, rstd = _ln_stats(alpha * res + acc + bias)
        h = xhat * g + b
        return [h, h, xhat, rstd]

    h1, h1b, xhat1, rstd1 = _row_mm(
        "a_out", [((u4,), None)], w_out, nt=False, tm=t512, tn=D, tk=_fit(2048, E),
        outs=[((T, D), F32, 'tile'), ((T, D), BF16, 'tile'), ((T, D), F32, 'tile'), ((T, 1), F32, 'row')],
        epi=ln_epi, epi_ins=[(b_out, 'col'), (xs, 'tile'), (g0, 'col'), (b0, 'col')])

    tkb = _fit(512, _gcd(WK, RQ, HV))
    (pb,) = _row_mm("b_in", [((h1b,), None)], wb_small, nt=False, tm=t1024, tn=_fit(1024, Z_OFF),
                    tk=_fit(1024, D), outs=[((T, Z_OFF), F32, 'tile')], epi=plain)
    (zb,) = _row_mm("b_in_gate", [((h1b,), None)], w_z, nt=False, tm=t1024, tn=_fit(2048, HV),
                    tk=_fit(1024, D), outs=[((T, HV), F32, 'tile')], epi=plain)
    c_lat, kr, cqn = _norm_prep(pb, kv_g, q_g, cos_t, sin_t, RKV, RQ, WK, tmw)
    (kv,) = _row_mm("kv_up", [((c_lat,), None)], w_kv, nt=False, tm=t1024, tn=_fit(2048, HV),
                    tk=_fit(1024, RKV), outs=[((T, 2 * HV), BF16, 'tile')], epi=plain)
    tnq = _fit(2048, HV)
    half_q = HV // tnq

    def q_epi(acc, ins, i, j):
        return [jnp.where(j >= half_q, _rope(acc, ins[0], ins[1]), acc)]

    (q_all,) = _row_mm("q_up", [((cqn,), None)], w_q, nt=False, tm=t1024, tn=tnq, tk=_fit(1024, RQ),
                       outs=[((T, 2 * HV), BF16, 'tile')], epi=q_epi,
                       epi_ins=[(cos_t, 'row'), (sin_t, 'row')])
    o, lse = _attn_fwd(q_all, kv, kr, H, tq, scale)

    def loss_epi(acc, ins, i, j):
        res, g, b, target = ins
        xhat, rstd = _ln_stats(alpha * res + acc)
        diff = xhat * g + b - target
        dr, dg, db = _ln_bwd(diff / D, xhat, rstd, g)
        return [dr, 0.5 * jnp.sum(diff * diff, keepdims=True) / D, dg, db]

    dr1, loss_part, dg1, db1 = _row_mm(
        "b_out", [((o, zb), _gate)], w_bout, nt=False, tm=tmw, tn=D, tk=_fit(2048, HV),
        outs=[((T, D), F32, 'tile'), ((1, 1), F32, 'acc'), ((1, D), F32, 'acc'), ((1, D), F32, 'acc')],
        epi=loss_epi, epi_ins=[(h1, 'tile'), (g1, 'col'), (b1, 'col'), (tgt, 'tile')])

    def gate_bwd_epi(acc, ins, i, j):
        return [acc * _silu(ins[1]), acc * ins[0] * _silu_grad(ins[1])]

    do, dz = _row_mm(
        "b_out_bwd", [((dr1,), None)], w_bout, nt=True, tm=tmw, tn=_fit(2048, HV), tk=_fit(1024, D),
        outs=[((T, HV), BF16, 'tile'), ((T, HV), BF16, 'tile')], epi=gate_bwd_epi,
        epi_ins=[(o, 'tile'), (zb, 'tile')])
    gw_bout = _tn_mm("dw_b_out", (o, zb), _gate, [((dr1,), None)], tn=_fit(1024, D), tk=t512, out_dtype=BF16)
    dqn, dqr, dkn, dkr_h, dv = _attn_bwd(q_all, kv, kr, do, o, lse, H, tq, scale)
    dqr_pre, dkr_pre = _rope_bwd(dqr, dkr_h, cos_t, sin_t, H, tmw)

    def cq_bwd_epi(acc, ins, i, j):
        dx, dg = _rms_bwd(acc, ins[0], ins[1])
        return [dx, dg]

    dcq, dqg = _row_mm(
        "q_up_bwd", [((dqn,), None), ((dqr_pre,), None)], w_q, nt=True, tm=tmw, tn=RQ, tk=_fit(2048, HV),
        outs=[((T, RQ), BF16, 'tile'), ((1, RQ), F32, 'acc')], epi=cq_bwd_epi,
        epi_ins=[(pb, pl.BlockSpec((tmw, RQ), lambda i, j, k: (i, WK // RQ))), (q_g, 'col')])
    gw_q = _tn_mm("dw_q_up", (cqn,), None, [((dqn,), None), ((dqr_pre,), None)],
                  tn=_fit(2048, HV), tk=t1024, out_dtype=BF16)

    def ckv_bwd_epi(acc, ins, i, j):
        blk, dkr_t, g = ins
        dx, dg = _rms_bwd(acc, blk[:, :RKV], g)
        parts = [dx, dkr_t]
        if WK > RKV + LANES:
            parts.append(jnp.zeros((dx.shape[0], WK - RKV - LANES), F32))
        return [jnp.concatenate(parts, axis=1), dg]

    dckv, dkvg = _row_mm(
        "kv_up_bwd", [((dkn,), None), ((dv,), None)], w_kv, nt=True, tm=tmw, tn=RKV, tk=_fit(2048, HV),
        outs=[((T, WK), BF16, pl.BlockSpec((tmw, WK), lambda i, j, k: (i, 0))), ((1, RKV), F32, 'acc')],
        epi=ckv_bwd_epi,
        epi_ins=[(pb, pl.BlockSpec((tmw, WK), lambda i, j, k: (i, 0))), (dkr_pre, 'row'), (kv_g, 'col')])
    gw_kv = _tn_mm("dw_kv_up", (c_lat,), None, [((dkn,), None), ((dv,), None)],
                   tn=_fit(2048, HV), tk=t1024, out_dtype=BF16)

    def ln1_bwd_epi(acc, ins, i, j):
        dr_up, xhat, rstd, g = ins
        dr, dg, db = _ln_bwd(alpha * dr_up + acc, xhat, rstd, g)
        return [dr, dg, db]

    dp_segs = [((dckv,), None), ((dcq,), None), ((dz,), None)]
    dr0, dg0, db0 = _row_mm(
        "b_in_bwd", dp_segs, wb_all, nt=True, tm=t512, tn=D, tk=tkb,
        outs=[((T, D), F32, 'tile'), ((1, D), F32, 'acc'), ((1, D), F32, 'acc')], epi=ln1_bwd_epi,
        epi_ins=[(dr1, 'tile'), (xhat1, 'tile'), (rstd1, 'row'), (g0, 'col')])
    gw_ball = _tn_mm("dw_b_in", (h1b,), None, dp_segs, tn=tkb, tk=t1024, out_dtype=BF16)

    def conv_branch_bwd_epi(acc, ins, i, j):
        u1_t, z, g, b = ins
        xhat, rstd = _ln_stats(u1_t)
        u2 = xhat * g + b
        du3 = acc * _silu(z)
        dz_a = acc * _silu(u2) * _silu_grad(z)
        du1, dg, db = _ln_bwd(du3 * _silu_grad(u2), xhat, rstd, g)
        return [du1, dz_a, dg, db]

    du1, dz_a, dng, dnb = _row_mm(
        "a_out_bwd", [((dr0,), None)], w_out, nt=True, tm=tmw, tn=E, tk=_fit(1024, D),
        outs=[((T, E), F32, 'tile'), ((T, E), BF16, 'tile'), ((1, E), F32, 'acc'), ((1, E), F32, 'acc')],
        epi=conv_branch_bwd_epi,
        epi_ins=[(u1, 'tile'), (proj, pl.BlockSpec((tmw, E), lambda i, j, k: (i, 2))), (norm_g, 'col'),
                 (norm_b, 'col')])
    gw_out, dbo = _tn_mm("dw_a_out", (u4,), None, [((dr0,), None)], tn=_fit(1024, D), tk=t1024,
                         out_dtype=BF16, colsum=True)
    dval, dgate, dcw, dcb = _conv_bwd(du1, proj, conv_w, E, tmw)
    dproj_segs = [((dval,), None), ((dgate,), None), ((dz_a,), None)]
    (grad_x,) = _row_mm(
        "a_in_bwd", dproj_segs, w_in, nt=True, tm=t512, tn=_fit(1024, D), tk=_fit(2048, E),
        outs=[((T, D), F32, 'tile')], epi=lambda acc, ins, i, j: [alpha * ins[0] + acc],
        epi_ins=[(dr0, 'tile')])
    gw_in, dbi = _tn_mm("dw_a_in", (xb,), None, dproj_segs, tn=_fit(2048, E), tk=t512, out_dtype=BF16,
                        colsum=True)

    shard_cols = lambda g: jnp.moveaxis(g.reshape(g.shape[0], N_CHIPS, -1), 1, 0)
    shard_rows = lambda g: g.reshape(N_CHIPS, g.shape[0] // N_CHIPS, g.shape[1])
    gq = gw_q.reshape(RQ, 2, H, LANES)
    g_uq = jnp.concatenate(
        [gq[:, 0], gq[:, 1, :, :ROPE_HALF], gq[:, 1, :, 2 * ROPE_HALF:3 * ROPE_HALF]], axis=2)
    g_wd_full = jnp.concatenate(
        [gw_ball[:, :RKV], gw_ball[:, RKV:RKV + ROPE_HALF],
         gw_ball[:, RKV + 2 * ROPE_HALF:RKV + 3 * ROPE_HALF]], axis=1)
    big_grads = [
        shard_cols(gw_in),
        shard_rows(gw_out),
        shard_rows(g_wd_full),
        shard_rows(gw_kv[:, :HV]),
        shard_rows(gw_kv[:, HV:]),
        shard_cols(gw_ball[:, WK:]),
        shard_rows(g_uq.reshape(RQ, H * (DN + ROPE_DIM))),
        shard_rows(gw_bout),
    ]
    received = _scatter_grads(big_grads)
    names = ["a_w_in", "a_w_out", "kv_w_down", "kv_w_uk", "kv_w_uv", "b_w_in", "b_w_uq", "b_w_out"]
    core_sums = [_sum_chips("sum_" + n, r) for n, r in zip(names, received)]
    sibling_sums = _swap_cores(core_sums)
    big_w = [a_w_in, a_w_out, kv_w_down, kv_w_uk, kv_w_uv, b_w_in, b_w_uq, b_w_out]
    big_m = [m_a_w_in, m_a_w_out, m_kv_w_down, m_kv_w_uk, m_kv_w_uv, m_b_w_in, m_b_w_uq, m_b_w_out]
    big_v = [v_a_w_in, v_a_w_out, v_kv_w_down, v_kv_w_uk, v_kv_w_uv, v_b_w_in, v_b_w_uq, v_b_w_out]
    big_out = {}
    for n, mine, theirs, w, m, v in zip(names, core_sums, sibling_sums, big_w, big_m, big_v):
        res = _adamw("adamw_" + n, [mine, theirs], *(a.reshape(mine.shape) for a in (w, m, v)))
        big_out[n] = [r.reshape(w.shape) for r in res]

    small_full = [jnp.concatenate([dg0, dg1]), jnp.concatenate([db0, db1]), dbi, dcw, dcb, dng, dnb, dbo,
                  dkvg, dqg]
    sflat = jnp.concatenate([g.reshape(-1) for g in small_full])
    summed = _all_reduce_small(_pack_rows(sflat, F32, 8 * LANES)).reshape(-1)
    soff = 0
    sgrads = []
    for g in small_full:
        sgrads.append(summed[soff:soff + g.size].reshape(g.shape))
        soff += g.size
    local_cols = lambda g, n: lax.dynamic_slice_in_dim(g, chip * n, n, axis=g.ndim - 1)
    snames = ["ln_g", "ln_b", "a_b_in", "a_conv_w", "a_conv_b", "a_norm_g", "a_norm_b", "a_b_out",
              "kv_norm_g", "b_q_norm_g"]
    small_w = [ln_g, ln_b, a_b_in, a_conv_w, a_conv_b, a_norm_g, a_norm_b, a_b_out, kv_norm_g, b_q_norm_g]
    small_m = [m_ln_g, m_ln_b, m_a_b_in, m_a_conv_w, m_a_conv_b, m_a_norm_g, m_a_norm_b, m_a_b_out,
               m_kv_norm_g, m_b_q_norm_g]
    small_v = [v_ln_g, v_ln_b, v_a_b_in, v_a_conv_w, v_a_conv_b, v_a_norm_g, v_a_norm_b, v_a_b_out,
               v_kv_norm_g, v_b_q_norm_g]
    sharded = {"a_b_in", "a_conv_w", "a_conv_b", "a_norm_g", "a_norm_b", "a_b_out"}
    local_g = [(local_cols(g, w.shape[-1]) if n in sharded else g).reshape(w.shape)
               for n, g, w in zip(snames, sgrads, small_w)]
    pack_small = lambda arrs: _pack_rows(jnp.concatenate([a.reshape(-1) for a in arrs]), F32, 8 * LANES)
    sres = _adamw("adamw_vectors", [pack_small(local_g)], pack_small(small_w), pack_small(small_m),
                  pack_small(small_v))
    small_out = {}
    soff = 0
    for n, w in zip(snames, small_w):
        small_out[n] = [r.reshape(-1)[soff:soff + w.size].reshape(w.shape) for r in sres]
        soff += w.size

    loss = lax.psum(loss_part[0, 0], ("x", "y", "c"))
    order = ["ln_g", "ln_b", "a_w_in", "a_b_in", "a_conv_w", "a_conv_b", "a_norm_g", "a_norm_b", "a_w_out",
             "a_b_out", "kv_w_down", "kv_norm_g", "kv_w_uk", "kv_w_uv", "b_w_in", "b_q_norm_g", "b_w_uq",
             "b_w_out"]
    outs = {**big_out, **small_out}
    result = [loss, grad_x[None]]
    for part in range(4):
        result += [outs[n][part] for n in order]
    return tuple(result)
```

```python
import functools
import math

import jax
import jax.numpy as jnp
from jax import lax
from jax.experimental import pallas as pl
from jax.experimental.pallas import tpu as pltpu

F32, BF16 = jnp.float32, jnp.bfloat16
NN = (((1,), (0,)), ((), ()))
NT = (((1,), (1,)), ((), ()))
TN = (((0,), (0,)), ((), ()))
MESH = pl.DeviceIdType.MESH
ANY = pl.BlockSpec(memory_space=pl.ANY)

LANES = 128
BF16_ROWS = 16
VMEM_LIMIT = 56 * 1024 * 1024
N_CHIPS = 4
LN_EPS = 1e-5
RMS_EPS = 1e-6
MASK_VALUE = -1e30
ROPE_THETA = 10000.0
ROPE_DIM = 64
ROPE_HALF = ROPE_DIM // 2
ADAM_LR, ADAM_B1, ADAM_B2, ADAM_EPS, ADAM_WD, ADAM_STEP = 0.001, 0.9, 0.999, 1e-08, 0.01, 10

MAX_TILE = 2048
TM_WIDE = 256
TQ = 512
CONV_HALO = 32
CONV_LC = 512
CONV_RC = 32
ADAM_ROWS = 64


def _dot(a, b, dims):
    return lax.dot_general(a.astype(BF16), b.astype(BF16), dims, preferred_element_type=F32)


def _sig(x):
    return 1.0 / (1.0 + jnp.exp(-x))


def _params(n_axes):
    return pltpu.CompilerParams(dimension_semantics=("arbitrary",) * n_axes, vmem_limit_bytes=VMEM_LIMIT)


def _gcd(*v):
    return functools.reduce(math.gcd, v)


def _fit(want, dim):
    return math.gcd(min(want, MAX_TILE), dim)


def _row_mm(name, a_segs, b, *, nt, tm, tn, tk, outs, epi, epi_ins=()):
    M = a_segs[0][0][0].shape[0]
    N = b.shape[0] if nt else b.shape[1]
    nkb = [arrs[0].shape[1] // tk for arrs, _ in a_segs]
    koff = [sum(nkb[:s]) for s in range(len(nkb))]
    ni, nj, nk = M // tm, N // tn, sum(nkb)
    assert M % tm == 0 and N % tn == 0 and all(arrs[0].shape[1] % tk == 0 for arrs, _ in a_segs), name
    assert (b.shape[1] if nt else b.shape[0]) == nk * tk, name

    def spec_of(shape, kind):
        if isinstance(kind, pl.BlockSpec):
            return kind
        if kind == 'tile':
            return pl.BlockSpec((tm, tn), lambda i, j, k: (i, j))
        if kind == 'row':
            return pl.BlockSpec((tm, shape[1]), lambda i, j, k: (i, 0))
        if kind == 'col':
            return pl.BlockSpec((1, tn), lambda i, j, k: (0, j))
        assert kind == 'acc' and nj == 1, name
        return pl.BlockSpec(shape, lambda i, j, k: (0,) * len(shape))

    in_specs, operands = [], []
    for s, (arrs, _) in enumerate(a_segs):
        for arr in arrs:
            in_specs.append(pl.BlockSpec(
                (tm, tk), lambda i, j, k, s=s: (i, jnp.clip(k - koff[s], 0, nkb[s] - 1))))
            operands.append(arr)
    in_specs.append(pl.BlockSpec((tn, tk), lambda i, j, k: (j, k)) if nt
                    else pl.BlockSpec((tk, tn), lambda i, j, k: (k, j)))
    operands.append(b)
    for arr, kind in epi_ins:
        in_specs.append(spec_of(arr.shape, kind))
        operands.append(arr)
    out_specs = [spec_of(shape, kind) for shape, _, kind in outs]
    out_shape = [jax.ShapeDtypeStruct(shape, dtype) for shape, dtype, _ in outs]
    n_seg_refs = [len(arrs) for arrs, _ in a_segs]

    def body(*refs):
        pos = 0
        seg_refs = []
        for n in n_seg_refs:
            seg_refs.append(refs[pos:pos + n])
            pos += n
        b_ref = refs[pos]
        e_refs = refs[pos + 1:pos + 1 + len(epi_ins)]
        o_refs = refs[pos + 1 + len(epi_ins):pos + 1 + len(epi_ins) + len(outs)]
        i, j, k = pl.program_id(0), pl.program_id(1), pl.program_id(2)

        def product(fn, rs):
            a = rs[0][...] if fn is None else fn(*[r[...] for r in rs])
            return _dot(a, b_ref[...], NT if nt else NN)

        def finish(acc):
            res = epi(acc, [r[...] for r in e_refs], i, j)
            for o_ref, (_, _, kind), r in zip(o_refs, outs, res):
                if isinstance(kind, str) and kind == 'acc':
                    @pl.when(i == 0)
                    def _(o_ref=o_ref, r=r):
                        o_ref[...] = r

                    @pl.when(i > 0)
                    def _(o_ref=o_ref, r=r):
                        o_ref[...] += r
                else:
                    o_ref[...] = r.astype(o_ref.dtype)

        if nk == 1:
            finish(product(a_segs[0][1], seg_refs[0]))
            return
        acc_ref = refs[-1]

        @pl.when(k == 0)
        def _():
            acc_ref[...] = jnp.zeros_like(acc_ref)

        for s, ((_, fn), rs) in enumerate(zip(a_segs, seg_refs)):
            def accumulate(fn=fn, rs=rs):
                acc_ref[...] += product(fn, rs)
            if len(a_segs) == 1:
                accumulate()
            else:
                pl.when(jnp.logical_and(k >= koff[s], k < koff[s] + nkb[s]))(accumulate)

        @pl.when(k == nk - 1)
        def _():
            finish(acc_ref[...])

    return pl.pallas_call(
        body, name=name, grid=(ni, nj, nk), in_specs=in_specs, out_specs=out_specs, out_shape=out_shape,
        scratch_shapes=[] if nk == 1 else [pltpu.VMEM((tm, tn), F32)], compiler_params=_params(3),
    )(*operands)


def _tn_mm(name, a_arrs, a_fn, b_segs, *, tn, tk, out_dtype, shard_major=False, colsum=False):
    T, M = a_arrs[0].shape
    nbj = [arrs[0].shape[1] // tn for arrs, _ in b_segs]
    joff = [sum(nbj[:s]) for s in range(len(nbj))]
    nj, nk = sum(nbj), T // tk
    N = nj * tn
    assert T % tk == 0 and all(arrs[0].shape[1] % tn == 0 for arrs, _ in b_segs), name

    in_specs = [pl.BlockSpec((tk, M), lambda j, k: (k, 0)) for _ in a_arrs]
    operands = list(a_arrs)
    for s, (arrs, _) in enumerate(b_segs):
        for arr in arrs:
            in_specs.append(pl.BlockSpec(
                (tk, tn), lambda j, k, s=s: (k, jnp.clip(j - joff[s], 0, nbj[s] - 1))))
            operands.append(arr)
    if shard_major:
        per = (N // N_CHIPS) // tn
        assert per * tn * N_CHIPS == N, name
        out_shape = [jax.ShapeDtypeStruct((N_CHIPS, M, N // N_CHIPS), out_dtype)]
        out_specs = [pl.BlockSpec((1, M, tn), lambda j, k: (j // per, 0, j % per))]
    else:
        out_shape = [jax.ShapeDtypeStruct((M, N), out_dtype)]
        out_specs = [pl.BlockSpec((M, tn), lambda j, k: (0, j))]
    if colsum:
        out_shape.append(jax.ShapeDtypeStruct((1, N), F32))
        out_specs.append(pl.BlockSpec((1, tn), lambda j, k: (0, j)))
    n_a = len(a_arrs)
    n_seg_refs = [len(arrs) for arrs, _ in b_segs]

    def body(*refs):
        a_refs = refs[:n_a]
        pos = n_a
        seg_refs = []
        for n in n_seg_refs:
            seg_refs.append(refs[pos:pos + n])
            pos += n
        o_ref = refs[pos]
        cs_ref = refs[pos + 1] if colsum else None
        acc_ref = refs[-1]
        j, k = pl.program_id(0), pl.program_id(1)

        @pl.when(k == 0)
        def _():
            acc_ref[...] = jnp.zeros_like(acc_ref)
            if colsum:
                cs_ref[...] = jnp.zeros_like(cs_ref)

        for s, ((_, fn), rs) in enumerate(zip(b_segs, seg_refs)):
            def accumulate(fn=fn, rs=rs):
                a = a_refs[0][...] if a_fn is None else a_fn(*[r[...] for r in a_refs])
                bt = rs[0][...] if fn is None else fn(*[r[...] for r in rs])
                acc_ref[...] += _dot(a, bt, TN)
                if colsum:
                    cs_ref[...] += jnp.sum(bt.astype(F32), axis=0, keepdims=True)
            if len(b_segs) == 1:
                accumulate()
            else:
                pl.when(jnp.logical_and(j >= joff[s], j < joff[s] + nbj[s]))(accumulate)

        @pl.when(k == nk - 1)
        def _():
            if shard_major:
                o_ref[0] = acc_ref[...].astype(o_ref.dtype)
            else:
                o_ref[...] = acc_ref[...].astype(o_ref.dtype)

    res = pl.pallas_call(
        body, name=name, grid=(nj, nk), in_specs=in_specs, out_specs=out_specs, out_shape=out_shape,
        scratch_shapes=[pltpu.VMEM((M, tn), F32)], compiler_params=_params(2),
    )(*operands)
    return res if colsum else res[0]


def _silu(z):
    return z * _sig(z)


def _silu_grad(z):
    s = _sig(z)
    return s * (1.0 + z * (1.0 - s))


def _gate(o, z):
    return o * _silu(z)


def _ln_stats(r):
    mu = jnp.mean(r, axis=1, keepdims=True)
    xc = r - mu
    var = jnp.mean(xc * xc, axis=1, keepdims=True)
    rstd = lax.rsqrt(var + LN_EPS)
    return xc * rstd, rstd


def _ln_bwd(dy, xhat, rstd, g):
    dxh = dy * g
    m1 = jnp.mean(dxh, axis=1, keepdims=True)
    m2 = jnp.mean(dxh * xhat, axis=1, keepdims=True)
    return (rstd * (dxh - m1 - xhat * m2), jnp.sum(dy * xhat, axis=0, keepdims=True),
            jnp.sum(dy, axis=0, keepdims=True))


def _rms_fwd(x, g):
    rstd = lax.rsqrt(jnp.mean(x * x, axis=1, keepdims=True) + RMS_EPS)
    return x * rstd * g


def _rms_bwd(dy, x, g):
    rstd = lax.rsqrt(jnp.mean(x * x, axis=1, keepdims=True) + RMS_EPS)
    xn = x * rstd
    dxn = dy * g
    return rstd * (dxn - xn * jnp.mean(dxn * xn, axis=1, keepdims=True)), jnp.sum(dy * xn, axis=0, keepdims=True)


def _rope(x, cos, sin, transpose=False):
    parts = []
    for g in range(x.shape[1] // LANES):
        xg = x[:, g * LANES:(g + 1) * LANES]
        if transpose:
            parts.append(xg * cos + pltpu.roll(xg * sin, LANES // 2, 1))
        else:
            parts.append(xg * cos + pltpu.roll(xg, LANES // 2, 1) * sin)
    return parts[0] if len(parts) == 1 else jnp.concatenate(parts, axis=1)


def _conv_fwd(proj, conv_w, conv_b, E, tm):
    T = proj.shape[0]
    kc = conv_w.shape[0]
    lc, hb, rc = min(CONV_LC, E), CONV_HALO, min(CONV_RC, tm)
    nl, ni, ratio = E // lc, T // tm, tm // hb
    gate_off = E // lc

    def body(val_ref, gate_ref, valh_ref, gateh_ref, w_ref, cb_ref, u1_ref, ubuf):
        i = pl.program_id(1)
        ubuf[hb:, :] = val_ref[...] * _sig(gate_ref[...])
        halo = valh_ref[...] * _sig(gateh_ref[...])
        ubuf[0:hb, :] = jnp.where(i > 0, halo, 0.0)
        for r0 in range(0, tm, rc):
            acc = jnp.zeros((rc, lc), F32) + cb_ref[...]
            for k in range(kc):
                acc += w_ref[k:k + 1, :] * ubuf[r0 + hb - (kc - 1) + k:r0 + hb - (kc - 1) + k + rc, :]
            u1_ref[r0:r0 + rc, :] = acc

    return pl.pallas_call(
        body, name="conv_fwd", grid=(nl, ni),
        in_specs=[
            pl.BlockSpec((tm, lc), lambda l, i: (i, l)),
            pl.BlockSpec((tm, lc), lambda l, i: (i, gate_off + l)),
            pl.BlockSpec((hb, lc), lambda l, i: (jnp.maximum(i * ratio - 1, 0), l)),
            pl.BlockSpec((hb, lc), lambda l, i: (jnp.maximum(i * ratio - 1, 0), gate_off + l)),
            pl.BlockSpec((kc, lc), lambda l, i: (0, l)),
            pl.BlockSpec((1, lc), lambda l, i: (0, l)),
        ],
        out_specs=pl.BlockSpec((tm, lc), lambda l, i: (i, l)),
        out_shape=jax.ShapeDtypeStruct((T, E), F32),
        scratch_shapes=[pltpu.VMEM((hb + tm, lc), F32)], compiler_params=_params(2),
    )(proj, proj, proj, proj, conv_w, conv_b)


def _conv_post(u1, proj, norm_g, norm_b, E, tm):
    T = u1.shape[0]

    def body(u1_ref, z_ref, g_ref, b_ref, u4_ref):
        xhat, _ = _ln_stats(u1_ref[...])
        u4_ref[...] = (_silu(xhat * g_ref[...] + b_ref[...]) * _silu(z_ref[...])).astype(BF16)

    return pl.pallas_call(
        body, name="conv_post", grid=(T // tm,),
        in_specs=[pl.BlockSpec((tm, E), lambda i: (i, 0)), pl.BlockSpec((tm, E), lambda i: (i, 2)),
                  pl.BlockSpec((1, E), lambda i: (0, 0)), pl.BlockSpec((1, E), lambda i: (0, 0))],
        out_specs=pl.BlockSpec((tm, E), lambda i: (i, 0)),
        out_shape=jax.ShapeDtypeStruct((T, E), BF16), compiler_params=_params(1),
    )(u1, proj, norm_g, norm_b)


def _conv_bwd(du1, proj, conv_w, E, tm):
    T = du1.shape[0]
    kc = conv_w.shape[0]
    lc, hb, rc = min(CONV_LC, E), CONV_HALO, min(CONV_RC, tm)
    nl, ni, ratio = E // lc, T // tm, tm // hb
    gate_off = E // lc
    last_halo = T // hb - 1

    def body(du_ref, dun_ref, val_ref, gate_ref, valh_ref, gateh_ref, w_ref,
             dval_ref, dgate_ref, dw_ref, db_ref, ubuf, dbuf):
        i = pl.program_id(1)
        val, sg = val_ref[...], _sig(gate_ref[...])
        ubuf[hb:, :] = val * sg
        halo = valh_ref[...] * _sig(gateh_ref[...])
        ubuf[0:hb, :] = jnp.where(i > 0, halo, 0.0)
        dbuf[0:tm, :] = du_ref[...]
        dbuf[tm:, :] = jnp.where(i < ni - 1, dun_ref[...], 0.0)

        @pl.when(i == 0)
        def _():
            dw_ref[...] = jnp.zeros_like(dw_ref)
            db_ref[...] = jnp.zeros_like(db_ref)

        db_ref[...] += jnp.sum(du_ref[...], axis=0, keepdims=True)
        for k in range(kc):
            acc = jnp.zeros((rc, lc), F32)
            for r0 in range(0, tm, rc):
                acc += ubuf[r0 + hb - (kc - 1) + k:r0 + hb - (kc - 1) + k + rc, :] * dbuf[r0:r0 + rc, :]
            dw_ref[k:k + 1, :] += jnp.sum(acc, axis=0, keepdims=True)
        for r0 in range(0, tm, rc):
            acc = jnp.zeros((rc, lc), F32)
            for k in range(kc):
                acc += w_ref[k:k + 1, :] * dbuf[r0 + (kc - 1) - k:r0 + (kc - 1) - k + rc, :]
            v, s = val[r0:r0 + rc, :], sg[r0:r0 + rc, :]
            dval_ref[r0:r0 + rc, :] = (acc * s).astype(BF16)
            dgate_ref[r0:r0 + rc, :] = (acc * v * s * (1.0 - s)).astype(BF16)

    return pl.pallas_call(
        body, name="conv_bwd", grid=(nl, ni),
        in_specs=[
            pl.BlockSpec((tm, lc), lambda l, i: (i, l)),
            pl.BlockSpec((hb, lc), lambda l, i: (jnp.minimum((i + 1) * ratio, last_halo), l)),
            pl.BlockSpec((tm, lc), lambda l, i: (i, l)),
            pl.BlockSpec((tm, lc), lambda l, i: (i, gate_off + l)),
            pl.BlockSpec((hb, lc), lambda l, i: (jnp.maximum(i * ratio - 1, 0), l)),
            pl.BlockSpec((hb, lc), lambda l, i: (jnp.maximum(i * ratio - 1, 0), gate_off + l)),
            pl.BlockSpec((kc, lc), lambda l, i: (0, l)),
        ],
        out_specs=[pl.BlockSpec((tm, lc), lambda l, i: (i, l)), pl.BlockSpec((tm, lc), lambda l, i: (i, l)),
                   pl.BlockSpec((kc, lc), lambda l, i: (0, l)), pl.BlockSpec((1, lc), lambda l, i: (0, l))],
        out_shape=[jax.ShapeDtypeStruct((T, E), BF16), jax.ShapeDtypeStruct((T, E), BF16),
                   jax.ShapeDtypeStruct((kc, E), F32), jax.ShapeDtypeStruct((1, E), F32)],
        scratch_shapes=[pltpu.VMEM((hb + tm, lc), F32), pltpu.VMEM((tm + hb, lc), F32)],
        compiler_params=_params(2),
    )(du1, du1, proj, proj, proj, proj, conv_w)


def _norm_prep(pb, kv_g, q_g, cos, sin, rkv, rq, wk, tm):
    T = pb.shape[0]

    def body(ckv_ref, cq_ref, kg_ref, qg_ref, cos_ref, sin_ref, c_ref, kr_ref, cqn_ref):
        blk = ckv_ref[...]
        c_ref[...] = _rms_fwd(blk[:, :rkv], kg_ref[...]).astype(BF16)
        kr_ref[...] = _rope(blk[:, rkv:rkv + LANES], cos_ref[...], sin_ref[...]).astype(BF16)
        cqn_ref[...] = _rms_fwd(cq_ref[...], qg_ref[...]).astype(BF16)

    return pl.pallas_call(
        body, name="norm_prep", grid=(T // tm,),
        in_specs=[pl.BlockSpec((tm, wk), lambda i: (i, 0)), pl.BlockSpec((tm, rq), lambda i: (i, wk // rq)),
                  pl.BlockSpec((1, rkv), lambda i: (0, 0)), pl.BlockSpec((1, rq), lambda i: (0, 0)),
                  pl.BlockSpec((tm, LANES), lambda i: (i, 0)), pl.BlockSpec((tm, LANES), lambda i: (i, 0))],
        out_specs=[pl.BlockSpec((tm, rkv), lambda i: (i, 0)), pl.BlockSpec((tm, LANES), lambda i: (i, 0)),
                   pl.BlockSpec((tm, rq), lambda i: (i, 0))],
        out_shape=[jax.ShapeDtypeStruct((T, rkv), BF16), jax.ShapeDtypeStruct((T, LANES), BF16),
                   jax.ShapeDtypeStruct((T, rq), BF16)],
        compiler_params=_params(1),
    )(pb, pb, kv_g, q_g, cos, sin)


def _attn_fwd(q_all, kv, kr, H, tq, scale):
    T = q_all.shape[0]
    nq = T // tq
    pair = 2
    W = pair * LANES
    assert H % pair == 0
    hp_n = H // pair

    def body(qn_ref, qr_ref, kn_ref, kr_ref, v_ref, o_ref, lse_ref, *scratch):
        qi = pl.program_id(1)
        chains = [scratch[4 * a:4 * a + 4] for a in range(pair)]
        lanes = [slice(a * LANES, (a + 1) * LANES) for a in range(pair)]
        groups = [slice(c * LANES, (c + 1) * LANES) for c in range(tq // LANES)]

        def fold(x, op):
            r = x[:, groups[0]]
            for gsl in groups[1:]:
                r = op(r, x[:, gsl])
            return r

        for _, m_sc, l_sc, acc_sc in chains:
            m_sc[...] = jnp.full_like(m_sc, MASK_VALUE)
            l_sc[...] = jnp.zeros_like(l_sc)
            acc_sc[...] = jnp.zeros_like(acc_sc)

        def scores(j, masked):
            rows = pl.ds(pl.multiple_of(j * tq, tq), tq)
            krope = kr_ref[rows, :]
            for a, (s_sc, m_sc, _, _) in enumerate(chains):
                q = jnp.concatenate([qn_ref[:, lanes[a]], qr_ref[:, lanes[a]]], axis=1)
                k = jnp.concatenate([kn_ref[rows, lanes[a]], krope], axis=1)
                s = _dot(q, k, NT) * scale
                if masked:
                    row = lax.broadcasted_iota(jnp.int32, s.shape, 0)
                    col = lax.broadcasted_iota(jnp.int32, s.shape, 1)
                    s = jnp.where(col <= row, s, MASK_VALUE)
                s_sc[j] = s
                m_sc[...] = jnp.maximum(m_sc[...], fold(s, jnp.maximum))

        def unmasked(j, carry):
            scores(j, False)
            return carry

        lax.fori_loop(0, qi, unmasked, 0)
        scores(qi, True)
        for _, m_sc, _, _ in chains:
            m_sc[...] = jnp.broadcast_to(jnp.max(m_sc[...], axis=1, keepdims=True), m_sc.shape)

        def weigh(j, carry):
            rows = pl.ds(pl.multiple_of(j * tq, tq), tq)
            for a, (s_sc, m_sc, l_sc, acc_sc) in enumerate(chains):
                s, m = s_sc[j], m_sc[...]
                p = jnp.concatenate([jnp.exp(s[:, gsl] - m) for gsl in groups], axis=1)
                l_sc[...] += fold(p, jnp.add)
                acc_sc[...] += _dot(p, v_ref[rows, lanes[a]], NN)
            return carry

        lax.fori_loop(0, qi + 1, weigh, 0)
        for a, (_, m_sc, l_sc, acc_sc) in enumerate(chains):
            l = jnp.sum(l_sc[...], axis=1, keepdims=True)
            o_ref[:, lanes[a]] = acc_sc[...] / l
            lse_ref[a] = m_sc[:, 0:1] + jnp.log(l)

    chain_scratch = [pltpu.VMEM((nq, tq, tq), F32), pltpu.VMEM((tq, LANES), F32), pltpu.VMEM((tq, LANES), F32),
                     pltpu.VMEM((tq, LANES), F32)]
    return pl.pallas_call(
        body, name="attn_fwd", grid=(hp_n, nq),
        in_specs=[pl.BlockSpec((tq, W), lambda hp, qi: (qi, hp)),
                  pl.BlockSpec((tq, W), lambda hp, qi: (qi, hp_n + hp)),
                  pl.BlockSpec((T, W), lambda hp, qi: (0, hp)),
                  pl.BlockSpec((T, LANES), lambda hp, qi: (0, 0)),
                  pl.BlockSpec((T, W), lambda hp, qi: (0, hp_n + hp))],
        out_specs=[pl.BlockSpec((tq, W), lambda hp, qi: (qi, hp)),
                   pl.BlockSpec((pair, tq, 1), lambda hp, qi: (hp, qi, 0))],
        out_shape=[jax.ShapeDtypeStruct((T, H * LANES), F32), jax.ShapeDtypeStruct((H, T, 1), F32)],
        scratch_shapes=chain_scratch * pair, compiler_params=_params(2),
    )(q_all, q_all, kv, kr, kv)


def _attn_bwd(q_all, kv, kr, do, o, lse, H, tq, scale):
    T = q_all.shape[0]
    nq = T // tq
    HV = H * LANES
    pair = 2
    tk2 = pair * tq
    ng = T // tk2
    assert ng * tk2 == T

    def body(qn_ref, qr_ref, kn_ref, kr_ref, v_ref, do_ref, o_ref, lse_ref,
             dqn_ref, dqr_ref, dkn_ref, dkr_ref, dv_ref, dq_sc, dk_sc, dv_sc):
        g = pl.program_id(1)

        @pl.when(g == 0)
        def _():
            dq_sc[...] = jnp.zeros_like(dq_sc)

        dk_sc[...] = jnp.zeros_like(dk_sc)
        dv_sc[...] = jnp.zeros_like(dv_sc)

        def block(qi, modes):
            rows = pl.ds(pl.multiple_of(qi * tq, tq), tq)
            q = jnp.concatenate([qn_ref[rows, :], qr_ref[rows, :]], axis=1)
            dov = do_ref[rows, :]
            delta = jnp.sum(dov.astype(F32) * o_ref[rows, :], axis=1, keepdims=True)
            lse_q = lse_ref[0, rows, :]
            dq = None
            for c, masked in enumerate(modes):
                if masked is None:
                    continue
                kr_ = slice(c * tq, (c + 1) * tq)
                k = jnp.concatenate([kn_ref[kr_, :], kr_ref[kr_, :]], axis=1)
                s = _dot(q, k, NT) * scale
                if masked:
                    row = lax.broadcasted_iota(jnp.int32, s.shape, 0)
                    col = lax.broadcasted_iota(jnp.int32, s.shape, 1)
                    s = jnp.where(col <= row, s, MASK_VALUE)
                p = jnp.exp(s - lse_q)
                dv_sc[kr_, :] += _dot(p, dov, TN)
                dp = _dot(dov, v_ref[kr_, :], NT)
                ds = (p * (dp - delta) * scale).astype(BF16)
                dk_sc[kr_, :] += _dot(ds, q, TN)
                part = _dot(ds, k, NN)
                dq = part if dq is None else dq + part
            dq_sc[rows, :] += dq

        block(pair * g, (True, None))
        block(pair * g + 1, (False, True))

        def below(qi, carry):
            block(qi, (False, False))
            return carry

        lax.fori_loop(pair * g + pair, nq, below, 0)
        dkn_ref[...] = dk_sc[:, :LANES].astype(BF16)
        dkr_ref[...] = dk_sc[:, LANES:]
        dv_ref[...] = dv_sc[...].astype(BF16)

        @pl.when(g == ng - 1)
        def _():
            dqn_ref[...] = dq_sc[:, :LANES].astype(BF16)
            dqr_ref[...] = dq_sc[:, LANES:]

    whole = lambda col: pl.BlockSpec((T, LANES), col)
    tile = lambda col: pl.BlockSpec((tk2, LANES), col)
    return pl.pallas_call(
        body, name="attn_bwd", grid=(H, ng),
        in_specs=[whole(lambda h, g: (0, h)), whole(lambda h, g: (0, H + h)),
                  tile(lambda h, g: (g, h)), tile(lambda h, g: (g, 0)), tile(lambda h, g: (g, H + h)),
                  whole(lambda h, g: (0, h)), whole(lambda h, g: (0, h)),
                  pl.BlockSpec((1, T, 1), lambda h, g: (h, 0, 0))],
        out_specs=[whole(lambda h, g: (0, h)), whole(lambda h, g: (0, h)),
                   tile(lambda h, g: (g, h)), tile(lambda h, g: (g, h)), tile(lambda h, g: (g, h))],
        out_shape=[jax.ShapeDtypeStruct((T, HV), BF16), jax.ShapeDtypeStruct((T, HV), F32),
                   jax.ShapeDtypeStruct((T, HV), BF16), jax.ShapeDtypeStruct((T, HV), F32),
                   jax.ShapeDtypeStruct((T, HV), BF16)],
        scratch_shapes=[pltpu.VMEM((T, 2 * LANES), F32), pltpu.VMEM((tk2, 2 * LANES), F32),
                        pltpu.VMEM((tk2, LANES), F32)],
        compiler_params=_params(2),
    )(q_all, q_all, kv, kr, kv, do, o, lse)


def _rope_bwd(dqr, dkr_heads, cos, sin, H, tm):
    T, HV = dqr.shape

    def body(dqr_ref, dkr_ref, cos_ref, sin_ref, dq_ref, dk_ref):
        c, s = cos_ref[...], sin_ref[...]
        dq_ref[...] = _rope(dqr_ref[...], c, s, transpose=True).astype(BF16)
        dk = dkr_ref[...]
        tot = dk[:, 0:LANES]
        for h in range(1, H):
            tot = tot + dk[:, h * LANES:(h + 1) * LANES]
        dk_ref[...] = _rope(tot, c, s, transpose=True)

    return pl.pallas_call(
        body, name="rope_bwd", grid=(T // tm,),
        in_specs=[pl.BlockSpec((tm, HV), lambda i: (i, 0)), pl.BlockSpec((tm, HV), lambda i: (i, 0)),
                  pl.BlockSpec((tm, LANES), lambda i: (i, 0)), pl.BlockSpec((tm, LANES), lambda i: (i, 0))],
        out_specs=[pl.BlockSpec((tm, HV), lambda i: (i, 0)), pl.BlockSpec((tm, LANES), lambda i: (i, 0))],
        out_shape=[jax.ShapeDtypeStruct((T, HV), BF16), jax.ShapeDtypeStruct((T, LANES), F32)],
        compiler_params=_params(1),
    )(dqr, dkr_heads, cos, sin)


def _place():
    x, y, c = lax.axis_index("x"), lax.axis_index("y"), lax.axis_index("c")
    chips = [(1 - x, y), (x, 1 - y), (1 - x, 1 - y)]
    return x, y, c, chips


def _all_gather_chips(pack):
    rows = pack.shape[0]
    half = rows // 2
    assert half * 2 == rows and half % BF16_ROWS == 0

    def body(w_ref, out_ref, send_sems, recv_sems, local_sem):
        x, y, c, chips = _place()
        sibling = (x, y, 1 - c)

        def region(px, py, pc):
            return out_ref.at[2 * px + py, pl.ds(pc * half, half), :]

        def copy(k, block, to, src=None):
            return pltpu.make_async_remote_copy(
                src_ref=region(*block) if src is None else src, dst_ref=region(*block),
                send_sem=send_sems.at[k], recv_sem=recv_sems.at[k], device_id=to, device_id_type=MESH)

        mine = pltpu.make_async_copy(w_ref, out_ref.at[2 * x + y], local_sem)
        mine.start()
        my_half = w_ref.at[pl.ds(c * half, half), :]
        first = [copy(j, (x, y, c), (*chip, c), src=my_half) for j, chip in enumerate(chips)]
        for cp in first:
            cp.start()
        passed = [copy(3 + j, (*chip, c), sibling) for j, chip in enumerate(chips)]
        for j, chip in enumerate(chips):
            copy(j, (*chip, c), (x, y, c)).wait_recv()
            passed[j].start()
        for j, chip in enumerate(chips):
            copy(3 + j, (*chip, 1 - c), (x, y, c)).wait_recv()
        for cp in first + passed:
            cp.wait_send()
        mine.wait()

    return pl.pallas_call(
        body, name="gather_weights", in_specs=[ANY], out_specs=ANY,
        out_shape=jax.ShapeDtypeStruct((N_CHIPS,) + pack.shape, pack.dtype),
        scratch_shapes=[pltpu.SemaphoreType.DMA((6,)), pltpu.SemaphoreType.DMA((6,)), pltpu.SemaphoreType.DMA],
    )(pack)


def _scatter_grads(grads):
    n = len(grads)

    def body(*refs):
        g_refs, r_refs = refs[:n], refs[n:2 * n]
        send_sems, recv_sems, local_sems = refs[2 * n:]
        x, y, c, chips = _place()
        me = 2 * x + y
        locals_, sends = [], []
        for w in range(n):
            cp = pltpu.make_async_copy(g_refs[w].at[me], r_refs[w].at[me], local_sems.at[w])
            cp.start()
            locals_.append(cp)
            for j, (px, py) in enumerate(chips):
                cp = pltpu.make_async_remote_copy(
                    src_ref=g_refs[w].at[2 * px + py], dst_ref=r_refs[w].at[me],
                    send_sem=send_sems.at[3 * w + j], recv_sem=recv_sems.at[3 * w + j],
                    device_id=(px, py, c), device_id_type=MESH)
                cp.start()
                sends.append(cp)
        for w in range(n):
            for j, (px, py) in enumerate(chips):
                pltpu.make_async_remote_copy(
                    src_ref=g_refs[w].at[me], dst_ref=r_refs[w].at[2 * px + py],
                    send_sem=send_sems.at[3 * w + j], recv_sem=recv_sems.at[3 * w + j],
                    device_id=(px, py, c), device_id_type=MESH).wait_recv()
        for cp in sends:
            cp.wait_send()
        for cp in locals_:
            cp.wait()

    return pl.pallas_call(
        body, name="scatter_grads", in_specs=[ANY] * n, out_specs=[ANY] * n,
        out_shape=[jax.ShapeDtypeStruct(g.shape, g.dtype) for g in grads],
        scratch_shapes=[pltpu.SemaphoreType.DMA((3 * n,)), pltpu.SemaphoreType.DMA((3 * n,)),
                        pltpu.SemaphoreType.DMA((n,))],
    )(*grads)


def _swap_cores(parts):
    n = len(parts)

    def body(*refs):
        p_refs, r_refs = refs[:n], refs[n:2 * n]
        send_sems, recv_sems = refs[2 * n:]
        x, y, c, _ = _place()
        copies = [pltpu.make_async_remote_copy(
            src_ref=p_refs[w], dst_ref=r_refs[w], send_sem=send_sems.at[w], recv_sem=recv_sems.at[w],
            device_id=(x, y, 1 - c), device_id_type=MESH) for w in range(n)]
        for cp in copies:
            cp.start()
        for cp in copies:
            cp.wait()

    return pl.pallas_call(
        body, name="swap_cores", in_specs=[ANY] * n, out_specs=[ANY] * n,
        out_shape=[jax.ShapeDtypeStruct(p.shape, p.dtype) for p in parts],
        scratch_shapes=[pltpu.SemaphoreType.DMA((n,)), pltpu.SemaphoreType.DMA((n,))],
    )(*parts)


def _all_reduce_small(part):
    n_dev = 8

    def body(p_ref, out_ref, buf, send_sems, recv_sems):
        x, y, c, _ = _place()
        me = 4 * x + 2 * y + c
        buf[me] = p_ref[...]
        copies = []
        for k in range(1, n_dev):
            kx, ky, kc = (k >> 2) & 1, (k >> 1) & 1, k & 1
            peer = (x ^ kx, y ^ ky, c ^ kc)
            cp = pltpu.make_async_remote_copy(
                src_ref=p_ref, dst_ref=buf.at[me], send_sem=send_sems.at[k - 1], recv_sem=recv_sems.at[k - 1],
                device_id=peer, device_id_type=MESH)
            cp.start()
            copies.append(cp)
        for k in range(1, n_dev):
            pltpu.make_async_remote_copy(
                src_ref=p_ref, dst_ref=buf.at[me ^ k], send_sem=send_sems.at[k - 1], recv_sem=recv_sems.at[k - 1],
                device_id=(x, y, c), device_id_type=MESH).wait_recv()
        for cp in copies:
            cp.wait_send()
        tot = buf[0]
        for d in range(1, n_dev):
            tot = tot + buf[d]
        out_ref[...] = tot

    vm = pl.BlockSpec(memory_space=pltpu.VMEM)
    return pl.pallas_call(
        body, name="all_reduce_small", in_specs=[vm], out_specs=vm,
        out_shape=jax.ShapeDtypeStruct(part.shape, F32),
        scratch_shapes=[pltpu.VMEM((n_dev,) + part.shape, F32), pltpu.SemaphoreType.DMA((n_dev - 1,)),
                        pltpu.SemaphoreType.DMA((n_dev - 1,))],
    )(part)


def _sum_chips(name, r):
    _, rows, cols = r.shape
    tr = _gcd(ADAM_ROWS, rows)

    def body(r_ref, o_ref):
        tot = r_ref[0].astype(F32)
        for q in range(1, N_CHIPS):
            tot = tot + r_ref[q].astype(F32)
        o_ref[...] = tot

    return pl.pallas_call(
        body, name=name, grid=(rows // tr,),
        in_specs=[pl.BlockSpec((N_CHIPS, tr, cols), lambda i: (0, i, 0))],
        out_specs=pl.BlockSpec((tr, cols), lambda i: (i, 0)),
        out_shape=jax.ShapeDtypeStruct((rows, cols), F32), compiler_params=_params(1),
    )(r)


def _adamw(name, g_parts, w, m, v):
    rows, cols = w.shape
    tr = _gcd(ADAM_ROWS, rows)
    n = len(g_parts)

    def body(*refs):
        g = refs[0][...]
        for r in refs[1:n]:
            g = g + r[...]
        w_ref, m_ref, v_ref, go_ref, d_ref, mo_ref, vo_ref = refs[n:]
        mn = ADAM_B1 * m_ref[...] + (1.0 - ADAM_B1) * g
        vn = ADAM_B2 * v_ref[...] + (1.0 - ADAM_B2) * jnp.square(g)
        m_hat = mn / (1.0 - ADAM_B1 ** ADAM_STEP)
        v_hat = vn / (1.0 - ADAM_B2 ** ADAM_STEP)
        go_ref[...] = g
        d_ref[...] = -ADAM_LR * (m_hat / (jnp.sqrt(v_hat) + ADAM_EPS) + ADAM_WD * w_ref[...])
        mo_ref[...] = mn
        vo_ref[...] = vn

    spec = pl.BlockSpec((tr, cols), lambda i: (i, 0))
    return pl.pallas_call(
        body, name=name, grid=(rows // tr,), in_specs=[spec] * (n + 3), out_specs=[spec] * 4,
        out_shape=[jax.ShapeDtypeStruct((rows, cols), F32)] * 4, compiler_params=_params(1),
    )(*g_parts, w, m, v)


def _pack_rows(flat, dtype, multiple):
    n = flat.shape[0]
    total = -(-n // multiple) * multiple
    return jnp.pad(flat, (0, total - n)).astype(dtype).reshape(total // LANES, LANES)


def kernel(x, positions, ln_g, ln_b, a_w_in, a_b_in, a_conv_w, a_conv_b, a_norm_g, a_norm_b, a_w_out, a_b_out, kv_w_down, kv_norm_g, kv_w_uk, kv_w_uv, b_w_in, b_q_norm_g, b_w_uq, b_w_out, loss_target, m_ln_g, m_ln_b, m_a_w_in, m_a_b_in, m_a_conv_w, m_a_conv_b, m_a_norm_g, m_a_norm_b, m_a_w_out, m_a_b_out, m_kv_w_down, m_kv_norm_g, m_kv_w_uk, m_kv_w_uv, m_b_w_in, m_b_q_norm_g, m_b_w_uq, m_b_w_out, v_ln_g, v_ln_b, v_a_w_in, v_a_b_in, v_a_conv_w, v_a_conv_b, v_a_norm_g, v_a_norm_b, v_a_w_out, v_a_b_out, v_kv_w_down, v_kv_norm_g, v_kv_w_uk, v_kv_w_uv, v_b_w_in, v_b_q_norm_g, v_b_w_uq, v_b_w_out):
    T, D = x.shape[1], x.shape[2]
    E = N_CHIPS * a_w_out.shape[1]
    KC = a_conv_w.shape[1]
    RKV = kv_norm_g.shape[0]
    H, DN = kv_w_uk.shape[1], kv_w_uk.shape[2]
    RQ = b_q_norm_g.shape[1]
    HV = N_CHIPS * b_w_out.shape[1]
    assert DN == LANES and kv_w_uv.shape[2] == LANES and HV == H * LANES
    assert kv_w_down.shape[1] == RKV + ROPE_DIM and b_w_uq.shape[3] == DN + ROPE_DIM
    assert ln_g.shape[0] == 2 and a_w_in.shape[0] == 1 and b_w_in.shape[0] == 1
    alpha = (2.0 * ln_g.shape[0]) ** 0.25
    scale = 1.0 / math.sqrt(DN + ROPE_DIM)
    WK = -(-(RKV + LANES) // 256) * 256
    assert WK % RQ == 0
    Z_OFF = WK + RQ
    tmw, tq = min(TM_WIDE, T), min(TQ, T)
    t512, t1024 = _fit(512, T), _fit(1024, T)
    xs = x[0]
    xb = xs.astype(BF16)
    tgt = loss_target[0]
    px, py = lax.axis_index("x"), lax.axis_index("y")
    chip = 2 * px + py

    mats = [a_w_in[0], a_w_out[0], kv_w_down, kv_w_uk, kv_w_uv, b_w_in[0], b_w_uq[0], b_w_out[0]]
    vecs = [a_b_in[0], a_conv_w[0], a_conv_b[0], a_norm_g[0], a_norm_b[0], a_b_out[0]]
    flat = jnp.concatenate(
        [w.astype(BF16).reshape(-1) for w in mats]
        + [lax.bitcast_convert_type(w.reshape(-1), BF16).reshape(-1) for w in vecs])
    pack = _pack_rows(flat, BF16, 2 * BF16_ROWS * LANES)
    gathered = _all_gather_chips(pack).reshape(N_CHIPS, -1)
    off = 0
    full = []
    for w in mats:
        full.append(gathered[:, off:off + w.size].reshape((N_CHIPS,) + w.shape))
        off += w.size
    fvec = []
    for w in vecs:
        bits = gathered[:, off:off + 2 * w.size].reshape((N_CHIPS,) + w.shape + (2,))
        fvec.append(lax.bitcast_convert_type(bits, F32))
        off += 2 * w.size
    g_win, g_wout, g_wd, g_uk, g_uv, g_wbin, g_wuq, g_wbout = full
    cols = lambda g: jnp.moveaxis(g, 0, -2).reshape(g.shape[1:-1] + (N_CHIPS * g.shape[-1],))
    w_in = cols(g_win)
    w_out = g_wout.reshape(E, D)
    wd = g_wd.reshape(D, RKV + ROPE_DIM)
    zpad = jnp.zeros((D, ROPE_HALF), BF16)
    wd_p = jnp.concatenate(
        [wd[:, :RKV], wd[:, RKV:RKV + ROPE_HALF], zpad, wd[:, RKV + ROPE_HALF:], zpad,
         jnp.zeros((D, WK - RKV - LANES), BF16)], axis=1)
    w_bin = cols(g_wbin)
    w_z = w_bin[:, RQ:]
    wb_small = jnp.concatenate([wd_p, w_bin[:, :RQ]], axis=1)
    wb_all = jnp.concatenate([wd_p, w_bin], axis=1)
    w_kv = jnp.concatenate([g_uk.reshape(RKV, HV), g_uv.reshape(RKV, HV)], axis=1)
    wuq = g_wuq.reshape(RQ, H, DN + ROPE_DIM)
    zq = jnp.zeros((RQ, H, ROPE_HALF), BF16)
    w_qr = jnp.concatenate([wuq[:, :, DN:DN + ROPE_HALF], zq, wuq[:, :, DN + ROPE_HALF:], zq], axis=2)
    w_q = jnp.concatenate([wuq[:, :, :DN].reshape(RQ, HV), w_qr.reshape(RQ, HV)], axis=1)
    w_bout = g_wbout.reshape(HV, D)
    b_in = cols(fvec[0][:, None, :])
    conv_w = cols(fvec[1])
    conv_b, norm_g, norm_b, b_out = (cols(f[:, None, :]) for f in fvec[2:])

    freqs = ROPE_THETA ** (-jnp.arange(0, ROPE_DIM, 2, dtype=F32) / ROPE_DIM)
    ang = positions[0].astype(F32)[:, None] * freqs
    cs, sn = jnp.cos(ang), jnp.sin(ang)
    ones, zeros = jnp.ones_like(cs), jnp.zeros_like(cs)
    cos_t = jnp.concatenate([cs, ones, cs, ones], axis=1)
    sin_t = jnp.concatenate([-sn, zeros, sn, zeros], axis=1)

    row = lambda a: a.reshape(1, -1)
    g0, b0, g1, b1 = row(ln_g[0]), row(ln_b[0]), row(ln_g[1]), row(ln_b[1])
    kv_g, q_g = row(kv_norm_g), row(b_q_norm_g[0])
    plain = lambda acc, ins, i, j: [acc]

    (proj,) = _row_mm("a_in", [((xb,), None)], w_in, nt=False, tm=t1024, tn=_fit(1536, 3 * E), tk=_fit(1024, D),
                      outs=[((T, 3 * E), F32, 'tile')], epi=lambda acc, ins, i, j: [acc + ins[0]],
                      epi_ins=[(b_in, 'col')])
    u1 = _conv_fwd(proj, conv_w, conv_b, E, tmw)
    u4 = _conv_post(u1, proj, norm_g, norm_b, E, tmw)

    def ln_epi(acc, ins, i, j):
        bias, res, g, b = ins
        xhat, rstd = _ln_stats(alpha * res + acc + bias)
        h = xhat * g + b
        return [h, h, xhat, rstd]

    h1, h1b, xhat1, rstd1 = _row_mm(
        "a_out", [((u4,), None)], w_out, nt=False, tm=t512, tn=D, tk=_fit(2048, E),
        outs=[((T, D), F32, 'tile'), ((T, D), BF16, 'tile'), ((T, D), F32, 'tile'), ((T, 1), F32, 'row')],
        epi=ln_epi, epi_ins=[(b_out, 'col'), (xs, 'tile'), (g0, 'col'), (b0, 'col')])

    tkb = _fit(512, _gcd(WK, RQ, HV))
    (pb,) = _row_mm("b_in", [((h1b,), None)], wb_small, nt=False, tm=t1024, tn=_fit(1024, Z_OFF),
                    tk=_fit(1024, D), outs=[((T, Z_OFF), F32, 'tile')], epi=plain)
    (zb,) = _row_mm("b_in_gate", [((h1b,), None)], w_z, nt=False, tm=t1024, tn=_fit(2048, HV),
                    tk=_fit(1024, D), outs=[((T, HV), F32, 'tile')], epi=plain)
    c_lat, kr, cqn = _norm_prep(pb, kv_g, q_g, cos_t, sin_t, RKV, RQ, WK, tmw)
    (kv,) = _row_mm("kv_up", [((c_lat,), None)], w_kv, nt=False, tm=t1024, tn=_fit(2048, HV),
                    tk=_fit(1024, RKV), outs=[((T, 2 * HV), BF16, 'tile')], epi=plain)
    tnq = _fit(2048, HV)
    half_q = HV // tnq

    def q_epi(acc, ins, i, j):
        return [jnp.where(j >= half_q, _rope(acc, ins[0], ins[1]), acc)]

    (q_all,) = _row_mm("q_up", [((cqn,), None)], w_q, nt=False, tm=t1024, tn=tnq, tk=_fit(1024, RQ),
                       outs=[((T, 2 * HV), BF16, 'tile')], epi=q_epi,
                       epi_ins=[(cos_t, 'row'), (sin_t, 'row')])
    o, lse = _attn_fwd(q_all, kv, kr, H, tq, scale)

    def loss_epi(acc, ins, i, j):
        res, g, b, target = ins
        xhat, rstd = _ln_stats(alpha * res + acc)
        diff = xhat * g + b - target
        dr, dg, db = _ln_bwd(diff / D, xhat, rstd, g)
        return [dr, 0.5 * jnp.sum(diff * diff, keepdims=True) / D, dg, db]

    dr1, loss_part, dg1, db1 = _row_mm(
        "b_out", [((o, zb), _gate)], w_bout, nt=False, tm=tmw, tn=D, tk=_fit(2048, HV),
        outs=[((T, D), F32, 'tile'), ((1, 1), F32, 'acc'), ((1, D), F32, 'acc'), ((1, D), F32, 'acc')],
        epi=loss_epi, epi_ins=[(h1, 'tile'), (g1, 'col'), (b1, 'col'), (tgt, 'tile')])

    def gate_bwd_epi(acc, ins, i, j):
        return [acc * _silu(ins[1]), acc * ins[0] * _silu_grad(ins[1])]

    do, dz = _row_mm(
        "b_out_bwd", [((dr1,), None)], w_bout, nt=True, tm=tmw, tn=_fit(2048, HV), tk=_fit(1024, D),
        outs=[((T, HV), BF16, 'tile'), ((T, HV), BF16, 'tile')], epi=gate_bwd_epi,
        epi_ins=[(o, 'tile'), (zb, 'tile')])
    gw_bout = _tn_mm("dw_b_out", (o, zb), _gate, [((dr1,), None)], tn=_fit(1024, D), tk=t512, out_dtype=BF16)
    dqn, dqr, dkn, dkr_h, dv = _attn_bwd(q_all, kv, kr, do, o, lse, H, tq, scale)
    dqr_pre, dkr_pre = _rope_bwd(dqr, dkr_h, cos_t, sin_t, H, tmw)

    def cq_bwd_epi(acc, ins, i, j):
        dx, dg = _rms_bwd(acc, ins[0], ins[1])
        return [dx, dg]

    dcq, dqg = _row_mm(
        "q_up_bwd", [((dqn,), None), ((dqr_pre,), None)], w_q, nt=True, tm=tmw, tn=RQ, tk=_fit(2048, HV),
        outs=[((T, RQ), BF16, 'tile'), ((1, RQ), F32, 'acc')], epi=cq_bwd_epi,
        epi_ins=[(pb, pl.BlockSpec((tmw, RQ), lambda i, j, k: (i, WK // RQ))), (q_g, 'col')])
    gw_q = _tn_mm("dw_q_up", (cqn,), None, [((dqn,), None), ((dqr_pre,), None)],
                  tn=_fit(2048, HV), tk=t1024, out_dtype=BF16)

    def ckv_bwd_epi(acc, ins, i, j):
        blk, dkr_t, g = ins
        dx, dg = _rms_bwd(acc, blk[:, :RKV], g)
        parts = [dx, dkr_t]
        if WK > RKV + LANES:
            parts.append(jnp.zeros((dx.shape[0], WK - RKV - LANES), F32))
        return [jnp.concatenate(parts, axis=1), dg]

    dckv, dkvg = _row_mm(
        "kv_up_bwd", [((dkn,), None), ((dv,), None)], w_kv, nt=True, tm=tmw, tn=RKV, tk=_fit(2048, HV),
        outs=[((T, WK), BF16, pl.BlockSpec((tmw, WK), lambda i, j, k: (i, 0))), ((1, RKV), F32, 'acc')],
        epi=ckv_bwd_epi,
        epi_ins=[(pb, pl.BlockSpec((tmw, WK), lambda i, j, k: (i, 0))), (dkr_pre, 'row'), (kv_g, 'col')])
    gw_kv = _tn_mm("dw_kv_up", (c_lat,), None, [((dkn,), None), ((dv,), None)],
                   tn=_fit(2048, HV), tk=t1024, out_dtype=BF16)

    def ln1_bwd_epi(acc, ins, i, j):
        dr_up, xhat, rstd, g = ins
        dr, dg, db = _ln_bwd(alpha * dr_up + acc, xhat, rstd, g)
        return [dr, dg, db]

    dp_segs = [((dckv,), None), ((dcq,), None), ((dz,), None)]
    dr0, dg0, db0 = _row_mm(
        "b_in_bwd", dp_segs, wb_all, nt=True, tm=t512, tn=D, tk=tkb,
        outs=[((T, D), F32, 'tile'), ((1, D), F32, 'acc'), ((1, D), F32, 'acc')], epi=ln1_bwd_epi,
        epi_ins=[(dr1, 'tile'), (xhat1, 'tile'), (rstd1, 'row'), (g0, 'col')])
    gw_ball = _tn_mm("dw_b_in", (h1b,), None, dp_segs, tn=tkb, tk=t1024, out_dtype=BF16)

    def conv_branch_bwd_epi(acc, ins, i, j):
        u1_t, z, g, b = ins
        xhat, rstd = _ln_stats(u1_t)
        u2 = xhat * g + b
        du3 = acc * _silu(z)
        dz_a = acc * _silu(u2) * _silu_grad(z)
        du1, dg, db = _ln_bwd(du3 * _silu_grad(u2), xhat, rstd, g)
        return [du1, dz_a, dg, db]

    du1, dz_a, dng, dnb = _row_mm(
        "a_out_bwd", [((dr0,), None)], w_out, nt=True, tm=tmw, tn=E, tk=_fit(1024, D),
        outs=[((T, E), F32, 'tile'), ((T, E), BF16, 'tile'), ((1, E), F32, 'acc'), ((1, E), F32, 'acc')],
        epi=conv_branch_bwd_epi,
        epi_ins=[(u1, 'tile'), (proj, pl.BlockSpec((tmw, E), lambda i, j, k: (i, 2))), (norm_g, 'col'),
                 (norm_b, 'col')])
    gw_out, dbo = _tn_mm("dw_a_out", (u4,), None, [((dr0,), None)], tn=_fit(1024, D), tk=t1024,
                         out_dtype=BF16, colsum=True)
    dval, dgate, dcw, dcb = _conv_bwd(du1, proj, conv_w, E, tmw)
    dproj_segs = [((dval,), None), ((dgate,), None), ((dz_a,), None)]
    (grad_x,) = _row_mm(
        "a_in_bwd", dproj_segs, w_in, nt=True, tm=t512, tn=_fit(1024, D), tk=_fit(2048, E),
        outs=[((T, D), F32, 'tile')], epi=lambda acc, ins, i, j: [alpha * ins[0] + acc],
        epi_ins=[(dr0, 'tile')])
    gw_in, dbi = _tn_mm("dw_a_in", (xb,), None, dproj_segs, tn=_fit(2048, E), tk=t512, out_dtype=BF16,
                        colsum=True)

    shard_cols = lambda g: jnp.moveaxis(g.reshape(g.shape[0], N_CHIPS, -1), 1, 0)
    shard_rows = lambda g: g.reshape(N_CHIPS, g.shape[0] // N_CHIPS, g.shape[1])
    gq = gw_q.reshape(RQ, 2, H, LANES)
    g_uq = jnp.concatenate(
        [gq[:, 0], gq[:, 1, :, :ROPE_HALF], gq[:, 1, :, 2 * ROPE_HALF:3 * ROPE_HALF]], axis=2)
    g_wd_full = jnp.concatenate(
        [gw_ball[:, :RKV], gw_ball[:, RKV:RKV + ROPE_HALF],
         gw_ball[:, RKV + 2 * ROPE_HALF:RKV + 3 * ROPE_HALF]], axis=1)
    big_grads = [
        shard_cols(gw_in),
        shard_rows(gw_out),
        shard_rows(g_wd_full),
        shard_rows(gw_kv[:, :HV]),
        shard_rows(gw_kv[:, HV:]),
        shard_cols(gw_ball[:, WK:]),
        shard_rows(g_uq.reshape(RQ, H * (DN + ROPE_DIM))),
        shard_rows(gw_bout),
    ]
    received = _scatter_grads(big_grads)
    names = ["a_w_in", "a_w_out", "kv_w_down", "kv_w_uk", "kv_w_uv", "b_w_in", "b_w_uq", "b_w_out"]
    core_sums = [_sum_chips("sum_" + n, r) for n, r in zip(names, received)]
    sibling_sums = _swap_cores(core_sums)
    big_w = [a_w_in, a_w_out, kv_w_down, kv_w_uk, kv_w_uv, b_w_in, b_w_uq, b_w_out]
    big_m = [m_a_w_in, m_a_w_out, m_kv_w_down, m_kv_w_uk, m_kv_w_uv, m_b_w_in, m_b_w_uq, m_b_w_out]
    big_v = [v_a_w_in, v_a_w_out, v_kv_w_down, v_kv_w_uk, v_kv_w_uv, v_b_w_in, v_b_w_uq, v_b_w_out]
    big_out = {}
    for n, mine, theirs, w, m, v in zip(names, core_sums, sibling_sums, big_w, big_m, big_v):
        res = _adamw("adamw_" + n, [mine, theirs], *(a.reshape(mine.shape) for a in (w, m, v)))
        big_out[n] = [r.reshape(w.shape) for r in res]

    small_full = [jnp.concatenate([dg0, dg1]), jnp.concatenate([db0, db1]), dbi, dcw, dcb, dng, dnb, dbo,
                  dkvg, dqg]
    sflat = jnp.concatenate([g.reshape(-1) for g in small_full])
    summed = _all_reduce_small(_pack_rows(sflat, F32, 8 * LANES)).reshape(-1)
    soff = 0
    sgrads = []
    for g in small_full:
        sgrads.append(summed[soff:soff + g.size].reshape(g.shape))
        soff += g.size
    local_cols = lambda g, n: lax.dynamic_slice_in_dim(g, chip * n, n, axis=g.ndim - 1)
    snames = ["ln_g", "ln_b", "a_b_in", "a_conv_w", "a_conv_b", "a_norm_g", "a_norm_b", "a_b_out",
              "kv_norm_g", "b_q_norm_g"]
    small_w = [ln_g, ln_b, a_b_in, a_conv_w, a_conv_b, a_norm_g, a_norm_b, a_b_out, kv_norm_g, b_q_norm_g]
    small_m = [m_ln_g, m_ln_b, m_a_b_in, m_a_conv_w, m_a_conv_b, m_a_norm_g, m_a_norm_b, m_a_b_out,
               m_kv_norm_g, m_b_q_norm_g]
    small_v = [v_ln_g, v_ln_b, v_a_b_in, v_a_conv_w, v_a_conv_b, v_a_norm_g, v_a_norm_b, v_a_b_out,
               v_kv_norm_g, v_b_q_norm_g]
    sharded = {"a_b_in", "a_conv_w", "a_conv_b", "a_norm_g", "a_norm_b", "a_b_out"}
    local_g = [(local_cols(g, w.shape[-1]) if n in sharded else g).reshape(w.shape)
               for n, g, w in zip(snames, sgrads, small_w)]
    pack_small = lambda arrs: _pack_rows(jnp.concatenate([a.reshape(-1) for a in arrs]), F32, 8 * LANES)
    sres = _adamw("adamw_vectors", [pack_small(local_g)], pack_small(small_w), pack_small(small_m),
                  pack_small(small_v))
    small_out = {}
    soff = 0
    for n, w in zip(snames, small_w):
        small_out[n] = [r.reshape(-1)[soff:soff + w.size].reshape(w.shape) for r in sres]
        soff += w.size

    loss = lax.psum(loss_part[0, 0], ("x", "y", "c"))
    order = ["ln_g", "ln_b", "a_w_in", "a_b_in", "a_conv_w", "a_conv_b", "a_norm_g", "a_norm_b", "a_w_out",
             "a_b_out", "kv_w_down", "kv_norm_g", "kv_w_uk", "kv_w_uv", "b_w_in", "b_q_norm_g", "b_w_uq",
             "b_w_out"]
    outs = {**big_out, **small_out}
    result = [loss, grad_x[None]]
    for part in range(4):
        result += [outs[n][part] for n in order]
    return tuple(result)
```

```python
import functools
import math

import jax
import jax.numpy as jnp
from jax import lax
from jax.experimental import pallas as pl
from jax.experimental.pallas import tpu as pltpu

F32, BF16 = jnp.float32, jnp.bfloat16
NN = (((1,), (0,)), ((), ()))
NT = (((1,), (1,)), ((), ()))
TN = (((0,), (0,)), ((), ()))
MESH = pl.DeviceIdType.MESH
ANY = pl.BlockSpec(memory_space=pl.ANY)

LANES = 128
BF16_ROWS = 16
VMEM_LIMIT = 56 * 1024 * 1024
N_CHIPS = 4
LN_EPS = 1e-5
RMS_EPS = 1e-6
MASK_VALUE = -1e30
ROPE_THETA = 10000.0
ROPE_DIM = 64
ROPE_HALF = ROPE_DIM // 2
ADAM_LR, ADAM_B1, ADAM_B2, ADAM_EPS, ADAM_WD, ADAM_STEP = 0.001, 0.9, 0.999, 1e-08, 0.01, 10

MAX_TILE = 2048
TM_WIDE = 256
TQ = 512
CONV_HALO = 32
CONV_LC = 512
CONV_RC = 32
ADAM_ROWS = 64


def _dot(a, b, dims):
    return lax.dot_general(a.astype(BF16), b.astype(BF16), dims, preferred_element_type=F32)


def _sig(x):
    return 1.0 / (1.0 + jnp.exp(-x))


def _params(n_axes):
    return pltpu.CompilerParams(dimension_semantics=("arbitrary",) * n_axes, vmem_limit_bytes=VMEM_LIMIT)


def _gcd(*v):
    return functools.reduce(math.gcd, v)


def _fit(want, dim):
    return math.gcd(min(want, MAX_TILE), dim)


def _row_mm(name, a_segs, b, *, nt, tm, tn, tk, outs, epi, epi_ins=()):
    M = a_segs[0][0][0].shape[0]
    N = b.shape[0] if nt else b.shape[1]
    nkb = [arrs[0].shape[1] // tk for arrs, _ in a_segs]
    koff = [sum(nkb[:s]) for s in range(len(nkb))]
    ni, nj, nk = M // tm, N // tn, sum(nkb)
    assert M % tm == 0 and N % tn == 0 and all(arrs[0].shape[1] % tk == 0 for arrs, _ in a_segs), name
    assert (b.shape[1] if nt else b.shape[0]) == nk * tk, name

    def spec_of(shape, kind):
        if isinstance(kind, pl.BlockSpec):
            return kind
        if kind == 'tile':
            return pl.BlockSpec((tm, tn), lambda i, j, k: (i, j))
        if kind == 'row':
            return pl.BlockSpec((tm, shape[1]), lambda i, j, k: (i, 0))
        if kind == 'col':
            return pl.BlockSpec((1, tn), lambda i, j, k: (0, j))
        assert kind == 'acc' and nj == 1, name
        return pl.BlockSpec(shape, lambda i, j, k: (0,) * len(shape))

    in_specs, operands = [], []
    for s, (arrs, _) in enumerate(a_segs):
        for arr in arrs:
            in_specs.append(pl.BlockSpec(
                (tm, tk), lambda i, j, k, s=s: (i, jnp.clip(k - koff[s], 0, nkb[s] - 1))))
            operands.append(arr)
    in_specs.append(pl.BlockSpec((tn, tk), lambda i, j, k: (j, k)) if nt
                    else pl.BlockSpec((tk, tn), lambda i, j, k: (k, j)))
    operands.append(b)
    for arr, kind in epi_ins:
        in_specs.append(spec_of(arr.shape, kind))
        operands.append(arr)
    out_specs = [spec_of(shape, kind) for shape, _, kind in outs]
    out_shape = [jax.ShapeDtypeStruct(shape, dtype) for shape, dtype, _ in outs]
    n_seg_refs = [len(arrs) for arrs, _ in a_segs]

    def body(*refs):
        pos = 0
        seg_refs = []
        for n in n_seg_refs:
            seg_refs.append(refs[pos:pos + n])
            pos += n
        b_ref = refs[pos]
        e_refs = refs[pos + 1:pos + 1 + len(epi_ins)]
        o_refs = refs[pos + 1 + len(epi_ins):pos + 1 + len(epi_ins) + len(outs)]
        i, j, k = pl.program_id(0), pl.program_id(1), pl.program_id(2)

        def product(fn, rs):
            a = rs[0][...] if fn is None else fn(*[r[...] for r in rs])
            return _dot(a, b_ref[...], NT if nt else NN)

        def finish(acc):
            res = epi(acc, [r[...] for r in e_refs], i, j)
            for o_ref, (_, _, kind), r in zip(o_refs, outs, res):
                if isinstance(kind, str) and kind == 'acc':
                    @pl.when(i == 0)
                    def _(o_ref=o_ref, r=r):
                        o_ref[...] = r

                    @pl.when(i > 0)
                    def _(o_ref=o_ref, r=r):
                        o_ref[...] += r
                else:
                    o_ref[...] = r.astype(o_ref.dtype)

        if nk == 1:
            finish(product(a_segs[0][1], seg_refs[0]))
            return
        acc_ref = refs[-1]

        @pl.when(k == 0)
        def _():
            acc_ref[...] = jnp.zeros_like(acc_ref)

        for s, ((_, fn), rs) in enumerate(zip(a_segs, seg_refs)):
            def accumulate(fn=fn, rs=rs):
                acc_ref[...] += product(fn, rs)
            if len(a_segs) == 1:
                accumulate()
            else:
                pl.when(jnp.logical_and(k >= koff[s], k < koff[s] + nkb[s]))(accumulate)

        @pl.when(k == nk - 1)
        def _():
            finish(acc_ref[...])

    return pl.pallas_call(
        body, name=name, grid=(ni, nj, nk), in_specs=in_specs, out_specs=out_specs, out_shape=out_shape,
        scratch_shapes=[] if nk == 1 else [pltpu.VMEM((tm, tn), F32)], compiler_params=_params(3),
    )(*operands)


def _tn_mm(name, a_arrs, a_fn, b_segs, *, tn, tk, out_dtype, shard_major=False, colsum=False):
    T, M = a_arrs[0].shape
    nbj = [arrs[0].shape[1] // tn for arrs, _ in b_segs]
    joff = [sum(nbj[:s]) for s in range(len(nbj))]
    nj, nk = sum(nbj), T // tk
    N = nj * tn
    assert T % tk == 0 and all(arrs[0].shape[1] % tn == 0 for arrs, _ in b_segs), name

    in_specs = [pl.BlockSpec((tk, M), lambda j, k: (k, 0)) for _ in a_arrs]
    operands = list(a_arrs)
    for s, (arrs, _) in enumerate(b_segs):
        for arr in arrs:
            in_specs.append(pl.BlockSpec(
                (tk, tn), lambda j, k, s=s: (k, jnp.clip(j - joff[s], 0, nbj[s] - 1))))
            operands.append(arr)
    if shard_major:
        per = (N // N_CHIPS) // tn
        assert per * tn * N_CHIPS == N, name
        out_shape = [jax.ShapeDtypeStruct((N_CHIPS, M, N // N_CHIPS), out_dtype)]
        out_specs = [pl.BlockSpec((1, M, tn), lambda j, k: (j // per, 0, j % per))]
    else:
        out_shape = [jax.ShapeDtypeStruct((M, N), out_dtype)]
        out_specs = [pl.BlockSpec((M, tn), lambda j, k: (0, j))]
    if colsum:
        out_shape.append(jax.ShapeDtypeStruct((1, N), F32))
        out_specs.append(pl.BlockSpec((1, tn), lambda j, k: (0, j)))
    n_a = len(a_arrs)
    n_seg_refs = [len(arrs) for arrs, _ in b_segs]

    def body(*refs):
        a_refs = refs[:n_a]
        pos = n_a
        seg_refs = []
        for n in n_seg_refs:
            seg_refs.append(refs[pos:pos + n])
            pos += n
        o_ref = refs[pos]
        cs_ref = refs[pos + 1] if colsum else None
        acc_ref = refs[-1]
        j, k = pl.program_id(0), pl.program_id(1)

        @pl.when(k == 0)
        def _():
            acc_ref[...] = jnp.zeros_like(acc_ref)
            if colsum:
                cs_ref[...] = jnp.zeros_like(cs_ref)

        for s, ((_, fn), rs) in enumerate(zip(b_segs, seg_refs)):
            def accumulate(fn=fn, rs=rs):
                a = a_refs[0][...] if a_fn is None else a_fn(*[r[...] for r in a_refs])
                bt = rs[0][...] if fn is None else fn(*[r[...] for r in rs])
                acc_ref[...] += _dot(a, bt, TN)
                if colsum:
                    cs_ref[...] += jnp.sum(bt.astype(F32), axis=0, keepdims=True)
            if len(b_segs) == 1:
                accumulate()
            else:
                pl.when(jnp.logical_and(j >= joff[s], j < joff[s] + nbj[s]))(accumulate)

        @pl.when(k == nk - 1)
        def _():
            if shard_major:
                o_ref[0] = acc_ref[...].astype(o_ref.dtype)
            else:
                o_ref[...] = acc_ref[...].astype(o_ref.dtype)

    res = pl.pallas_call(
        body, name=name, grid=(nj, nk), in_specs=in_specs, out_specs=out_specs, out_shape=out_shape,
        scratch_shapes=[pltpu.VMEM((M, tn), F32)], compiler_params=_params(2),
    )(*operands)
    return res if colsum else res[0]


def _silu(z):
    return z * _sig(z)


def _silu_grad(z):
    s = _sig(z)
    return s * (1.0 + z * (1.0 - s))


def _gate(o, z):
    return o * _silu(z)


def _ln_stats(r):
    mu = jnp.mean(r, axis=1, keepdims=True)
    xc = r - mu
    var = jnp.mean(xc * xc, axis=1, keepdims=True)
    rstd = lax.rsqrt(var + LN_EPS)
    return xc * rstd, rstd


def _ln_bwd(dy, xhat, rstd, g):
    dxh = dy * g
    m1 = jnp.mean(dxh, axis=1, keepdims=True)
    m2 = jnp.mean(dxh * xhat, axis=1, keepdims=True)
    return (rstd * (dxh - m1 - xhat * m2), jnp.sum(dy * xhat, axis=0, keepdims=True),
            jnp.sum(dy, axis=0, keepdims=True))


def _rms_fwd(x, g):
    rstd = lax.rsqrt(jnp.mean(x * x, axis=1, keepdims=True) + RMS_EPS)
    return x * rstd * g


def _rms_bwd(dy, x, g):
    rstd = lax.rsqrt(jnp.mean(x * x, axis=1, keepdims=True) + RMS_EPS)
    xn = x * rstd
    dxn = dy * g
    return rstd * (dxn - xn * jnp.mean(dxn * xn, axis=1, keepdims=True)), jnp.sum(dy * xn, axis=0, keepdims=True)


def _rope(x, cos, sin, transpose=False):
    parts = []
    for g in range(x.shape[1] // LANES):
        xg = x[:, g * LANES:(g + 1) * LANES]
        if transpose:
            parts.append(xg * cos + pltpu.roll(xg * sin, LANES // 2, 1))
        else:
            parts.append(xg * cos + pltpu.roll(xg, LANES // 2, 1) * sin)
    return parts[0] if len(parts) == 1 else jnp.concatenate(parts, axis=1)


def _conv_fwd(proj, conv_w, conv_b, E, tm):
    T = proj.shape[0]
    kc = conv_w.shape[0]
    lc, hb, rc = min(CONV_LC, E), CONV_HALO, min(CONV_RC, tm)
    nl, ni, ratio = E // lc, T // tm, tm // hb
    gate_off = E // lc

    def body(val_ref, gate_ref, valh_ref, gateh_ref, w_ref, cb_ref, u1_ref, ubuf):
        i = pl.program_id(1)
        ubuf[hb:, :] = val_ref[...] * _sig(gate_ref[...])
        halo = valh_ref[...] * _sig(gateh_ref[...])
        ubuf[0:hb, :] = jnp.where(i > 0, halo, 0.0)
        for r0 in range(0, tm, rc):
            acc = jnp.zeros((rc, lc), F32) + cb_ref[...]
            for k in range(kc):
                acc += w_ref[k:k + 1, :] * ubuf[r0 + hb - (kc - 1) + k:r0 + hb - (kc - 1) + k + rc, :]
            u1_ref[r0:r0 + rc, :] = acc

    return pl.pallas_call(
        body, name="conv_fwd", grid=(nl, ni),
        in_specs=[
            pl.BlockSpec((tm, lc), lambda l, i: (i, l)),
            pl.BlockSpec((tm, lc), lambda l, i: (i, gate_off + l)),
            pl.BlockSpec((hb, lc), lambda l, i: (jnp.maximum(i * ratio - 1, 0), l)),
            pl.BlockSpec((hb, lc), lambda l, i: (jnp.maximum(i * ratio - 1, 0), gate_off + l)),
            pl.BlockSpec((kc, lc), lambda l, i: (0, l)),
            pl.BlockSpec((1, lc), lambda l, i: (0, l)),
        ],
        out_specs=pl.BlockSpec((tm, lc), lambda l, i: (i, l)),
        out_shape=jax.ShapeDtypeStruct((T, E), F32),
        scratch_shapes=[pltpu.VMEM((hb + tm, lc), F32)], compiler_params=_params(2),
    )(proj, proj, proj, proj, conv_w, conv_b)


def _conv_post(u1, proj, norm_g, norm_b, E, tm):
    T = u1.shape[0]

    def body(u1_ref, z_ref, g_ref, b_ref, u4_ref):
        xhat, _ = _ln_stats(u1_ref[...])
        u4_ref[...] = (_silu(xhat * g_ref[...] + b_ref[...]) * _silu(z_ref[...])).astype(BF16)

    return pl.pallas_call(
        body, name="conv_post", grid=(T // tm,),
        in_specs=[pl.BlockSpec((tm, E), lambda i: (i, 0)), pl.BlockSpec((tm, E), lambda i: (i, 2)),
                  pl.BlockSpec((1, E), lambda i: (0, 0)), pl.BlockSpec((1, E), lambda i: (0, 0))],
        out_specs=pl.BlockSpec((tm, E), lambda i: (i, 0)),
        out_shape=jax.ShapeDtypeStruct((T, E), BF16), compiler_params=_params(1),
    )(u1, proj, norm_g, norm_b)


def _conv_bwd(du1, proj, conv_w, E, tm):
    T = du1.shape[0]
    kc = conv_w.shape[0]
    lc, hb, rc = min(CONV_LC, E), CONV_HALO, min(CONV_RC, tm)
    nl, ni, ratio = E // lc, T // tm, tm // hb
    gate_off = E // lc
    last_halo = T // hb - 1

    def body(du_ref, dun_ref, val_ref, gate_ref, valh_ref, gateh_ref, w_ref,
             dval_ref, dgate_ref, dw_ref, db_ref, ubuf, dbuf):
        i = pl.program_id(1)
        val, sg = val_ref[...], _sig(gate_ref[...])
        ubuf[hb:, :] = val * sg
        halo = valh_ref[...] * _sig(gateh_ref[...])
        ubuf[0:hb, :] = jnp.where(i > 0, halo, 0.0)
        dbuf[0:tm, :] = du_ref[...]
        dbuf[tm:, :] = jnp.where(i < ni - 1, dun_ref[...], 0.0)

        @pl.when(i == 0)
        def _():
            dw_ref[...] = jnp.zeros_like(dw_ref)
            db_ref[...] = jnp.zeros_like(db_ref)

        db_ref[...] += jnp.sum(du_ref[...], axis=0, keepdims=True)
        for k in range(kc):
            acc = jnp.zeros((rc, lc), F32)
            for r0 in range(0, tm, rc):
                acc += ubuf[r0 + hb - (kc - 1) + k:r0 + hb - (kc - 1) + k + rc, :] * dbuf[r0:r0 + rc, :]
            dw_ref[k:k + 1, :] += jnp.sum(acc, axis=0, keepdims=True)
        for r0 in range(0, tm, rc):
            acc = jnp.zeros((rc, lc), F32)
            for k in range(kc):
                acc += w_ref[k:k + 1, :] * dbuf[r0 + (kc - 1) - k:r0 + (kc - 1) - k + rc, :]
            v, s = val[r0:r0 + rc, :], sg[r0:r0 + rc, :]
            dval_ref[r0:r0 + rc, :] = (acc * s).astype(BF16)
            dgate_ref[r0:r0 + rc, :] = (acc * v * s * (1.0 - s)).astype(BF16)

    return pl.pallas_call(
        body, name="conv_bwd", grid=(nl, ni),
        in_specs=[
            pl.BlockSpec((tm, lc), lambda l, i: (i, l)),
            pl.BlockSpec((hb, lc), lambda l, i: (jnp.minimum((i + 1) * ratio, last_halo), l)),
            pl.BlockSpec((tm, lc), lambda l, i: (i, l)),
            pl.BlockSpec((tm, lc), lambda l, i: (i, gate_off + l)),
            pl.BlockSpec((hb, lc), lambda l, i: (jnp.maximum(i * ratio - 1, 0), l)),
            pl.BlockSpec((hb, lc), lambda l, i: (jnp.maximum(i * ratio - 1, 0), gate_off + l)),
            pl.BlockSpec((kc, lc), lambda l, i: (0, l)),
        ],
        out_specs=[pl.BlockSpec((tm, lc), lambda l, i: (i, l)), pl.BlockSpec((tm, lc), lambda l, i: (i, l)),
                   pl.BlockSpec((kc, lc), lambda l, i: (0, l)), pl.BlockSpec((1, lc), lambda l, i: (0, l))],
        out_shape=[jax.ShapeDtypeStruct((T, E), BF16), jax.ShapeDtypeStruct((T, E), BF16),
                   jax.ShapeDtypeStruct((kc, E), F32), jax.ShapeDtypeStruct((1, E), F32)],
        scratch_shapes=[pltpu.VMEM((hb + tm, lc), F32), pltpu.VMEM((tm + hb, lc), F32)],
        compiler_params=_params(2),
    )(du1, du1, proj, proj, proj, proj, conv_w)


def _norm_prep(pb, kv_g, q_g, cos, sin, rkv, rq, wk, tm):
    T = pb.shape[0]

    def body(ckv_ref, cq_ref, kg_ref, qg_ref, cos_ref, sin_ref, c_ref, kr_ref, cqn_ref):
        blk = ckv_ref[...]
        c_ref[...] = _rms_fwd(blk[:, :rkv], kg_ref[...]).astype(BF16)
        kr_ref[...] = _rope(blk[:, rkv:rkv + LANES], cos_ref[...], sin_ref[...]).astype(BF16)
        cqn_ref[...] = _rms_fwd(cq_ref[...], qg_ref[...]).astype(BF16)

    return pl.pallas_call(
        body, name="norm_prep", grid=(T // tm,),
        in_specs=[pl.BlockSpec((tm, wk), lambda i: (i, 0)), pl.BlockSpec((tm, rq), lambda i: (i, wk // rq)),
                  pl.BlockSpec((1, rkv), lambda i: (0, 0)), pl.BlockSpec((1, rq), lambda i: (0, 0)),
                  pl.BlockSpec((tm, LANES), lambda i: (i, 0)), pl.BlockSpec((tm, LANES), lambda i: (i, 0))],
        out_specs=[pl.BlockSpec((tm, rkv), lambda i: (i, 0)), pl.BlockSpec((tm, LANES), lambda i: (i, 0)),
                   pl.BlockSpec((tm, rq), lambda i: (i, 0))],
        out_shape=[jax.ShapeDtypeStruct((T, rkv), BF16), jax.ShapeDtypeStruct((T, LANES), BF16),
                   jax.ShapeDtypeStruct((T, rq), BF16)],
        compiler_params=_params(1),
    )(pb, pb, kv_g, q_g, cos, sin)


def _attn_fwd(q_all, kv, kr, H, tq, scale):
    T = q_all.shape[0]
    nq = T // tq
    pair = 2
    W = pair * LANES
    assert H % pair == 0
    hp_n = H // pair

    def body(qn_ref, qr_ref, kn_ref, kr_ref, v_ref, o_ref, lse_ref, *scratch):
        qi = pl.program_id(1)
        chains = [scratch[4 * a:4 * a + 4] for a in range(pair)]
        lanes = [slice(a * LANES, (a + 1) * LANES) for a in range(pair)]
        groups = [slice(c * LANES, (c + 1) * LANES) for c in range(tq // LANES)]

        def fold(x, op):
            r = x[:, groups[0]]
            for gsl in groups[1:]:
                r = op(r, x[:, gsl])
            return r

        for _, m_sc, l_sc, acc_sc in chains:
            m_sc[...] = jnp.full_like(m_sc, MASK_VALUE)
            l_sc[...] = jnp.zeros_like(l_sc)
            acc_sc[...] = jnp.zeros_like(acc_sc)

        def scores(j, masked):
            rows = pl.ds(pl.multiple_of(j * tq, tq), tq)
            krope = kr_ref[rows, :]
            for a, (s_sc, m_sc, _, _) in enumerate(chains):
                q = jnp.concatenate([qn_ref[:, lanes[a]], qr_ref[:, lanes[a]]], axis=1)
                k = jnp.concatenate([kn_ref[rows, lanes[a]], krope], axis=1)
                s = _dot(q, k, NT) * scale
                if masked:
                    row = lax.broadcasted_iota(jnp.int32, s.shape, 0)
                    col = lax.broadcasted_iota(jnp.int32, s.shape, 1)
                    s = jnp.where(col <= row, s, MASK_VALUE)
                s_sc[j] = s
                m_sc[...] = jnp.maximum(m_sc[...], fold(s, jnp.maximum))

        def unmasked(j, carry):
            scores(j, False)
            return carry

        lax.fori_loop(0, qi, unmasked, 0)
        scores(qi, True)
        for _, m_sc, _, _ in chains:
            m_sc[...] = jnp.broadcast_to(jnp.max(m_sc[...], axis=1, keepdims=True), m_sc.shape)

        def weigh(j, carry):
            rows = pl.ds(pl.multiple_of(j * tq, tq), tq)
            for a, (s_sc, m_sc, l_sc, acc_sc) in enumerate(chains):
                s, m = s_sc[j], m_sc[...]
                p = jnp.concatenate([jnp.exp(s[:, gsl] - m) for gsl in groups], axis=1)
                l_sc[...] += fold(p, jnp.add)
                acc_sc[...] += _dot(p, v_ref[rows, lanes[a]], NN)
            return carry

        lax.fori_loop(0, qi + 1, weigh, 0)
        for a, (_, m_sc, l_sc, acc_sc) in enumerate(chains):
            l = jnp.sum(l_sc[...], axis=1, keepdims=True)
            o_ref[:, lanes[a]] = acc_sc[...] / l
            lse_ref[a] = m_sc[:, 0:1] + jnp.log(l)

    chain_scratch = [pltpu.VMEM((nq, tq, tq), F32), pltpu.VMEM((tq, LANES), F32), pltpu.VMEM((tq, LANES), F32),
                     pltpu.VMEM((tq, LANES), F32)]
    return pl.pallas_call(
        body, name="attn_fwd", grid=(hp_n, nq),
        in_specs=[pl.BlockSpec((tq, W), lambda hp, qi: (qi, hp)),
                  pl.BlockSpec((tq, W), lambda hp, qi: (qi, hp_n + hp)),
                  pl.BlockSpec((T, W), lambda hp, qi: (0, hp)),
                  pl.BlockSpec((T, LANES), lambda hp, qi: (0, 0)),
                  pl.BlockSpec((T, W), lambda hp, qi: (0, hp_n + hp))],
        out_specs=[pl.BlockSpec((tq, W), lambda hp, qi: (qi, hp)),
                   pl.BlockSpec((pair, tq, 1), lambda hp, qi: (hp, qi, 0))],
        out_shape=[jax.ShapeDtypeStruct((T, H * LANES), F32), jax.ShapeDtypeStruct((H, T, 1), F32)],
        scratch_shapes=chain_scratch * pair, compiler_params=_params(2),
    )(q_all, q_all, kv, kr, kv)


def _attn_bwd(q_all, kv, kr, do, o, lse, H, tq, scale):
    T = q_all.shape[0]
    nq = T // tq
    HV = H * LANES
    pair = 2
    tk2 = pair * tq
    ng = T // tk2
    assert ng * tk2 == T

    def body(qn_ref, qr_ref, kn_ref, kr_ref, v_ref, do_ref, o_ref, lse_ref,
             dqn_ref, dqr_ref, dkn_ref, dkr_ref, dv_ref, dq_sc, dk_sc, dv_sc):
        g = pl.program_id(1)

        @pl.when(g == 0)
        def _():
            dq_sc[...] = jnp.zeros_like(dq_sc)

        dk_sc[...] = jnp.zeros_like(dk_sc)
        dv_sc[...] = jnp.zeros_like(dv_sc)

        def block(qi, modes):
            rows = pl.ds(pl.multiple_of(qi * tq, tq), tq)
            q = jnp.concatenate([qn_ref[rows, :], qr_ref[rows, :]], axis=1)
            dov = do_ref[rows, :]
            delta = jnp.sum(dov.astype(F32) * o_ref[rows, :], axis=1, keepdims=True)
            lse_q = lse_ref[0, rows, :]
            dq = None
            for c, masked in enumerate(modes):
                if masked is None:
                    continue
                kr_ = slice(c * tq, (c + 1) * tq)
                k = jnp.concatenate([kn_ref[kr_, :], kr_ref[kr_, :]], axis=1)
                s = _dot(q, k, NT) * scale
                if masked:
                    row = lax.broadcasted_iota(jnp.int32, s.shape, 0)
                    col = lax.broadcasted_iota(jnp.int32, s.shape, 1)
                    s = jnp.where(col <= row, s, MASK_VALUE)
                p = jnp.exp(s - lse_q)
                dv_sc[kr_, :] += _dot(p, dov, TN)
                dp = _dot(dov, v_ref[kr_, :], NT)
                ds = (p * (dp - delta) * scale).astype(BF16)
                dk_sc[kr_, :] += _dot(ds, q, TN)
                part = _dot(ds, k, NN)
                dq = part if dq is None else dq + part
            dq_sc[rows, :] += dq

        block(pair * g, (True, None))
        block(pair * g + 1, (False, True))

        def below(qi, carry):
            block(qi, (False, False))
            return carry

        lax.fori_loop(pair * g + pair, nq, below, 0)
        dkn_ref[...] = dk_sc[:, :LANES].astype(BF16)
        dkr_ref[...] = dk_sc[:, LANES:]
        dv_ref[...] = dv_sc[...].astype(BF16)

        @pl.when(g == ng - 1)
        def _():
            dqn_ref[...] = dq_sc[:, :LANES].astype(BF16)
            dqr_ref[...] = dq_sc[:, LANES:]

    whole = lambda col: pl.BlockSpec((T, LANES), col)
    tile = lambda col: pl.BlockSpec((tk2, LANES), col)
    return pl.pallas_call(
        body, name="attn_bwd", grid=(H, ng),
        in_specs=[whole(lambda h, g: (0, h)), whole(lambda h, g: (0, H + h)),
                  tile(lambda h, g: (g, h)), tile(lambda h, g: (g, 0)), tile(lambda h, g: (g, H + h)),
                  whole(lambda h, g: (0, h)), whole(lambda h, g: (0, h)),
                  pl.BlockSpec((1, T, 1), lambda h, g: (h, 0, 0))],
        out_specs=[whole(lambda h, g: (0, h)), whole(lambda h, g: (0, h)),
                   tile(lambda h, g: (g, h)), tile(lambda h, g: (g, h)), tile(lambda h, g: (g, h))],
        out_shape=[jax.ShapeDtypeStruct((T, HV), BF16), jax.ShapeDtypeStruct((T, HV), F32),
                   jax.ShapeDtypeStruct((T, HV), BF16), jax.ShapeDtypeStruct((T, HV), F32),
                   jax.ShapeDtypeStruct((T, HV), BF16)],
        scratch_shapes=[pltpu.VMEM((T, 2 * LANES), F32), pltpu.VMEM((tk2, 2 * LANES), F32),
                        pltpu.VMEM((tk2, LANES), F32)],
        compiler_params=_params(2),
    )(q_all, q_all, kv, kr, kv, do, o, lse)


def _rope_bwd(dqr, dkr_heads, cos, sin, H, tm):
    T, HV = dqr.shape

    def body(dqr_ref, dkr_ref, cos_ref, sin_ref, dq_ref, dk_ref):
        c, s = cos_ref[...], sin_ref[...]
        dq_ref[...] = _rope(dqr_ref[...], c, s, transpose=True).astype(BF16)
        dk = dkr_ref[...]
        tot = dk[:, 0:LANES]
        for h in range(1, H):
            tot = tot + dk[:, h * LANES:(h + 1) * LANES]
        dk_ref[...] = _rope(tot, c, s, transpose=True)

    return pl.pallas_call(
        body, name="rope_bwd", grid=(T // tm,),
        in_specs=[pl.BlockSpec((tm, HV), lambda i: (i, 0)), pl.BlockSpec((tm, HV), lambda i: (i, 0)),
                  pl.BlockSpec((tm, LANES), lambda i: (i, 0)), pl.BlockSpec((tm, LANES), lambda i: (i, 0))],
        out_specs=[pl.BlockSpec((tm, HV), lambda i: (i, 0)), pl.BlockSpec((tm, LANES), lambda i: (i, 0))],
        out_shape=[jax.ShapeDtypeStruct((T, HV), BF16), jax.ShapeDtypeStruct((T, LANES), F32)],
        compiler_params=_params(1),
    )(dqr, dkr_heads, cos, sin)


def _place():
    x, y, c = lax.axis_index("x"), lax.axis_index("y"), lax.axis_index("c")
    chips = [(1 - x, y), (x, 1 - y), (1 - x, 1 - y)]
    return x, y, c, chips


def _all_gather_chips(pack):
    rows = pack.shape[0]
    half = rows // 2
    assert half * 2 == rows and half % BF16_ROWS == 0

    def body(w_ref, out_ref, send_sems, recv_sems, local_sem):
        x, y, c, chips = _place()
        sibling = (x, y, 1 - c)

        def region(px, py, pc):
            return out_ref.at[2 * px + py, pl.ds(pc * half, half), :]

        def copy(k, block, to, src=None):
            return pltpu.make_async_remote_copy(
                src_ref=region(*block) if src is None else src, dst_ref=region(*block),
                send_sem=send_sems.at[k], recv_sem=recv_sems.at[k], device_id=to, device_id_type=MESH)

        mine = pltpu.make_async_copy(w_ref, out_ref.at[2 * x + y], local_sem)
        mine.start()
        my_half = w_ref.at[pl.ds(c * half, half), :]
        first = [copy(j, (x, y, c), (*chip, c), src=my_half) for j, chip in enumerate(chips)]
        for cp in first:
            cp.start()
        passed = [copy(3 + j, (*chip, c), sibling) for j, chip in enumerate(chips)]
        for j, chip in enumerate(chips):
            copy(j, (*chip, c), (x, y, c)).wait_recv()
            passed[j].start()
        for j, chip in enumerate(chips):
            copy(3 + j, (*chip, 1 - c), (x, y, c)).wait_recv()
        for cp in first + passed:
            cp.wait_send()
        mine.wait()

    return pl.pallas_call(
        body, name="gather_weights", in_specs=[ANY], out_specs=ANY,
        out_shape=jax.ShapeDtypeStruct((N_CHIPS,) + pack.shape, pack.dtype),
        scratch_shapes=[pltpu.SemaphoreType.DMA((6,)), pltpu.SemaphoreType.DMA((6,)), pltpu.SemaphoreType.DMA],
    )(pack)


def _scatter_grads(grads):
    n = len(grads)

    def body(*refs):
        g_refs, r_refs = refs[:n], refs[n:2 * n]
        send_sems, recv_sems, local_sems = refs[2 * n:]
        x, y, c, chips = _place()
        me = 2 * x + y
        locals_, sends = [], []
        for w in range(n):
            cp = pltpu.make_async_copy(g_refs[w].at[me], r_refs[w].at[me], local_sems.at[w])
            cp.start()
            locals_.append(cp)
            for j, (px, py) in enumerate(chips):
                cp = pltpu.make_async_remote_copy(
                    src_ref=g_refs[w].at[2 * px + py], dst_ref=r_refs[w].at[me],
                    send_sem=send_sems.at[3 * w + j], recv_sem=recv_sems.at[3 * w + j],
                    device_id=(px, py, c), device_id_type=MESH)
                cp.start()
                sends.append(cp)
        for w in range(n):
            for j, (px, py) in enumerate(chips):
                pltpu.make_async_remote_copy(
                    src_ref=g_refs[w].at[me], dst_ref=r_refs[w].at[2 * px + py],
                    send_sem=send_sems.at[3 * w + j], recv_sem=recv_sems.at[3 * w + j],
                    device_id=(px, py, c), device_id_type=MESH).wait_recv()
        for cp in sends:
            cp.wait_send()
        for cp in locals_:
            cp.wait()

    return pl.pallas_call(
        body, name="scatter_grads", in_specs=[ANY] * n, out_specs=[ANY] * n,
        out_shape=[jax.ShapeDtypeStruct(g.shape, g.dtype) for g in grads],
        scratch_shapes=[pltpu.SemaphoreType.DMA((3 * n,)), pltpu.SemaphoreType.DMA((3 * n,)),
                        pltpu.SemaphoreType.DMA((n,))],
    )(*grads)


def _swap_cores(parts):
    n = len(parts)

    def body(*refs):
        p_refs, r_refs = refs[:n], refs[n:2 * n]
        send_sems, recv_sems = refs[2 * n:]
        x, y, c, _ = _place()
        copies = [pltpu.make_async_remote_copy(
            src_ref=p_refs[w], dst_ref=r_refs[w], send_sem=send_sems.at[w], recv_sem=recv_sems.at[w],
            device_id=(x, y, 1 - c), device_id_type=MESH) for w in range(n)]
        for cp in copies:
            cp.start()
        for cp in copies:
            cp.wait()

    return pl.pallas_call(
        body, name="swap_cores", in_specs=[ANY] * n, out_specs=[ANY] * n,
        out_shape=[jax.ShapeDtypeStruct(p.shape, p.dtype) for p in parts],
        scratch_shapes=[pltpu.SemaphoreType.DMA((n,)), pltpu.SemaphoreType.DMA((n,))],
    )(*parts)


HBM = pl.BlockSpec(memory_space=pltpu.HBM)
SEM = pl.BlockSpec(memory_space=pltpu.SEMAPHORE)
EFFECT = pltpu.SideEffectType.DATAFLOW_SIDE_EFFECTING


def _push_copies(a_refs, l_refs, send_sems, recv_sems, by_target):
    x, y, c, chips = _place()
    me = 2 * x + y
    out = []
    for w, (a_ref, l_ref) in enumerate(zip(a_refs, l_refs)):
        for j, (px, py) in enumerate(chips):
            peer = 2 * px + py
            out.append((
                pltpu.make_async_remote_copy(
                    src_ref=a_ref.at[peer] if by_target else a_ref, dst_ref=l_ref.at[me],
                    send_sem=send_sems.at[3 * w + j], recv_sem=recv_sems.at[3 * w + j],
                    device_id=(px, py, c), device_id_type=MESH),
                pltpu.make_async_remote_copy(
                    src_ref=a_ref.at[me] if by_target else a_ref, dst_ref=l_ref.at[peer],
                    send_sem=send_sems.at[3 * w + j], recv_sem=recv_sems.at[3 * w + j],
                    device_id=(px, py, c), device_id_type=MESH)))
    return out


def _push_start(name, arrs, by_target):
    n = len(arrs)
    lands = [lax.empty((N_CHIPS,) + (a.shape[1:] if by_target else a.shape), a.dtype) for a in arrs]

    def body(*refs):
        a_refs, l_refs = refs[:n], refs[n:2 * n]
        send_sems, recv_sems = refs[2 * n], refs[2 * n + 1]
        token = refs[-1]
        for send, _ in _push_copies(a_refs, l_refs, send_sems, recv_sems, by_target):
            send.start()
        token[...] = jnp.zeros_like(token)

    res = pl.pallas_call(
        body, name=name,
        out_shape=(pltpu.SemaphoreType.DMA((3 * n,)), pltpu.SemaphoreType.DMA((3 * n,)),
                   *[pltpu.HBM(a.shape, a.dtype) for a in arrs], *[pltpu.HBM(l.shape, l.dtype) for l in lands],
                   jax.ShapeDtypeStruct((8, LANES), F32)),
        in_specs=[HBM] * (2 * n), out_specs=(SEM, SEM, *[HBM] * (2 * n), pl.BlockSpec(memory_space=pltpu.VMEM)),
        input_output_aliases={i: 2 + i for i in range(2 * n)},
        compiler_params=pltpu.CompilerParams(has_side_effects=EFFECT),
    )(*[pltpu.with_memory_space_constraint(a, pltpu.HBM) for a in list(arrs) + lands])
    return res[0], res[1], list(res[2:2 + n]), list(res[2 + n:2 + 2 * n]), res[-1]


def _push_wait(name, send_sems, recv_sems, arrs, lands, after, by_target):
    n = len(arrs)

    def body(*refs):
        a_refs, l_refs = refs[:n], refs[n:2 * n]
        s_sems, r_sems = refs[2 * n], refs[2 * n + 1]
        for send, recv in _push_copies(a_refs, l_refs, s_sems, r_sems, by_target):
            send.wait_send()
            recv.wait_recv()

    res = pl.pallas_call(
        body, name=name,
        out_shape=[pltpu.HBM(a.shape, a.dtype) for a in list(arrs) + list(lands)],
        in_specs=[HBM] * (2 * n) + [SEM, SEM] + [ANY] * len(after), out_specs=[HBM] * (2 * n),
        input_output_aliases={i: i for i in range(2 * n)},
        compiler_params=pltpu.CompilerParams(has_side_effects=EFFECT),
    )(*arrs, *lands, send_sems, recv_sems, *after)
    return list(res[:n]), list(res[n:])


def _all_reduce_small(part):
    n_dev = 8

    def body(p_ref, out_ref, buf, send_sems, recv_sems):
        x, y, c, _ = _place()
        me = 4 * x + 2 * y + c
        buf[me] = p_ref[...]
        copies = []
        for k in range(1, n_dev):
            kx, ky, kc = (k >> 2) & 1, (k >> 1) & 1, k & 1
            peer = (x ^ kx, y ^ ky, c ^ kc)
            cp = pltpu.make_async_remote_copy(
                src_ref=p_ref, dst_ref=buf.at[me], send_sem=send_sems.at[k - 1], recv_sem=recv_sems.at[k - 1],
                device_id=peer, device_id_type=MESH)
            cp.start()
            copies.append(cp)
        for k in range(1, n_dev):
            pltpu.make_async_remote_copy(
                src_ref=p_ref, dst_ref=buf.at[me ^ k], send_sem=send_sems.at[k - 1], recv_sem=recv_sems.at[k - 1],
                device_id=(x, y, c), device_id_type=MESH).wait_recv()
        for cp in copies:
            cp.wait_send()
        tot = buf[0]
        for d in range(1, n_dev):
            tot = tot + buf[d]
        out_ref[...] = tot

    vm = pl.BlockSpec(memory_space=pltpu.VMEM)
    return pl.pallas_call(
        body, name="all_reduce_small", in_specs=[vm], out_specs=vm,
        out_shape=jax.ShapeDtypeStruct(part.shape, F32),
        scratch_shapes=[pltpu.VMEM((n_dev,) + part.shape, F32), pltpu.SemaphoreType.DMA((n_dev - 1,)),
                        pltpu.SemaphoreType.DMA((n_dev - 1,))],
    )(part)


def _sum_chips(name, r):
    _, rows, cols = r.shape
    tr = _gcd(ADAM_ROWS, rows)

    def body(r_ref, o_ref):
        tot = r_ref[0].astype(F32)
        for q in range(1, N_CHIPS):
            tot = tot + r_ref[q].astype(F32)
        o_ref[...] = tot

    return pl.pallas_call(
        body, name=name, grid=(rows // tr,),
        in_specs=[pl.BlockSpec((N_CHIPS, tr, cols), lambda i: (0, i, 0))],
        out_specs=pl.BlockSpec((tr, cols), lambda i: (i, 0)),
        out_shape=jax.ShapeDtypeStruct((rows, cols), F32), compiler_params=_params(1),
    )(r)


def _adamw(name, g_parts, w, m, v):
    rows, cols = w.shape
    tr = _gcd(ADAM_ROWS, rows)
    n = len(g_parts)

    def body(*refs):
        g = refs[0][...]
        for r in refs[1:n]:
            g = g + r[...]
        w_ref, m_ref, v_ref, go_ref, d_ref, mo_ref, vo_ref = refs[n:]
        mn = ADAM_B1 * m_ref[...] + (1.0 - ADAM_B1) * g
        vn = ADAM_B2 * v_ref[...] + (1.0 - ADAM_B2) * jnp.square(g)
        m_hat = mn / (1.0 - ADAM_B1 ** ADAM_STEP)
        v_hat = vn / (1.0 - ADAM_B2 ** ADAM_STEP)
        go_ref[...] = g
        d_ref[...] = -ADAM_LR * (m_hat / (jnp.sqrt(v_hat) + ADAM_EPS) + ADAM_WD * w_ref[...])
        mo_ref[...] = mn
        vo_ref[...] = vn

    spec = pl.BlockSpec((tr, cols), lambda i: (i, 0))
    return pl.pallas_call(
        body, name=name, grid=(rows // tr,), in_specs=[spec] * (n + 3), out_specs=[spec] * 4,
        out_shape=[jax.ShapeDtypeStruct((rows, cols), F32)] * 4, compiler_params=_params(1),
    )(*g_parts, w, m, v)


def _pack_rows(flat, dtype, multiple):
    n = flat.shape[0]
    total = -(-n // multiple) * multiple
    return jnp.pad(flat, (0, total - n)).astype(dtype).reshape(total // LANES, LANES)


def kernel(x, positions, ln_g, ln_b, a_w_in, a_b_in, a_conv_w, a_conv_b, a_norm_g, a_norm_b, a_w_out, a_b_out, kv_w_down, kv_norm_g, kv_w_uk, kv_w_uv, b_w_in, b_q_norm_g, b_w_uq, b_w_out, loss_target, m_ln_g, m_ln_b, m_a_w_in, m_a_b_in, m_a_conv_w, m_a_conv_b, m_a_norm_g, m_a_norm_b, m_a_w_out, m_a_b_out, m_kv_w_down, m_kv_norm_g, m_kv_w_uk, m_kv_w_uv, m_b_w_in, m_b_q_norm_g, m_b_w_uq, m_b_w_out, v_ln_g, v_ln_b, v_a_w_in, v_a_b_in, v_a_conv_w, v_a_conv_b, v_a_norm_g, v_a_norm_b, v_a_w_out, v_a_b_out, v_kv_w_down, v_kv_norm_g, v_kv_w_uk, v_kv_w_uv, v_b_w_in, v_b_q_norm_g, v_b_w_uq, v_b_w_out):
    T, D = x.shape[1], x.shape[2]
    E = N_CHIPS * a_w_out.shape[1]
    KC = a_conv_w.shape[1]
    RKV = kv_norm_g.shape[0]
    H, DN = kv_w_uk.shape[1], kv_w_uk.shape[2]
    RQ = b_q_norm_g.shape[1]
    HV = N_CHIPS * b_w_out.shape[1]
    assert DN == LANES and kv_w_uv.shape[2] == LANES and HV == H * LANES
    assert kv_w_down.shape[1] == RKV + ROPE_DIM and b_w_uq.shape[3] == DN + ROPE_DIM
    assert ln_g.shape[0] == 2 and a_w_in.shape[0] == 1 and b_w_in.shape[0] == 1
    alpha = (2.0 * ln_g.shape[0]) ** 0.25
    scale = 1.0 / math.sqrt(DN + ROPE_DIM)
    WK = -(-(RKV + LANES) // 256) * 256
    assert WK % RQ == 0
    Z_OFF = WK + RQ
    tmw, tq = min(TM_WIDE, T), min(TQ, T)
    t512, t1024 = _fit(512, T), _fit(1024, T)
    xs = x[0]
    tgt = loss_target[0]
    px, py = lax.axis_index("x"), lax.axis_index("y")
    chip = 2 * px + py

    mats = [a_w_out[0], kv_w_down, kv_w_uk, kv_w_uv, b_w_in[0], b_w_uq[0], b_w_out[0]]
    vecs = [a_b_in[0], a_conv_w[0], a_conv_b[0], a_norm_g[0], a_norm_b[0], a_b_out[0]]
    first = jnp.concatenate(
        [a_w_in[0].astype(BF16).reshape(-1)]
        + [lax.bitcast_convert_type(w.reshape(-1), BF16).reshape(-1) for w in vecs])
    rest = _pack_rows(jnp.concatenate([w.astype(BF16).reshape(-1) for w in mats]), BF16, BF16_ROWS * LANES)
    gathered = _all_gather_chips(_pack_rows(first, BF16, 2 * BF16_ROWS * LANES)).reshape(N_CHIPS, -1)
    gathered, rest = lax.optimization_barrier((gathered, rest))
    rest_sems = _push_start("gather_rest_start", [rest], by_target=False)
    g_win = gathered[:, :a_w_in[0].size].reshape((N_CHIPS,) + a_w_in[0].shape)
    off = a_w_in[0].size
    fvec = []
    for w in vecs:
        bits = gathered[:, off:off + 2 * w.size].reshape((N_CHIPS,) + w.shape + (2,))
        fvec.append(lax.bitcast_convert_type(bits, F32))
        off += 2 * w.size
    cols = lambda g: jnp.moveaxis(g, 0, -2).reshape(g.shape[1:-1] + (N_CHIPS * g.shape[-1],))
    w_in = cols(g_win)
    b_in = cols(fvec[0][:, None, :])
    conv_w = cols(fvec[1])
    conv_b, norm_g, norm_b, b_out = (cols(f[:, None, :]) for f in fvec[2:])
    row = lambda a: a.reshape(1, -1)
    g0, b0, g1, b1 = row(ln_g[0]), row(ln_b[0]), row(ln_g[1]), row(ln_b[1])
    kv_g, q_g = row(kv_norm_g), row(b_q_norm_g[0])
    plain = lambda acc, ins, i, j: [acc]

    xb = (xs + rest_sems[4][0, 0]).astype(BF16)
    (proj,) = _row_mm("a_in", [((xb,), None)], w_in, nt=False, tm=t1024, tn=_fit(1536, 3 * E), tk=_fit(1024, D),
                      outs=[((T, 3 * E), F32, 'tile')], epi=lambda acc, ins, i, j: [acc + ins[0]],
                      epi_ins=[(b_in, 'col')])
    u1 = _conv_fwd(proj, conv_w, conv_b, E, tmw)
    u4 = _conv_post(u1, proj, norm_g, norm_b, E, tmw)

    (rest,), (landed,) = _push_wait("gather_rest_wait", *rest_sems[:4], after=[u4], by_target=False)
    landed = lax.dynamic_update_slice(landed, rest[None], (chip, 0, 0)).reshape(N_CHIPS, -1)
    off = 0
    full = []
    for w in mats:
        full.append(landed[:, off:off + w.size].reshape((N_CHIPS,) + w.shape))
        off += w.size
    g_wout, g_wd, g_uk, g_uv, g_wbin, g_wuq, g_wbout = full
    w_out = g_wout.reshape(E, D)
    wd = g_wd.reshape(D, RKV + ROPE_DIM)
    zpad = jnp.zeros((D, ROPE_HALF), BF16)
    wd_p = jnp.concatenate(
        [wd[:, :RKV], wd[:, RKV:RKV + ROPE_HALF], zpad, wd[:, RKV + ROPE_HALF:], zpad,
         jnp.zeros((D, WK - RKV - LANES), BF16)], axis=1)
    w_bin = cols(g_wbin)
    w_z = w_bin[:, RQ:]
    wb_small = jnp.concatenate([wd_p, w_bin[:, :RQ]], axis=1)
    wb_all = jnp.concatenate([wd_p, w_bin], axis=1)
    w_kv = jnp.concatenate([g_uk.reshape(RKV, HV), g_uv.reshape(RKV, HV)], axis=1)
    wuq = g_wuq.reshape(RQ, H, DN + ROPE_DIM)
    zq = jnp.zeros((RQ, H, ROPE_HALF), BF16)
    w_qr = jnp.concatenate([wuq[:, :, DN:DN + ROPE_HALF], zq, wuq[:, :, DN + ROPE_HALF:], zq], axis=2)
    w_q = jnp.concatenate([wuq[:, :, :DN].reshape(RQ, HV), w_qr.reshape(RQ, HV)], axis=1)
    w_bout = g_wbout.reshape(HV, D)

    freqs = ROPE_THETA ** (-jnp.arange(0, ROPE_DIM, 2, dtype=F32) / ROPE_DIM)
    ang = positions[0].astype(F32)[:, None] * freqs
    cs, sn = jnp.cos(ang), jnp.sin(ang)
    ones, zeros = jnp.ones_like(cs), jnp.zeros_like(cs)
    cos_t = jnp.concatenate([cs, ones, cs, ones], axis=1)
    sin_t = jnp.concatenate([-sn, zeros, sn, zeros], axis=1)

    def ln_epi(acc, ins, i, j):
        bias, res, g, b = ins
        xhat, rstd = _ln_stats(alpha * res + acc + bias)
        h = xhat * g + b
        return [h, h, xhat, rstd]

    h1, h1b, xhat1, rstd1 = _row_mm(
        "a_out", [((u4,), None)], w_out, nt=False, tm=t512, tn=D, tk=_fit(2048, E),
        outs=[((T, D), F32, 'tile'), ((T, D), BF16, 'tile'), ((T, D), F32, 'tile'), ((T, 1), F32, 'row')],
        epi=ln_epi, epi_ins=[(b_out, 'col'), (xs, 'tile'), (g0, 'col'), (b0, 'col')])

    tkb = _fit(512, _gcd(WK, RQ, HV))
    (pb,) = _row_mm("b_in", [((h1b,), None)], wb_small, nt=False, tm=t1024, tn=_fit(1024, Z_OFF),
                    tk=_fit(1024, D), outs=[((T, Z_OFF), F32, 'tile')], epi=plain)
    (zb,) = _row_mm("b_in_gate", [((h1b,), None)], w_z, nt=False, tm=t1024, tn=_fit(2048, HV),
                    tk=_fit(1024, D), outs=[((T, HV), F32, 'tile')], epi=plain)
    c_lat, kr, cqn = _norm_prep(pb, kv_g, q_g, cos_t, sin_t, RKV, RQ, WK, tmw)
    (kv,) = _row_mm("kv_up", [((c_lat,), None)], w_kv, nt=False, tm=t1024, tn=_fit(2048, HV),
                    tk=_fit(1024, RKV), outs=[((T, 2 * HV), BF16, 'tile')], epi=plain)
    tnq = _fit(2048, HV)
    half_q = HV // tnq

    def q_epi(acc, ins, i, j):
        return [jnp.where(j >= half_q, _rope(acc, ins[0], ins[1]), acc)]

    (q_all,) = _row_mm("q_up", [((cqn,), None)], w_q, nt=False, tm=t1024, tn=tnq, tk=_fit(1024, RQ),
                       outs=[((T, 2 * HV), BF16, 'tile')], epi=q_epi,
                       epi_ins=[(cos_t, 'row'), (sin_t, 'row')])
    o, lse = _attn_fwd(q_all, kv, kr, H, tq, scale)

    def loss_epi(acc, ins, i, j):
        res, g, b, target = ins
        xhat, rstd = _ln_stats(alpha * res + acc)
        diff = xhat * g + b - target
        dr, dg, db = _ln_bwd(diff / D, xhat, rstd, g)
        return [dr, 0.5 * jnp.sum(diff * diff, keepdims=True) / D, dg, db]

    dr1, loss_part, dg1, db1 = _row_mm(
        "b_out", [((o, zb), _gate)], w_bout, nt=False, tm=tmw, tn=D, tk=_fit(2048, HV),
        outs=[((T, D), F32, 'tile'), ((1, 1), F32, 'acc'), ((1, D), F32, 'acc'), ((1, D), F32, 'acc')],
        epi=loss_epi, epi_ins=[(h1, 'tile'), (g1, 'col'), (b1, 'col'), (tgt, 'tile')])

    def gate_bwd_epi(acc, ins, i, j):
        return [acc * _silu(ins[1]), acc * ins[0] * _silu_grad(ins[1])]

    do, dz = _row_mm(
        "b_out_bwd", [((dr1,), None)], w_bout, nt=True, tm=tmw, tn=_fit(2048, HV), tk=_fit(1024, D),
        outs=[((T, HV), BF16, 'tile'), ((T, HV), BF16, 'tile')], epi=gate_bwd_epi,
        epi_ins=[(o, 'tile'), (zb, 'tile')])
    gw_bout = _tn_mm("dw_b_out", (o, zb), _gate, [((dr1,), None)], tn=_fit(1024, D), tk=t512, out_dtype=BF16)
    dqn, dqr, dkn, dkr_h, dv = _attn_bwd(q_all, kv, kr, do, o, lse, H, tq, scale)
    dqr_pre, dkr_pre = _rope_bwd(dqr, dkr_h, cos_t, sin_t, H, tmw)

    def cq_bwd_epi(acc, ins, i, j):
        dx, dg = _rms_bwd(acc, ins[0], ins[1])
        return [dx, dg]

    dcq, dqg = _row_mm(
        "q_up_bwd", [((dqn,), None), ((dqr_pre,), None)], w_q, nt=True, tm=tmw, tn=RQ, tk=_fit(2048, HV),
        outs=[((T, RQ), BF16, 'tile'), ((1, RQ), F32, 'acc')], epi=cq_bwd_epi,
        epi_ins=[(pb, pl.BlockSpec((tmw, RQ), lambda i, j, k: (i, WK // RQ))), (q_g, 'col')])
    gw_q = _tn_mm("dw_q_up", (cqn,), None, [((dqn,), None), ((dqr_pre,), None)],
                  tn=_fit(2048, HV), tk=t1024, out_dtype=BF16)

    def ckv_bwd_epi(acc, ins, i, j):
        blk, dkr_t, g = ins
        dx, dg = _rms_bwd(acc, blk[:, :RKV], g)
        parts = [dx, dkr_t]
        if WK > RKV + LANES:
            parts.append(jnp.zeros((dx.shape[0], WK - RKV - LANES), F32))
        return [jnp.concatenate(parts, axis=1), dg]

    dckv, dkvg = _row_mm(
        "kv_up_bwd", [((dkn,), None), ((dv,), None)], w_kv, nt=True, tm=tmw, tn=RKV, tk=_fit(2048, HV),
        outs=[((T, WK), BF16, pl.BlockSpec((tmw, WK), lambda i, j, k: (i, 0))), ((1, RKV), F32, 'acc')],
        epi=ckv_bwd_epi,
        epi_ins=[(pb, pl.BlockSpec((tmw, WK), lambda i, j, k: (i, 0))), (dkr_pre, 'row'), (kv_g, 'col')])
    gw_kv = _tn_mm("dw_kv_up", (c_lat,), None, [((dkn,), None), ((dv,), None)],
                   tn=_fit(2048, HV), tk=t1024, out_dtype=BF16)

    def ln1_bwd_epi(acc, ins, i, j):
        dr_up, xhat, rstd, g = ins
        dr, dg, db = _ln_bwd(alpha * dr_up + acc, xhat, rstd, g)
        return [dr, dg, db]

    dp_segs = [((dckv,), None), ((dcq,), None), ((dz,), None)]
    gw_ball = _tn_mm("dw_b_in", (h1b,), None, dp_segs, tn=tkb, tk=t1024, out_dtype=BF16)

    shard_cols = lambda g: jnp.moveaxis(g.reshape(g.shape[0], N_CHIPS, -1), 1, 0)
    shard_rows = lambda g: g.reshape(N_CHIPS, g.shape[0] // N_CHIPS, g.shape[1])
    gq = gw_q.reshape(RQ, 2, H, LANES)
    g_uq = jnp.concatenate(
        [gq[:, 0], gq[:, 1, :, :ROPE_HALF], gq[:, 1, :, 2 * ROPE_HALF:3 * ROPE_HALF]], axis=2)
    g_wd_full = jnp.concatenate(
        [gw_ball[:, :RKV], gw_ball[:, RKV:RKV + ROPE_HALF],
         gw_ball[:, RKV + 2 * ROPE_HALF:RKV + 3 * ROPE_HALF]], axis=1)
    late_names = ["kv_w_down", "kv_w_uk", "kv_w_uv", "b_w_in", "b_w_uq", "b_w_out"]
    late_grads = [
        shard_rows(g_wd_full),
        shard_rows(gw_kv[:, :HV]),
        shard_rows(gw_kv[:, HV:]),
        shard_cols(gw_ball[:, WK:]),
        shard_rows(g_uq.reshape(RQ, H * (DN + ROPE_DIM))),
        shard_rows(gw_bout),
    ]
    late_sems = _push_start("scatter_late_start", late_grads, by_target=True)

    dr0, dg0, db0 = _row_mm(
        "b_in_bwd", dp_segs, wb_all, nt=True, tm=t512, tn=D, tk=tkb,
        outs=[((T, D), F32, 'tile'), ((1, D), F32, 'acc'), ((1, D), F32, 'acc')], epi=ln1_bwd_epi,
        epi_ins=[(dr1, 'tile'), (xhat1, 'tile'), (rstd1, 'row'), (g0 + late_sems[4][0, 0], 'col')])

    def conv_branch_bwd_epi(acc, ins, i, j):
        u1_t, z, g, b = ins
        xhat, rstd = _ln_stats(u1_t)
        u2 = xhat * g + b
        du3 = acc * _silu(z)
        dz_a = acc * _silu(u2) * _silu_grad(z)
        du1, dg, db = _ln_bwd(du3 * _silu_grad(u2), xhat, rstd, g)
        return [du1, dz_a, dg, db]

    du1, dz_a, dng, dnb = _row_mm(
        "a_out_bwd", [((dr0,), None)], w_out, nt=True, tm=tmw, tn=E, tk=_fit(1024, D),
        outs=[((T, E), F32, 'tile'), ((T, E), BF16, 'tile'), ((1, E), F32, 'acc'), ((1, E), F32, 'acc')],
        epi=conv_branch_bwd_epi,
        epi_ins=[(u1, 'tile'), (proj, pl.BlockSpec((tmw, E), lambda i, j, k: (i, 2))), (norm_g, 'col'),
                 (norm_b, 'col')])
    gw_out, dbo = _tn_mm("dw_a_out", (u4,), None, [((dr0,), None)], tn=_fit(1024, D), tk=t1024,
                         out_dtype=BF16, colsum=True)
    dval, dgate, dcw, dcb = _conv_bwd(du1, proj, conv_w, E, tmw)
    dproj_segs = [((dval,), None), ((dgate,), None), ((dz_a,), None)]
    gw_in, dbi = _tn_mm("dw_a_in", (xb,), None, dproj_segs, tn=_fit(2048, E), tk=t512, out_dtype=BF16,
                        colsum=True)

    def own_block_in_place(sent, landed):
        own = lax.dynamic_index_in_dim(sent, chip, 0, keepdims=True)
        return lax.dynamic_update_slice(landed, own, (chip,) + (0,) * (sent.ndim - 1))

    late_sent, late_landed = _push_wait("scatter_late_wait", *late_sems[:4], after=[dbi], by_target=True)
    early_names = ["a_w_in", "a_w_out"]
    early_grads = [shard_cols(gw_in), shard_rows(gw_out)]
    early_sems = _push_start("scatter_early_start", early_grads, by_target=True)
    (grad_x,) = _row_mm(
        "a_in_bwd", dproj_segs, w_in, nt=True, tm=t512, tn=_fit(1024, D), tk=_fit(2048, E),
        outs=[((T, D), F32, 'tile')], epi=lambda acc, ins, i, j: [alpha * ins[0] + acc + ins[1]],
        epi_ins=[(dr0, 'tile'), (jnp.zeros((1, D), F32) + early_sems[4][0, 0], 'col')])
    late_sums = [_sum_chips("sum_" + n, own_block_in_place(s, l))
                 for n, s, l in zip(late_names, late_sent, late_landed)]
    early_sent, early_landed = _push_wait("scatter_early_wait", *early_sems[:4],
                                          after=[grad_x, late_sums[-1]], by_target=True)
    early_sums = [_sum_chips("sum_" + n, own_block_in_place(s, l))
                  for n, s, l in zip(early_names, early_sent, early_landed)]
    names = early_names + late_names
    core_sums = early_sums + late_sums
    sibling_sums = _swap_cores(core_sums)
    big_w = [a_w_in, a_w_out, kv_w_down, kv_w_uk, kv_w_uv, b_w_in, b_w_uq, b_w_out]
    big_m = [m_a_w_in, m_a_w_out, m_kv_w_down, m_kv_w_uk, m_kv_w_uv, m_b_w_in, m_b_w_uq, m_b_w_out]
    big_v = [v_a_w_in, v_a_w_out, v_kv_w_down, v_kv_w_uk, v_kv_w_uv, v_b_w_in, v_b_w_uq, v_b_w_out]
    big_out = {}
    for n, mine, theirs, w, m, v in zip(names, core_sums, sibling_sums, big_w, big_m, big_v):
        res = _adamw("adamw_" + n, [mine, theirs], *(a.reshape(mine.shape) for a in (w, m, v)))
        big_out[n] = [r.reshape(w.shape) for r in res]

    small_full = [jnp.concatenate([dg0, dg1]), jnp.concatenate([db0, db1]), dbi, dcw, dcb, dng, dnb, dbo,
                  dkvg, dqg]
    sflat = jnp.concatenate([g.reshape(-1) for g in small_full])
    summed = _all_reduce_small(_pack_rows(sflat, F32, 8 * LANES)).reshape(-1)
    soff = 0
    sgrads = []
    for g in small_full:
        sgrads.append(summed[soff:soff + g.size].reshape(g.shape))
        soff += g.size
    local_cols = lambda g, n: lax.dynamic_slice_in_dim(g, chip * n, n, axis=g.ndim - 1)
    snames = ["ln_g", "ln_b", "a_b_in", "a_conv_w", "a_conv_b", "a_norm_g", "a_norm_b", "a_b_out",
              "kv_norm_g", "b_q_norm_g"]
    small_w = [ln_g, ln_b, a_b_in, a_conv_w, a_conv_b, a_norm_g, a_norm_b, a_b_out, kv_norm_g, b_q_norm_g]
    small_m = [m_ln_g, m_ln_b, m_a_b_in, m_a_conv_w, m_a_conv_b, m_a_norm_g, m_a_norm_b, m_a_b_out,
               m_kv_norm_g, m_b_q_norm_g]
    small_v = [v_ln_g, v_ln_b, v_a_b_in, v_a_conv_w, v_a_conv_b, v_a_norm_g, v_a_norm_b, v_a_b_out,
               v_kv_norm_g, v_b_q_norm_g]
    sharded = {"a_b_in", "a_conv_w", "a_conv_b", "a_norm_g", "a_norm_b", "a_b_out"}
    local_g = [(local_cols(g, w.shape[-1]) if n in sharded else g).reshape(w.shape)
               for n, g, w in zip(snames, sgrads, small_w)]
    pack_small = lambda arrs: _pack_rows(jnp.concatenate([a.reshape(-1) for a in arrs]), F32, 8 * LANES)
    sres = _adamw("adamw_vectors", [pack_small(local_g)], pack_small(small_w), pack_small(small_m),
                  pack_small(small_v))
    small_out = {}
    soff = 0
    for n, w in zip(snames, small_w):
        small_out[n] = [r.reshape(-1)[soff:soff + w.size].reshape(w.shape) for r in sres]
        soff += w.size

    loss = lax.psum(loss_part[0, 0], ("x", "y", "c"))
    order = ["ln_g", "ln_b", "a_w_in", "a_b_in", "a_conv_w", "a_conv_b", "a_norm_g", "a_norm_b", "a_w_out",
             "a_b_out", "kv_w_down", "kv_norm_g", "kv_w_uk", "kv_w_uv", "b_w_in", "b_q_norm_g", "b_w_uq",
             "b_w_out"]
    outs = {**big_out, **small_out}
    result = [loss, grad_x[None]]
    for part in range(4):
        result += [outs[n][part] for n in order]
    return tuple(result)
```

```python
import functools
import math

import jax
import jax.numpy as jnp
from jax import lax
from jax.experimental import pallas as pl
from jax.experimental.pallas import tpu as pltpu

F32, BF16 = jnp.float32, jnp.bfloat16
NN = (((1,), (0,)), ((), ()))
NT = (((1,), (1,)), ((), ()))
TN = (((0,), (0,)), ((), ()))
MESH = pl.DeviceIdType.MESH
ANY = pl.BlockSpec(memory_space=pl.ANY)

LANES = 128
BF16_ROWS = 16
VMEM_LIMIT = 56 * 1024 * 1024
N_CHIPS = 4
LN_EPS = 1e-5
RMS_EPS = 1e-6
MASK_VALUE = -1e30
ROPE_THETA = 10000.0
ROPE_DIM = 64
ROPE_HALF = ROPE_DIM // 2
ADAM_LR, ADAM_B1, ADAM_B2, ADAM_EPS, ADAM_WD, ADAM_STEP = 0.001, 0.9, 0.999, 1e-08, 0.01, 10

MAX_TILE = 2048
TM_WIDE = 256
TQ = 512
CONV_HALO = 32
CONV_LC = 512
CONV_SUB = 256
SUBLANES = 8
CONV_RC = 32
ADAM_ROWS = 64


def _dot(a, b, dims):
    return lax.dot_general(a.astype(BF16), b.astype(BF16), dims, preferred_element_type=F32)


def _sig(x):
    return 1.0 / (1.0 + jnp.exp(-x))


def _params(n_axes):
    return pltpu.CompilerParams(dimension_semantics=("arbitrary",) * n_axes, vmem_limit_bytes=VMEM_LIMIT)


def _gcd(*v):
    return functools.reduce(math.gcd, v)


def _fit(want, dim):
    return math.gcd(min(want, MAX_TILE), dim)


def _row_mm(name, a_segs, b, *, nt, tm, tn, tk, outs, epi, epi_ins=()):
    M = a_segs[0][0][0].shape[0]
    N = b.shape[0] if nt else b.shape[1]
    nkb = [arrs[0].shape[1] // tk for arrs, _ in a_segs]
    koff = [sum(nkb[:s]) for s in range(len(nkb))]
    ni, nj, nk = M // tm, N // tn, sum(nkb)
    assert M % tm == 0 and N % tn == 0 and all(arrs[0].shape[1] % tk == 0 for arrs, _ in a_segs), name
    assert (b.shape[1] if nt else b.shape[0]) == nk * tk, name

    def spec_of(shape, kind):
        if isinstance(kind, pl.BlockSpec):
            return kind
        if kind == 'tile':
            return pl.BlockSpec((tm, tn), lambda i, j, k: (i, j))
        if kind == 'row':
            return pl.BlockSpec((tm, shape[1]), lambda i, j, k: (i, 0))
        if kind == 'col':
            return pl.BlockSpec((1, tn), lambda i, j, k: (0, j))
        assert kind == 'acc' and nj == 1, name
        return pl.BlockSpec(shape, lambda i, j, k: (0,) * len(shape))

    in_specs, operands = [], []
    for s, (arrs, _) in enumerate(a_segs):
        for arr in arrs:
            in_specs.append(pl.BlockSpec(
                (tm, tk), lambda i, j, k, s=s: (i, jnp.clip(k - koff[s], 0, nkb[s] - 1))))
            operands.append(arr)
    in_specs.append(pl.BlockSpec((tn, tk), lambda i, j, k: (j, k)) if nt
                    else pl.BlockSpec((tk, tn), lambda i, j, k: (k, j)))
    operands.append(b)
    for arr, kind in epi_ins:
        in_specs.append(spec_of(arr.shape, kind))
        operands.append(arr)
    out_specs = [spec_of(shape, kind) for shape, _, kind in outs]
    out_shape = [jax.ShapeDtypeStruct(shape, dtype) for shape, dtype, _ in outs]
    n_seg_refs = [len(arrs) for arrs, _ in a_segs]

    def body(*refs):
        pos = 0
        seg_refs = []
        for n in n_seg_refs:
            seg_refs.append(refs[pos:pos + n])
            pos += n
        b_ref = refs[pos]
        e_refs = refs[pos + 1:pos + 1 + len(epi_ins)]
        o_refs = refs[pos + 1 + len(epi_ins):pos + 1 + len(epi_ins) + len(outs)]
        i, j, k = pl.program_id(0), pl.program_id(1), pl.program_id(2)

        def product(fn, rs):
            a = rs[0][...] if fn is None else fn(*[r[...] for r in rs])
            return _dot(a, b_ref[...], NT if nt else NN)

        def finish(acc):
            res = epi(acc, [r[...] for r in e_refs], i, j)
            for o_ref, (_, _, kind), r in zip(o_refs, outs, res):
                if isinstance(kind, str) and kind == 'acc':
                    @pl.when(i == 0)
                    def _(o_ref=o_ref, r=r):
                        o_ref[...] = r

                    @pl.when(i > 0)
                    def _(o_ref=o_ref, r=r):
                        o_ref[...] += r
                else:
                    o_ref[...] = r.astype(o_ref.dtype)

        if nk == 1:
            finish(product(a_segs[0][1], seg_refs[0]))
            return
        acc_ref = refs[-1]

        @pl.when(k == 0)
        def _():
            acc_ref[...] = jnp.zeros_like(acc_ref)

        for s, ((_, fn), rs) in enumerate(zip(a_segs, seg_refs)):
            def accumulate(fn=fn, rs=rs):
                acc_ref[...] += product(fn, rs)
            if len(a_segs) == 1:
                accumulate()
            else:
                pl.when(jnp.logical_and(k >= koff[s], k < koff[s] + nkb[s]))(accumulate)

        @pl.when(k == nk - 1)
        def _():
            finish(acc_ref[...])

    return pl.pallas_call(
        body, name=name, grid=(ni, nj, nk), in_specs=in_specs, out_specs=out_specs, out_shape=out_shape,
        scratch_shapes=[] if nk == 1 else [pltpu.VMEM((tm, tn), F32)], compiler_params=_params(3),
    )(*operands)


def _tn_mm(name, a_arrs, a_fn, b_segs, *, tn, tk, out_dtype, shard_major=False, colsum=False):
    T, M = a_arrs[0].shape
    nbj = [arrs[0].shape[1] // tn for arrs, _ in b_segs]
    joff = [sum(nbj[:s]) for s in range(len(nbj))]
    nj, nk = sum(nbj), T // tk
    N = nj * tn
    assert T % tk == 0 and all(arrs[0].shape[1] % tn == 0 for arrs, _ in b_segs), name

    in_specs = [pl.BlockSpec((tk, M), lambda j, k: (k, 0)) for _ in a_arrs]
    operands = list(a_arrs)
    for s, (arrs, _) in enumerate(b_segs):
        for arr in arrs:
            in_specs.append(pl.BlockSpec(
                (tk, tn), lambda j, k, s=s: (k, jnp.clip(j - joff[s], 0, nbj[s] - 1))))
            operands.append(arr)
    if shard_major:
        per = (N // N_CHIPS) // tn
        assert per * tn * N_CHIPS == N, name
        out_shape = [jax.ShapeDtypeStruct((N_CHIPS, M, N // N_CHIPS), out_dtype)]
        out_specs = [pl.BlockSpec((1, M, tn), lambda j, k: (j // per, 0, j % per))]
    else:
        out_shape = [jax.ShapeDtypeStruct((M, N), out_dtype)]
        out_specs = [pl.BlockSpec((M, tn), lambda j, k: (0, j))]
    if colsum:
        out_shape.append(jax.ShapeDtypeStruct((1, N), F32))
        out_specs.append(pl.BlockSpec((1, tn), lambda j, k: (0, j)))
    n_a = len(a_arrs)
    n_seg_refs = [len(arrs) for arrs, _ in b_segs]

    def body(*refs):
        a_refs = refs[:n_a]
        pos = n_a
        seg_refs = []
        for n in n_seg_refs:
            seg_refs.append(refs[pos:pos + n])
            pos += n
        o_ref = refs[pos]
        cs_ref = refs[pos + 1] if colsum else None
        acc_ref = refs[-1]
        j, k = pl.program_id(0), pl.program_id(1)

        @pl.when(k == 0)
        def _():
            acc_ref[...] = jnp.zeros_like(acc_ref)
            if colsum:
                cs_ref[...] = jnp.zeros_like(cs_ref)

        for s, ((_, fn), rs) in enumerate(zip(b_segs, seg_refs)):
            def accumulate(fn=fn, rs=rs):
                a = a_refs[0][...] if a_fn is None else a_fn(*[r[...] for r in a_refs])
                bt = rs[0][...] if fn is None else fn(*[r[...] for r in rs])
                acc_ref[...] += _dot(a, bt, TN)
                if colsum:
                    cs_ref[...] += jnp.sum(bt.astype(F32), axis=0, keepdims=True)
            if len(b_segs) == 1:
                accumulate()
            else:
                pl.when(jnp.logical_and(j >= joff[s], j < joff[s] + nbj[s]))(accumulate)

        @pl.when(k == nk - 1)
        def _():
            if shard_major:
                o_ref[0] = acc_ref[...].astype(o_ref.dtype)
            else:
                o_ref[...] = acc_ref[...].astype(o_ref.dtype)

    res = pl.pallas_call(
        body, name=name, grid=(nj, nk), in_specs=in_specs, out_specs=out_specs, out_shape=out_shape,
        scratch_shapes=[pltpu.VMEM((M, tn), F32)], compiler_params=_params(2),
    )(*operands)
    return res if colsum else res[0]


def _silu(z):
    return z * _sig(z)


def _silu_grad(z):
    s = _sig(z)
    return s * (1.0 + z * (1.0 - s))


def _gate(o, z):
    return o * _silu(z)


def _ln_stats(r):
    mu = jnp.mean(r, axis=1, keepdims=True)
    xc = r - mu
    var = jnp.mean(xc * xc, axis=1, keepdims=True)
    rstd = lax.rsqrt(var + LN_EPS)
    return xc * rstd, rstd


def _ln_bwd(dy, xhat, rstd, g):
    dxh = dy * g
    m1 = jnp.mean(dxh, axis=1, keepdims=True)
    m2 = jnp.mean(dxh * xhat, axis=1, keepdims=True)
    return (rstd * (dxh - m1 - xhat * m2), jnp.sum(dy * xhat, axis=0, keepdims=True),
            jnp.sum(dy, axis=0, keepdims=True))


def _rms_fwd(x, g):
    rstd = lax.rsqrt(jnp.mean(x * x, axis=1, keepdims=True) + RMS_EPS)
    return x * rstd * g


def _rms_bwd(dy, x, g):
    rstd = lax.rsqrt(jnp.mean(x * x, axis=1, keepdims=True) + RMS_EPS)
    xn = x * rstd
    dxn = dy * g
    return rstd * (dxn - xn * jnp.mean(dxn * xn, axis=1, keepdims=True)), jnp.sum(dy * xn, axis=0, keepdims=True)


def _rope(x, cos, sin, transpose=False):
    parts = []
    for g in range(x.shape[1] // LANES):
        xg = x[:, g * LANES:(g + 1) * LANES]
        if transpose:
            parts.append(xg * cos + pltpu.roll(xg * sin, LANES // 2, 1))
        else:
            parts.append(xg * cos + pltpu.roll(xg, LANES // 2, 1) * sin)
    return parts[0] if len(parts) == 1 else jnp.concatenate(parts, axis=1)


def _shifted_rows(window, rc):
    n = window.shape[0]
    for b in range(SUBLANES):
        rolled = window if b == 0 else pltpu.roll(window, n - b, 0)
        for a8 in range(0, n - rc - b + 1, SUBLANES):
            yield a8 + b, rolled[a8:a8 + rc]


def _conv_fwd(proj, conv_w, conv_b, E, tm):
    T = proj.shape[0]
    kc = conv_w.shape[0]
    lc, hb, rc = min(CONV_LC, E), CONV_HALO, min(CONV_RC, tm)
    nl, ni, ratio = E // lc, T // tm, tm // hb
    gate_off = E // lc

    sub = min(CONV_SUB, lc)
    base = hb - (kc - 1)

    def body(val_ref, gate_ref, valh_ref, gateh_ref, w_ref, cb_ref, u1_ref, ubuf):
        i = pl.program_id(1)
        ubuf[hb:, :] = val_ref[...] * _sig(gate_ref[...])
        halo = valh_ref[...] * _sig(gateh_ref[...])
        ubuf[0:hb, :] = jnp.where(i > 0, halo, 0.0)
        for l0 in range(0, lc, sub):
            ls = slice(l0, l0 + sub)
            for r0 in range(0, tm, rc):
                acc = jnp.zeros((rc, sub), F32) + cb_ref[:, ls]
                for off, rows in _shifted_rows(ubuf[r0:r0 + hb + rc, ls], rc):
                    if 0 <= off - base < kc:
                        acc += w_ref[off - base:off - base + 1, ls] * rows
                u1_ref[r0:r0 + rc, ls] = acc

    return pl.pallas_call(
        body, name="conv_fwd", grid=(nl, ni),
        in_specs=[
            pl.BlockSpec((tm, lc), lambda l, i: (i, l)),
            pl.BlockSpec((tm, lc), lambda l, i: (i, gate_off + l)),
            pl.BlockSpec((hb, lc), lambda l, i: (jnp.maximum(i * ratio - 1, 0), l)),
            pl.BlockSpec((hb, lc), lambda l, i: (jnp.maximum(i * ratio - 1, 0), gate_off + l)),
            pl.BlockSpec((kc, lc), lambda l, i: (0, l)),
            pl.BlockSpec((1, lc), lambda l, i: (0, l)),
        ],
        out_specs=pl.BlockSpec((tm, lc), lambda l, i: (i, l)),
        out_shape=jax.ShapeDtypeStruct((T, E), F32),
        scratch_shapes=[pltpu.VMEM((hb + tm, lc), F32)], compiler_params=_params(2),
    )(proj, proj, proj, proj, conv_w, conv_b)


def _conv_post(u1, proj, norm_g, norm_b, E, tm):
    T = u1.shape[0]

    def body(u1_ref, z_ref, g_ref, b_ref, u4_ref):
        xhat, _ = _ln_stats(u1_ref[...])
        u4_ref[...] = (_silu(xhat * g_ref[...] + b_ref[...]) * _silu(z_ref[...])).astype(BF16)

    return pl.pallas_call(
        body, name="conv_post", grid=(T // tm,),
        in_specs=[pl.BlockSpec((tm, E), lambda i: (i, 0)), pl.BlockSpec((tm, E), lambda i: (i, 2)),
                  pl.BlockSpec((1, E), lambda i: (0, 0)), pl.BlockSpec((1, E), lambda i: (0, 0))],
        out_specs=pl.BlockSpec((tm, E), lambda i: (i, 0)),
        out_shape=jax.ShapeDtypeStruct((T, E), BF16), compiler_params=_params(1),
    )(u1, proj, norm_g, norm_b)


def _conv_bwd(du1, proj, conv_w, E, tm):
    T = du1.shape[0]
    kc = conv_w.shape[0]
    lc, hb, rc = min(CONV_LC, E), CONV_HALO, min(CONV_RC, tm)
    nl, ni, ratio = E // lc, T // tm, tm // hb
    gate_off = E // lc
    last_halo = T // hb - 1

    sub = min(CONV_SUB, lc)
    base = hb - (kc - 1)

    def body(du_ref, dun_ref, val_ref, gate_ref, valh_ref, gateh_ref, w_ref,
             dval_ref, dgate_ref, dw_ref, db_ref, ubuf, dbuf, sbuf, dw_sc):
        i = pl.program_id(1)
        sbuf[...] = _sig(gate_ref[...])
        ubuf[hb:, :] = val_ref[...] * sbuf[...]
        halo = valh_ref[...] * _sig(gateh_ref[...])
        ubuf[0:hb, :] = jnp.where(i > 0, halo, 0.0)
        dbuf[0:tm, :] = du_ref[...]
        dbuf[tm:, :] = jnp.where(i < ni - 1, dun_ref[...], 0.0)

        @pl.when(i == 0)
        def _():
            dw_sc[...] = jnp.zeros_like(dw_sc)
            db_ref[...] = jnp.zeros_like(db_ref)

        db_ref[...] += jnp.sum(du_ref[...], axis=0, keepdims=True)
        for l0 in range(0, lc, sub):
            ls = slice(l0, l0 + sub)
            for r0 in range(0, tm, rc):
                dwin = dbuf[r0:r0 + rc + hb, ls]
                dchunk = dwin[0:rc]
                for off, rows in _shifted_rows(ubuf[r0:r0 + hb + rc, ls], rc):
                    k = off - base
                    if 0 <= k < kc:
                        prod = rows * dchunk
                        part = prod[0:SUBLANES]
                        for s8 in range(SUBLANES, rc, SUBLANES):
                            part = part + prod[s8:s8 + SUBLANES]
                        dw_sc[k, :, ls] += part
                acc = jnp.zeros((rc, sub), F32)
                for off, rows in _shifted_rows(dwin, rc):
                    k = (kc - 1) - off
                    if 0 <= k < kc:
                        acc += w_ref[k:k + 1, ls] * rows
                v, s = val_ref[r0:r0 + rc, ls], sbuf[r0:r0 + rc, ls]
                dval_ref[r0:r0 + rc, ls] = (acc * s).astype(BF16)
                dgate_ref[r0:r0 + rc, ls] = (acc * v * s * (1.0 - s)).astype(BF16)

        @pl.when(i == ni - 1)
        def _():
            for k in range(kc):
                dw_ref[k:k + 1, :] = jnp.sum(dw_sc[k], axis=0, keepdims=True)

    return pl.pallas_call(
        body, name="conv_bwd", grid=(nl, ni),
        in_specs=[
            pl.BlockSpec((tm, lc), lambda l, i: (i, l)),
            pl.BlockSpec((hb, lc), lambda l, i: (jnp.minimum((i + 1) * ratio, last_halo), l)),
            pl.BlockSpec((tm, lc), lambda l, i: (i, l)),
            pl.BlockSpec((tm, lc), lambda l, i: (i, gate_off + l)),
            pl.BlockSpec((hb, lc), lambda l, i: (jnp.maximum(i * ratio - 1, 0), l)),
            pl.BlockSpec((hb, lc), lambda l, i: (jnp.maximum(i * ratio - 1, 0), gate_off + l)),
            pl.BlockSpec((kc, lc), lambda l, i: (0, l)),
        ],
        out_specs=[pl.BlockSpec((tm, lc), lambda l, i: (i, l)), pl.BlockSpec((tm, lc), lambda l, i: (i, l)),
                   pl.BlockSpec((kc, lc), lambda l, i: (0, l)), pl.BlockSpec((1, lc), lambda l, i: (0, l))],
        out_shape=[jax.ShapeDtypeStruct((T, E), BF16), jax.ShapeDtypeStruct((T, E), BF16),
                   jax.ShapeDtypeStruct((kc, E), F32), jax.ShapeDtypeStruct((1, E), F32)],
        scratch_shapes=[pltpu.VMEM((hb + tm, lc), F32), pltpu.VMEM((tm + hb, lc), F32),
                        pltpu.VMEM((tm, lc), F32), pltpu.VMEM((kc, SUBLANES, lc), F32)],
        compiler_params=_params(2),
    )(du1, du1, proj, proj, proj, proj, conv_w)


def _norm_prep(pb, kv_g, q_g, cos, sin, rkv, rq, wk, tm):
    T = pb.shape[0]

    def body(ckv_ref, cq_ref, kg_ref, qg_ref, cos_ref, sin_ref, c_ref, kr_ref, cqn_ref):
        blk = ckv_ref[...]
        c_ref[...] = _rms_fwd(blk[:, :rkv], kg_ref[...]).astype(BF16)
        kr_ref[...] = _rope(blk[:, rkv:rkv + LANES], cos_ref[...], sin_ref[...]).astype(BF16)
        cqn_ref[...] = _rms_fwd(cq_ref[...], qg_ref[...]).astype(BF16)

    return pl.pallas_call(
        body, name="norm_prep", grid=(T // tm,),
        in_specs=[pl.BlockSpec((tm, wk), lambda i: (i, 0)), pl.BlockSpec((tm, rq), lambda i: (i, wk // rq)),
                  pl.BlockSpec((1, rkv), lambda i: (0, 0)), pl.BlockSpec((1, rq), lambda i: (0, 0)),
                  pl.BlockSpec((tm, LANES), lambda i: (i, 0)), pl.BlockSpec((tm, LANES), lambda i: (i, 0))],
        out_specs=[pl.BlockSpec((tm, rkv), lambda i: (i, 0)), pl.BlockSpec((tm, LANES), lambda i: (i, 0)),
                   pl.BlockSpec((tm, rq), lambda i: (i, 0))],
        out_shape=[jax.ShapeDtypeStruct((T, rkv), BF16), jax.ShapeDtypeStruct((T, LANES), BF16),
                   jax.ShapeDtypeStruct((T, rq), BF16)],
        compiler_params=_params(1),
    )(pb, pb, kv_g, q_g, cos, sin)


def _attn_fwd(q_all, kv, kr, H, tq, scale):
    T = q_all.shape[0]
    nq = T // tq
    pair = 2
    W = pair * LANES
    assert H % pair == 0
    hp_n = H // pair

    def body(qn_ref, qr_ref, kn_ref, kr_ref, v_ref, o_ref, lse_ref, *scratch):
        qi = pl.program_id(1)
        chains = [scratch[4 * a:4 * a + 4] for a in range(pair)]
        lanes = [slice(a * LANES, (a + 1) * LANES) for a in range(pair)]
        groups = [slice(c * LANES, (c + 1) * LANES) for c in range(tq // LANES)]

        def fold(x, op):
            r = x[:, groups[0]]
            for gsl in groups[1:]:
                r = op(r, x[:, gsl])
            return r

        for _, m_sc, l_sc, acc_sc in chains:
            m_sc[...] = jnp.full_like(m_sc, MASK_VALUE)
            l_sc[...] = jnp.zeros_like(l_sc)
            acc_sc[...] = jnp.zeros_like(acc_sc)

        def scores(j, masked):
            rows = pl.ds(pl.multiple_of(j * tq, tq), tq)
            krope = kr_ref[rows, :]
            for a, (s_sc, m_sc, _, _) in enumerate(chains):
                q = jnp.concatenate([qn_ref[:, lanes[a]], qr_ref[:, lanes[a]]], axis=1)
                k = jnp.concatenate([kn_ref[rows, lanes[a]], krope], axis=1)
                s = _dot(q, k, NT) * scale
                if masked:
                    row = lax.broadcasted_iota(jnp.int32, s.shape, 0)
                    col = lax.broadcasted_iota(jnp.int32, s.shape, 1)
                    s = jnp.where(col <= row, s, MASK_VALUE)
                s_sc[j] = s
                m_sc[...] = jnp.maximum(m_sc[...], fold(s, jnp.maximum))

        def unmasked(j, carry):
            scores(j, False)
            return carry

        lax.fori_loop(0, qi, unmasked, 0)
        scores(qi, True)
        for _, m_sc, _, _ in chains:
            m_sc[...] = jnp.broadcast_to(jnp.max(m_sc[...], axis=1, keepdims=True), m_sc.shape)

        def weigh(j, carry):
            rows = pl.ds(pl.multiple_of(j * tq, tq), tq)
            for a, (s_sc, m_sc, l_sc, acc_sc) in enumerate(chains):
                s, m = s_sc[j], m_sc[...]
                p = jnp.concatenate([jnp.exp(s[:, gsl] - m) for gsl in groups], axis=1)
                l_sc[...] += fold(p, jnp.add)
                acc_sc[...] += _dot(p, v_ref[rows, lanes[a]], NN)
            return carry

        lax.fori_loop(0, qi + 1, weigh, 0)
        for a, (_, m_sc, l_sc, acc_sc) in enumerate(chains):
            l = jnp.sum(l_sc[...], axis=1, keepdims=True)
            o_ref[:, lanes[a]] = acc_sc[...] / l
            lse_ref[a] = m_sc[:, 0:1] + jnp.log(l)

    chain_scratch = [pltpu.VMEM((nq, tq, tq), F32), pltpu.VMEM((tq, LANES), F32), pltpu.VMEM((tq, LANES), F32),
                     pltpu.VMEM((tq, LANES), F32)]
    return pl.pallas_call(
        body, name="attn_fwd", grid=(hp_n, nq),
        in_specs=[pl.BlockSpec((tq, W), lambda hp, qi: (qi, hp)),
                  pl.BlockSpec((tq, W), lambda hp, qi: (qi, hp_n + hp)),
                  pl.BlockSpec((T, W), lambda hp, qi: (0, hp)),
                  pl.BlockSpec((T, LANES), lambda hp, qi: (0, 0)),
                  pl.BlockSpec((T, W), lambda hp, qi: (0, hp_n + hp))],
        out_specs=[pl.BlockSpec((tq, W), lambda hp, qi: (qi, hp)),
                   pl.BlockSpec((pair, tq, 1), lambda hp, qi: (hp, qi, 0))],
        out_shape=[jax.ShapeDtypeStruct((T, H * LANES), F32), jax.ShapeDtypeStruct((H, T, 1), F32)],
        scratch_shapes=chain_scratch * pair, compiler_params=_params(2),
    )(q_all, q_all, kv, kr, kv)


def _attn_bwd(q_all, kv, kr, do, o, lse, H, tq, scale):
    T = q_all.shape[0]
    nq = T // tq
    HV = H * LANES
    pair = 2
    tk2 = pair * tq
    ng = T // tk2
    assert ng * tk2 == T

    def body(qn_ref, qr_ref, kn_ref, kr_ref, v_ref, do_ref, o_ref, lse_ref,
             dqn_ref, dqr_ref, dkn_ref, dkr_ref, dv_ref, dq_sc, dk_sc, dv_sc):
        g = pl.program_id(1)

        @pl.when(g == 0)
        def _():
            dq_sc[...] = jnp.zeros_like(dq_sc)

        dk_sc[...] = jnp.zeros_like(dk_sc)
        dv_sc[...] = jnp.zeros_like(dv_sc)

        def block(qi, modes):
            rows = pl.ds(pl.multiple_of(qi * tq, tq), tq)
            q = jnp.concatenate([qn_ref[rows, :], qr_ref[rows, :]], axis=1)
            dov = do_ref[rows, :]
            delta = jnp.sum(dov.astype(F32) * o_ref[rows, :], axis=1, keepdims=True)
            lse_q = lse_ref[0, rows, :]
            dq = None
            for c, masked in enumerate(modes):
                if masked is None:
                    continue
                kr_ = slice(c * tq, (c + 1) * tq)
                k = jnp.concatenate([kn_ref[kr_, :], kr_ref[kr_, :]], axis=1)
                s = _dot(q, k, NT) * scale
                if masked:
                    row = lax.broadcasted_iota(jnp.int32, s.shape, 0)
                    col = lax.broadcasted_iota(jnp.int32, s.shape, 1)
                    s = jnp.where(col <= row, s, MASK_VALUE)
                p = jnp.exp(s - lse_q)
                dv_sc[kr_, :] += _dot(p, dov, TN)
                dp = _dot(dov, v_ref[kr_, :], NT)
                ds = (p * (dp - delta) * scale).astype(BF16)
                dk_sc[kr_, :] += _dot(ds, q, TN)
                part = _dot(ds, k, NN)
                dq = part if dq is None else dq + part
            dq_sc[rows, :] += dq

        block(pair * g, (True, None))
        block(pair * g + 1, (False, True))

        def below(qi, carry):
            block(qi, (False, False))
            return carry

        lax.fori_loop(pair * g + pair, nq, below, 0)
        dkn_ref[...] = dk_sc[:, :LANES].astype(BF16)
        dkr_ref[...] = dk_sc[:, LANES:]
        dv_ref[...] = dv_sc[...].astype(BF16)

        @pl.when(g == ng - 1)
        def _():
            dqn_ref[...] = dq_sc[:, :LANES].astype(BF16)
            dqr_ref[...] = dq_sc[:, LANES:]

    whole = lambda col: pl.BlockSpec((T, LANES), col)
    tile = lambda col: pl.BlockSpec((tk2, LANES), col)
    return pl.pallas_call(
        body, name="attn_bwd", grid=(H, ng),
        in_specs=[whole(lambda h, g: (0, h)), whole(lambda h, g: (0, H + h)),
                  tile(lambda h, g: (g, h)), tile(lambda h, g: (g, 0)), tile(lambda h, g: (g, H + h)),
                  whole(lambda h, g: (0, h)), whole(lambda h, g: (0, h)),
                  pl.BlockSpec((1, T, 1), lambda h, g: (h, 0, 0))],
        out_specs=[whole(lambda h, g: (0, h)), whole(lambda h, g: (0, h)),
                   tile(lambda h, g: (g, h)), tile(lambda h, g: (g, h)), tile(lambda h, g: (g, h))],
        out_shape=[jax.ShapeDtypeStruct((T, HV), BF16), jax.ShapeDtypeStruct((T, HV), F32),
                   jax.ShapeDtypeStruct((T, HV), BF16), jax.ShapeDtypeStruct((T, HV), F32),
                   jax.ShapeDtypeStruct((T, HV), BF16)],
        scratch_shapes=[pltpu.VMEM((T, 2 * LANES), F32), pltpu.VMEM((tk2, 2 * LANES), F32),
                        pltpu.VMEM((tk2, LANES), F32)],
        compiler_params=_params(2),
    )(q_all, q_all, kv, kr, kv, do, o, lse)


def _rope_bwd(dqr, dkr_heads, cos, sin, H, tm):
    T, HV = dqr.shape

    def body(dqr_ref, dkr_ref, cos_ref, sin_ref, dq_ref, dk_ref):
        c, s = cos_ref[...], sin_ref[...]
        dq_ref[...] = _rope(dqr_ref[...], c, s, transpose=True).astype(BF16)
        dk = dkr_ref[...]
        tot = dk[:, 0:LANES]
        for h in range(1, H):
            tot = tot + dk[:, h * LANES:(h + 1) * LANES]
        dk_ref[...] = _rope(tot, c, s, transpose=True)

    return pl.pallas_call(
        body, name="rope_bwd", grid=(T // tm,),
        in_specs=[pl.BlockSpec((tm, HV), lambda i: (i, 0)), pl.BlockSpec((tm, HV), lambda i: (i, 0)),
                  pl.BlockSpec((tm, LANES), lambda i: (i, 0)), pl.BlockSpec((tm, LANES), lambda i: (i, 0))],
        out_specs=[pl.BlockSpec((tm, HV), lambda i: (i, 0)), pl.BlockSpec((tm, LANES), lambda i: (i, 0))],
        out_shape=[jax.ShapeDtypeStruct((T, HV), BF16), jax.ShapeDtypeStruct((T, LANES), F32)],
        compiler_params=_params(1),
    )(dqr, dkr_heads, cos, sin)


def _place():
    x, y, c = lax.axis_index("x"), lax.axis_index("y"), lax.axis_index("c")
    chips = [(1 - x, y), (x, 1 - y), (1 - x, 1 - y)]
    return x, y, c, chips


def _all_gather_chips(pack):
    rows = pack.shape[0]
    half = rows // 2
    assert half * 2 == rows and half % BF16_ROWS == 0

    def body(w_ref, out_ref, send_sems, recv_sems, local_sem):
        x, y, c, chips = _place()
        sibling = (x, y, 1 - c)

        def region(px, py, pc):
            return out_ref.at[2 * px + py, pl.ds(pc * half, half), :]

        def copy(k, block, to, src=None):
            return pltpu.make_async_remote_copy(
                src_ref=region(*block) if src is None else src, dst_ref=region(*block),
                send_sem=send_sems.at[k], recv_sem=recv_sems.at[k], device_id=to, device_id_type=MESH)

        mine = pltpu.make_async_copy(w_ref, out_ref.at[2 * x + y], local_sem)
        mine.start()
        my_half = w_ref.at[pl.ds(c * half, half), :]
        first = [copy(j, (x, y, c), (*chip, c), src=my_half) for j, chip in enumerate(chips)]
        for cp in first:
            cp.start()
        passed = [copy(3 + j, (*chip, c), sibling) for j, chip in enumerate(chips)]
        for j, chip in enumerate(chips):
            copy(j, (*chip, c), (x, y, c)).wait_recv()
            passed[j].start()
        for j, chip in enumerate(chips):
            copy(3 + j, (*chip, 1 - c), (x, y, c)).wait_recv()
        for cp in first + passed:
            cp.wait_send()
        mine.wait()

    return pl.pallas_call(
        body, name="gather_weights", in_specs=[ANY], out_specs=ANY,
        out_shape=jax.ShapeDtypeStruct((N_CHIPS,) + pack.shape, pack.dtype),
        scratch_shapes=[pltpu.SemaphoreType.DMA((6,)), pltpu.SemaphoreType.DMA((6,)), pltpu.SemaphoreType.DMA],
    )(pack)


def _scatter_grads(grads):
    n = len(grads)

    def body(*refs):
        g_refs, r_refs = refs[:n], refs[n:2 * n]
        send_sems, recv_sems, local_sems = refs[2 * n:]
        x, y, c, chips = _place()
        me = 2 * x + y
        locals_, sends = [], []
        for w in range(n):
            cp = pltpu.make_async_copy(g_refs[w].at[me], r_refs[w].at[me], local_sems.at[w])
            cp.start()
            locals_.append(cp)
            for j, (px, py) in enumerate(chips):
                cp = pltpu.make_async_remote_copy(
                    src_ref=g_refs[w].at[2 * px + py], dst_ref=r_refs[w].at[me],
                    send_sem=send_sems.at[3 * w + j], recv_sem=recv_sems.at[3 * w + j],
                    device_id=(px, py, c), device_id_type=MESH)
                cp.start()
                sends.append(cp)
        for w in range(n):
            for j, (px, py) in enumerate(chips):
                pltpu.make_async_remote_copy(
                    src_ref=g_refs[w].at[me], dst_ref=r_refs[w].at[2 * px + py],
                    send_sem=send_sems.at[3 * w + j], recv_sem=recv_sems.at[3 * w + j],
                    device_id=(px, py, c), device_id_type=MESH).wait_recv()
        for cp in sends:
            cp.wait_send()
        for cp in locals_:
            cp.wait()

    return pl.pallas_call(
        body, name="scatter_grads", in_specs=[ANY] * n, out_specs=[ANY] * n,
        out_shape=[jax.ShapeDtypeStruct(g.shape, g.dtype) for g in grads],
        scratch_shapes=[pltpu.SemaphoreType.DMA((3 * n,)), pltpu.SemaphoreType.DMA((3 * n,)),
                        pltpu.SemaphoreType.DMA((n,))],
    )(*grads)


def _swap_cores(parts):
    n = len(parts)

    def body(*refs):
        p_refs, r_refs = refs[:n], refs[n:2 * n]
        send_sems, recv_sems = refs[2 * n:]
        x, y, c, _ = _place()
        copies = [pltpu.make_async_remote_copy(
            src_ref=p_refs[w], dst_ref=r_refs[w], send_sem=send_sems.at[w], recv_sem=recv_sems.at[w],
            device_id=(x, y, 1 - c), device_id_type=MESH) for w in range(n)]
        for cp in copies:
            cp.start()
        for cp in copies:
            cp.wait()

    return pl.pallas_call(
        body, name="swap_cores", in_specs=[ANY] * n, out_specs=[ANY] * n,
        out_shape=[jax.ShapeDtypeStruct(p.shape, p.dtype) for p in parts],
        scratch_shapes=[pltpu.SemaphoreType.DMA((n,)), pltpu.SemaphoreType.DMA((n,))],
    )(*parts)


HBM = pl.BlockSpec(memory_space=pltpu.HBM)
SEM = pl.BlockSpec(memory_space=pltpu.SEMAPHORE)
EFFECT = pltpu.SideEffectType.DATAFLOW_SIDE_EFFECTING


def _push_copies(a_refs, l_refs, send_sems, recv_sems, by_target):
    x, y, c, chips = _place()
    me = 2 * x + y
    out = []
    for w, (a_ref, l_ref) in enumerate(zip(a_refs, l_refs)):
        for j, (px, py) in enumerate(chips):
            peer = 2 * px + py
            out.append((
                pltpu.make_async_remote_copy(
                    src_ref=a_ref.at[peer] if by_target else a_ref, dst_ref=l_ref.at[me],
                    send_sem=send_sems.at[3 * w + j], recv_sem=recv_sems.at[3 * w + j],
                    device_id=(px, py, c), device_id_type=MESH),
                pltpu.make_async_remote_copy(
                    src_ref=a_ref.at[me] if by_target else a_ref, dst_ref=l_ref.at[peer],
                    send_sem=send_sems.at[3 * w + j], recv_sem=recv_sems.at[3 * w + j],
                    device_id=(px, py, c), device_id_type=MESH)))
    return out


def _push_start(name, arrs, by_target):
    n = len(arrs)
    lands = [lax.empty((N_CHIPS,) + (a.shape[1:] if by_target else a.shape), a.dtype) for a in arrs]

    def body(*refs):
        a_refs, l_refs = refs[:n], refs[n:2 * n]
        send_sems, recv_sems = refs[2 * n], refs[2 * n + 1]
        token = refs[-1]
        for send, _ in _push_copies(a_refs, l_refs, send_sems, recv_sems, by_target):
            send.start()
        token[...] = jnp.zeros_like(token)

    res = pl.pallas_call(
        body, name=name,
        out_shape=(pltpu.SemaphoreType.DMA((3 * n,)), pltpu.SemaphoreType.DMA((3 * n,)),
                   *[pltpu.HBM(a.shape, a.dtype) for a in arrs], *[pltpu.HBM(l.shape, l.dtype) for l in lands],
                   jax.ShapeDtypeStruct((8, LANES), F32)),
        in_specs=[HBM] * (2 * n), out_specs=(SEM, SEM, *[HBM] * (2 * n), pl.BlockSpec(memory_space=pltpu.VMEM)),
        input_output_aliases={i: 2 + i for i in range(2 * n)},
        compiler_params=pltpu.CompilerParams(has_side_effects=EFFECT),
    )(*[pltpu.with_memory_space_constraint(a, pltpu.HBM) for a in list(arrs) + lands])
    return res[0], res[1], list(res[2:2 + n]), list(res[2 + n:2 + 2 * n]), res[-1]


def _push_wait(name, send_sems, recv_sems, arrs, lands, after, by_target):
    n = len(arrs)

    def body(*refs):
        a_refs, l_refs = refs[:n], refs[n:2 * n]
        s_sems, r_sems = refs[2 * n], refs[2 * n + 1]
        for send, recv in _push_copies(a_refs, l_refs, s_sems, r_sems, by_target):
            send.wait_send()
            recv.wait_recv()

    res = pl.pallas_call(
        body, name=name,
        out_shape=[pltpu.HBM(a.shape, a.dtype) for a in list(arrs) + list(lands)],
        in_specs=[HBM] * (2 * n) + [SEM, SEM] + [ANY] * len(after), out_specs=[HBM] * (2 * n),
        input_output_aliases={i: i for i in range(2 * n)},
        compiler_params=pltpu.CompilerParams(has_side_effects=EFFECT),
    )(*arrs, *lands, send_sems, recv_sems, *after)
    return list(res[:n]), list(res[n:])


def _all_reduce_small(part):
    n_dev = 8

    def body(p_ref, out_ref, buf, send_sems, recv_sems):
        x, y, c, _ = _place()
        me = 4 * x + 2 * y + c
        buf[me] = p_ref[...]
        copies = []
        for k in range(1, n_dev):
            kx, ky, kc = (k >> 2) & 1, (k >> 1) & 1, k & 1
            peer = (x ^ kx, y ^ ky, c ^ kc)
            cp = pltpu.make_async_remote_copy(
                src_ref=p_ref, dst_ref=buf.at[me], send_sem=send_sems.at[k - 1], recv_sem=recv_sems.at[k - 1],
                device_id=peer, device_id_type=MESH)
            cp.start()
            copies.append(cp)
        for k in range(1, n_dev):
            pltpu.make_async_remote_copy(
                src_ref=p_ref, dst_ref=buf.at[me ^ k], send_sem=send_sems.at[k - 1], recv_sem=recv_sems.at[k - 1],
                device_id=(x, y, c), device_id_type=MESH).wait_recv()
        for cp in copies:
            cp.wait_send()
        tot = buf[0]
        for d in range(1, n_dev):
            tot = tot + buf[d]
        out_ref[...] = tot

    vm = pl.BlockSpec(memory_space=pltpu.VMEM)
    return pl.pallas_call(
        body, name="all_reduce_small", in_specs=[vm], out_specs=vm,
        out_shape=jax.ShapeDtypeStruct(part.shape, F32),
        scratch_shapes=[pltpu.VMEM((n_dev,) + part.shape, F32), pltpu.SemaphoreType.DMA((n_dev - 1,)),
                        pltpu.SemaphoreType.DMA((n_dev - 1,))],
    )(part)


def _row_tiles(shape):
    ax = next(d for d, s in enumerate(shape) if s > 1)
    tr = _gcd(ADAM_ROWS, shape[ax])
    block = tuple(tr if d == ax else s for d, s in enumerate(shape))
    return shape[ax] // tr, block, lambda i: tuple(i if d == ax else 0 for d in range(len(shape)))


def _sum_chips(name, r):
    shape = r.shape[1:]
    steps, block, index = _row_tiles(shape)

    def body(r_ref, o_ref):
        tot = r_ref[0].astype(F32)
        for q in range(1, N_CHIPS):
            tot = tot + r_ref[q].astype(F32)
        o_ref[...] = tot

    return pl.pallas_call(
        body, name=name, grid=(steps,),
        in_specs=[pl.BlockSpec((N_CHIPS,) + block, lambda i: (0,) + index(i))],
        out_specs=pl.BlockSpec(block, index),
        out_shape=jax.ShapeDtypeStruct(shape, F32), compiler_params=_params(1),
    )(r)


def _adamw_math(g, w, m, v):
    mn = ADAM_B1 * m + (1.0 - ADAM_B1) * g
    vn = ADAM_B2 * v + (1.0 - ADAM_B2) * jnp.square(g)
    m_hat = mn / (1.0 - ADAM_B1 ** ADAM_STEP)
    v_hat = vn / (1.0 - ADAM_B2 ** ADAM_STEP)
    return -ADAM_LR * (m_hat / (jnp.sqrt(v_hat) + ADAM_EPS) + ADAM_WD * w), mn, vn


def _adamw(name, g_parts, w, m, v):
    steps, block, index = _row_tiles(w.shape)
    n = len(g_parts)

    def body(*refs):
        g = refs[0][...]
        for r in refs[1:n]:
            g = g + r[...]
        w_ref, m_ref, v_ref, go_ref, d_ref, mo_ref, vo_ref = refs[n:]
        go_ref[...] = g
        d_ref[...], mo_ref[...], vo_ref[...] = _adamw_math(g, w_ref[...], m_ref[...], v_ref[...])

    spec = pl.BlockSpec(block, index)
    return pl.pallas_call(
        body, name=name, grid=(steps,), in_specs=[spec] * (n + 3), out_specs=[spec] * 4,
        out_shape=[jax.ShapeDtypeStruct(w.shape, F32)] * 4, compiler_params=_params(1),
    )(*g_parts, w, m, v)


def _adamw_vectors(items):
    n = len(items)

    def body(*refs):
        ins, outs = refs[:4 * n], refs[4 * n:]
        for k in range(n):
            g, w, m, v = (r[...] for r in ins[4 * k:4 * k + 4])
            outs[3 * k][...], outs[3 * k + 1][...], outs[3 * k + 2][...] = _adamw_math(g, w, m, v)

    vm = pl.BlockSpec(memory_space=pltpu.VMEM)
    res = pl.pallas_call(
        body, name="adamw_vectors", in_specs=[vm] * (4 * n), out_specs=[vm] * (3 * n),
        out_shape=[jax.ShapeDtypeStruct(it[1].shape, F32) for it in items for _ in range(3)],
    )(*[a for it in items for a in it])
    return [res[3 * k:3 * k + 3] for k in range(n)]


def _pack_rows(flat, dtype, multiple):
    n = flat.shape[0]
    total = -(-n // multiple) * multiple
    return jnp.pad(flat, (0, total - n)).astype(dtype).reshape(total // LANES, LANES)


def kernel(x, positions, ln_g, ln_b, a_w_in, a_b_in, a_conv_w, a_conv_b, a_norm_g, a_norm_b, a_w_out, a_b_out, kv_w_down, kv_norm_g, kv_w_uk, kv_w_uv, b_w_in, b_q_norm_g, b_w_uq, b_w_out, loss_target, m_ln_g, m_ln_b, m_a_w_in, m_a_b_in, m_a_conv_w, m_a_conv_b, m_a_norm_g, m_a_norm_b, m_a_w_out, m_a_b_out, m_kv_w_down, m_kv_norm_g, m_kv_w_uk, m_kv_w_uv, m_b_w_in, m_b_q_norm_g, m_b_w_uq, m_b_w_out, v_ln_g, v_ln_b, v_a_w_in, v_a_b_in, v_a_conv_w, v_a_conv_b, v_a_norm_g, v_a_norm_b, v_a_w_out, v_a_b_out, v_kv_w_down, v_kv_norm_g, v_kv_w_uk, v_kv_w_uv, v_b_w_in, v_b_q_norm_g, v_b_w_uq, v_b_w_out):
    T, D = x.shape[1], x.shape[2]
    E = N_CHIPS * a_w_out.shape[1]
    KC = a_conv_w.shape[1]
    RKV = kv_norm_g.shape[0]
    H, DN = kv_w_uk.shape[1], kv_w_uk.shape[2]
    RQ = b_q_norm_g.shape[1]
    HV = N_CHIPS * b_w_out.shape[1]
    assert DN == LANES and kv_w_uv.shape[2] == LANES and HV == H * LANES
    assert kv_w_down.shape[1] == RKV + ROPE_DIM and b_w_uq.shape[3] == DN + ROPE_DIM
    assert ln_g.shape[0] == 2 and a_w_in.shape[0] == 1 and b_w_in.shape[0] == 1
    alpha = (2.0 * ln_g.shape[0]) ** 0.25
    scale = 1.0 / math.sqrt(DN + ROPE_DIM)
    WK = -(-(RKV + LANES) // 256) * 256
    assert WK % RQ == 0
    Z_OFF = WK + RQ
    tmw, tq = min(TM_WIDE, T), min(TQ, T)
    t512, t1024 = _fit(512, T), _fit(1024, T)
    xs = x[0]
    tgt = loss_target[0]
    px, py = lax.axis_index("x"), lax.axis_index("y")
    chip = 2 * px + py

    mats = [a_w_out[0], kv_w_down, kv_w_uk, kv_w_uv, b_w_in[0], b_w_uq[0], b_w_out[0]]
    vecs = [a_b_in[0], a_conv_w[0], a_conv_b[0], a_norm_g[0], a_norm_b[0], a_b_out[0]]
    first = jnp.concatenate(
        [a_w_in[0].astype(BF16).reshape(-1)]
        + [lax.bitcast_convert_type(w.reshape(-1), BF16).reshape(-1) for w in vecs])
    rest = [w.astype(BF16) for w in mats]
    gathered = _all_gather_chips(_pack_rows(first, BF16, 2 * BF16_ROWS * LANES)).reshape(N_CHIPS, -1)
    gathered, rest = lax.optimization_barrier((gathered, rest))
    rest_sems = _push_start("gather_rest_start", rest, by_target=False)
    g_win = gathered[:, :a_w_in[0].size].reshape((N_CHIPS,) + a_w_in[0].shape)
    off = a_w_in[0].size
    fvec = []
    for w in vecs:
        bits = gathered[:, off:off + 2 * w.size].reshape((N_CHIPS,) + w.shape + (2,))
        fvec.append(lax.bitcast_convert_type(bits, F32))
        off += 2 * w.size
    cols = lambda g: jnp.moveaxis(g, 0, -2).reshape(g.shape[1:-1] + (N_CHIPS * g.shape[-1],))
    w_in = cols(g_win)
    b_in = cols(fvec[0][:, None, :])
    conv_w = cols(fvec[1])
    conv_b, norm_g, norm_b, b_out = (cols(f[:, None, :]) for f in fvec[2:])
    row = lambda a: a.reshape(1, -1)
    g0, b0, g1, b1 = row(ln_g[0]), row(ln_b[0]), row(ln_g[1]), row(ln_b[1])
    kv_g, q_g = row(kv_norm_g), row(b_q_norm_g[0])
    plain = lambda acc, ins, i, j: [acc]

    b_in = b_in + rest_sems[4][0, 0]
    (proj,) = _row_mm("a_in", [((xs,), None)], w_in, nt=False, tm=t1024, tn=_fit(1536, 3 * E), tk=_fit(1024, D),
                      outs=[((T, 3 * E), F32, 'tile')], epi=lambda acc, ins, i, j: [acc + ins[0]],
                      epi_ins=[(b_in, 'col')])
    u1 = _conv_fwd(proj, conv_w, conv_b, E, tmw)
    u4 = _conv_post(u1, proj, norm_g, norm_b, E, tmw)

    rest, landed = _push_wait("gather_rest_wait", *rest_sems[:4], after=[u4], by_target=False)
    g_wout, g_wd, g_uk, g_uv, g_wbin, g_wuq, g_wbout = [
        lax.dynamic_update_slice(l, w[None], (chip,) + (0,) * w.ndim) for w, l in zip(rest, landed)]
    w_out = g_wout.reshape(E, D)
    wd = g_wd.reshape(D, RKV + ROPE_DIM)
    zpad = jnp.zeros((D, ROPE_HALF), BF16)
    wd_p = jnp.concatenate(
        [wd[:, :RKV], wd[:, RKV:RKV + ROPE_HALF], zpad, wd[:, RKV + ROPE_HALF:], zpad,
         jnp.zeros((D, WK - RKV - LANES), BF16)], axis=1)
    w_bin = cols(g_wbin)
    w_z = w_bin[:, RQ:]
    wb_small = jnp.concatenate([wd_p, w_bin[:, :RQ]], axis=1)
    wb_all = jnp.concatenate([wd_p, w_bin], axis=1)
    w_kv = jnp.concatenate([g_uk.reshape(RKV, HV), g_uv.reshape(RKV, HV)], axis=1)
    wuq = g_wuq.reshape(RQ, H, DN + ROPE_DIM)
    zq = jnp.zeros((RQ, H, ROPE_HALF), BF16)
    w_qr = jnp.concatenate([wuq[:, :, DN:DN + ROPE_HALF], zq, wuq[:, :, DN + ROPE_HALF:], zq], axis=2)
    w_q = jnp.concatenate([wuq[:, :, :DN].reshape(RQ, HV), w_qr.reshape(RQ, HV)], axis=1)
    w_bout = g_wbout.reshape(HV, D)

    freqs = ROPE_THETA ** (-jnp.arange(0, ROPE_DIM, 2, dtype=F32) / ROPE_DIM)
    ang = positions[0].astype(F32)[:, None] * freqs
    cs, sn = jnp.cos(ang), jnp.sin(ang)
    ones, zeros = jnp.ones_like(cs), jnp.zeros_like(cs)
    cos_t = jnp.concatenate([cs, ones, cs, ones], axis=1)
    sin_t = jnp.concatenate([-sn, zeros, sn, zeros], axis=1)

    def ln_epi(acc, ins, i, j):
        bias, res, g, b = ins
        xhat, rstd = _ln_stats(alpha * res + acc + bias)
        h = xhat * g + b
        return [h, h, xhat, rstd]

    h1, h1b, xhat1, rstd1 = _row_mm(
        "a_out", [((u4,), None)], w_out, nt=False, tm=t512, tn=D, tk=_fit(2048, E),
        outs=[((T, D), F32, 'tile'), ((T, D), BF16, 'tile'), ((T, D), F32, 'tile'), ((T, 1), F32, 'row')],
        epi=ln_epi, epi_ins=[(b_out, 'col'), (xs, 'tile'), (g0, 'col'), (b0, 'col')])

    tkb = _fit(512, _gcd(WK, RQ, HV))
    (pb,) = _row_mm("b_in", [((h1b,), None)], wb_small, nt=False, tm=t1024, tn=_fit(1024, Z_OFF),
                    tk=_fit(1024, D), outs=[((T, Z_OFF), F32, 'tile')], epi=plain)
    (zb,) = _row_mm("b_in_gate", [((h1b,), None)], w_z, nt=False, tm=t1024, tn=_fit(2048, HV),
                    tk=_fit(1024, D), outs=[((T, HV), F32, 'tile')], epi=plain)
    c_lat, kr, cqn = _norm_prep(pb, kv_g, q_g, cos_t, sin_t, RKV, RQ, WK, tmw)
    (kv,) = _row_mm("kv_up", [((c_lat,), None)], w_kv, nt=False, tm=t1024, tn=_fit(2048, HV),
                    tk=_fit(1024, RKV), outs=[((T, 2 * HV), BF16, 'tile')], epi=plain)
    tnq = _fit(2048, HV)
    half_q = HV // tnq

    def q_epi(acc, ins, i, j):
        return [jnp.where(j >= half_q, _rope(acc, ins[0], ins[1]), acc)]

    (q_all,) = _row_mm("q_up", [((cqn,), None)], w_q, nt=False, tm=t1024, tn=tnq, tk=_fit(1024, RQ),
                       outs=[((T, 2 * HV), BF16, 'tile')], epi=q_epi,
                       epi_ins=[(cos_t, 'row'), (sin_t, 'row')])
    o, lse = _attn_fwd(q_all, kv, kr, H, tq, scale)

    def loss_epi(acc, ins, i, j):
        res, g, b, target = ins
        xhat, rstd = _ln_stats(alpha * res + acc)
        diff = xhat * g + b - target
        dr, dg, db = _ln_bwd(diff / D, xhat, rstd, g)
        return [dr, 0.5 * jnp.sum(diff * diff, keepdims=True) / D, dg, db]

    dr1, loss_part, dg1, db1 = _row_mm(
        "b_out", [((o, zb), _gate)], w_bout, nt=False, tm=tmw, tn=D, tk=_fit(2048, HV),
        outs=[((T, D), F32, 'tile'), ((1, 1), F32, 'acc'), ((1, D), F32, 'acc'), ((1, D), F32, 'acc')],
        epi=loss_epi, epi_ins=[(h1, 'tile'), (g1, 'col'), (b1, 'col'), (tgt, 'tile')])

    def gate_bwd_epi(acc, ins, i, j):
        return [acc * _silu(ins[1]), acc * ins[0] * _silu_grad(ins[1])]

    do, dz = _row_mm(
        "b_out_bwd", [((dr1,), None)], w_bout, nt=True, tm=tmw, tn=_fit(2048, HV), tk=_fit(1024, D),
        outs=[((T, HV), BF16, 'tile'), ((T, HV), BF16, 'tile')], epi=gate_bwd_epi,
        epi_ins=[(o, 'tile'), (zb, 'tile')])
    gw_bout = _tn_mm("dw_b_out", (o, zb), _gate, [((dr1,), None)], tn=_fit(1024, D), tk=t512, out_dtype=BF16)
    dqn, dqr, dkn, dkr_h, dv = _attn_bwd(q_all, kv, kr, do, o, lse, H, tq, scale)
    dqr_pre, dkr_pre = _rope_bwd(dqr, dkr_h, cos_t, sin_t, H, tmw)

    def cq_bwd_epi(acc, ins, i, j):
        dx, dg = _rms_bwd(acc, ins[0], ins[1])
        return [dx, dg]

    dcq, dqg = _row_mm(
        "q_up_bwd", [((dqn,), None), ((dqr_pre,), None)], w_q, nt=True, tm=tmw, tn=RQ, tk=_fit(2048, HV),
        outs=[((T, RQ), BF16, 'tile'), ((1, RQ), F32, 'acc')], epi=cq_bwd_epi,
        epi_ins=[(pb, pl.BlockSpec((tmw, RQ), lambda i, j, k: (i, WK // RQ))), (q_g, 'col')])
    gw_q = _tn_mm("dw_q_up", (cqn,), None, [((dqn,), None), ((dqr_pre,), None)],
                  tn=_fit(2048, HV), tk=t1024, out_dtype=BF16)

    def ckv_bwd_epi(acc, ins, i, j):
        blk, dkr_t, g = ins
        dx, dg = _rms_bwd(acc, blk[:, :RKV], g)
        parts = [dx, dkr_t]
        if WK > RKV + LANES:
            parts.append(jnp.zeros((dx.shape[0], WK - RKV - LANES), F32))
        return [jnp.concatenate(parts, axis=1), dg]

    dckv, dkvg = _row_mm(
        "kv_up_bwd", [((dkn,), None), ((dv,), None)], w_kv, nt=True, tm=tmw, tn=RKV, tk=_fit(2048, HV),
        outs=[((T, WK), BF16, pl.BlockSpec((tmw, WK), lambda i, j, k: (i, 0))), ((1, RKV), F32, 'acc')],
        epi=ckv_bwd_epi,
        epi_ins=[(pb, pl.BlockSpec((tmw, WK), lambda i, j, k: (i, 0))), (dkr_pre, 'row'), (kv_g, 'col')])
    gw_kv = _tn_mm("dw_kv_up", (c_lat,), None, [((dkn,), None), ((dv,), None)],
                   tn=_fit(2048, HV), tk=t1024, out_dtype=BF16)

    def ln1_bwd_epi(acc, ins, i, j):
        dr_up, xhat, rstd, g = ins
        dr, dg, db = _ln_bwd(alpha * dr_up + acc, xhat, rstd, g)
        return [dr, dg, db]

    dp_segs = [((dckv,), None), ((dcq,), None), ((dz,), None)]
    gw_ball = _tn_mm("dw_b_in", (h1b,), None, dp_segs, tn=tkb, tk=t1024, out_dtype=BF16)

    shard_cols = lambda g: jnp.moveaxis(g.reshape(g.shape[0], N_CHIPS, -1), 1, 0)
    shard_rows = lambda g: g.reshape(N_CHIPS, g.shape[0] // N_CHIPS, g.shape[1])
    gq = gw_q.reshape(RQ, 2, H, LANES)
    g_uq = jnp.concatenate(
        [gq[:, 0], gq[:, 1, :, :ROPE_HALF], gq[:, 1, :, 2 * ROPE_HALF:3 * ROPE_HALF]], axis=2)
    g_wd_full = jnp.concatenate(
        [gw_ball[:, :RKV], gw_ball[:, RKV:RKV + ROPE_HALF],
         gw_ball[:, RKV + 2 * ROPE_HALF:RKV + 3 * ROPE_HALF]], axis=1)
    late_names = ["kv_w_down", "kv_w_uk", "kv_w_uv", "b_w_in", "b_w_uq", "b_w_out"]
    late_w = [kv_w_down, kv_w_uk, kv_w_uv, b_w_in, b_w_uq, b_w_out]
    chip_major = [g_wd_full, gw_kv[:, :HV], gw_kv[:, HV:], shard_cols(gw_ball[:, WK:]), g_uq, gw_bout]
    late_grads = [g.reshape((N_CHIPS,) + w.shape) for g, w in zip(chip_major, late_w)]
    late_sems = _push_start("scatter_late_start", late_grads, by_target=True)

    dr0, dg0, db0 = _row_mm(
        "b_in_bwd", dp_segs, wb_all, nt=True, tm=t512, tn=D, tk=tkb,
        outs=[((T, D), F32, 'tile'), ((1, D), F32, 'acc'), ((1, D), F32, 'acc')], epi=ln1_bwd_epi,
        epi_ins=[(dr1, 'tile'), (xhat1, 'tile'), (rstd1, 'row'), (g0 + late_sems[4][0, 0], 'col')])

    def conv_branch_bwd_epi(acc, ins, i, j):
        u1_t, z, g, b = ins
        xhat, rstd = _ln_stats(u1_t)
        u2 = xhat * g + b
        du3 = acc * _silu(z)
        dz_a = acc * _silu(u2) * _silu_grad(z)
        du1, dg, db = _ln_bwd(du3 * _silu_grad(u2), xhat, rstd, g)
        return [du1, dz_a, dg, db]

    du1, dz_a, dng, dnb = _row_mm(
        "a_out_bwd", [((dr0,), None)], w_out, nt=True, tm=tmw, tn=E, tk=_fit(1024, D),
        outs=[((T, E), F32, 'tile'), ((T, E), BF16, 'tile'), ((1, E), F32, 'acc'), ((1, E), F32, 'acc')],
        epi=conv_branch_bwd_epi,
        epi_ins=[(u1, 'tile'), (proj, pl.BlockSpec((tmw, E), lambda i, j, k: (i, 2))), (norm_g, 'col'),
                 (norm_b, 'col')])
    gw_out, dbo = _tn_mm("dw_a_out", (u4,), None, [((dr0,), None)], tn=_fit(1024, D), tk=t1024,
                         out_dtype=BF16, colsum=True)
    dval, dgate, dcw, dcb = _conv_bwd(du1, proj, conv_w, E, tmw)
    dproj_segs = [((dval,), None), ((dgate,), None), ((dz_a,), None)]
    gw_in, dbi = _tn_mm("dw_a_in", (xs,), None, dproj_segs, tn=_fit(2048, E), tk=t512, out_dtype=BF16,
                        colsum=True)

    def own_block_in_place(sent, landed):
        own = lax.dynamic_index_in_dim(sent, chip, 0, keepdims=True)
        return lax.dynamic_update_slice(landed, own, (chip,) + (0,) * (sent.ndim - 1))

    late_sent, late_landed = _push_wait("scatter_late_wait", *late_sems[:4], after=[dbi], by_target=True)
    early_names = ["a_w_in", "a_w_out"]
    early_grads = [shard_cols(gw_in).reshape((N_CHIPS,) + a_w_in.shape),
                   gw_out.reshape((N_CHIPS,) + a_w_out.shape)]
    early_sems = _push_start("scatter_early_start", early_grads, by_target=True)
    (grad_x,) = _row_mm(
        "a_in_bwd", dproj_segs, w_in, nt=True, tm=t512, tn=_fit(1024, D), tk=_fit(2048, E),
        outs=[((T, D), F32, 'tile')], epi=lambda acc, ins, i, j: [alpha * ins[0] + acc + ins[1]],
        epi_ins=[(dr0, 'tile'), (jnp.zeros((1, D), F32) + early_sems[4][0, 0], 'col')])
    late_sums = [_sum_chips("sum_" + n, own_block_in_place(s, l))
                 for n, s, l in zip(late_names, late_sent, late_landed)]
    early_sent, early_landed = _push_wait("scatter_early_wait", *early_sems[:4],
                                          after=[grad_x, late_sums[-1]], by_target=True)
    early_sums = [_sum_chips("sum_" + n, own_block_in_place(s, l))
                  for n, s, l in zip(early_names, early_sent, early_landed)]
    names = early_names + late_names
    core_sums = early_sums + late_sums
    sibling_sums = _swap_cores(core_sums)
    big_w = [a_w_in, a_w_out, kv_w_down, kv_w_uk, kv_w_uv, b_w_in, b_w_uq, b_w_out]
    big_m = [m_a_w_in, m_a_w_out, m_kv_w_down, m_kv_w_uk, m_kv_w_uv, m_b_w_in, m_b_w_uq, m_b_w_out]
    big_v = [v_a_w_in, v_a_w_out, v_kv_w_down, v_kv_w_uk, v_kv_w_uv, v_b_w_in, v_b_w_uq, v_b_w_out]
    big_out = {}
    for n, mine, theirs, w, m, v in zip(names, core_sums, sibling_sums, big_w, big_m, big_v):
        big_out[n] = _adamw("adamw_" + n, [mine, theirs], w, m, v)

    small_full = [jnp.concatenate([dg0, dg1]), jnp.concatenate([db0, db1]), dbi, dcw, dcb, dng, dnb, dbo,
                  dkvg, dqg]
    sflat = jnp.concatenate([g.reshape(-1) for g in small_full])
    summed = _all_reduce_small(_pack_rows(sflat, F32, 8 * LANES)).reshape(-1)
    soff = 0
    sgrads = []
    for g in small_full:
        sgrads.append(summed[soff:soff + g.size].reshape(g.shape))
        soff += g.size
    local_cols = lambda g, n: lax.dynamic_slice_in_dim(g, chip * n, n, axis=g.ndim - 1)
    snames = ["ln_g", "ln_b", "a_b_in", "a_conv_w", "a_conv_b", "a_norm_g", "a_norm_b", "a_b_out",
              "kv_norm_g", "b_q_norm_g"]
    small_w = [ln_g, ln_b, a_b_in, a_conv_w, a_conv_b, a_norm_g, a_norm_b, a_b_out, kv_norm_g, b_q_norm_g]
    small_m = [m_ln_g, m_ln_b, m_a_b_in, m_a_conv_w, m_a_conv_b, m_a_norm_g, m_a_norm_b, m_a_b_out,
               m_kv_norm_g, m_b_q_norm_g]
    small_v = [v_ln_g, v_ln_b, v_a_b_in, v_a_conv_w, v_a_conv_b, v_a_norm_g, v_a_norm_b, v_a_b_out,
               v_kv_norm_g, v_b_q_norm_g]
    sharded = {"a_b_in", "a_conv_w", "a_conv_b", "a_norm_g", "a_norm_b", "a_b_out"}
    local_g = [(local_cols(g, w.shape[-1]) if n in sharded else g).reshape(w.shape)
               for n, g, w in zip(snames, sgrads, small_w)]
    at_least_2d = lambda a: a.reshape((1,) + a.shape) if a.ndim == 1 else a
    sres = _adamw_vectors([tuple(at_least_2d(a) for a in item)
                           for item in zip(local_g, small_w, small_m, small_v)])
    small_out = {n: [g] + [r.reshape(w.shape) for r in res]
                 for n, g, w, res in zip(snames, local_g, small_w, sres)}

    loss = lax.psum(loss_part[0, 0], ("x", "y", "c"))
    order = ["ln_g", "ln_b", "a_w_in", "a_b_in", "a_conv_w", "a_conv_b", "a_norm_g", "a_norm_b", "a_w_out",
             "a_b_out", "kv_w_down", "kv_norm_g", "kv_w_uk", "kv_w_uv", "b_w_in", "b_q_norm_g", "b_w_uq",
             "b_w_out"]
    outs = {**big_out, **small_out}
    result = [loss, grad_x[None]]
    for part in range(4):
        result += [outs[n][part] for n in order]
    return tuple(result)
```

```python
import functools
import math

import jax
import jax.numpy as jnp
from jax import lax
from jax.experimental import pallas as pl
from jax.experimental.pallas import tpu as pltpu

F32, BF16 = jnp.float32, jnp.bfloat16
NN = (((1,), (0,)), ((), ()))
NT = (((1,), (1,)), ((), ()))
TN = (((0,), (0,)), ((), ()))
MESH = pl.DeviceIdType.MESH
ANY = pl.BlockSpec(memory_space=pl.ANY)

LANES = 128
BF16_ROWS = 16
VMEM_LIMIT = 56 * 1024 * 1024
N_CHIPS = 4
LN_EPS = 1e-5
RMS_EPS = 1e-6
MASK_VALUE = -1e30
ROPE_THETA = 10000.0
ROPE_DIM = 64
ROPE_HALF = ROPE_DIM // 2
ADAM_LR, ADAM_B1, ADAM_B2, ADAM_EPS, ADAM_WD, ADAM_STEP = 0.001, 0.9, 0.999, 1e-08, 0.01, 10

MAX_TILE = 2048
TM_WIDE = 256
TQ = 512
CONV_HALO = 32
CONV_LC = 512
CONV_SUB = 256
SUBLANES = 8
CONV_RC = 32
ADAM_ROWS = 64


def _dot(a, b, dims):
    return lax.dot_general(a.astype(BF16), b.astype(BF16), dims, preferred_element_type=F32)


def _sig(x):
    return 1.0 / (1.0 + jnp.exp(-x))


def _params(n_axes):
    return pltpu.CompilerParams(dimension_semantics=("arbitrary",) * n_axes, vmem_limit_bytes=VMEM_LIMIT)


def _gcd(*v):
    return functools.reduce(math.gcd, v)


def _fit(want, dim):
    return math.gcd(min(want, MAX_TILE), dim)


def _row_mm(name, a_segs, b, *, nt, tm, tn, tk, outs, epi, epi_ins=()):
    M = a_segs[0][0][0].shape[0]
    N = b.shape[0] if nt else b.shape[1]
    nkb = [arrs[0].shape[1] // tk for arrs, _ in a_segs]
    koff = [sum(nkb[:s]) for s in range(len(nkb))]
    ni, nj, nk = M // tm, N // tn, sum(nkb)
    assert M % tm == 0 and N % tn == 0 and all(arrs[0].shape[1] % tk == 0 for arrs, _ in a_segs), name
    assert (b.shape[1] if nt else b.shape[0]) == nk * tk, name

    def spec_of(shape, kind):
        if isinstance(kind, pl.BlockSpec):
            return kind
        if kind == 'tile':
            return pl.BlockSpec((tm, tn), lambda i, j, k: (i, j))
        if kind == 'row':
            return pl.BlockSpec((tm, shape[1]), lambda i, j, k: (i, 0))
        if kind == 'col':
            return pl.BlockSpec((1, tn), lambda i, j, k: (0, j))
        assert kind == 'acc' and nj == 1, name
        return pl.BlockSpec(shape, lambda i, j, k: (0,) * len(shape))

    in_specs, operands = [], []
    for s, (arrs, _) in enumerate(a_segs):
        for arr in arrs:
            in_specs.append(pl.BlockSpec(
                (tm, tk), lambda i, j, k, s=s: (i, jnp.clip(k - koff[s], 0, nkb[s] - 1))))
            operands.append(arr)
    in_specs.append(pl.BlockSpec((tn, tk), lambda i, j, k: (j, k)) if nt
                    else pl.BlockSpec((tk, tn), lambda i, j, k: (k, j)))
    operands.append(b)
    for arr, kind in epi_ins:
        in_specs.append(spec_of(arr.shape, kind))
        operands.append(arr)
    out_specs = [spec_of(shape, kind) for shape, _, kind in outs]
    out_shape = [jax.ShapeDtypeStruct(shape, dtype) for shape, dtype, _ in outs]
    n_seg_refs = [len(arrs) for arrs, _ in a_segs]

    def body(*refs):
        pos = 0
        seg_refs = []
        for n in n_seg_refs:
            seg_refs.append(refs[pos:pos + n])
            pos += n
        b_ref = refs[pos]
        e_refs = refs[pos + 1:pos + 1 + len(epi_ins)]
        o_refs = refs[pos + 1 + len(epi_ins):pos + 1 + len(epi_ins) + len(outs)]
        i, j, k = pl.program_id(0), pl.program_id(1), pl.program_id(2)

        def product(fn, rs):
            a = rs[0][...] if fn is None else fn(*[r[...] for r in rs])
            return _dot(a, b_ref[...], NT if nt else NN)

        def finish(acc):
            res = epi(acc, [r[...] for r in e_refs], i, j)
            for o_ref, (_, _, kind), r in zip(o_refs, outs, res):
                if isinstance(kind, str) and kind == 'acc':
                    @pl.when(i == 0)
                    def _(o_ref=o_ref, r=r):
                        o_ref[...] = r

                    @pl.when(i > 0)
                    def _(o_ref=o_ref, r=r):
                        o_ref[...] += r
                else:
                    o_ref[...] = r.astype(o_ref.dtype)

        if nk == 1:
            finish(product(a_segs[0][1], seg_refs[0]))
            return
        acc_ref = refs[-1]

        @pl.when(k == 0)
        def _():
            acc_ref[...] = jnp.zeros_like(acc_ref)

        for s, ((_, fn), rs) in enumerate(zip(a_segs, seg_refs)):
            def accumulate(fn=fn, rs=rs):
                acc_ref[...] += product(fn, rs)
            if len(a_segs) == 1:
                accumulate()
            else:
                pl.when(jnp.logical_and(k >= koff[s], k < koff[s] + nkb[s]))(accumulate)

        @pl.when(k == nk - 1)
        def _():
            finish(acc_ref[...])

    return pl.pallas_call(
        body, name=name, grid=(ni, nj, nk), in_specs=in_specs, out_specs=out_specs, out_shape=out_shape,
        scratch_shapes=[] if nk == 1 else [pltpu.VMEM((tm, tn), F32)], compiler_params=_params(3),
    )(*operands)


def _tn_mm(name, a_arrs, a_fn, b_segs, *, tn, tk, out_dtype, shard_major=False, colsum=False):
    T, M = a_arrs[0].shape
    nbj = [arrs[0].shape[1] // tn for arrs, _ in b_segs]
    joff = [sum(nbj[:s]) for s in range(len(nbj))]
    nj, nk = sum(nbj), T // tk
    N = nj * tn
    assert T % tk == 0 and all(arrs[0].shape[1] % tn == 0 for arrs, _ in b_segs), name

    in_specs = [pl.BlockSpec((tk, M), lambda j, k: (k, 0)) for _ in a_arrs]
    operands = list(a_arrs)
    for s, (arrs, _) in enumerate(b_segs):
        for arr in arrs:
            in_specs.append(pl.BlockSpec(
                (tk, tn), lambda j, k, s=s: (k, jnp.clip(j - joff[s], 0, nbj[s] - 1))))
            operands.append(arr)
    if shard_major:
        per = (N // N_CHIPS) // tn
        assert per * tn * N_CHIPS == N, name
        out_shape = [jax.ShapeDtypeStruct((N_CHIPS, M, N // N_CHIPS), out_dtype)]
        out_specs = [pl.BlockSpec((1, M, tn), lambda j, k: (j // per, 0, j % per))]
    else:
        out_shape = [jax.ShapeDtypeStruct((M, N), out_dtype)]
        out_specs = [pl.BlockSpec((M, tn), lambda j, k: (0, j))]
    if colsum:
        out_shape.append(jax.ShapeDtypeStruct((1, N), F32))
        out_specs.append(pl.BlockSpec((1, tn), lambda j, k: (0, j)))
    n_a = len(a_arrs)
    n_seg_refs = [len(arrs) for arrs, _ in b_segs]

    def body(*refs):
        a_refs = refs[:n_a]
        pos = n_a
        seg_refs = []
        for n in n_seg_refs:
            seg_refs.append(refs[pos:pos + n])
            pos += n
        o_ref = refs[pos]
        cs_ref = refs[pos + 1] if colsum else None
        acc_ref = refs[-1]
        j, k = pl.program_id(0), pl.program_id(1)

        @pl.when(k == 0)
        def _():
            acc_ref[...] = jnp.zeros_like(acc_ref)
            if colsum:
                cs_ref[...] = jnp.zeros_like(cs_ref)

        for s, ((_, fn), rs) in enumerate(zip(b_segs, seg_refs)):
            def accumulate(fn=fn, rs=rs):
                a = a_refs[0][...] if a_fn is None else a_fn(*[r[...] for r in a_refs])
                bt = rs[0][...] if fn is None else fn(*[r[...] for r in rs])
                acc_ref[...] += _dot(a, bt, TN)
                if colsum:
                    cs_ref[...] += jnp.sum(bt.astype(F32), axis=0, keepdims=True)
            if len(b_segs) == 1:
                accumulate()
            else:
                pl.when(jnp.logical_and(j >= joff[s], j < joff[s] + nbj[s]))(accumulate)

        @pl.when(k == nk - 1)
        def _():
            if shard_major:
                o_ref[0] = acc_ref[...].astype(o_ref.dtype)
            else:
                o_ref[...] = acc_ref[...].astype(o_ref.dtype)

    res = pl.pallas_call(
        body, name=name, grid=(nj, nk), in_specs=in_specs, out_specs=out_specs, out_shape=out_shape,
        scratch_shapes=[pltpu.VMEM((M, tn), F32)], compiler_params=_params(2),
    )(*operands)
    return res if colsum else res[0]


def _silu(z):
    return z * _sig(z)


def _silu_grad(z):
    s = _sig(z)
    return s * (1.0 + z * (1.0 - s))


def _gate(o, z):
    return o * _silu(z)


def _ln_stats(r):
    mu = jnp.mean(r, axis=1, keepdims=True)
    xc = r - mu
    var = jnp.mean(xc * xc, axis=1, keepdims=True)
    rstd = lax.rsqrt(var + LN_EPS)
    return xc * rstd, rstd


def _ln_bwd(dy, xhat, rstd, g):
    dxh = dy * g
    m1 = jnp.mean(dxh, axis=1, keepdims=True)
    m2 = jnp.mean(dxh * xhat, axis=1, keepdims=True)
    return (rstd * (dxh - m1 - xhat * m2), jnp.sum(dy * xhat, axis=0, keepdims=True),
            jnp.sum(dy, axis=0, keepdims=True))


def _rms_fwd(x, g):
    rstd = lax.rsqrt(jnp.mean(x * x, axis=1, keepdims=True) + RMS_EPS)
    return x * rstd * g


def _rms_bwd(dy, x, g):
    rstd = lax.rsqrt(jnp.mean(x * x, axis=1, keepdims=True) + RMS_EPS)
    xn = x * rstd
    dxn = dy * g
    return rstd * (dxn - xn * jnp.mean(dxn * xn, axis=1, keepdims=True)), jnp.sum(dy * xn, axis=0, keepdims=True)


def _rope(x, cos, sin, transpose=False):
    parts = []
    for g in range(x.shape[1] // LANES):
        xg = x[:, g * LANES:(g + 1) * LANES]
        if transpose:
            parts.append(xg * cos + pltpu.roll(xg * sin, LANES // 2, 1))
        else:
            parts.append(xg * cos + pltpu.roll(xg, LANES // 2, 1) * sin)
    return parts[0] if len(parts) == 1 else jnp.concatenate(parts, axis=1)


def _shifted_rows(window, rc):
    n = window.shape[0]
    for b in range(SUBLANES):
        rolled = window if b == 0 else pltpu.roll(window, n - b, 0)
        for a8 in range(0, n - rc - b + 1, SUBLANES):
            yield a8 + b, rolled[a8:a8 + rc]


def _conv_fwd(proj, conv_w, conv_b, E, tm):
    T = proj.shape[0]
    kc = conv_w.shape[0]
    lc, hb, rc = min(CONV_LC, E), CONV_HALO, min(CONV_RC, tm)
    nl, ni, ratio = E // lc, T // tm, tm // hb
    gate_off = E // lc

    sub = min(CONV_SUB, lc)
    base = hb - (kc - 1)

    def body(val_ref, gate_ref, valh_ref, gateh_ref, w_ref, cb_ref, u1_ref, ubuf):
        i = pl.program_id(1)
        ubuf[hb:, :] = val_ref[...] * _sig(gate_ref[...])
        halo = valh_ref[...] * _sig(gateh_ref[...])
        ubuf[0:hb, :] = jnp.where(i > 0, halo, 0.0)
        for l0 in range(0, lc, sub):
            ls = slice(l0, l0 + sub)
            for r0 in range(0, tm, rc):
                acc = jnp.zeros((rc, sub), F32) + cb_ref[:, ls]
                for off, rows in _shifted_rows(ubuf[r0:r0 + hb + rc, ls], rc):
                    if 0 <= off - base < kc:
                        acc += w_ref[off - base:off - base + 1, ls] * rows
                u1_ref[r0:r0 + rc, ls] = acc

    return pl.pallas_call(
        body, name="conv_fwd", grid=(nl, ni),
        in_specs=[
            pl.BlockSpec((tm, lc), lambda l, i: (i, l)),
            pl.BlockSpec((tm, lc), lambda l, i: (i, gate_off + l)),
            pl.BlockSpec((hb, lc), lambda l, i: (jnp.maximum(i * ratio - 1, 0), l)),
            pl.BlockSpec((hb, lc), lambda l, i: (jnp.maximum(i * ratio - 1, 0), gate_off + l)),
            pl.BlockSpec((kc, lc), lambda l, i: (0, l)),
            pl.BlockSpec((1, lc), lambda l, i: (0, l)),
        ],
        out_specs=pl.BlockSpec((tm, lc), lambda l, i: (i, l)),
        out_shape=jax.ShapeDtypeStruct((T, E), F32),
        scratch_shapes=[pltpu.VMEM((hb + tm, lc), F32)], compiler_params=_params(2),
    )(proj, proj, proj, proj, conv_w, conv_b)


def _conv_post(u1, proj, norm_g, norm_b, E, tm):
    T = u1.shape[0]

    def body(u1_ref, z_ref, g_ref, b_ref, u4_ref):
        xhat, _ = _ln_stats(u1_ref[...])
        u4_ref[...] = (_silu(xhat * g_ref[...] + b_ref[...]) * _silu(z_ref[...])).astype(BF16)

    return pl.pallas_call(
        body, name="conv_post", grid=(T // tm,),
        in_specs=[pl.BlockSpec((tm, E), lambda i: (i, 0)), pl.BlockSpec((tm, E), lambda i: (i, 2)),
                  pl.BlockSpec((1, E), lambda i: (0, 0)), pl.BlockSpec((1, E), lambda i: (0, 0))],
        out_specs=pl.BlockSpec((tm, E), lambda i: (i, 0)),
        out_shape=jax.ShapeDtypeStruct((T, E), BF16), compiler_params=_params(1),
    )(u1, proj, norm_g, norm_b)


def _conv_bwd(du1, proj, conv_w, E, tm):
    T = du1.shape[0]
    kc = conv_w.shape[0]
    lc, hb, rc = min(CONV_LC, E), CONV_HALO, min(CONV_RC, tm)
    nl, ni, ratio = E // lc, T // tm, tm // hb
    gate_off = E // lc
    last_halo = T // hb - 1

    sub = min(CONV_SUB, lc)
    base = hb - (kc - 1)

    def body(du_ref, dun_ref, val_ref, gate_ref, valh_ref, gateh_ref, w_ref,
             dval_ref, dgate_ref, dw_ref, db_ref, ubuf, dbuf, sbuf, dw_sc):
        i = pl.program_id(1)
        sbuf[...] = _sig(gate_ref[...])
        ubuf[hb:, :] = val_ref[...] * sbuf[...]
        halo = valh_ref[...] * _sig(gateh_ref[...])
        ubuf[0:hb, :] = jnp.where(i > 0, halo, 0.0)
        dbuf[0:tm, :] = du_ref[...]
        dbuf[tm:, :] = jnp.where(i < ni - 1, dun_ref[...], 0.0)

        @pl.when(i == 0)
        def _():
            dw_sc[...] = jnp.zeros_like(dw_sc)
            db_ref[...] = jnp.zeros_like(db_ref)

        db_ref[...] += jnp.sum(du_ref[...], axis=0, keepdims=True)
        for l0 in range(0, lc, sub):
            ls = slice(l0, l0 + sub)
            for r0 in range(0, tm, rc):
                dwin = dbuf[r0:r0 + rc + hb, ls]
                dchunk = dwin[0:rc]
                for off, rows in _shifted_rows(ubuf[r0:r0 + hb + rc, ls], rc):
                    k = off - base
                    if 0 <= k < kc:
                        prod = rows * dchunk
                        part = prod[0:SUBLANES]
                        for s8 in range(SUBLANES, rc, SUBLANES):
                            part = part + prod[s8:s8 + SUBLANES]
                        dw_sc[k, :, ls] += part
                acc = jnp.zeros((rc, sub), F32)
                for off, rows in _shifted_rows(dwin, rc):
                    k = (kc - 1) - off
                    if 0 <= k < kc:
                        acc += w_ref[k:k + 1, ls] * rows
                v, s = val_ref[r0:r0 + rc, ls], sbuf[r0:r0 + rc, ls]
                dval_ref[r0:r0 + rc, ls] = (acc * s).astype(BF16)
                dgate_ref[r0:r0 + rc, ls] = (acc * v * s * (1.0 - s)).astype(BF16)

        @pl.when(i == ni - 1)
        def _():
            for k in range(kc):
                dw_ref[k:k + 1, :] = jnp.sum(dw_sc[k], axis=0, keepdims=True)

    return pl.pallas_call(
        body, name="conv_bwd", grid=(nl, ni),
        in_specs=[
            pl.BlockSpec((tm, lc), lambda l, i: (i, l)),
            pl.BlockSpec((hb, lc), lambda l, i: (jnp.minimum((i + 1) * ratio, last_halo), l)),
            pl.BlockSpec((tm, lc), lambda l, i: (i, l)),
            pl.BlockSpec((tm, lc), lambda l, i: (i, gate_off + l)),
            pl.BlockSpec((hb, lc), lambda l, i: (jnp.maximum(i * ratio - 1, 0), l)),
            pl.BlockSpec((hb, lc), lambda l, i: (jnp.maximum(i * ratio - 1, 0), gate_off + l)),
            pl.BlockSpec((kc, lc), lambda l, i: (0, l)),
        ],
        out_specs=[pl.BlockSpec((tm, lc), lambda l, i: (i, l)), pl.BlockSpec((tm, lc), lambda l, i: (i, l)),
                   pl.BlockSpec((kc, lc), lambda l, i: (0, l)), pl.BlockSpec((1, lc), lambda l, i: (0, l))],
        out_shape=[jax.ShapeDtypeStruct((T, E), BF16), jax.ShapeDtypeStruct((T, E), BF16),
                   jax.ShapeDtypeStruct((kc, E), F32), jax.ShapeDtypeStruct((1, E), F32)],
        scratch_shapes=[pltpu.VMEM((hb + tm, lc), F32), pltpu.VMEM((tm + hb, lc), F32),
                        pltpu.VMEM((tm, lc), F32), pltpu.VMEM((kc, SUBLANES, lc), F32)],
        compiler_params=_params(2),
    )(du1, du1, proj, proj, proj, proj, conv_w)


def _norm_prep(pb, kv_g, q_g, cos, sin, rkv, rq, wk, tm):
    T = pb.shape[0]

    def body(ckv_ref, cq_ref, kg_ref, qg_ref, cos_ref, sin_ref, c_ref, kr_ref, cqn_ref):
        blk = ckv_ref[...]
        c_ref[...] = _rms_fwd(blk[:, :rkv], kg_ref[...]).astype(BF16)
        kr_ref[...] = _rope(blk[:, rkv:rkv + LANES], cos_ref[...], sin_ref[...]).astype(BF16)
        cqn_ref[...] = _rms_fwd(cq_ref[...], qg_ref[...]).astype(BF16)

    return pl.pallas_call(
        body, name="norm_prep", grid=(T // tm,),
        in_specs=[pl.BlockSpec((tm, wk), lambda i: (i, 0)), pl.BlockSpec((tm, rq), lambda i: (i, wk // rq)),
                  pl.BlockSpec((1, rkv), lambda i: (0, 0)), pl.BlockSpec((1, rq), lambda i: (0, 0)),
                  pl.BlockSpec((tm, LANES), lambda i: (i, 0)), pl.BlockSpec((tm, LANES), lambda i: (i, 0))],
        out_specs=[pl.BlockSpec((tm, rkv), lambda i: (i, 0)), pl.BlockSpec((tm, LANES), lambda i: (i, 0)),
                   pl.BlockSpec((tm, rq), lambda i: (i, 0))],
        out_shape=[jax.ShapeDtypeStruct((T, rkv), BF16), jax.ShapeDtypeStruct((T, LANES), BF16),
                   jax.ShapeDtypeStruct((T, rq), BF16)],
        compiler_params=_params(1),
    )(pb, pb, kv_g, q_g, cos, sin)


def _attn_fwd(q_all, kv, kr, H, tq, scale):
    T = q_all.shape[0]
    nq = T // tq
    pair = 2
    W = pair * LANES
    assert H % pair == 0
    hp_n = H // pair

    def body(qn_ref, qr_ref, kn_ref, kr_ref, v_ref, o_ref, lse_ref, *scratch):
        qi = pl.program_id(1)
        chains = [scratch[4 * a:4 * a + 4] for a in range(pair)]
        lanes = [slice(a * LANES, (a + 1) * LANES) for a in range(pair)]
        groups = [slice(c * LANES, (c + 1) * LANES) for c in range(tq // LANES)]

        def fold(x, op):
            r = x[:, groups[0]]
            for gsl in groups[1:]:
                r = op(r, x[:, gsl])
            return r

        for _, m_sc, l_sc, acc_sc in chains:
            m_sc[...] = jnp.full_like(m_sc, MASK_VALUE)
            l_sc[...] = jnp.zeros_like(l_sc)
            acc_sc[...] = jnp.zeros_like(acc_sc)

        def scores(j, masked):
            rows = pl.ds(pl.multiple_of(j * tq, tq), tq)
            krope = kr_ref[rows, :]
            for a, (s_sc, m_sc, _, _) in enumerate(chains):
                q = jnp.concatenate([qn_ref[:, lanes[a]], qr_ref[:, lanes[a]]], axis=1)
                k = jnp.concatenate([kn_ref[rows, lanes[a]], krope], axis=1)
                s = _dot(q, k, NT) * scale
                if masked:
                    row = lax.broadcasted_iota(jnp.int32, s.shape, 0)
                    col = lax.broadcasted_iota(jnp.int32, s.shape, 1)
                    s = jnp.where(col <= row, s, MASK_VALUE)
                s_sc[j] = s
                m_sc[...] = jnp.maximum(m_sc[...], fold(s, jnp.maximum))

        def unmasked(j, carry):
            scores(j, False)
            return carry

        lax.fori_loop(0, qi, unmasked, 0)
        scores(qi, True)
        for _, m_sc, _, _ in chains:
            m_sc[...] = jnp.broadcast_to(jnp.max(m_sc[...], axis=1, keepdims=True), m_sc.shape)

        def weigh(j, carry):
            rows = pl.ds(pl.multiple_of(j * tq, tq), tq)
            for a, (s_sc, m_sc, l_sc, acc_sc) in enumerate(chains):
                s, m = s_sc[j], m_sc[...]
                p = jnp.concatenate([jnp.exp(s[:, gsl] - m) for gsl in groups], axis=1)
                l_sc[...] += fold(p, jnp.add)
                acc_sc[...] += _dot(p, v_ref[rows, lanes[a]], NN)
            return carry

        lax.fori_loop(0, qi + 1, weigh, 0)
        for a, (_, m_sc, l_sc, acc_sc) in enumerate(chains):
            l = jnp.sum(l_sc[...], axis=1, keepdims=True)
            o_ref[:, lanes[a]] = acc_sc[...] / l
            lse_ref[a] = m_sc[:, 0:1] + jnp.log(l)

    chain_scratch = [pltpu.VMEM((nq, tq, tq), F32), pltpu.VMEM((tq, LANES), F32), pltpu.VMEM((tq, LANES), F32),
                     pltpu.VMEM((tq, LANES), F32)]
    return pl.pallas_call(
        body, name="attn_fwd", grid=(hp_n, nq),
        in_specs=[pl.BlockSpec((tq, W), lambda hp, qi: (qi, hp)),
                  pl.BlockSpec((tq, W), lambda hp, qi: (qi, hp_n + hp)),
                  pl.BlockSpec((T, W), lambda hp, qi: (0, hp)),
                  pl.BlockSpec((T, LANES), lambda hp, qi: (0, 0)),
                  pl.BlockSpec((T, W), lambda hp, qi: (0, hp_n + hp))],
        out_specs=[pl.BlockSpec((tq, W), lambda hp, qi: (qi, hp)),
                   pl.BlockSpec((pair, tq, 1), lambda hp, qi: (hp, qi, 0))],
        out_shape=[jax.ShapeDtypeStruct((T, H * LANES), F32), jax.ShapeDtypeStruct((H, T, 1), F32)],
        scratch_shapes=chain_scratch * pair, compiler_params=_params(2),
    )(q_all, q_all, kv, kr, kv)


def _attn_bwd(q_all, kv, kr, do, o, lse, H, tq, scale):
    T = q_all.shape[0]
    nq = T // tq
    HV = H * LANES
    pair = 2
    tk2 = pair * tq
    ng = T // tk2
    assert ng * tk2 == T

    def body(qn_ref, qr_ref, kn_ref, kr_ref, v_ref, do_ref, o_ref, lse_ref,
             dqn_ref, dqr_ref, dkn_ref, dkr_ref, dv_ref, dq_sc, dk_sc, dv_sc):
        g = pl.program_id(1)

        @pl.when(g == 0)
        def _():
            dq_sc[...] = jnp.zeros_like(dq_sc)

        dk_sc[...] = jnp.zeros_like(dk_sc)
        dv_sc[...] = jnp.zeros_like(dv_sc)

        def block(qi, modes):
            rows = pl.ds(pl.multiple_of(qi * tq, tq), tq)
            q = jnp.concatenate([qn_ref[rows, :], qr_ref[rows, :]], axis=1)
            dov = do_ref[rows, :]
            delta = jnp.sum(dov.astype(F32) * o_ref[rows, :], axis=1, keepdims=True)
            lse_q = lse_ref[0, rows, :]
            dq = None
            for c, masked in enumerate(modes):
                if masked is None:
                    continue
                kr_ = slice(c * tq, (c + 1) * tq)
                k = jnp.concatenate([kn_ref[kr_, :], kr_ref[kr_, :]], axis=1)
                s = _dot(q, k, NT) * scale
                if masked:
                    row = lax.broadcasted_iota(jnp.int32, s.shape, 0)
                    col = lax.broadcasted_iota(jnp.int32, s.shape, 1)
                    s = jnp.where(col <= row, s, MASK_VALUE)
                p = jnp.exp(s - lse_q)
                dv_sc[kr_, :] += _dot(p, dov, TN)
                dp = _dot(dov, v_ref[kr_, :], NT)
                ds = (p * (dp - delta) * scale).astype(BF16)
                dk_sc[kr_, :] += _dot(ds, q, TN)
                part = _dot(ds, k, NN)
                dq = part if dq is None else dq + part
            dq_sc[rows, :] += dq

        block(pair * g, (True, None))
        block(pair * g + 1, (False, True))

        def below(qi, carry):
            block(qi, (False, False))
            return carry

        lax.fori_loop(pair * g + pair, nq, below, 0)
        dkn_ref[...] = dk_sc[:, :LANES].astype(BF16)
        dkr_ref[...] = dk_sc[:, LANES:]
        dv_ref[...] = dv_sc[...].astype(BF16)

        @pl.when(g == ng - 1)
        def _():
            dqn_ref[...] = dq_sc[:, :LANES].astype(BF16)
            dqr_ref[...] = dq_sc[:, LANES:]

    whole = lambda col: pl.BlockSpec((T, LANES), col)
    tile = lambda col: pl.BlockSpec((tk2, LANES), col)
    return pl.pallas_call(
        body, name="attn_bwd", grid=(H, ng),
        in_specs=[whole(lambda h, g: (0, h)), whole(lambda h, g: (0, H + h)),
                  tile(lambda h, g: (g, h)), tile(lambda h, g: (g, 0)), tile(lambda h, g: (g, H + h)),
                  whole(lambda h, g: (0, h)), whole(lambda h, g: (0, h)),
                  pl.BlockSpec((1, T, 1), lambda h, g: (h, 0, 0))],
        out_specs=[whole(lambda h, g: (0, h)), whole(lambda h, g: (0, h)),
                   tile(lambda h, g: (g, h)), tile(lambda h, g: (g, h)), tile(lambda h, g: (g, h))],
        out_shape=[jax.ShapeDtypeStruct((T, HV), BF16), jax.ShapeDtypeStruct((T, HV), F32),
                   jax.ShapeDtypeStruct((T, HV), BF16), jax.ShapeDtypeStruct((T, HV), F32),
                   jax.ShapeDtypeStruct((T, HV), BF16)],
        scratch_shapes=[pltpu.VMEM((T, 2 * LANES), F32), pltpu.VMEM((tk2, 2 * LANES), F32),
                        pltpu.VMEM((tk2, LANES), F32)],
        compiler_params=_params(2),
    )(q_all, q_all, kv, kr, kv, do, o, lse)


def _rope_bwd(dqr, dkr_heads, cos, sin, H, tm):
    T, HV = dqr.shape

    def body(dqr_ref, dkr_ref, cos_ref, sin_ref, dq_ref, dk_ref):
        c, s = cos_ref[...], sin_ref[...]
        dq_ref[...] = _rope(dqr_ref[...], c, s, transpose=True).astype(BF16)
        dk = dkr_ref[...]
        tot = dk[:, 0:LANES]
        for h in range(1, H):
            tot = tot + dk[:, h * LANES:(h + 1) * LANES]
        dk_ref[...] = _rope(tot, c, s, transpose=True)

    return pl.pallas_call(
        body, name="rope_bwd", grid=(T // tm,),
        in_specs=[pl.BlockSpec((tm, HV), lambda i: (i, 0)), pl.BlockSpec((tm, HV), lambda i: (i, 0)),
                  pl.BlockSpec((tm, LANES), lambda i: (i, 0)), pl.BlockSpec((tm, LANES), lambda i: (i, 0))],
        out_specs=[pl.BlockSpec((tm, HV), lambda i: (i, 0)), pl.BlockSpec((tm, LANES), lambda i: (i, 0))],
        out_shape=[jax.ShapeDtypeStruct((T, HV), BF16), jax.ShapeDtypeStruct((T, LANES), F32)],
        compiler_params=_params(1),
    )(dqr, dkr_heads, cos, sin)


def _place():
    x, y, c = lax.axis_index("x"), lax.axis_index("y"), lax.axis_index("c")
    chips = [(1 - x, y), (x, 1 - y), (1 - x, 1 - y)]
    return x, y, c, chips


def _all_gather_chips(pack):
    rows = pack.shape[0]
    half = rows // 2
    assert half * 2 == rows and half % BF16_ROWS == 0

    def body(w_ref, out_ref, send_sems, recv_sems, local_sem):
        x, y, c, chips = _place()
        sibling = (x, y, 1 - c)

        def region(px, py, pc):
            return out_ref.at[2 * px + py, pl.ds(pc * half, half), :]

        def copy(k, block, to, src=None):
            return pltpu.make_async_remote_copy(
                src_ref=region(*block) if src is None else src, dst_ref=region(*block),
                send_sem=send_sems.at[k], recv_sem=recv_sems.at[k], device_id=to, device_id_type=MESH)

        mine = pltpu.make_async_copy(w_ref, out_ref.at[2 * x + y], local_sem)
        mine.start()
        my_half = w_ref.at[pl.ds(c * half, half), :]
        first = [copy(j, (x, y, c), (*chip, c), src=my_half) for j, chip in enumerate(chips)]
        for cp in first:
            cp.start()
        passed = [copy(3 + j, (*chip, c), sibling) for j, chip in enumerate(chips)]
        for j, chip in enumerate(chips):
            copy(j, (*chip, c), (x, y, c)).wait_recv()
            passed[j].start()
        for j, chip in enumerate(chips):
            copy(3 + j, (*chip, 1 - c), (x, y, c)).wait_recv()
        for cp in first + passed:
            cp.wait_send()
        mine.wait()

    return pl.pallas_call(
        body, name="gather_weights", in_specs=[ANY], out_specs=ANY,
        out_shape=jax.ShapeDtypeStruct((N_CHIPS,) + pack.shape, pack.dtype),
        scratch_shapes=[pltpu.SemaphoreType.DMA((6,)), pltpu.SemaphoreType.DMA((6,)), pltpu.SemaphoreType.DMA],
    )(pack)


def _swap_cores(name, parts):
    n = len(parts)

    def body(*refs):
        p_refs, r_refs = refs[:n], refs[n:2 * n]
        send_sems, recv_sems = refs[2 * n:]
        x, y, c, _ = _place()
        copies = [pltpu.make_async_remote_copy(
            src_ref=p_refs[w], dst_ref=r_refs[w], send_sem=send_sems.at[w], recv_sem=recv_sems.at[w],
            device_id=(x, y, 1 - c), device_id_type=MESH) for w in range(n)]
        for cp in copies:
            cp.start()
        for cp in copies:
            cp.wait()

    return pl.pallas_call(
        body, name=name, in_specs=[ANY] * n, out_specs=[ANY] * n,
        out_shape=[jax.ShapeDtypeStruct(p.shape, p.dtype) for p in parts],
        scratch_shapes=[pltpu.SemaphoreType.DMA((n,)), pltpu.SemaphoreType.DMA((n,))],
    )(*parts)


HBM = pl.BlockSpec(memory_space=pltpu.HBM)
SEM = pl.BlockSpec(memory_space=pltpu.SEMAPHORE)
EFFECT = pltpu.SideEffectType.DATAFLOW_SIDE_EFFECTING


def _push_copies(a_refs, l_refs, send_sems, recv_sems, by_target):
    x, y, c, chips = _place()
    me = 2 * x + y
    out = []
    for w, (a_ref, l_ref) in enumerate(zip(a_refs, l_refs)):
        for j, (px, py) in enumerate(chips):
            peer = 2 * px + py
            out.append((
                pltpu.make_async_remote_copy(
                    src_ref=a_ref.at[peer] if by_target else a_ref, dst_ref=l_ref.at[me],
                    send_sem=send_sems.at[3 * w + j], recv_sem=recv_sems.at[3 * w + j],
                    device_id=(px, py, c), device_id_type=MESH),
                pltpu.make_async_remote_copy(
                    src_ref=a_ref.at[me] if by_target else a_ref, dst_ref=l_ref.at[peer],
                    send_sem=send_sems.at[3 * w + j], recv_sem=recv_sems.at[3 * w + j],
                    device_id=(px, py, c), device_id_type=MESH)))
    return out


def _push_start(name, arrs, by_target):
    n = len(arrs)
    lands = [lax.empty((N_CHIPS,) + (a.shape[1:] if by_target else a.shape), a.dtype) for a in arrs]

    def body(*refs):
        a_refs, l_refs = refs[:n], refs[n:2 * n]
        send_sems, recv_sems = refs[2 * n], refs[2 * n + 1]
        token = refs[-1]
        for send, _ in _push_copies(a_refs, l_refs, send_sems, recv_sems, by_target):
            send.start()
        token[...] = jnp.zeros_like(token)

    res = pl.pallas_call(
        body, name=name,
        out_shape=(pltpu.SemaphoreType.DMA((3 * n,)), pltpu.SemaphoreType.DMA((3 * n,)),
                   *[pltpu.HBM(a.shape, a.dtype) for a in arrs], *[pltpu.HBM(l.shape, l.dtype) for l in lands],
                   jax.ShapeDtypeStruct((8, LANES), F32)),
        in_specs=[HBM] * (2 * n), out_specs=(SEM, SEM, *[HBM] * (2 * n), pl.BlockSpec(memory_space=pltpu.VMEM)),
        input_output_aliases={i: 2 + i for i in range(2 * n)},
        compiler_params=pltpu.CompilerParams(has_side_effects=EFFECT),
    )(*[pltpu.with_memory_space_constraint(a, pltpu.HBM) for a in list(arrs) + lands])
    return res[0], res[1], list(res[2:2 + n]), list(res[2 + n:2 + 2 * n]), res[-1]


def _push_wait(name, send_sems, recv_sems, arrs, lands, after, by_target):
    n = len(arrs)

    def body(*refs):
        a_refs, l_refs = refs[:n], refs[n:2 * n]
        s_sems, r_sems = refs[2 * n], refs[2 * n + 1]
        for send, recv in _push_copies(a_refs, l_refs, s_sems, r_sems, by_target):
            send.wait_send()
            recv.wait_recv()

    res = pl.pallas_call(
        body, name=name,
        out_shape=[pltpu.HBM(a.shape, a.dtype) for a in list(arrs) + list(lands)],
        in_specs=[HBM] * (2 * n) + [SEM, SEM] + [ANY] * len(after), out_specs=[HBM] * (2 * n),
        input_output_aliases={i: i for i in range(2 * n)},
        compiler_params=pltpu.CompilerParams(has_side_effects=EFFECT),
    )(*arrs, *lands, send_sems, recv_sems, *after)
    return list(res[:n]), list(res[n:])


def _all_reduce_small(part):
    n_dev = 8

    def body(p_ref, out_ref, buf, send_sems, recv_sems):
        x, y, c, _ = _place()
        me = 4 * x + 2 * y + c
        buf[me] = p_ref[...]
        copies = []
        for k in range(1, n_dev):
            kx, ky, kc = (k >> 2) & 1, (k >> 1) & 1, k & 1
            peer = (x ^ kx, y ^ ky, c ^ kc)
            cp = pltpu.make_async_remote_copy(
                src_ref=p_ref, dst_ref=buf.at[me], send_sem=send_sems.at[k - 1], recv_sem=recv_sems.at[k - 1],
                device_id=peer, device_id_type=MESH)
            cp.start()
            copies.append(cp)
        for k in range(1, n_dev):
            pltpu.make_async_remote_copy(
                src_ref=p_ref, dst_ref=buf.at[me ^ k], send_sem=send_sems.at[k - 1], recv_sem=recv_sems.at[k - 1],
                device_id=(x, y, c), device_id_type=MESH).wait_recv()
        for cp in copies:
            cp.wait_send()
        tot = buf[0]
        for d in range(1, n_dev):
            tot = tot + buf[d]
        out_ref[...] = tot

    vm = pl.BlockSpec(memory_space=pltpu.VMEM)
    return pl.pallas_call(
        body, name="all_reduce_small", in_specs=[vm], out_specs=vm,
        out_shape=jax.ShapeDtypeStruct(part.shape, F32),
        scratch_shapes=[pltpu.VMEM((n_dev,) + part.shape, F32), pltpu.SemaphoreType.DMA((n_dev - 1,)),
                        pltpu.SemaphoreType.DMA((n_dev - 1,))],
    )(part)


def _row_tiles(shape):
    ax = next(d for d, s in enumerate(shape) if s > 1)
    tr = _gcd(ADAM_ROWS, shape[ax])
    block = tuple(tr if d == ax else s for d, s in enumerate(shape))
    return shape[ax] // tr, block, lambda i: tuple(i if d == ax else 0 for d in range(len(shape)))


def _sum_chips(name, r):
    shape = r.shape[1:]
    steps, block, index = _row_tiles(shape)

    def body(r_ref, o_ref):
        tot = r_ref[0].astype(F32)
        for q in range(1, N_CHIPS):
            tot = tot + r_ref[q].astype(F32)
        o_ref[...] = tot

    return pl.pallas_call(
        body, name=name, grid=(steps,),
        in_specs=[pl.BlockSpec((N_CHIPS,) + block, lambda i: (0,) + index(i))],
        out_specs=pl.BlockSpec(block, index),
        out_shape=jax.ShapeDtypeStruct(shape, F32), compiler_params=_params(1),
    )(r)


def _adamw_math(g, w, m, v):
    mn = ADAM_B1 * m + (1.0 - ADAM_B1) * g
    vn = ADAM_B2 * v + (1.0 - ADAM_B2) * jnp.square(g)
    m_hat = mn / (1.0 - ADAM_B1 ** ADAM_STEP)
    v_hat = vn / (1.0 - ADAM_B2 ** ADAM_STEP)
    return -ADAM_LR * (m_hat / (jnp.sqrt(v_hat) + ADAM_EPS) + ADAM_WD * w), mn, vn


def _adamw(name, g_parts, w, m, v):
    steps, block, index = _row_tiles(w.shape)
    n = len(g_parts)

    def body(*refs):
        g = refs[0][...]
        for r in refs[1:n]:
            g = g + r[...]
        w_ref, m_ref, v_ref, go_ref, d_ref, mo_ref, vo_ref = refs[n:]
        go_ref[...] = g
        d_ref[...], mo_ref[...], vo_ref[...] = _adamw_math(g, w_ref[...], m_ref[...], v_ref[...])

    spec = pl.BlockSpec(block, index)
    return pl.pallas_call(
        body, name=name, grid=(steps,), in_specs=[spec] * (n + 3), out_specs=[spec] * 4,
        out_shape=[jax.ShapeDtypeStruct(w.shape, F32)] * 4, compiler_params=_params(1),
    )(*g_parts, w, m, v)


def _adamw_vectors(items):
    n = len(items)

    def body(*refs):
        ins, outs = refs[:4 * n], refs[4 * n:]
        for k in range(n):
            g, w, m, v = (r[...] for r in ins[4 * k:4 * k + 4])
            outs[3 * k][...], outs[3 * k + 1][...], outs[3 * k + 2][...] = _adamw_math(g, w, m, v)

    vm = pl.BlockSpec(memory_space=pltpu.VMEM)
    res = pl.pallas_call(
        body, name="adamw_vectors", in_specs=[vm] * (4 * n), out_specs=[vm] * (3 * n),
        out_shape=[jax.ShapeDtypeStruct(it[1].shape, F32) for it in items for _ in range(3)],
    )(*[a for it in items for a in it])
    return [res[3 * k:3 * k + 3] for k in range(n)]


def _pack_rows(flat, dtype, multiple):
    n = flat.shape[0]
    total = -(-n // multiple) * multiple
    return jnp.pad(flat, (0, total - n)).astype(dtype).reshape(total // LANES, LANES)


def kernel(x, positions, ln_g, ln_b, a_w_in, a_b_in, a_conv_w, a_conv_b, a_norm_g, a_norm_b, a_w_out, a_b_out, kv_w_down, kv_norm_g, kv_w_uk, kv_w_uv, b_w_in, b_q_norm_g, b_w_uq, b_w_out, loss_target, m_ln_g, m_ln_b, m_a_w_in, m_a_b_in, m_a_conv_w, m_a_conv_b, m_a_norm_g, m_a_norm_b, m_a_w_out, m_a_b_out, m_kv_w_down, m_kv_norm_g, m_kv_w_uk, m_kv_w_uv, m_b_w_in, m_b_q_norm_g, m_b_w_uq, m_b_w_out, v_ln_g, v_ln_b, v_a_w_in, v_a_b_in, v_a_conv_w, v_a_conv_b, v_a_norm_g, v_a_norm_b, v_a_w_out, v_a_b_out, v_kv_w_down, v_kv_norm_g, v_kv_w_uk, v_kv_w_uv, v_b_w_in, v_b_q_norm_g, v_b_w_uq, v_b_w_out):
    T, D = x.shape[1], x.shape[2]
    E = N_CHIPS * a_w_out.shape[1]
    KC = a_conv_w.shape[1]
    RKV = kv_norm_g.shape[0]
    H, DN = kv_w_uk.shape[1], kv_w_uk.shape[2]
    RQ = b_q_norm_g.shape[1]
    HV = N_CHIPS * b_w_out.shape[1]
    assert DN == LANES and kv_w_uv.shape[2] == LANES and HV == H * LANES
    assert kv_w_down.shape[1] == RKV + ROPE_DIM and b_w_uq.shape[3] == DN + ROPE_DIM
    assert ln_g.shape[0] == 2 and a_w_in.shape[0] == 1 and b_w_in.shape[0] == 1
    alpha = (2.0 * ln_g.shape[0]) ** 0.25
    scale = 1.0 / math.sqrt(DN + ROPE_DIM)
    WK = -(-(RKV + LANES) // 256) * 256
    assert WK % RQ == 0
    Z_OFF = WK + RQ
    tmw, tq = min(TM_WIDE, T), min(TQ, T)
    t512, t1024 = _fit(512, T), _fit(1024, T)
    xs = x[0]
    tgt = loss_target[0]
    px, py = lax.axis_index("x"), lax.axis_index("y")
    chip = 2 * px + py

    mats = [a_w_out[0], kv_w_down, kv_w_uk, kv_w_uv, b_w_in[0], b_w_uq[0], b_w_out[0]]
    vecs = [a_b_in[0], a_conv_w[0], a_conv_b[0], a_norm_g[0], a_norm_b[0], a_b_out[0]]
    first = jnp.concatenate(
        [a_w_in[0].astype(BF16).reshape(-1)]
        + [lax.bitcast_convert_type(w.reshape(-1), BF16).reshape(-1) for w in vecs])
    rest = [w.astype(BF16) for w in mats]
    gathered = _all_gather_chips(_pack_rows(first, BF16, 2 * BF16_ROWS * LANES)).reshape(N_CHIPS, -1)
    gathered, rest = lax.optimization_barrier((gathered, rest))
    rest_sems = _push_start("gather_rest_start", rest, by_target=False)
    g_win = gathered[:, :a_w_in[0].size].reshape((N_CHIPS,) + a_w_in[0].shape)
    off = a_w_in[0].size
    fvec = []
    for w in vecs:
        bits = gathered[:, off:off + 2 * w.size].reshape((N_CHIPS,) + w.shape + (2,))
        fvec.append(lax.bitcast_convert_type(bits, F32))
        off += 2 * w.size
    cols = lambda g: jnp.moveaxis(g, 0, -2).reshape(g.shape[1:-1] + (N_CHIPS * g.shape[-1],))
    w_in = cols(g_win)
    b_in = cols(fvec[0][:, None, :])
    conv_w = cols(fvec[1])
    conv_b, norm_g, norm_b, b_out = (cols(f[:, None, :]) for f in fvec[2:])
    row = lambda a: a.reshape(1, -1)
    g0, b0, g1, b1 = row(ln_g[0]), row(ln_b[0]), row(ln_g[1]), row(ln_b[1])
    kv_g, q_g = row(kv_norm_g), row(b_q_norm_g[0])
    plain = lambda acc, ins, i, j: [acc]

    b_in = b_in + rest_sems[4][0, 0]
    (proj,) = _row_mm("a_in", [((xs,), None)], w_in, nt=False, tm=t1024, tn=_fit(1536, 3 * E), tk=_fit(1024, D),
                      outs=[((T, 3 * E), F32, 'tile')], epi=lambda acc, ins, i, j: [acc + ins[0]],
                      epi_ins=[(b_in, 'col')])
    u1 = _conv_fwd(proj, conv_w, conv_b, E, tmw)
    u4 = _conv_post(u1, proj, norm_g, norm_b, E, tmw)

    rest, landed = _push_wait("gather_rest_wait", *rest_sems[:4], after=[u4], by_target=False)
    g_wout, g_wd, g_uk, g_uv, g_wbin, g_wuq, g_wbout = [
        lax.dynamic_update_slice(l, w[None], (chip,) + (0,) * w.ndim) for w, l in zip(rest, landed)]
    w_out = g_wout.reshape(E, D)
    wd = g_wd.reshape(D, RKV + ROPE_DIM)
    zpad = jnp.zeros((D, ROPE_HALF), BF16)
    wd_p = jnp.concatenate(
        [wd[:, :RKV], wd[:, RKV:RKV + ROPE_HALF], zpad, wd[:, RKV + ROPE_HALF:], zpad,
         jnp.zeros((D, WK - RKV - LANES), BF16)], axis=1)
    w_bin = cols(g_wbin)
    w_z = w_bin[:, RQ:]
    wb_small = jnp.concatenate([wd_p, w_bin[:, :RQ]], axis=1)
    wb_all = jnp.concatenate([wd_p, w_bin], axis=1)
    w_kv = jnp.concatenate([g_uk.reshape(RKV, HV), g_uv.reshape(RKV, HV)], axis=1)
    wuq = g_wuq.reshape(RQ, H, DN + ROPE_DIM)
    zq = jnp.zeros((RQ, H, ROPE_HALF), BF16)
    w_qr = jnp.concatenate([wuq[:, :, DN:DN + ROPE_HALF], zq, wuq[:, :, DN + ROPE_HALF:], zq], axis=2)
    w_q = jnp.concatenate([wuq[:, :, :DN].reshape(RQ, HV), w_qr.reshape(RQ, HV)], axis=1)
    w_bout = g_wbout.reshape(HV, D)

    freqs = ROPE_THETA ** (-jnp.arange(0, ROPE_DIM, 2, dtype=F32) / ROPE_DIM)
    ang = positions[0].astype(F32)[:, None] * freqs
    cs, sn = jnp.cos(ang), jnp.sin(ang)
    ones, zeros = jnp.ones_like(cs), jnp.zeros_like(cs)
    cos_t = jnp.concatenate([cs, ones, cs, ones], axis=1)
    sin_t = jnp.concatenate([-sn, zeros, sn, zeros], axis=1)

    def ln_epi(acc, ins, i, j):
        bias, res, g, b = ins
        xhat, rstd = _ln_stats(alpha * res + acc + bias)
        h = xhat * g + b
        return [h, h, xhat, rstd]

    h1, h1b, xhat1, rstd1 = _row_mm(
        "a_out", [((u4,), None)], w_out, nt=False, tm=t512, tn=D, tk=_fit(2048, E),
        outs=[((T, D), F32, 'tile'), ((T, D), BF16, 'tile'), ((T, D), F32, 'tile'), ((T, 1), F32, 'row')],
        epi=ln_epi, epi_ins=[(b_out, 'col'), (xs, 'tile'), (g0, 'col'), (b0, 'col')])

    tkb = _fit(512, _gcd(WK, RQ, HV))
    (pb,) = _row_mm("b_in", [((h1b,), None)], wb_small, nt=False, tm=t1024, tn=_fit(1024, Z_OFF),
                    tk=_fit(1024, D), outs=[((T, Z_OFF), F32, 'tile')], epi=plain)
    (zb,) = _row_mm("b_in_gate", [((h1b,), None)], w_z, nt=False, tm=t1024, tn=_fit(2048, HV),
                    tk=_fit(1024, D), outs=[((T, HV), F32, 'tile')], epi=plain)
    c_lat, kr, cqn = _norm_prep(pb, kv_g, q_g, cos_t, sin_t, RKV, RQ, WK, tmw)
    (kv,) = _row_mm("kv_up", [((c_lat,), None)], w_kv, nt=False, tm=t1024, tn=_fit(2048, HV),
                    tk=_fit(1024, RKV), outs=[((T, 2 * HV), BF16, 'tile')], epi=plain)
    tnq = _fit(2048, HV)
    half_q = HV // tnq

    def q_epi(acc, ins, i, j):
        return [jnp.where(j >= half_q, _rope(acc, ins[0], ins[1]), acc)]

    (q_all,) = _row_mm("q_up", [((cqn,), None)], w_q, nt=False, tm=t1024, tn=tnq, tk=_fit(1024, RQ),
                       outs=[((T, 2 * HV), BF16, 'tile')], epi=q_epi,
                       epi_ins=[(cos_t, 'row'), (sin_t, 'row')])
    o, lse = _attn_fwd(q_all, kv, kr, H, tq, scale)

    def loss_epi(acc, ins, i, j):
        res, g, b, target = ins
        xhat, rstd = _ln_stats(alpha * res + acc)
        diff = xhat * g + b - target
        dr, dg, db = _ln_bwd(diff / D, xhat, rstd, g)
        return [dr, 0.5 * jnp.sum(diff * diff, keepdims=True) / D, dg, db]

    dr1, loss_part, dg1, db1 = _row_mm(
        "b_out", [((o, zb), _gate)], w_bout, nt=False, tm=tmw, tn=D, tk=_fit(2048, HV),
        outs=[((T, D), F32, 'tile'), ((1, 1), F32, 'acc'), ((1, D), F32, 'acc'), ((1, D), F32, 'acc')],
        epi=loss_epi, epi_ins=[(h1, 'tile'), (g1, 'col'), (b1, 'col'), (tgt, 'tile')])

    def gate_bwd_epi(acc, ins, i, j):
        return [acc * _silu(ins[1]), acc * ins[0] * _silu_grad(ins[1])]

    do, dz = _row_mm(
        "b_out_bwd", [((dr1,), None)], w_bout, nt=True, tm=tmw, tn=_fit(2048, HV), tk=_fit(1024, D),
        outs=[((T, HV), BF16, 'tile'), ((T, HV), BF16, 'tile')], epi=gate_bwd_epi,
        epi_ins=[(o, 'tile'), (zb, 'tile')])
    gw_bout = _tn_mm("dw_b_out", (o, zb), _gate, [((dr1,), None)], tn=_fit(1024, D), tk=t512, out_dtype=BF16)
    dqn, dqr, dkn, dkr_h, dv = _attn_bwd(q_all, kv, kr, do, o, lse, H, tq, scale)
    dqr_pre, dkr_pre = _rope_bwd(dqr, dkr_h, cos_t, sin_t, H, tmw)

    def cq_bwd_epi(acc, ins, i, j):
        dx, dg = _rms_bwd(acc, ins[0], ins[1])
        return [dx, dg]

    dcq, dqg = _row_mm(
        "q_up_bwd", [((dqn,), None), ((dqr_pre,), None)], w_q, nt=True, tm=t1024, tn=RQ, tk=_fit(2048, HV),
        outs=[((T, RQ), BF16, 'tile'), ((1, RQ), F32, 'acc')], epi=cq_bwd_epi,
        epi_ins=[(pb, pl.BlockSpec((t1024, RQ), lambda i, j, k: (i, WK // RQ))), (q_g, 'col')])
    gw_q = _tn_mm("dw_q_up", (cqn,), None, [((dqn,), None), ((dqr_pre,), None)],
                  tn=_fit(2048, HV), tk=t1024, out_dtype=BF16)

    def ckv_bwd_epi(acc, ins, i, j):
        blk, dkr_t, g = ins
        dx, dg = _rms_bwd(acc, blk[:, :RKV], g)
        parts = [dx, dkr_t]
        if WK > RKV + LANES:
            parts.append(jnp.zeros((dx.shape[0], WK - RKV - LANES), F32))
        return [jnp.concatenate(parts, axis=1), dg]

    dckv, dkvg = _row_mm(
        "kv_up_bwd", [((dkn,), None), ((dv,), None)], w_kv, nt=True, tm=t1024, tn=RKV, tk=_fit(2048, HV),
        outs=[((T, WK), BF16, pl.BlockSpec((t1024, WK), lambda i, j, k: (i, 0))), ((1, RKV), F32, 'acc')],
        epi=ckv_bwd_epi,
        epi_ins=[(pb, pl.BlockSpec((t1024, WK), lambda i, j, k: (i, 0))), (dkr_pre, 'row'), (kv_g, 'col')])
    gw_kv = _tn_mm("dw_kv_up", (c_lat,), None, [((dkn,), None), ((dv,), None)],
                   tn=_fit(2048, HV), tk=t1024, out_dtype=BF16)

    def ln1_bwd_epi(acc, ins, i, j):
        dr_up, xhat, rstd, g = ins
        dr, dg, db = _ln_bwd(alpha * dr_up + acc, xhat, rstd, g)
        return [dr, dg, db]

    dp_segs = [((dckv,), None), ((dcq,), None), ((dz,), None)]
    gw_ball = _tn_mm("dw_b_in", (h1b,), None, dp_segs, tn=tkb, tk=t1024, out_dtype=BF16)

    shard_cols = lambda g: jnp.moveaxis(g.reshape(g.shape[0], N_CHIPS, -1), 1, 0)
    shard_rows = lambda g: g.reshape(N_CHIPS, g.shape[0] // N_CHIPS, g.shape[1])
    gq = gw_q.reshape(RQ, 2, H, LANES)
    g_uq = jnp.concatenate(
        [gq[:, 0], gq[:, 1, :, :ROPE_HALF], gq[:, 1, :, 2 * ROPE_HALF:3 * ROPE_HALF]], axis=2)
    g_wd_full = jnp.concatenate(
        [gw_ball[:, :RKV], gw_ball[:, RKV:RKV + ROPE_HALF],
         gw_ball[:, RKV + 2 * ROPE_HALF:RKV + 3 * ROPE_HALF]], axis=1)
    late_names = ["kv_w_down", "kv_w_uk", "kv_w_uv", "b_w_in", "b_w_uq", "b_w_out"]
    late_w = [kv_w_down, kv_w_uk, kv_w_uv, b_w_in, b_w_uq, b_w_out]
    chip_major = [g_wd_full, gw_kv[:, :HV], gw_kv[:, HV:], shard_cols(gw_ball[:, WK:]), g_uq, gw_bout]
    late_grads = [g.reshape((N_CHIPS,) + w.shape) for g, w in zip(chip_major, late_w)]
    late_sems = _push_start("scatter_late_start", late_grads, by_target=True)

    dr0, dg0, db0 = _row_mm(
        "b_in_bwd", dp_segs, wb_all, nt=True, tm=t512, tn=D, tk=tkb,
        outs=[((T, D), F32, 'tile'), ((1, D), F32, 'acc'), ((1, D), F32, 'acc')], epi=ln1_bwd_epi,
        epi_ins=[(dr1, 'tile'), (xhat1, 'tile'), (rstd1, 'row'), (g0 + late_sems[4][0, 0], 'col')])

    def conv_branch_bwd_epi(acc, ins, i, j):
        u1_t, z, g, b = ins
        xhat, rstd = _ln_stats(u1_t)
        u2 = xhat * g + b
        du3 = acc * _silu(z)
        dz_a = acc * _silu(u2) * _silu_grad(z)
        du1, dg, db = _ln_bwd(du3 * _silu_grad(u2), xhat, rstd, g)
        return [du1, dz_a, dg, db]

    du1, dz_a, dng, dnb = _row_mm(
        "a_out_bwd", [((dr0,), None)], w_out, nt=True, tm=tmw, tn=E, tk=_fit(1024, D),
        outs=[((T, E), F32, 'tile'), ((T, E), BF16, 'tile'), ((1, E), F32, 'acc'), ((1, E), F32, 'acc')],
        epi=conv_branch_bwd_epi,
        epi_ins=[(u1, 'tile'), (proj, pl.BlockSpec((tmw, E), lambda i, j, k: (i, 2))), (norm_g, 'col'),
                 (norm_b, 'col')])
    gw_out, dbo = _tn_mm("dw_a_out", (u4,), None, [((dr0,), None)], tn=_fit(1024, D), tk=t1024,
                         out_dtype=BF16, colsum=True)
    mid_sems = _push_start("scatter_mid_start", [gw_out.reshape((N_CHIPS,) + a_w_out.shape)], by_target=True)
    dval, dgate, dcw, dcb = _conv_bwd(du1, proj, conv_w + mid_sems[4][0, 0], E, tmw)
    dproj_segs = [((dval,), None), ((dgate,), None), ((dz_a,), None)]
    gw_in, dbi = _tn_mm("dw_a_in", (xs,), None, dproj_segs, tn=_fit(2048, E), tk=t512, out_dtype=BF16,
                        colsum=True)

    def own_block_in_place(sent, landed):
        own = lax.dynamic_index_in_dim(sent, chip, 0, keepdims=True)
        return lax.dynamic_update_slice(landed, own, (chip,) + (0,) * (sent.ndim - 1))

    def reduce_and_update(tag, names_, sent, landed, w_, m_, v_):
        sums = [_sum_chips("sum_" + n, own_block_in_place(s, l)) for n, s, l in zip(names_, sent, landed)]
        theirs = _swap_cores("swap_cores_" + tag, sums)
        return {n: _adamw("adamw_" + n, [mine, other], w, m, v)
                for n, mine, other, w, m, v in zip(names_, sums, theirs, w_, m_, v_)}

    late_sent, late_landed = _push_wait("scatter_late_wait", *late_sems[:4], after=[dbi], by_target=True)
    mid_sent, mid_landed = _push_wait("scatter_mid_wait", *mid_sems[:4], after=[dbi], by_target=True)
    early_sems = _push_start("scatter_early_start", [shard_cols(gw_in).reshape((N_CHIPS,) + a_w_in.shape)],
                             by_target=True)
    (grad_x,) = _row_mm(
        "a_in_bwd", dproj_segs, w_in, nt=True, tm=t512, tn=_fit(1024, D), tk=_fit(2048, E),
        outs=[((T, D), F32, 'tile')], epi=lambda acc, ins, i, j: [alpha * ins[0] + acc + ins[1]],
        epi_ins=[(dr0, 'tile'), (jnp.zeros((1, D), F32) + early_sems[4][0, 0], 'col')])
    big_out = reduce_and_update(
        "late", ["a_w_out"] + late_names, mid_sent + late_sent, mid_landed + late_landed,
        [a_w_out] + late_w, [m_a_w_out, m_kv_w_down, m_kv_w_uk, m_kv_w_uv, m_b_w_in, m_b_w_uq, m_b_w_out],
        [v_a_w_out, v_kv_w_down, v_kv_w_uk, v_kv_w_uv, v_b_w_in, v_b_w_uq, v_b_w_out])

    small_full = [jnp.concatenate([dg0, dg1]), jnp.concatenate([db0, db1]), dbi, dcw, dcb, dng, dnb, dbo,
                  dkvg, dqg]
    sflat = jnp.concatenate([g.reshape(-1) for g in small_full])
    summed = _all_reduce_small(_pack_rows(sflat, F32, 8 * LANES)).reshape(-1)
    soff = 0
    sgrads = []
    for g in small_full:
        sgrads.append(summed[soff:soff + g.size].reshape(g.shape))
        soff += g.size
    local_cols = lambda g, n: lax.dynamic_slice_in_dim(g, chip * n, n, axis=g.ndim - 1)
    snames = ["ln_g", "ln_b", "a_b_in", "a_conv_w", "a_conv_b", "a_norm_g", "a_norm_b", "a_b_out",
              "kv_norm_g", "b_q_norm_g"]
    small_w = [ln_g, ln_b, a_b_in, a_conv_w, a_conv_b, a_norm_g, a_norm_b, a_b_out, kv_norm_g, b_q_norm_g]
    small_m = [m_ln_g, m_ln_b, m_a_b_in, m_a_conv_w, m_a_conv_b, m_a_norm_g, m_a_norm_b, m_a_b_out,
               m_kv_norm_g, m_b_q_norm_g]
    small_v = [v_ln_g, v_ln_b, v_a_b_in, v_a_conv_w, v_a_conv_b, v_a_norm_g, v_a_norm_b, v_a_b_out,
               v_kv_norm_g, v_b_q_norm_g]
    sharded = {"a_b_in", "a_conv_w", "a_conv_b", "a_norm_g", "a_norm_b", "a_b_out"}
    local_g = [(local_cols(g, w.shape[-1]) if n in sharded else g).reshape(w.shape)
               for n, g, w in zip(snames, sgrads, small_w)]
    at_least_2d = lambda a: a.reshape((1,) + a.shape) if a.ndim == 1 else a
    sres = _adamw_vectors([tuple(at_least_2d(a) for a in item)
                           for item in zip(local_g, small_w, small_m, small_v)])
    small_out = {n: [g] + [r.reshape(w.shape) for r in res]
                 for n, g, w, res in zip(snames, local_g, small_w, sres)}

    early_sent, early_landed = _push_wait(
        "scatter_early_wait", *early_sems[:4], by_target=True,
        after=[grad_x, big_out["b_w_out"][1], small_out["b_q_norm_g"][1]])
    big_out.update(reduce_and_update("early", ["a_w_in"], early_sent, early_landed,
                                     [a_w_in], [m_a_w_in], [v_a_w_in]))

    loss = lax.psum(loss_part[0, 0], ("x", "y", "c"))
    order = ["ln_g", "ln_b", "a_w_in", "a_b_in", "a_conv_w", "a_conv_b", "a_norm_g", "a_norm_b", "a_w_out",
             "a_b_out", "kv_w_down", "kv_norm_g", "kv_w_uk", "kv_w_uv", "b_w_in", "b_q_norm_g", "b_w_uq",
             "b_w_out"]
    outs = {**big_out, **small_out}
    result = [loss, grad_x[None]]
    for part in range(4):
        result += [outs[n][part] for n in order]
    return tuple(result)
```

```python
import functools
import math

import jax
import jax.numpy as jnp
from jax import lax
from jax.experimental import pallas as pl
from jax.experimental.pallas import tpu as pltpu

F32, BF16 = jnp.float32, jnp.bfloat16
NN = (((1,), (0,)), ((), ()))
NT = (((1,), (1,)), ((), ()))
TN = (((0,), (0,)), ((), ()))
MESH = pl.DeviceIdType.MESH
ANY = pl.BlockSpec(memory_space=pl.ANY)

LANES = 128
BF16_ROWS = 16
VMEM_LIMIT = 56 * 1024 * 1024
N_CHIPS = 4
LN_EPS = 1e-5
RMS_EPS = 1e-6
MASK_VALUE = -1e30
LOG2_E = math.log2(math.e)
ROPE_THETA = 10000.0
ROPE_DIM = 64
ROPE_HALF = ROPE_DIM // 2
ADAM_LR, ADAM_B1, ADAM_B2, ADAM_EPS, ADAM_WD, ADAM_STEP = 0.001, 0.9, 0.999, 1e-08, 0.01, 10

MAX_TILE = 2048
TM_WIDE = 256
TQ = 512
CONV_HALO = 32
CONV_LC = 512
CONV_SUB = 256
SUBLANES = 8
CONV_RC = 32
ADAM_ROWS = 64


def _dot(a, b, dims):
    return lax.dot_general(a.astype(BF16), b.astype(BF16), dims, preferred_element_type=F32)


def _sig(x):
    return 1.0 / (1.0 + jnp.exp(-x))


def _params(n_axes):
    return pltpu.CompilerParams(dimension_semantics=("arbitrary",) * n_axes, vmem_limit_bytes=VMEM_LIMIT)


def _gcd(*v):
    return functools.reduce(math.gcd, v)


def _fit(want, dim):
    return math.gcd(min(want, MAX_TILE), dim)


def _row_mm(name, a_segs, b, *, nt, tm, tn, tk, outs, epi, epi_ins=(), b_whole=False):
    M = a_segs[0][0][0].shape[0]
    N = b.shape[0] if nt else b.shape[1]
    nkb = [arrs[0].shape[1] // tk for arrs, _ in a_segs]
    koff = [sum(nkb[:s]) for s in range(len(nkb))]
    ni, nj, nk = M // tm, N // tn, sum(nkb)
    assert M % tm == 0 and N % tn == 0 and all(arrs[0].shape[1] % tk == 0 for arrs, _ in a_segs), name
    assert (b.shape[1] if nt else b.shape[0]) == nk * tk, name

    def spec_of(shape, kind):
        if isinstance(kind, pl.BlockSpec):
            return kind
        if kind == 'tile':
            return pl.BlockSpec((tm, tn), lambda i, j, k: (i, j))
        if kind == 'row':
            return pl.BlockSpec((tm, shape[1]), lambda i, j, k: (i, 0))
        if kind == 'col':
            return pl.BlockSpec((1, tn), lambda i, j, k: (0, j))
        assert kind == 'acc' and nj == 1, name
        return pl.BlockSpec(shape, lambda i, j, k: (0,) * len(shape))

    in_specs, operands = [], []
    for s, (arrs, _) in enumerate(a_segs):
        for arr in arrs:
            in_specs.append(pl.BlockSpec(
                (tm, tk), lambda i, j, k, s=s: (i, jnp.clip(k - koff[s], 0, nkb[s] - 1))))
            operands.append(arr)
    if b_whole:
        assert nt and nj == 1 and all(n == 1 for n in nkb), name
        in_specs.append(pl.BlockSpec(b.shape, lambda i, j, k: (0, 0)))
    else:
        in_specs.append(pl.BlockSpec((tn, tk), lambda i, j, k: (j, k)) if nt
                        else pl.BlockSpec((tk, tn), lambda i, j, k: (k, j)))
    operands.append(b)
    for arr, kind in epi_ins:
        in_specs.append(spec_of(arr.shape, kind))
        operands.append(arr)
    out_specs = [spec_of(shape, kind) for shape, _, kind in outs]
    out_shape = [jax.ShapeDtypeStruct(shape, dtype) for shape, dtype, _ in outs]
    n_seg_refs = [len(arrs) for arrs, _ in a_segs]

    def body(*refs):
        pos = 0
        seg_refs = []
        for n in n_seg_refs:
            seg_refs.append(refs[pos:pos + n])
            pos += n
        b_ref = refs[pos]
        e_refs = refs[pos + 1:pos + 1 + len(epi_ins)]
        o_refs = refs[pos + 1 + len(epi_ins):pos + 1 + len(epi_ins) + len(outs)]
        i, j, k = pl.program_id(0), pl.program_id(1), pl.program_id(2)

        def product(fn, rs, s=0):
            a = rs[0][...] if fn is None else fn(*[r[...] for r in rs])
            bt = b_ref[:, koff[s] * tk:(koff[s] + 1) * tk] if b_whole else b_ref[...]
            return _dot(a, bt, NT if nt else NN)

        def finish(acc):
            res = epi(acc, [r[...] for r in e_refs], i, j)
            for o_ref, (_, _, kind), r in zip(o_refs, outs, res):
                if isinstance(kind, str) and kind == 'acc':
                    @pl.when(i == 0)
                    def _(o_ref=o_ref, r=r):
                        o_ref[...] = r

                    @pl.when(i > 0)
                    def _(o_ref=o_ref, r=r):
                        o_ref[...] += r
                else:
                    o_ref[...] = r.astype(o_ref.dtype)

        if nk == 1:
            finish(product(a_segs[0][1], seg_refs[0]))
            return
        acc_ref = refs[-1]

        @pl.when(k == 0)
        def _():
            acc_ref[...] = jnp.zeros_like(acc_ref)

        for s, ((_, fn), rs) in enumerate(zip(a_segs, seg_refs)):
            def accumulate(fn=fn, rs=rs, s=s):
                acc_ref[...] += product(fn, rs, s)
            if len(a_segs) == 1:
                accumulate()
            else:
                pl.when(jnp.logical_and(k >= koff[s], k < koff[s] + nkb[s]))(accumulate)

        @pl.when(k == nk - 1)
        def _():
            finish(acc_ref[...])

    return pl.pallas_call(
        body, name=name, grid=(ni, nj, nk), in_specs=in_specs, out_specs=out_specs, out_shape=out_shape,
        scratch_shapes=[] if nk == 1 else [pltpu.VMEM((tm, tn), F32)], compiler_params=_params(3),
    )(*operands)


def _tn_mm(name, a_arrs, a_fn, b_segs, *, tn, tk, out_dtype, shard_major=False, colsum=False):
    T, M = a_arrs[0].shape
    nbj = [arrs[0].shape[1] // tn for arrs, _ in b_segs]
    joff = [sum(nbj[:s]) for s in range(len(nbj))]
    nj, nk = sum(nbj), T // tk
    N = nj * tn
    assert T % tk == 0 and all(arrs[0].shape[1] % tn == 0 for arrs, _ in b_segs), name

    in_specs = [pl.BlockSpec((tk, M), lambda j, k: (k, 0)) for _ in a_arrs]
    operands = list(a_arrs)
    for s, (arrs, _) in enumerate(b_segs):
        for arr in arrs:
            in_specs.append(pl.BlockSpec(
                (tk, tn), lambda j, k, s=s: (k, jnp.clip(j - joff[s], 0, nbj[s] - 1))))
            operands.append(arr)
    if shard_major:
        per = (N // N_CHIPS) // tn
        assert per * tn * N_CHIPS == N, name
        out_shape = [jax.ShapeDtypeStruct((N_CHIPS, M, N // N_CHIPS), out_dtype)]
        out_specs = [pl.BlockSpec((1, M, tn), lambda j, k: (j // per, 0, j % per))]
    else:
        out_shape = [jax.ShapeDtypeStruct((M, N), out_dtype)]
        out_specs = [pl.BlockSpec((M, tn), lambda j, k: (0, j))]
    if colsum:
        out_shape.append(jax.ShapeDtypeStruct((1, N), F32))
        out_specs.append(pl.BlockSpec((1, tn), lambda j, k: (0, j)))
    n_a = len(a_arrs)
    n_seg_refs = [len(arrs) for arrs, _ in b_segs]

    def body(*refs):
        a_refs = refs[:n_a]
        pos = n_a
        seg_refs = []
        for n in n_seg_refs:
            seg_refs.append(refs[pos:pos + n])
            pos += n
        o_ref = refs[pos]
        cs_ref = refs[pos + 1] if colsum else None
        acc_ref = refs[-1]
        j, k = pl.program_id(0), pl.program_id(1)

        @pl.when(k == 0)
        def _():
            acc_ref[...] = jnp.zeros_like(acc_ref)
            if colsum:
                cs_ref[...] = jnp.zeros_like(cs_ref)

        for s, ((_, fn), rs) in enumerate(zip(b_segs, seg_refs)):
            def accumulate(fn=fn, rs=rs):
                a = a_refs[0][...] if a_fn is None else a_fn(*[r[...] for r in a_refs])
                bt = rs[0][...] if fn is None else fn(*[r[...] for r in rs])
                acc_ref[...] += _dot(a, bt, TN)
                if colsum:
                    cs_ref[...] += jnp.sum(bt.astype(F32), axis=0, keepdims=True)
            if len(b_segs) == 1:
                accumulate()
            else:
                pl.when(jnp.logical_and(j >= joff[s], j < joff[s] + nbj[s]))(accumulate)

        @pl.when(k == nk - 1)
        def _():
            if shard_major:
                o_ref[0] = acc_ref[...].astype(o_ref.dtype)
            else:
                o_ref[...] = acc_ref[...].astype(o_ref.dtype)

    res = pl.pallas_call(
        body, name=name, grid=(nj, nk), in_specs=in_specs, out_specs=out_specs, out_shape=out_shape,
        scratch_shapes=[pltpu.VMEM((M, tn), F32)], compiler_params=_params(2),
    )(*operands)
    return res if colsum else res[0]


def _silu(z):
    return z * _sig(z)


def _silu_grad(z):
    s = _sig(z)
    return s * (1.0 + z * (1.0 - s))


def _gate(o, z):
    return o * _silu(z)


def _ln_stats(r):
    mu = jnp.mean(r, axis=1, keepdims=True)
    xc = r - mu
    var = jnp.mean(xc * xc, axis=1, keepdims=True)
    rstd = lax.rsqrt(var + LN_EPS)
    return xc * rstd, rstd


def _ln_bwd(dy, xhat, rstd, g):
    dxh = dy * g
    m1 = jnp.mean(dxh, axis=1, keepdims=True)
    m2 = jnp.mean(dxh * xhat, axis=1, keepdims=True)
    return (rstd * (dxh - m1 - xhat * m2), jnp.sum(dy * xhat, axis=0, keepdims=True),
            jnp.sum(dy, axis=0, keepdims=True))


def _rms_fwd(x, g):
    rstd = lax.rsqrt(jnp.mean(x * x, axis=1, keepdims=True) + RMS_EPS)
    return x * rstd * g


def _rms_bwd(dy, x, g):
    rstd = lax.rsqrt(jnp.mean(x * x, axis=1, keepdims=True) + RMS_EPS)
    xn = x * rstd
    dxn = dy * g
    return rstd * (dxn - xn * jnp.mean(dxn * xn, axis=1, keepdims=True)), jnp.sum(dy * xn, axis=0, keepdims=True)


def _rope(x, cos, sin, transpose=False):
    parts = []
    for g in range(x.shape[1] // LANES):
        xg = x[:, g * LANES:(g + 1) * LANES]
        if transpose:
            parts.append(xg * cos + pltpu.roll(xg * sin, LANES // 2, 1))
        else:
            parts.append(xg * cos + pltpu.roll(xg, LANES // 2, 1) * sin)
    return parts[0] if len(parts) == 1 else jnp.concatenate(parts, axis=1)


def _shifted_rows(window, rc):
    n = window.shape[0]
    for b in range(SUBLANES):
        rolled = window if b == 0 else pltpu.roll(window, n - b, 0)
        for a8 in range(0, n - rc - b + 1, SUBLANES):
            yield a8 + b, rolled[a8:a8 + rc]


def _conv_fwd(proj, conv_w, conv_b, E, tm):
    T = proj.shape[0]
    kc = conv_w.shape[0]
    lc, hb, rc = min(CONV_LC, E), CONV_HALO, min(CONV_RC, tm)
    nl, ni, ratio = E // lc, T // tm, tm // hb
    gate_off = E // lc

    sub = min(CONV_SUB, lc)
    base = hb - (kc - 1)

    def body(val_ref, gate_ref, valh_ref, gateh_ref, w_ref, cb_ref, u1_ref, ubuf):
        i = pl.program_id(1)
        ubuf[hb:, :] = val_ref[...] * _sig(gate_ref[...])
        halo = valh_ref[...] * _sig(gateh_ref[...])
        ubuf[0:hb, :] = jnp.where(i > 0, halo, 0.0)
        for l0 in range(0, lc, sub):
            ls = slice(l0, l0 + sub)
            for r0 in range(0, tm, rc):
                acc = jnp.zeros((rc, sub), F32) + cb_ref[:, ls]
                for off, rows in _shifted_rows(ubuf[r0:r0 + hb + rc, ls], rc):
                    if 0 <= off - base < kc:
                        acc += w_ref[off - base:off - base + 1, ls] * rows
                u1_ref[r0:r0 + rc, ls] = acc

    return pl.pallas_call(
        body, name="conv_fwd", grid=(nl, ni),
        in_specs=[
            pl.BlockSpec((tm, lc), lambda l, i: (i, l)),
            pl.BlockSpec((tm, lc), lambda l, i: (i, gate_off + l)),
            pl.BlockSpec((hb, lc), lambda l, i: (jnp.maximum(i * ratio - 1, 0), l)),
            pl.BlockSpec((hb, lc), lambda l, i: (jnp.maximum(i * ratio - 1, 0), gate_off + l)),
            pl.BlockSpec((kc, lc), lambda l, i: (0, l)),
            pl.BlockSpec((1, lc), lambda l, i: (0, l)),
        ],
        out_specs=pl.BlockSpec((tm, lc), lambda l, i: (i, l)),
        out_shape=jax.ShapeDtypeStruct((T, E), F32),
        scratch_shapes=[pltpu.VMEM((hb + tm, lc), F32)], compiler_params=_params(2),
    )(proj, proj, proj, proj, conv_w, conv_b)


def _conv_post(u1, proj, norm_g, norm_b, E, tm):
    T = u1.shape[0]

    def body(u1_ref, z_ref, g_ref, b_ref, u4_ref):
        xhat, _ = _ln_stats(u1_ref[...])
        u4_ref[...] = (_silu(xhat * g_ref[...] + b_ref[...]) * _silu(z_ref[...])).astype(BF16)

    return pl.pallas_call(
        body, name="conv_post", grid=(T // tm,),
        in_specs=[pl.BlockSpec((tm, E), lambda i: (i, 0)), pl.BlockSpec((tm, E), lambda i: (i, 2)),
                  pl.BlockSpec((1, E), lambda i: (0, 0)), pl.BlockSpec((1, E), lambda i: (0, 0))],
        out_specs=pl.BlockSpec((tm, E), lambda i: (i, 0)),
        out_shape=jax.ShapeDtypeStruct((T, E), BF16), compiler_params=_params(1),
    )(u1, proj, norm_g, norm_b)


def _conv_bwd(du1, proj, conv_w, E, tm):
    T = du1.shape[0]
    kc = conv_w.shape[0]
    lc, hb, rc = min(CONV_LC, E), CONV_HALO, min(CONV_RC, tm)
    nl, ni, ratio = E // lc, T // tm, tm // hb
    gate_off = E // lc
    last_halo = T // hb - 1

    sub = min(CONV_SUB, lc)
    base = hb - (kc - 1)

    def body(du_ref, dun_ref, val_ref, gate_ref, valh_ref, gateh_ref, w_ref,
             dval_ref, dgate_ref, dw_ref, db_ref, ubuf, dbuf, sbuf, dw_sc):
        i = pl.program_id(1)
        sbuf[...] = _sig(gate_ref[...])
        ubuf[hb:, :] = val_ref[...] * sbuf[...]
        halo = valh_ref[...] * _sig(gateh_ref[...])
        ubuf[0:hb, :] = jnp.where(i > 0, halo, 0.0)
        dbuf[0:tm, :] = du_ref[...]
        dbuf[tm:, :] = jnp.where(i < ni - 1, dun_ref[...], 0.0)

        @pl.when(i == 0)
        def _():
            dw_sc[...] = jnp.zeros_like(dw_sc)
            db_ref[...] = jnp.zeros_like(db_ref)

        db_ref[...] += jnp.sum(du_ref[...], axis=0, keepdims=True)
        for l0 in range(0, lc, sub):
            ls = slice(l0, l0 + sub)
            for r0 in range(0, tm, rc):
                dwin = dbuf[r0:r0 + rc + hb, ls]
                dchunk = dwin[0:rc]
                for off, rows in _shifted_rows(ubuf[r0:r0 + hb + rc, ls], rc):
                    k = off - base
                    if 0 <= k < kc:
                        prod = rows * dchunk
                        part = prod[0:SUBLANES]
                        for s8 in range(SUBLANES, rc, SUBLANES):
                            part = part + prod[s8:s8 + SUBLANES]
                        dw_sc[k, :, ls] += part
                acc = jnp.zeros((rc, sub), F32)
                for off, rows in _shifted_rows(dwin, rc):
                    k = (kc - 1) - off
                    if 0 <= k < kc:
                        acc += w_ref[k:k + 1, ls] * rows
                v, s = val_ref[r0:r0 + rc, ls], sbuf[r0:r0 + rc, ls]
                dval_ref[r0:r0 + rc, ls] = (acc * s).astype(BF16)
                dgate_ref[r0:r0 + rc, ls] = (acc * v * s * (1.0 - s)).astype(BF16)

        @pl.when(i == ni - 1)
        def _():
            for k in range(kc):
                dw_ref[k:k + 1, :] = jnp.sum(dw_sc[k], axis=0, keepdims=True)

    return pl.pallas_call(
        body, name="conv_bwd", grid=(nl, ni),
        in_specs=[
            pl.BlockSpec((tm, lc), lambda l, i: (i, l)),
            pl.BlockSpec((hb, lc), lambda l, i: (jnp.minimum((i + 1) * ratio, last_halo), l)),
            pl.BlockSpec((tm, lc), lambda l, i: (i, l)),
            pl.BlockSpec((tm, lc), lambda l, i: (i, gate_off + l)),
            pl.BlockSpec((hb, lc), lambda l, i: (jnp.maximum(i * ratio - 1, 0), l)),
            pl.BlockSpec((hb, lc), lambda l, i: (jnp.maximum(i * ratio - 1, 0), gate_off + l)),
            pl.BlockSpec((kc, lc), lambda l, i: (0, l)),
        ],
        out_specs=[pl.BlockSpec((tm, lc), lambda l, i: (i, l)), pl.BlockSpec((tm, lc), lambda l, i: (i, l)),
                   pl.BlockSpec((kc, lc), lambda l, i: (0, l)), pl.BlockSpec((1, lc), lambda l, i: (0, l))],
        out_shape=[jax.ShapeDtypeStruct((T, E), BF16), jax.ShapeDtypeStruct((T, E), BF16),
                   jax.ShapeDtypeStruct((kc, E), F32), jax.ShapeDtypeStruct((1, E), F32)],
        scratch_shapes=[pltpu.VMEM((hb + tm, lc), F32), pltpu.VMEM((tm + hb, lc), F32),
                        pltpu.VMEM((tm, lc), F32), pltpu.VMEM((kc, SUBLANES, lc), F32)],
        compiler_params=_params(2),
    )(du1, du1, proj, proj, proj, proj, conv_w)


def _norm_prep(pb, kv_g, q_g, cos, sin, rkv, rq, wk, tm):
    T = pb.shape[0]

    def body(ckv_ref, cq_ref, kg_ref, qg_ref, cos_ref, sin_ref, c_ref, kr_ref, cqn_ref):
        blk = ckv_ref[...]
        c_ref[...] = _rms_fwd(blk[:, :rkv], kg_ref[...]).astype(BF16)
        kr_ref[...] = _rope(blk[:, rkv:rkv + LANES], cos_ref[...], sin_ref[...]).astype(BF16)
        cqn_ref[...] = _rms_fwd(cq_ref[...], qg_ref[...]).astype(BF16)

    return pl.pallas_call(
        body, name="norm_prep", grid=(T // tm,),
        in_specs=[pl.BlockSpec((tm, wk), lambda i: (i, 0)), pl.BlockSpec((tm, rq), lambda i: (i, wk // rq)),
                  pl.BlockSpec((1, rkv), lambda i: (0, 0)), pl.BlockSpec((1, rq), lambda i: (0, 0)),
                  pl.BlockSpec((tm, LANES), lambda i: (i, 0)), pl.BlockSpec((tm, LANES), lambda i: (i, 0))],
        out_specs=[pl.BlockSpec((tm, rkv), lambda i: (i, 0)), pl.BlockSpec((tm, LANES), lambda i: (i, 0)),
                   pl.BlockSpec((tm, rq), lambda i: (i, 0))],
        out_shape=[jax.ShapeDtypeStruct((T, rkv), BF16), jax.ShapeDtypeStruct((T, LANES), BF16),
                   jax.ShapeDtypeStruct((T, rq), BF16)],
        compiler_params=_params(1),
    )(pb, pb, kv_g, q_g, cos, sin)


def _attn_fwd(q_all, kv, kr, H, tq, scale):
    T = q_all.shape[0]
    nq = T // tq
    pair = 2
    W = pair * LANES
    assert H % pair == 0
    hp_n = H // pair

    def body(qn_ref, qr_ref, kn_ref, kr_ref, v_ref, o_ref, lse_ref, *scratch):
        qi = pl.program_id(1)
        chains = [scratch[4 * a:4 * a + 4] for a in range(pair)]
        lanes = [slice(a * LANES, (a + 1) * LANES) for a in range(pair)]
        groups = [slice(c * LANES, (c + 1) * LANES) for c in range(tq // LANES)]

        def fold(x, op):
            r = x[:, groups[0]]
            for gsl in groups[1:]:
                r = op(r, x[:, gsl])
            return r

        for _, m_sc, l_sc, acc_sc in chains:
            m_sc[...] = jnp.full_like(m_sc, MASK_VALUE)
            l_sc[...] = jnp.zeros_like(l_sc)
            acc_sc[...] = jnp.zeros_like(acc_sc)

        def scores(j, masked):
            rows = pl.ds(pl.multiple_of(j * tq, tq), tq)
            krope = kr_ref[rows, :]
            for a, (s_sc, m_sc, _, _) in enumerate(chains):
                q = jnp.concatenate([qn_ref[:, lanes[a]], qr_ref[:, lanes[a]]], axis=1)
                k = jnp.concatenate([kn_ref[rows, lanes[a]], krope], axis=1)
                s = _dot(q, k, NT) * (scale * LOG2_E)
                if masked:
                    row = lax.broadcasted_iota(jnp.int32, s.shape, 0)
                    col = lax.broadcasted_iota(jnp.int32, s.shape, 1)
                    s = jnp.where(col <= row, s, MASK_VALUE)
                s_sc[j] = s
                m_sc[...] = jnp.maximum(m_sc[...], fold(s, jnp.maximum))

        def two_per_trip(fn, count):
            def two(p, carry):
                fn(2 * p)
                fn(2 * p + 1)
                return carry

            lax.fori_loop(0, count // 2, two, 0)

            @pl.when(count % 2 == 1)
            def _():
                fn(count - 1)

        two_per_trip(functools.partial(scores, masked=False), qi)
        scores(qi, True)
        for _, m_sc, _, _ in chains:
            m_sc[...] = jnp.broadcast_to(jnp.max(m_sc[...], axis=1, keepdims=True), m_sc.shape)

        def weigh(j):
            rows = pl.ds(pl.multiple_of(j * tq, tq), tq)
            for a, (s_sc, m_sc, l_sc, acc_sc) in enumerate(chains):
                s, m = s_sc[j], m_sc[...]
                p = jnp.concatenate([jnp.exp2(s[:, gsl] - m) for gsl in groups], axis=1)
                l_sc[...] += fold(p, jnp.add)
                acc_sc[...] += _dot(p, v_ref[rows, lanes[a]], NN)

        two_per_trip(weigh, qi + 1)
        for a, (_, m_sc, l_sc, acc_sc) in enumerate(chains):
            l = jnp.sum(l_sc[...], axis=1, keepdims=True)
            o_ref[:, lanes[a]] = acc_sc[...] / l
            lse_ref[a] = m_sc[:, 0:1] * (1.0 / LOG2_E) + jnp.log(l)

    chain_scratch = [pltpu.VMEM((nq, tq, tq), F32), pltpu.VMEM((tq, LANES), F32), pltpu.VMEM((tq, LANES), F32),
                     pltpu.VMEM((tq, LANES), F32)]
    return pl.pallas_call(
        body, name="attn_fwd", grid=(hp_n, nq),
        in_specs=[pl.BlockSpec((tq, W), lambda hp, qi: (qi, hp)),
                  pl.BlockSpec((tq, W), lambda hp, qi: (qi, hp_n + hp)),
                  pl.BlockSpec((T, W), lambda hp, qi: (0, hp)),
                  pl.BlockSpec((T, LANES), lambda hp, qi: (0, 0)),
                  pl.BlockSpec((T, W), lambda hp, qi: (0, hp_n + hp))],
        out_specs=[pl.BlockSpec((tq, W), lambda hp, qi: (qi, hp)),
                   pl.BlockSpec((pair, tq, 1), lambda hp, qi: (hp, qi, 0))],
        out_shape=[jax.ShapeDtypeStruct((T, H * LANES), F32), jax.ShapeDtypeStruct((H, T, 1), F32)],
        scratch_shapes=chain_scratch * pair, compiler_params=_params(2),
    )(q_all, q_all, kv, kr, kv)


def _attn_bwd(q_all, kv, kr, do, o, lse, H, tq, scale):
    T = q_all.shape[0]
    nq = T // tq
    HV = H * LANES
    pair = 2
    tk2 = pair * tq
    ng = T // tk2
    assert ng * tk2 == T

    def body(qn_ref, qr_ref, kn_ref, kr_ref, v_ref, do_ref, o_ref, lse_ref,
             dqn_ref, dqr_ref, dkn_ref, dkr_ref, dv_ref, dq_sc, dk_sc, dv_sc):
        g = pl.program_id(1)

        @pl.when(g == 0)
        def _():
            dq_sc[...] = jnp.zeros_like(dq_sc)

        dk_sc[...] = jnp.zeros_like(dk_sc)
        dv_sc[...] = jnp.zeros_like(dv_sc)

        def block(qi, modes):
            rows = pl.ds(pl.multiple_of(qi * tq, tq), tq)
            q = jnp.concatenate([qn_ref[rows, :], qr_ref[rows, :]], axis=1)
            dov = do_ref[rows, :]
            delta = jnp.sum(dov.astype(F32) * o_ref[rows, :], axis=1, keepdims=True)
            lse_q = lse_ref[0, rows, :]
            dq = None
            for c, masked in enumerate(modes):
                if masked is None:
                    continue
                kr_ = slice(c * tq, (c + 1) * tq)
                k = jnp.concatenate([kn_ref[kr_, :], kr_ref[kr_, :]], axis=1)
                s = _dot(q, k, NT) * scale
                if masked:
                    row = lax.broadcasted_iota(jnp.int32, s.shape, 0)
                    col = lax.broadcasted_iota(jnp.int32, s.shape, 1)
                    s = jnp.where(col <= row, s, MASK_VALUE)
                p = jnp.exp(s - lse_q)
                dv_sc[kr_, :] += _dot(p, dov, TN)
                dp = _dot(dov, v_ref[kr_, :], NT)
                ds = (p * (dp - delta) * scale).astype(BF16)
                dk_sc[kr_, :] += _dot(ds, q, TN)
                part = _dot(ds, k, NN)
                dq = part if dq is None else dq + part
            dq_sc[rows, :] += dq

        block(pair * g, (True, None))
        block(pair * g + 1, (False, True))

        def below(qi, carry):
            block(qi, (False, False))
            return carry

        lax.fori_loop(pair * g + pair, nq, below, 0)
        dkn_ref[...] = dk_sc[:, :LANES].astype(BF16)
        dkr_ref[...] = dk_sc[:, LANES:]
        dv_ref[...] = dv_sc[...].astype(BF16)

        @pl.when(g == ng - 1)
        def _():
            dqn_ref[...] = dq_sc[:, :LANES].astype(BF16)
            dqr_ref[...] = dq_sc[:, LANES:]

    whole = lambda col: pl.BlockSpec((T, LANES), col)
    tile = lambda col: pl.BlockSpec((tk2, LANES), col)
    return pl.pallas_call(
        body, name="attn_bwd", grid=(H, ng),
        in_specs=[whole(lambda h, g: (0, h)), whole(lambda h, g: (0, H + h)),
                  tile(lambda h, g: (g, h)), tile(lambda h, g: (g, 0)), tile(lambda h, g: (g, H + h)),
                  whole(lambda h, g: (0, h)), whole(lambda h, g: (0, h)),
                  pl.BlockSpec((1, T, 1), lambda h, g: (h, 0, 0))],
        out_specs=[whole(lambda h, g: (0, h)), whole(lambda h, g: (0, h)),
                   tile(lambda h, g: (g, h)), tile(lambda h, g: (g, h)), tile(lambda h, g: (g, h))],
        out_shape=[jax.ShapeDtypeStruct((T, HV), BF16), jax.ShapeDtypeStruct((T, HV), F32),
                   jax.ShapeDtypeStruct((T, HV), BF16), jax.ShapeDtypeStruct((T, HV), F32),
                   jax.ShapeDtypeStruct((T, HV), BF16)],
        scratch_shapes=[pltpu.VMEM((T, 2 * LANES), F32), pltpu.VMEM((tk2, 2 * LANES), F32),
                        pltpu.VMEM((tk2, LANES), F32)],
        compiler_params=_params(2),
    )(q_all, q_all, kv, kr, kv, do, o, lse)


def _rope_bwd(dqr, dkr_heads, cos, sin, H, tm):
    T, HV = dqr.shape

    def body(dqr_ref, dkr_ref, cos_ref, sin_ref, dq_ref, dk_ref):
        c, s = cos_ref[...], sin_ref[...]
        dq_ref[...] = _rope(dqr_ref[...], c, s, transpose=True).astype(BF16)
        dk = dkr_ref[...]
        tot = dk[:, 0:LANES]
        for h in range(1, H):
            tot = tot + dk[:, h * LANES:(h + 1) * LANES]
        dk_ref[...] = _rope(tot, c, s, transpose=True)

    return pl.pallas_call(
        body, name="rope_bwd", grid=(T // tm,),
        in_specs=[pl.BlockSpec((tm, HV), lambda i: (i, 0)), pl.BlockSpec((tm, HV), lambda i: (i, 0)),
                  pl.BlockSpec((tm, LANES), lambda i: (i, 0)), pl.BlockSpec((tm, LANES), lambda i: (i, 0))],
        out_specs=[pl.BlockSpec((tm, HV), lambda i: (i, 0)), pl.BlockSpec((tm, LANES), lambda i: (i, 0))],
        out_shape=[jax.ShapeDtypeStruct((T, HV), BF16), jax.ShapeDtypeStruct((T, LANES), F32)],
        compiler_params=_params(1),
    )(dqr, dkr_heads, cos, sin)


def _place():
    x, y, c = lax.axis_index("x"), lax.axis_index("y"), lax.axis_index("c")
    chips = [(1 - x, y), (x, 1 - y), (1 - x, 1 - y)]
    return x, y, c, chips


def _all_gather_chips(pack):
    rows = pack.shape[0]
    half = rows // 2
    assert half * 2 == rows and half % BF16_ROWS == 0

    def body(w_ref, out_ref, send_sems, recv_sems, local_sem):
        x, y, c, chips = _place()
        sibling = (x, y, 1 - c)

        def region(px, py, pc):
            return out_ref.at[2 * px + py, pl.ds(pc * half, half), :]

        def copy(k, block, to, src=None):
            return pltpu.make_async_remote_copy(
                src_ref=region(*block) if src is None else src, dst_ref=region(*block),
                send_sem=send_sems.at[k], recv_sem=recv_sems.at[k], device_id=to, device_id_type=MESH)

        mine = pltpu.make_async_copy(w_ref, out_ref.at[2 * x + y], local_sem)
        mine.start()
        my_half = w_ref.at[pl.ds(c * half, half), :]
        first = [copy(j, (x, y, c), (*chip, c), src=my_half) for j, chip in enumerate(chips)]
        for cp in first:
            cp.start()
        passed = [copy(3 + j, (*chip, c), sibling) for j, chip in enumerate(chips)]
        for j, chip in enumerate(chips):
            copy(j, (*chip, c), (x, y, c)).wait_recv()
            passed[j].start()
        for j, chip in enumerate(chips):
            copy(3 + j, (*chip, 1 - c), (x, y, c)).wait_recv()
        for cp in first + passed:
            cp.wait_send()
        mine.wait()

    return pl.pallas_call(
        body, name="gather_weights", in_specs=[ANY], out_specs=ANY,
        out_shape=jax.ShapeDtypeStruct((N_CHIPS,) + pack.shape, pack.dtype),
        scratch_shapes=[pltpu.SemaphoreType.DMA((6,)), pltpu.SemaphoreType.DMA((6,)), pltpu.SemaphoreType.DMA],
    )(pack)


def _swap_cores(name, parts):
    n = len(parts)

    def body(*refs):
        p_refs, r_refs = refs[:n], refs[n:2 * n]
        send_sems, recv_sems = refs[2 * n:]
        x, y, c, _ = _place()
        copies = [pltpu.make_async_remote_copy(
            src_ref=p_refs[w], dst_ref=r_refs[w], send_sem=send_sems.at[w], recv_sem=recv_sems.at[w],
            device_id=(x, y, 1 - c), device_id_type=MESH) for w in range(n)]
        for cp in copies:
            cp.start()
        for cp in copies:
            cp.wait()

    return pl.pallas_call(
        body, name=name, in_specs=[ANY] * n, out_specs=[ANY] * n,
        out_shape=[jax.ShapeDtypeStruct(p.shape, p.dtype) for p in parts],
        scratch_shapes=[pltpu.SemaphoreType.DMA((n,)), pltpu.SemaphoreType.DMA((n,))],
    )(*parts)


HBM = pl.BlockSpec(memory_space=pltpu.HBM)
SEM = pl.BlockSpec(memory_space=pltpu.SEMAPHORE)
EFFECT = pltpu.SideEffectType.DATAFLOW_SIDE_EFFECTING


def _push_copies(a_refs, l_refs, send_sems, recv_sems, by_target):
    x, y, c, chips = _place()
    me = 2 * x + y
    out = []
    for w, (a_ref, l_ref) in enumerate(zip(a_refs, l_refs)):
        for j, (px, py) in enumerate(chips):
            peer = 2 * px + py
            out.append((
                pltpu.make_async_remote_copy(
                    src_ref=a_ref.at[peer] if by_target else a_ref, dst_ref=l_ref.at[me],
                    send_sem=send_sems.at[3 * w + j], recv_sem=recv_sems.at[3 * w + j],
                    device_id=(px, py, c), device_id_type=MESH),
                pltpu.make_async_remote_copy(
                    src_ref=a_ref.at[me] if by_target else a_ref, dst_ref=l_ref.at[peer],
                    send_sem=send_sems.at[3 * w + j], recv_sem=recv_sems.at[3 * w + j],
                    device_id=(px, py, c), device_id_type=MESH)))
    return out


def _push_start(name, arrs, by_target):
    n = len(arrs)
    lands = [lax.empty((N_CHIPS,) + (a.shape[1:] if by_target else a.shape), a.dtype) for a in arrs]

    def body(*refs):
        a_refs, l_refs = refs[:n], refs[n:2 * n]
        send_sems, recv_sems = refs[2 * n], refs[2 * n + 1]
        token = refs[-1]
        for send, _ in _push_copies(a_refs, l_refs, send_sems, recv_sems, by_target):
            send.start()
        token[...] = jnp.zeros_like(token)

    res = pl.pallas_call(
        body, name=name,
        out_shape=(pltpu.SemaphoreType.DMA((3 * n,)), pltpu.SemaphoreType.DMA((3 * n,)),
                   *[pltpu.HBM(a.shape, a.dtype) for a in arrs], *[pltpu.HBM(l.shape, l.dtype) for l in lands],
                   jax.ShapeDtypeStruct((8, LANES), F32)),
        in_specs=[HBM] * (2 * n), out_specs=(SEM, SEM, *[HBM] * (2 * n), pl.BlockSpec(memory_space=pltpu.VMEM)),
        input_output_aliases={i: 2 + i for i in range(2 * n)},
        compiler_params=pltpu.CompilerParams(has_side_effects=EFFECT),
    )(*[pltpu.with_memory_space_constraint(a, pltpu.HBM) for a in list(arrs) + lands])
    return res[0], res[1], list(res[2:2 + n]), list(res[2 + n:2 + 2 * n]), res[-1]


def _push_wait(name, send_sems, recv_sems, arrs, lands, after, by_target):
    n = len(arrs)

    def body(*refs):
        a_refs, l_refs = refs[:n], refs[n:2 * n]
        s_sems, r_sems = refs[2 * n], refs[2 * n + 1]
        for send, recv in _push_copies(a_refs, l_refs, s_sems, r_sems, by_target):
            send.wait_send()
            recv.wait_recv()

    res = pl.pallas_call(
        body, name=name,
        out_shape=[pltpu.HBM(a.shape, a.dtype) for a in list(arrs) + list(lands)],
        in_specs=[HBM] * (2 * n) + [SEM, SEM] + [ANY] * len(after), out_specs=[HBM] * (2 * n),
        input_output_aliases={i: i for i in range(2 * n)},
        compiler_params=pltpu.CompilerParams(has_side_effects=EFFECT),
    )(*arrs, *lands, send_sems, recv_sems, *after)
    return list(res[:n]), list(res[n:])


def _all_reduce_small(part):
    n_dev = 8

    def body(p_ref, out_ref, buf, send_sems, recv_sems):
        x, y, c, _ = _place()
        me = 4 * x + 2 * y + c
        buf[me] = p_ref[...]
        copies = []
        for k in range(1, n_dev):
            kx, ky, kc = (k >> 2) & 1, (k >> 1) & 1, k & 1
            peer = (x ^ kx, y ^ ky, c ^ kc)
            cp = pltpu.make_async_remote_copy(
                src_ref=p_ref, dst_ref=buf.at[me], send_sem=send_sems.at[k - 1], recv_sem=recv_sems.at[k - 1],
                device_id=peer, device_id_type=MESH)
            cp.start()
            copies.append(cp)
        for k in range(1, n_dev):
            pltpu.make_async_remote_copy(
                src_ref=p_ref, dst_ref=buf.at[me ^ k], send_sem=send_sems.at[k - 1], recv_sem=recv_sems.at[k - 1],
                device_id=(x, y, c), device_id_type=MESH).wait_recv()
        for cp in copies:
            cp.wait_send()
        tot = buf[0]
        for d in range(1, n_dev):
            tot = tot + buf[d]
        out_ref[...] = tot

    vm = pl.BlockSpec(memory_space=pltpu.VMEM)
    return pl.pallas_call(
        body, name="all_reduce_small", in_specs=[vm], out_specs=vm,
        out_shape=jax.ShapeDtypeStruct(part.shape, F32),
        scratch_shapes=[pltpu.VMEM((n_dev,) + part.shape, F32), pltpu.SemaphoreType.DMA((n_dev - 1,)),
                        pltpu.SemaphoreType.DMA((n_dev - 1,))],
    )(part)


def _row_tiles(shape):
    ax = next(d for d, s in enumerate(shape) if s > 1)
    tr = _gcd(ADAM_ROWS, shape[ax])
    block = tuple(tr if d == ax else s for d, s in enumerate(shape))
    return shape[ax] // tr, block, lambda i: tuple(i if d == ax else 0 for d in range(len(shape)))


def _sum_chips(name, r):
    shape = r.shape[1:]
    steps, block, index = _row_tiles(shape)

    def body(r_ref, o_ref):
        tot = r_ref[0].astype(F32)
        for q in range(1, N_CHIPS):
            tot = tot + r_ref[q].astype(F32)
        o_ref[...] = tot

    return pl.pallas_call(
        body, name=name, grid=(steps,),
        in_specs=[pl.BlockSpec((N_CHIPS,) + block, lambda i: (0,) + index(i))],
        out_specs=pl.BlockSpec(block, index),
        out_shape=jax.ShapeDtypeStruct(shape, F32), compiler_params=_params(1),
    )(r)


def _adamw_math(g, w, m, v):
    mn = ADAM_B1 * m + (1.0 - ADAM_B1) * g
    vn = ADAM_B2 * v + (1.0 - ADAM_B2) * jnp.square(g)
    m_hat = mn / (1.0 - ADAM_B1 ** ADAM_STEP)
    v_hat = vn / (1.0 - ADAM_B2 ** ADAM_STEP)
    return -ADAM_LR * (m_hat / (jnp.sqrt(v_hat) + ADAM_EPS) + ADAM_WD * w), mn, vn


def _adamw(name, g_parts, w, m, v):
    steps, block, index = _row_tiles(w.shape)
    n = len(g_parts)

    def body(*refs):
        g = refs[0][...]
        for r in refs[1:n]:
            g = g + r[...]
        w_ref, m_ref, v_ref, go_ref, d_ref, mo_ref, vo_ref = refs[n:]
        go_ref[...] = g
        d_ref[...], mo_ref[...], vo_ref[...] = _adamw_math(g, w_ref[...], m_ref[...], v_ref[...])

    spec = pl.BlockSpec(block, index)
    return pl.pallas_call(
        body, name=name, grid=(steps,), in_specs=[spec] * (n + 3), out_specs=[spec] * 4,
        out_shape=[jax.ShapeDtypeStruct(w.shape, F32)] * 4, compiler_params=_params(1),
    )(*g_parts, w, m, v)


def _adamw_vectors(items):
    n = len(items)

    def body(*refs):
        ins, outs = refs[:4 * n], refs[4 * n:]
        for k in range(n):
            g, w, m, v = (r[...] for r in ins[4 * k:4 * k + 4])
            outs[3 * k][...], outs[3 * k + 1][...], outs[3 * k + 2][...] = _adamw_math(g, w, m, v)

    vm = pl.BlockSpec(memory_space=pltpu.VMEM)
    res = pl.pallas_call(
        body, name="adamw_vectors", in_specs=[vm] * (4 * n), out_specs=[vm] * (3 * n),
        out_shape=[jax.ShapeDtypeStruct(it[1].shape, F32) for it in items for _ in range(3)],
    )(*[a for it in items for a in it])
    return [res[3 * k:3 * k + 3] for k in range(n)]


def _pack_rows(flat, dtype, multiple):
    n = flat.shape[0]
    total = -(-n // multiple) * multiple
    return jnp.pad(flat, (0, total - n)).astype(dtype).reshape(total // LANES, LANES)


def kernel(x, positions, ln_g, ln_b, a_w_in, a_b_in, a_conv_w, a_conv_b, a_norm_g, a_norm_b, a_w_out, a_b_out, kv_w_down, kv_norm_g, kv_w_uk, kv_w_uv, b_w_in, b_q_norm_g, b_w_uq, b_w_out, loss_target, m_ln_g, m_ln_b, m_a_w_in, m_a_b_in, m_a_conv_w, m_a_conv_b, m_a_norm_g, m_a_norm_b, m_a_w_out, m_a_b_out, m_kv_w_down, m_kv_norm_g, m_kv_w_uk, m_kv_w_uv, m_b_w_in, m_b_q_norm_g, m_b_w_uq, m_b_w_out, v_ln_g, v_ln_b, v_a_w_in, v_a_b_in, v_a_conv_w, v_a_conv_b, v_a_norm_g, v_a_norm_b, v_a_w_out, v_a_b_out, v_kv_w_down, v_kv_norm_g, v_kv_w_uk, v_kv_w_uv, v_b_w_in, v_b_q_norm_g, v_b_w_uq, v_b_w_out):
    T, D = x.shape[1], x.shape[2]
    E = N_CHIPS * a_w_out.shape[1]
    KC = a_conv_w.shape[1]
    RKV = kv_norm_g.shape[0]
    H, DN = kv_w_uk.shape[1], kv_w_uk.shape[2]
    RQ = b_q_norm_g.shape[1]
    HV = N_CHIPS * b_w_out.shape[1]
    assert DN == LANES and kv_w_uv.shape[2] == LANES and HV == H * LANES
    assert kv_w_down.shape[1] == RKV + ROPE_DIM and b_w_uq.shape[3] == DN + ROPE_DIM
    assert ln_g.shape[0] == 2 and a_w_in.shape[0] == 1 and b_w_in.shape[0] == 1
    alpha = (2.0 * ln_g.shape[0]) ** 0.25
    scale = 1.0 / math.sqrt(DN + ROPE_DIM)
    WK = -(-(RKV + LANES) // 256) * 256
    assert WK % RQ == 0
    Z_OFF = WK + RQ
    tmw, tq = min(TM_WIDE, T), min(TQ, T)
    t512, t1024 = _fit(512, T), _fit(1024, T)
    xs = x[0]
    tgt = loss_target[0]
    px, py = lax.axis_index("x"), lax.axis_index("y")
    chip = 2 * px + py

    mats = [a_w_out[0], kv_w_down, kv_w_uk, kv_w_uv, b_w_in[0], b_w_uq[0], b_w_out[0]]
    vecs = [a_b_in[0], a_conv_w[0], a_conv_b[0], a_norm_g[0], a_norm_b[0], a_b_out[0]]
    first = jnp.concatenate(
        [a_w_in[0].astype(BF16).reshape(-1)]
        + [lax.bitcast_convert_type(w.reshape(-1), BF16).reshape(-1) for w in vecs])
    rest = [w.astype(BF16) for w in mats]
    gathered = _all_gather_chips(_pack_rows(first, BF16, 2 * BF16_ROWS * LANES)).reshape(N_CHIPS, -1)
    gathered, rest = lax.optimization_barrier((gathered, rest))
    rest_sems = _push_start("gather_rest_start", rest, by_target=False)
    g_win = gathered[:, :a_w_in[0].size].reshape((N_CHIPS,) + a_w_in[0].shape)
    off = a_w_in[0].size
    fvec = []
    for w in vecs:
        bits = gathered[:, off:off + 2 * w.size].reshape((N_CHIPS,) + w.shape + (2,))
        fvec.append(lax.bitcast_convert_type(bits, F32))
        off += 2 * w.size
    cols = lambda g: jnp.moveaxis(g, 0, -2).reshape(g.shape[1:-1] + (N_CHIPS * g.shape[-1],))
    w_in = cols(g_win)
    b_in = cols(fvec[0][:, None, :])
    conv_w = cols(fvec[1])
    conv_b, norm_g, norm_b, b_out = (cols(f[:, None, :]) for f in fvec[2:])
    row = lambda a: a.reshape(1, -1)
    g0, b0, g1, b1 = row(ln_g[0]), row(ln_b[0]), row(ln_g[1]), row(ln_b[1])
    kv_g, q_g = row(kv_norm_g), row(b_q_norm_g[0])
    plain = lambda acc, ins, i, j: [acc]

    b_in = b_in + rest_sems[4][0, 0]
    (proj,) = _row_mm("a_in", [((xs,), None)], w_in, nt=False, tm=t1024, tn=_fit(1536, 3 * E), tk=_fit(1024, D),
                      outs=[((T, 3 * E), F32, 'tile')], epi=lambda acc, ins, i, j: [acc + ins[0]],
                      epi_ins=[(b_in, 'col')])
    u1 = _conv_fwd(proj, conv_w, conv_b, E, tmw)
    u4 = _conv_post(u1, proj, norm_g, norm_b, E, tmw)

    rest, landed = _push_wait("gather_rest_wait", *rest_sems[:4], after=[u4], by_target=False)
    g_wout, g_wd, g_uk, g_uv, g_wbin, g_wuq, g_wbout = [
        lax.dynamic_update_slice(l, w[None], (chip,) + (0,) * w.ndim) for w, l in zip(rest, landed)]
    w_out = g_wout.reshape(E, D)
    wd = g_wd.reshape(D, RKV + ROPE_DIM)
    zpad = jnp.zeros((D, ROPE_HALF), BF16)
    wd_p = jnp.concatenate(
        [wd[:, :RKV], wd[:, RKV:RKV + ROPE_HALF], zpad, wd[:, RKV + ROPE_HALF:], zpad,
         jnp.zeros((D, WK - RKV - LANES), BF16)], axis=1)
    w_bin = cols(g_wbin)
    w_z = w_bin[:, RQ:]
    wb_small = jnp.concatenate([wd_p, w_bin[:, :RQ]], axis=1)
    wb_all = jnp.concatenate([wd_p, w_bin], axis=1)
    w_kv = jnp.concatenate([g_uk.reshape(RKV, HV), g_uv.reshape(RKV, HV)], axis=1)
    wuq = g_wuq.reshape(RQ, H, DN + ROPE_DIM)
    zq = jnp.zeros((RQ, H, ROPE_HALF), BF16)
    w_qr = jnp.concatenate([wuq[:, :, DN:DN + ROPE_HALF], zq, wuq[:, :, DN + ROPE_HALF:], zq], axis=2)
    w_q = jnp.concatenate([wuq[:, :, :DN].reshape(RQ, HV), w_qr.reshape(RQ, HV)], axis=1)
    w_bout = g_wbout.reshape(HV, D)

    freqs = ROPE_THETA ** (-jnp.arange(0, ROPE_DIM, 2, dtype=F32) / ROPE_DIM)
    ang = positions[0].astype(F32)[:, None] * freqs
    cs, sn = jnp.cos(ang), jnp.sin(ang)
    ones, zeros = jnp.ones_like(cs), jnp.zeros_like(cs)
    cos_t = jnp.concatenate([cs, ones, cs, ones], axis=1)
    sin_t = jnp.concatenate([-sn, zeros, sn, zeros], axis=1)

    def ln_epi(acc, ins, i, j):
        bias, res, g, b = ins
        xhat, rstd = _ln_stats(alpha * res + acc + bias)
        h = xhat * g + b
        return [h, h, xhat, rstd]

    h1, h1b, xhat1, rstd1 = _row_mm(
        "a_out", [((u4,), None)], w_out, nt=False, tm=t512, tn=D, tk=_fit(2048, E),
        outs=[((T, D), F32, 'tile'), ((T, D), BF16, 'tile'), ((T, D), F32, 'tile'), ((T, 1), F32, 'row')],
        epi=ln_epi, epi_ins=[(b_out, 'col'), (xs, 'tile'), (g0, 'col'), (b0, 'col')])

    tkb = _fit(512, _gcd(WK, RQ, HV))
    (pb,) = _row_mm("b_in", [((h1b,), None)], wb_small, nt=False, tm=t1024, tn=_fit(1024, Z_OFF),
                    tk=_fit(1024, D), outs=[((T, Z_OFF), F32, 'tile')], epi=plain)
    (zb,) = _row_mm("b_in_gate", [((h1b,), None)], w_z, nt=False, tm=t1024, tn=_fit(2048, HV),
                    tk=_fit(1024, D), outs=[((T, HV), F32, 'tile')], epi=plain)
    c_lat, kr, cqn = _norm_prep(pb, kv_g, q_g, cos_t, sin_t, RKV, RQ, WK, tmw)
    (kv,) = _row_mm("kv_up", [((c_lat,), None)], w_kv, nt=False, tm=t1024, tn=_fit(2048, HV),
                    tk=_fit(1024, RKV), outs=[((T, 2 * HV), BF16, 'tile')], epi=plain)
    tnq = _fit(2048, HV)
    half_q = HV // tnq

    def q_epi(acc, ins, i, j):
        return [jnp.where(j >= half_q, _rope(acc, ins[0], ins[1]), acc)]

    (q_all,) = _row_mm("q_up", [((cqn,), None)], w_q, nt=False, tm=t1024, tn=tnq, tk=_fit(1024, RQ),
                       outs=[((T, 2 * HV), BF16, 'tile')], epi=q_epi,
                       epi_ins=[(cos_t, 'row'), (sin_t, 'row')])
    o, lse = _attn_fwd(q_all, kv, kr, H, tq, scale)

    def loss_epi(acc, ins, i, j):
        res, g, b, target = ins
        xhat, rstd = _ln_stats(alpha * res + acc)
        diff = xhat * g + b - target
        dr, dg, db = _ln_bwd(diff / D, xhat, rstd, g)
        return [dr, 0.5 * jnp.sum(diff * diff, keepdims=True) / D, dg, db]

    dr1, loss_part, dg1, db1 = _row_mm(
        "b_out", [((o, zb), _gate)], w_bout, nt=False, tm=tmw, tn=D, tk=_fit(2048, HV),
        outs=[((T, D), F32, 'tile'), ((1, 1), F32, 'acc'), ((1, D), F32, 'acc'), ((1, D), F32, 'acc')],
        epi=loss_epi, epi_ins=[(h1, 'tile'), (g1, 'col'), (b1, 'col'), (tgt, 'tile')])

    def gate_bwd_epi(acc, ins, i, j):
        return [acc * _silu(ins[1]), acc * ins[0] * _silu_grad(ins[1])]

    do, dz = _row_mm(
        "b_out_bwd", [((dr1,), None)], w_bout, nt=True, tm=tmw, tn=_fit(2048, HV), tk=_fit(1024, D),
        outs=[((T, HV), BF16, 'tile'), ((T, HV), BF16, 'tile')], epi=gate_bwd_epi,
        epi_ins=[(o, 'tile'), (zb, 'tile')])
    gw_bout = _tn_mm("dw_b_out", (o, zb), _gate, [((dr1,), None)], tn=_fit(1024, D), tk=t512, out_dtype=BF16)
    dqn, dqr, dkn, dkr_h, dv = _attn_bwd(q_all, kv, kr, do, o, lse, H, tq, scale)
    dqr_pre, dkr_pre = _rope_bwd(dqr, dkr_h, cos_t, sin_t, H, tmw)

    def cq_bwd_epi(acc, ins, i, j):
        dx, dg = _rms_bwd(acc, ins[0], ins[1])
        return [dx, dg]

    dcq, dqg = _row_mm(
        "q_up_bwd", [((dqn,), None), ((dqr_pre,), None)], w_q, nt=True, tm=t1024, tn=RQ, tk=_fit(2048, HV),
        outs=[((T, RQ), BF16, 'tile'), ((1, RQ), F32, 'acc')], epi=cq_bwd_epi,
        epi_ins=[(pb, pl.BlockSpec((t1024, RQ), lambda i, j, k: (i, WK // RQ))), (q_g, 'col')])
    gw_q = _tn_mm("dw_q_up", (cqn,), None, [((dqn,), None), ((dqr_pre,), None)],
                  tn=_fit(2048, HV), tk=t1024, out_dtype=BF16)

    def ckv_bwd_epi(acc, ins, i, j):
        blk, dkr_t, g = ins
        dx, dg = _rms_bwd(acc, blk[:, :RKV], g)
        parts = [dx, dkr_t]
        if WK > RKV + LANES:
            parts.append(jnp.zeros((dx.shape[0], WK - RKV - LANES), F32))
        return [jnp.concatenate(parts, axis=1), dg]

    dckv, dkvg = _row_mm(
        "kv_up_bwd", [((dkn,), None), ((dv,), None)], w_kv, nt=True, tm=t1024, tn=RKV, tk=_fit(2048, HV),
        outs=[((T, WK), BF16, pl.BlockSpec((t1024, WK), lambda i, j, k: (i, 0))), ((1, RKV), F32, 'acc')],
        epi=ckv_bwd_epi,
        epi_ins=[(pb, pl.BlockSpec((t1024, WK), lambda i, j, k: (i, 0))), (dkr_pre, 'row'), (kv_g, 'col')])
    gw_kv = _tn_mm("dw_kv_up", (c_lat,), None, [((dkn,), None), ((dv,), None)],
                   tn=_fit(2048, HV), tk=t1024, out_dtype=BF16)

    def ln1_bwd_epi(acc, ins, i, j):
        dr_up, xhat, rstd, g = ins
        dr, dg, db = _ln_bwd(alpha * dr_up + acc, xhat, rstd, g)
        return [dr, dg, db]

    dp_segs = [((dckv,), None), ((dcq,), None), ((dz,), None)]
    gw_ball = _tn_mm("dw_b_in", (h1b,), None, dp_segs, tn=tkb, tk=t1024, out_dtype=BF16)

    shard_cols = lambda g: jnp.moveaxis(g.reshape(g.shape[0], N_CHIPS, -1), 1, 0)
    shard_rows = lambda g: g.reshape(N_CHIPS, g.shape[0] // N_CHIPS, g.shape[1])
    gq = gw_q.reshape(RQ, 2, H, LANES)
    g_uq = jnp.concatenate(
        [gq[:, 0], gq[:, 1, :, :ROPE_HALF], gq[:, 1, :, 2 * ROPE_HALF:3 * ROPE_HALF]], axis=2)
    g_wd_full = jnp.concatenate(
        [gw_ball[:, :RKV], gw_ball[:, RKV:RKV + ROPE_HALF],
         gw_ball[:, RKV + 2 * ROPE_HALF:RKV + 3 * ROPE_HALF]], axis=1)
    late_names = ["kv_w_down", "kv_w_uk", "kv_w_uv", "b_w_in", "b_w_uq", "b_w_out"]
    late_w = [kv_w_down, kv_w_uk, kv_w_uv, b_w_in, b_w_uq, b_w_out]
    chip_major = [g_wd_full, gw_kv[:, :HV], gw_kv[:, HV:], shard_cols(gw_ball[:, WK:]), g_uq, gw_bout]
    late_grads = [g.reshape((N_CHIPS,) + w.shape) for g, w in zip(chip_major, late_w)]
    late_sems = _push_start("scatter_late_start", late_grads, by_target=True)

    dr0, dg0, db0 = _row_mm(
        "b_in_bwd", dp_segs, wb_all, nt=True, tm=t512, tn=D, tk=tkb,
        outs=[((T, D), F32, 'tile'), ((1, D), F32, 'acc'), ((1, D), F32, 'acc')], epi=ln1_bwd_epi,
        epi_ins=[(dr1, 'tile'), (xhat1, 'tile'), (rstd1, 'row'), (g0 + late_sems[4][0, 0], 'col')])

    def conv_branch_bwd_epi(acc, ins, i, j):
        u1_t, z, g, b = ins
        xhat, rstd = _ln_stats(u1_t)
        u2 = xhat * g + b
        du3 = acc * _silu(z)
        dz_a = acc * _silu(u2) * _silu_grad(z)
        du1, dg, db = _ln_bwd(du3 * _silu_grad(u2), xhat, rstd, g)
        return [du1, dz_a, dg, db]

    du1, dz_a, dng, dnb = _row_mm(
        "a_out_bwd", [((dr0,), None)], w_out, nt=True, tm=tmw, tn=E, tk=_fit(1024, D),
        outs=[((T, E), F32, 'tile'), ((T, E), BF16, 'tile'), ((1, E), F32, 'acc'), ((1, E), F32, 'acc')],
        epi=conv_branch_bwd_epi,
        epi_ins=[(u1, 'tile'), (proj, pl.BlockSpec((tmw, E), lambda i, j, k: (i, 2))), (norm_g, 'col'),
                 (norm_b, 'col')])
    gw_out, dbo = _tn_mm("dw_a_out", (u4,), None, [((dr0,), None)], tn=_fit(1024, D), tk=t1024,
                         out_dtype=BF16, colsum=True)
    mid_sems = _push_start("scatter_mid_start", [gw_out.reshape((N_CHIPS,) + a_w_out.shape)], by_target=True)
    dval, dgate, dcw, dcb = _conv_bwd(du1, proj, conv_w + mid_sems[4][0, 0], E, tmw)
    dproj_segs = [((dval,), None), ((dgate,), None), ((dz_a,), None)]
    gw_in, dbi = _tn_mm("dw_a_in", (xs,), None, dproj_segs, tn=_fit(2048, E), tk=t512, out_dtype=BF16,
                        colsum=True)

    def own_block_in_place(sent, landed):
        own = lax.dynamic_index_in_dim(sent, chip, 0, keepdims=True)
        return lax.dynamic_update_slice(landed, own, (chip,) + (0,) * (sent.ndim - 1))

    def reduce_and_update(tag, names_, sent, landed, w_, m_, v_):
        sums = [_sum_chips("sum_" + n, own_block_in_place(s, l)) for n, s, l in zip(names_, sent, landed)]
        theirs = _swap_cores("swap_cores_" + tag, sums)
        return {n: _adamw("adamw_" + n, [mine, other], w, m, v)
                for n, mine, other, w, m, v in zip(names_, sums, theirs, w_, m_, v_)}

    late_sent, late_landed = _push_wait("scatter_late_wait", *late_sems[:4], after=[dbi], by_target=True)
    mid_sent, mid_landed = _push_wait("scatter_mid_wait", *mid_sems[:4], after=[dbi], by_target=True)
    early_sems = _push_start("scatter_early_start", [shard_cols(gw_in).reshape((N_CHIPS,) + a_w_in.shape)],
                             by_target=True)
    (grad_x,) = _row_mm(
        "a_in_bwd", dproj_segs, w_in, nt=True, tm=tmw, tn=D, tk=E, b_whole=True,
        outs=[((T, D), F32, 'tile')], epi=lambda acc, ins, i, j: [alpha * ins[0] + acc + ins[1]],
        epi_ins=[(dr0, 'tile'), (jnp.zeros((1, D), F32) + early_sems[4][0, 0], 'col')])
    big_out = reduce_and_update(
        "late", ["a_w_out"] + late_names, mid_sent + late_sent, mid_landed + late_landed,
        [a_w_out] + late_w, [m_a_w_out, m_kv_w_down, m_kv_w_uk, m_kv_w_uv, m_b_w_in, m_b_w_uq, m_b_w_out],
        [v_a_w_out, v_kv_w_down, v_kv_w_uk, v_kv_w_uv, v_b_w_in, v_b_w_uq, v_b_w_out])

    small_full = [jnp.concatenate([dg0, dg1]), jnp.concatenate([db0, db1]), dbi, dcw, dcb, dng, dnb, dbo,
                  dkvg, dqg]
    sflat = jnp.concatenate([g.reshape(-1) for g in small_full])
    summed = _all_reduce_small(_pack_rows(sflat, F32, 8 * LANES)).reshape(-1)
    soff = 0
    sgrads = []
    for g in small_full:
        sgrads.append(summed[soff:soff + g.size].reshape(g.shape))
        soff += g.size
    local_cols = lambda g, n: lax.dynamic_slice_in_dim(g, chip * n, n, axis=g.ndim - 1)
    snames = ["ln_g", "ln_b", "a_b_in", "a_conv_w", "a_conv_b", "a_norm_g", "a_norm_b", "a_b_out",
              "kv_norm_g", "b_q_norm_g"]
    small_w = [ln_g, ln_b, a_b_in, a_conv_w, a_conv_b, a_norm_g, a_norm_b, a_b_out, kv_norm_g, b_q_norm_g]
    small_m = [m_ln_g, m_ln_b, m_a_b_in, m_a_conv_w, m_a_conv_b, m_a_norm_g, m_a_norm_b, m_a_b_out,
               m_kv_norm_g, m_b_q_norm_g]
    small_v = [v_ln_g, v_ln_b, v_a_b_in, v_a_conv_w, v_a_conv_b, v_a_norm_g, v_a_norm_b, v_a_b_out,
               v_kv_norm_g, v_b_q_norm_g]
    sharded = {"a_b_in", "a_conv_w", "a_conv_b", "a_norm_g", "a_norm_b", "a_b_out"}
    local_g = [(local_cols(g, w.shape[-1]) if n in sharded else g).reshape(w.shape)
               for n, g, w in zip(snames, sgrads, small_w)]
    at_least_2d = lambda a: a.reshape((1,) + a.shape) if a.ndim == 1 else a
    sres = _adamw_vectors([tuple(at_least_2d(a) for a in item)
                           for item in zip(local_g, small_w, small_m, small_v)])
    small_out = {n: [g] + [r.reshape(w.shape) for r in res]
                 for n, g, w, res in zip(snames, local_g, small_w, sres)}

    early_sent, early_landed = _push_wait(
        "scatter_early_wait", *early_sems[:4], by_target=True,
        after=[grad_x, big_out["b_w_out"][1], small_out["b_q_norm_g"][1]])
    big_out.update(reduce_and_update("early", ["a_w_in"], early_sent, early_landed,
                                     [a_w_in], [m_a_w_in], [v_a_w_in]))

    loss = lax.psum(loss_part[0, 0], ("x", "y", "c"))
    order = ["ln_g", "ln_b", "a_w_in", "a_b_in", "a_conv_w", "a_conv_b", "a_norm_g", "a_norm_b", "a_w_out",
             "a_b_out", "kv_w_down", "kv_norm_g", "kv_w_uk", "kv_w_uv", "b_w_in", "b_q_norm_g", "b_w_uq",
             "b_w_out"]
    outs = {**big_out, **small_out}
    result = [loss, grad_x[None]]
    for part in range(4):
        result += [outs[n][part] for n in order]
    return tuple(result)
```

```python
import functools
import math

import jax
import jax.numpy as jnp
from jax import lax
from jax.experimental import pallas as pl
from jax.experimental.pallas import tpu as pltpu

F32, BF16 = jnp.float32, jnp.bfloat16
NN = (((1,), (0,)), ((), ()))
NT = (((1,), (1,)), ((), ()))
TN = (((0,), (0,)), ((), ()))
MESH = pl.DeviceIdType.MESH
ANY = pl.BlockSpec(memory_space=pl.ANY)

LANES = 128
BF16_ROWS = 16
VMEM_LIMIT = 56 * 1024 * 1024
N_CHIPS = 4
LN_EPS = 1e-5
RMS_EPS = 1e-6
MASK_VALUE = -1e30
LOG2_E = math.log2(math.e)
ROPE_THETA = 10000.0
ROPE_DIM = 64
ROPE_HALF = ROPE_DIM // 2
ADAM_LR, ADAM_B1, ADAM_B2, ADAM_EPS, ADAM_WD, ADAM_STEP = 0.001, 0.9, 0.999, 1e-08, 0.01, 10

MAX_TILE = 2048
TM_WIDE = 256
TQ = 512
CONV_HALO = 32
CONV_LC = 512
CONV_SUB = 256
SUBLANES = 8
CONV_RC = 32
ADAM_ROWS = 64


def _dot(a, b, dims):
    return lax.dot_general(a.astype(BF16), b.astype(BF16), dims, preferred_element_type=F32)


def _sig(x):
    return 1.0 / (1.0 + jnp.exp(-x))


def _params(n_axes):
    return pltpu.CompilerParams(dimension_semantics=("arbitrary",) * n_axes, vmem_limit_bytes=VMEM_LIMIT)


def _gcd(*v):
    return functools.reduce(math.gcd, v)


def _fit(want, dim):
    return math.gcd(min(want, MAX_TILE), dim)


def _row_mm(name, a_segs, b, *, nt, tm, tn, tk, outs, epi, epi_ins=(), b_whole=False):
    M = a_segs[0][0][0].shape[0]
    N = b.shape[0] if nt else b.shape[1]
    nkb = [arrs[0].shape[1] // tk for arrs, _ in a_segs]
    koff = [sum(nkb[:s]) for s in range(len(nkb))]
    ni, nj, nk = M // tm, N // tn, sum(nkb)
    assert M % tm == 0 and N % tn == 0 and all(arrs[0].shape[1] % tk == 0 for arrs, _ in a_segs), name
    assert (b.shape[1] if nt else b.shape[0]) == nk * tk, name

    def spec_of(shape, kind):
        if isinstance(kind, pl.BlockSpec):
            return kind
        if kind == 'tile':
            return pl.BlockSpec((tm, tn), lambda i, j, k: (i, j))
        if kind == 'row':
            return pl.BlockSpec((tm, shape[1]), lambda i, j, k: (i, 0))
        if kind == 'col':
            return pl.BlockSpec((1, tn), lambda i, j, k: (0, j))
        assert kind == 'acc' and nj == 1, name
        return pl.BlockSpec(shape, lambda i, j, k: (0,) * len(shape))

    in_specs, operands = [], []
    for s, (arrs, _) in enumerate(a_segs):
        for arr in arrs:
            in_specs.append(pl.BlockSpec(
                (tm, tk), lambda i, j, k, s=s: (i, jnp.clip(k - koff[s], 0, nkb[s] - 1))))
            operands.append(arr)
    if b_whole:
        assert nt and nj == 1 and all(n == 1 for n in nkb), name
        in_specs.append(pl.BlockSpec(b.shape, lambda i, j, k: (0, 0)))
    else:
        in_specs.append(pl.BlockSpec((tn, tk), lambda i, j, k: (j, k)) if nt
                        else pl.BlockSpec((tk, tn), lambda i, j, k: (k, j)))
    operands.append(b)
    for arr, kind in epi_ins:
        in_specs.append(spec_of(arr.shape, kind))
        operands.append(arr)
    out_specs = [spec_of(shape, kind) for shape, _, kind in outs]
    out_shape = [jax.ShapeDtypeStruct(shape, dtype) for shape, dtype, _ in outs]
    n_seg_refs = [len(arrs) for arrs, _ in a_segs]

    def body(*refs):
        pos = 0
        seg_refs = []
        for n in n_seg_refs:
            seg_refs.append(refs[pos:pos + n])
            pos += n
        b_ref = refs[pos]
        e_refs = refs[pos + 1:pos + 1 + len(epi_ins)]
        o_refs = refs[pos + 1 + len(epi_ins):pos + 1 + len(epi_ins) + len(outs)]
        i, j, k = pl.program_id(0), pl.program_id(1), pl.program_id(2)

        def product(fn, rs, s=0):
            a = rs[0][...] if fn is None else fn(*[r[...] for r in rs])
            bt = b_ref[:, koff[s] * tk:(koff[s] + 1) * tk] if b_whole else b_ref[...]
            return _dot(a, bt, NT if nt else NN)

        def finish(acc):
            res = epi(acc, [r[...] for r in e_refs], i, j)
            for o_ref, (_, _, kind), r in zip(o_refs, outs, res):
                if isinstance(kind, str) and kind == 'acc':
                    @pl.when(i == 0)
                    def _(o_ref=o_ref, r=r):
                        o_ref[...] = r

                    @pl.when(i > 0)
                    def _(o_ref=o_ref, r=r):
                        o_ref[...] += r
                else:
                    o_ref[...] = r.astype(o_ref.dtype)

        if nk == 1:
            finish(product(a_segs[0][1], seg_refs[0]))
            return
        acc_ref = refs[-1]

        @pl.when(k == 0)
        def _():
            acc_ref[...] = jnp.zeros_like(acc_ref)

        for s, ((_, fn), rs) in enumerate(zip(a_segs, seg_refs)):
            def accumulate(fn=fn, rs=rs, s=s):
                acc_ref[...] += product(fn, rs, s)
            if len(a_segs) == 1:
                accumulate()
            else:
                pl.when(jnp.logical_and(k >= koff[s], k < koff[s] + nkb[s]))(accumulate)

        @pl.when(k == nk - 1)
        def _():
            finish(acc_ref[...])

    return pl.pallas_call(
        body, name=name, grid=(ni, nj, nk), in_specs=in_specs, out_specs=out_specs, out_shape=out_shape,
        scratch_shapes=[] if nk == 1 else [pltpu.VMEM((tm, tn), F32)], compiler_params=_params(3),
    )(*operands)


def _tn_mm(name, a_arrs, a_fn, b_segs, *, tn, tk, out_dtype, shard_major=False, colsum=False):
    T, M = a_arrs[0].shape
    nbj = [arrs[0].shape[1] // tn for arrs, _ in b_segs]
    joff = [sum(nbj[:s]) for s in range(len(nbj))]
    nj, nk = sum(nbj), T // tk
    N = nj * tn
    assert T % tk == 0 and all(arrs[0].shape[1] % tn == 0 for arrs, _ in b_segs), name

    in_specs = [pl.BlockSpec((tk, M), lambda j, k: (k, 0)) for _ in a_arrs]
    operands = list(a_arrs)
    for s, (arrs, _) in enumerate(b_segs):
        for arr in arrs:
            in_specs.append(pl.BlockSpec(
                (tk, tn), lambda j, k, s=s: (k, jnp.clip(j - joff[s], 0, nbj[s] - 1))))
            operands.append(arr)
    if shard_major:
        per = (N // N_CHIPS) // tn
        assert per * tn * N_CHIPS == N, name
        out_shape = [jax.ShapeDtypeStruct((N_CHIPS, M, N // N_CHIPS), out_dtype)]
        out_specs = [pl.BlockSpec((1, M, tn), lambda j, k: (j // per, 0, j % per))]
    else:
        out_shape = [jax.ShapeDtypeStruct((M, N), out_dtype)]
        out_specs = [pl.BlockSpec((M, tn), lambda j, k: (0, j))]
    if colsum:
        out_shape.append(jax.ShapeDtypeStruct((1, N), F32))
        out_specs.append(pl.BlockSpec((1, tn), lambda j, k: (0, j)))
    n_a = len(a_arrs)
    n_seg_refs = [len(arrs) for arrs, _ in b_segs]

    def body(*refs):
        a_refs = refs[:n_a]
        pos = n_a
        seg_refs = []
        for n in n_seg_refs:
            seg_refs.append(refs[pos:pos + n])
            pos += n
        o_ref = refs[pos]
        cs_ref = refs[pos + 1] if colsum else None
        acc_ref = refs[-1]
        j, k = pl.program_id(0), pl.program_id(1)

        @pl.when(k == 0)
        def _():
            acc_ref[...] = jnp.zeros_like(acc_ref)
            if colsum:
                cs_ref[...] = jnp.zeros_like(cs_ref)

        for s, ((_, fn), rs) in enumerate(zip(b_segs, seg_refs)):
            def accumulate(fn=fn, rs=rs):
                a = a_refs[0][...] if a_fn is None else a_fn(*[r[...] for r in a_refs])
                bt = rs[0][...] if fn is None else fn(*[r[...] for r in rs])
                acc_ref[...] += _dot(a, bt, TN)
                if colsum:
                    cs_ref[...] += jnp.sum(bt.astype(F32), axis=0, keepdims=True)
            if len(b_segs) == 1:
                accumulate()
            else:
                pl.when(jnp.logical_and(j >= joff[s], j < joff[s] + nbj[s]))(accumulate)

        @pl.when(k == nk - 1)
        def _():
            if shard_major:
                o_ref[0] = acc_ref[...].astype(o_ref.dtype)
            else:
                o_ref[...] = acc_ref[...].astype(o_ref.dtype)

    res = pl.pallas_call(
        body, name=name, grid=(nj, nk), in_specs=in_specs, out_specs=out_specs, out_shape=out_shape,
        scratch_shapes=[pltpu.VMEM((M, tn), F32)], compiler_params=_params(2),
    )(*operands)
    return res if colsum else res[0]


def _silu(z):
    return z * _sig(z)


def _silu_grad(z):
    s = _sig(z)
    return s * (1.0 + z * (1.0 - s))


def _gate(o, z):
    return o * _silu(z)


def _ln_stats(r):
    mu = jnp.mean(r, axis=1, keepdims=True)
    xc = r - mu
    var = jnp.mean(xc * xc, axis=1, keepdims=True)
    rstd = lax.rsqrt(var + LN_EPS)
    return xc * rstd, rstd


def _ln_bwd(dy, xhat, rstd, g):
    dxh = dy * g
    m1 = jnp.mean(dxh, axis=1, keepdims=True)
    m2 = jnp.mean(dxh * xhat, axis=1, keepdims=True)
    return (rstd * (dxh - m1 - xhat * m2), jnp.sum(dy * xhat, axis=0, keepdims=True),
            jnp.sum(dy, axis=0, keepdims=True))


def _rms_fwd(x, g):
    rstd = lax.rsqrt(jnp.mean(x * x, axis=1, keepdims=True) + RMS_EPS)
    return x * rstd * g


def _rms_bwd(dy, x, g):
    rstd = lax.rsqrt(jnp.mean(x * x, axis=1, keepdims=True) + RMS_EPS)
    xn = x * rstd
    dxn = dy * g
    return rstd * (dxn - xn * jnp.mean(dxn * xn, axis=1, keepdims=True)), jnp.sum(dy * xn, axis=0, keepdims=True)


def _rope(x, cos, sin, transpose=False):
    parts = []
    for g in range(x.shape[1] // LANES):
        xg = x[:, g * LANES:(g + 1) * LANES]
        if transpose:
            parts.append(xg * cos + pltpu.roll(xg * sin, LANES // 2, 1))
        else:
            parts.append(xg * cos + pltpu.roll(xg, LANES // 2, 1) * sin)
    return parts[0] if len(parts) == 1 else jnp.concatenate(parts, axis=1)


def _shifted_rows(window, rc):
    n = window.shape[0]
    for b in range(SUBLANES):
        rolled = window if b == 0 else pltpu.roll(window, n - b, 0)
        for a8 in range(0, n - rc - b + 1, SUBLANES):
            yield a8 + b, rolled[a8:a8 + rc]


def _conv_fwd(proj, conv_w, conv_b, E, tm):
    T = proj.shape[0]
    kc = conv_w.shape[0]
    lc, hb, rc = min(CONV_LC, E), CONV_HALO, min(CONV_RC, tm)
    nl, ni, ratio = E // lc, T // tm, tm // hb
    gate_off = E // lc

    sub = min(CONV_SUB, lc)
    base = hb - (kc - 1)

    def body(val_ref, gate_ref, valh_ref, gateh_ref, w_ref, cb_ref, u1_ref, ubuf):
        i = pl.program_id(1)
        ubuf[hb:, :] = val_ref[...] * _sig(gate_ref[...])
        halo = valh_ref[...] * _sig(gateh_ref[...])
        ubuf[0:hb, :] = jnp.where(i > 0, halo, 0.0)
        for l0 in range(0, lc, sub):
            ls = slice(l0, l0 + sub)
            for r0 in range(0, tm, rc):
                acc = jnp.zeros((rc, sub), F32) + cb_ref[:, ls]
                for off, rows in _shifted_rows(ubuf[r0:r0 + hb + rc, ls], rc):
                    if 0 <= off - base < kc:
                        acc += w_ref[off - base:off - base + 1, ls] * rows
                u1_ref[r0:r0 + rc, ls] = acc

    return pl.pallas_call(
        body, name="conv_fwd", grid=(nl, ni),
        in_specs=[
            pl.BlockSpec((tm, lc), lambda l, i: (i, l)),
            pl.BlockSpec((tm, lc), lambda l, i: (i, gate_off + l)),
            pl.BlockSpec((hb, lc), lambda l, i: (jnp.maximum(i * ratio - 1, 0), l)),
            pl.BlockSpec((hb, lc), lambda l, i: (jnp.maximum(i * ratio - 1, 0), gate_off + l)),
            pl.BlockSpec((kc, lc), lambda l, i: (0, l)),
            pl.BlockSpec((1, lc), lambda l, i: (0, l)),
        ],
        out_specs=pl.BlockSpec((tm, lc), lambda l, i: (i, l)),
        out_shape=jax.ShapeDtypeStruct((T, E), F32),
        scratch_shapes=[pltpu.VMEM((hb + tm, lc), F32)], compiler_params=_params(2),
    )(proj, proj, proj, proj, conv_w, conv_b)


def _conv_post(u1, proj, norm_g, norm_b, E, tm):
    T = u1.shape[0]

    def body(u1_ref, z_ref, g_ref, b_ref, u4_ref):
        xhat, _ = _ln_stats(u1_ref[...])
        u4_ref[...] = (_silu(xhat * g_ref[...] + b_ref[...]) * _silu(z_ref[...])).astype(BF16)

    return pl.pallas_call(
        body, name="conv_post", grid=(T // tm,),
        in_specs=[pl.BlockSpec((tm, E), lambda i: (i, 0)), pl.BlockSpec((tm, E), lambda i: (i, 2)),
                  pl.BlockSpec((1, E), lambda i: (0, 0)), pl.BlockSpec((1, E), lambda i: (0, 0))],
        out_specs=pl.BlockSpec((tm, E), lambda i: (i, 0)),
        out_shape=jax.ShapeDtypeStruct((T, E), BF16), compiler_params=_params(1),
    )(u1, proj, norm_g, norm_b)


def _conv_bwd(du1, proj, conv_w, E, tm):
    T = du1.shape[0]
    kc = conv_w.shape[0]
    lc, hb, rc = min(CONV_LC, E), CONV_HALO, min(CONV_RC, tm)
    nl, ni, ratio = E // lc, T // tm, tm // hb
    gate_off = E // lc
    last_halo = T // hb - 1

    sub = min(CONV_SUB, lc)
    base = hb - (kc - 1)

    def body(du_ref, dun_ref, val_ref, gate_ref, valh_ref, gateh_ref, w_ref,
             dval_ref, dgate_ref, dw_ref, db_ref, ubuf, dbuf, sbuf, dw_sc):
        i = pl.program_id(1)
        sbuf[...] = _sig(gate_ref[...])
        ubuf[hb:, :] = val_ref[...] * sbuf[...]
        halo = valh_ref[...] * _sig(gateh_ref[...])
        ubuf[0:hb, :] = jnp.where(i > 0, halo, 0.0)
        dbuf[0:tm, :] = du_ref[...]
        dbuf[tm:, :] = jnp.where(i < ni - 1, dun_ref[...], 0.0)

        @pl.when(i == 0)
        def _():
            dw_sc[...] = jnp.zeros_like(dw_sc)
            db_ref[...] = jnp.zeros_like(db_ref)

        db_ref[...] += jnp.sum(du_ref[...], axis=0, keepdims=True)
        for l0 in range(0, lc, sub):
            ls = slice(l0, l0 + sub)
            for r0 in range(0, tm, rc):
                dwin = dbuf[r0:r0 + rc + hb, ls]
                dchunk = dwin[0:rc]
                for off, rows in _shifted_rows(ubuf[r0:r0 + hb + rc, ls], rc):
                    k = off - base
                    if 0 <= k < kc:
                        prod = rows * dchunk
                        part = prod[0:SUBLANES]
                        for s8 in range(SUBLANES, rc, SUBLANES):
                            part = part + prod[s8:s8 + SUBLANES]
                        dw_sc[k, :, ls] += part
                acc = jnp.zeros((rc, sub), F32)
                for off, rows in _shifted_rows(dwin, rc):
                    k = (kc - 1) - off
                    if 0 <= k < kc:
                        acc += w_ref[k:k + 1, ls] * rows
                v, s = val_ref[r0:r0 + rc, ls], sbuf[r0:r0 + rc, ls]
                dval_ref[r0:r0 + rc, ls] = (acc * s).astype(BF16)
                dgate_ref[r0:r0 + rc, ls] = (acc * v * s * (1.0 - s)).astype(BF16)

        @pl.when(i == ni - 1)
        def _():
            for k in range(kc):
                dw_ref[k:k + 1, :] = jnp.sum(dw_sc[k], axis=0, keepdims=True)

    return pl.pallas_call(
        body, name="conv_bwd", grid=(nl, ni),
        in_specs=[
            pl.BlockSpec((tm, lc), lambda l, i: (i, l)),
            pl.BlockSpec((hb, lc), lambda l, i: (jnp.minimum((i + 1) * ratio, last_halo), l)),
            pl.BlockSpec((tm, lc), lambda l, i: (i, l)),
            pl.BlockSpec((tm, lc), lambda l, i: (i, gate_off + l)),
            pl.BlockSpec((hb, lc), lambda l, i: (jnp.maximum(i * ratio - 1, 0), l)),
            pl.BlockSpec((hb, lc), lambda l, i: (jnp.maximum(i * ratio - 1, 0), gate_off + l)),
            pl.BlockSpec((kc, lc), lambda l, i: (0, l)),
        ],
        out_specs=[pl.BlockSpec((tm, lc), lambda l, i: (i, l)), pl.BlockSpec((tm, lc), lambda l, i: (i, l)),
                   pl.BlockSpec((kc, lc), lambda l, i: (0, l)), pl.BlockSpec((1, lc), lambda l, i: (0, l))],
        out_shape=[jax.ShapeDtypeStruct((T, E), BF16), jax.ShapeDtypeStruct((T, E), BF16),
                   jax.ShapeDtypeStruct((kc, E), F32), jax.ShapeDtypeStruct((1, E), F32)],
        scratch_shapes=[pltpu.VMEM((hb + tm, lc), F32), pltpu.VMEM((tm + hb, lc), F32),
                        pltpu.VMEM((tm, lc), F32), pltpu.VMEM((kc, SUBLANES, lc), F32)],
        compiler_params=_params(2),
    )(du1, du1, proj, proj, proj, proj, conv_w)


def _norm_prep(pb, kv_g, q_g, cos, sin, rkv, rq, wk, tm):
    T = pb.shape[0]

    def body(ckv_ref, cq_ref, kg_ref, qg_ref, cos_ref, sin_ref, c_ref, kr_ref, cqn_ref):
        blk = ckv_ref[...]
        c_ref[...] = _rms_fwd(blk[:, :rkv], kg_ref[...]).astype(BF16)
        kr_ref[...] = _rope(blk[:, rkv:rkv + LANES], cos_ref[...], sin_ref[...]).astype(BF16)
        cqn_ref[...] = _rms_fwd(cq_ref[...], qg_ref[...]).astype(BF16)

    return pl.pallas_call(
        body, name="norm_prep", grid=(T // tm,),
        in_specs=[pl.BlockSpec((tm, wk), lambda i: (i, 0)), pl.BlockSpec((tm, rq), lambda i: (i, wk // rq)),
                  pl.BlockSpec((1, rkv), lambda i: (0, 0)), pl.BlockSpec((1, rq), lambda i: (0, 0)),
                  pl.BlockSpec((tm, LANES), lambda i: (i, 0)), pl.BlockSpec((tm, LANES), lambda i: (i, 0))],
        out_specs=[pl.BlockSpec((tm, rkv), lambda i: (i, 0)), pl.BlockSpec((tm, LANES), lambda i: (i, 0)),
                   pl.BlockSpec((tm, rq), lambda i: (i, 0))],
        out_shape=[jax.ShapeDtypeStruct((T, rkv), BF16), jax.ShapeDtypeStruct((T, LANES), BF16),
                   jax.ShapeDtypeStruct((T, rq), BF16)],
        compiler_params=_params(1),
    )(pb, pb, kv_g, q_g, cos, sin)


def _attn_fwd(q_all, kv, kr, H, tq, scale):
    T = q_all.shape[0]
    nq = T // tq
    pair = 2
    W = pair * LANES
    assert H % pair == 0
    hp_n = H // pair

    def body(qn_ref, qr_ref, kn_ref, kr_ref, v_ref, o_ref, lse_ref, *scratch):
        qi = pl.program_id(1)
        chains = [scratch[4 * a:4 * a + 4] for a in range(pair)]
        lanes = [slice(a * LANES, (a + 1) * LANES) for a in range(pair)]
        groups = [slice(c * LANES, (c + 1) * LANES) for c in range(tq // LANES)]

        def fold(x, op):
            r = x[:, groups[0]]
            for gsl in groups[1:]:
                r = op(r, x[:, gsl])
            return r

        for _, m_sc, l_sc, acc_sc in chains:
            m_sc[...] = jnp.full_like(m_sc, MASK_VALUE)
            l_sc[...] = jnp.zeros_like(l_sc)
            acc_sc[...] = jnp.zeros_like(acc_sc)

        def scores(j, masked):
            rows = pl.ds(pl.multiple_of(j * tq, tq), tq)
            krope = kr_ref[rows, :]
            for a, (s_sc, m_sc, _, _) in enumerate(chains):
                q = jnp.concatenate([qn_ref[:, lanes[a]], qr_ref[:, lanes[a]]], axis=1)
                k = jnp.concatenate([kn_ref[rows, lanes[a]], krope], axis=1)
                s = _dot(q, k, NT) * (scale * LOG2_E)
                if masked:
                    row = lax.broadcasted_iota(jnp.int32, s.shape, 0)
                    col = lax.broadcasted_iota(jnp.int32, s.shape, 1)
                    s = jnp.where(col <= row, s, MASK_VALUE)
                s_sc[j] = s
                m_sc[...] = jnp.maximum(m_sc[...], fold(s, jnp.maximum))

        def two_per_trip(fn, count):
            def two(p, carry):
                fn(2 * p)
                fn(2 * p + 1)
                return carry

            lax.fori_loop(0, count // 2, two, 0)

            @pl.when(count % 2 == 1)
            def _():
                fn(count - 1)

        two_per_trip(functools.partial(scores, masked=False), qi)
        scores(qi, True)
        for _, m_sc, _, _ in chains:
            m_sc[...] = jnp.broadcast_to(jnp.max(m_sc[...], axis=1, keepdims=True), m_sc.shape)

        def weigh(j):
            rows = pl.ds(pl.multiple_of(j * tq, tq), tq)
            for a, (s_sc, m_sc, l_sc, acc_sc) in enumerate(chains):
                s, m = s_sc[j], m_sc[...]
                p = jnp.concatenate([jnp.exp2(s[:, gsl] - m) for gsl in groups], axis=1)
                l_sc[...] += fold(p, jnp.add)
                acc_sc[...] += _dot(p, v_ref[rows, lanes[a]], NN)

        two_per_trip(weigh, qi + 1)
        for a, (_, m_sc, l_sc, acc_sc) in enumerate(chains):
            l = jnp.sum(l_sc[...], axis=1, keepdims=True)
            o_ref[:, lanes[a]] = acc_sc[...] / l
            lse_ref[a] = m_sc[:, 0:1] * (1.0 / LOG2_E) + jnp.log(l)

    chain_scratch = [pltpu.VMEM((nq, tq, tq), F32), pltpu.VMEM((tq, LANES), F32), pltpu.VMEM((tq, LANES), F32),
                     pltpu.VMEM((tq, LANES), F32)]
    return pl.pallas_call(
        body, name="attn_fwd", grid=(hp_n, nq),
        in_specs=[pl.BlockSpec((tq, W), lambda hp, qi: (qi, hp)),
                  pl.BlockSpec((tq, W), lambda hp, qi: (qi, hp_n + hp)),
                  pl.BlockSpec((T, W), lambda hp, qi: (0, hp)),
                  pl.BlockSpec((T, LANES), lambda hp, qi: (0, 0)),
                  pl.BlockSpec((T, W), lambda hp, qi: (0, hp_n + hp))],
        out_specs=[pl.BlockSpec((tq, W), lambda hp, qi: (qi, hp)),
                   pl.BlockSpec((pair, tq, 1), lambda hp, qi: (hp, qi, 0))],
        out_shape=[jax.ShapeDtypeStruct((T, H * LANES), F32), jax.ShapeDtypeStruct((H, T, 1), F32)],
        scratch_shapes=chain_scratch * pair, compiler_params=_params(2),
    )(q_all, q_all, kv, kr, kv)


def _attn_bwd(q_all, kv, kr, do, o, lse, H, tq, scale):
    T = q_all.shape[0]
    nq = T // tq
    HV = H * LANES
    pair = 2
    tk2 = pair * tq
    ng = T // tk2
    assert ng * tk2 == T and pair == 2

    def body(qn_ref, qr_ref, kn_ref, kr_ref, v_ref, do_ref, o_ref, lse_ref,
             dqn_ref, dqr_ref, dkn_ref, dkr_ref, dv_ref, dq_sc, dk_sc, dv_sc):
        g = pl.program_id(1)

        @pl.when(g == 0)
        def _():
            dq_sc[...] = jnp.zeros_like(dq_sc)

        dk_sc[...] = jnp.zeros_like(dk_sc)
        dv_sc[...] = jnp.zeros_like(dv_sc)

        def block(qi, modes):
            rows = pl.ds(pl.multiple_of(qi * tq, tq), tq)
            q = jnp.concatenate([qn_ref[rows, :], qr_ref[rows, :]], axis=1)
            dov = do_ref[rows, :]
            delta = jnp.sum(dov.astype(F32) * o_ref[rows, :], axis=1, keepdims=True)
            lse_q = lse_ref[0, rows, :]
            dq = None
            for c, masked in enumerate(modes):
                if masked is None:
                    continue
                kr_ = slice(c * tq, (c + 1) * tq)
                k = jnp.concatenate([kn_ref[kr_, :], kr_ref[kr_, :]], axis=1)
                s = _dot(q, k, NT) * scale
                if masked:
                    row = lax.broadcasted_iota(jnp.int32, s.shape, 0)
                    col = lax.broadcasted_iota(jnp.int32, s.shape, 1)
                    s = jnp.where(col <= row, s, MASK_VALUE)
                p = jnp.exp(s - lse_q)
                dv_sc[kr_, :] += _dot(p, dov, TN)
                dp = _dot(dov, v_ref[kr_, :], NT)
                ds = (p * (dp - delta) * scale).astype(BF16)
                dk_sc[kr_, :] += _dot(ds, q, TN)
                part = _dot(ds, k, NN)
                dq = part if dq is None else dq + part
            dq_sc[rows, :] += dq

        block(pair * g, (True, None))
        block(pair * g + 1, (False, True))

        def below(trip, carry):
            qi = pair * g + pair + 2 * trip
            block(qi, (False, False))
            block(qi + 1, (False, False))
            return carry

        lax.fori_loop(0, (nq - pair * g - pair) // 2, below, 0)
        dkn_ref[...] = dk_sc[:, :LANES].astype(BF16)
        dkr_ref[...] = dk_sc[:, LANES:]
        dv_ref[...] = dv_sc[...].astype(BF16)

        @pl.when(g == ng - 1)
        def _():
            dqn_ref[...] = dq_sc[:, :LANES].astype(BF16)
            dqr_ref[...] = dq_sc[:, LANES:]

    whole = lambda col: pl.BlockSpec((T, LANES), col)
    tile = lambda col: pl.BlockSpec((tk2, LANES), col)
    return pl.pallas_call(
        body, name="attn_bwd", grid=(H, ng),
        in_specs=[whole(lambda h, g: (0, h)), whole(lambda h, g: (0, H + h)),
                  tile(lambda h, g: (g, h)), tile(lambda h, g: (g, 0)), tile(lambda h, g: (g, H + h)),
                  whole(lambda h, g: (0, h)), whole(lambda h, g: (0, h)),
                  pl.BlockSpec((1, T, 1), lambda h, g: (h, 0, 0))],
        out_specs=[whole(lambda h, g: (0, h)), whole(lambda h, g: (0, h)),
                   tile(lambda h, g: (g, h)), tile(lambda h, g: (g, h)), tile(lambda h, g: (g, h))],
        out_shape=[jax.ShapeDtypeStruct((T, HV), BF16), jax.ShapeDtypeStruct((T, HV), F32),
                   jax.ShapeDtypeStruct((T, HV), BF16), jax.ShapeDtypeStruct((T, HV), F32),
                   jax.ShapeDtypeStruct((T, HV), BF16)],
        scratch_shapes=[pltpu.VMEM((T, 2 * LANES), F32), pltpu.VMEM((tk2, 2 * LANES), F32),
                        pltpu.VMEM((tk2, LANES), F32)],
        compiler_params=_params(2),
    )(q_all, q_all, kv, kr, kv, do, o, lse)


def _rope_bwd(dqr, dkr_heads, cos, sin, H, tm):
    T, HV = dqr.shape

    def body(dqr_ref, dkr_ref, cos_ref, sin_ref, dq_ref, dk_ref):
        c, s = cos_ref[...], sin_ref[...]
        dq_ref[...] = _rope(dqr_ref[...], c, s, transpose=True).astype(BF16)
        dk = dkr_ref[...]
        tot = dk[:, 0:LANES]
        for h in range(1, H):
            tot = tot + dk[:, h * LANES:(h + 1) * LANES]
        dk_ref[...] = _rope(tot, c, s, transpose=True)

    return pl.pallas_call(
        body, name="rope_bwd", grid=(T // tm,),
        in_specs=[pl.BlockSpec((tm, HV), lambda i: (i, 0)), pl.BlockSpec((tm, HV), lambda i: (i, 0)),
                  pl.BlockSpec((tm, LANES), lambda i: (i, 0)), pl.BlockSpec((tm, LANES), lambda i: (i, 0))],
        out_specs=[pl.BlockSpec((tm, HV), lambda i: (i, 0)), pl.BlockSpec((tm, LANES), lambda i: (i, 0))],
        out_shape=[jax.ShapeDtypeStruct((T, HV), BF16), jax.ShapeDtypeStruct((T, LANES), F32)],
        compiler_params=_params(1),
    )(dqr, dkr_heads, cos, sin)


def _place():
    x, y, c = lax.axis_index("x"), lax.axis_index("y"), lax.axis_index("c")
    chips = [(1 - x, y), (x, 1 - y), (1 - x, 1 - y)]
    return x, y, c, chips


def _all_gather_chips(pack):
    rows = pack.shape[0]
    half = rows // 2
    assert half * 2 == rows and half % BF16_ROWS == 0

    def body(w_ref, out_ref, send_sems, recv_sems, local_sem):
        x, y, c, chips = _place()
        sibling = (x, y, 1 - c)

        def region(px, py, pc):
            return out_ref.at[2 * px + py, pl.ds(pc * half, half), :]

        def copy(k, block, to, src=None):
            return pltpu.make_async_remote_copy(
                src_ref=region(*block) if src is None else src, dst_ref=region(*block),
                send_sem=send_sems.at[k], recv_sem=recv_sems.at[k], device_id=to, device_id_type=MESH)

        mine = pltpu.make_async_copy(w_ref, out_ref.at[2 * x + y], local_sem)
        mine.start()
        my_half = w_ref.at[pl.ds(c * half, half), :]
        first = [copy(j, (x, y, c), (*chip, c), src=my_half) for j, chip in enumerate(chips)]
        for cp in first:
            cp.start()
        passed = [copy(3 + j, (*chip, c), sibling) for j, chip in enumerate(chips)]
        for j, chip in enumerate(chips):
            copy(j, (*chip, c), (x, y, c)).wait_recv()
            passed[j].start()
        for j, chip in enumerate(chips):
            copy(3 + j, (*chip, 1 - c), (x, y, c)).wait_recv()
        for cp in first + passed:
            cp.wait_send()
        mine.wait()

    return pl.pallas_call(
        body, name="gather_weights", in_specs=[ANY], out_specs=ANY,
        out_shape=jax.ShapeDtypeStruct((N_CHIPS,) + pack.shape, pack.dtype),
        scratch_shapes=[pltpu.SemaphoreType.DMA((6,)), pltpu.SemaphoreType.DMA((6,)), pltpu.SemaphoreType.DMA],
    )(pack)


def _swap_cores(name, parts):
    n = len(parts)

    def body(*refs):
        p_refs, r_refs = refs[:n], refs[n:2 * n]
        send_sems, recv_sems = refs[2 * n:]
        x, y, c, _ = _place()
        copies = [pltpu.make_async_remote_copy(
            src_ref=p_refs[w], dst_ref=r_refs[w], send_sem=send_sems.at[w], recv_sem=recv_sems.at[w],
            device_id=(x, y, 1 - c), device_id_type=MESH) for w in range(n)]
        for cp in copies:
            cp.start()
        for cp in copies:
            cp.wait()

    return pl.pallas_call(
        body, name=name, in_specs=[ANY] * n, out_specs=[ANY] * n,
        out_shape=[jax.ShapeDtypeStruct(p.shape, p.dtype) for p in parts],
        scratch_shapes=[pltpu.SemaphoreType.DMA((n,)), pltpu.SemaphoreType.DMA((n,))],
    )(*parts)


HBM = pl.BlockSpec(memory_space=pltpu.HBM)
SEM = pl.BlockSpec(memory_space=pltpu.SEMAPHORE)
EFFECT = pltpu.SideEffectType.DATAFLOW_SIDE_EFFECTING


def _push_copies(a_refs, l_refs, send_sems, recv_sems, by_target):
    x, y, c, chips = _place()
    me = 2 * x + y
    out = []
    for w, (a_ref, l_ref) in enumerate(zip(a_refs, l_refs)):
        for j, (px, py) in enumerate(chips):
            peer = 2 * px + py
            out.append((
                pltpu.make_async_remote_copy(
                    src_ref=a_ref.at[peer] if by_target else a_ref, dst_ref=l_ref.at[me],
                    send_sem=send_sems.at[3 * w + j], recv_sem=recv_sems.at[3 * w + j],
                    device_id=(px, py, c), device_id_type=MESH),
                pltpu.make_async_remote_copy(
                    src_ref=a_ref.at[me] if by_target else a_ref, dst_ref=l_ref.at[peer],
                    send_sem=send_sems.at[3 * w + j], recv_sem=recv_sems.at[3 * w + j],
                    device_id=(px, py, c), device_id_type=MESH)))
    return out


def _push_start(name, arrs, by_target):
    n = len(arrs)
    lands = [lax.empty((N_CHIPS,) + (a.shape[1:] if by_target else a.shape), a.dtype) for a in arrs]

    def body(*refs):
        a_refs, l_refs = refs[:n], refs[n:2 * n]
        send_sems, recv_sems = refs[2 * n], refs[2 * n + 1]
        token = refs[-1]
        for send, _ in _push_copies(a_refs, l_refs, send_sems, recv_sems, by_target):
            send.start()
        token[...] = jnp.zeros_like(token)

    res = pl.pallas_call(
        body, name=name,
        out_shape=(pltpu.SemaphoreType.DMA((3 * n,)), pltpu.SemaphoreType.DMA((3 * n,)),
                   *[pltpu.HBM(a.shape, a.dtype) for a in arrs], *[pltpu.HBM(l.shape, l.dtype) for l in lands],
                   jax.ShapeDtypeStruct((8, LANES), F32)),
        in_specs=[HBM] * (2 * n), out_specs=(SEM, SEM, *[HBM] * (2 * n), pl.BlockSpec(memory_space=pltpu.VMEM)),
        input_output_aliases={i: 2 + i for i in range(2 * n)},
        compiler_params=pltpu.CompilerParams(has_side_effects=EFFECT),
    )(*[pltpu.with_memory_space_constraint(a, pltpu.HBM) for a in list(arrs) + lands])
    return res[0], res[1], list(res[2:2 + n]), list(res[2 + n:2 + 2 * n]), res[-1]


def _push_wait(name, send_sems, recv_sems, arrs, lands, after, by_target):
    n = len(arrs)

    def body(*refs):
        a_refs, l_refs = refs[:n], refs[n:2 * n]
        s_sems, r_sems = refs[2 * n], refs[2 * n + 1]
        for send, recv in _push_copies(a_refs, l_refs, s_sems, r_sems, by_target):
            send.wait_send()
            recv.wait_recv()

    res = pl.pallas_call(
        body, name=name,
        out_shape=[pltpu.HBM(a.shape, a.dtype) for a in list(arrs) + list(lands)],
        in_specs=[HBM] * (2 * n) + [SEM, SEM] + [ANY] * len(after), out_specs=[HBM] * (2 * n),
        input_output_aliases={i: i for i in range(2 * n)},
        compiler_params=pltpu.CompilerParams(has_side_effects=EFFECT),
    )(*arrs, *lands, send_sems, recv_sems, *after)
    return list(res[:n]), list(res[n:])


def _all_reduce_small(part):
    n_dev = 8

    def body(p_ref, out_ref, buf, send_sems, recv_sems):
        x, y, c, _ = _place()
        me = 4 * x + 2 * y + c
        buf[me] = p_ref[...]
        copies = []
        for k in range(1, n_dev):
            kx, ky, kc = (k >> 2) & 1, (k >> 1) & 1, k & 1
            peer = (x ^ kx, y ^ ky, c ^ kc)
            cp = pltpu.make_async_remote_copy(
                src_ref=p_ref, dst_ref=buf.at[me], send_sem=send_sems.at[k - 1], recv_sem=recv_sems.at[k - 1],
                device_id=peer, device_id_type=MESH)
            cp.start()
            copies.append(cp)
        for k in range(1, n_dev):
            pltpu.make_async_remote_copy(
                src_ref=p_ref, dst_ref=buf.at[me ^ k], send_sem=send_sems.at[k - 1], recv_sem=recv_sems.at[k - 1],
                device_id=(x, y, c), device_id_type=MESH).wait_recv()
        for cp in copies:
            cp.wait_send()
        tot = buf[0]
        for d in range(1, n_dev):
            tot = tot + buf[d]
        out_ref[...] = tot

    vm = pl.BlockSpec(memory_space=pltpu.VMEM)
    return pl.pallas_call(
        body, name="all_reduce_small", in_specs=[vm], out_specs=vm,
        out_shape=jax.ShapeDtypeStruct(part.shape, F32),
        scratch_shapes=[pltpu.VMEM((n_dev,) + part.shape, F32), pltpu.SemaphoreType.DMA((n_dev - 1,)),
                        pltpu.SemaphoreType.DMA((n_dev - 1,))],
    )(part)


def _row_tiles(shape):
    ax = next(d for d, s in enumerate(shape) if s > 1)
    tr = _gcd(ADAM_ROWS, shape[ax])
    block = tuple(tr if d == ax else s for d, s in enumerate(shape))
    return shape[ax] // tr, block, lambda i: tuple(i if d == ax else 0 for d in range(len(shape)))


def _sum_chips(name, r):
    shape = r.shape[1:]
    steps, block, index = _row_tiles(shape)

    def body(r_ref, o_ref):
        tot = r_ref[0].astype(F32)
        for q in range(1, N_CHIPS):
            tot = tot + r_ref[q].astype(F32)
        o_ref[...] = tot

    return pl.pallas_call(
        body, name=name, grid=(steps,),
        in_specs=[pl.BlockSpec((N_CHIPS,) + block, lambda i: (0,) + index(i))],
        out_specs=pl.BlockSpec(block, index),
        out_shape=jax.ShapeDtypeStruct(shape, F32), compiler_params=_params(1),
    )(r)


def _adamw_math(g, w, m, v):
    mn = ADAM_B1 * m + (1.0 - ADAM_B1) * g
    vn = ADAM_B2 * v + (1.0 - ADAM_B2) * jnp.square(g)
    m_hat = mn / (1.0 - ADAM_B1 ** ADAM_STEP)
    v_hat = vn / (1.0 - ADAM_B2 ** ADAM_STEP)
    return -ADAM_LR * (m_hat / (jnp.sqrt(v_hat) + ADAM_EPS) + ADAM_WD * w), mn, vn


def _adamw(name, g_parts, w, m, v):
    steps, block, index = _row_tiles(w.shape)
    n = len(g_parts)

    def body(*refs):
        g = refs[0][...]
        for r in refs[1:n]:
            g = g + r[...]
        w_ref, m_ref, v_ref, go_ref, d_ref, mo_ref, vo_ref = refs[n:]
        go_ref[...] = g
        d_ref[...], mo_ref[...], vo_ref[...] = _adamw_math(g, w_ref[...], m_ref[...], v_ref[...])

    spec = pl.BlockSpec(block, index)
    return pl.pallas_call(
        body, name=name, grid=(steps,), in_specs=[spec] * (n + 3), out_specs=[spec] * 4,
        out_shape=[jax.ShapeDtypeStruct(w.shape, F32)] * 4, compiler_params=_params(1),
    )(*g_parts, w, m, v)


def _adamw_vectors(items):
    n = len(items)

    def body(*refs):
        ins, outs = refs[:4 * n], refs[4 * n:]
        for k in range(n):
            g, w, m, v = (r[...] for r in ins[4 * k:4 * k + 4])
            outs[3 * k][...], outs[3 * k + 1][...], outs[3 * k + 2][...] = _adamw_math(g, w, m, v)

    vm = pl.BlockSpec(memory_space=pltpu.VMEM)
    res = pl.pallas_call(
        body, name="adamw_vectors", in_specs=[vm] * (4 * n), out_specs=[vm] * (3 * n),
        out_shape=[jax.ShapeDtypeStruct(it[1].shape, F32) for it in items for _ in range(3)],
    )(*[a for it in items for a in it])
    return [res[3 * k:3 * k + 3] for k in range(n)]


def _pack_rows(flat, dtype, multiple):
    n = flat.shape[0]
    total = -(-n // multiple) * multiple
    return jnp.pad(flat, (0, total - n)).astype(dtype).reshape(total // LANES, LANES)


def kernel(x, positions, ln_g, ln_b, a_w_in, a_b_in, a_conv_w, a_conv_b, a_norm_g, a_norm_b, a_w_out, a_b_out, kv_w_down, kv_norm_g, kv_w_uk, kv_w_uv, b_w_in, b_q_norm_g, b_w_uq, b_w_out, loss_target, m_ln_g, m_ln_b, m_a_w_in, m_a_b_in, m_a_conv_w, m_a_conv_b, m_a_norm_g, m_a_norm_b, m_a_w_out, m_a_b_out, m_kv_w_down, m_kv_norm_g, m_kv_w_uk, m_kv_w_uv, m_b_w_in, m_b_q_norm_g, m_b_w_uq, m_b_w_out, v_ln_g, v_ln_b, v_a_w_in, v_a_b_in, v_a_conv_w, v_a_conv_b, v_a_norm_g, v_a_norm_b, v_a_w_out, v_a_b_out, v_kv_w_down, v_kv_norm_g, v_kv_w_uk, v_kv_w_uv, v_b_w_in, v_b_q_norm_g, v_b_w_uq, v_b_w_out):
    T, D = x.shape[1], x.shape[2]
    E = N_CHIPS * a_w_out.shape[1]
    KC = a_conv_w.shape[1]
    RKV = kv_norm_g.shape[0]
    H, DN = kv_w_uk.shape[1], kv_w_uk.shape[2]
    RQ = b_q_norm_g.shape[1]
    HV = N_CHIPS * b_w_out.shape[1]
    assert DN == LANES and kv_w_uv.shape[2] == LANES and HV == H * LANES
    assert kv_w_down.shape[1] == RKV + ROPE_DIM and b_w_uq.shape[3] == DN + ROPE_DIM
    assert ln_g.shape[0] == 2 and a_w_in.shape[0] == 1 and b_w_in.shape[0] == 1
    alpha = (2.0 * ln_g.shape[0]) ** 0.25
    scale = 1.0 / math.sqrt(DN + ROPE_DIM)
    WK = -(-(RKV + LANES) // 256) * 256
    assert WK % RQ == 0
    Z_OFF = WK + RQ
    tmw, tq = min(TM_WIDE, T), min(TQ, T)
    t512, t1024 = _fit(512, T), _fit(1024, T)
    xs = x[0]
    tgt = loss_target[0]
    px, py = lax.axis_index("x"), lax.axis_index("y")
    chip = 2 * px + py

    mats = [a_w_out[0], kv_w_down, kv_w_uk, kv_w_uv, b_w_in[0], b_w_uq[0], b_w_out[0]]
    vecs = [a_b_in[0], a_conv_w[0], a_conv_b[0], a_norm_g[0], a_norm_b[0], a_b_out[0]]
    first = jnp.concatenate(
        [a_w_in[0].astype(BF16).reshape(-1)]
        + [lax.bitcast_convert_type(w.reshape(-1), BF16).reshape(-1) for w in vecs])
    rest = [w.astype(BF16) for w in mats]
    gathered = _all_gather_chips(_pack_rows(first, BF16, 2 * BF16_ROWS * LANES)).reshape(N_CHIPS, -1)
    gathered, rest = lax.optimization_barrier((gathered, rest))
    rest_sems = _push_start("gather_rest_start", rest, by_target=False)
    g_win = gathered[:, :a_w_in[0].size].reshape((N_CHIPS,) + a_w_in[0].shape)
    off = a_w_in[0].size
    fvec = []
    for w in vecs:
        bits = gathered[:, off:off + 2 * w.size].reshape((N_CHIPS,) + w.shape + (2,))
        fvec.append(lax.bitcast_convert_type(bits, F32))
        off += 2 * w.size
    cols = lambda g: jnp.moveaxis(g, 0, -2).reshape(g.shape[1:-1] + (N_CHIPS * g.shape[-1],))
    w_in = cols(g_win)
    b_in = cols(fvec[0][:, None, :])
    conv_w = cols(fvec[1])
    conv_b, norm_g, norm_b, b_out = (cols(f[:, None, :]) for f in fvec[2:])
    row = lambda a: a.reshape(1, -1)
    g0, b0, g1, b1 = row(ln_g[0]), row(ln_b[0]), row(ln_g[1]), row(ln_b[1])
    kv_g, q_g = row(kv_norm_g), row(b_q_norm_g[0])
    plain = lambda acc, ins, i, j: [acc]

    b_in = b_in + rest_sems[4][0, 0]
    (proj,) = _row_mm("a_in", [((xs,), None)], w_in, nt=False, tm=t1024, tn=_fit(1536, 3 * E), tk=_fit(1024, D),
                      outs=[((T, 3 * E), F32, 'tile')], epi=lambda acc, ins, i, j: [acc + ins[0]],
                      epi_ins=[(b_in, 'col')])
    u1 = _conv_fwd(proj, conv_w, conv_b, E, tmw)
    u4 = _conv_post(u1, proj, norm_g, norm_b, E, tmw)

    rest, landed = _push_wait("gather_rest_wait", *rest_sems[:4], after=[u4], by_target=False)
    g_wout, g_wd, g_uk, g_uv, g_wbin, g_wuq, g_wbout = [
        lax.dynamic_update_slice(l, w[None], (chip,) + (0,) * w.ndim) for w, l in zip(rest, landed)]
    w_out = g_wout.reshape(E, D)
    wd = g_wd.reshape(D, RKV + ROPE_DIM)
    zpad = jnp.zeros((D, ROPE_HALF), BF16)
    wd_p = jnp.concatenate(
        [wd[:, :RKV], wd[:, RKV:RKV + ROPE_HALF], zpad, wd[:, RKV + ROPE_HALF:], zpad,
         jnp.zeros((D, WK - RKV - LANES), BF16)], axis=1)
    w_bin = cols(g_wbin)
    w_z = w_bin[:, RQ:]
    wb_small = jnp.concatenate([wd_p, w_bin[:, :RQ]], axis=1)
    wb_all = jnp.concatenate([wd_p, w_bin], axis=1)
    w_kv = jnp.concatenate([g_uk.reshape(RKV, HV), g_uv.reshape(RKV, HV)], axis=1)
    wuq = g_wuq.reshape(RQ, H, DN + ROPE_DIM)
    zq = jnp.zeros((RQ, H, ROPE_HALF), BF16)
    w_qr = jnp.concatenate([wuq[:, :, DN:DN + ROPE_HALF], zq, wuq[:, :, DN + ROPE_HALF:], zq], axis=2)
    w_q = jnp.concatenate([wuq[:, :, :DN].reshape(RQ, HV), w_qr.reshape(RQ, HV)], axis=1)
    w_bout = g_wbout.reshape(HV, D)

    freqs = ROPE_THETA ** (-jnp.arange(0, ROPE_DIM, 2, dtype=F32) / ROPE_DIM)
    ang = positions[0].astype(F32)[:, None] * freqs
    cs, sn = jnp.cos(ang), jnp.sin(ang)
    ones, zeros = jnp.ones_like(cs), jnp.zeros_like(cs)
    cos_t = jnp.concatenate([cs, ones, cs, ones], axis=1)
    sin_t = jnp.concatenate([-sn, zeros, sn, zeros], axis=1)

    def ln_epi(acc, ins, i, j):
        bias, res, g, b = ins
        xhat, rstd = _ln_stats(alpha * res + acc + bias)
        h = xhat * g + b
        return [h, h, xhat, rstd]

    h1, h1b, xhat1, rstd1 = _row_mm(
        "a_out", [((u4,), None)], w_out, nt=False, tm=t512, tn=D, tk=_fit(2048, E),
        outs=[((T, D), F32, 'tile'), ((T, D), BF16, 'tile'), ((T, D), F32, 'tile'), ((T, 1), F32, 'row')],
        epi=ln_epi, epi_ins=[(b_out, 'col'), (xs, 'tile'), (g0, 'col'), (b0, 'col')])

    tkb = _fit(512, _gcd(WK, RQ, HV))
    (pb,) = _row_mm("b_in", [((h1b,), None)], wb_small, nt=False, tm=t1024, tn=_fit(1024, Z_OFF),
                    tk=_fit(1024, D), outs=[((T, Z_OFF), F32, 'tile')], epi=plain)
    (zb,) = _row_mm("b_in_gate", [((h1b,), None)], w_z, nt=False, tm=t1024, tn=_fit(2048, HV),
                    tk=_fit(1024, D), outs=[((T, HV), F32, 'tile')], epi=plain)
    c_lat, kr, cqn = _norm_prep(pb, kv_g, q_g, cos_t, sin_t, RKV, RQ, WK, tmw)
    (kv,) = _row_mm("kv_up", [((c_lat,), None)], w_kv, nt=False, tm=t1024, tn=_fit(2048, HV),
                    tk=_fit(1024, RKV), outs=[((T, 2 * HV), BF16, 'tile')], epi=plain)
    tnq = _fit(2048, HV)
    half_q = HV // tnq

    def q_epi(acc, ins, i, j):
        return [jnp.where(j >= half_q, _rope(acc, ins[0], ins[1]), acc)]

    (q_all,) = _row_mm("q_up", [((cqn,), None)], w_q, nt=False, tm=t1024, tn=tnq, tk=_fit(1024, RQ),
                       outs=[((T, 2 * HV), BF16, 'tile')], epi=q_epi,
                       epi_ins=[(cos_t, 'row'), (sin_t, 'row')])
    o, lse = _attn_fwd(q_all, kv, kr, H, tq, scale)

    def loss_epi(acc, ins, i, j):
        res, g, b, target = ins
        xhat, rstd = _ln_stats(alpha * res + acc)
        diff = xhat * g + b - target
        dr, dg, db = _ln_bwd(diff / D, xhat, rstd, g)
        return [dr, 0.5 * jnp.sum(diff * diff, keepdims=True) / D, dg, db]

    dr1, loss_part, dg1, db1 = _row_mm(
        "b_out", [((o, zb), _gate)], w_bout, nt=False, tm=tmw, tn=D, tk=_fit(2048, HV),
        outs=[((T, D), F32, 'tile'), ((1, 1), F32, 'acc'), ((1, D), F32, 'acc'), ((1, D), F32, 'acc')],
        epi=loss_epi, epi_ins=[(h1, 'tile'), (g1, 'col'), (b1, 'col'), (tgt, 'tile')])

    def gate_bwd_epi(acc, ins, i, j):
        return [acc * _silu(ins[1]), acc * ins[0] * _silu_grad(ins[1])]

    do, dz = _row_mm(
        "b_out_bwd", [((dr1,), None)], w_bout, nt=True, tm=tmw, tn=_fit(2048, HV), tk=_fit(1024, D),
        outs=[((T, HV), BF16, 'tile'), ((T, HV), BF16, 'tile')], epi=gate_bwd_epi,
        epi_ins=[(o, 'tile'), (zb, 'tile')])
    gw_bout = _tn_mm("dw_b_out", (o, zb), _gate, [((dr1,), None)], tn=_fit(1024, D), tk=t512, out_dtype=BF16)
    dqn, dqr, dkn, dkr_h, dv = _attn_bwd(q_all, kv, kr, do, o, lse, H, tq, scale)
    dqr_pre, dkr_pre = _rope_bwd(dqr, dkr_h, cos_t, sin_t, H, tmw)

    def cq_bwd_epi(acc, ins, i, j):
        dx, dg = _rms_bwd(acc, ins[0], ins[1])
        return [dx, dg]

    dcq, dqg = _row_mm(
        "q_up_bwd", [((dqn,), None), ((dqr_pre,), None)], w_q, nt=True, tm=t1024, tn=RQ, tk=_fit(2048, HV),
        outs=[((T, RQ), BF16, 'tile'), ((1, RQ), F32, 'acc')], epi=cq_bwd_epi,
        epi_ins=[(pb, pl.BlockSpec((t1024, RQ), lambda i, j, k: (i, WK // RQ))), (q_g, 'col')])
    gw_q = _tn_mm("dw_q_up", (cqn,), None, [((dqn,), None), ((dqr_pre,), None)],
                  tn=_fit(2048, HV), tk=t1024, out_dtype=BF16)

    def ckv_bwd_epi(acc, ins, i, j):
        blk, dkr_t, g = ins
        dx, dg = _rms_bwd(acc, blk[:, :RKV], g)
        parts = [dx, dkr_t]
        if WK > RKV + LANES:
            parts.append(jnp.zeros((dx.shape[0], WK - RKV - LANES), F32))
        return [jnp.concatenate(parts, axis=1), dg]

    dckv, dkvg = _row_mm(
        "kv_up_bwd", [((dkn,), None), ((dv,), None)], w_kv, nt=True, tm=t1024, tn=RKV, tk=_fit(2048, HV),
        outs=[((T, WK), BF16, pl.BlockSpec((t1024, WK), lambda i, j, k: (i, 0))), ((1, RKV), F32, 'acc')],
        epi=ckv_bwd_epi,
        epi_ins=[(pb, pl.BlockSpec((t1024, WK), lambda i, j, k: (i, 0))), (dkr_pre, 'row'), (kv_g, 'col')])
    gw_kv = _tn_mm("dw_kv_up", (c_lat,), None, [((dkn,), None), ((dv,), None)],
                   tn=_fit(2048, HV), tk=t1024, out_dtype=BF16)

    def ln1_bwd_epi(acc, ins, i, j):
        dr_up, xhat, rstd, g = ins
        dr, dg, db = _ln_bwd(alpha * dr_up + acc, xhat, rstd, g)
        return [dr, dg, db]

    dp_segs = [((dckv,), None), ((dcq,), None), ((dz,), None)]
    gw_ball = _tn_mm("dw_b_in", (h1b,), None, dp_segs, tn=tkb, tk=t1024, out_dtype=BF16)

    shard_cols = lambda g: jnp.moveaxis(g.reshape(g.shape[0], N_CHIPS, -1), 1, 0)
    shard_rows = lambda g: g.reshape(N_CHIPS, g.shape[0] // N_CHIPS, g.shape[1])
    gq = gw_q.reshape(RQ, 2, H, LANES)
    g_uq = jnp.concatenate(
        [gq[:, 0], gq[:, 1, :, :ROPE_HALF], gq[:, 1, :, 2 * ROPE_HALF:3 * ROPE_HALF]], axis=2)
    g_wd_full = jnp.concatenate(
        [gw_ball[:, :RKV], gw_ball[:, RKV:RKV + ROPE_HALF],
         gw_ball[:, RKV + 2 * ROPE_HALF:RKV + 3 * ROPE_HALF]], axis=1)
    late_names = ["kv_w_down", "kv_w_uk", "kv_w_uv", "b_w_in", "b_w_uq", "b_w_out"]
    late_w = [kv_w_down, kv_w_uk, kv_w_uv, b_w_in, b_w_uq, b_w_out]
    chip_major = [g_wd_full, gw_kv[:, :HV], gw_kv[:, HV:], shard_cols(gw_ball[:, WK:]), g_uq, gw_bout]
    late_grads = [g.reshape((N_CHIPS,) + w.shape) for g, w in zip(chip_major, late_w)]
    late_sems = _push_start("scatter_late_start", late_grads, by_target=True)

    dr0, dg0, db0 = _row_mm(
        "b_in_bwd", dp_segs, wb_all, nt=True, tm=t512, tn=D, tk=tkb,
        outs=[((T, D), F32, 'tile'), ((1, D), F32, 'acc'), ((1, D), F32, 'acc')], epi=ln1_bwd_epi,
        epi_ins=[(dr1, 'tile'), (xhat1, 'tile'), (rstd1, 'row'), (g0 + late_sems[4][0, 0], 'col')])

    def conv_branch_bwd_epi(acc, ins, i, j):
        u1_t, z, g, b = ins
        xhat, rstd = _ln_stats(u1_t)
        u2 = xhat * g + b
        du3 = acc * _silu(z)
        dz_a = acc * _silu(u2) * _silu_grad(z)
        du1, dg, db = _ln_bwd(du3 * _silu_grad(u2), xhat, rstd, g)
        return [du1, dz_a, dg, db]

    du1, dz_a, dng, dnb = _row_mm(
        "a_out_bwd", [((dr0,), None)], w_out, nt=True, tm=tmw, tn=E, tk=_fit(1024, D),
        outs=[((T, E), F32, 'tile'), ((T, E), BF16, 'tile'), ((1, E), F32, 'acc'), ((1, E), F32, 'acc')],
        epi=conv_branch_bwd_epi,
        epi_ins=[(u1, 'tile'), (proj, pl.BlockSpec((tmw, E), lambda i, j, k: (i, 2))), (norm_g, 'col'),
                 (norm_b, 'col')])
    gw_out, dbo = _tn_mm("dw_a_out", (u4,), None, [((dr0,), None)], tn=_fit(1024, D), tk=t1024,
                         out_dtype=BF16, colsum=True)
    mid_sems = _push_start("scatter_mid_start", [gw_out.reshape((N_CHIPS,) + a_w_out.shape)], by_target=True)
    dval, dgate, dcw, dcb = _conv_bwd(du1, proj, conv_w + mid_sems[4][0, 0], E, tmw)
    dproj_segs = [((dval,), None), ((dgate,), None), ((dz_a,), None)]
    gw_in, dbi = _tn_mm("dw_a_in", (xs,), None, dproj_segs, tn=_fit(2048, E), tk=t512, out_dtype=BF16,
                        colsum=True)

    def own_block_in_place(sent, landed):
        own = lax.dynamic_index_in_dim(sent, chip, 0, keepdims=True)
        return lax.dynamic_update_slice(landed, own, (chip,) + (0,) * (sent.ndim - 1))

    def reduce_and_update(tag, names_, sent, landed, w_, m_, v_):
        sums = [_sum_chips("sum_" + n, own_block_in_place(s, l)) for n, s, l in zip(names_, sent, landed)]
        theirs = _swap_cores("swap_cores_" + tag, sums)
        return {n: _adamw("adamw_" + n, [mine, other], w, m, v)
                for n, mine, other, w, m, v in zip(names_, sums, theirs, w_, m_, v_)}

    late_sent, late_landed = _push_wait("scatter_late_wait", *late_sems[:4], after=[dbi], by_target=True)
    mid_sent, mid_landed = _push_wait("scatter_mid_wait", *mid_sems[:4], after=[dbi], by_target=True)
    early_sems = _push_start("scatter_early_start", [shard_cols(gw_in).reshape((N_CHIPS,) + a_w_in.shape)],
                             by_target=True)
    (grad_x,) = _row_mm(
        "a_in_bwd", dproj_segs, w_in, nt=True, tm=tmw, tn=D, tk=E, b_whole=True,
        outs=[((T, D), F32, 'tile')], epi=lambda acc, ins, i, j: [alpha * ins[0] + acc + ins[1]],
        epi_ins=[(dr0, 'tile'), (jnp.zeros((1, D), F32) + early_sems[4][0, 0], 'col')])
    big_out = reduce_and_update(
        "late", ["a_w_out"] + late_names, mid_sent + late_sent, mid_landed + late_landed,
        [a_w_out] + late_w, [m_a_w_out, m_kv_w_down, m_kv_w_uk, m_kv_w_uv, m_b_w_in, m_b_w_uq, m_b_w_out],
        [v_a_w_out, v_kv_w_down, v_kv_w_uk, v_kv_w_uv, v_b_w_in, v_b_w_uq, v_b_w_out])

    small_full = [jnp.concatenate([dg0, dg1]), jnp.concatenate([db0, db1]), dbi, dcw, dcb, dng, dnb, dbo,
                  dkvg, dqg]
    sflat = jnp.concatenate([g.reshape(-1) for g in small_full])
    summed = _all_reduce_small(_pack_rows(sflat, F32, 8 * LANES)).reshape(-1)
    soff = 0
    sgrads = []
    for g in small_full:
        sgrads.append(summed[soff:soff + g.size].reshape(g.shape))
        soff += g.size
    local_cols = lambda g, n: lax.dynamic_slice_in_dim(g, chip * n, n, axis=g.ndim - 1)
    snames = ["ln_g", "ln_b", "a_b_in", "a_conv_w", "a_conv_b", "a_norm_g", "a_norm_b", "a_b_out",
              "kv_norm_g", "b_q_norm_g"]
    small_w = [ln_g, ln_b, a_b_in, a_conv_w, a_conv_b, a_norm_g, a_norm_b, a_b_out, kv_norm_g, b_q_norm_g]
    small_m = [m_ln_g, m_ln_b, m_a_b_in, m_a_conv_w, m_a_conv_b, m_a_norm_g, m_a_norm_b, m_a_b_out,
               m_kv_norm_g, m_b_q_norm_g]
    small_v = [v_ln_g, v_ln_b, v_a_b_in, v_a_conv_w, v_a_conv_b, v_a_norm_g, v_a_norm_b, v_a_b_out,
               v_kv_norm_g, v_b_q_norm_g]
    sharded = {"a_b_in", "a_conv_w", "a_conv_b", "a_norm_g", "a_norm_b", "a_b_out"}
    local_g = [(local_cols(g, w.shape[-1]) if n in sharded else g).reshape(w.shape)
               for n, g, w in zip(snames, sgrads, small_w)]
    at_least_2d = lambda a: a.reshape((1,) + a.shape) if a.ndim == 1 else a
    sres = _adamw_vectors([tuple(at_least_2d(a) for a in item)
                           for item in zip(local_g, small_w, small_m, small_v)])
    small_out = {n: [g] + [r.reshape(w.shape) for r in res]
                 for n, g, w, res in zip(snames, local_g, small_w, sres)}

    early_sent, early_landed = _push_wait(
        "scatter_early_wait", *early_sems[:4], by_target=True,
        after=[grad_x, big_out["b_w_out"][1], small_out["b_q_norm_g"][1]])
    big_out.update(reduce_and_update("early", ["a_w_in"], early_sent, early_landed,
                                     [a_w_in], [m_a_w_in], [v_a_w_in]))

    loss = lax.psum(loss_part[0, 0], ("x", "y", "c"))
    order = ["ln_g", "ln_b", "a_w_in", "a_b_in", "a_conv_w", "a_conv_b", "a_norm_g", "a_norm_b", "a_w_out",
             "a_b_out", "kv_w_down", "kv_norm_g", "kv_w_uk", "kv_w_uv", "b_w_in", "b_q_norm_g", "b_w_uq",
             "b_w_out"]
    outs = {**big_out, **small_out}
    result = [loss, grad_x[None]]
    for part in range(4):
        result += [outs[n][part] for n in order]
    return tuple(result)
```

```python
import functools
import math

import jax
import jax.numpy as jnp
from jax import lax
from jax.experimental import pallas as pl
from jax.experimental.pallas import tpu as pltpu

F32, BF16 = jnp.float32, jnp.bfloat16
NN = (((1,), (0,)), ((), ()))
NT = (((1,), (1,)), ((), ()))
TN = (((0,), (0,)), ((), ()))
MESH = pl.DeviceIdType.MESH
ANY = pl.BlockSpec(memory_space=pl.ANY)

LANES = 128
BF16_ROWS = 16
VMEM_LIMIT = 56 * 1024 * 1024
N_CHIPS = 4
LN_EPS = 1e-5
RMS_EPS = 1e-6
MASK_VALUE = -1e30
LOG2_E = math.log2(math.e)
ROPE_THETA = 10000.0
ROPE_DIM = 64
ROPE_HALF = ROPE_DIM // 2
ADAM_LR, ADAM_B1, ADAM_B2, ADAM_EPS, ADAM_WD, ADAM_STEP = 0.001, 0.9, 0.999, 1e-08, 0.01, 10

MAX_TILE = 2048
TM_WIDE = 256
TQ = 512
CONV_HALO = 32
CONV_LC = 512
CONV_SUB = 256
SUBLANES = 8
CONV_RC = 32
ADAM_ROWS = 64


def _dot(a, b, dims):
    return lax.dot_general(a.astype(BF16), b.astype(BF16), dims, preferred_element_type=F32)


def _sig(x):
    return 1.0 / (1.0 + jnp.exp(-x))


def _params(n_axes):
    return pltpu.CompilerParams(dimension_semantics=("arbitrary",) * n_axes, vmem_limit_bytes=VMEM_LIMIT)


def _gcd(*v):
    return functools.reduce(math.gcd, v)


def _fit(want, dim):
    return math.gcd(min(want, MAX_TILE), dim)


def _row_mm(name, a_segs, b, *, nt, tm, tn, tk, outs, epi, epi_ins=(), b_whole=False):
    M = a_segs[0][0][0].shape[0]
    N = b.shape[0] if nt else b.shape[1]
    nkb = [arrs[0].shape[1] // tk for arrs, _ in a_segs]
    koff = [sum(nkb[:s]) for s in range(len(nkb))]
    ni, nj, nk = M // tm, N // tn, sum(nkb)
    assert M % tm == 0 and N % tn == 0 and all(arrs[0].shape[1] % tk == 0 for arrs, _ in a_segs), name
    assert (b.shape[1] if nt else b.shape[0]) == nk * tk, name

    def spec_of(shape, kind):
        if isinstance(kind, pl.BlockSpec):
            return kind
        if kind == 'tile':
            return pl.BlockSpec((tm, tn), lambda i, j, k: (i, j))
        if kind == 'row':
            return pl.BlockSpec((tm, shape[1]), lambda i, j, k: (i, 0))
        if kind == 'col':
            return pl.BlockSpec((1, tn), lambda i, j, k: (0, j))
        assert kind == 'acc' and nj == 1, name
        return pl.BlockSpec(shape, lambda i, j, k: (0,) * len(shape))

    in_specs, operands = [], []
    for s, (arrs, _) in enumerate(a_segs):
        for arr in arrs:
            in_specs.append(pl.BlockSpec(
                (tm, tk), lambda i, j, k, s=s: (i, jnp.clip(k - koff[s], 0, nkb[s] - 1))))
            operands.append(arr)
    if b_whole:
        assert nt and nj == 1 and all(n == 1 for n in nkb), name
        in_specs.append(pl.BlockSpec(b.shape, lambda i, j, k: (0, 0)))
    else:
        in_specs.append(pl.BlockSpec((tn, tk), lambda i, j, k: (j, k)) if nt
                        else pl.BlockSpec((tk, tn), lambda i, j, k: (k, j)))
    operands.append(b)
    for arr, kind in epi_ins:
        in_specs.append(spec_of(arr.shape, kind))
        operands.append(arr)
    out_specs = [spec_of(shape, kind) for shape, _, kind in outs]
    out_shape = [jax.ShapeDtypeStruct(shape, dtype) for shape, dtype, _ in outs]
    n_seg_refs = [len(arrs) for arrs, _ in a_segs]

    def body(*refs):
        pos = 0
        seg_refs = []
        for n in n_seg_refs:
            seg_refs.append(refs[pos:pos + n])
            pos += n
        b_ref = refs[pos]
        e_refs = refs[pos + 1:pos + 1 + len(epi_ins)]
        o_refs = refs[pos + 1 + len(epi_ins):pos + 1 + len(epi_ins) + len(outs)]
        i, j, k = pl.program_id(0), pl.program_id(1), pl.program_id(2)

        def product(fn, rs, s=0):
            a = rs[0][...] if fn is None else fn(*[r[...] for r in rs])
            bt = b_ref[:, koff[s] * tk:(koff[s] + 1) * tk] if b_whole else b_ref[...]
            return _dot(a, bt, NT if nt else NN)

        def finish(acc):
            res = epi(acc, [r[...] for r in e_refs], i, j)
            for o_ref, (_, _, kind), r in zip(o_refs, outs, res):
                if isinstance(kind, str) and kind == 'acc':
                    @pl.when(i == 0)
                    def _(o_ref=o_ref, r=r):
                        o_ref[...] = r

                    @pl.when(i > 0)
                    def _(o_ref=o_ref, r=r):
                        o_ref[...] += r
                else:
                    o_ref[...] = r.astype(o_ref.dtype)

        if nk == 1:
            finish(product(a_segs[0][1], seg_refs[0]))
            return
        acc_ref = refs[-1]

        @pl.when(k == 0)
        def _():
            acc_ref[...] = jnp.zeros_like(acc_ref)

        for s, ((_, fn), rs) in enumerate(zip(a_segs, seg_refs)):
            def accumulate(fn=fn, rs=rs, s=s):
                acc_ref[...] += product(fn, rs, s)
            if len(a_segs) == 1:
                accumulate()
            else:
                pl.when(jnp.logical_and(k >= koff[s], k < koff[s] + nkb[s]))(accumulate)

        @pl.when(k == nk - 1)
        def _():
            finish(acc_ref[...])

    return pl.pallas_call(
        body, name=name, grid=(ni, nj, nk), in_specs=in_specs, out_specs=out_specs, out_shape=out_shape,
        scratch_shapes=[] if nk == 1 else [pltpu.VMEM((tm, tn), F32)], compiler_params=_params(3),
    )(*operands)


def _tn_mm(name, a_arrs, a_fn, b_segs, *, tn, tk, out_dtype, shard_major=False, colsum=False):
    T, M = a_arrs[0].shape
    nbj = [arrs[0].shape[1] // tn for arrs, _ in b_segs]
    joff = [sum(nbj[:s]) for s in range(len(nbj))]
    nj, nk = sum(nbj), T // tk
    N = nj * tn
    assert T % tk == 0 and all(arrs[0].shape[1] % tn == 0 for arrs, _ in b_segs), name

    in_specs = [pl.BlockSpec((tk, M), lambda j, k: (k, 0)) for _ in a_arrs]
    operands = list(a_arrs)
    for s, (arrs, _) in enumerate(b_segs):
        for arr in arrs:
            in_specs.append(pl.BlockSpec(
                (tk, tn), lambda j, k, s=s: (k, jnp.clip(j - joff[s], 0, nbj[s] - 1))))
            operands.append(arr)
    if shard_major:
        per = (N // N_CHIPS) // tn
        assert per * tn * N_CHIPS == N, name
        out_shape = [jax.ShapeDtypeStruct((N_CHIPS, M, N // N_CHIPS), out_dtype)]
        out_specs = [pl.BlockSpec((1, M, tn), lambda j, k: (j // per, 0, j % per))]
    else:
        out_shape = [jax.ShapeDtypeStruct((M, N), out_dtype)]
        out_specs = [pl.BlockSpec((M, tn), lambda j, k: (0, j))]
    if colsum:
        out_shape.append(jax.ShapeDtypeStruct((1, N), F32))
        out_specs.append(pl.BlockSpec((1, tn), lambda j, k: (0, j)))
    n_a = len(a_arrs)
    n_seg_refs = [len(arrs) for arrs, _ in b_segs]

    def body(*refs):
        a_refs = refs[:n_a]
        pos = n_a
        seg_refs = []
        for n in n_seg_refs:
            seg_refs.append(refs[pos:pos + n])
            pos += n
        o_ref = refs[pos]
        cs_ref = refs[pos + 1] if colsum else None
        acc_ref = refs[-1]
        j, k = pl.program_id(0), pl.program_id(1)

        @pl.when(k == 0)
        def _():
            acc_ref[...] = jnp.zeros_like(acc_ref)
            if colsum:
                cs_ref[...] = jnp.zeros_like(cs_ref)

        for s, ((_, fn), rs) in enumerate(zip(b_segs, seg_refs)):
            def accumulate(fn=fn, rs=rs):
                a = a_refs[0][...] if a_fn is None else a_fn(*[r[...] for r in a_refs])
                bt = rs[0][...] if fn is None else fn(*[r[...] for r in rs])
                acc_ref[...] += _dot(a, bt, TN)
                if colsum:
                    cs_ref[...] += jnp.sum(bt.astype(F32), axis=0, keepdims=True)
            if len(b_segs) == 1:
                accumulate()
            else:
                pl.when(jnp.logical_and(j >= joff[s], j < joff[s] + nbj[s]))(accumulate)

        @pl.when(k == nk - 1)
        def _():
            if shard_major:
                o_ref[0] = acc_ref[...].astype(o_ref.dtype)
            else:
                o_ref[...] = acc_ref[...].astype(o_ref.dtype)

    res = pl.pallas_call(
        body, name=name, grid=(nj, nk), in_specs=in_specs, out_specs=out_specs, out_shape=out_shape,
        scratch_shapes=[pltpu.VMEM((M, tn), F32)], compiler_params=_params(2),
    )(*operands)
    return res if colsum else res[0]


def _silu(z):
    return z * _sig(z)


def _silu_grad(z):
    s = _sig(z)
    return s * (1.0 + z * (1.0 - s))


def _gate(o, z):
    return o * _silu(z)


def _ln_stats(r):
    mu = jnp.mean(r, axis=1, keepdims=True)
    xc = r - mu
    var = jnp.mean(xc * xc, axis=1, keepdims=True)
    rstd = lax.rsqrt(var + LN_EPS)
    return xc * rstd, rstd


def _ln_bwd(dy, xhat, rstd, g):
    dxh = dy * g
    m1 = jnp.mean(dxh, axis=1, keepdims=True)
    m2 = jnp.mean(dxh * xhat, axis=1, keepdims=True)
    return (rstd * (dxh - m1 - xhat * m2), jnp.sum(dy * xhat, axis=0, keepdims=True),
            jnp.sum(dy, axis=0, keepdims=True))


def _rms_fwd(x, g):
    rstd = lax.rsqrt(jnp.mean(x * x, axis=1, keepdims=True) + RMS_EPS)
    return x * rstd * g


def _rms_bwd(dy, x, g):
    rstd = lax.rsqrt(jnp.mean(x * x, axis=1, keepdims=True) + RMS_EPS)
    xn = x * rstd
    dxn = dy * g
    return rstd * (dxn - xn * jnp.mean(dxn * xn, axis=1, keepdims=True)), jnp.sum(dy * xn, axis=0, keepdims=True)


def _rope(x, cos, sin, transpose=False):
    parts = []
    for g in range(x.shape[1] // LANES):
        xg = x[:, g * LANES:(g + 1) * LANES]
        if transpose:
            parts.append(xg * cos + pltpu.roll(xg * sin, LANES // 2, 1))
        else:
            parts.append(xg * cos + pltpu.roll(xg, LANES // 2, 1) * sin)
    return parts[0] if len(parts) == 1 else jnp.concatenate(parts, axis=1)


def _shifted_rows(window, rc):
    n = window.shape[0]
    for b in range(SUBLANES):
        rolled = window if b == 0 else pltpu.roll(window, n - b, 0)
        for a8 in range(0, n - rc - b + 1, SUBLANES):
            yield a8 + b, rolled[a8:a8 + rc]


def _conv_fwd(proj, conv_w, conv_b, E, tm):
    T = proj.shape[0]
    kc = conv_w.shape[0]
    lc, hb, rc = min(CONV_LC, E), CONV_HALO, min(CONV_RC, tm)
    nl, ni, ratio = E // lc, T // tm, tm // hb
    gate_off = E // lc

    sub = min(CONV_SUB, lc)
    base = hb - (kc - 1)

    def body(val_ref, gate_ref, valh_ref, gateh_ref, w_ref, cb_ref, u1_ref, ubuf):
        i = pl.program_id(1)
        ubuf[hb:, :] = val_ref[...] * _sig(gate_ref[...])
        halo = valh_ref[...] * _sig(gateh_ref[...])
        ubuf[0:hb, :] = jnp.where(i > 0, halo, 0.0)
        for l0 in range(0, lc, sub):
            ls = slice(l0, l0 + sub)
            for r0 in range(0, tm, rc):
                acc = jnp.zeros((rc, sub), F32) + cb_ref[:, ls]
                for off, rows in _shifted_rows(ubuf[r0:r0 + hb + rc, ls], rc):
                    if 0 <= off - base < kc:
                        acc += w_ref[off - base:off - base + 1, ls] * rows
                u1_ref[r0:r0 + rc, ls] = acc

    return pl.pallas_call(
        body, name="conv_fwd", grid=(nl, ni),
        in_specs=[
            pl.BlockSpec((tm, lc), lambda l, i: (i, l)),
            pl.BlockSpec((tm, lc), lambda l, i: (i, gate_off + l)),
            pl.BlockSpec((hb, lc), lambda l, i: (jnp.maximum(i * ratio - 1, 0), l)),
            pl.BlockSpec((hb, lc), lambda l, i: (jnp.maximum(i * ratio - 1, 0), gate_off + l)),
            pl.BlockSpec((kc, lc), lambda l, i: (0, l)),
            pl.BlockSpec((1, lc), lambda l, i: (0, l)),
        ],
        out_specs=pl.BlockSpec((tm, lc), lambda l, i: (i, l)),
        out_shape=jax.ShapeDtypeStruct((T, E), F32),
        scratch_shapes=[pltpu.VMEM((hb + tm, lc), F32)], compiler_params=_params(2),
    )(proj, proj, proj, proj, conv_w, conv_b)


def _conv_post(u1, proj, norm_g, norm_b, E, tm):
    T = u1.shape[0]

    def body(u1_ref, z_ref, g_ref, b_ref, u4_ref):
        xhat, _ = _ln_stats(u1_ref[...])
        u4_ref[...] = (_silu(xhat * g_ref[...] + b_ref[...]) * _silu(z_ref[...])).astype(BF16)

    return pl.pallas_call(
        body, name="conv_post", grid=(T // tm,),
        in_specs=[pl.BlockSpec((tm, E), lambda i: (i, 0)), pl.BlockSpec((tm, E), lambda i: (i, 2)),
                  pl.BlockSpec((1, E), lambda i: (0, 0)), pl.BlockSpec((1, E), lambda i: (0, 0))],
        out_specs=pl.BlockSpec((tm, E), lambda i: (i, 0)),
        out_shape=jax.ShapeDtypeStruct((T, E), BF16), compiler_params=_params(1),
    )(u1, proj, norm_g, norm_b)


def _conv_bwd(du1, proj, conv_w, E, tm):
    T = du1.shape[0]
    kc = conv_w.shape[0]
    lc, hb, rc = min(CONV_LC, E), CONV_HALO, min(CONV_RC, tm)
    nl, ni, ratio = E // lc, T // tm, tm // hb
    gate_off = E // lc
    last_halo = T // hb - 1

    sub = min(CONV_SUB, lc)
    base = hb - (kc - 1)

    def body(du_ref, dun_ref, val_ref, gate_ref, valh_ref, gateh_ref, w_ref,
             dval_ref, dgate_ref, dw_ref, db_ref, ubuf, dbuf, sbuf, dw_sc):
        i = pl.program_id(1)
        sbuf[...] = _sig(gate_ref[...])
        ubuf[hb:, :] = val_ref[...] * sbuf[...]
        halo = valh_ref[...] * _sig(gateh_ref[...])
        ubuf[0:hb, :] = jnp.where(i > 0, halo, 0.0)
        dbuf[0:tm, :] = du_ref[...]
        dbuf[tm:, :] = jnp.where(i < ni - 1, dun_ref[...], 0.0)

        @pl.when(i == 0)
        def _():
            dw_sc[...] = jnp.zeros_like(dw_sc)
            db_ref[...] = jnp.zeros_like(db_ref)

        db_ref[...] += jnp.sum(du_ref[...], axis=0, keepdims=True)
        for l0 in range(0, lc, sub):
            ls = slice(l0, l0 + sub)
            for r0 in range(0, tm, rc):
                dwin = dbuf[r0:r0 + rc + hb, ls]
                dchunk = dwin[0:rc]
                for off, rows in _shifted_rows(ubuf[r0:r0 + hb + rc, ls], rc):
                    k = off - base
                    if 0 <= k < kc:
                        prod = rows * dchunk
                        part = prod[0:SUBLANES]
                        for s8 in range(SUBLANES, rc, SUBLANES):
                            part = part + prod[s8:s8 + SUBLANES]
                        dw_sc[k, :, ls] += part
                acc = jnp.zeros((rc, sub), F32)
                for off, rows in _shifted_rows(dwin, rc):
                    k = (kc - 1) - off
                    if 0 <= k < kc:
                        acc += w_ref[k:k + 1, ls] * rows
                v, s = val_ref[r0:r0 + rc, ls], sbuf[r0:r0 + rc, ls]
                dval_ref[r0:r0 + rc, ls] = (acc * s).astype(BF16)
                dgate_ref[r0:r0 + rc, ls] = (acc * v * s * (1.0 - s)).astype(BF16)

        @pl.when(i == ni - 1)
        def _():
            for k in range(kc):
                dw_ref[k:k + 1, :] = jnp.sum(dw_sc[k], axis=0, keepdims=True)

    return pl.pallas_call(
        body, name="conv_bwd", grid=(nl, ni),
        in_specs=[
            pl.BlockSpec((tm, lc), lambda l, i: (i, l)),
            pl.BlockSpec((hb, lc), lambda l, i: (jnp.minimum((i + 1) * ratio, last_halo), l)),
            pl.BlockSpec((tm, lc), lambda l, i: (i, l)),
            pl.BlockSpec((tm, lc), lambda l, i: (i, gate_off + l)),
            pl.BlockSpec((hb, lc), lambda l, i: (jnp.maximum(i * ratio - 1, 0), l)),
            pl.BlockSpec((hb, lc), lambda l, i: (jnp.maximum(i * ratio - 1, 0), gate_off + l)),
            pl.BlockSpec((kc, lc), lambda l, i: (0, l)),
        ],
        out_specs=[pl.BlockSpec((tm, lc), lambda l, i: (i, l)), pl.BlockSpec((tm, lc), lambda l, i: (i, l)),
                   pl.BlockSpec((kc, lc), lambda l, i: (0, l)), pl.BlockSpec((1, lc), lambda l, i: (0, l))],
        out_shape=[jax.ShapeDtypeStruct((T, E), BF16), jax.ShapeDtypeStruct((T, E), BF16),
                   jax.ShapeDtypeStruct((kc, E), F32), jax.ShapeDtypeStruct((1, E), F32)],
        scratch_shapes=[pltpu.VMEM((hb + tm, lc), F32), pltpu.VMEM((tm + hb, lc), F32),
                        pltpu.VMEM((tm, lc), F32), pltpu.VMEM((kc, SUBLANES, lc), F32)],
        compiler_params=_params(2),
    )(du1, du1, proj, proj, proj, proj, conv_w)


def _norm_prep(pb, kv_g, q_g, cos, sin, rkv, rq, wk, tm):
    T = pb.shape[0]

    def body(ckv_ref, cq_ref, kg_ref, qg_ref, cos_ref, sin_ref, c_ref, kr_ref, cqn_ref):
        blk = ckv_ref[...]
        c_ref[...] = _rms_fwd(blk[:, :rkv], kg_ref[...]).astype(BF16)
        kr_ref[...] = _rope(blk[:, rkv:rkv + LANES], cos_ref[...], sin_ref[...]).astype(BF16)
        cqn_ref[...] = _rms_fwd(cq_ref[...], qg_ref[...]).astype(BF16)

    return pl.pallas_call(
        body, name="norm_prep", grid=(T // tm,),
        in_specs=[pl.BlockSpec((tm, wk), lambda i: (i, 0)), pl.BlockSpec((tm, rq), lambda i: (i, wk // rq)),
                  pl.BlockSpec((1, rkv), lambda i: (0, 0)), pl.BlockSpec((1, rq), lambda i: (0, 0)),
                  pl.BlockSpec((tm, LANES), lambda i: (i, 0)), pl.BlockSpec((tm, LANES), lambda i: (i, 0))],
        out_specs=[pl.BlockSpec((tm, rkv), lambda i: (i, 0)), pl.BlockSpec((tm, LANES), lambda i: (i, 0)),
                   pl.BlockSpec((tm, rq), lambda i: (i, 0))],
        out_shape=[jax.ShapeDtypeStruct((T, rkv), BF16), jax.ShapeDtypeStruct((T, LANES), BF16),
                   jax.ShapeDtypeStruct((T, rq), BF16)],
        compiler_params=_params(1),
    )(pb, pb, kv_g, q_g, cos, sin)


def _attn_fwd(q_all, kv, kr, H, tq, scale):
    T = q_all.shape[0]
    nq = T // tq
    pair = 2
    W = pair * LANES
    assert H % pair == 0
    hp_n = H // pair

    def body(qn_ref, qr_ref, kn_ref, kr_ref, v_ref, o_ref, lse_ref, *scratch):
        qi = pl.program_id(1)
        chains = [scratch[4 * a:4 * a + 4] for a in range(pair)]
        lanes = [slice(a * LANES, (a + 1) * LANES) for a in range(pair)]
        groups = [slice(c * LANES, (c + 1) * LANES) for c in range(tq // LANES)]

        def fold(x, op):
            r = x[:, groups[0]]
            for gsl in groups[1:]:
                r = op(r, x[:, gsl])
            return r

        for _, m_sc, l_sc, acc_sc in chains:
            m_sc[...] = jnp.full_like(m_sc, MASK_VALUE)
            l_sc[...] = jnp.zeros_like(l_sc)
            acc_sc[...] = jnp.zeros_like(acc_sc)

        def scores(j, masked):
            rows = pl.ds(pl.multiple_of(j * tq, tq), tq)
            krope = kr_ref[rows, :]
            for a, (s_sc, m_sc, _, _) in enumerate(chains):
                q = jnp.concatenate([qn_ref[:, lanes[a]], qr_ref[:, lanes[a]]], axis=1)
                k = jnp.concatenate([kn_ref[rows, lanes[a]], krope], axis=1)
                s = _dot(q, k, NT) * (scale * LOG2_E)
                if masked:
                    row = lax.broadcasted_iota(jnp.int32, s.shape, 0)
                    col = lax.broadcasted_iota(jnp.int32, s.shape, 1)
                    s = jnp.where(col <= row, s, MASK_VALUE)
                s_sc[j] = s
                m_sc[...] = jnp.maximum(m_sc[...], fold(s, jnp.maximum))

        def two_per_trip(fn, count):
            def two(p, carry):
                fn(2 * p)
                fn(2 * p + 1)
                return carry

            lax.fori_loop(0, count // 2, two, 0)

            @pl.when(count % 2 == 1)
            def _():
                fn(count - 1)

        two_per_trip(functools.partial(scores, masked=False), qi)
        scores(qi, True)
        for _, m_sc, _, _ in chains:
            m_sc[...] = jnp.broadcast_to(jnp.max(m_sc[...], axis=1, keepdims=True), m_sc.shape)

        def weigh(j):
            rows = pl.ds(pl.multiple_of(j * tq, tq), tq)
            for a, (s_sc, m_sc, l_sc, acc_sc) in enumerate(chains):
                s, m = s_sc[j], m_sc[...]
                p = jnp.concatenate([jnp.exp2(s[:, gsl] - m) for gsl in groups], axis=1)
                l_sc[...] += fold(p, jnp.add)
                acc_sc[...] += _dot(p, v_ref[rows, lanes[a]], NN)

        two_per_trip(weigh, qi + 1)
        for a, (_, m_sc, l_sc, acc_sc) in enumerate(chains):
            l = jnp.sum(l_sc[...], axis=1, keepdims=True)
            o_ref[:, lanes[a]] = acc_sc[...] / l
            lse_ref[a] = m_sc[:, 0:1] * (1.0 / LOG2_E) + jnp.log(l)

    chain_scratch = [pltpu.VMEM((nq, tq, tq), F32), pltpu.VMEM((tq, LANES), F32), pltpu.VMEM((tq, LANES), F32),
                     pltpu.VMEM((tq, LANES), F32)]
    return pl.pallas_call(
        body, name="attn_fwd", grid=(hp_n, nq),
        in_specs=[pl.BlockSpec((tq, W), lambda hp, qi: (qi, hp)),
                  pl.BlockSpec((tq, W), lambda hp, qi: (qi, hp_n + hp)),
                  pl.BlockSpec((T, W), lambda hp, qi: (0, hp)),
                  pl.BlockSpec((T, LANES), lambda hp, qi: (0, 0)),
                  pl.BlockSpec((T, W), lambda hp, qi: (0, hp_n + hp))],
        out_specs=[pl.BlockSpec((tq, W), lambda hp, qi: (qi, hp)),
                   pl.BlockSpec((pair, tq, 1), lambda hp, qi: (hp, qi, 0))],
        out_shape=[jax.ShapeDtypeStruct((T, H * LANES), F32), jax.ShapeDtypeStruct((H, T, 1), F32)],
        scratch_shapes=chain_scratch * pair, compiler_params=_params(2),
    )(q_all, q_all, kv, kr, kv)


def _attn_bwd(q_all, kv, kr, do, o, lse, cos, sin, H, tq, scale):
    T = q_all.shape[0]
    nq = T // tq
    HV = H * LANES
    pair = 2
    tk2 = pair * tq
    ng = T // tk2
    assert ng * tk2 == T and pair == 2

    def body(qn_ref, qr_ref, kn_ref, kr_ref, v_ref, do_ref, o_ref, lse_ref, cos_ref, sin_ref,
             dqn_ref, dqr_ref, dkn_ref, dkr_ref, dv_ref, dq_sc, dk_sc, dv_sc):
        g = pl.program_id(1)

        @pl.when(g == 0)
        def _():
            dq_sc[...] = jnp.zeros_like(dq_sc)

        key_rows = [slice(c * tq, (c + 1) * tq) for c in range(pair)]

        def block(qi, modes):
            rows = pl.ds(pl.multiple_of(qi * tq, tq), tq)
            q = jnp.concatenate([qn_ref[rows, :], qr_ref[rows, :]], axis=1)
            dov = do_ref[rows, :]
            delta = jnp.sum(dov.astype(F32) * o_ref[rows, :], axis=1, keepdims=True)
            lse_q = lse_ref[0, rows, :]
            dq, dkv = None, []
            for kr_, masked in zip(key_rows, modes):
                if masked is None:
                    dkv.append(None)
                    continue
                k = jnp.concatenate([kn_ref[kr_, :], kr_ref[kr_, :]], axis=1)
                s = _dot(q, k, NT) * scale
                if masked:
                    row = lax.broadcasted_iota(jnp.int32, s.shape, 0)
                    col = lax.broadcasted_iota(jnp.int32, s.shape, 1)
                    s = jnp.where(col <= row, s, MASK_VALUE)
                p = jnp.exp(s - lse_q)
                dv = _dot(p, dov, TN)
                dp = _dot(dov, v_ref[kr_, :], NT)
                ds = (p * (dp - delta) * scale).astype(BF16)
                dkv.append((_dot(ds, q, TN), dv))
                part = _dot(ds, k, NN)
                dq = part if dq is None else dq + part
            return rows, dq, dkv

        rows_a, dq_a, (kv_a0, _) = block(pair * g, (True, None))
        rows_b, dq_b, (kv_b0, kv_b1) = block(pair * g + 1, (False, True))
        dk_sc[key_rows[0], :] = kv_a0[0] + kv_b0[0]
        dv_sc[key_rows[0], :] = kv_a0[1] + kv_b0[1]
        dk_sc[key_rows[1], :] = kv_b1[0]
        dv_sc[key_rows[1], :] = kv_b1[1]
        dq_sc[rows_a, :] += dq_a
        dq_sc[rows_b, :] += dq_b

        def below(trip, carry):
            for qi in (pair * g + pair + 2 * trip, pair * g + pair + 2 * trip + 1):
                rows, dq, dkv = block(qi, (False, False))
                for kr_, (dk, dv) in zip(key_rows, dkv):
                    dk_sc[kr_, :] += dk
                    dv_sc[kr_, :] += dv
                dq_sc[rows, :] += dq
            return carry

        lax.fori_loop(0, (nq - pair * g - pair) // 2, below, 0)
        dkn_ref[...] = dk_sc[:, :LANES].astype(BF16)
        dkr_ref[...] = dk_sc[:, LANES:]
        dv_ref[...] = dv_sc[...].astype(BF16)

        @pl.when(g == ng - 1)
        def _():
            dqn_ref[...] = dq_sc[:, :LANES].astype(BF16)
            dqr_ref[...] = _rope(dq_sc[:, LANES:], cos_ref[...], sin_ref[...], transpose=True).astype(BF16)

    whole = lambda col: pl.BlockSpec((T, LANES), col)
    tile = lambda col: pl.BlockSpec((tk2, LANES), col)
    return pl.pallas_call(
        body, name="attn_bwd", grid=(H, ng),
        in_specs=[whole(lambda h, g: (0, h)), whole(lambda h, g: (0, H + h)),
                  tile(lambda h, g: (g, h)), tile(lambda h, g: (g, 0)), tile(lambda h, g: (g, H + h)),
                  whole(lambda h, g: (0, h)), whole(lambda h, g: (0, h)),
                  pl.BlockSpec((1, T, 1), lambda h, g: (h, 0, 0)),
                  whole(lambda h, g: (0, 0)), whole(lambda h, g: (0, 0))],
        out_specs=[whole(lambda h, g: (0, h)), whole(lambda h, g: (0, h)),
                   tile(lambda h, g: (g, h)), tile(lambda h, g: (g, h)), tile(lambda h, g: (g, h))],
        out_shape=[jax.ShapeDtypeStruct((T, HV), BF16), jax.ShapeDtypeStruct((T, HV), BF16),
                   jax.ShapeDtypeStruct((T, HV), BF16), jax.ShapeDtypeStruct((T, HV), F32),
                   jax.ShapeDtypeStruct((T, HV), BF16)],
        scratch_shapes=[pltpu.VMEM((T, 2 * LANES), F32), pltpu.VMEM((tk2, 2 * LANES), F32),
                        pltpu.VMEM((tk2, LANES), F32)],
        compiler_params=_params(2),
    )(q_all, q_all, kv, kr, kv, do, o, lse, cos, sin)


def _key_rope_bwd(dkr_heads, cos, sin, H, tm):
    T, HV = dkr_heads.shape

    def body(dkr_ref, cos_ref, sin_ref, dk_ref):
        dk = dkr_ref[...]
        tot = dk[:, 0:LANES]
        for h in range(1, H):
            tot = tot + dk[:, h * LANES:(h + 1) * LANES]
        dk_ref[...] = _rope(tot, cos_ref[...], sin_ref[...], transpose=True)

    return pl.pallas_call(
        body, name="key_rope_bwd", grid=(T // tm,),
        in_specs=[pl.BlockSpec((tm, HV), lambda i: (i, 0)),
                  pl.BlockSpec((tm, LANES), lambda i: (i, 0)), pl.BlockSpec((tm, LANES), lambda i: (i, 0))],
        out_specs=pl.BlockSpec((tm, LANES), lambda i: (i, 0)),
        out_shape=jax.ShapeDtypeStruct((T, LANES), F32), compiler_params=_params(1),
    )(dkr_heads, cos, sin)


def _place():
    x, y, c = lax.axis_index("x"), lax.axis_index("y"), lax.axis_index("c")
    chips = [(1 - x, y), (x, 1 - y), (1 - x, 1 - y)]
    return x, y, c, chips


def _all_gather_chips(pack):
    rows = pack.shape[0]
    half = rows // 2
    assert half * 2 == rows and half % BF16_ROWS == 0

    def body(w_ref, out_ref, send_sems, recv_sems, local_sem):
        x, y, c, chips = _place()
        sibling = (x, y, 1 - c)

        def region(px, py, pc):
            return out_ref.at[2 * px + py, pl.ds(pc * half, half), :]

        def copy(k, block, to, src=None):
            return pltpu.make_async_remote_copy(
                src_ref=region(*block) if src is None else src, dst_ref=region(*block),
                send_sem=send_sems.at[k], recv_sem=recv_sems.at[k], device_id=to, device_id_type=MESH)

        mine = pltpu.make_async_copy(w_ref, out_ref.at[2 * x + y], local_sem)
        mine.start()
        my_half = w_ref.at[pl.ds(c * half, half), :]
        first = [copy(j, (x, y, c), (*chip, c), src=my_half) for j, chip in enumerate(chips)]
        for cp in first:
            cp.start()
        passed = [copy(3 + j, (*chip, c), sibling) for j, chip in enumerate(chips)]
        for j, chip in enumerate(chips):
            copy(j, (*chip, c), (x, y, c)).wait_recv()
            passed[j].start()
        for j, chip in enumerate(chips):
            copy(3 + j, (*chip, 1 - c), (x, y, c)).wait_recv()
        for cp in first + passed:
            cp.wait_send()
        mine.wait()

    return pl.pallas_call(
        body, name="gather_weights", in_specs=[ANY], out_specs=ANY,
        out_shape=jax.ShapeDtypeStruct((N_CHIPS,) + pack.shape, pack.dtype),
        scratch_shapes=[pltpu.SemaphoreType.DMA((6,)), pltpu.SemaphoreType.DMA((6,)), pltpu.SemaphoreType.DMA],
    )(pack)


def _swap_cores(name, parts):
    n = len(parts)

    def body(*refs):
        p_refs, r_refs = refs[:n], refs[n:2 * n]
        send_sems, recv_sems = refs[2 * n:]
        x, y, c, _ = _place()
        copies = [pltpu.make_async_remote_copy(
            src_ref=p_refs[w], dst_ref=r_refs[w], send_sem=send_sems.at[w], recv_sem=recv_sems.at[w],
            device_id=(x, y, 1 - c), device_id_type=MESH) for w in range(n)]
        for cp in copies:
            cp.start()
        for cp in copies:
            cp.wait()

    return pl.pallas_call(
        body, name=name, in_specs=[ANY] * n, out_specs=[ANY] * n,
        out_shape=[jax.ShapeDtypeStruct(p.shape, p.dtype) for p in parts],
        scratch_shapes=[pltpu.SemaphoreType.DMA((n,)), pltpu.SemaphoreType.DMA((n,))],
    )(*parts)


HBM = pl.BlockSpec(memory_space=pltpu.HBM)
SEM = pl.BlockSpec(memory_space=pltpu.SEMAPHORE)
EFFECT = pltpu.SideEffectType.DATAFLOW_SIDE_EFFECTING


def _push_copies(a_refs, l_refs, send_sems, recv_sems, by_target):
    x, y, c, chips = _place()
    me = 2 * x + y
    out = []
    for w, (a_ref, l_ref) in enumerate(zip(a_refs, l_refs)):
        for j, (px, py) in enumerate(chips):
            peer = 2 * px + py
            out.append((
                pltpu.make_async_remote_copy(
                    src_ref=a_ref.at[peer] if by_target else a_ref, dst_ref=l_ref.at[me],
                    send_sem=send_sems.at[3 * w + j], recv_sem=recv_sems.at[3 * w + j],
                    device_id=(px, py, c), device_id_type=MESH),
                pltpu.make_async_remote_copy(
                    src_ref=a_ref.at[me] if by_target else a_ref, dst_ref=l_ref.at[peer],
                    send_sem=send_sems.at[3 * w + j], recv_sem=recv_sems.at[3 * w + j],
                    device_id=(px, py, c), device_id_type=MESH)))
    return out


def _push_start(name, arrs, by_target):
    n = len(arrs)
    lands = [lax.empty((N_CHIPS,) + (a.shape[1:] if by_target else a.shape), a.dtype) for a in arrs]

    def body(*refs):
        a_refs, l_refs = refs[:n], refs[n:2 * n]
        send_sems, recv_sems = refs[2 * n], refs[2 * n + 1]
        token = refs[-1]
        for send, _ in _push_copies(a_refs, l_refs, send_sems, recv_sems, by_target):
            send.start()
        token[...] = jnp.zeros_like(token)

    res = pl.pallas_call(
        body, name=name,
        out_shape=(pltpu.SemaphoreType.DMA((3 * n,)), pltpu.SemaphoreType.DMA((3 * n,)),
                   *[pltpu.HBM(a.shape, a.dtype) for a in arrs], *[pltpu.HBM(l.shape, l.dtype) for l in lands],
                   jax.ShapeDtypeStruct((8, LANES), F32)),
        in_specs=[HBM] * (2 * n), out_specs=(SEM, SEM, *[HBM] * (2 * n), pl.BlockSpec(memory_space=pltpu.VMEM)),
        input_output_aliases={i: 2 + i for i in range(2 * n)},
        compiler_params=pltpu.CompilerParams(has_side_effects=EFFECT),
    )(*[pltpu.with_memory_space_constraint(a, pltpu.HBM) for a in list(arrs) + lands])
    return res[0], res[1], list(res[2:2 + n]), list(res[2 + n:2 + 2 * n]), res[-1]


def _push_wait(name, send_sems, recv_sems, arrs, lands, after, by_target):
    n = len(arrs)

    def body(*refs):
        a_refs, l_refs = refs[:n], refs[n:2 * n]
        s_sems, r_sems = refs[2 * n], refs[2 * n + 1]
        for send, recv in _push_copies(a_refs, l_refs, s_sems, r_sems, by_target):
            send.wait_send()
            recv.wait_recv()

    res = pl.pallas_call(
        body, name=name,
        out_shape=[pltpu.HBM(a.shape, a.dtype) for a in list(arrs) + list(lands)],
        in_specs=[HBM] * (2 * n) + [SEM, SEM] + [ANY] * len(after), out_specs=[HBM] * (2 * n),
        input_output_aliases={i: i for i in range(2 * n)},
        compiler_params=pltpu.CompilerParams(has_side_effects=EFFECT),
    )(*arrs, *lands, send_sems, recv_sems, *after)
    return list(res[:n]), list(res[n:])


def _all_reduce_small(part):
    def body(p_ref, out_ref, sib_buf, chip_buf, send_sems, recv_sems):
        x, y, c, chips = _place()
        me = 2 * x + y
        swap = pltpu.make_async_remote_copy(
            src_ref=p_ref, dst_ref=sib_buf, send_sem=send_sems.at[0], recv_sem=recv_sems.at[0],
            device_id=(x, y, 1 - c), device_id_type=MESH)
        swap.start()
        swap.wait()
        chip_buf[me] = p_ref[...] + sib_buf[...]
        copies = []
        for j, (px, py) in enumerate(chips):
            cp = pltpu.make_async_remote_copy(
                src_ref=chip_buf.at[me], dst_ref=chip_buf.at[me], send_sem=send_sems.at[1 + j],
                recv_sem=recv_sems.at[1 + j], device_id=(px, py, c), device_id_type=MESH)
            cp.start()
            copies.append(cp)
        for j, (px, py) in enumerate(chips):
            pltpu.make_async_remote_copy(
                src_ref=chip_buf.at[me], dst_ref=chip_buf.at[2 * px + py], send_sem=send_sems.at[1 + j],
                recv_sem=recv_sems.at[1 + j], device_id=(px, py, c), device_id_type=MESH).wait_recv()
        for cp in copies:
            cp.wait_send()
        tot = chip_buf[0]
        for q in range(1, N_CHIPS):
            tot = tot + chip_buf[q]
        out_ref[...] = tot

    vm = pl.BlockSpec(memory_space=pltpu.VMEM)
    return pl.pallas_call(
        body, name="all_reduce_small", in_specs=[vm], out_specs=vm,
        out_shape=jax.ShapeDtypeStruct(part.shape, F32),
        scratch_shapes=[pltpu.VMEM(part.shape, F32), pltpu.VMEM((N_CHIPS,) + part.shape, F32),
                        pltpu.SemaphoreType.DMA((N_CHIPS,)), pltpu.SemaphoreType.DMA((N_CHIPS,))],
    )(part)


def _row_tiles(shape):
    ax = next(d for d, s in enumerate(shape) if s > 1)
    tr = _gcd(ADAM_ROWS, shape[ax])
    block = tuple(tr if d == ax else s for d, s in enumerate(shape))
    return shape[ax] // tr, block, lambda i: tuple(i if d == ax else 0 for d in range(len(shape)))


def _sum_chips(name, r):
    shape = r.shape[1:]
    steps, block, index = _row_tiles(shape)

    def body(r_ref, o_ref):
        tot = r_ref[0].astype(F32)
        for q in range(1, N_CHIPS):
            tot = tot + r_ref[q].astype(F32)
        o_ref[...] = tot

    return pl.pallas_call(
        body, name=name, grid=(steps,),
        in_specs=[pl.BlockSpec((N_CHIPS,) + block, lambda i: (0,) + index(i))],
        out_specs=pl.BlockSpec(block, index),
        out_shape=jax.ShapeDtypeStruct(shape, F32), compiler_params=_params(1),
    )(r)


def _adamw_math(g, w, m, v):
    mn = ADAM_B1 * m + (1.0 - ADAM_B1) * g
    vn = ADAM_B2 * v + (1.0 - ADAM_B2) * jnp.square(g)
    m_hat = mn / (1.0 - ADAM_B1 ** ADAM_STEP)
    v_hat = vn / (1.0 - ADAM_B2 ** ADAM_STEP)
    return -ADAM_LR * (m_hat / (jnp.sqrt(v_hat) + ADAM_EPS) + ADAM_WD * w), mn, vn


def _adamw(name, g_parts, w, m, v):
    steps, block, index = _row_tiles(w.shape)
    n = len(g_parts)

    def body(*refs):
        g = refs[0][...]
        for r in refs[1:n]:
            g = g + r[...]
        w_ref, m_ref, v_ref, go_ref, d_ref, mo_ref, vo_ref = refs[n:]
        go_ref[...] = g
        d_ref[...], mo_ref[...], vo_ref[...] = _adamw_math(g, w_ref[...], m_ref[...], v_ref[...])

    spec = pl.BlockSpec(block, index)
    return pl.pallas_call(
        body, name=name, grid=(steps,), in_specs=[spec] * (n + 3), out_specs=[spec] * 4,
        out_shape=[jax.ShapeDtypeStruct(w.shape, F32)] * 4, compiler_params=_params(1),
    )(*g_parts, w, m, v)


def _adamw_vectors(items):
    n = len(items)

    def body(*refs):
        ins, outs = refs[:4 * n], refs[4 * n:]
        for k in range(n):
            g, w, m, v = (r[...] for r in ins[4 * k:4 * k + 4])
            outs[3 * k][...], outs[3 * k + 1][...], outs[3 * k + 2][...] = _adamw_math(g, w, m, v)

    vm = pl.BlockSpec(memory_space=pltpu.VMEM)
    res = pl.pallas_call(
        body, name="adamw_vectors", in_specs=[vm] * (4 * n), out_specs=[vm] * (3 * n),
        out_shape=[jax.ShapeDtypeStruct(it[1].shape, F32) for it in items for _ in range(3)],
    )(*[a for it in items for a in it])
    return [res[3 * k:3 * k + 3] for k in range(n)]


def _pack_rows(flat, dtype, multiple):
    n = flat.shape[0]
    total = -(-n // multiple) * multiple
    return jnp.pad(flat, (0, total - n)).astype(dtype).reshape(total // LANES, LANES)


def kernel(x, positions, ln_g, ln_b, a_w_in, a_b_in, a_conv_w, a_conv_b, a_norm_g, a_norm_b, a_w_out, a_b_out, kv_w_down, kv_norm_g, kv_w_uk, kv_w_uv, b_w_in, b_q_norm_g, b_w_uq, b_w_out, loss_target, m_ln_g, m_ln_b, m_a_w_in, m_a_b_in, m_a_conv_w, m_a_conv_b, m_a_norm_g, m_a_norm_b, m_a_w_out, m_a_b_out, m_kv_w_down, m_kv_norm_g, m_kv_w_uk, m_kv_w_uv, m_b_w_in, m_b_q_norm_g, m_b_w_uq, m_b_w_out, v_ln_g, v_ln_b, v_a_w_in, v_a_b_in, v_a_conv_w, v_a_conv_b, v_a_norm_g, v_a_norm_b, v_a_w_out, v_a_b_out, v_kv_w_down, v_kv_norm_g, v_kv_w_uk, v_kv_w_uv, v_b_w_in, v_b_q_norm_g, v_b_w_uq, v_b_w_out):
    T, D = x.shape[1], x.shape[2]
    E = N_CHIPS * a_w_out.shape[1]
    KC = a_conv_w.shape[1]
    RKV = kv_norm_g.shape[0]
    H, DN = kv_w_uk.shape[1], kv_w_uk.shape[2]
    RQ = b_q_norm_g.shape[1]
    HV = N_CHIPS * b_w_out.shape[1]
    assert DN == LANES and kv_w_uv.shape[2] == LANES and HV == H * LANES
    assert kv_w_down.shape[1] == RKV + ROPE_DIM and b_w_uq.shape[3] == DN + ROPE_DIM
    assert ln_g.shape[0] == 2 and a_w_in.shape[0] == 1 and b_w_in.shape[0] == 1
    alpha = (2.0 * ln_g.shape[0]) ** 0.25
    scale = 1.0 / math.sqrt(DN + ROPE_DIM)
    WK = -(-(RKV + LANES) // 256) * 256
    assert WK % RQ == 0
    Z_OFF = WK + RQ
    tmw, tq = min(TM_WIDE, T), min(TQ, T)
    t512, t1024 = _fit(512, T), _fit(1024, T)
    xs = x[0]
    tgt = loss_target[0]
    px, py = lax.axis_index("x"), lax.axis_index("y")
    chip = 2 * px + py

    mats = [a_w_out[0], kv_w_down, kv_w_uk, kv_w_uv, b_w_in[0], b_w_uq[0], b_w_out[0]]
    vecs = [a_b_in[0], a_conv_w[0], a_conv_b[0], a_norm_g[0], a_norm_b[0], a_b_out[0]]
    first = jnp.concatenate(
        [a_w_in[0].astype(BF16).reshape(-1)]
        + [lax.bitcast_convert_type(w.reshape(-1), BF16).reshape(-1) for w in vecs])
    rest = [w.astype(BF16) for w in mats]
    gathered = _all_gather_chips(_pack_rows(first, BF16, 2 * BF16_ROWS * LANES)).reshape(N_CHIPS, -1)
    gathered, rest = lax.optimization_barrier((gathered, rest))
    rest_sems = _push_start("gather_rest_start", rest, by_target=False)
    g_win = gathered[:, :a_w_in[0].size].reshape((N_CHIPS,) + a_w_in[0].shape)
    off = a_w_in[0].size
    fvec = []
    for w in vecs:
        bits = gathered[:, off:off + 2 * w.size].reshape((N_CHIPS,) + w.shape + (2,))
        fvec.append(lax.bitcast_convert_type(bits, F32))
        off += 2 * w.size
    cols = lambda g: jnp.moveaxis(g, 0, -2).reshape(g.shape[1:-1] + (N_CHIPS * g.shape[-1],))
    w_in = cols(g_win)
    b_in = cols(fvec[0][:, None, :])
    conv_w = cols(fvec[1])
    conv_b, norm_g, norm_b, b_out = (cols(f[:, None, :]) for f in fvec[2:])
    row = lambda a: a.reshape(1, -1)
    g0, b0, g1, b1 = row(ln_g[0]), row(ln_b[0]), row(ln_g[1]), row(ln_b[1])
    kv_g, q_g = row(kv_norm_g), row(b_q_norm_g[0])
    plain = lambda acc, ins, i, j: [acc]

    b_in = b_in + rest_sems[4][0, 0]
    (proj,) = _row_mm("a_in", [((xs,), None)], w_in, nt=False, tm=t1024, tn=_fit(1536, 3 * E), tk=_fit(1024, D),
                      outs=[((T, 3 * E), F32, 'tile')], epi=lambda acc, ins, i, j: [acc + ins[0]],
                      epi_ins=[(b_in, 'col')])
    u1 = _conv_fwd(proj, conv_w, conv_b, E, tmw)
    u4 = _conv_post(u1, proj, norm_g, norm_b, E, tmw)

    rest, landed = _push_wait("gather_rest_wait", *rest_sems[:4], after=[u4], by_target=False)
    g_wout, g_wd, g_uk, g_uv, g_wbin, g_wuq, g_wbout = [
        lax.dynamic_update_slice(l, w[None], (chip,) + (0,) * w.ndim) for w, l in zip(rest, landed)]
    w_out = g_wout.reshape(E, D)
    wd = g_wd.reshape(D, RKV + ROPE_DIM)
    zpad = jnp.zeros((D, ROPE_HALF), BF16)
    wd_p = jnp.concatenate(
        [wd[:, :RKV], wd[:, RKV:RKV + ROPE_HALF], zpad, wd[:, RKV + ROPE_HALF:], zpad,
         jnp.zeros((D, WK - RKV - LANES), BF16)], axis=1)
    w_bin = cols(g_wbin)
    w_z = w_bin[:, RQ:]
    wb_small = jnp.concatenate([wd_p, w_bin[:, :RQ]], axis=1)
    wb_all = jnp.concatenate([wd_p, w_bin], axis=1)
    w_kv = jnp.concatenate([g_uk.reshape(RKV, HV), g_uv.reshape(RKV, HV)], axis=1)
    wuq = g_wuq.reshape(RQ, H, DN + ROPE_DIM)
    zq = jnp.zeros((RQ, H, ROPE_HALF), BF16)
    w_qr = jnp.concatenate([wuq[:, :, DN:DN + ROPE_HALF], zq, wuq[:, :, DN + ROPE_HALF:], zq], axis=2)
    w_q = jnp.concatenate([wuq[:, :, :DN].reshape(RQ, HV), w_qr.reshape(RQ, HV)], axis=1)
    w_bout = g_wbout.reshape(HV, D)

    freqs = ROPE_THETA ** (-jnp.arange(0, ROPE_DIM, 2, dtype=F32) / ROPE_DIM)
    ang = positions[0].astype(F32)[:, None] * freqs
    cs, sn = jnp.cos(ang), jnp.sin(ang)
    ones, zeros = jnp.ones_like(cs), jnp.zeros_like(cs)
    cos_t = jnp.concatenate([cs, ones, cs, ones], axis=1)
    sin_t = jnp.concatenate([-sn, zeros, sn, zeros], axis=1)

    def ln_epi(acc, ins, i, j):
        bias, res, g, b = ins
        xhat, rstd = _ln_stats(alpha * res + acc + bias)
        h = xhat * g + b
        return [h, h, xhat, rstd]

    h1, h1b, xhat1, rstd1 = _row_mm(
        "a_out", [((u4,), None)], w_out, nt=False, tm=t512, tn=D, tk=_fit(2048, E),
        outs=[((T, D), F32, 'tile'), ((T, D), BF16, 'tile'), ((T, D), F32, 'tile'), ((T, 1), F32, 'row')],
        epi=ln_epi, epi_ins=[(b_out, 'col'), (xs, 'tile'), (g0, 'col'), (b0, 'col')])

    tkb = _fit(512, _gcd(WK, RQ, HV))
    (pb,) = _row_mm("b_in", [((h1b,), None)], wb_small, nt=False, tm=t1024, tn=_fit(1024, Z_OFF),
                    tk=_fit(1024, D), outs=[((T, Z_OFF), F32, 'tile')], epi=plain)
    (zb,) = _row_mm("b_in_gate", [((h1b,), None)], w_z, nt=False, tm=t1024, tn=_fit(2048, HV),
                    tk=_fit(1024, D), outs=[((T, HV), F32, 'tile')], epi=plain)
    c_lat, kr, cqn = _norm_prep(pb, kv_g, q_g, cos_t, sin_t, RKV, RQ, WK, tmw)
    (kv,) = _row_mm("kv_up", [((c_lat,), None)], w_kv, nt=False, tm=t1024, tn=_fit(2048, HV),
                    tk=_fit(1024, RKV), outs=[((T, 2 * HV), BF16, 'tile')], epi=plain)
    tnq = _fit(2048, HV)
    half_q = HV // tnq

    def q_epi(acc, ins, i, j):
        return [jnp.where(j >= half_q, _rope(acc, ins[0], ins[1]), acc)]

    (q_all,) = _row_mm("q_up", [((cqn,), None)], w_q, nt=False, tm=t1024, tn=tnq, tk=_fit(1024, RQ),
                       outs=[((T, 2 * HV), BF16, 'tile')], epi=q_epi,
                       epi_ins=[(cos_t, 'row'), (sin_t, 'row')])
    o, lse = _attn_fwd(q_all, kv, kr, H, tq, scale)

    def loss_epi(acc, ins, i, j):
        res, g, b, target = ins
        xhat, rstd = _ln_stats(alpha * res + acc)
        diff = xhat * g + b - target
        dr, dg, db = _ln_bwd(diff / D, xhat, rstd, g)
        return [dr, 0.5 * jnp.sum(diff * diff, keepdims=True) / D, dg, db]

    dr1, loss_part, dg1, db1 = _row_mm(
        "b_out", [((o, zb), _gate)], w_bout, nt=False, tm=tmw, tn=D, tk=_fit(2048, HV),
        outs=[((T, D), F32, 'tile'), ((1, 1), F32, 'acc'), ((1, D), F32, 'acc'), ((1, D), F32, 'acc')],
        epi=loss_epi, epi_ins=[(h1, 'tile'), (g1, 'col'), (b1, 'col'), (tgt, 'tile')])

    def gate_bwd_epi(acc, ins, i, j):
        return [acc * _silu(ins[1]), acc * ins[0] * _silu_grad(ins[1])]

    do, dz = _row_mm(
        "b_out_bwd", [((dr1,), None)], w_bout, nt=True, tm=tmw, tn=_fit(2048, HV), tk=_fit(1024, D),
        outs=[((T, HV), BF16, 'tile'), ((T, HV), BF16, 'tile')], epi=gate_bwd_epi,
        epi_ins=[(o, 'tile'), (zb, 'tile')])
    gw_bout = _tn_mm("dw_b_out", (o, zb), _gate, [((dr1,), None)], tn=_fit(1024, D), tk=t512, out_dtype=BF16)
    dqn, dqr_pre, dkn, dkr_h, dv = _attn_bwd(q_all, kv, kr, do, o, lse, cos_t, sin_t, H, tq, scale)
    dkr_pre = _key_rope_bwd(dkr_h, cos_t, sin_t, H, tmw)

    def cq_bwd_epi(acc, ins, i, j):
        dx, dg = _rms_bwd(acc, ins[0], ins[1])
        return [dx, dg]

    dcq, dqg = _row_mm(
        "q_up_bwd", [((dqn,), None), ((dqr_pre,), None)], w_q, nt=True, tm=t1024, tn=RQ, tk=_fit(2048, HV),
        outs=[((T, RQ), BF16, 'tile'), ((1, RQ), F32, 'acc')], epi=cq_bwd_epi,
        epi_ins=[(pb, pl.BlockSpec((t1024, RQ), lambda i, j, k: (i, WK // RQ))), (q_g, 'col')])
    gw_q = _tn_mm("dw_q_up", (cqn,), None, [((dqn,), None), ((dqr_pre,), None)],
                  tn=_fit(2048, HV), tk=t1024, out_dtype=BF16)

    def ckv_bwd_epi(acc, ins, i, j):
        blk, dkr_t, g = ins
        dx, dg = _rms_bwd(acc, blk[:, :RKV], g)
        parts = [dx, dkr_t]
        if WK > RKV + LANES:
            parts.append(jnp.zeros((dx.shape[0], WK - RKV - LANES), F32))
        return [jnp.concatenate(parts, axis=1), dg]

    dckv, dkvg = _row_mm(
        "kv_up_bwd", [((dkn,), None), ((dv,), None)], w_kv, nt=True, tm=t1024, tn=RKV, tk=_fit(2048, HV),
        outs=[((T, WK), BF16, pl.BlockSpec((t1024, WK), lambda i, j, k: (i, 0))), ((1, RKV), F32, 'acc')],
        epi=ckv_bwd_epi,
        epi_ins=[(pb, pl.BlockSpec((t1024, WK), lambda i, j, k: (i, 0))), (dkr_pre, 'row'), (kv_g, 'col')])
    gw_kv = _tn_mm("dw_kv_up", (c_lat,), None, [((dkn,), None), ((dv,), None)],
                   tn=_fit(2048, HV), tk=t1024, out_dtype=BF16)

    def ln1_bwd_epi(acc, ins, i, j):
        dr_up, xhat, rstd, g = ins
        dr, dg, db = _ln_bwd(alpha * dr_up + acc, xhat, rstd, g)
        return [dr, dg, db]

    dp_segs = [((dckv,), None), ((dcq,), None), ((dz,), None)]
    gw_ball = _tn_mm("dw_b_in", (h1b,), None, dp_segs, tn=tkb, tk=t1024, out_dtype=BF16)

    shard_cols = lambda g: jnp.moveaxis(g.reshape(g.shape[0], N_CHIPS, -1), 1, 0)
    shard_rows = lambda g: g.reshape(N_CHIPS, g.shape[0] // N_CHIPS, g.shape[1])
    gq = gw_q.reshape(RQ, 2, H, LANES)
    g_uq = jnp.concatenate(
        [gq[:, 0], gq[:, 1, :, :ROPE_HALF], gq[:, 1, :, 2 * ROPE_HALF:3 * ROPE_HALF]], axis=2)
    g_wd_full = jnp.concatenate(
        [gw_ball[:, :RKV], gw_ball[:, RKV:RKV + ROPE_HALF],
         gw_ball[:, RKV + 2 * ROPE_HALF:RKV + 3 * ROPE_HALF]], axis=1)
    late_names = ["kv_w_down", "kv_w_uk", "kv_w_uv", "b_w_in", "b_w_uq", "b_w_out"]
    late_w = [kv_w_down, kv_w_uk, kv_w_uv, b_w_in, b_w_uq, b_w_out]
    chip_major = [g_wd_full, gw_kv[:, :HV], gw_kv[:, HV:], shard_cols(gw_ball[:, WK:]), g_uq, gw_bout]
    late_grads = [g.reshape((N_CHIPS,) + w.shape) for g, w in zip(chip_major, late_w)]
    late_sems = _push_start("scatter_late_start", late_grads, by_target=True)

    dr0, dg0, db0 = _row_mm(
        "b_in_bwd", dp_segs, wb_all, nt=True, tm=t1024, tn=D, tk=tkb,
        outs=[((T, D), F32, 'tile'), ((1, D), F32, 'acc'), ((1, D), F32, 'acc')], epi=ln1_bwd_epi,
        epi_ins=[(dr1, 'tile'), (xhat1, 'tile'), (rstd1, 'row'), (g0 + late_sems[4][0, 0], 'col')])

    def conv_branch_bwd_epi(acc, ins, i, j):
        u1_t, z, g, b = ins
        xhat, rstd = _ln_stats(u1_t)
        u2 = xhat * g + b
        du3 = acc * _silu(z)
        dz_a = acc * _silu(u2) * _silu_grad(z)
        du1, dg, db = _ln_bwd(du3 * _silu_grad(u2), xhat, rstd, g)
        return [du1, dz_a, dg, db]

    du1, dz_a, dng, dnb = _row_mm(
        "a_out_bwd", [((dr0,), None)], w_out, nt=True, tm=tmw, tn=E, tk=_fit(1024, D),
        outs=[((T, E), F32, 'tile'), ((T, E), BF16, 'tile'), ((1, E), F32, 'acc'), ((1, E), F32, 'acc')],
        epi=conv_branch_bwd_epi,
        epi_ins=[(u1, 'tile'), (proj, pl.BlockSpec((tmw, E), lambda i, j, k: (i, 2))), (norm_g, 'col'),
                 (norm_b, 'col')])
    gw_out, dbo = _tn_mm("dw_a_out", (u4,), None, [((dr0,), None)], tn=_fit(1024, D), tk=t1024,
                         out_dtype=BF16, colsum=True)
    mid_sems = _push_start("scatter_mid_start", [gw_out.reshape((N_CHIPS,) + a_w_out.shape)], by_target=True)
    dval, dgate, dcw, dcb = _conv_bwd(du1, proj, conv_w + mid_sems[4][0, 0], E, tmw)
    dproj_segs = [((dval,), None), ((dgate,), None), ((dz_a,), None)]
    gw_in, dbi = _tn_mm("dw_a_in", (xs,), None, dproj_segs, tn=_fit(2048, E), tk=t512, out_dtype=BF16,
                        colsum=True)

    def own_block_in_place(sent, landed):
        own = lax.dynamic_index_in_dim(sent, chip, 0, keepdims=True)
        return lax.dynamic_update_slice(landed, own, (chip,) + (0,) * (sent.ndim - 1))

    def reduce_and_update(tag, names_, sent, landed, w_, m_, v_):
        sums = [_sum_chips("sum_" + n, own_block_in_place(s, l)) for n, s, l in zip(names_, sent, landed)]
        theirs = _swap_cores("swap_cores_" + tag, sums)
        return {n: _adamw("adamw_" + n, [mine, other], w, m, v)
                for n, mine, other, w, m, v in zip(names_, sums, theirs, w_, m_, v_)}

    late_sent, late_landed = _push_wait("scatter_late_wait", *late_sems[:4], after=[dbi], by_target=True)
    mid_sent, mid_landed = _push_wait("scatter_mid_wait", *mid_sems[:4], after=[dbi], by_target=True)
    early_sems = _push_start("scatter_early_start", [shard_cols(gw_in).reshape((N_CHIPS,) + a_w_in.shape)],
                             by_target=True)
    (grad_x,) = _row_mm(
        "a_in_bwd", dproj_segs, w_in, nt=True, tm=tmw, tn=D, tk=E, b_whole=True,
        outs=[((T, D), F32, 'tile')], epi=lambda acc, ins, i, j: [alpha * ins[0] + acc + ins[1]],
        epi_ins=[(dr0, 'tile'), (jnp.zeros((1, D), F32) + early_sems[4][0, 0], 'col')])
    big_out = reduce_and_update(
        "late", ["a_w_out"] + late_names, mid_sent + late_sent, mid_landed + late_landed,
        [a_w_out] + late_w, [m_a_w_out, m_kv_w_down, m_kv_w_uk, m_kv_w_uv, m_b_w_in, m_b_w_uq, m_b_w_out],
        [v_a_w_out, v_kv_w_down, v_kv_w_uk, v_kv_w_uv, v_b_w_in, v_b_w_uq, v_b_w_out])

    small_full = [jnp.concatenate([dg0, dg1]), jnp.concatenate([db0, db1]), dbi, dcw, dcb, dng, dnb, dbo,
                  dkvg, dqg]
    sflat = jnp.concatenate([g.reshape(-1) for g in small_full])
    summed = _all_reduce_small(_pack_rows(sflat, F32, 8 * LANES)).reshape(-1)
    soff = 0
    sgrads = []
    for g in small_full:
        sgrads.append(summed[soff:soff + g.size].reshape(g.shape))
        soff += g.size
    local_cols = lambda g, n: lax.dynamic_slice_in_dim(g, chip * n, n, axis=g.ndim - 1)
    snames = ["ln_g", "ln_b", "a_b_in", "a_conv_w", "a_conv_b", "a_norm_g", "a_norm_b", "a_b_out",
              "kv_norm_g", "b_q_norm_g"]
    small_w = [ln_g, ln_b, a_b_in, a_conv_w, a_conv_b, a_norm_g, a_norm_b, a_b_out, kv_norm_g, b_q_norm_g]
    small_m = [m_ln_g, m_ln_b, m_a_b_in, m_a_conv_w, m_a_conv_b, m_a_norm_g, m_a_norm_b, m_a_b_out,
               m_kv_norm_g, m_b_q_norm_g]
    small_v = [v_ln_g, v_ln_b, v_a_b_in, v_a_conv_w, v_a_conv_b, v_a_norm_g, v_a_norm_b, v_a_b_out,
               v_kv_norm_g, v_b_q_norm_g]
    sharded = {"a_b_in", "a_conv_w", "a_conv_b", "a_norm_g", "a_norm_b", "a_b_out"}
    local_g = [(local_cols(g, w.shape[-1]) if n in sharded else g).reshape(w.shape)
               for n, g, w in zip(snames, sgrads, small_w)]
    at_least_2d = lambda a: a.reshape((1,) + a.shape) if a.ndim == 1 else a
    sres = _adamw_vectors([tuple(at_least_2d(a) for a in item)
                           for item in zip(local_g, small_w, small_m, small_v)])
    small_out = {n: [g] + [r.reshape(w.shape) for r in res]
                 for n, g, w, res in zip(snames, local_g, small_w, sres)}

    early_sent, early_landed = _push_wait(
        "scatter_early_wait", *early_sems[:4], by_target=True,
        after=[grad_x, big_out["b_w_out"][1], small_out["b_q_norm_g"][1]])
    big_out.update(reduce_and_update("early", ["a_w_in"], early_sent, early_landed,
                                     [a_w_in], [m_a_w_in], [v_a_w_in]))

    loss = lax.psum(loss_part[0, 0], ("x", "y", "c"))
    order = ["ln_g", "ln_b", "a_w_in", "a_b_in", "a_conv_w", "a_conv_b", "a_norm_g", "a_norm_b", "a_w_out",
             "a_b_out", "kv_w_down", "kv_norm_g", "kv_w_uk", "kv_w_uv", "b_w_in", "b_q_norm_g", "b_w_uq",
             "b_w_out"]
    outs = {**big_out, **small_out}
    result = [loss, grad_x[None]]
    for part in range(4):
        result += [outs[n][part] for n in order]
    return tuple(result)
```

```python
import functools
import math

import jax
import jax.numpy as jnp
from jax import lax
from jax.experimental import pallas as pl
from jax.experimental.pallas import tpu as pltpu

F32, BF16 = jnp.float32, jnp.bfloat16
NN = (((1,), (0,)), ((), ()))
NT = (((1,), (1,)), ((), ()))
TN = (((0,), (0,)), ((), ()))
MESH = pl.DeviceIdType.MESH
ANY = pl.BlockSpec(memory_space=pl.ANY)

LANES = 128
BF16_ROWS = 16
VMEM_LIMIT = 56 * 1024 * 1024
N_CHIPS = 4
LN_EPS = 1e-5
RMS_EPS = 1e-6
MASK_VALUE = -1e30
LOG2_E = math.log2(math.e)
ROPE_THETA = 10000.0
ROPE_DIM = 64
ROPE_HALF = ROPE_DIM // 2
ADAM_LR, ADAM_B1, ADAM_B2, ADAM_EPS, ADAM_WD, ADAM_STEP = 0.001, 0.9, 0.999, 1e-08, 0.01, 10

MAX_TILE = 2048
TM_WIDE = 256
TQ = 512
CONV_HALO = 32
CONV_LC = 512
CONV_SUB = 256
SUBLANES = 8
CONV_RC = 32
ADAM_ROWS = 64


def _dot(a, b, dims):
    return lax.dot_general(a.astype(BF16), b.astype(BF16), dims, preferred_element_type=F32)


def _sig(x):
    return 1.0 / (1.0 + jnp.exp(-x))


def _params(n_axes):
    return pltpu.CompilerParams(dimension_semantics=("arbitrary",) * n_axes, vmem_limit_bytes=VMEM_LIMIT)


def _gcd(*v):
    return functools.reduce(math.gcd, v)


def _fit(want, dim):
    return math.gcd(min(want, MAX_TILE), dim)


def _row_mm(name, a_segs, b, *, nt, tm, tn, tk, outs, epi, epi_ins=(), b_whole=False):
    M = a_segs[0][0][0].shape[0]
    stacked = b.ndim == 3
    if stacked:
        n_blk = b.shape[2]
        N = b.shape[1] if nt else N_CHIPS * n_blk
        assert (nt and b_whole) or (not nt and tn == n_blk and tk == b.shape[1]), name
    else:
        N = b.shape[0] if nt else b.shape[1]
    nkb = [arrs[0].shape[1] // tk for arrs, _ in a_segs]
    koff = [sum(nkb[:s]) for s in range(len(nkb))]
    ni, nj, nk = M // tm, N // tn, sum(nkb)
    assert M % tm == 0 and N % tn == 0 and all(arrs[0].shape[1] % tk == 0 for arrs, _ in a_segs), name
    assert stacked or (b.shape[1] if nt else b.shape[0]) == nk * tk, name

    def spec_of(shape, kind):
        if isinstance(kind, pl.BlockSpec):
            return kind
        if kind == 'tile':
            return pl.BlockSpec((tm, tn), lambda i, j, k: (i, j))
        if kind == 'row':
            return pl.BlockSpec((tm, shape[1]), lambda i, j, k: (i, 0))
        if kind == 'col':
            return pl.BlockSpec((1, tn), lambda i, j, k: (0, j))
        assert kind == 'acc' and nj == 1, name
        return pl.BlockSpec(shape, lambda i, j, k: (0,) * len(shape))

    in_specs, operands = [], []
    for s, (arrs, _) in enumerate(a_segs):
        for arr in arrs:
            in_specs.append(pl.BlockSpec(
                (tm, tk), lambda i, j, k, s=s: (i, jnp.clip(k - koff[s], 0, nkb[s] - 1))))
            operands.append(arr)
    if b_whole:
        assert nt and nj == 1 and all(n == 1 for n in nkb), name
        in_specs.append(pl.BlockSpec(b.shape, lambda i, j, k: (0,) * b.ndim))
    elif stacked:
        in_specs.append(pl.BlockSpec((1, tk, tn), lambda i, j, k: (j, 0, 0)))
    else:
        in_specs.append(pl.BlockSpec((tn, tk), lambda i, j, k: (j, k)) if nt
                        else pl.BlockSpec((tk, tn), lambda i, j, k: (k, j)))
    operands.append(b)
    for arr, kind in epi_ins:
        in_specs.append(spec_of(arr.shape, kind))
        operands.append(arr)
    out_specs = [spec_of(shape, kind) for shape, _, kind in outs]
    out_shape = [jax.ShapeDtypeStruct(shape, dtype) for shape, dtype, _ in outs]
    n_seg_refs = [len(arrs) for arrs, _ in a_segs]

    def body(*refs):
        pos = 0
        seg_refs = []
        for n in n_seg_refs:
            seg_refs.append(refs[pos:pos + n])
            pos += n
        b_ref = refs[pos]
        e_refs = refs[pos + 1:pos + 1 + len(epi_ins)]
        o_refs = refs[pos + 1 + len(epi_ins):pos + 1 + len(epi_ins) + len(outs)]
        i, j, k = pl.program_id(0), pl.program_id(1), pl.program_id(2)

        def product(fn, rs, s=0):
            a = rs[0][...] if fn is None else fn(*[r[...] for r in rs])
            if b_whole and stacked:
                lo, hi, tot = koff[s] * tk, (koff[s] + 1) * tk, None
                for q in range(lo // n_blk, (hi - 1) // n_blk + 1):
                    c0, c1 = max(lo, q * n_blk), min(hi, (q + 1) * n_blk)
                    part = _dot(a[:, c0 - lo:c1 - lo], b_ref[q, :, c0 - q * n_blk:c1 - q * n_blk], NT)
                    tot = part if tot is None else tot + part
                return tot
            if b_whole:
                return _dot(a, b_ref[:, koff[s] * tk:(koff[s] + 1) * tk], NT)
            return _dot(a, b_ref[0] if stacked else b_ref[...], NT if nt else NN)

        def finish(acc):
            res = epi(acc, [r[...] for r in e_refs], i, j)
            for o_ref, (_, _, kind), r in zip(o_refs, outs, res):
                if isinstance(kind, str) and kind == 'acc':
                    @pl.when(i == 0)
                    def _(o_ref=o_ref, r=r):
                        o_ref[...] = r

                    @pl.when(i > 0)
                    def _(o_ref=o_ref, r=r):
                        o_ref[...] += r
                else:
                    o_ref[...] = r.astype(o_ref.dtype)

        if nk == 1:
            finish(product(a_segs[0][1], seg_refs[0]))
            return
        acc_ref = refs[-1]

        @pl.when(k == 0)
        def _():
            acc_ref[...] = jnp.zeros_like(acc_ref)

        for s, ((_, fn), rs) in enumerate(zip(a_segs, seg_refs)):
            def accumulate(fn=fn, rs=rs, s=s):
                acc_ref[...] += product(fn, rs, s)
            if len(a_segs) == 1:
                accumulate()
            else:
                pl.when(jnp.logical_and(k >= koff[s], k < koff[s] + nkb[s]))(accumulate)

        @pl.when(k == nk - 1)
        def _():
            finish(acc_ref[...])

    return pl.pallas_call(
        body, name=name, grid=(ni, nj, nk), in_specs=in_specs, out_specs=out_specs, out_shape=out_shape,
        scratch_shapes=[] if nk == 1 else [pltpu.VMEM((tm, tn), F32)], compiler_params=_params(3),
    )(*operands)


def _tn_mm(name, a_arrs, a_fn, b_segs, *, tn, tk, out_dtype, shard_major=False, colsum=False):
    T, M = a_arrs[0].shape
    nbj = [arrs[0].shape[1] // tn for arrs, _ in b_segs]
    joff = [sum(nbj[:s]) for s in range(len(nbj))]
    nj, nk = sum(nbj), T // tk
    N = nj * tn
    assert T % tk == 0 and all(arrs[0].shape[1] % tn == 0 for arrs, _ in b_segs), name

    in_specs = [pl.BlockSpec((tk, M), lambda j, k: (k, 0)) for _ in a_arrs]
    operands = list(a_arrs)
    for s, (arrs, _) in enumerate(b_segs):
        for arr in arrs:
            in_specs.append(pl.BlockSpec(
                (tk, tn), lambda j, k, s=s: (k, jnp.clip(j - joff[s], 0, nbj[s] - 1))))
            operands.append(arr)
    if shard_major:
        per = (N // N_CHIPS) // tn
        assert per * tn * N_CHIPS == N, name
        out_shape = [jax.ShapeDtypeStruct((N_CHIPS, M, N // N_CHIPS), out_dtype)]
        out_specs = [pl.BlockSpec((1, M, tn), lambda j, k: (j // per, 0, j % per))]
    else:
        out_shape = [jax.ShapeDtypeStruct((M, N), out_dtype)]
        out_specs = [pl.BlockSpec((M, tn), lambda j, k: (0, j))]
    if colsum:
        out_shape.append(jax.ShapeDtypeStruct((1, N), F32))
        out_specs.append(pl.BlockSpec((1, tn), lambda j, k: (0, j)))
    n_a = len(a_arrs)
    n_seg_refs = [len(arrs) for arrs, _ in b_segs]

    def body(*refs):
        a_refs = refs[:n_a]
        pos = n_a
        seg_refs = []
        for n in n_seg_refs:
            seg_refs.append(refs[pos:pos + n])
            pos += n
        o_ref = refs[pos]
        cs_ref = refs[pos + 1] if colsum else None
        acc_ref = refs[-1]
        j, k = pl.program_id(0), pl.program_id(1)

        @pl.when(k == 0)
        def _():
            acc_ref[...] = jnp.zeros_like(acc_ref)
            if colsum:
                cs_ref[...] = jnp.zeros_like(cs_ref)

        for s, ((_, fn), rs) in enumerate(zip(b_segs, seg_refs)):
            def accumulate(fn=fn, rs=rs):
                a = a_refs[0][...] if a_fn is None else a_fn(*[r[...] for r in a_refs])
                bt = rs[0][...] if fn is None else fn(*[r[...] for r in rs])
                acc_ref[...] += _dot(a, bt, TN)
                if colsum:
                    cs_ref[...] += jnp.sum(bt.astype(F32), axis=0, keepdims=True)
            if len(b_segs) == 1:
                accumulate()
            else:
                pl.when(jnp.logical_and(j >= joff[s], j < joff[s] + nbj[s]))(accumulate)

        @pl.when(k == nk - 1)
        def _():
            if shard_major:
                o_ref[0] = acc_ref[...].astype(o_ref.dtype)
            else:
                o_ref[...] = acc_ref[...].astype(o_ref.dtype)

    res = pl.pallas_call(
        body, name=name, grid=(nj, nk), in_specs=in_specs, out_specs=out_specs, out_shape=out_shape,
        scratch_shapes=[pltpu.VMEM((M, tn), F32)], compiler_params=_params(2),
    )(*operands)
    return res if colsum else res[0]


def _silu(z):
    return z * _sig(z)


def _silu_grad(z):
    s = _sig(z)
    return s * (1.0 + z * (1.0 - s))


def _gate(o, z):
    return o * _silu(z)


def _ln_stats(r):
    mu = jnp.mean(r, axis=1, keepdims=True)
    xc = r - mu
    var = jnp.mean(xc * xc, axis=1, keepdims=True)
    rstd = lax.rsqrt(var + LN_EPS)
    return xc * rstd, rstd


def _ln_bwd(dy, xhat, rstd, g):
    dxh = dy * g
    m1 = jnp.mean(dxh, axis=1, keepdims=True)
    m2 = jnp.mean(dxh * xhat, axis=1, keepdims=True)
    return (rstd * (dxh - m1 - xhat * m2), jnp.sum(dy * xhat, axis=0, keepdims=True),
            jnp.sum(dy, axis=0, keepdims=True))


def _rms_fwd(x, g):
    rstd = lax.rsqrt(jnp.mean(x * x, axis=1, keepdims=True) + RMS_EPS)
    return x * rstd * g


def _rms_bwd(dy, x, g):
    rstd = lax.rsqrt(jnp.mean(x * x, axis=1, keepdims=True) + RMS_EPS)
    xn = x * rstd
    dxn = dy * g
    return rstd * (dxn - xn * jnp.mean(dxn * xn, axis=1, keepdims=True)), jnp.sum(dy * xn, axis=0, keepdims=True)


def _rope(x, cos, sin, transpose=False):
    parts = []
    for g in range(x.shape[1] // LANES):
        xg = x[:, g * LANES:(g + 1) * LANES]
        if transpose:
            parts.append(xg * cos + pltpu.roll(xg * sin, LANES // 2, 1))
        else:
            parts.append(xg * cos + pltpu.roll(xg, LANES // 2, 1) * sin)
    return parts[0] if len(parts) == 1 else jnp.concatenate(parts, axis=1)


def _shifted_rows(window, rc):
    n = window.shape[0]
    for b in range(SUBLANES):
        rolled = window if b == 0 else pltpu.roll(window, n - b, 0)
        for a8 in range(0, n - rc - b + 1, SUBLANES):
            yield a8 + b, rolled[a8:a8 + rc]


def _conv_fwd(proj, conv_w, conv_b, E, tm):
    T = proj.shape[0]
    kc = conv_w.shape[0]
    lc, hb, rc = min(CONV_LC, E), CONV_HALO, min(CONV_RC, tm)
    nl, ni, ratio = E // lc, T // tm, tm // hb
    gate_off = E // lc

    sub = min(CONV_SUB, lc)
    base = hb - (kc - 1)

    def body(val_ref, gate_ref, valh_ref, gateh_ref, w_ref, cb_ref, u1_ref, ubuf):
        i = pl.program_id(1)
        ubuf[hb:, :] = val_ref[...] * _sig(gate_ref[...])
        halo = valh_ref[...] * _sig(gateh_ref[...])
        ubuf[0:hb, :] = jnp.where(i > 0, halo, 0.0)
        for l0 in range(0, lc, sub):
            ls = slice(l0, l0 + sub)
            for r0 in range(0, tm, rc):
                acc = jnp.zeros((rc, sub), F32) + cb_ref[:, ls]
                for off, rows in _shifted_rows(ubuf[r0:r0 + hb + rc, ls], rc):
                    if 0 <= off - base < kc:
                        acc += w_ref[off - base:off - base + 1, ls] * rows
                u1_ref[r0:r0 + rc, ls] = acc

    return pl.pallas_call(
        body, name="conv_fwd", grid=(nl, ni),
        in_specs=[
            pl.BlockSpec((tm, lc), lambda l, i: (i, l)),
            pl.BlockSpec((tm, lc), lambda l, i: (i, gate_off + l)),
            pl.BlockSpec((hb, lc), lambda l, i: (jnp.maximum(i * ratio - 1, 0), l)),
            pl.BlockSpec((hb, lc), lambda l, i: (jnp.maximum(i * ratio - 1, 0), gate_off + l)),
            pl.BlockSpec((kc, lc), lambda l, i: (0, l)),
            pl.BlockSpec((1, lc), lambda l, i: (0, l)),
        ],
        out_specs=pl.BlockSpec((tm, lc), lambda l, i: (i, l)),
        out_shape=jax.ShapeDtypeStruct((T, E), F32),
        scratch_shapes=[pltpu.VMEM((hb + tm, lc), F32)], compiler_params=_params(2),
    )(proj, proj, proj, proj, conv_w, conv_b)


def _conv_post(u1, proj, norm_g, norm_b, E, tm):
    T = u1.shape[0]

    def body(u1_ref, z_ref, g_ref, b_ref, u4_ref):
        xhat, _ = _ln_stats(u1_ref[...])
        u4_ref[...] = (_silu(xhat * g_ref[...] + b_ref[...]) * _silu(z_ref[...])).astype(BF16)

    return pl.pallas_call(
        body, name="conv_post", grid=(T // tm,),
        in_specs=[pl.BlockSpec((tm, E), lambda i: (i, 0)), pl.BlockSpec((tm, E), lambda i: (i, 2)),
                  pl.BlockSpec((1, E), lambda i: (0, 0)), pl.BlockSpec((1, E), lambda i: (0, 0))],
        out_specs=pl.BlockSpec((tm, E), lambda i: (i, 0)),
        out_shape=jax.ShapeDtypeStruct((T, E), BF16), compiler_params=_params(1),
    )(u1, proj, norm_g, norm_b)


def _conv_bwd(du1, proj, conv_w, E, tm):
    T = du1.shape[0]
    kc = conv_w.shape[0]
    lc, hb, rc = min(CONV_LC, E), CONV_HALO, min(CONV_RC, tm)
    nl, ni, ratio = E // lc, T // tm, tm // hb
    gate_off = E // lc
    last_halo = T // hb - 1

    sub = min(CONV_SUB, lc)
    base = hb - (kc - 1)

    def body(du_ref, dun_ref, val_ref, gate_ref, valh_ref, gateh_ref, w_ref,
             dval_ref, dgate_ref, dw_ref, db_ref, ubuf, dbuf, sbuf, dw_sc):
        i = pl.program_id(1)
        sbuf[...] = _sig(gate_ref[...])
        ubuf[hb:, :] = val_ref[...] * sbuf[...]
        halo = valh_ref[...] * _sig(gateh_ref[...])
        ubuf[0:hb, :] = jnp.where(i > 0, halo, 0.0)
        dbuf[0:tm, :] = du_ref[...]
        dbuf[tm:, :] = jnp.where(i < ni - 1, dun_ref[...], 0.0)

        @pl.when(i == 0)
        def _():
            dw_sc[...] = jnp.zeros_like(dw_sc)
            db_ref[...] = jnp.zeros_like(db_ref)

        db_ref[...] += jnp.sum(du_ref[...], axis=0, keepdims=True)
        for l0 in range(0, lc, sub):
            ls = slice(l0, l0 + sub)
            for r0 in range(0, tm, rc):
                dwin = dbuf[r0:r0 + rc + hb, ls]
                dchunk = dwin[0:rc]
                for off, rows in _shifted_rows(ubuf[r0:r0 + hb + rc, ls], rc):
                    k = off - base
                    if 0 <= k < kc:
                        prod = rows * dchunk
                        part = prod[0:SUBLANES]
                        for s8 in range(SUBLANES, rc, SUBLANES):
                            part = part + prod[s8:s8 + SUBLANES]
                        dw_sc[k, :, ls] += part
                acc = jnp.zeros((rc, sub), F32)
                for off, rows in _shifted_rows(dwin, rc):
                    k = (kc - 1) - off
                    if 0 <= k < kc:
                        acc += w_ref[k:k + 1, ls] * rows
                v, s = val_ref[r0:r0 + rc, ls], sbuf[r0:r0 + rc, ls]
                dval_ref[r0:r0 + rc, ls] = (acc * s).astype(BF16)
                dgate_ref[r0:r0 + rc, ls] = (acc * v * s * (1.0 - s)).astype(BF16)

        @pl.when(i == ni - 1)
        def _():
            for k in range(kc):
                dw_ref[k:k + 1, :] = jnp.sum(dw_sc[k], axis=0, keepdims=True)

    return pl.pallas_call(
        body, name="conv_bwd", grid=(nl, ni),
        in_specs=[
            pl.BlockSpec((tm, lc), lambda l, i: (i, l)),
            pl.BlockSpec((hb, lc), lambda l, i: (jnp.minimum((i + 1) * ratio, last_halo), l)),
            pl.BlockSpec((tm, lc), lambda l, i: (i, l)),
            pl.BlockSpec((tm, lc), lambda l, i: (i, gate_off + l)),
            pl.BlockSpec((hb, lc), lambda l, i: (jnp.maximum(i * ratio - 1, 0), l)),
            pl.BlockSpec((hb, lc), lambda l, i: (jnp.maximum(i * ratio - 1, 0), gate_off + l)),
            pl.BlockSpec((kc, lc), lambda l, i: (0, l)),
        ],
        out_specs=[pl.BlockSpec((tm, lc), lambda l, i: (i, l)), pl.BlockSpec((tm, lc), lambda l, i: (i, l)),
                   pl.BlockSpec((kc, lc), lambda l, i: (0, l)), pl.BlockSpec((1, lc), lambda l, i: (0, l))],
        out_shape=[jax.ShapeDtypeStruct((T, E), BF16), jax.ShapeDtypeStruct((T, E), BF16),
                   jax.ShapeDtypeStruct((kc, E), F32), jax.ShapeDtypeStruct((1, E), F32)],
        scratch_shapes=[pltpu.VMEM((hb + tm, lc), F32), pltpu.VMEM((tm + hb, lc), F32),
                        pltpu.VMEM((tm, lc), F32), pltpu.VMEM((kc, SUBLANES, lc), F32)],
        compiler_params=_params(2),
    )(du1, du1, proj, proj, proj, proj, conv_w)


def _norm_prep(pb, kv_g, q_g, cos, sin, rkv, rq, wk, tm):
    T = pb.shape[0]

    def body(ckv_ref, cq_ref, kg_ref, qg_ref, cos_ref, sin_ref, c_ref, kr_ref, cqn_ref):
        blk = ckv_ref[...]
        c_ref[...] = _rms_fwd(blk[:, :rkv], kg_ref[...]).astype(BF16)
        kr_ref[...] = _rope(blk[:, rkv:rkv + LANES], cos_ref[...], sin_ref[...]).astype(BF16)
        cqn_ref[...] = _rms_fwd(cq_ref[...], qg_ref[...]).astype(BF16)

    return pl.pallas_call(
        body, name="norm_prep", grid=(T // tm,),
        in_specs=[pl.BlockSpec((tm, wk), lambda i: (i, 0)), pl.BlockSpec((tm, rq), lambda i: (i, wk // rq)),
                  pl.BlockSpec((1, rkv), lambda i: (0, 0)), pl.BlockSpec((1, rq), lambda i: (0, 0)),
                  pl.BlockSpec((tm, LANES), lambda i: (i, 0)), pl.BlockSpec((tm, LANES), lambda i: (i, 0))],
        out_specs=[pl.BlockSpec((tm, rkv), lambda i: (i, 0)), pl.BlockSpec((tm, LANES), lambda i: (i, 0)),
                   pl.BlockSpec((tm, rq), lambda i: (i, 0))],
        out_shape=[jax.ShapeDtypeStruct((T, rkv), BF16), jax.ShapeDtypeStruct((T, LANES), BF16),
                   jax.ShapeDtypeStruct((T, rq), BF16)],
        compiler_params=_params(1),
    )(pb, pb, kv_g, q_g, cos, sin)


def _attn_fwd(q_all, kv, kr, H, tq, scale):
    T = q_all.shape[0]
    nq = T // tq
    pair = 2
    W = pair * LANES
    assert H % pair == 0
    hp_n = H // pair

    def body(qn_ref, qr_ref, kn_ref, kr_ref, v_ref, o_ref, lse_ref, *scratch):
        qi = pl.program_id(1)
        chains = [scratch[4 * a:4 * a + 4] for a in range(pair)]
        lanes = [slice(a * LANES, (a + 1) * LANES) for a in range(pair)]
        groups = [slice(c * LANES, (c + 1) * LANES) for c in range(tq // LANES)]

        def fold(x, op):
            r = x[:, groups[0]]
            for gsl in groups[1:]:
                r = op(r, x[:, gsl])
            return r

        for _, m_sc, l_sc, acc_sc in chains:
            m_sc[...] = jnp.full_like(m_sc, MASK_VALUE)
            l_sc[...] = jnp.zeros_like(l_sc)
            acc_sc[...] = jnp.zeros_like(acc_sc)

        def scores(j, masked):
            rows = pl.ds(pl.multiple_of(j * tq, tq), tq)
            krope = kr_ref[rows, :]
            for a, (s_sc, m_sc, _, _) in enumerate(chains):
                q = jnp.concatenate([qn_ref[:, lanes[a]], qr_ref[:, lanes[a]]], axis=1)
                k = jnp.concatenate([kn_ref[rows, lanes[a]], krope], axis=1)
                s = _dot(q, k, NT) * (scale * LOG2_E)
                if masked:
                    row = lax.broadcasted_iota(jnp.int32, s.shape, 0)
                    col = lax.broadcasted_iota(jnp.int32, s.shape, 1)
                    s = jnp.where(col <= row, s, MASK_VALUE)
                s_sc[j] = s
                m_sc[...] = jnp.maximum(m_sc[...], fold(s, jnp.maximum))

        def two_per_trip(fn, count):
            def two(p, carry):
                fn(2 * p)
                fn(2 * p + 1)
                return carry

            lax.fori_loop(0, count // 2, two, 0)

            @pl.when(count % 2 == 1)
            def _():
                fn(count - 1)

        two_per_trip(functools.partial(scores, masked=False), qi)
        scores(qi, True)
        for _, m_sc, _, _ in chains:
            m_sc[...] = jnp.broadcast_to(jnp.max(m_sc[...], axis=1, keepdims=True), m_sc.shape)

        def weigh(j):
            rows = pl.ds(pl.multiple_of(j * tq, tq), tq)
            for a, (s_sc, m_sc, l_sc, acc_sc) in enumerate(chains):
                s, m = s_sc[j], m_sc[...]
                p = jnp.concatenate([jnp.exp2(s[:, gsl] - m) for gsl in groups], axis=1)
                l_sc[...] += fold(p, jnp.add)
                acc_sc[...] += _dot(p, v_ref[rows, lanes[a]], NN)

        two_per_trip(weigh, qi + 1)
        for a, (_, m_sc, l_sc, acc_sc) in enumerate(chains):
            l = jnp.sum(l_sc[...], axis=1, keepdims=True)
            o_ref[:, lanes[a]] = acc_sc[...] / l
            lse_ref[a] = m_sc[:, 0:1] * (1.0 / LOG2_E) + jnp.log(l)

    chain_scratch = [pltpu.VMEM((nq, tq, tq), F32), pltpu.VMEM((tq, LANES), F32), pltpu.VMEM((tq, LANES), F32),
                     pltpu.VMEM((tq, LANES), F32)]
    return pl.pallas_call(
        body, name="attn_fwd", grid=(hp_n, nq),
        in_specs=[pl.BlockSpec((tq, W), lambda hp, qi: (qi, hp)),
                  pl.BlockSpec((tq, W), lambda hp, qi: (qi, hp_n + hp)),
                  pl.BlockSpec((T, W), lambda hp, qi: (0, hp)),
                  pl.BlockSpec((T, LANES), lambda hp, qi: (0, 0)),
                  pl.BlockSpec((T, W), lambda hp, qi: (0, hp_n + hp))],
        out_specs=[pl.BlockSpec((tq, W), lambda hp, qi: (qi, hp)),
                   pl.BlockSpec((pair, tq, 1), lambda hp, qi: (hp, qi, 0))],
        out_shape=[jax.ShapeDtypeStruct((T, H * LANES), F32), jax.ShapeDtypeStruct((H, T, 1), F32)],
        scratch_shapes=chain_scratch * pair, compiler_params=_params(2),
    )(q_all, q_all, kv, kr, kv)


def _attn_bwd(q_all, kv, kr, do, o, lse, cos, sin, H, tq, scale):
    T = q_all.shape[0]
    nq = T // tq
    HV = H * LANES
    pair = 2
    tk2 = pair * tq
    ng = T // tk2
    assert ng * tk2 == T and pair == 2

    def body(qn_ref, qr_ref, kn_ref, kr_ref, v_ref, do_ref, o_ref, lse_ref, cos_ref, sin_ref,
             dqn_ref, dqr_ref, dkn_ref, dkr_ref, dv_ref, dq_sc, dk_sc, dv_sc):
        g = pl.program_id(1)

        @pl.when(g == 0)
        def _():
            dq_sc[...] = jnp.zeros_like(dq_sc)

        key_rows = [slice(c * tq, (c + 1) * tq) for c in range(pair)]

        def block(qi, modes):
            rows = pl.ds(pl.multiple_of(qi * tq, tq), tq)
            q = jnp.concatenate([qn_ref[rows, :], qr_ref[rows, :]], axis=1)
            dov = do_ref[rows, :]
            delta = jnp.sum(dov.astype(F32) * o_ref[rows, :], axis=1, keepdims=True)
            lse_q = lse_ref[0, rows, :]
            dq, dkv = None, []
            for kr_, masked in zip(key_rows, modes):
                if masked is None:
                    dkv.append(None)
                    continue
                k = jnp.concatenate([kn_ref[kr_, :], kr_ref[kr_, :]], axis=1)
                s = _dot(q, k, NT) * scale
                if masked:
                    row = lax.broadcasted_iota(jnp.int32, s.shape, 0)
                    col = lax.broadcasted_iota(jnp.int32, s.shape, 1)
                    s = jnp.where(col <= row, s, MASK_VALUE)
                p = jnp.exp(s - lse_q)
                dv = _dot(p, dov, TN)
                dp = _dot(dov, v_ref[kr_, :], NT)
                ds = (p * (dp - delta) * scale).astype(BF16)
                dkv.append((_dot(ds, q, TN), dv))
                part = _dot(ds, k, NN)
                dq = part if dq is None else dq + part
            return rows, dq, dkv

        rows_a, dq_a, (kv_a0, _) = block(pair * g, (True, None))
        rows_b, dq_b, (kv_b0, kv_b1) = block(pair * g + 1, (False, True))
        dk_sc[key_rows[0], :] = kv_a0[0] + kv_b0[0]
        dv_sc[key_rows[0], :] = kv_a0[1] + kv_b0[1]
        dk_sc[key_rows[1], :] = kv_b1[0]
        dv_sc[key_rows[1], :] = kv_b1[1]
        dq_sc[rows_a, :] += dq_a
        dq_sc[rows_b, :] += dq_b

        def below(trip, carry):
            for qi in (pair * g + pair + 2 * trip, pair * g + pair + 2 * trip + 1):
                rows, dq, dkv = block(qi, (False, False))
                for kr_, (dk, dv) in zip(key_rows, dkv):
                    dk_sc[kr_, :] += dk
                    dv_sc[kr_, :] += dv
                dq_sc[rows, :] += dq
            return carry

        lax.fori_loop(0, (nq - pair * g - pair) // 2, below, 0)
        dkn_ref[...] = dk_sc[:, :LANES].astype(BF16)
        dkr_ref[...] = dk_sc[:, LANES:]
        dv_ref[...] = dv_sc[...].astype(BF16)

        @pl.when(g == ng - 1)
        def _():
            dqn_ref[...] = dq_sc[:, :LANES].astype(BF16)
            dqr_ref[...] = _rope(dq_sc[:, LANES:], cos_ref[...], sin_ref[...], transpose=True).astype(BF16)

    whole = lambda col: pl.BlockSpec((T, LANES), col)
    tile = lambda col: pl.BlockSpec((tk2, LANES), col)
    return pl.pallas_call(
        body, name="attn_bwd", grid=(H, ng),
        in_specs=[whole(lambda h, g: (0, h)), whole(lambda h, g: (0, H + h)),
                  tile(lambda h, g: (g, h)), tile(lambda h, g: (g, 0)), tile(lambda h, g: (g, H + h)),
                  whole(lambda h, g: (0, h)), whole(lambda h, g: (0, h)),
                  pl.BlockSpec((1, T, 1), lambda h, g: (h, 0, 0)),
                  whole(lambda h, g: (0, 0)), whole(lambda h, g: (0, 0))],
        out_specs=[whole(lambda h, g: (0, h)), whole(lambda h, g: (0, h)),
                   tile(lambda h, g: (g, h)), tile(lambda h, g: (g, h)), tile(lambda h, g: (g, h))],
        out_shape=[jax.ShapeDtypeStruct((T, HV), BF16), jax.ShapeDtypeStruct((T, HV), BF16),
                   jax.ShapeDtypeStruct((T, HV), BF16), jax.ShapeDtypeStruct((T, HV), F32),
                   jax.ShapeDtypeStruct((T, HV), BF16)],
        scratch_shapes=[pltpu.VMEM((T, 2 * LANES), F32), pltpu.VMEM((tk2, 2 * LANES), F32),
                        pltpu.VMEM((tk2, LANES), F32)],
        compiler_params=_params(2),
    )(q_all, q_all, kv, kr, kv, do, o, lse, cos, sin)


def _key_rope_bwd(dkr_heads, cos, sin, H, tm):
    T, HV = dkr_heads.shape

    def body(dkr_ref, cos_ref, sin_ref, dk_ref):
        dk = dkr_ref[...]
        tot = dk[:, 0:LANES]
        for h in range(1, H):
            tot = tot + dk[:, h * LANES:(h + 1) * LANES]
        dk_ref[...] = _rope(tot, cos_ref[...], sin_ref[...], transpose=True)

    return pl.pallas_call(
        body, name="key_rope_bwd", grid=(T // tm,),
        in_specs=[pl.BlockSpec((tm, HV), lambda i: (i, 0)),
                  pl.BlockSpec((tm, LANES), lambda i: (i, 0)), pl.BlockSpec((tm, LANES), lambda i: (i, 0))],
        out_specs=pl.BlockSpec((tm, LANES), lambda i: (i, 0)),
        out_shape=jax.ShapeDtypeStruct((T, LANES), F32), compiler_params=_params(1),
    )(dkr_heads, cos, sin)


def _place():
    x, y, c = lax.axis_index("x"), lax.axis_index("y"), lax.axis_index("c")
    chips = [(1 - x, y), (x, 1 - y), (1 - x, 1 - y)]
    return x, y, c, chips


def _all_gather_chips(arrs):
    n = len(arrs)
    halves = [a.shape[0] // 2 for a in arrs]
    assert all(h * 2 == a.shape[0] and h % BF16_ROWS == 0 for h, a in zip(halves, arrs))

    def body(*refs):
        w_refs, out_refs = refs[:n], refs[n:2 * n]
        send_sems, recv_sems, local_sems = refs[2 * n:]
        x, y, c, chips = _place()
        sibling = (x, y, 1 - c)
        waits = []
        for w, (w_ref, out_ref, half) in enumerate(zip(w_refs, out_refs, halves)):
            def region(px, py, pc, out_ref=out_ref, half=half):
                return out_ref.at[2 * px + py, pl.ds(pc * half, half), :]

            def copy(k, block, to, src=None, w=w, region=region):
                return pltpu.make_async_remote_copy(
                    src_ref=region(*block) if src is None else src, dst_ref=region(*block),
                    send_sem=send_sems.at[6 * w + k], recv_sem=recv_sems.at[6 * w + k],
                    device_id=to, device_id_type=MESH)

            mine = pltpu.make_async_copy(w_ref, out_ref.at[2 * x + y], local_sems.at[w])
            mine.start()
            my_half = w_ref.at[pl.ds(c * half, half), :]
            first = [copy(j, (x, y, c), (*chip, c), src=my_half) for j, chip in enumerate(chips)]
            for cp in first:
                cp.start()
            waits.append((copy, mine, first))
        for copy, mine, first in waits:
            passed = [copy(3 + j, (*chip, c), sibling) for j, chip in enumerate(chips)]
            for j, chip in enumerate(chips):
                copy(j, (*chip, c), (x, y, c)).wait_recv()
                passed[j].start()
            for j, chip in enumerate(chips):
                copy(3 + j, (*chip, 1 - c), (x, y, c)).wait_recv()
            for cp in first + passed:
                cp.wait_send()
            mine.wait()

    return pl.pallas_call(
        body, name="gather_weights", in_specs=[ANY] * n, out_specs=[ANY] * n,
        out_shape=[jax.ShapeDtypeStruct((N_CHIPS,) + a.shape, a.dtype) for a in arrs],
        scratch_shapes=[pltpu.SemaphoreType.DMA((6 * n,)), pltpu.SemaphoreType.DMA((6 * n,)),
                        pltpu.SemaphoreType.DMA((n,))],
    )(*arrs)


def _swap_cores(name, parts):
    n = len(parts)

    def body(*refs):
        p_refs, r_refs = refs[:n], refs[n:2 * n]
        send_sems, recv_sems = refs[2 * n:]
        x, y, c, _ = _place()
        copies = [pltpu.make_async_remote_copy(
            src_ref=p_refs[w], dst_ref=r_refs[w], send_sem=send_sems.at[w], recv_sem=recv_sems.at[w],
            device_id=(x, y, 1 - c), device_id_type=MESH) for w in range(n)]
        for cp in copies:
            cp.start()
        for cp in copies:
            cp.wait()

    return pl.pallas_call(
        body, name=name, in_specs=[ANY] * n, out_specs=[ANY] * n,
        out_shape=[jax.ShapeDtypeStruct(p.shape, p.dtype) for p in parts],
        scratch_shapes=[pltpu.SemaphoreType.DMA((n,)), pltpu.SemaphoreType.DMA((n,))],
    )(*parts)


HBM = pl.BlockSpec(memory_space=pltpu.HBM)
SEM = pl.BlockSpec(memory_space=pltpu.SEMAPHORE)
EFFECT = pltpu.SideEffectType.DATAFLOW_SIDE_EFFECTING


def _push_copies(a_refs, l_refs, send_sems, recv_sems, by_target):
    x, y, c, chips = _place()
    me = 2 * x + y

    def part(a_ref, q):
        if by_target == 'cols':
            n = a_ref.shape[-1] // N_CHIPS
            return a_ref.at[(slice(None),) * (len(a_ref.shape) - 1) + (pl.ds(pl.multiple_of(q * n, LANES), n),)]
        return a_ref.at[q] if by_target else a_ref

    out = []
    for w, (a_ref, l_ref) in enumerate(zip(a_refs, l_refs)):
        for j, (px, py) in enumerate(chips):
            peer = 2 * px + py
            out.append((
                pltpu.make_async_remote_copy(
                    src_ref=part(a_ref, peer), dst_ref=l_ref.at[me],
                    send_sem=send_sems.at[3 * w + j], recv_sem=recv_sems.at[3 * w + j],
                    device_id=(px, py, c), device_id_type=MESH),
                pltpu.make_async_remote_copy(
                    src_ref=part(a_ref, me), dst_ref=l_ref.at[peer],
                    send_sem=send_sems.at[3 * w + j], recv_sem=recv_sems.at[3 * w + j],
                    device_id=(px, py, c), device_id_type=MESH)))
    return out


def _landing_shape(a, by_target):
    if by_target == 'cols':
        return (N_CHIPS,) + a.shape[:-1] + (a.shape[-1] // N_CHIPS,)
    return (N_CHIPS,) + (a.shape[1:] if by_target else a.shape)


def _push_start(name, arrs, by_target):
    n = len(arrs)
    lands = [lax.empty(_landing_shape(a, by_target), a.dtype) for a in arrs]

    def body(*refs):
        a_refs, l_refs = refs[:n], refs[n:2 * n]
        send_sems, recv_sems = refs[2 * n], refs[2 * n + 1]
        token = refs[-1]
        for send, _ in _push_copies(a_refs, l_refs, send_sems, recv_sems, by_target):
            send.start()
        token[...] = jnp.zeros_like(token)

    res = pl.pallas_call(
        body, name=name,
        out_shape=(pltpu.SemaphoreType.DMA((3 * n,)), pltpu.SemaphoreType.DMA((3 * n,)),
                   *[pltpu.HBM(a.shape, a.dtype) for a in arrs], *[pltpu.HBM(l.shape, l.dtype) for l in lands],
                   jax.ShapeDtypeStruct((8, LANES), F32)),
        in_specs=[HBM] * (2 * n), out_specs=(SEM, SEM, *[HBM] * (2 * n), pl.BlockSpec(memory_space=pltpu.VMEM)),
        input_output_aliases={i: 2 + i for i in range(2 * n)},
        compiler_params=pltpu.CompilerParams(has_side_effects=EFFECT),
    )(*[pltpu.with_memory_space_constraint(a, pltpu.HBM) for a in list(arrs) + lands])
    return res[0], res[1], list(res[2:2 + n]), list(res[2 + n:2 + 2 * n]), res[-1]


def _push_wait(name, send_sems, recv_sems, arrs, lands, after, by_target):
    n = len(arrs)

    def body(*refs):
        a_refs, l_refs = refs[:n], refs[n:2 * n]
        s_sems, r_sems = refs[2 * n], refs[2 * n + 1]
        for send, recv in _push_copies(a_refs, l_refs, s_sems, r_sems, by_target):
            send.wait_send()
            recv.wait_recv()

    res = pl.pallas_call(
        body, name=name,
        out_shape=[pltpu.HBM(a.shape, a.dtype) for a in list(arrs) + list(lands)],
        in_specs=[HBM] * (2 * n) + [SEM, SEM] + [ANY] * len(after), out_specs=[HBM] * (2 * n),
        input_output_aliases={i: i for i in range(2 * n)},
        compiler_params=pltpu.CompilerParams(has_side_effects=EFFECT),
    )(*arrs, *lands, send_sems, recv_sems, *after)
    return list(res[:n]), list(res[n:])


def _all_reduce_small(part):
    def body(p_ref, out_ref, sib_buf, chip_buf, send_sems, recv_sems):
        x, y, c, chips = _place()
        me = 2 * x + y
        swap = pltpu.make_async_remote_copy(
            src_ref=p_ref, dst_ref=sib_buf, send_sem=send_sems.at[0], recv_sem=recv_sems.at[0],
            device_id=(x, y, 1 - c), device_id_type=MESH)
        swap.start()
        swap.wait()
        chip_buf[me] = p_ref[...] + sib_buf[...]
        copies = []
        for j, (px, py) in enumerate(chips):
            cp = pltpu.make_async_remote_copy(
                src_ref=chip_buf.at[me], dst_ref=chip_buf.at[me], send_sem=send_sems.at[1 + j],
                recv_sem=recv_sems.at[1 + j], device_id=(px, py, c), device_id_type=MESH)
            cp.start()
            copies.append(cp)
        for j, (px, py) in enumerate(chips):
            pltpu.make_async_remote_copy(
                src_ref=chip_buf.at[me], dst_ref=chip_buf.at[2 * px + py], send_sem=send_sems.at[1 + j],
                recv_sem=recv_sems.at[1 + j], device_id=(px, py, c), device_id_type=MESH).wait_recv()
        for cp in copies:
            cp.wait_send()
        tot = chip_buf[0]
        for q in range(1, N_CHIPS):
            tot = tot + chip_buf[q]
        out_ref[...] = tot

    vm = pl.BlockSpec(memory_space=pltpu.VMEM)
    return pl.pallas_call(
        body, name="all_reduce_small", in_specs=[vm], out_specs=vm,
        out_shape=jax.ShapeDtypeStruct(part.shape, F32),
        scratch_shapes=[pltpu.VMEM(part.shape, F32), pltpu.VMEM((N_CHIPS,) + part.shape, F32),
                        pltpu.SemaphoreType.DMA((N_CHIPS,)), pltpu.SemaphoreType.DMA((N_CHIPS,))],
    )(part)


def _row_tiles(shape):
    ax = next(d for d, s in enumerate(shape) if s > 1)
    tr = _gcd(ADAM_ROWS, shape[ax])
    block = tuple(tr if d == ax else s for d, s in enumerate(shape))
    return shape[ax] // tr, block, lambda i: tuple(i if d == ax else 0 for d in range(len(shape)))


def _sum_chips(name, landed, sent, chip, by_target):
    shape = landed.shape[1:]
    steps, block, index = _row_tiles(shape)
    if by_target == 'cols':
        own_spec = pl.BlockSpec(block, lambda i, c: index(i)[:-1] + (c[0],))
    else:
        own_spec = pl.BlockSpec((1,) + block, lambda i, c: (c[0],) + index(i))

    def body(chip_ref, l_ref, s_ref, o_ref):
        own = (s_ref[...] if by_target == 'cols' else s_ref[0]).astype(F32)
        tot = None
        for q in range(N_CHIPS):
            term = jnp.where(chip_ref[0] == q, own, l_ref[q].astype(F32))
            tot = term if tot is None else tot + term
        o_ref[...] = tot

    return pl.pallas_call(
        body, name=name,
        grid_spec=pltpu.PrefetchScalarGridSpec(
            num_scalar_prefetch=1, grid=(steps,),
            in_specs=[pl.BlockSpec((N_CHIPS,) + block, lambda i, c: (0,) + index(i)), own_spec],
            out_specs=pl.BlockSpec(block, lambda i, c: index(i))),
        out_shape=jax.ShapeDtypeStruct(shape, F32), compiler_params=_params(1),
    )(chip, landed, sent)


def _adamw_math(g, w, m, v):
    mn = ADAM_B1 * m + (1.0 - ADAM_B1) * g
    vn = ADAM_B2 * v + (1.0 - ADAM_B2) * jnp.square(g)
    m_hat = mn / (1.0 - ADAM_B1 ** ADAM_STEP)
    v_hat = vn / (1.0 - ADAM_B2 ** ADAM_STEP)
    return -ADAM_LR * (m_hat / (jnp.sqrt(v_hat) + ADAM_EPS) + ADAM_WD * w), mn, vn


def _adamw(name, g_parts, w, m, v):
    steps, block, index = _row_tiles(w.shape)
    n = len(g_parts)

    def body(*refs):
        g = refs[0][...]
        for r in refs[1:n]:
            g = g + r[...]
        w_ref, m_ref, v_ref, go_ref, d_ref, mo_ref, vo_ref = refs[n:]
        go_ref[...] = g
        d_ref[...], mo_ref[...], vo_ref[...] = _adamw_math(g, w_ref[...], m_ref[...], v_ref[...])

    spec = pl.BlockSpec(block, index)
    return pl.pallas_call(
        body, name=name, grid=(steps,), in_specs=[spec] * (n + 3), out_specs=[spec] * 4,
        out_shape=[jax.ShapeDtypeStruct(w.shape, F32)] * 4, compiler_params=_params(1),
    )(*g_parts, w, m, v)


def _adamw_vectors(items):
    n = len(items)

    def body(*refs):
        ins, outs = refs[:4 * n], refs[4 * n:]
        for k in range(n):
            g, w, m, v = (r[...] for r in ins[4 * k:4 * k + 4])
            outs[3 * k][...], outs[3 * k + 1][...], outs[3 * k + 2][...] = _adamw_math(g, w, m, v)

    vm = pl.BlockSpec(memory_space=pltpu.VMEM)
    res = pl.pallas_call(
        body, name="adamw_vectors", in_specs=[vm] * (4 * n), out_specs=[vm] * (3 * n),
        out_shape=[jax.ShapeDtypeStruct(it[1].shape, F32) for it in items for _ in range(3)],
    )(*[a for it in items for a in it])
    return [res[3 * k:3 * k + 3] for k in range(n)]


def _pack_rows(flat, dtype, multiple):
    n = flat.shape[0]
    total = -(-n // multiple) * multiple
    return jnp.pad(flat, (0, total - n)).astype(dtype).reshape(total // LANES, LANES)


def kernel(x, positions, ln_g, ln_b, a_w_in, a_b_in, a_conv_w, a_conv_b, a_norm_g, a_norm_b, a_w_out, a_b_out, kv_w_down, kv_norm_g, kv_w_uk, kv_w_uv, b_w_in, b_q_norm_g, b_w_uq, b_w_out, loss_target, m_ln_g, m_ln_b, m_a_w_in, m_a_b_in, m_a_conv_w, m_a_conv_b, m_a_norm_g, m_a_norm_b, m_a_w_out, m_a_b_out, m_kv_w_down, m_kv_norm_g, m_kv_w_uk, m_kv_w_uv, m_b_w_in, m_b_q_norm_g, m_b_w_uq, m_b_w_out, v_ln_g, v_ln_b, v_a_w_in, v_a_b_in, v_a_conv_w, v_a_conv_b, v_a_norm_g, v_a_norm_b, v_a_w_out, v_a_b_out, v_kv_w_down, v_kv_norm_g, v_kv_w_uk, v_kv_w_uv, v_b_w_in, v_b_q_norm_g, v_b_w_uq, v_b_w_out):
    T, D = x.shape[1], x.shape[2]
    E = N_CHIPS * a_w_out.shape[1]
    KC = a_conv_w.shape[1]
    RKV = kv_norm_g.shape[0]
    H, DN = kv_w_uk.shape[1], kv_w_uk.shape[2]
    RQ = b_q_norm_g.shape[1]
    HV = N_CHIPS * b_w_out.shape[1]
    assert DN == LANES and kv_w_uv.shape[2] == LANES and HV == H * LANES
    assert kv_w_down.shape[1] == RKV + ROPE_DIM and b_w_uq.shape[3] == DN + ROPE_DIM
    assert ln_g.shape[0] == 2 and a_w_in.shape[0] == 1 and b_w_in.shape[0] == 1
    alpha = (2.0 * ln_g.shape[0]) ** 0.25
    scale = 1.0 / math.sqrt(DN + ROPE_DIM)
    WK = -(-(RKV + LANES) // 256) * 256
    assert WK % RQ == 0
    Z_OFF = WK + RQ
    tmw, tq = min(TM_WIDE, T), min(TQ, T)
    t512, t1024 = _fit(512, T), _fit(1024, T)
    xs = x[0]
    tgt = loss_target[0]
    px, py = lax.axis_index("x"), lax.axis_index("y")
    chip = 2 * px + py

    mats = [a_w_out[0], kv_w_down, kv_w_uk, kv_w_uv, b_w_in[0], b_w_uq[0], b_w_out[0]]
    vecs = [a_b_in[0], a_conv_w[0], a_conv_b[0], a_norm_g[0], a_norm_b[0], a_b_out[0]]
    vec_bits = jnp.concatenate([lax.bitcast_convert_type(w.reshape(-1), BF16).reshape(-1) for w in vecs])
    rest = [w.astype(BF16) for w in mats]
    g_win, gathered = _all_gather_chips(
        [a_w_in[0].astype(BF16), _pack_rows(vec_bits, BF16, 2 * BF16_ROWS * LANES)])
    gathered = gathered.reshape(N_CHIPS, -1)
    gathered, rest = lax.optimization_barrier((gathered, rest))
    rest_sems = _push_start("gather_rest_start", rest, by_target=False)
    off = 0
    fvec = []
    for w in vecs:
        bits = gathered[:, off:off + 2 * w.size].reshape((N_CHIPS,) + w.shape + (2,))
        fvec.append(lax.bitcast_convert_type(bits, F32))
        off += 2 * w.size
    cols = lambda g: jnp.moveaxis(g, 0, -2).reshape(g.shape[1:-1] + (N_CHIPS * g.shape[-1],))
    b_in = cols(fvec[0][:, None, :])
    conv_w = cols(fvec[1])
    conv_b, norm_g, norm_b, b_out = (cols(f[:, None, :]) for f in fvec[2:])
    row = lambda a: a.reshape(1, -1)
    g0, b0, g1, b1 = row(ln_g[0]), row(ln_b[0]), row(ln_g[1]), row(ln_b[1])
    kv_g, q_g = row(kv_norm_g), row(b_q_norm_g[0])
    plain = lambda acc, ins, i, j: [acc]

    b_in = b_in + rest_sems[4][0, 0]
    (proj,) = _row_mm("a_in", [((xs,), None)], g_win, nt=False, tm=t1024, tn=3 * E // N_CHIPS, tk=D,
                      outs=[((T, 3 * E), F32, 'tile')], epi=lambda acc, ins, i, j: [acc + ins[0]],
                      epi_ins=[(b_in, 'col')])
    u1 = _conv_fwd(proj, conv_w, conv_b, E, tmw)
    u4 = _conv_post(u1, proj, norm_g, norm_b, E, tmw)

    rest, landed = _push_wait("gather_rest_wait", *rest_sems[:4], after=[u4], by_target=False)
    g_wout, g_wd, g_uk, g_uv, g_wbin, g_wuq, g_wbout = [
        lax.dynamic_update_slice(l, w[None], (chip,) + (0,) * w.ndim) for w, l in zip(rest, landed)]
    w_out = g_wout.reshape(E, D)
    wd = g_wd.reshape(D, RKV + ROPE_DIM)
    zpad = jnp.zeros((D, ROPE_HALF), BF16)
    wd_p = jnp.concatenate(
        [wd[:, :RKV], wd[:, RKV:RKV + ROPE_HALF], zpad, wd[:, RKV + ROPE_HALF:], zpad,
         jnp.zeros((D, WK - RKV - LANES), BF16)], axis=1)
    w_bin = cols(g_wbin)
    w_z = w_bin[:, RQ:]
    wb_small = jnp.concatenate([wd_p, w_bin[:, :RQ]], axis=1)
    wb_all = jnp.concatenate([wd_p, w_bin], axis=1)
    w_kv = jnp.concatenate([g_uk.reshape(RKV, HV), g_uv.reshape(RKV, HV)], axis=1)
    wuq = g_wuq.reshape(RQ, H, DN + ROPE_DIM)
    zq = jnp.zeros((RQ, H, ROPE_HALF), BF16)
    w_qr = jnp.concatenate([wuq[:, :, DN:DN + ROPE_HALF], zq, wuq[:, :, DN + ROPE_HALF:], zq], axis=2)
    w_q = jnp.concatenate([wuq[:, :, :DN].reshape(RQ, HV), w_qr.reshape(RQ, HV)], axis=1)
    w_bout = g_wbout.reshape(HV, D)

    freqs = ROPE_THETA ** (-jnp.arange(0, ROPE_DIM, 2, dtype=F32) / ROPE_DIM)
    ang = positions[0].astype(F32)[:, None] * freqs
    cs, sn = jnp.cos(ang), jnp.sin(ang)
    ones, zeros = jnp.ones_like(cs), jnp.zeros_like(cs)
    cos_t = jnp.concatenate([cs, ones, cs, ones], axis=1)
    sin_t = jnp.concatenate([-sn, zeros, sn, zeros], axis=1)

    def ln_epi(acc, ins, i, j):
        bias, res, g, b = ins
        xhat, rstd = _ln_stats(alpha * res + acc + bias)
        h = xhat * g + b
        return [h, h, xhat, rstd]

    h1, h1b, xhat1, rstd1 = _row_mm(
        "a_out", [((u4,), None)], w_out, nt=False, tm=t512, tn=D, tk=_fit(2048, E),
        outs=[((T, D), F32, 'tile'), ((T, D), BF16, 'tile'), ((T, D), F32, 'tile'), ((T, 1), F32, 'row')],
        epi=ln_epi, epi_ins=[(b_out, 'col'), (xs, 'tile'), (g0, 'col'), (b0, 'col')])

    tkb = _fit(512, _gcd(WK, RQ, HV))
    (pb,) = _row_mm("b_in", [((h1b,), None)], wb_small, nt=False, tm=t1024, tn=_fit(1024, Z_OFF),
                    tk=_fit(1024, D), outs=[((T, Z_OFF), F32, 'tile')], epi=plain)
    (zb,) = _row_mm("b_in_gate", [((h1b,), None)], w_z, nt=False, tm=t1024, tn=_fit(2048, HV),
                    tk=_fit(1024, D), outs=[((T, HV), F32, 'tile')], epi=plain)
    c_lat, kr, cqn = _norm_prep(pb, kv_g, q_g, cos_t, sin_t, RKV, RQ, WK, tmw)
    (kv,) = _row_mm("kv_up", [((c_lat,), None)], w_kv, nt=False, tm=t1024, tn=_fit(2048, HV),
                    tk=_fit(1024, RKV), outs=[((T, 2 * HV), BF16, 'tile')], epi=plain)
    tnq = _fit(2048, HV)
    half_q = HV // tnq

    def q_epi(acc, ins, i, j):
        return [jnp.where(j >= half_q, _rope(acc, ins[0], ins[1]), acc)]

    (q_all,) = _row_mm("q_up", [((cqn,), None)], w_q, nt=False, tm=t1024, tn=tnq, tk=_fit(1024, RQ),
                       outs=[((T, 2 * HV), BF16, 'tile')], epi=q_epi,
                       epi_ins=[(cos_t, 'row'), (sin_t, 'row')])
    o, lse = _attn_fwd(q_all, kv, kr, H, tq, scale)

    def loss_epi(acc, ins, i, j):
        res, g, b, target = ins
        xhat, rstd = _ln_stats(alpha * res + acc)
        diff = xhat * g + b - target
        dr, dg, db = _ln_bwd(diff / D, xhat, rstd, g)
        return [dr, 0.5 * jnp.sum(diff * diff, keepdims=True) / D, dg, db]

    dr1, loss_part, dg1, db1 = _row_mm(
        "b_out", [((o, zb), _gate)], w_bout, nt=False, tm=tmw, tn=D, tk=_fit(2048, HV),
        outs=[((T, D), F32, 'tile'), ((1, 1), F32, 'acc'), ((1, D), F32, 'acc'), ((1, D), F32, 'acc')],
        epi=loss_epi, epi_ins=[(h1, 'tile'), (g1, 'col'), (b1, 'col'), (tgt, 'tile')])

    def gate_bwd_epi(acc, ins, i, j):
        return [acc * _silu(ins[1]), acc * ins[0] * _silu_grad(ins[1])]

    do, dz = _row_mm(
        "b_out_bwd", [((dr1,), None)], w_bout, nt=True, tm=tmw, tn=_fit(2048, HV), tk=_fit(1024, D),
        outs=[((T, HV), BF16, 'tile'), ((T, HV), BF16, 'tile')], epi=gate_bwd_epi,
        epi_ins=[(o, 'tile'), (zb, 'tile')])
    gw_bout = _tn_mm("dw_b_out", (o, zb), _gate, [((dr1,), None)], tn=_fit(1024, D), tk=t512, out_dtype=BF16)
    dqn, dqr_pre, dkn, dkr_h, dv = _attn_bwd(q_all, kv, kr, do, o, lse, cos_t, sin_t, H, tq, scale)
    dkr_pre = _key_rope_bwd(dkr_h, cos_t, sin_t, H, tmw)

    def cq_bwd_epi(acc, ins, i, j):
        dx, dg = _rms_bwd(acc, ins[0], ins[1])
        return [dx, dg]

    dcq, dqg = _row_mm(
        "q_up_bwd", [((dqn,), None), ((dqr_pre,), None)], w_q, nt=True, tm=t1024, tn=RQ, tk=_fit(2048, HV),
        outs=[((T, RQ), BF16, 'tile'), ((1, RQ), F32, 'acc')], epi=cq_bwd_epi,
        epi_ins=[(pb, pl.BlockSpec((t1024, RQ), lambda i, j, k: (i, WK // RQ))), (q_g, 'col')])
    gw_q = _tn_mm("dw_q_up", (cqn,), None, [((dqn,), None), ((dqr_pre,), None)],
                  tn=_fit(2048, HV), tk=t1024, out_dtype=BF16)

    def ckv_bwd_epi(acc, ins, i, j):
        blk, dkr_t, g = ins
        dx, dg = _rms_bwd(acc, blk[:, :RKV], g)
        parts = [dx, dkr_t]
        if WK > RKV + LANES:
            parts.append(jnp.zeros((dx.shape[0], WK - RKV - LANES), F32))
        return [jnp.concatenate(parts, axis=1), dg]

    dckv, dkvg = _row_mm(
        "kv_up_bwd", [((dkn,), None), ((dv,), None)], w_kv, nt=True, tm=t1024, tn=RKV, tk=_fit(2048, HV),
        outs=[((T, WK), BF16, pl.BlockSpec((t1024, WK), lambda i, j, k: (i, 0))), ((1, RKV), F32, 'acc')],
        epi=ckv_bwd_epi,
        epi_ins=[(pb, pl.BlockSpec((t1024, WK), lambda i, j, k: (i, 0))), (dkr_pre, 'row'), (kv_g, 'col')])
    gw_kv = _tn_mm("dw_kv_up", (c_lat,), None, [((dkn,), None), ((dv,), None)],
                   tn=_fit(2048, HV), tk=t1024, out_dtype=BF16)

    def ln1_bwd_epi(acc, ins, i, j):
        dr_up, xhat, rstd, g = ins
        dr, dg, db = _ln_bwd(alpha * dr_up + acc, xhat, rstd, g)
        return [dr, dg, db]

    dp_segs = [((dckv,), None), ((dcq,), None), ((dz,), None)]
    gw_ball = _tn_mm("dw_b_in", (h1b,), None, dp_segs, tn=tkb, tk=t1024, out_dtype=BF16)

    shard_cols = lambda g: jnp.moveaxis(g.reshape(g.shape[0], N_CHIPS, -1), 1, 0)
    shard_rows = lambda g: g.reshape(N_CHIPS, g.shape[0] // N_CHIPS, g.shape[1])
    gq = gw_q.reshape(RQ, 2, H, LANES)
    g_uq = jnp.concatenate(
        [gq[:, 0], gq[:, 1, :, :ROPE_HALF], gq[:, 1, :, 2 * ROPE_HALF:3 * ROPE_HALF]], axis=2)
    g_wd_full = jnp.concatenate(
        [gw_ball[:, :RKV], gw_ball[:, RKV:RKV + ROPE_HALF],
         gw_ball[:, RKV + 2 * ROPE_HALF:RKV + 3 * ROPE_HALF]], axis=1)
    late_names = ["kv_w_down", "kv_w_uk", "kv_w_uv", "b_w_in", "b_w_uq", "b_w_out"]
    late_w = [kv_w_down, kv_w_uk, kv_w_uv, b_w_in, b_w_uq, b_w_out]
    chip_major = [g_wd_full, gw_kv[:, :HV], gw_kv[:, HV:], shard_cols(gw_ball[:, WK:]), g_uq, gw_bout]
    late_grads = [g.reshape((N_CHIPS,) + w.shape) for g, w in zip(chip_major, late_w)]
    late_sems = _push_start("scatter_late_start", late_grads, by_target=True)

    dr0, dg0, db0 = _row_mm(
        "b_in_bwd", dp_segs, wb_all, nt=True, tm=t1024, tn=D, tk=tkb,
        outs=[((T, D), F32, 'tile'), ((1, D), F32, 'acc'), ((1, D), F32, 'acc')], epi=ln1_bwd_epi,
        epi_ins=[(dr1, 'tile'), (xhat1, 'tile'), (rstd1, 'row'), (g0 + late_sems[4][0, 0], 'col')])

    def conv_branch_bwd_epi(acc, ins, i, j):
        u1_t, z, g, b = ins
        xhat, rstd = _ln_stats(u1_t)
        u2 = xhat * g + b
        du3 = acc * _silu(z)
        dz_a = acc * _silu(u2) * _silu_grad(z)
        du1, dg, db = _ln_bwd(du3 * _silu_grad(u2), xhat, rstd, g)
        return [du1, dz_a, dg, db]

    du1, dz_a, dng, dnb = _row_mm(
        "a_out_bwd", [((dr0,), None)], w_out, nt=True, tm=tmw, tn=E, tk=_fit(1024, D),
        outs=[((T, E), F32, 'tile'), ((T, E), BF16, 'tile'), ((1, E), F32, 'acc'), ((1, E), F32, 'acc')],
        epi=conv_branch_bwd_epi,
        epi_ins=[(u1, 'tile'), (proj, pl.BlockSpec((tmw, E), lambda i, j, k: (i, 2))), (norm_g, 'col'),
                 (norm_b, 'col')])
    gw_out, dbo = _tn_mm("dw_a_out", (u4,), None, [((dr0,), None)], tn=_fit(1024, D), tk=t1024,
                         out_dtype=BF16, colsum=True)
    mid_sems = _push_start("scatter_mid_start", [gw_out.reshape((N_CHIPS,) + a_w_out.shape)], by_target=True)
    dval, dgate, dcw, dcb = _conv_bwd(du1, proj, conv_w + mid_sems[4][0, 0], E, tmw)
    dproj_segs = [((dval,), None), ((dgate,), None), ((dz_a,), None)]
    gw_in, dbi = _tn_mm("dw_a_in", (xs,), None, dproj_segs, tn=_fit(2048, E), tk=t512, out_dtype=BF16,
                        colsum=True)

    chip_word = chip.reshape(1).astype(jnp.int32)

    def reduce_and_update(tag, names_, sent, landed, w_, m_, v_, by_target=True):
        sums = [_sum_chips("sum_" + n, l, s, chip_word, by_target) for n, s, l in zip(names_, sent, landed)]
        theirs = _swap_cores("swap_cores_" + tag, sums)
        return {n: _adamw("adamw_" + n, [mine, other], w, m, v)
                for n, mine, other, w, m, v in zip(names_, sums, theirs, w_, m_, v_)}

    late_sent, late_landed = _push_wait("scatter_late_wait", *late_sems[:4], after=[dbi], by_target=True)
    mid_sent, mid_landed = _push_wait("scatter_mid_wait", *mid_sems[:4], after=[dbi], by_target=True)
    early_sems = _push_start("scatter_early_start", [gw_in.reshape(a_w_in.shape[:-1] + (3 * E,))],
                             by_target='cols')
    (grad_x,) = _row_mm(
        "a_in_bwd", dproj_segs, g_win, nt=True, tm=tmw, tn=D, tk=E, b_whole=True,
        outs=[((T, D), F32, 'tile')], epi=lambda acc, ins, i, j: [alpha * ins[0] + acc + ins[1]],
        epi_ins=[(dr0, 'tile'), (jnp.zeros((1, D), F32) + early_sems[4][0, 0], 'col')])
    big_out = reduce_and_update(
        "late", ["a_w_out"] + late_names, mid_sent + late_sent, mid_landed + late_landed,
        [a_w_out] + late_w, [m_a_w_out, m_kv_w_down, m_kv_w_uk, m_kv_w_uv, m_b_w_in, m_b_w_uq, m_b_w_out],
        [v_a_w_out, v_kv_w_down, v_kv_w_uk, v_kv_w_uv, v_b_w_in, v_b_w_uq, v_b_w_out])

    small_full = [jnp.concatenate([dg0, dg1]), jnp.concatenate([db0, db1]), dbi, dcw, dcb, dng, dnb, dbo,
                  dkvg, dqg]
    sflat = jnp.concatenate([g.reshape(-1) for g in small_full])
    summed = _all_reduce_small(_pack_rows(sflat, F32, 8 * LANES)).reshape(-1)
    soff = 0
    sgrads = []
    for g in small_full:
        sgrads.append(summed[soff:soff + g.size].reshape(g.shape))
        soff += g.size
    local_cols = lambda g, n: lax.dynamic_slice_in_dim(g, chip * n, n, axis=g.ndim - 1)
    snames = ["ln_g", "ln_b", "a_b_in", "a_conv_w", "a_conv_b", "a_norm_g", "a_norm_b", "a_b_out",
              "kv_norm_g", "b_q_norm_g"]
    small_w = [ln_g, ln_b, a_b_in, a_conv_w, a_conv_b, a_norm_g, a_norm_b, a_b_out, kv_norm_g, b_q_norm_g]
    small_m = [m_ln_g, m_ln_b, m_a_b_in, m_a_conv_w, m_a_conv_b, m_a_norm_g, m_a_norm_b, m_a_b_out,
               m_kv_norm_g, m_b_q_norm_g]
    small_v = [v_ln_g, v_ln_b, v_a_b_in, v_a_conv_w, v_a_conv_b, v_a_norm_g, v_a_norm_b, v_a_b_out,
               v_kv_norm_g, v_b_q_norm_g]
    sharded = {"a_b_in", "a_conv_w", "a_conv_b", "a_norm_g", "a_norm_b", "a_b_out"}
    local_g = [(local_cols(g, w.shape[-1]) if n in sharded else g).reshape(w.shape)
               for n, g, w in zip(snames, sgrads, small_w)]
    at_least_2d = lambda a: a.reshape((1,) + a.shape) if a.ndim == 1 else a
    sres = _adamw_vectors([tuple(at_least_2d(a) for a in item)
                           for item in zip(local_g, small_w, small_m, small_v)])
    small_out = {n: [g] + [r.reshape(w.shape) for r in res]
                 for n, g, w, res in zip(snames, local_g, small_w, sres)}

    early_sent, early_landed = _push_wait(
        "scatter_early_wait", *early_sems[:4], by_target='cols',
        after=[grad_x, big_out["b_w_out"][1], small_out["b_q_norm_g"][1]])
    big_out.update(reduce_and_update("early", ["a_w_in"], early_sent, early_landed,
                                     [a_w_in], [m_a_w_in], [v_a_w_in], by_target='cols'))

    loss = lax.psum(loss_part[0, 0], ("x", "y", "c"))
    order = ["ln_g", "ln_b", "a_w_in", "a_b_in", "a_conv_w", "a_conv_b", "a_norm_g", "a_norm_b", "a_w_out",
             "a_b_out", "kv_w_down", "kv_norm_g", "kv_w_uk", "kv_w_uv", "b_w_in", "b_q_norm_g", "b_w_uq",
             "b_w_out"]
    outs = {**big_out, **small_out}
    result = [loss, grad_x[None]]
    for part in range(4):
        result += [outs[n][part] for n in order]
    return tuple(result)
```

```python
import functools
import math

import jax
import jax.numpy as jnp
from jax import lax
from jax.experimental import pallas as pl
from jax.experimental.pallas import tpu as pltpu

F32, BF16 = jnp.float32, jnp.bfloat16
NN = (((1,), (0,)), ((), ()))
NT = (((1,), (1,)), ((), ()))
TN = (((0,), (0,)), ((), ()))
MESH = pl.DeviceIdType.MESH
ANY = pl.BlockSpec(memory_space=pl.ANY)

LANES = 128
BF16_ROWS = 16
VMEM_LIMIT = 56 * 1024 * 1024
N_CHIPS = 4
LN_EPS = 1e-5
RMS_EPS = 1e-6
MASK_VALUE = -1e30
LOG2_E = math.log2(math.e)
ROPE_THETA = 10000.0
ROPE_DIM = 64
ROPE_HALF = ROPE_DIM // 2
ADAM_LR, ADAM_B1, ADAM_B2, ADAM_EPS, ADAM_WD, ADAM_STEP = 0.001, 0.9, 0.999, 1e-08, 0.01, 10

MAX_TILE = 2048
TM_WIDE = 256
TQ = 512
CONV_HALO = 32
CONV_LC = 512
CONV_SUB = 256
SUBLANES = 8
CONV_RC = 32
ADAM_ROWS = 64


def _dot(a, b, dims):
    return lax.dot_general(a.astype(BF16), b.astype(BF16), dims, preferred_element_type=F32)


def _sig(x):
    return 0.5 * jnp.tanh(0.5 * x) + 0.5


def _params(n_axes):
    return pltpu.CompilerParams(dimension_semantics=("arbitrary",) * n_axes, vmem_limit_bytes=VMEM_LIMIT)


def _gcd(*v):
    return functools.reduce(math.gcd, v)


def _fit(want, dim):
    return math.gcd(min(want, MAX_TILE), dim)


def _row_mm(name, a_segs, b, *, nt, tm, tn, tk, outs, epi, epi_ins=(), b_whole=False):
    M = a_segs[0][0][0].shape[0]
    stacked = b.ndim == 3
    if stacked:
        n_blk = b.shape[2]
        N = b.shape[1] if nt else N_CHIPS * n_blk
        assert (nt and b_whole) or (not nt and tn == n_blk and tk == b.shape[1]), name
    else:
        N = b.shape[0] if nt else b.shape[1]
    nkb = [arrs[0].shape[1] // tk for arrs, _ in a_segs]
    koff = [sum(nkb[:s]) for s in range(len(nkb))]
    ni, nj, nk = M // tm, N // tn, sum(nkb)
    assert M % tm == 0 and N % tn == 0 and all(arrs[0].shape[1] % tk == 0 for arrs, _ in a_segs), name
    assert stacked or (b.shape[1] if nt else b.shape[0]) == nk * tk, name

    def spec_of(shape, kind):
        if isinstance(kind, pl.BlockSpec):
            return kind
        if kind == 'tile':
            return pl.BlockSpec((tm, tn), lambda i, j, k: (i, j))
        if kind == 'row':
            return pl.BlockSpec((tm, shape[1]), lambda i, j, k: (i, 0))
        if kind == 'col':
            return pl.BlockSpec((1, tn), lambda i, j, k: (0, j))
        assert kind == 'acc' and nj == 1, name
        return pl.BlockSpec(shape, lambda i, j, k: (0,) * len(shape))

    in_specs, operands = [], []
    for s, (arrs, _) in enumerate(a_segs):
        for arr in arrs:
            in_specs.append(pl.BlockSpec(
                (tm, tk), lambda i, j, k, s=s: (i, jnp.clip(k - koff[s], 0, nkb[s] - 1))))
            operands.append(arr)
    if b_whole:
        assert nt and nj == 1 and all(n == 1 for n in nkb), name
        in_specs.append(pl.BlockSpec(b.shape, lambda i, j, k: (0,) * b.ndim))
    elif stacked:
        in_specs.append(pl.BlockSpec((1, tk, tn), lambda i, j, k: (j, 0, 0)))
    else:
        in_specs.append(pl.BlockSpec((tn, tk), lambda i, j, k: (j, k)) if nt
                        else pl.BlockSpec((tk, tn), lambda i, j, k: (k, j)))
    operands.append(b)
    for arr, kind in epi_ins:
        in_specs.append(spec_of(arr.shape, kind))
        operands.append(arr)
    out_specs = [spec_of(shape, kind) for shape, _, kind in outs]
    out_shape = [jax.ShapeDtypeStruct(shape, dtype) for shape, dtype, _ in outs]
    n_seg_refs = [len(arrs) for arrs, _ in a_segs]

    def body(*refs):
        pos = 0
        seg_refs = []
        for n in n_seg_refs:
            seg_refs.append(refs[pos:pos + n])
            pos += n
        b_ref = refs[pos]
        e_refs = refs[pos + 1:pos + 1 + len(epi_ins)]
        o_refs = refs[pos + 1 + len(epi_ins):pos + 1 + len(epi_ins) + len(outs)]
        i, j, k = pl.program_id(0), pl.program_id(1), pl.program_id(2)

        def product(fn, rs, s=0):
            a = rs[0][...] if fn is None else fn(*[r[...] for r in rs])
            if b_whole and stacked:
                lo, hi, tot = koff[s] * tk, (koff[s] + 1) * tk, None
                for q in range(lo // n_blk, (hi - 1) // n_blk + 1):
                    c0, c1 = max(lo, q * n_blk), min(hi, (q + 1) * n_blk)
                    part = _dot(a[:, c0 - lo:c1 - lo], b_ref[q, :, c0 - q * n_blk:c1 - q * n_blk], NT)
                    tot = part if tot is None else tot + part
                return tot
            if b_whole:
                return _dot(a, b_ref[:, koff[s] * tk:(koff[s] + 1) * tk], NT)
            return _dot(a, b_ref[0] if stacked else b_ref[...], NT if nt else NN)

        def finish(acc):
            res = epi(acc, [r[...] for r in e_refs], i, j)
            for o_ref, (_, _, kind), r in zip(o_refs, outs, res):
                if isinstance(kind, str) and kind == 'acc':
                    @pl.when(i == 0)
                    def _(o_ref=o_ref, r=r):
                        o_ref[...] = r

                    @pl.when(i > 0)
                    def _(o_ref=o_ref, r=r):
                        o_ref[...] += r
                else:
                    o_ref[...] = r.astype(o_ref.dtype)

        if nk == 1:
            finish(product(a_segs[0][1], seg_refs[0]))
            return
        acc_ref = refs[-1]

        @pl.when(k == 0)
        def _():
            acc_ref[...] = jnp.zeros_like(acc_ref)

        for s, ((_, fn), rs) in enumerate(zip(a_segs, seg_refs)):
            def accumulate(fn=fn, rs=rs, s=s):
                acc_ref[...] += product(fn, rs, s)
            if len(a_segs) == 1:
                accumulate()
            else:
                pl.when(jnp.logical_and(k >= koff[s], k < koff[s] + nkb[s]))(accumulate)

        @pl.when(k == nk - 1)
        def _():
            finish(acc_ref[...])

    return pl.pallas_call(
        body, name=name, grid=(ni, nj, nk), in_specs=in_specs, out_specs=out_specs, out_shape=out_shape,
        scratch_shapes=[] if nk == 1 else [pltpu.VMEM((tm, tn), F32)], compiler_params=_params(3),
    )(*operands)


def _tn_mm(name, a_arrs, a_fn, b_segs, *, tn, tk, out_dtype, shard_major=False, colsum=False):
    T, M = a_arrs[0].shape
    nbj = [arrs[0].shape[1] // tn for arrs, _ in b_segs]
    joff = [sum(nbj[:s]) for s in range(len(nbj))]
    nj, nk = sum(nbj), T // tk
    N = nj * tn
    assert T % tk == 0 and all(arrs[0].shape[1] % tn == 0 for arrs, _ in b_segs), name

    in_specs = [pl.BlockSpec((tk, M), lambda j, k: (k, 0)) for _ in a_arrs]
    operands = list(a_arrs)
    for s, (arrs, _) in enumerate(b_segs):
        for arr in arrs:
            in_specs.append(pl.BlockSpec(
                (tk, tn), lambda j, k, s=s: (k, jnp.clip(j - joff[s], 0, nbj[s] - 1))))
            operands.append(arr)
    if shard_major:
        per = (N // N_CHIPS) // tn
        assert per * tn * N_CHIPS == N, name
        out_shape = [jax.ShapeDtypeStruct((N_CHIPS, M, N // N_CHIPS), out_dtype)]
        out_specs = [pl.BlockSpec((1, M, tn), lambda j, k: (j // per, 0, j % per))]
    else:
        out_shape = [jax.ShapeDtypeStruct((M, N), out_dtype)]
        out_specs = [pl.BlockSpec((M, tn), lambda j, k: (0, j))]
    if colsum:
        out_shape.append(jax.ShapeDtypeStruct((1, N), F32))
        out_specs.append(pl.BlockSpec((1, tn), lambda j, k: (0, j)))
    n_a = len(a_arrs)
    n_seg_refs = [len(arrs) for arrs, _ in b_segs]

    def body(*refs):
        a_refs = refs[:n_a]
        pos = n_a
        seg_refs = []
        for n in n_seg_refs:
            seg_refs.append(refs[pos:pos + n])
            pos += n
        o_ref = refs[pos]
        cs_ref = refs[pos + 1] if colsum else None
        acc_ref = refs[-1]
        j, k = pl.program_id(0), pl.program_id(1)

        @pl.when(k == 0)
        def _():
            acc_ref[...] = jnp.zeros_like(acc_ref)
            if colsum:
                cs_ref[...] = jnp.zeros_like(cs_ref)

        for s, ((_, fn), rs) in enumerate(zip(b_segs, seg_refs)):
            def accumulate(fn=fn, rs=rs):
                a = a_refs[0][...] if a_fn is None else a_fn(*[r[...] for r in a_refs])
                bt = rs[0][...] if fn is None else fn(*[r[...] for r in rs])
                acc_ref[...] += _dot(a, bt, TN)
                if colsum:
                    cs_ref[...] += jnp.sum(bt.astype(F32), axis=0, keepdims=True)
            if len(b_segs) == 1:
                accumulate()
            else:
                pl.when(jnp.logical_and(j >= joff[s], j < joff[s] + nbj[s]))(accumulate)

        @pl.when(k == nk - 1)
        def _():
            if shard_major:
                o_ref[0] = acc_ref[...].astype(o_ref.dtype)
            else:
                o_ref[...] = acc_ref[...].astype(o_ref.dtype)

    res = pl.pallas_call(
        body, name=name, grid=(nj, nk), in_specs=in_specs, out_specs=out_specs, out_shape=out_shape,
        scratch_shapes=[pltpu.VMEM((M, tn), F32)], compiler_params=_params(2),
    )(*operands)
    return res if colsum else res[0]


def _silu(z):
    return z * _sig(z)


def _silu_and_grad(z):
    s = _sig(z)
    return z * s, s * (1.0 + z * (1.0 - s))


def _gate(o, z):
    return o * _silu(z)


def _ln_stats(r):
    mu = jnp.mean(r, axis=1, keepdims=True)
    xc = r - mu
    var = jnp.mean(xc * xc, axis=1, keepdims=True)
    rstd = lax.rsqrt(var + LN_EPS)
    return xc * rstd, rstd


def _ln_bwd(dy, xhat, rstd, g):
    dxh = dy * g
    m1 = jnp.mean(dxh, axis=1, keepdims=True)
    m2 = jnp.mean(dxh * xhat, axis=1, keepdims=True)
    return (rstd * (dxh - m1 - xhat * m2), jnp.sum(dy * xhat, axis=0, keepdims=True),
            jnp.sum(dy, axis=0, keepdims=True))


def _rms_fwd(x, g):
    rstd = lax.rsqrt(jnp.mean(x * x, axis=1, keepdims=True) + RMS_EPS)
    return x * rstd * g


def _rms_bwd(dy, x, g):
    rstd = lax.rsqrt(jnp.mean(x * x, axis=1, keepdims=True) + RMS_EPS)
    xn = x * rstd
    dxn = dy * g
    return rstd * (dxn - xn * jnp.mean(dxn * xn, axis=1, keepdims=True)), jnp.sum(dy * xn, axis=0, keepdims=True)


def _rope(x, cos, sin, transpose=False):
    parts = []
    for g in range(x.shape[1] // LANES):
        xg = x[:, g * LANES:(g + 1) * LANES]
        if transpose:
            parts.append(xg * cos + pltpu.roll(xg * sin, LANES // 2, 1))
        else:
            parts.append(xg * cos + pltpu.roll(xg, LANES // 2, 1) * sin)
    return parts[0] if len(parts) == 1 else jnp.concatenate(parts, axis=1)


def _shifted_rows(window, rc):
    n = window.shape[0]
    for b in range(SUBLANES):
        rolled = window if b == 0 else pltpu.roll(window, n - b, 0)
        for a8 in range(0, n - rc - b + 1, SUBLANES):
            yield a8 + b, rolled[a8:a8 + rc]


def _conv_fwd(proj, conv_w, conv_b, E, tm):
    T = proj.shape[0]
    kc = conv_w.shape[0]
    lc, hb, rc = min(CONV_LC, E), CONV_HALO, min(CONV_RC, tm)
    nl, ni, ratio = E // lc, T // tm, tm // hb
    gate_off = E // lc

    sub = min(CONV_SUB, lc)
    base = hb - (kc - 1)

    def body(val_ref, gate_ref, valh_ref, gateh_ref, w_ref, cb_ref, u1_ref, ubuf):
        i = pl.program_id(1)
        ubuf[hb:, :] = val_ref[...] * _sig(gate_ref[...])
        halo = valh_ref[...] * _sig(gateh_ref[...])
        ubuf[0:hb, :] = jnp.where(i > 0, halo, 0.0)
        for l0 in range(0, lc, sub):
            ls = slice(l0, l0 + sub)
            for r0 in range(0, tm, rc):
                acc = jnp.zeros((rc, sub), F32) + cb_ref[:, ls]
                for off, rows in _shifted_rows(ubuf[r0:r0 + hb + rc, ls], rc):
                    if 0 <= off - base < kc:
                        acc += w_ref[off - base:off - base + 1, ls] * rows
                u1_ref[r0:r0 + rc, ls] = acc

    return pl.pallas_call(
        body, name="conv_fwd", grid=(nl, ni),
        in_specs=[
            pl.BlockSpec((tm, lc), lambda l, i: (i, l)),
            pl.BlockSpec((tm, lc), lambda l, i: (i, gate_off + l)),
            pl.BlockSpec((hb, lc), lambda l, i: (jnp.maximum(i * ratio - 1, 0), l)),
            pl.BlockSpec((hb, lc), lambda l, i: (jnp.maximum(i * ratio - 1, 0), gate_off + l)),
            pl.BlockSpec((kc, lc), lambda l, i: (0, l)),
            pl.BlockSpec((1, lc), lambda l, i: (0, l)),
        ],
        out_specs=pl.BlockSpec((tm, lc), lambda l, i: (i, l)),
        out_shape=jax.ShapeDtypeStruct((T, E), F32),
        scratch_shapes=[pltpu.VMEM((hb + tm, lc), F32)], compiler_params=_params(2),
    )(proj, proj, proj, proj, conv_w, conv_b)


def _conv_post(u1, proj, norm_g, norm_b, E, tm):
    T = u1.shape[0]

    def body(u1_ref, z_ref, g_ref, b_ref, u4_ref):
        xhat, _ = _ln_stats(u1_ref[...])
        u4_ref[...] = (_silu(xhat * g_ref[...] + b_ref[...]) * _silu(z_ref[...])).astype(BF16)

    return pl.pallas_call(
        body, name="conv_post", grid=(T // tm,),
        in_specs=[pl.BlockSpec((tm, E), lambda i: (i, 0)), pl.BlockSpec((tm, E), lambda i: (i, 2)),
                  pl.BlockSpec((1, E), lambda i: (0, 0)), pl.BlockSpec((1, E), lambda i: (0, 0))],
        out_specs=pl.BlockSpec((tm, E), lambda i: (i, 0)),
        out_shape=jax.ShapeDtypeStruct((T, E), BF16), compiler_params=_params(1),
    )(u1, proj, norm_g, norm_b)


def _conv_bwd(du1, proj, conv_w, E, tm):
    T = du1.shape[0]
    kc = conv_w.shape[0]
    lc, hb, rc = min(CONV_LC, E), CONV_HALO, min(CONV_RC, tm)
    nl, ni, ratio = E // lc, T // tm, tm // hb
    gate_off = E // lc
    last_halo = T // hb - 1

    sub = min(CONV_SUB, lc)
    base = hb - (kc - 1)

    def body(du_ref, dun_ref, val_ref, gate_ref, valh_ref, gateh_ref, w_ref,
             dval_ref, dgate_ref, dw_ref, db_ref, ubuf, dbuf, sbuf, dw_sc):
        i = pl.program_id(1)
        sbuf[...] = _sig(gate_ref[...])
        ubuf[hb:, :] = val_ref[...] * sbuf[...]
        halo = valh_ref[...] * _sig(gateh_ref[...])
        ubuf[0:hb, :] = jnp.where(i > 0, halo, 0.0)
        dbuf[0:tm, :] = du_ref[...]
        dbuf[tm:, :] = jnp.where(i < ni - 1, dun_ref[...], 0.0)

        @pl.when(i == 0)
        def _():
            dw_sc[...] = jnp.zeros_like(dw_sc)
            db_ref[...] = jnp.zeros_like(db_ref)

        db_ref[...] += jnp.sum(du_ref[...], axis=0, keepdims=True)
        for l0 in range(0, lc, sub):
            ls = slice(l0, l0 + sub)
            for r0 in range(0, tm, rc):
                dwin = dbuf[r0:r0 + rc + hb, ls]
                dchunk = dwin[0:rc]
                for off, rows in _shifted_rows(ubuf[r0:r0 + hb + rc, ls], rc):
                    k = off - base
                    if 0 <= k < kc:
                        prod = rows * dchunk
                        part = prod[0:SUBLANES]
                        for s8 in range(SUBLANES, rc, SUBLANES):
                            part = part + prod[s8:s8 + SUBLANES]
                        dw_sc[k, :, ls] += part
                acc = jnp.zeros((rc, sub), F32)
                for off, rows in _shifted_rows(dwin, rc):
                    k = (kc - 1) - off
                    if 0 <= k < kc:
                        acc += w_ref[k:k + 1, ls] * rows
                v, s = val_ref[r0:r0 + rc, ls], sbuf[r0:r0 + rc, ls]
                dval_ref[r0:r0 + rc, ls] = (acc * s).astype(BF16)
                dgate_ref[r0:r0 + rc, ls] = (acc * v * s * (1.0 - s)).astype(BF16)

        @pl.when(i == ni - 1)
        def _():
            for k in range(kc):
                dw_ref[k:k + 1, :] = jnp.sum(dw_sc[k], axis=0, keepdims=True)

    return pl.pallas_call(
        body, name="conv_bwd", grid=(nl, ni),
        in_specs=[
            pl.BlockSpec((tm, lc), lambda l, i: (i, l)),
            pl.BlockSpec((hb, lc), lambda l, i: (jnp.minimum((i + 1) * ratio, last_halo), l)),
            pl.BlockSpec((tm, lc), lambda l, i: (i, l)),
            pl.BlockSpec((tm, lc), lambda l, i: (i, gate_off + l)),
            pl.BlockSpec((hb, lc), lambda l, i: (jnp.maximum(i * ratio - 1, 0), l)),
            pl.BlockSpec((hb, lc), lambda l, i: (jnp.maximum(i * ratio - 1, 0), gate_off + l)),
            pl.BlockSpec((kc, lc), lambda l, i: (0, l)),
        ],
        out_specs=[pl.BlockSpec((tm, lc), lambda l, i: (i, l)), pl.BlockSpec((tm, lc), lambda l, i: (i, l)),
                   pl.BlockSpec((kc, lc), lambda l, i: (0, l)), pl.BlockSpec((1, lc), lambda l, i: (0, l))],
        out_shape=[jax.ShapeDtypeStruct((T, E), BF16), jax.ShapeDtypeStruct((T, E), BF16),
                   jax.ShapeDtypeStruct((kc, E), F32), jax.ShapeDtypeStruct((1, E), F32)],
        scratch_shapes=[pltpu.VMEM((hb + tm, lc), F32), pltpu.VMEM((tm + hb, lc), F32),
                        pltpu.VMEM((tm, lc), F32), pltpu.VMEM((kc, SUBLANES, lc), F32)],
        compiler_params=_params(2),
    )(du1, du1, proj, proj, proj, proj, conv_w)


def _attn_fwd(q_all, kv, kr, H, tq, scale):
    T = q_all.shape[0]
    nq = T // tq
    pair = 2
    W = pair * LANES
    assert H % pair == 0
    hp_n = H // pair

    def body(qn_ref, qr_ref, kn_ref, kr_ref, v_ref, o_ref, lse_ref, *scratch):
        qi = pl.program_id(1)
        chains = [scratch[4 * a:4 * a + 4] for a in range(pair)]
        lanes = [slice(a * LANES, (a + 1) * LANES) for a in range(pair)]
        groups = [slice(c * LANES, (c + 1) * LANES) for c in range(tq // LANES)]

        def fold(x, op):
            r = x[:, groups[0]]
            for gsl in groups[1:]:
                r = op(r, x[:, gsl])
            return r

        for _, m_sc, l_sc, acc_sc in chains:
            m_sc[...] = jnp.full_like(m_sc, MASK_VALUE)
            l_sc[...] = jnp.zeros_like(l_sc)
            acc_sc[...] = jnp.zeros_like(acc_sc)

        def scores(j, masked):
            rows = pl.ds(pl.multiple_of(j * tq, tq), tq)
            krope = kr_ref[rows, :]
            for a, (s_sc, m_sc, _, _) in enumerate(chains):
                q = jnp.concatenate([qn_ref[:, lanes[a]], qr_ref[:, lanes[a]]], axis=1)
                k = jnp.concatenate([kn_ref[rows, lanes[a]], krope], axis=1)
                s = _dot(q, k, NT) * (scale * LOG2_E)
                if masked:
                    row = lax.broadcasted_iota(jnp.int32, s.shape, 0)
                    col = lax.broadcasted_iota(jnp.int32, s.shape, 1)
                    s = jnp.where(col <= row, s, MASK_VALUE)
                s_sc[j] = s
                m_sc[...] = jnp.maximum(m_sc[...], fold(s, jnp.maximum))

        def two_per_trip(fn, count):
            def two(p, carry):
                fn(2 * p)
                fn(2 * p + 1)
                return carry

            lax.fori_loop(0, count // 2, two, 0)

            @pl.when(count % 2 == 1)
            def _():
                fn(count - 1)

        two_per_trip(functools.partial(scores, masked=False), qi)
        scores(qi, True)
        for _, m_sc, _, _ in chains:
            m_sc[...] = jnp.broadcast_to(jnp.max(m_sc[...], axis=1, keepdims=True), m_sc.shape)

        def weigh(j):
            rows = pl.ds(pl.multiple_of(j * tq, tq), tq)
            for a, (s_sc, m_sc, l_sc, acc_sc) in enumerate(chains):
                s, m = s_sc[j], m_sc[...]
                p = jnp.concatenate([jnp.exp2(s[:, gsl] - m) for gsl in groups], axis=1)
                l_sc[...] += fold(p, jnp.add)
                acc_sc[...] += _dot(p, v_ref[rows, lanes[a]], NN)

        two_per_trip(weigh, qi + 1)
        for a, (_, m_sc, l_sc, acc_sc) in enumerate(chains):
            l = jnp.sum(l_sc[...], axis=1, keepdims=True)
            o_ref[:, lanes[a]] = acc_sc[...] / l
            lse_ref[a] = m_sc[:, 0:1] * (1.0 / LOG2_E) + jnp.log(l)

    chain_scratch = [pltpu.VMEM((nq, tq, tq), F32), pltpu.VMEM((tq, LANES), F32), pltpu.VMEM((tq, LANES), F32),
                     pltpu.VMEM((tq, LANES), F32)]
    return pl.pallas_call(
        body, name="attn_fwd", grid=(hp_n, nq),
        in_specs=[pl.BlockSpec((tq, W), lambda hp, qi: (qi, hp)),
                  pl.BlockSpec((tq, W), lambda hp, qi: (qi, hp_n + hp)),
                  pl.BlockSpec((T, W), lambda hp, qi: (0, hp)),
                  pl.BlockSpec((T, LANES), lambda hp, qi: (0, 0)),
                  pl.BlockSpec((T, W), lambda hp, qi: (0, hp_n + hp))],
        out_specs=[pl.BlockSpec((tq, W), lambda hp, qi: (qi, hp)),
                   pl.BlockSpec((pair, tq, 1), lambda hp, qi: (hp, qi, 0))],
        out_shape=[jax.ShapeDtypeStruct((T, H * LANES), F32), jax.ShapeDtypeStruct((H, T, 1), F32)],
        scratch_shapes=chain_scratch * pair, compiler_params=_params(2),
    )(q_all, q_all, kv, kr, kv)


def _attn_bwd(q_all, kv, kr, do, o, lse, cos, sin, H, tq, scale):
    T = q_all.shape[0]
    nq = T // tq
    HV = H * LANES
    pair = 2
    tk2 = pair * tq
    ng = T // tk2
    assert ng * tk2 == T and pair == 2

    def body(qn_ref, qr_ref, kn_ref, kr_ref, v_ref, do_ref, o_ref, lse_ref, cos_ref, sin_ref,
             dqn_ref, dqr_ref, dkn_ref, dkr_ref, dv_ref, dq_sc, dk_sc, dv_sc):
        g = pl.program_id(1)

        @pl.when(g == 0)
        def _():
            dq_sc[...] = jnp.zeros_like(dq_sc)

        key_rows = [slice(c * tq, (c + 1) * tq) for c in range(pair)]

        def block(qi, modes):
            rows = pl.ds(pl.multiple_of(qi * tq, tq), tq)
            q = jnp.concatenate([qn_ref[rows, :], qr_ref[rows, :]], axis=1)
            dov = do_ref[rows, :]
            delta = jnp.sum(dov.astype(F32) * o_ref[rows, :], axis=1, keepdims=True)
            lse_q = lse_ref[0, rows, :]
            dq, dkv = None, []
            for kr_, masked in zip(key_rows, modes):
                if masked is None:
                    dkv.append(None)
                    continue
                k = jnp.concatenate([kn_ref[kr_, :], kr_ref[kr_, :]], axis=1)
                s = _dot(q, k, NT) * scale
                if masked:
                    row = lax.broadcasted_iota(jnp.int32, s.shape, 0)
                    col = lax.broadcasted_iota(jnp.int32, s.shape, 1)
                    s = jnp.where(col <= row, s, MASK_VALUE)
                p = jnp.exp(s - lse_q)
                dv = _dot(p, dov, TN)
                dp = _dot(dov, v_ref[kr_, :], NT)
                ds = (p * (dp - delta) * scale).astype(BF16)
                dkv.append((_dot(ds, q, TN), dv))
                part = _dot(ds, k, NN)
                dq = part if dq is None else dq + part
            return rows, dq, dkv

        rows_a, dq_a, (kv_a0, _) = block(pair * g, (True, None))
        rows_b, dq_b, (kv_b0, kv_b1) = block(pair * g + 1, (False, True))
        dk_sc[key_rows[0], :] = kv_a0[0] + kv_b0[0]
        dv_sc[key_rows[0], :] = kv_a0[1] + kv_b0[1]
        dk_sc[key_rows[1], :] = kv_b1[0]
        dv_sc[key_rows[1], :] = kv_b1[1]
        dq_sc[rows_a, :] += dq_a
        dq_sc[rows_b, :] += dq_b

        def below(trip, carry):
            for qi in (pair * g + pair + 2 * trip, pair * g + pair + 2 * trip + 1):
                rows, dq, dkv = block(qi, (False, False))
                for kr_, (dk, dv) in zip(key_rows, dkv):
                    dk_sc[kr_, :] += dk
                    dv_sc[kr_, :] += dv
                dq_sc[rows, :] += dq
            return carry

        lax.fori_loop(0, (nq - pair * g - pair) // 2, below, 0)
        dkn_ref[...] = dk_sc[:, :LANES].astype(BF16)
        dkr_ref[...] = dk_sc[:, LANES:]
        dv_ref[...] = dv_sc[...].astype(BF16)

        @pl.when(g == ng - 1)
        def _():
            dqn_ref[...] = dq_sc[:, :LANES].astype(BF16)
            dqr_ref[...] = _rope(dq_sc[:, LANES:], cos_ref[...], sin_ref[...], transpose=True).astype(BF16)

    whole = lambda col: pl.BlockSpec((T, LANES), col)
    tile = lambda col: pl.BlockSpec((tk2, LANES), col)
    return pl.pallas_call(
        body, name="attn_bwd", grid=(H, ng),
        in_specs=[whole(lambda h, g: (0, h)), whole(lambda h, g: (0, H + h)),
                  tile(lambda h, g: (g, h)), tile(lambda h, g: (g, 0)), tile(lambda h, g: (g, H + h)),
                  whole(lambda h, g: (0, h)), whole(lambda h, g: (0, h)),
                  pl.BlockSpec((1, T, 1), lambda h, g: (h, 0, 0)),
                  whole(lambda h, g: (0, 0)), whole(lambda h, g: (0, 0))],
        out_specs=[whole(lambda h, g: (0, h)), whole(lambda h, g: (0, h)),
                   tile(lambda h, g: (g, h)), tile(lambda h, g: (g, h)), tile(lambda h, g: (g, h))],
        out_shape=[jax.ShapeDtypeStruct((T, HV), BF16), jax.ShapeDtypeStruct((T, HV), BF16),
                   jax.ShapeDtypeStruct((T, HV), BF16), jax.ShapeDtypeStruct((T, HV), F32),
                   jax.ShapeDtypeStruct((T, HV), BF16)],
        scratch_shapes=[pltpu.VMEM((T, 2 * LANES), F32), pltpu.VMEM((tk2, 2 * LANES), F32),
                        pltpu.VMEM((tk2, LANES), F32)],
        compiler_params=_params(2),
    )(q_all, q_all, kv, kr, kv, do, o, lse, cos, sin)


def _key_rope_bwd(dkr_heads, cos, sin, H, tm):
    T, HV = dkr_heads.shape

    def body(dkr_ref, cos_ref, sin_ref, dk_ref):
        dk = dkr_ref[...]
        tot = dk[:, 0:LANES]
        for h in range(1, H):
            tot = tot + dk[:, h * LANES:(h + 1) * LANES]
        dk_ref[...] = _rope(tot, cos_ref[...], sin_ref[...], transpose=True)

    return pl.pallas_call(
        body, name="key_rope_bwd", grid=(T // tm,),
        in_specs=[pl.BlockSpec((tm, HV), lambda i: (i, 0)),
                  pl.BlockSpec((tm, LANES), lambda i: (i, 0)), pl.BlockSpec((tm, LANES), lambda i: (i, 0))],
        out_specs=pl.BlockSpec((tm, LANES), lambda i: (i, 0)),
        out_shape=jax.ShapeDtypeStruct((T, LANES), F32), compiler_params=_params(1),
    )(dkr_heads, cos, sin)


def _place():
    x, y, c = lax.axis_index("x"), lax.axis_index("y"), lax.axis_index("c")
    chips = [(1 - x, y), (x, 1 - y), (1 - x, 1 - y)]
    return x, y, c, chips


def _all_gather_chips(arrs):
    n = len(arrs)
    halves = [a.shape[0] // 2 for a in arrs]
    assert all(h * 2 == a.shape[0] and h % BF16_ROWS == 0 for h, a in zip(halves, arrs))

    def body(*refs):
        w_refs, out_refs = refs[:n], refs[n:2 * n]
        send_sems, recv_sems, local_sems = refs[2 * n:]
        x, y, c, chips = _place()
        sibling = (x, y, 1 - c)
        waits = []
        for w, (w_ref, out_ref, half) in enumerate(zip(w_refs, out_refs, halves)):
            def region(px, py, pc, out_ref=out_ref, half=half):
                return out_ref.at[2 * px + py, pl.ds(pc * half, half), :]

            def copy(k, block, to, src=None, w=w, region=region):
                return pltpu.make_async_remote_copy(
                    src_ref=region(*block) if src is None else src, dst_ref=region(*block),
                    send_sem=send_sems.at[6 * w + k], recv_sem=recv_sems.at[6 * w + k],
                    device_id=to, device_id_type=MESH)

            mine = pltpu.make_async_copy(w_ref, out_ref.at[2 * x + y], local_sems.at[w])
            mine.start()
            my_half = w_ref.at[pl.ds(c * half, half), :]
            first = [copy(j, (x, y, c), (*chip, c), src=my_half) for j, chip in enumerate(chips)]
            for cp in first:
                cp.start()
            waits.append((copy, mine, first))
        for copy, mine, first in waits:
            passed = [copy(3 + j, (*chip, c), sibling) for j, chip in enumerate(chips)]
            for j, chip in enumerate(chips):
                copy(j, (*chip, c), (x, y, c)).wait_recv()
                passed[j].start()
            for j, chip in enumerate(chips):
                copy(3 + j, (*chip, 1 - c), (x, y, c)).wait_recv()
            for cp in first + passed:
                cp.wait_send()
            mine.wait()

    return pl.pallas_call(
        body, name="gather_weights", in_specs=[ANY] * n, out_specs=[ANY] * n,
        out_shape=[jax.ShapeDtypeStruct((N_CHIPS,) + a.shape, a.dtype) for a in arrs],
        scratch_shapes=[pltpu.SemaphoreType.DMA((6 * n,)), pltpu.SemaphoreType.DMA((6 * n,)),
                        pltpu.SemaphoreType.DMA((n,))],
    )(*arrs)


def _swap_cores(name, parts):
    n = len(parts)

    def body(*refs):
        p_refs, r_refs = refs[:n], refs[n:2 * n]
        send_sems, recv_sems = refs[2 * n:]
        x, y, c, _ = _place()
        copies = [pltpu.make_async_remote_copy(
            src_ref=p_refs[w], dst_ref=r_refs[w], send_sem=send_sems.at[w], recv_sem=recv_sems.at[w],
            device_id=(x, y, 1 - c), device_id_type=MESH) for w in range(n)]
        for cp in copies:
            cp.start()
        for cp in copies:
            cp.wait()

    return pl.pallas_call(
        body, name=name, in_specs=[ANY] * n, out_specs=[ANY] * n,
        out_shape=[jax.ShapeDtypeStruct(p.shape, p.dtype) for p in parts],
        scratch_shapes=[pltpu.SemaphoreType.DMA((n,)), pltpu.SemaphoreType.DMA((n,))],
    )(*parts)


HBM = pl.BlockSpec(memory_space=pltpu.HBM)
SEM = pl.BlockSpec(memory_space=pltpu.SEMAPHORE)
EFFECT = pltpu.SideEffectType.DATAFLOW_SIDE_EFFECTING


def _push_copies(a_refs, l_refs, send_sems, recv_sems, by_target):
    x, y, c, chips = _place()
    me = 2 * x + y

    def part(a_ref, q):
        if by_target == 'cols':
            n = a_ref.shape[-1] // N_CHIPS
            return a_ref.at[(slice(None),) * (len(a_ref.shape) - 1) + (pl.ds(pl.multiple_of(q * n, LANES), n),)]
        return a_ref.at[q] if by_target else a_ref

    out = []
    for w, (a_ref, l_ref) in enumerate(zip(a_refs, l_refs)):
        for j, (px, py) in enumerate(chips):
            peer = 2 * px + py
            out.append((
                pltpu.make_async_remote_copy(
                    src_ref=part(a_ref, peer), dst_ref=l_ref.at[me],
                    send_sem=send_sems.at[3 * w + j], recv_sem=recv_sems.at[3 * w + j],
                    device_id=(px, py, c), device_id_type=MESH),
                pltpu.make_async_remote_copy(
                    src_ref=part(a_ref, me), dst_ref=l_ref.at[peer],
                    send_sem=send_sems.at[3 * w + j], recv_sem=recv_sems.at[3 * w + j],
                    device_id=(px, py, c), device_id_type=MESH)))
    return out


def _landing_shape(a, by_target):
    if by_target == 'cols':
        return (N_CHIPS,) + a.shape[:-1] + (a.shape[-1] // N_CHIPS,)
    return (N_CHIPS,) + (a.shape[1:] if by_target else a.shape)


def _push_start(name, arrs, by_target):
    n = len(arrs)
    lands = [lax.empty(_landing_shape(a, by_target), a.dtype) for a in arrs]

    def body(*refs):
        a_refs, l_refs = refs[:n], refs[n:2 * n]
        send_sems, recv_sems = refs[2 * n], refs[2 * n + 1]
        token = refs[-1]
        for send, _ in _push_copies(a_refs, l_refs, send_sems, recv_sems, by_target):
            send.start()
        token[...] = jnp.zeros_like(token)

    res = pl.pallas_call(
        body, name=name,
        out_shape=(pltpu.SemaphoreType.DMA((3 * n,)), pltpu.SemaphoreType.DMA((3 * n,)),
                   *[pltpu.HBM(a.shape, a.dtype) for a in arrs], *[pltpu.HBM(l.shape, l.dtype) for l in lands],
                   jax.ShapeDtypeStruct((8, LANES), F32)),
        in_specs=[HBM] * (2 * n), out_specs=(SEM, SEM, *[HBM] * (2 * n), pl.BlockSpec(memory_space=pltpu.VMEM)),
        input_output_aliases={i: 2 + i for i in range(2 * n)},
        compiler_params=pltpu.CompilerParams(has_side_effects=EFFECT),
    )(*[pltpu.with_memory_space_constraint(a, pltpu.HBM) for a in list(arrs) + lands])
    return res[0], res[1], list(res[2:2 + n]), list(res[2 + n:2 + 2 * n]), res[-1]


def _push_wait(name, send_sems, recv_sems, arrs, lands, after, by_target):
    n = len(arrs)

    def body(*refs):
        a_refs, l_refs = refs[:n], refs[n:2 * n]
        s_sems, r_sems = refs[2 * n], refs[2 * n + 1]
        for send, recv in _push_copies(a_refs, l_refs, s_sems, r_sems, by_target):
            send.wait_send()
            recv.wait_recv()

    res = pl.pallas_call(
        body, name=name,
        out_shape=[pltpu.HBM(a.shape, a.dtype) for a in list(arrs) + list(lands)],
        in_specs=[HBM] * (2 * n) + [SEM, SEM] + [ANY] * len(after), out_specs=[HBM] * (2 * n),
        input_output_aliases={i: i for i in range(2 * n)},
        compiler_params=pltpu.CompilerParams(has_side_effects=EFFECT),
    )(*arrs, *lands, send_sems, recv_sems, *after)
    return list(res[:n]), list(res[n:])


def _all_reduce_small(part):
    def body(p_ref, out_ref, sib_buf, chip_buf, send_sems, recv_sems):
        x, y, c, chips = _place()
        me = 2 * x + y
        swap = pltpu.make_async_remote_copy(
            src_ref=p_ref, dst_ref=sib_buf, send_sem=send_sems.at[0], recv_sem=recv_sems.at[0],
            device_id=(x, y, 1 - c), device_id_type=MESH)
        swap.start()
        swap.wait()
        chip_buf[me] = p_ref[...] + sib_buf[...]
        copies = []
        for j, (px, py) in enumerate(chips):
            cp = pltpu.make_async_remote_copy(
                src_ref=chip_buf.at[me], dst_ref=chip_buf.at[me], send_sem=send_sems.at[1 + j],
                recv_sem=recv_sems.at[1 + j], device_id=(px, py, c), device_id_type=MESH)
            cp.start()
            copies.append(cp)
        for j, (px, py) in enumerate(chips):
            pltpu.make_async_remote_copy(
                src_ref=chip_buf.at[me], dst_ref=chip_buf.at[2 * px + py], send_sem=send_sems.at[1 + j],
                recv_sem=recv_sems.at[1 + j], device_id=(px, py, c), device_id_type=MESH).wait_recv()
        for cp in copies:
            cp.wait_send()
        tot = chip_buf[0]
        for q in range(1, N_CHIPS):
            tot = tot + chip_buf[q]
        out_ref[...] = tot

    vm = pl.BlockSpec(memory_space=pltpu.VMEM)
    return pl.pallas_call(
        body, name="all_reduce_small", in_specs=[vm], out_specs=vm,
        out_shape=jax.ShapeDtypeStruct(part.shape, F32),
        scratch_shapes=[pltpu.VMEM(part.shape, F32), pltpu.VMEM((N_CHIPS,) + part.shape, F32),
                        pltpu.SemaphoreType.DMA((N_CHIPS,)), pltpu.SemaphoreType.DMA((N_CHIPS,))],
    )(part)


def _row_tiles(shape):
    ax = next(d for d, s in enumerate(shape) if s > 1)
    tr = _gcd(ADAM_ROWS, shape[ax])
    block = tuple(tr if d == ax else s for d, s in enumerate(shape))
    return shape[ax] // tr, block, lambda i: tuple(i if d == ax else 0 for d in range(len(shape)))


def _sum_chips(name, landed, sent, chip, by_target):
    shape = landed.shape[1:]
    steps, block, index = _row_tiles(shape)
    if by_target == 'cols':
        own_spec = pl.BlockSpec(block, lambda i, c: index(i)[:-1] + (c[0],))
    else:
        own_spec = pl.BlockSpec((1,) + block, lambda i, c: (c[0],) + index(i))

    def body(chip_ref, l_ref, s_ref, o_ref):
        own = (s_ref[...] if by_target == 'cols' else s_ref[0]).astype(F32)
        tot = None
        for q in range(N_CHIPS):
            term = jnp.where(chip_ref[0] == q, own, l_ref[q].astype(F32))
            tot = term if tot is None else tot + term
        o_ref[...] = tot

    return pl.pallas_call(
        body, name=name,
        grid_spec=pltpu.PrefetchScalarGridSpec(
            num_scalar_prefetch=1, grid=(steps,),
            in_specs=[pl.BlockSpec((N_CHIPS,) + block, lambda i, c: (0,) + index(i)), own_spec],
            out_specs=pl.BlockSpec(block, lambda i, c: index(i))),
        out_shape=jax.ShapeDtypeStruct(shape, F32), compiler_params=_params(1),
    )(chip, landed, sent)


def _adamw_math(g, w, m, v):
    mn = ADAM_B1 * m + (1.0 - ADAM_B1) * g
    vn = ADAM_B2 * v + (1.0 - ADAM_B2) * jnp.square(g)
    m_hat = mn / (1.0 - ADAM_B1 ** ADAM_STEP)
    v_hat = vn / (1.0 - ADAM_B2 ** ADAM_STEP)
    return -ADAM_LR * (m_hat / (jnp.sqrt(v_hat) + ADAM_EPS) + ADAM_WD * w), mn, vn


def _adamw(name, g_parts, w, m, v):
    steps, block, index = _row_tiles(w.shape)
    n = len(g_parts)

    def body(*refs):
        g = refs[0][...]
        for r in refs[1:n]:
            g = g + r[...]
        w_ref, m_ref, v_ref, go_ref, d_ref, mo_ref, vo_ref = refs[n:]
        go_ref[...] = g
        d_ref[...], mo_ref[...], vo_ref[...] = _adamw_math(g, w_ref[...], m_ref[...], v_ref[...])

    spec = pl.BlockSpec(block, index)
    return pl.pallas_call(
        body, name=name, grid=(steps,), in_specs=[spec] * (n + 3), out_specs=[spec] * 4,
        out_shape=[jax.ShapeDtypeStruct(w.shape, F32)] * 4, compiler_params=_params(1),
    )(*g_parts, w, m, v)


def _adamw_vectors(items):
    n = len(items)

    def body(*refs):
        ins, outs = refs[:4 * n], refs[4 * n:]
        for k in range(n):
            g, w, m, v = (r[...] for r in ins[4 * k:4 * k + 4])
            outs[3 * k][...], outs[3 * k + 1][...], outs[3 * k + 2][...] = _adamw_math(g, w, m, v)

    vm = pl.BlockSpec(memory_space=pltpu.VMEM)
    res = pl.pallas_call(
        body, name="adamw_vectors", in_specs=[vm] * (4 * n), out_specs=[vm] * (3 * n),
        out_shape=[jax.ShapeDtypeStruct(it[1].shape, F32) for it in items for _ in range(3)],
    )(*[a for it in items for a in it])
    return [res[3 * k:3 * k + 3] for k in range(n)]


def _pack_rows(flat, dtype, multiple):
    n = flat.shape[0]
    total = -(-n // multiple) * multiple
    return jnp.pad(flat, (0, total - n)).astype(dtype).reshape(total // LANES, LANES)


def kernel(x, positions, ln_g, ln_b, a_w_in, a_b_in, a_conv_w, a_conv_b, a_norm_g, a_norm_b, a_w_out, a_b_out, kv_w_down, kv_norm_g, kv_w_uk, kv_w_uv, b_w_in, b_q_norm_g, b_w_uq, b_w_out, loss_target, m_ln_g, m_ln_b, m_a_w_in, m_a_b_in, m_a_conv_w, m_a_conv_b, m_a_norm_g, m_a_norm_b, m_a_w_out, m_a_b_out, m_kv_w_down, m_kv_norm_g, m_kv_w_uk, m_kv_w_uv, m_b_w_in, m_b_q_norm_g, m_b_w_uq, m_b_w_out, v_ln_g, v_ln_b, v_a_w_in, v_a_b_in, v_a_conv_w, v_a_conv_b, v_a_norm_g, v_a_norm_b, v_a_w_out, v_a_b_out, v_kv_w_down, v_kv_norm_g, v_kv_w_uk, v_kv_w_uv, v_b_w_in, v_b_q_norm_g, v_b_w_uq, v_b_w_out):
    T, D = x.shape[1], x.shape[2]
    E = N_CHIPS * a_w_out.shape[1]
    KC = a_conv_w.shape[1]
    RKV = kv_norm_g.shape[0]
    H, DN = kv_w_uk.shape[1], kv_w_uk.shape[2]
    RQ = b_q_norm_g.shape[1]
    HV = N_CHIPS * b_w_out.shape[1]
    assert DN == LANES and kv_w_uv.shape[2] == LANES and HV == H * LANES
    assert kv_w_down.shape[1] == RKV + ROPE_DIM and b_w_uq.shape[3] == DN + ROPE_DIM
    assert ln_g.shape[0] == 2 and a_w_in.shape[0] == 1 and b_w_in.shape[0] == 1
    alpha = (2.0 * ln_g.shape[0]) ** 0.25
    scale = 1.0 / math.sqrt(DN + ROPE_DIM)
    WK = -(-(RKV + LANES) // 256) * 256
    assert WK % RQ == 0
    Z_OFF = WK + RQ
    tmw, tq = min(TM_WIDE, T), min(TQ, T)
    t512, t1024 = _fit(512, T), _fit(1024, T)
    xs = x[0]
    tgt = loss_target[0]
    px, py = lax.axis_index("x"), lax.axis_index("y")
    chip = 2 * px + py

    mats = [a_w_out[0], kv_w_down, kv_w_uk, kv_w_uv, b_w_in[0], b_w_uq[0], b_w_out[0]]
    vecs = [a_b_in[0], a_conv_w[0], a_conv_b[0], a_norm_g[0], a_norm_b[0], a_b_out[0]]
    vec_bits = jnp.concatenate([lax.bitcast_convert_type(w.reshape(-1), BF16).reshape(-1) for w in vecs])
    rest = [w.astype(BF16) for w in mats]
    g_win, gathered = _all_gather_chips(
        [a_w_in[0].astype(BF16), _pack_rows(vec_bits, BF16, 2 * BF16_ROWS * LANES)])
    gathered = gathered.reshape(N_CHIPS, -1)
    gathered, rest = lax.optimization_barrier((gathered, rest))
    rest_sems = _push_start("gather_rest_start", rest, by_target=False)
    off = 0
    fvec = []
    for w in vecs:
        bits = gathered[:, off:off + 2 * w.size].reshape((N_CHIPS,) + w.shape + (2,))
        fvec.append(lax.bitcast_convert_type(bits, F32))
        off += 2 * w.size
    cols = lambda g: jnp.moveaxis(g, 0, -2).reshape(g.shape[1:-1] + (N_CHIPS * g.shape[-1],))
    b_in = cols(fvec[0][:, None, :])
    conv_w = cols(fvec[1])
    conv_b, norm_g, norm_b, b_out = (cols(f[:, None, :]) for f in fvec[2:])
    row = lambda a: a.reshape(1, -1)
    g0, b0, g1, b1 = row(ln_g[0]), row(ln_b[0]), row(ln_g[1]), row(ln_b[1])
    kv_g, q_g = row(kv_norm_g), row(b_q_norm_g[0])
    plain = lambda acc, ins, i, j: [acc]

    b_in = b_in + rest_sems[4][0, 0]
    (proj,) = _row_mm("a_in", [((xs,), None)], g_win, nt=False, tm=t1024, tn=3 * E // N_CHIPS, tk=D,
                      outs=[((T, 3 * E), F32, 'tile')], epi=lambda acc, ins, i, j: [acc + ins[0]],
                      epi_ins=[(b_in, 'col')])
    u1 = _conv_fwd(proj, conv_w, conv_b, E, tmw)
    u4 = _conv_post(u1, proj, norm_g, norm_b, E, tmw)

    rest, landed = _push_wait("gather_rest_wait", *rest_sems[:4], after=[u4], by_target=False)
    g_wout, g_wd, g_uk, g_uv, g_wbin, g_wuq, g_wbout = [
        lax.dynamic_update_slice(l, w[None], (chip,) + (0,) * w.ndim) for w, l in zip(rest, landed)]
    w_out = g_wout.reshape(E, D)
    wd = g_wd.reshape(D, RKV + ROPE_DIM)
    zpad = jnp.zeros((D, ROPE_HALF), BF16)
    wd_p = jnp.concatenate(
        [wd[:, :RKV], wd[:, RKV:RKV + ROPE_HALF], zpad, wd[:, RKV + ROPE_HALF:], zpad,
         jnp.zeros((D, WK - RKV - LANES), BF16)], axis=1)
    w_bin = cols(g_wbin)
    w_z = w_bin[:, RQ:]
    wb_small = jnp.concatenate([wd_p, w_bin[:, :RQ]], axis=1)
    wb_all = jnp.concatenate([wd_p, w_bin], axis=1)
    w_kv = jnp.concatenate([g_uk.reshape(RKV, HV), g_uv.reshape(RKV, HV)], axis=1)
    wuq = g_wuq.reshape(RQ, H, DN + ROPE_DIM)
    zq = jnp.zeros((RQ, H, ROPE_HALF), BF16)
    w_qr = jnp.concatenate([wuq[:, :, DN:DN + ROPE_HALF], zq, wuq[:, :, DN + ROPE_HALF:], zq], axis=2)
    w_q = jnp.concatenate([wuq[:, :, :DN].reshape(RQ, HV), w_qr.reshape(RQ, HV)], axis=1)
    w_bout = g_wbout.reshape(HV, D)

    freqs = ROPE_THETA ** (-jnp.arange(0, ROPE_DIM, 2, dtype=F32) / ROPE_DIM)
    ang = positions[0].astype(F32)[:, None] * freqs
    cs, sn = jnp.cos(ang), jnp.sin(ang)
    ones, zeros = jnp.ones_like(cs), jnp.zeros_like(cs)
    cos_t = jnp.concatenate([cs, ones, cs, ones], axis=1)
    sin_t = jnp.concatenate([-sn, zeros, sn, zeros], axis=1)

    def ln_epi(acc, ins, i, j):
        bias, res, g, b = ins
        xhat, rstd = _ln_stats(alpha * res + acc + bias)
        h = xhat * g + b
        return [h, h, xhat, rstd]

    h1, h1b, xhat1, rstd1 = _row_mm(
        "a_out", [((u4,), None)], w_out, nt=False, tm=t512, tn=D, tk=_fit(2048, E),
        outs=[((T, D), F32, 'tile'), ((T, D), BF16, 'tile'), ((T, D), F32, 'tile'), ((T, 1), F32, 'row')],
        epi=ln_epi, epi_ins=[(b_out, 'col'), (xs, 'tile'), (g0, 'col'), (b0, 'col')])

    tkb = _fit(512, _gcd(WK, RQ, HV))
    def latents_epi(acc, ins, i, j):
        kg, qg, cos_, sin_ = ins
        return [acc, _rms_fwd(acc[:, :RKV], kg), _rope(acc[:, RKV:RKV + LANES], cos_, sin_),
                _rms_fwd(acc[:, WK:WK + RQ], qg)]

    whole_row = lambda a: (a, pl.BlockSpec(a.shape, lambda i, j, k: (0, 0)))
    pb, c_lat, kr, cqn = _row_mm(
        "b_in", [((h1b,), None)], wb_small, nt=False, tm=t512, tn=Z_OFF, tk=_fit(1024, D),
        outs=[((T, Z_OFF), F32, 'tile'), ((T, RKV), BF16, 'row'), ((T, LANES), BF16, 'row'),
              ((T, RQ), BF16, 'row')],
        epi=latents_epi, epi_ins=[whole_row(kv_g), whole_row(q_g), (cos_t, 'row'), (sin_t, 'row')])
    (zb,) = _row_mm("b_in_gate", [((h1b,), None)], w_z, nt=False, tm=t1024, tn=_fit(2048, HV),
                    tk=_fit(1024, D), outs=[((T, HV), F32, 'tile')], epi=plain)
    (kv,) = _row_mm("kv_up", [((c_lat,), None)], w_kv, nt=False, tm=t1024, tn=_fit(2048, HV),
                    tk=_fit(1024, RKV), outs=[((T, 2 * HV), BF16, 'tile')], epi=plain)
    tnq = _fit(2048, HV)
    half_q = HV // tnq

    def q_epi(acc, ins, i, j):
        return [jnp.where(j >= half_q, _rope(acc, ins[0], ins[1]), acc)]

    (q_all,) = _row_mm("q_up", [((cqn,), None)], w_q, nt=False, tm=t1024, tn=tnq, tk=_fit(1024, RQ),
                       outs=[((T, 2 * HV), BF16, 'tile')], epi=q_epi,
                       epi_ins=[(cos_t, 'row'), (sin_t, 'row')])
    o, lse = _attn_fwd(q_all, kv, kr, H, tq, scale)

    def loss_epi(acc, ins, i, j):
        res, g, b, target = ins
        xhat, rstd = _ln_stats(alpha * res + acc)
        diff = xhat * g + b - target
        dr, dg, db = _ln_bwd(diff / D, xhat, rstd, g)
        return [dr, 0.5 * jnp.sum(diff * diff, keepdims=True) / D, dg, db]

    dr1, loss_part, dg1, db1 = _row_mm(
        "b_out", [((o, zb), _gate)], w_bout, nt=False, tm=tmw, tn=D, tk=_fit(2048, HV),
        outs=[((T, D), F32, 'tile'), ((1, 1), F32, 'acc'), ((1, D), F32, 'acc'), ((1, D), F32, 'acc')],
        epi=loss_epi, epi_ins=[(h1, 'tile'), (g1, 'col'), (b1, 'col'), (tgt, 'tile')])

    def gate_bwd_epi(acc, ins, i, j):
        gate, gate_grad = _silu_and_grad(ins[1])
        return [acc * gate, acc * ins[0] * gate_grad]

    do, dz = _row_mm(
        "b_out_bwd", [((dr1,), None)], w_bout, nt=True, tm=tmw, tn=_fit(2048, HV), tk=_fit(1024, D),
        outs=[((T, HV), BF16, 'tile'), ((T, HV), BF16, 'tile')], epi=gate_bwd_epi,
        epi_ins=[(o, 'tile'), (zb, 'tile')])
    gw_bout = _tn_mm("dw_b_out", (o, zb), _gate, [((dr1,), None)], tn=_fit(1024, D), tk=t512, out_dtype=BF16)
    dqn, dqr_pre, dkn, dkr_h, dv = _attn_bwd(q_all, kv, kr, do, o, lse, cos_t, sin_t, H, tq, scale)
    dkr_pre = _key_rope_bwd(dkr_h, cos_t, sin_t, H, tmw)

    def cq_bwd_epi(acc, ins, i, j):
        dx, dg = _rms_bwd(acc, ins[0], ins[1])
        return [dx, dg]

    dcq, dqg = _row_mm(
        "q_up_bwd", [((dqn,), None), ((dqr_pre,), None)], w_q, nt=True, tm=t1024, tn=RQ, tk=_fit(2048, HV),
        outs=[((T, RQ), BF16, 'tile'), ((1, RQ), F32, 'acc')], epi=cq_bwd_epi,
        epi_ins=[(pb, pl.BlockSpec((t1024, RQ), lambda i, j, k: (i, WK // RQ))), (q_g, 'col')])
    gw_q = _tn_mm("dw_q_up", (cqn,), None, [((dqn,), None), ((dqr_pre,), None)],
                  tn=_fit(2048, HV), tk=t1024, out_dtype=BF16)

    def ckv_bwd_epi(acc, ins, i, j):
        blk, dkr_t, g = ins
        dx, dg = _rms_bwd(acc, blk[:, :RKV], g)
        parts = [dx, dkr_t]
        if WK > RKV + LANES:
            parts.append(jnp.zeros((dx.shape[0], WK - RKV - LANES), F32))
        return [jnp.concatenate(parts, axis=1), dg]

    dckv, dkvg = _row_mm(
        "kv_up_bwd", [((dkn,), None), ((dv,), None)], w_kv, nt=True, tm=t1024, tn=RKV, tk=_fit(2048, HV),
        outs=[((T, WK), BF16, pl.BlockSpec((t1024, WK), lambda i, j, k: (i, 0))), ((1, RKV), F32, 'acc')],
        epi=ckv_bwd_epi,
        epi_ins=[(pb, pl.BlockSpec((t1024, WK), lambda i, j, k: (i, 0))), (dkr_pre, 'row'), (kv_g, 'col')])
    gw_kv = _tn_mm("dw_kv_up", (c_lat,), None, [((dkn,), None), ((dv,), None)],
                   tn=_fit(2048, HV), tk=t1024, out_dtype=BF16)

    def ln1_bwd_epi(acc, ins, i, j):
        dr_up, xhat, rstd, g = ins
        dr, dg, db = _ln_bwd(alpha * dr_up + acc, xhat, rstd, g)
        return [dr, dg, db]

    dp_segs = [((dckv,), None), ((dcq,), None), ((dz,), None)]
    gw_ball = _tn_mm("dw_b_in", (h1b,), None, dp_segs, tn=tkb, tk=t1024, out_dtype=BF16)

    shard_cols = lambda g: jnp.moveaxis(g.reshape(g.shape[0], N_CHIPS, -1), 1, 0)
    shard_rows = lambda g: g.reshape(N_CHIPS, g.shape[0] // N_CHIPS, g.shape[1])
    gq = gw_q.reshape(RQ, 2, H, LANES)
    g_uq = jnp.concatenate(
        [gq[:, 0], gq[:, 1, :, :ROPE_HALF], gq[:, 1, :, 2 * ROPE_HALF:3 * ROPE_HALF]], axis=2)
    g_wd_full = jnp.concatenate(
        [gw_ball[:, :RKV], gw_ball[:, RKV:RKV + ROPE_HALF],
         gw_ball[:, RKV + 2 * ROPE_HALF:RKV + 3 * ROPE_HALF]], axis=1)
    late_names = ["kv_w_down", "kv_w_uk", "kv_w_uv", "b_w_in", "b_w_uq", "b_w_out"]
    late_w = [kv_w_down, kv_w_uk, kv_w_uv, b_w_in, b_w_uq, b_w_out]
    chip_major = [g_wd_full, gw_kv[:, :HV], gw_kv[:, HV:], shard_cols(gw_ball[:, WK:]), g_uq, gw_bout]
    late_grads = [g.reshape((N_CHIPS,) + w.shape) for g, w in zip(chip_major, late_w)]
    late_sems = _push_start("scatter_late_start", late_grads, by_target=True)

    dr0, dg0, db0 = _row_mm(
        "b_in_bwd", dp_segs, wb_all, nt=True, tm=t1024, tn=D, tk=tkb,
        outs=[((T, D), F32, 'tile'), ((1, D), F32, 'acc'), ((1, D), F32, 'acc')], epi=ln1_bwd_epi,
        epi_ins=[(dr1, 'tile'), (xhat1, 'tile'), (rstd1, 'row'), (g0 + late_sems[4][0, 0], 'col')])

    def conv_branch_bwd_epi(acc, ins, i, j):
        u1_t, z, g, b = ins
        xhat, rstd = _ln_stats(u1_t)
        u2 = xhat * g + b
        gate, gate_grad = _silu_and_grad(z)
        act, act_grad = _silu_and_grad(u2)
        dz_a = acc * act * gate_grad
        du1, dg, db = _ln_bwd(acc * gate * act_grad, xhat, rstd, g)
        return [du1, dz_a, dg, db]

    du1, dz_a, dng, dnb = _row_mm(
        "a_out_bwd", [((dr0,), None)], w_out, nt=True, tm=tmw, tn=E, tk=_fit(1024, D),
        outs=[((T, E), F32, 'tile'), ((T, E), BF16, 'tile'), ((1, E), F32, 'acc'), ((1, E), F32, 'acc')],
        epi=conv_branch_bwd_epi,
        epi_ins=[(u1, 'tile'), (proj, pl.BlockSpec((tmw, E), lambda i, j, k: (i, 2))), (norm_g, 'col'),
                 (norm_b, 'col')])
    gw_out, dbo = _tn_mm("dw_a_out", (u4,), None, [((dr0,), None)], tn=_fit(1024, D), tk=t1024,
                         out_dtype=BF16, colsum=True)
    mid_sems = _push_start("scatter_mid_start", [gw_out.reshape((N_CHIPS,) + a_w_out.shape)], by_target=True)
    dval, dgate, dcw, dcb = _conv_bwd(du1, proj, conv_w + mid_sems[4][0, 0], E, tmw)
    dproj_segs = [((dval,), None), ((dgate,), None), ((dz_a,), None)]
    gw_in, dbi = _tn_mm("dw_a_in", (xs,), None, dproj_segs, tn=_fit(2048, E), tk=t512, out_dtype=BF16,
                        colsum=True)

    chip_word = chip.reshape(1).astype(jnp.int32)

    def reduce_and_update(tag, names_, sent, landed, w_, m_, v_, by_target=True):
        sums = [_sum_chips("sum_" + n, l, s, chip_word, by_target) for n, s, l in zip(names_, sent, landed)]
        theirs = _swap_cores("swap_cores_" + tag, sums)
        return {n: _adamw("adamw_" + n, [mine, other], w, m, v)
                for n, mine, other, w, m, v in zip(names_, sums, theirs, w_, m_, v_)}

    late_sent, late_landed = _push_wait("scatter_late_wait", *late_sems[:4], after=[dbi], by_target=True)
    mid_sent, mid_landed = _push_wait("scatter_mid_wait", *mid_sems[:4], after=[dbi], by_target=True)
    early_sems = _push_start("scatter_early_start", [gw_in.reshape(a_w_in.shape[:-1] + (3 * E,))],
                             by_target='cols')
    (grad_x,) = _row_mm(
        "a_in_bwd", dproj_segs, g_win, nt=True, tm=t512, tn=D, tk=E, b_whole=True,
        outs=[((T, D), F32, 'tile')], epi=lambda acc, ins, i, j: [alpha * ins[0] + acc + ins[1]],
        epi_ins=[(dr0, 'tile'), (jnp.zeros((1, D), F32) + early_sems[4][0, 0], 'col')])
    big_out = reduce_and_update(
        "late", ["a_w_out"] + late_names, mid_sent + late_sent, mid_landed + late_landed,
        [a_w_out] + late_w, [m_a_w_out, m_kv_w_down, m_kv_w_uk, m_kv_w_uv, m_b_w_in, m_b_w_uq, m_b_w_out],
        [v_a_w_out, v_kv_w_down, v_kv_w_uk, v_kv_w_uv, v_b_w_in, v_b_w_uq, v_b_w_out])

    small_full = [jnp.concatenate([dg0, dg1]), jnp.concatenate([db0, db1]), dbi, dcw, dcb, dng, dnb, dbo,
                  dkvg, dqg]
    sflat = jnp.concatenate([g.reshape(-1) for g in small_full])
    summed = _all_reduce_small(_pack_rows(sflat, F32, 8 * LANES)).reshape(-1)
    soff = 0
    sgrads = []
    for g in small_full:
        sgrads.append(summed[soff:soff + g.size].reshape(g.shape))
        soff += g.size
    local_cols = lambda g, n: lax.dynamic_slice_in_dim(g, chip * n, n, axis=g.ndim - 1)
    snames = ["ln_g", "ln_b", "a_b_in", "a_conv_w", "a_conv_b", "a_norm_g", "a_norm_b", "a_b_out",
              "kv_norm_g", "b_q_norm_g"]
    small_w = [ln_g, ln_b, a_b_in, a_conv_w, a_conv_b, a_norm_g, a_norm_b, a_b_out, kv_norm_g, b_q_norm_g]
    small_m = [m_ln_g, m_ln_b, m_a_b_in, m_a_conv_w, m_a_conv_b, m_a_norm_g, m_a_norm_b, m_a_b_out,
               m_kv_norm_g, m_b_q_norm_g]
    small_v = [v_ln_g, v_ln_b, v_a_b_in, v_a_conv_w, v_a_conv_b, v_a_norm_g, v_a_norm_b, v_a_b_out,
               v_kv_norm_g, v_b_q_norm_g]
    sharded = {"a_b_in", "a_conv_w", "a_conv_b", "a_norm_g", "a_norm_b", "a_b_out"}
    local_g = [(local_cols(g, w.shape[-1]) if n in sharded else g).reshape(w.shape)
               for n, g, w in zip(snames, sgrads, small_w)]
    at_least_2d = lambda a: a.reshape((1,) + a.shape) if a.ndim == 1 else a
    sres = _adamw_vectors([tuple(at_least_2d(a) for a in item)
                           for item in zip(local_g, small_w, small_m, small_v)])
    small_out = {n: [g] + [r.reshape(w.shape) for r in res]
                 for n, g, w, res in zip(snames, local_g, small_w, sres)}

    early_sent, early_landed = _push_wait(
        "scatter_early_wait", *early_sems[:4], by_target='cols',
        after=[grad_x, big_out["b_w_out"][1], small_out["b_q_norm_g"][1]])
    big_out.update(reduce_and_update("early", ["a_w_in"], early_sent, early_landed,
                                     [a_w_in], [m_a_w_in], [v_a_w_in], by_target='cols'))

    loss = lax.psum(loss_part[0, 0], ("x", "y", "c"))
    order = ["ln_g", "ln_b", "a_w_in", "a_b_in", "a_conv_w", "a_conv_b", "a_norm_g", "a_norm_b", "a_w_out",
             "a_b_out", "kv_w_down", "kv_norm_g", "kv_w_uk", "kv_w_uv", "b_w_in", "b_q_norm_g", "b_w_uq",
             "b_w_out"]
    outs = {**big_out, **small_out}
    result = [loss, grad_x[None]]
    for part in range(4):
        result += [outs[n][part] for n in order]
    return tuple(result)
```

```python
import functools
import math

import jax
import jax.numpy as jnp
from jax import lax
from jax.experimental import pallas as pl
from jax.experimental.pallas import tpu as pltpu

F32, BF16 = jnp.float32, jnp.bfloat16
NN = (((1,), (0,)), ((), ()))
NT = (((1,), (1,)), ((), ()))
TN = (((0,), (0,)), ((), ()))
MESH = pl.DeviceIdType.MESH
ANY = pl.BlockSpec(memory_space=pl.ANY)

LANES = 128
BF16_ROWS = 16
VMEM_LIMIT = 56 * 1024 * 1024
N_CHIPS = 4
LN_EPS = 1e-5
RMS_EPS = 1e-6
MASK_VALUE = -1e30
LOG2_E = math.log2(math.e)
ROPE_THETA = 10000.0
ROPE_DIM = 64
ROPE_HALF = ROPE_DIM // 2
ADAM_LR, ADAM_B1, ADAM_B2, ADAM_EPS, ADAM_WD, ADAM_STEP = 0.001, 0.9, 0.999, 1e-08, 0.01, 10

MAX_TILE = 2048
TM_WIDE = 256
TQ = 512
CONV_HALO = 32
CONV_LC = 1024
CONV_SUB = 256
SUBLANES = 8
CONV_RC = 32
ADAM_ROWS = 128


def _dot(a, b, dims):
    return lax.dot_general(a.astype(BF16), b.astype(BF16), dims, preferred_element_type=F32)


def _sig(x):
    return 0.5 * jnp.tanh(0.5 * x) + 0.5


def _params(n_axes):
    return pltpu.CompilerParams(dimension_semantics=("arbitrary",) * n_axes, vmem_limit_bytes=VMEM_LIMIT)


def _gcd(*v):
    return functools.reduce(math.gcd, v)


def _fit(want, dim):
    return math.gcd(min(want, MAX_TILE), dim)


def _row_mm(name, a_segs, b, *, nt, tm, tn, tk, outs, epi, epi_ins=(), b_whole=False):
    M = a_segs[0][0][0].shape[0]
    stacked = b.ndim == 3
    if stacked:
        n_blk = b.shape[2]
        N = b.shape[1] if nt else N_CHIPS * n_blk
        assert (nt and b_whole) or (not nt and tn == n_blk and tk == b.shape[1]), name
    else:
        N = b.shape[0] if nt else b.shape[1]
    nkb = [arrs[0].shape[1] // tk for arrs, _ in a_segs]
    koff = [sum(nkb[:s]) for s in range(len(nkb))]
    ni, nj, nk = M // tm, N // tn, sum(nkb)
    assert M % tm == 0 and N % tn == 0 and all(arrs[0].shape[1] % tk == 0 for arrs, _ in a_segs), name
    assert stacked or (b.shape[1] if nt else b.shape[0]) == nk * tk, name

    def spec_of(shape, kind):
        if isinstance(kind, pl.BlockSpec):
            return kind
        if kind == 'tile':
            return pl.BlockSpec((tm, tn), lambda i, j, k: (i, j))
        if kind == 'row':
            return pl.BlockSpec((tm, shape[1]), lambda i, j, k: (i, 0))
        if kind == 'col':
            return pl.BlockSpec((1, tn), lambda i, j, k: (0, j))
        assert kind == 'acc' and nj == 1, name
        return pl.BlockSpec(shape, lambda i, j, k: (0,) * len(shape))

    in_specs, operands = [], []
    for s, (arrs, _) in enumerate(a_segs):
        for arr in arrs:
            in_specs.append(pl.BlockSpec(
                (tm, tk), lambda i, j, k, s=s: (i, jnp.clip(k - koff[s], 0, nkb[s] - 1))))
            operands.append(arr)
    if b_whole:
        assert nt and nj == 1 and all(n == 1 for n in nkb), name
        in_specs.append(pl.BlockSpec(b.shape, lambda i, j, k: (0,) * b.ndim))
    elif stacked:
        in_specs.append(pl.BlockSpec((1, tk, tn), lambda i, j, k: (j, 0, 0)))
    else:
        in_specs.append(pl.BlockSpec((tn, tk), lambda i, j, k: (j, k)) if nt
                        else pl.BlockSpec((tk, tn), lambda i, j, k: (k, j)))
    operands.append(b)
    for arr, kind in epi_ins:
        in_specs.append(spec_of(arr.shape, kind))
        operands.append(arr)
    out_specs = [spec_of(shape, kind) for shape, _, kind in outs]
    out_shape = [jax.ShapeDtypeStruct(shape, dtype) for shape, dtype, _ in outs]
    n_seg_refs = [len(arrs) for arrs, _ in a_segs]

    def body(*refs):
        pos = 0
        seg_refs = []
        for n in n_seg_refs:
            seg_refs.append(refs[pos:pos + n])
            pos += n
        b_ref = refs[pos]
        e_refs = refs[pos + 1:pos + 1 + len(epi_ins)]
        o_refs = refs[pos + 1 + len(epi_ins):pos + 1 + len(epi_ins) + len(outs)]
        i, j, k = pl.program_id(0), pl.program_id(1), pl.program_id(2)

        def product(fn, rs, s=0):
            a = rs[0][...] if fn is None else fn(*[r[...] for r in rs])
            if b_whole and stacked:
                lo, hi, tot = koff[s] * tk, (koff[s] + 1) * tk, None
                for q in range(lo // n_blk, (hi - 1) // n_blk + 1):
                    c0, c1 = max(lo, q * n_blk), min(hi, (q + 1) * n_blk)
                    part = _dot(a[:, c0 - lo:c1 - lo], b_ref[q, :, c0 - q * n_blk:c1 - q * n_blk], NT)
                    tot = part if tot is None else tot + part
                return tot
            if b_whole:
                return _dot(a, b_ref[:, koff[s] * tk:(koff[s] + 1) * tk], NT)
            return _dot(a, b_ref[0] if stacked else b_ref[...], NT if nt else NN)

        def finish(acc):
            res = epi(acc, [r[...] for r in e_refs], i, j)
            for o_ref, (_, _, kind), r in zip(o_refs, outs, res):
                if isinstance(kind, str) and kind == 'acc':
                    @pl.when(i == 0)
                    def _(o_ref=o_ref, r=r):
                        o_ref[...] = r

                    @pl.when(i > 0)
                    def _(o_ref=o_ref, r=r):
                        o_ref[...] += r
                else:
                    o_ref[...] = r.astype(o_ref.dtype)

        if nk == 1:
            finish(product(a_segs[0][1], seg_refs[0]))
            return
        acc_ref = refs[-1]

        @pl.when(k == 0)
        def _():
            acc_ref[...] = jnp.zeros_like(acc_ref)

        for s, ((_, fn), rs) in enumerate(zip(a_segs, seg_refs)):
            def accumulate(fn=fn, rs=rs, s=s):
                acc_ref[...] += product(fn, rs, s)
            if len(a_segs) == 1:
                accumulate()
            else:
                pl.when(jnp.logical_and(k >= koff[s], k < koff[s] + nkb[s]))(accumulate)

        @pl.when(k == nk - 1)
        def _():
            finish(acc_ref[...])

    return pl.pallas_call(
        body, name=name, grid=(ni, nj, nk), in_specs=in_specs, out_specs=out_specs, out_shape=out_shape,
        scratch_shapes=[] if nk == 1 else [pltpu.VMEM((tm, tn), F32)], compiler_params=_params(3),
    )(*operands)


def _tn_mm(name, a_arrs, a_fn, b_segs, *, tn, tk, out_dtype, shard_major=False, colsum=False):
    T, M = a_arrs[0].shape
    nbj = [arrs[0].shape[1] // tn for arrs, _ in b_segs]
    joff = [sum(nbj[:s]) for s in range(len(nbj))]
    nj, nk = sum(nbj), T // tk
    N = nj * tn
    assert T % tk == 0 and all(arrs[0].shape[1] % tn == 0 for arrs, _ in b_segs), name

    in_specs = [pl.BlockSpec((tk, M), lambda j, k: (k, 0)) for _ in a_arrs]
    operands = list(a_arrs)
    for s, (arrs, _) in enumerate(b_segs):
        for arr in arrs:
            in_specs.append(pl.BlockSpec(
                (tk, tn), lambda j, k, s=s: (k, jnp.clip(j - joff[s], 0, nbj[s] - 1))))
            operands.append(arr)
    if shard_major:
        per = (N // N_CHIPS) // tn
        assert per * tn * N_CHIPS == N, name
        out_shape = [jax.ShapeDtypeStruct((N_CHIPS, M, N // N_CHIPS), out_dtype)]
        out_specs = [pl.BlockSpec((1, M, tn), lambda j, k: (j // per, 0, j % per))]
    else:
        out_shape = [jax.ShapeDtypeStruct((M, N), out_dtype)]
        out_specs = [pl.BlockSpec((M, tn), lambda j, k: (0, j))]
    if colsum:
        out_shape.append(jax.ShapeDtypeStruct((1, N), F32))
        out_specs.append(pl.BlockSpec((1, tn), lambda j, k: (0, j)))
    n_a = len(a_arrs)
    n_seg_refs = [len(arrs) for arrs, _ in b_segs]

    def body(*refs):
        a_refs = refs[:n_a]
        pos = n_a
        seg_refs = []
        for n in n_seg_refs:
            seg_refs.append(refs[pos:pos + n])
            pos += n
        o_ref = refs[pos]
        cs_ref = refs[pos + 1] if colsum else None
        acc_ref = refs[-1]
        j, k = pl.program_id(0), pl.program_id(1)

        @pl.when(k == 0)
        def _():
            acc_ref[...] = jnp.zeros_like(acc_ref)
            if colsum:
                cs_ref[...] = jnp.zeros_like(cs_ref)

        for s, ((_, fn), rs) in enumerate(zip(b_segs, seg_refs)):
            def accumulate(fn=fn, rs=rs):
                a = a_refs[0][...] if a_fn is None else a_fn(*[r[...] for r in a_refs])
                bt = rs[0][...] if fn is None else fn(*[r[...] for r in rs])
                acc_ref[...] += _dot(a, bt, TN)
                if colsum:
                    cs_ref[...] += jnp.sum(bt.astype(F32), axis=0, keepdims=True)
            if len(b_segs) == 1:
                accumulate()
            else:
                pl.when(jnp.logical_and(j >= joff[s], j < joff[s] + nbj[s]))(accumulate)

        @pl.when(k == nk - 1)
        def _():
            if shard_major:
                o_ref[0] = acc_ref[...].astype(o_ref.dtype)
            else:
                o_ref[...] = acc_ref[...].astype(o_ref.dtype)

    res = pl.pallas_call(
        body, name=name, grid=(nj, nk), in_specs=in_specs, out_specs=out_specs, out_shape=out_shape,
        scratch_shapes=[pltpu.VMEM((M, tn), F32)], compiler_params=_params(2),
    )(*operands)
    return res if colsum else res[0]


def _silu(z):
    return z * _sig(z)


def _silu_and_grad(z):
    s = _sig(z)
    return z * s, s * (1.0 + z * (1.0 - s))


def _gate(o, z):
    return o * _silu(z)


def _ln_stats(r):
    mu = jnp.mean(r, axis=1, keepdims=True)
    xc = r - mu
    var = jnp.mean(xc * xc, axis=1, keepdims=True)
    rstd = lax.rsqrt(var + LN_EPS)
    return xc * rstd, rstd


def _ln_bwd(dy, xhat, rstd, g):
    dxh = dy * g
    m1 = jnp.mean(dxh, axis=1, keepdims=True)
    m2 = jnp.mean(dxh * xhat, axis=1, keepdims=True)
    return (rstd * (dxh - m1 - xhat * m2), jnp.sum(dy * xhat, axis=0, keepdims=True),
            jnp.sum(dy, axis=0, keepdims=True))


def _rms_fwd(x, g):
    rstd = lax.rsqrt(jnp.mean(x * x, axis=1, keepdims=True) + RMS_EPS)
    return x * rstd * g


def _rms_bwd(dy, x, g):
    rstd = lax.rsqrt(jnp.mean(x * x, axis=1, keepdims=True) + RMS_EPS)
    xn = x * rstd
    dxn = dy * g
    return rstd * (dxn - xn * jnp.mean(dxn * xn, axis=1, keepdims=True)), jnp.sum(dy * xn, axis=0, keepdims=True)


def _rope(x, cos, sin, transpose=False):
    parts = []
    for g in range(x.shape[1] // LANES):
        xg = x[:, g * LANES:(g + 1) * LANES]
        if transpose:
            parts.append(xg * cos + pltpu.roll(xg * sin, LANES // 2, 1))
        else:
            parts.append(xg * cos + pltpu.roll(xg, LANES // 2, 1) * sin)
    return parts[0] if len(parts) == 1 else jnp.concatenate(parts, axis=1)


def _shifted_rows(window, rc):
    n = window.shape[0]
    for b in range(SUBLANES):
        rolled = window if b == 0 else pltpu.roll(window, n - b, 0)
        for a8 in range(0, n - rc - b + 1, SUBLANES):
            yield a8 + b, rolled[a8:a8 + rc]


def _conv_fwd(proj, conv_w, conv_b, E, tm):
    T = proj.shape[0]
    kc = conv_w.shape[0]
    lc, hb, rc = min(CONV_LC, E), CONV_HALO, min(CONV_RC, tm)
    nl, ni, ratio = E // lc, T // tm, tm // hb
    gate_off = E // lc

    sub = min(CONV_SUB, lc)
    base = hb - (kc - 1)

    def body(val_ref, gate_ref, valh_ref, gateh_ref, w_ref, cb_ref, u1_ref, ubuf):
        i = pl.program_id(1)
        ubuf[hb:, :] = val_ref[...] * _sig(gate_ref[...])
        halo = valh_ref[...] * _sig(gateh_ref[...])
        ubuf[0:hb, :] = jnp.where(i > 0, halo, 0.0)
        for l0 in range(0, lc, sub):
            ls = slice(l0, l0 + sub)
            for r0 in range(0, tm, rc):
                acc = jnp.zeros((rc, sub), F32) + cb_ref[:, ls]
                for off, rows in _shifted_rows(ubuf[r0:r0 + hb + rc, ls], rc):
                    if 0 <= off - base < kc:
                        acc += w_ref[off - base:off - base + 1, ls] * rows
                u1_ref[r0:r0 + rc, ls] = acc

    return pl.pallas_call(
        body, name="conv_fwd", grid=(nl, ni),
        in_specs=[
            pl.BlockSpec((tm, lc), lambda l, i: (i, l)),
            pl.BlockSpec((tm, lc), lambda l, i: (i, gate_off + l)),
            pl.BlockSpec((hb, lc), lambda l, i: (jnp.maximum(i * ratio - 1, 0), l)),
            pl.BlockSpec((hb, lc), lambda l, i: (jnp.maximum(i * ratio - 1, 0), gate_off + l)),
            pl.BlockSpec((kc, lc), lambda l, i: (0, l)),
            pl.BlockSpec((1, lc), lambda l, i: (0, l)),
        ],
        out_specs=pl.BlockSpec((tm, lc), lambda l, i: (i, l)),
        out_shape=jax.ShapeDtypeStruct((T, E), F32),
        scratch_shapes=[pltpu.VMEM((hb + tm, lc), F32)], compiler_params=_params(2),
    )(proj, proj, proj, proj, conv_w, conv_b)


def _conv_post(u1, proj, norm_g, norm_b, E, tm):
    T = u1.shape[0]

    def body(u1_ref, z_ref, g_ref, b_ref, u4_ref):
        xhat, _ = _ln_stats(u1_ref[...])
        u4_ref[...] = (_silu(xhat * g_ref[...] + b_ref[...]) * _silu(z_ref[...])).astype(BF16)

    return pl.pallas_call(
        body, name="conv_post", grid=(T // tm,),
        in_specs=[pl.BlockSpec((tm, E), lambda i: (i, 0)), pl.BlockSpec((tm, E), lambda i: (i, 2)),
                  pl.BlockSpec((1, E), lambda i: (0, 0)), pl.BlockSpec((1, E), lambda i: (0, 0))],
        out_specs=pl.BlockSpec((tm, E), lambda i: (i, 0)),
        out_shape=jax.ShapeDtypeStruct((T, E), BF16), compiler_params=_params(1),
    )(u1, proj, norm_g, norm_b)


def _conv_bwd(du1, proj, conv_w, E, tm):
    T = du1.shape[0]
    kc = conv_w.shape[0]
    lc, hb, rc = min(CONV_LC, E), CONV_HALO, min(CONV_RC, tm)
    nl, ni, ratio = E // lc, T // tm, tm // hb
    gate_off = E // lc
    last_halo = T // hb - 1

    sub = min(CONV_SUB, lc)
    base = hb - (kc - 1)

    def body(du_ref, dun_ref, val_ref, gate_ref, valh_ref, gateh_ref, w_ref,
             dval_ref, dgate_ref, dw_ref, db_ref, ubuf, dbuf, sbuf, dw_sc):
        i = pl.program_id(1)
        sbuf[...] = _sig(gate_ref[...])
        ubuf[hb:, :] = val_ref[...] * sbuf[...]
        halo = valh_ref[...] * _sig(gateh_ref[...])
        ubuf[0:hb, :] = jnp.where(i > 0, halo, 0.0)
        dbuf[0:tm, :] = du_ref[...]
        dbuf[tm:, :] = jnp.where(i < ni - 1, dun_ref[...], 0.0)

        @pl.when(i == 0)
        def _():
            dw_sc[...] = jnp.zeros_like(dw_sc)
            db_ref[...] = jnp.zeros_like(db_ref)

        db_ref[...] += jnp.sum(du_ref[...], axis=0, keepdims=True)
        for l0 in range(0, lc, sub):
            ls = slice(l0, l0 + sub)
            for r0 in range(0, tm, rc):
                dwin = dbuf[r0:r0 + rc + hb, ls]
                dchunk = dwin[0:rc]
                for off, rows in _shifted_rows(ubuf[r0:r0 + hb + rc, ls], rc):
                    k = off - base
                    if 0 <= k < kc:
                        prod = rows * dchunk
                        part = prod[0:SUBLANES]
                        for s8 in range(SUBLANES, rc, SUBLANES):
                            part = part + prod[s8:s8 + SUBLANES]
                        dw_sc[k, :, ls] += part
                acc = jnp.zeros((rc, sub), F32)
                for off, rows in _shifted_rows(dwin, rc):
                    k = (kc - 1) - off
                    if 0 <= k < kc:
                        acc += w_ref[k:k + 1, ls] * rows
                v, s = val_ref[r0:r0 + rc, ls], sbuf[r0:r0 + rc, ls]
                dval_ref[r0:r0 + rc, ls] = (acc * s).astype(BF16)
                dgate_ref[r0:r0 + rc, ls] = (acc * v * s * (1.0 - s)).astype(BF16)

        @pl.when(i == ni - 1)
        def _():
            for k in range(kc):
                dw_ref[k:k + 1, :] = jnp.sum(dw_sc[k], axis=0, keepdims=True)

    return pl.pallas_call(
        body, name="conv_bwd", grid=(nl, ni),
        in_specs=[
            pl.BlockSpec((tm, lc), lambda l, i: (i, l)),
            pl.BlockSpec((hb, lc), lambda l, i: (jnp.minimum((i + 1) * ratio, last_halo), l)),
            pl.BlockSpec((tm, lc), lambda l, i: (i, l)),
            pl.BlockSpec((tm, lc), lambda l, i: (i, gate_off + l)),
            pl.BlockSpec((hb, lc), lambda l, i: (jnp.maximum(i * ratio - 1, 0), l)),
            pl.BlockSpec((hb, lc), lambda l, i: (jnp.maximum(i * ratio - 1, 0), gate_off + l)),
            pl.BlockSpec((kc, lc), lambda l, i: (0, l)),
        ],
        out_specs=[pl.BlockSpec((tm, lc), lambda l, i: (i, l)), pl.BlockSpec((tm, lc), lambda l, i: (i, l)),
                   pl.BlockSpec((kc, lc), lambda l, i: (0, l)), pl.BlockSpec((1, lc), lambda l, i: (0, l))],
        out_shape=[jax.ShapeDtypeStruct((T, E), BF16), jax.ShapeDtypeStruct((T, E), BF16),
                   jax.ShapeDtypeStruct((kc, E), F32), jax.ShapeDtypeStruct((1, E), F32)],
        scratch_shapes=[pltpu.VMEM((hb + tm, lc), F32), pltpu.VMEM((tm + hb, lc), F32),
                        pltpu.VMEM((tm, lc), F32), pltpu.VMEM((kc, SUBLANES, lc), F32)],
        compiler_params=_params(2),
    )(du1, du1, proj, proj, proj, proj, conv_w)


def _attn_fwd(q_all, kv, kr, H, tq, scale):
    T = q_all.shape[0]
    nq = T // tq
    pair = 2
    W = pair * LANES
    assert H % pair == 0
    hp_n = H // pair

    def body(qn_ref, qr_ref, kn_ref, kr_ref, v_ref, o_ref, lse_ref, *scratch):
        qi = pl.program_id(1)
        chains = [scratch[4 * a:4 * a + 4] for a in range(pair)]
        lanes = [slice(a * LANES, (a + 1) * LANES) for a in range(pair)]
        groups = [slice(c * LANES, (c + 1) * LANES) for c in range(tq // LANES)]

        def fold(x, op):
            r = x[:, groups[0]]
            for gsl in groups[1:]:
                r = op(r, x[:, gsl])
            return r

        for _, m_sc, l_sc, acc_sc in chains:
            m_sc[...] = jnp.full_like(m_sc, MASK_VALUE)
            l_sc[...] = jnp.zeros_like(l_sc)
            acc_sc[...] = jnp.zeros_like(acc_sc)

        def scores(j, masked):
            rows = pl.ds(pl.multiple_of(j * tq, tq), tq)
            krope = kr_ref[rows, :]
            for a, (s_sc, m_sc, _, _) in enumerate(chains):
                q = jnp.concatenate([qn_ref[:, lanes[a]], qr_ref[:, lanes[a]]], axis=1)
                k = jnp.concatenate([kn_ref[rows, lanes[a]], krope], axis=1)
                s = _dot(q, k, NT) * (scale * LOG2_E)
                if masked:
                    row = lax.broadcasted_iota(jnp.int32, s.shape, 0)
                    col = lax.broadcasted_iota(jnp.int32, s.shape, 1)
                    s = jnp.where(col <= row, s, MASK_VALUE)
                s_sc[j] = s
                m_sc[...] = jnp.maximum(m_sc[...], fold(s, jnp.maximum))

        def two_per_trip(fn, count):
            def two(p, carry):
                fn(2 * p)
                fn(2 * p + 1)
                return carry

            lax.fori_loop(0, count // 2, two, 0)

            @pl.when(count % 2 == 1)
            def _():
                fn(count - 1)

        two_per_trip(functools.partial(scores, masked=False), qi)
        scores(qi, True)
        for _, m_sc, _, _ in chains:
            m_sc[...] = jnp.broadcast_to(jnp.max(m_sc[...], axis=1, keepdims=True), m_sc.shape)

        def weigh(j):
            rows = pl.ds(pl.multiple_of(j * tq, tq), tq)
            for a, (s_sc, m_sc, l_sc, acc_sc) in enumerate(chains):
                s, m = s_sc[j], m_sc[...]
                p = jnp.concatenate([jnp.exp2(s[:, gsl] - m) for gsl in groups], axis=1)
                l_sc[...] += fold(p, jnp.add)
                acc_sc[...] += _dot(p, v_ref[rows, lanes[a]], NN)

        two_per_trip(weigh, qi + 1)
        for a, (_, m_sc, l_sc, acc_sc) in enumerate(chains):
            l = jnp.sum(l_sc[...], axis=1, keepdims=True)
            o_ref[:, lanes[a]] = acc_sc[...] / l
            lse_ref[a] = m_sc[:, 0:1] * (1.0 / LOG2_E) + jnp.log(l)

    chain_scratch = [pltpu.VMEM((nq, tq, tq), F32), pltpu.VMEM((tq, LANES), F32), pltpu.VMEM((tq, LANES), F32),
                     pltpu.VMEM((tq, LANES), F32)]
    return pl.pallas_call(
        body, name="attn_fwd", grid=(hp_n, nq),
        in_specs=[pl.BlockSpec((tq, W), lambda hp, qi: (qi, hp)),
                  pl.BlockSpec((tq, W), lambda hp, qi: (qi, hp_n + hp)),
                  pl.BlockSpec((T, W), lambda hp, qi: (0, hp)),
                  pl.BlockSpec((T, LANES), lambda hp, qi: (0, 0)),
                  pl.BlockSpec((T, W), lambda hp, qi: (0, hp_n + hp))],
        out_specs=[pl.BlockSpec((tq, W), lambda hp, qi: (qi, hp)),
                   pl.BlockSpec((pair, tq, 1), lambda hp, qi: (hp, qi, 0))],
        out_shape=[jax.ShapeDtypeStruct((T, H * LANES), F32), jax.ShapeDtypeStruct((H, T, 1), F32)],
        scratch_shapes=chain_scratch * pair, compiler_params=_params(2),
    )(q_all, q_all, kv, kr, kv)


def _attn_bwd(q_all, kv, kr, do, o, lse, cos, sin, H, tq, scale):
    T = q_all.shape[0]
    nq = T // tq
    HV = H * LANES
    pair = 2
    tk2 = pair * tq
    ng = T // tk2
    assert ng * tk2 == T and pair == 2

    def body(qn_ref, qr_ref, kn_ref, kr_ref, v_ref, do_ref, o_ref, lse_ref, cos_ref, sin_ref,
             dqn_ref, dqr_ref, dkn_ref, dkr_ref, dv_ref, dq_sc, dk_sc, dv_sc):
        g = pl.program_id(1)

        @pl.when(g == 0)
        def _():
            dq_sc[...] = jnp.zeros_like(dq_sc)

        key_rows = [slice(c * tq, (c + 1) * tq) for c in range(pair)]

        def block(qi, modes):
            rows = pl.ds(pl.multiple_of(qi * tq, tq), tq)
            q = jnp.concatenate([qn_ref[rows, :], qr_ref[rows, :]], axis=1)
            dov = do_ref[rows, :]
            delta = jnp.sum(dov.astype(F32) * o_ref[rows, :], axis=1, keepdims=True)
            lse_q = lse_ref[0, rows, :]
            dq, dkv = None, []
            for kr_, masked in zip(key_rows, modes):
                if masked is None:
                    dkv.append(None)
                    continue
                k = jnp.concatenate([kn_ref[kr_, :], kr_ref[kr_, :]], axis=1)
                s = _dot(q, k, NT) * scale
                if masked:
                    row = lax.broadcasted_iota(jnp.int32, s.shape, 0)
                    col = lax.broadcasted_iota(jnp.int32, s.shape, 1)
                    s = jnp.where(col <= row, s, MASK_VALUE)
                p = jnp.exp(s - lse_q)
                dv = _dot(p, dov, TN)
                dp = _dot(dov, v_ref[kr_, :], NT)
                ds = (p * (dp - delta) * scale).astype(BF16)
                dkv.append((_dot(ds, q, TN), dv))
                part = _dot(ds, k, NN)
                dq = part if dq is None else dq + part
            return rows, dq, dkv

        rows_a, dq_a, (kv_a0, _) = block(pair * g, (True, None))
        rows_b, dq_b, (kv_b0, kv_b1) = block(pair * g + 1, (False, True))
        dk_sc[key_rows[0], :] = kv_a0[0] + kv_b0[0]
        dv_sc[key_rows[0], :] = kv_a0[1] + kv_b0[1]
        dk_sc[key_rows[1], :] = kv_b1[0]
        dv_sc[key_rows[1], :] = kv_b1[1]
        dq_sc[rows_a, :] += dq_a
        dq_sc[rows_b, :] += dq_b

        def below(trip, carry):
            for qi in (pair * g + pair + 2 * trip, pair * g + pair + 2 * trip + 1):
                rows, dq, dkv = block(qi, (False, False))
                for kr_, (dk, dv) in zip(key_rows, dkv):
                    dk_sc[kr_, :] += dk
                    dv_sc[kr_, :] += dv
                dq_sc[rows, :] += dq
            return carry

        lax.fori_loop(0, (nq - pair * g - pair) // 2, below, 0)
        dkn_ref[...] = dk_sc[:, :LANES].astype(BF16)
        dkr_ref[...] = dk_sc[:, LANES:]
        dv_ref[...] = dv_sc[...].astype(BF16)

        @pl.when(g == ng - 1)
        def _():
            dqn_ref[...] = dq_sc[:, :LANES].astype(BF16)
            dqr_ref[...] = _rope(dq_sc[:, LANES:], cos_ref[...], sin_ref[...], transpose=True).astype(BF16)

    whole = lambda col: pl.BlockSpec((T, LANES), col)
    tile = lambda col: pl.BlockSpec((tk2, LANES), col)
    return pl.pallas_call(
        body, name="attn_bwd", grid=(H, ng),
        in_specs=[whole(lambda h, g: (0, h)), whole(lambda h, g: (0, H + h)),
                  tile(lambda h, g: (g, h)), tile(lambda h, g: (g, 0)), tile(lambda h, g: (g, H + h)),
                  whole(lambda h, g: (0, h)), whole(lambda h, g: (0, h)),
                  pl.BlockSpec((1, T, 1), lambda h, g: (h, 0, 0)),
                  whole(lambda h, g: (0, 0)), whole(lambda h, g: (0, 0))],
        out_specs=[whole(lambda h, g: (0, h)), whole(lambda h, g: (0, h)),
                   tile(lambda h, g: (g, h)), tile(lambda h, g: (g, h)), tile(lambda h, g: (g, h))],
        out_shape=[jax.ShapeDtypeStruct((T, HV), BF16), jax.ShapeDtypeStruct((T, HV), BF16),
                   jax.ShapeDtypeStruct((T, HV), BF16), jax.ShapeDtypeStruct((T, HV), F32),
                   jax.ShapeDtypeStruct((T, HV), BF16)],
        scratch_shapes=[pltpu.VMEM((T, 2 * LANES), F32), pltpu.VMEM((tk2, 2 * LANES), F32),
                        pltpu.VMEM((tk2, LANES), F32)],
        compiler_params=_params(2),
    )(q_all, q_all, kv, kr, kv, do, o, lse, cos, sin)


def _key_rope_bwd(dkr_heads, cos, sin, H, tm):
    T, HV = dkr_heads.shape

    def body(dkr_ref, cos_ref, sin_ref, dk_ref):
        dk = dkr_ref[...]
        tot = dk[:, 0:LANES]
        for h in range(1, H):
            tot = tot + dk[:, h * LANES:(h + 1) * LANES]
        dk_ref[...] = _rope(tot, cos_ref[...], sin_ref[...], transpose=True)

    return pl.pallas_call(
        body, name="key_rope_bwd", grid=(T // tm,),
        in_specs=[pl.BlockSpec((tm, HV), lambda i: (i, 0)),
                  pl.BlockSpec((tm, LANES), lambda i: (i, 0)), pl.BlockSpec((tm, LANES), lambda i: (i, 0))],
        out_specs=pl.BlockSpec((tm, LANES), lambda i: (i, 0)),
        out_shape=jax.ShapeDtypeStruct((T, LANES), F32), compiler_params=_params(1),
    )(dkr_heads, cos, sin)


def _place():
    x, y, c = lax.axis_index("x"), lax.axis_index("y"), lax.axis_index("c")
    chips = [(1 - x, y), (x, 1 - y), (1 - x, 1 - y)]
    return x, y, c, chips


def _all_gather_chips(arrs):
    n = len(arrs)
    halves = [a.shape[0] // 2 for a in arrs]
    assert all(h * 2 == a.shape[0] and h % BF16_ROWS == 0 for h, a in zip(halves, arrs))

    def body(*refs):
        w_refs, out_refs = refs[:n], refs[n:2 * n]
        send_sems, recv_sems, local_sems = refs[2 * n:]
        x, y, c, chips = _place()
        sibling = (x, y, 1 - c)
        waits = []
        for w, (w_ref, out_ref, half) in enumerate(zip(w_refs, out_refs, halves)):
            def region(px, py, pc, out_ref=out_ref, half=half):
                return out_ref.at[2 * px + py, pl.ds(pc * half, half), :]

            def copy(k, block, to, src=None, w=w, region=region):
                return pltpu.make_async_remote_copy(
                    src_ref=region(*block) if src is None else src, dst_ref=region(*block),
                    send_sem=send_sems.at[6 * w + k], recv_sem=recv_sems.at[6 * w + k],
                    device_id=to, device_id_type=MESH)

            mine = pltpu.make_async_copy(w_ref, out_ref.at[2 * x + y], local_sems.at[w])
            mine.start()
            my_half = w_ref.at[pl.ds(c * half, half), :]
            first = [copy(j, (x, y, c), (*chip, c), src=my_half) for j, chip in enumerate(chips)]
            for cp in first:
                cp.start()
            waits.append((copy, mine, first))
        for copy, mine, first in waits:
            passed = [copy(3 + j, (*chip, c), sibling) for j, chip in enumerate(chips)]
            for j, chip in enumerate(chips):
                copy(j, (*chip, c), (x, y, c)).wait_recv()
                passed[j].start()
            for j, chip in enumerate(chips):
                copy(3 + j, (*chip, 1 - c), (x, y, c)).wait_recv()
            for cp in first + passed:
                cp.wait_send()
            mine.wait()

    return pl.pallas_call(
        body, name="gather_weights", in_specs=[ANY] * n, out_specs=[ANY] * n,
        out_shape=[jax.ShapeDtypeStruct((N_CHIPS,) + a.shape, a.dtype) for a in arrs],
        scratch_shapes=[pltpu.SemaphoreType.DMA((6 * n,)), pltpu.SemaphoreType.DMA((6 * n,)),
                        pltpu.SemaphoreType.DMA((n,))],
    )(*arrs)


def _swap_cores(name, parts):
    n = len(parts)

    def body(*refs):
        p_refs, r_refs = refs[:n], refs[n:2 * n]
        send_sems, recv_sems = refs[2 * n:]
        x, y, c, _ = _place()
        copies = [pltpu.make_async_remote_copy(
            src_ref=p_refs[w], dst_ref=r_refs[w], send_sem=send_sems.at[w], recv_sem=recv_sems.at[w],
            device_id=(x, y, 1 - c), device_id_type=MESH) for w in range(n)]
        for cp in copies:
            cp.start()
        for cp in copies:
            cp.wait()

    return pl.pallas_call(
        body, name=name, in_specs=[ANY] * n, out_specs=[ANY] * n,
        out_shape=[jax.ShapeDtypeStruct(p.shape, p.dtype) for p in parts],
        scratch_shapes=[pltpu.SemaphoreType.DMA((n,)), pltpu.SemaphoreType.DMA((n,))],
    )(*parts)


HBM = pl.BlockSpec(memory_space=pltpu.HBM)
SEM = pl.BlockSpec(memory_space=pltpu.SEMAPHORE)
EFFECT = pltpu.SideEffectType.DATAFLOW_SIDE_EFFECTING


def _push_copies(a_refs, l_refs, send_sems, recv_sems, by_target):
    x, y, c, chips = _place()
    me = 2 * x + y

    def part(a_ref, q):
        if by_target == 'cols':
            n = a_ref.shape[-1] // N_CHIPS
            return a_ref.at[(slice(None),) * (len(a_ref.shape) - 1) + (pl.ds(pl.multiple_of(q * n, LANES), n),)]
        return a_ref.at[q] if by_target else a_ref

    out = []
    for w, (a_ref, l_ref) in enumerate(zip(a_refs, l_refs)):
        for j, (px, py) in enumerate(chips):
            peer = 2 * px + py
            out.append((
                pltpu.make_async_remote_copy(
                    src_ref=part(a_ref, peer), dst_ref=l_ref.at[me],
                    send_sem=send_sems.at[3 * w + j], recv_sem=recv_sems.at[3 * w + j],
                    device_id=(px, py, c), device_id_type=MESH),
                pltpu.make_async_remote_copy(
                    src_ref=part(a_ref, me), dst_ref=l_ref.at[peer],
                    send_sem=send_sems.at[3 * w + j], recv_sem=recv_sems.at[3 * w + j],
                    device_id=(px, py, c), device_id_type=MESH)))
    return out


def _landing_shape(a, by_target):
    if by_target == 'cols':
        return (N_CHIPS,) + a.shape[:-1] + (a.shape[-1] // N_CHIPS,)
    return (N_CHIPS,) + (a.shape[1:] if by_target else a.shape)


def _push_start(name, arrs, by_target):
    n = len(arrs)
    lands = [lax.empty(_landing_shape(a, by_target), a.dtype) for a in arrs]

    def body(*refs):
        a_refs, l_refs = refs[:n], refs[n:2 * n]
        send_sems, recv_sems = refs[2 * n], refs[2 * n + 1]
        token = refs[-1]
        for send, _ in _push_copies(a_refs, l_refs, send_sems, recv_sems, by_target):
            send.start()
        token[...] = jnp.zeros_like(token)

    res = pl.pallas_call(
        body, name=name,
        out_shape=(pltpu.SemaphoreType.DMA((3 * n,)), pltpu.SemaphoreType.DMA((3 * n,)),
                   *[pltpu.HBM(a.shape, a.dtype) for a in arrs], *[pltpu.HBM(l.shape, l.dtype) for l in lands],
                   jax.ShapeDtypeStruct((8, LANES), F32)),
        in_specs=[HBM] * (2 * n), out_specs=(SEM, SEM, *[HBM] * (2 * n), pl.BlockSpec(memory_space=pltpu.VMEM)),
        input_output_aliases={i: 2 + i for i in range(2 * n)},
        compiler_params=pltpu.CompilerParams(has_side_effects=EFFECT),
    )(*[pltpu.with_memory_space_constraint(a, pltpu.HBM) for a in list(arrs) + lands])
    return res[0], res[1], list(res[2:2 + n]), list(res[2 + n:2 + 2 * n]), res[-1]


def _push_wait(name, send_sems, recv_sems, arrs, lands, after, by_target):
    n = len(arrs)

    def body(*refs):
        a_refs, l_refs = refs[:n], refs[n:2 * n]
        s_sems, r_sems = refs[2 * n], refs[2 * n + 1]
        for send, recv in _push_copies(a_refs, l_refs, s_sems, r_sems, by_target):
            send.wait_send()
            recv.wait_recv()

    res = pl.pallas_call(
        body, name=name,
        out_shape=[pltpu.HBM(a.shape, a.dtype) for a in list(arrs) + list(lands)],
        in_specs=[HBM] * (2 * n) + [SEM, SEM] + [ANY] * len(after), out_specs=[HBM] * (2 * n),
        input_output_aliases={i: i for i in range(2 * n)},
        compiler_params=pltpu.CompilerParams(has_side_effects=EFFECT),
    )(*arrs, *lands, send_sems, recv_sems, *after)
    return list(res[:n]), list(res[n:])


def _all_reduce_small(part):
    def body(p_ref, out_ref, sib_buf, chip_buf, send_sems, recv_sems):
        x, y, c, chips = _place()
        me = 2 * x + y
        swap = pltpu.make_async_remote_copy(
            src_ref=p_ref, dst_ref=sib_buf, send_sem=send_sems.at[0], recv_sem=recv_sems.at[0],
            device_id=(x, y, 1 - c), device_id_type=MESH)
        swap.start()
        swap.wait()
        chip_buf[me] = p_ref[...] + sib_buf[...]
        copies = []
        for j, (px, py) in enumerate(chips):
            cp = pltpu.make_async_remote_copy(
                src_ref=chip_buf.at[me], dst_ref=chip_buf.at[me], send_sem=send_sems.at[1 + j],
                recv_sem=recv_sems.at[1 + j], device_id=(px, py, c), device_id_type=MESH)
            cp.start()
            copies.append(cp)
        for j, (px, py) in enumerate(chips):
            pltpu.make_async_remote_copy(
                src_ref=chip_buf.at[me], dst_ref=chip_buf.at[2 * px + py], send_sem=send_sems.at[1 + j],
                recv_sem=recv_sems.at[1 + j], device_id=(px, py, c), device_id_type=MESH).wait_recv()
        for cp in copies:
            cp.wait_send()
        tot = chip_buf[0]
        for q in range(1, N_CHIPS):
            tot = tot + chip_buf[q]
        out_ref[...] = tot

    vm = pl.BlockSpec(memory_space=pltpu.VMEM)
    return pl.pallas_call(
        body, name="all_reduce_small", in_specs=[vm], out_specs=vm,
        out_shape=jax.ShapeDtypeStruct(part.shape, F32),
        scratch_shapes=[pltpu.VMEM(part.shape, F32), pltpu.VMEM((N_CHIPS,) + part.shape, F32),
                        pltpu.SemaphoreType.DMA((N_CHIPS,)), pltpu.SemaphoreType.DMA((N_CHIPS,))],
    )(part)


def _row_tiles(shape):
    ax = next(d for d, s in enumerate(shape) if s > 1)
    tr = _gcd(ADAM_ROWS, shape[ax])
    block = tuple(tr if d == ax else s for d, s in enumerate(shape))
    return shape[ax] // tr, block, lambda i: tuple(i if d == ax else 0 for d in range(len(shape)))


def _sum_chips(name, landed, sent, chip, by_target):
    shape = landed.shape[1:]
    steps, block, index = _row_tiles(shape)
    if by_target == 'cols':
        own_spec = pl.BlockSpec(block, lambda i, c: index(i)[:-1] + (c[0],))
    else:
        own_spec = pl.BlockSpec((1,) + block, lambda i, c: (c[0],) + index(i))

    def body(chip_ref, l_ref, s_ref, o_ref):
        own = (s_ref[...] if by_target == 'cols' else s_ref[0]).astype(F32)
        tot = None
        for q in range(N_CHIPS):
            term = jnp.where(chip_ref[0] == q, own, l_ref[q].astype(F32))
            tot = term if tot is None else tot + term
        o_ref[...] = tot

    return pl.pallas_call(
        body, name=name,
        grid_spec=pltpu.PrefetchScalarGridSpec(
            num_scalar_prefetch=1, grid=(steps,),
            in_specs=[pl.BlockSpec((N_CHIPS,) + block, lambda i, c: (0,) + index(i)), own_spec],
            out_specs=pl.BlockSpec(block, lambda i, c: index(i))),
        out_shape=jax.ShapeDtypeStruct(shape, F32), compiler_params=_params(1),
    )(chip, landed, sent)


def _adamw_math(g, w, m, v):
    mn = ADAM_B1 * m + (1.0 - ADAM_B1) * g
    vn = ADAM_B2 * v + (1.0 - ADAM_B2) * jnp.square(g)
    m_hat = mn / (1.0 - ADAM_B1 ** ADAM_STEP)
    v_hat = vn / (1.0 - ADAM_B2 ** ADAM_STEP)
    return -ADAM_LR * (m_hat / (jnp.sqrt(v_hat) + ADAM_EPS) + ADAM_WD * w), mn, vn


def _adamw(name, g_parts, w, m, v):
    steps, block, index = _row_tiles(w.shape)
    n = len(g_parts)

    def body(*refs):
        g = refs[0][...]
        for r in refs[1:n]:
            g = g + r[...]
        w_ref, m_ref, v_ref, go_ref, d_ref, mo_ref, vo_ref = refs[n:]
        go_ref[...] = g
        d_ref[...], mo_ref[...], vo_ref[...] = _adamw_math(g, w_ref[...], m_ref[...], v_ref[...])

    spec = pl.BlockSpec(block, index)
    return pl.pallas_call(
        body, name=name, grid=(steps,), in_specs=[spec] * (n + 3), out_specs=[spec] * 4,
        out_shape=[jax.ShapeDtypeStruct(w.shape, F32)] * 4, compiler_params=_params(1),
    )(*g_parts, w, m, v)


def _adamw_vectors(items):
    n = len(items)

    def body(*refs):
        ins, outs = refs[:4 * n], refs[4 * n:]
        for k in range(n):
            g, w, m, v = (r[...] for r in ins[4 * k:4 * k + 4])
            outs[3 * k][...], outs[3 * k + 1][...], outs[3 * k + 2][...] = _adamw_math(g, w, m, v)

    vm = pl.BlockSpec(memory_space=pltpu.VMEM)
    res = pl.pallas_call(
        body, name="adamw_vectors", in_specs=[vm] * (4 * n), out_specs=[vm] * (3 * n),
        out_shape=[jax.ShapeDtypeStruct(it[1].shape, F32) for it in items for _ in range(3)],
    )(*[a for it in items for a in it])
    return [res[3 * k:3 * k + 3] for k in range(n)]


def _pack_rows(flat, dtype, multiple):
    n = flat.shape[0]
    total = -(-n // multiple) * multiple
    return jnp.pad(flat, (0, total - n)).astype(dtype).reshape(total // LANES, LANES)


def kernel(x, positions, ln_g, ln_b, a_w_in, a_b_in, a_conv_w, a_conv_b, a_norm_g, a_norm_b, a_w_out, a_b_out, kv_w_down, kv_norm_g, kv_w_uk, kv_w_uv, b_w_in, b_q_norm_g, b_w_uq, b_w_out, loss_target, m_ln_g, m_ln_b, m_a_w_in, m_a_b_in, m_a_conv_w, m_a_conv_b, m_a_norm_g, m_a_norm_b, m_a_w_out, m_a_b_out, m_kv_w_down, m_kv_norm_g, m_kv_w_uk, m_kv_w_uv, m_b_w_in, m_b_q_norm_g, m_b_w_uq, m_b_w_out, v_ln_g, v_ln_b, v_a_w_in, v_a_b_in, v_a_conv_w, v_a_conv_b, v_a_norm_g, v_a_norm_b, v_a_w_out, v_a_b_out, v_kv_w_down, v_kv_norm_g, v_kv_w_uk, v_kv_w_uv, v_b_w_in, v_b_q_norm_g, v_b_w_uq, v_b_w_out):
    T, D = x.shape[1], x.shape[2]
    E = N_CHIPS * a_w_out.shape[1]
    KC = a_conv_w.shape[1]
    RKV = kv_norm_g.shape[0]
    H, DN = kv_w_uk.shape[1], kv_w_uk.shape[2]
    RQ = b_q_norm_g.shape[1]
    HV = N_CHIPS * b_w_out.shape[1]
    assert DN == LANES and kv_w_uv.shape[2] == LANES and HV == H * LANES
    assert kv_w_down.shape[1] == RKV + ROPE_DIM and b_w_uq.shape[3] == DN + ROPE_DIM
    assert ln_g.shape[0] == 2 and a_w_in.shape[0] == 1 and b_w_in.shape[0] == 1
    alpha = (2.0 * ln_g.shape[0]) ** 0.25
    scale = 1.0 / math.sqrt(DN + ROPE_DIM)
    WK = -(-(RKV + LANES) // 256) * 256
    assert WK % RQ == 0
    Z_OFF = WK + RQ
    tmw, tq = min(TM_WIDE, T), min(TQ, T)
    t512, t1024 = _fit(512, T), _fit(1024, T)
    xs = x[0]
    tgt = loss_target[0]
    px, py = lax.axis_index("x"), lax.axis_index("y")
    chip = 2 * px + py

    mats = [a_w_out[0], kv_w_down, kv_w_uk, kv_w_uv, b_w_in[0], b_w_uq[0], b_w_out[0]]
    vecs = [a_b_in[0], a_conv_w[0], a_conv_b[0], a_norm_g[0], a_norm_b[0], a_b_out[0]]
    vec_bits = jnp.concatenate([lax.bitcast_convert_type(w.reshape(-1), BF16).reshape(-1) for w in vecs])
    rest = [w.astype(BF16) for w in mats]
    g_win, gathered = _all_gather_chips(
        [a_w_in[0].astype(BF16), _pack_rows(vec_bits, BF16, 2 * BF16_ROWS * LANES)])
    gathered = gathered.reshape(N_CHIPS, -1)
    gathered, rest = lax.optimization_barrier((gathered, rest))
    rest_sems = _push_start("gather_rest_start", rest, by_target=False)
    off = 0
    fvec = []
    for w in vecs:
        bits = gathered[:, off:off + 2 * w.size].reshape((N_CHIPS,) + w.shape + (2,))
        fvec.append(lax.bitcast_convert_type(bits, F32))
        off += 2 * w.size
    cols = lambda g: jnp.moveaxis(g, 0, -2).reshape(g.shape[1:-1] + (N_CHIPS * g.shape[-1],))
    b_in = cols(fvec[0][:, None, :])
    conv_w = cols(fvec[1])
    conv_b, norm_g, norm_b, b_out = (cols(f[:, None, :]) for f in fvec[2:])
    row = lambda a: a.reshape(1, -1)
    g0, b0, g1, b1 = row(ln_g[0]), row(ln_b[0]), row(ln_g[1]), row(ln_b[1])
    kv_g, q_g = row(kv_norm_g), row(b_q_norm_g[0])
    plain = lambda acc, ins, i, j: [acc]

    b_in = b_in + rest_sems[4][0, 0]
    (proj,) = _row_mm("a_in", [((xs,), None)], g_win, nt=False, tm=t1024, tn=3 * E // N_CHIPS, tk=D,
                      outs=[((T, 3 * E), F32, 'tile')], epi=lambda acc, ins, i, j: [acc + ins[0]],
                      epi_ins=[(b_in, 'col')])
    u1 = _conv_fwd(proj, conv_w, conv_b, E, tmw)
    u4 = _conv_post(u1, proj, norm_g, norm_b, E, tmw)

    rest, landed = _push_wait("gather_rest_wait", *rest_sems[:4], after=[u4], by_target=False)
    g_wout, g_wd, g_uk, g_uv, g_wbin, g_wuq, g_wbout = [
        lax.dynamic_update_slice(l, w[None], (chip,) + (0,) * w.ndim) for w, l in zip(rest, landed)]
    w_out = g_wout.reshape(E, D)
    wd = g_wd.reshape(D, RKV + ROPE_DIM)
    zpad = jnp.zeros((D, ROPE_HALF), BF16)
    wd_p = jnp.concatenate(
        [wd[:, :RKV], wd[:, RKV:RKV + ROPE_HALF], zpad, wd[:, RKV + ROPE_HALF:], zpad,
         jnp.zeros((D, WK - RKV - LANES), BF16)], axis=1)
    w_bin = cols(g_wbin)
    w_z = w_bin[:, RQ:]
    wb_small = jnp.concatenate([wd_p, w_bin[:, :RQ]], axis=1)
    wb_all = jnp.concatenate([wd_p, w_bin], axis=1)
    w_kv = jnp.concatenate([g_uk.reshape(RKV, HV), g_uv.reshape(RKV, HV)], axis=1)
    wuq = g_wuq.reshape(RQ, H, DN + ROPE_DIM)
    zq = jnp.zeros((RQ, H, ROPE_HALF), BF16)
    w_qr = jnp.concatenate([wuq[:, :, DN:DN + ROPE_HALF], zq, wuq[:, :, DN + ROPE_HALF:], zq], axis=2)
    w_q = jnp.concatenate([wuq[:, :, :DN].reshape(RQ, HV), w_qr.reshape(RQ, HV)], axis=1)
    w_bout = g_wbout.reshape(HV, D)

    freqs = ROPE_THETA ** (-jnp.arange(0, ROPE_DIM, 2, dtype=F32) / ROPE_DIM)
    ang = positions[0].astype(F32)[:, None] * freqs
    cs, sn = jnp.cos(ang), jnp.sin(ang)
    ones, zeros = jnp.ones_like(cs), jnp.zeros_like(cs)
    cos_t = jnp.concatenate([cs, ones, cs, ones], axis=1)
    sin_t = jnp.concatenate([-sn, zeros, sn, zeros], axis=1)

    def ln_epi(acc, ins, i, j):
        bias, res, g, b = ins
        xhat, rstd = _ln_stats(alpha * res + acc + bias)
        h = xhat * g + b
        return [h, h, xhat, rstd]

    h1, h1b, xhat1, rstd1 = _row_mm(
        "a_out", [((u4,), None)], w_out, nt=False, tm=t512, tn=D, tk=_fit(2048, E),
        outs=[((T, D), F32, 'tile'), ((T, D), BF16, 'tile'), ((T, D), F32, 'tile'), ((T, 1), F32, 'row')],
        epi=ln_epi, epi_ins=[(b_out, 'col'), (xs, 'tile'), (g0, 'col'), (b0, 'col')])

    tkb = _fit(512, _gcd(WK, RQ, HV))
    def latents_epi(acc, ins, i, j):
        kg, qg, cos_, sin_ = ins
        return [acc, _rms_fwd(acc[:, :RKV], kg), _rope(acc[:, RKV:RKV + LANES], cos_, sin_),
                _rms_fwd(acc[:, WK:WK + RQ], qg)]

    whole_row = lambda a: (a, pl.BlockSpec(a.shape, lambda i, j, k: (0, 0)))
    pb, c_lat, kr, cqn = _row_mm(
        "b_in", [((h1b,), None)], wb_small, nt=False, tm=t512, tn=Z_OFF, tk=_fit(1024, D),
        outs=[((T, Z_OFF), F32, 'tile'), ((T, RKV), BF16, 'row'), ((T, LANES), BF16, 'row'),
              ((T, RQ), BF16, 'row')],
        epi=latents_epi, epi_ins=[whole_row(kv_g), whole_row(q_g), (cos_t, 'row'), (sin_t, 'row')])
    (zb,) = _row_mm("b_in_gate", [((h1b,), None)], w_z, nt=False, tm=t1024, tn=_fit(2048, HV),
                    tk=_fit(1024, D), outs=[((T, HV), F32, 'tile')], epi=plain)
    (kv,) = _row_mm("kv_up", [((c_lat,), None)], w_kv, nt=False, tm=t1024, tn=_fit(2048, HV),
                    tk=_fit(1024, RKV), outs=[((T, 2 * HV), BF16, 'tile')], epi=plain)
    tnq = _fit(2048, HV)
    half_q = HV // tnq

    def q_epi(acc, ins, i, j):
        return [jnp.where(j >= half_q, _rope(acc, ins[0], ins[1]), acc)]

    (q_all,) = _row_mm("q_up", [((cqn,), None)], w_q, nt=False, tm=t1024, tn=tnq, tk=_fit(1024, RQ),
                       outs=[((T, 2 * HV), BF16, 'tile')], epi=q_epi,
                       epi_ins=[(cos_t, 'row'), (sin_t, 'row')])
    o, lse = _attn_fwd(q_all, kv, kr, H, tq, scale)

    def loss_epi(acc, ins, i, j):
        res, g, b, target = ins
        xhat, rstd = _ln_stats(alpha * res + acc)
        diff = xhat * g + b - target
        dr, dg, db = _ln_bwd(diff / D, xhat, rstd, g)
        return [dr, 0.5 * jnp.sum(diff * diff, keepdims=True) / D, dg, db]

    dr1, loss_part, dg1, db1 = _row_mm(
        "b_out", [((o, zb), _gate)], w_bout, nt=False, tm=tmw, tn=D, tk=_fit(2048, HV),
        outs=[((T, D), F32, 'tile'), ((1, 1), F32, 'acc'), ((1, D), F32, 'acc'), ((1, D), F32, 'acc')],
        epi=loss_epi, epi_ins=[(h1, 'tile'), (g1, 'col'), (b1, 'col'), (tgt, 'tile')])

    def gate_bwd_epi(acc, ins, i, j):
        gate, gate_grad = _silu_and_grad(ins[1])
        return [acc * gate, acc * ins[0] * gate_grad]

    do, dz = _row_mm(
        "b_out_bwd", [((dr1,), None)], w_bout, nt=True, tm=tmw, tn=_fit(2048, HV), tk=_fit(1024, D),
        outs=[((T, HV), BF16, 'tile'), ((T, HV), BF16, 'tile')], epi=gate_bwd_epi,
        epi_ins=[(o, 'tile'), (zb, 'tile')])
    gw_bout = _tn_mm("dw_b_out", (o, zb), _gate, [((dr1,), None)], tn=_fit(1024, D), tk=t512, out_dtype=BF16)
    dqn, dqr_pre, dkn, dkr_h, dv = _attn_bwd(q_all, kv, kr, do, o, lse, cos_t, sin_t, H, tq, scale)
    dkr_pre = _key_rope_bwd(dkr_h, cos_t, sin_t, H, tmw)

    def cq_bwd_epi(acc, ins, i, j):
        dx, dg = _rms_bwd(acc, ins[0], ins[1])
        return [dx, dg]

    dcq, dqg = _row_mm(
        "q_up_bwd", [((dqn,), None), ((dqr_pre,), None)], w_q, nt=True, tm=t1024, tn=RQ, tk=_fit(2048, HV),
        outs=[((T, RQ), BF16, 'tile'), ((1, RQ), F32, 'acc')], epi=cq_bwd_epi,
        epi_ins=[(pb, pl.BlockSpec((t1024, RQ), lambda i, j, k: (i, WK // RQ))), (q_g, 'col')])
    gw_q = _tn_mm("dw_q_up", (cqn,), None, [((dqn,), None), ((dqr_pre,), None)],
                  tn=_fit(2048, HV), tk=t1024, out_dtype=BF16)

    def ckv_bwd_epi(acc, ins, i, j):
        blk, dkr_t, g = ins
        dx, dg = _rms_bwd(acc, blk[:, :RKV], g)
        parts = [dx, dkr_t]
        if WK > RKV + LANES:
            parts.append(jnp.zeros((dx.shape[0], WK - RKV - LANES), F32))
        return [jnp.concatenate(parts, axis=1), dg]

    dckv, dkvg = _row_mm(
        "kv_up_bwd", [((dkn,), None), ((dv,), None)], w_kv, nt=True, tm=t1024, tn=RKV, tk=_fit(2048, HV),
        outs=[((T, WK), BF16, pl.BlockSpec((t1024, WK), lambda i, j, k: (i, 0))), ((1, RKV), F32, 'acc')],
        epi=ckv_bwd_epi,
        epi_ins=[(pb, pl.BlockSpec((t1024, WK), lambda i, j, k: (i, 0))), (dkr_pre, 'row'), (kv_g, 'col')])
    gw_kv = _tn_mm("dw_kv_up", (c_lat,), None, [((dkn,), None), ((dv,), None)],
                   tn=_fit(2048, HV), tk=t1024, out_dtype=BF16)

    def ln1_bwd_epi(acc, ins, i, j):
        dr_up, xhat, rstd, g = ins
        dr, dg, db = _ln_bwd(alpha * dr_up + acc, xhat, rstd, g)
        return [dr, dg, db]

    dp_segs = [((dckv,), None), ((dcq,), None), ((dz,), None)]
    gw_ball = _tn_mm("dw_b_in", (h1b,), None, dp_segs, tn=tkb, tk=t1024, out_dtype=BF16)

    shard_cols = lambda g: jnp.moveaxis(g.reshape(g.shape[0], N_CHIPS, -1), 1, 0)
    shard_rows = lambda g: g.reshape(N_CHIPS, g.shape[0] // N_CHIPS, g.shape[1])
    gq = gw_q.reshape(RQ, 2, H, LANES)
    g_uq = jnp.concatenate(
        [gq[:, 0], gq[:, 1, :, :ROPE_HALF], gq[:, 1, :, 2 * ROPE_HALF:3 * ROPE_HALF]], axis=2)
    g_wd_full = jnp.concatenate(
        [gw_ball[:, :RKV], gw_ball[:, RKV:RKV + ROPE_HALF],
         gw_ball[:, RKV + 2 * ROPE_HALF:RKV + 3 * ROPE_HALF]], axis=1)
    late_names = ["kv_w_down", "kv_w_uk", "kv_w_uv", "b_w_in", "b_w_uq", "b_w_out"]
    late_w = [kv_w_down, kv_w_uk, kv_w_uv, b_w_in, b_w_uq, b_w_out]
    chip_major = [g_wd_full, gw_kv[:, :HV], gw_kv[:, HV:], shard_cols(gw_ball[:, WK:]), g_uq, gw_bout]
    late_grads = [g.reshape((N_CHIPS,) + w.shape) for g, w in zip(chip_major, late_w)]
    late_sems = _push_start("scatter_late_start", late_grads, by_target=True)

    dr0, dg0, db0 = _row_mm(
        "b_in_bwd", dp_segs, wb_all, nt=True, tm=t1024, tn=D, tk=tkb,
        outs=[((T, D), F32, 'tile'), ((1, D), F32, 'acc'), ((1, D), F32, 'acc')], epi=ln1_bwd_epi,
        epi_ins=[(dr1, 'tile'), (xhat1, 'tile'), (rstd1, 'row'), (g0 + late_sems[4][0, 0], 'col')])

    def conv_branch_bwd_epi(acc, ins, i, j):
        u1_t, z, g, b = ins
        xhat, rstd = _ln_stats(u1_t)
        u2 = xhat * g + b
        gate, gate_grad = _silu_and_grad(z)
        act, act_grad = _silu_and_grad(u2)
        dz_a = acc * act * gate_grad
        du1, dg, db = _ln_bwd(acc * gate * act_grad, xhat, rstd, g)
        return [du1, dz_a, dg, db]

    du1, dz_a, dng, dnb = _row_mm(
        "a_out_bwd", [((dr0,), None)], w_out, nt=True, tm=tmw, tn=E, tk=_fit(1024, D),
        outs=[((T, E), F32, 'tile'), ((T, E), BF16, 'tile'), ((1, E), F32, 'acc'), ((1, E), F32, 'acc')],
        epi=conv_branch_bwd_epi,
        epi_ins=[(u1, 'tile'), (proj, pl.BlockSpec((tmw, E), lambda i, j, k: (i, 2))), (norm_g, 'col'),
                 (norm_b, 'col')])
    gw_out, dbo = _tn_mm("dw_a_out", (u4,), None, [((dr0,), None)], tn=_fit(1024, D), tk=t1024,
                         out_dtype=BF16, colsum=True)
    mid_sems = _push_start("scatter_mid_start", [gw_out.reshape((N_CHIPS,) + a_w_out.shape)], by_target=True)
    dval, dgate, dcw, dcb = _conv_bwd(du1, proj, conv_w + mid_sems[4][0, 0], E, tmw)
    dproj_segs = [((dval,), None), ((dgate,), None), ((dz_a,), None)]
    gw_in, dbi = _tn_mm("dw_a_in", (xs,), None, dproj_segs, tn=_fit(2048, E), tk=t1024, out_dtype=BF16,
                        colsum=True)

    chip_word = chip.reshape(1).astype(jnp.int32)

    def reduce_and_update(tag, names_, sent, landed, w_, m_, v_, by_target=True):
        sums = [_sum_chips("sum_" + n, l, s, chip_word, by_target) for n, s, l in zip(names_, sent, landed)]
        theirs = _swap_cores("swap_cores_" + tag, sums)
        return {n: _adamw("adamw_" + n, [mine, other], w, m, v)
                for n, mine, other, w, m, v in zip(names_, sums, theirs, w_, m_, v_)}

    late_sent, late_landed = _push_wait("scatter_late_wait", *late_sems[:4], after=[dbi], by_target=True)
    mid_sent, mid_landed = _push_wait("scatter_mid_wait", *mid_sems[:4], after=[dbi], by_target=True)
    early_sems = _push_start("scatter_early_start", [gw_in.reshape(a_w_in.shape[:-1] + (3 * E,))],
                             by_target='cols')
    (grad_x,) = _row_mm(
        "a_in_bwd", dproj_segs, g_win, nt=True, tm=t512, tn=D, tk=E, b_whole=True,
        outs=[((T, D), F32, 'tile')], epi=lambda acc, ins, i, j: [alpha * ins[0] + acc + ins[1]],
        epi_ins=[(dr0, 'tile'), (jnp.zeros((1, D), F32) + early_sems[4][0, 0], 'col')])
    big_out = reduce_and_update(
        "late", ["a_w_out"] + late_names, mid_sent + late_sent, mid_landed + late_landed,
        [a_w_out] + late_w, [m_a_w_out, m_kv_w_down, m_kv_w_uk, m_kv_w_uv, m_b_w_in, m_b_w_uq, m_b_w_out],
        [v_a_w_out, v_kv_w_down, v_kv_w_uk, v_kv_w_uv, v_b_w_in, v_b_w_uq, v_b_w_out])

    small_full = [jnp.concatenate([dg0, dg1]), jnp.concatenate([db0, db1]), dbi, dcw, dcb, dng, dnb, dbo,
                  dkvg, dqg]
    sflat = jnp.concatenate([g.reshape(-1) for g in small_full])
    summed = _all_reduce_small(_pack_rows(sflat, F32, 8 * LANES)).reshape(-1)
    soff = 0
    sgrads = []
    for g in small_full:
        sgrads.append(summed[soff:soff + g.size].reshape(g.shape))
        soff += g.size
    local_cols = lambda g, n: lax.dynamic_slice_in_dim(g, chip * n, n, axis=g.ndim - 1)
    snames = ["ln_g", "ln_b", "a_b_in", "a_conv_w", "a_conv_b", "a_norm_g", "a_norm_b", "a_b_out",
              "kv_norm_g", "b_q_norm_g"]
    small_w = [ln_g, ln_b, a_b_in, a_conv_w, a_conv_b, a_norm_g, a_norm_b, a_b_out, kv_norm_g, b_q_norm_g]
    small_m = [m_ln_g, m_ln_b, m_a_b_in, m_a_conv_w, m_a_conv_b, m_a_norm_g, m_a_norm_b, m_a_b_out,
               m_kv_norm_g, m_b_q_norm_g]
    small_v = [v_ln_g, v_ln_b, v_a_b_in, v_a_conv_w, v_a_conv_b, v_a_norm_g, v_a_norm_b, v_a_b_out,
               v_kv_norm_g, v_b_q_norm_g]
    sharded = {"a_b_in", "a_conv_w", "a_conv_b", "a_norm_g", "a_norm_b", "a_b_out"}
    local_g = [(local_cols(g, w.shape[-1]) if n in sharded else g).reshape(w.shape)
               for n, g, w in zip(snames, sgrads, small_w)]
    at_least_2d = lambda a: a.reshape((1,) + a.shape) if a.ndim == 1 else a
    sres = _adamw_vectors([tuple(at_least_2d(a) for a in item)
                           for item in zip(local_g, small_w, small_m, small_v)])
    small_out = {n: [g] + [r.reshape(w.shape) for r in res]
                 for n, g, w, res in zip(snames, local_g, small_w, sres)}

    early_sent, early_landed = _push_wait(
        "scatter_early_wait", *early_sems[:4], by_target='cols',
        after=[grad_x, big_out["b_w_out"][1], small_out["b_q_norm_g"][1]])
    big_out.update(reduce_and_update("early", ["a_w_in"], early_sent, early_landed,
                                     [a_w_in], [m_a_w_in], [v_a_w_in], by_target='cols'))

    loss = lax.psum(loss_part[0, 0], ("x", "y", "c"))
    order = ["ln_g", "ln_b", "a_w_in", "a_b_in", "a_conv_w", "a_conv_b", "a_norm_g", "a_norm_b", "a_w_out",
             "a_b_out", "kv_w_down", "kv_norm_g", "kv_w_uk", "kv_w_uv", "b_w_in", "b_q_norm_g", "b_w_uq",
             "b_w_out"]
    outs = {**big_out, **small_out}
    result = [loss, grad_x[None]]
    for part in range(4):
        result += [outs[n][part] for n in order]
    return tuple(result)
```

```python
import functools
import math

import jax
import jax.numpy as jnp
from jax import lax
from jax.experimental import pallas as pl
from jax.experimental.pallas import tpu as pltpu

F32, BF16 = jnp.float32, jnp.bfloat16
NN = (((1,), (0,)), ((), ()))
NT = (((1,), (1,)), ((), ()))
TN = (((0,), (0,)), ((), ()))
MESH = pl.DeviceIdType.MESH
ANY = pl.BlockSpec(memory_space=pl.ANY)

LANES = 128
BF16_ROWS = 16
VMEM_LIMIT = 56 * 1024 * 1024
N_CHIPS = 4
LN_EPS = 1e-5
RMS_EPS = 1e-6
MASK_VALUE = -1e30
LOG2_E = math.log2(math.e)
ROPE_THETA = 10000.0
ROPE_DIM = 64
ROPE_HALF = ROPE_DIM // 2
ADAM_LR, ADAM_B1, ADAM_B2, ADAM_EPS, ADAM_WD, ADAM_STEP = 0.001, 0.9, 0.999, 1e-08, 0.01, 10

MAX_TILE = 2048
TM_WIDE = 256
TQ = 512
CONV_HALO = 32
CONV_LC = 1024
CONV_SUB = 256
SUBLANES = 8
CONV_RC = 32
ADAM_ROWS = 128


def _dot(a, b, dims):
    return lax.dot_general(a.astype(BF16), b.astype(BF16), dims, preferred_element_type=F32)


def _sig(x):
    return 0.5 * jnp.tanh(0.5 * x) + 0.5


def _params(n_axes):
    return pltpu.CompilerParams(dimension_semantics=("arbitrary",) * n_axes, vmem_limit_bytes=VMEM_LIMIT)


def _gcd(*v):
    return functools.reduce(math.gcd, v)


def _fit(want, dim):
    return math.gcd(min(want, MAX_TILE), dim)


def _row_mm(name, a_segs, b, *, nt, tm, tn, tk, outs, epi, epi_ins=(), b_whole=False):
    M = a_segs[0][0][0].shape[0]
    stacked = b.ndim == 3
    if stacked:
        n_blk = b.shape[2]
        N = b.shape[1] if nt else N_CHIPS * n_blk
        assert (nt and b_whole) or (not nt and tn == n_blk and tk == b.shape[1]), name
    else:
        N = b.shape[0] if nt else b.shape[1]
    nkb = [arrs[0].shape[1] // tk for arrs, _ in a_segs]
    koff = [sum(nkb[:s]) for s in range(len(nkb))]
    ni, nj, nk = M // tm, N // tn, sum(nkb)
    assert M % tm == 0 and N % tn == 0 and all(arrs[0].shape[1] % tk == 0 for arrs, _ in a_segs), name
    assert stacked or (b.shape[1] if nt else b.shape[0]) == nk * tk, name

    def spec_of(shape, kind):
        if isinstance(kind, pl.BlockSpec):
            return kind
        if kind == 'tile':
            return pl.BlockSpec((tm, tn), lambda i, j, k: (i, j))
        if kind == 'row':
            return pl.BlockSpec((tm, shape[1]), lambda i, j, k: (i, 0))
        if kind == 'col':
            return pl.BlockSpec((1, tn), lambda i, j, k: (0, j))
        assert kind == 'acc' and nj == 1, name
        return pl.BlockSpec(shape, lambda i, j, k: (0,) * len(shape))

    in_specs, operands = [], []
    for s, (arrs, _) in enumerate(a_segs):
        for arr in arrs:
            in_specs.append(pl.BlockSpec(
                (tm, tk), lambda i, j, k, s=s: (i, jnp.clip(k - koff[s], 0, nkb[s] - 1))))
            operands.append(arr)
    if b_whole:
        assert nt and nj == 1 and all(n == 1 for n in nkb), name
        in_specs.append(pl.BlockSpec(b.shape, lambda i, j, k: (0,) * b.ndim))
    elif stacked:
        in_specs.append(pl.BlockSpec((1, tk, tn), lambda i, j, k: (j, 0, 0)))
    else:
        in_specs.append(pl.BlockSpec((tn, tk), lambda i, j, k: (j, k)) if nt
                        else pl.BlockSpec((tk, tn), lambda i, j, k: (k, j)))
    operands.append(b)
    for arr, kind in epi_ins:
        in_specs.append(spec_of(arr.shape, kind))
        operands.append(arr)
    out_specs = [spec_of(shape, kind) for shape, _, kind in outs]
    out_shape = [jax.ShapeDtypeStruct(shape, dtype) for shape, dtype, _ in outs]
    n_seg_refs = [len(arrs) for arrs, _ in a_segs]

    def body(*refs):
        pos = 0
        seg_refs = []
        for n in n_seg_refs:
            seg_refs.append(refs[pos:pos + n])
            pos += n
        b_ref = refs[pos]
        e_refs = refs[pos + 1:pos + 1 + len(epi_ins)]
        o_refs = refs[pos + 1 + len(epi_ins):pos + 1 + len(epi_ins) + len(outs)]
        i, j, k = pl.program_id(0), pl.program_id(1), pl.program_id(2)

        def product(fn, rs, s=0):
            a = rs[0][...] if fn is None else fn(*[r[...] for r in rs])
            if b_whole and stacked:
                lo, hi, tot = koff[s] * tk, (koff[s] + 1) * tk, None
                for q in range(lo // n_blk, (hi - 1) // n_blk + 1):
                    c0, c1 = max(lo, q * n_blk), min(hi, (q + 1) * n_blk)
                    part = _dot(a[:, c0 - lo:c1 - lo], b_ref[q, :, c0 - q * n_blk:c1 - q * n_blk], NT)
                    tot = part if tot is None else tot + part
                return tot
            if b_whole:
                return _dot(a, b_ref[:, koff[s] * tk:(koff[s] + 1) * tk], NT)
            return _dot(a, b_ref[0] if stacked else b_ref[...], NT if nt else NN)

        def finish(acc):
            res = epi(acc, [r[...] for r in e_refs], i, j)
            for o_ref, (_, _, kind), r in zip(o_refs, outs, res):
                if isinstance(kind, str) and kind == 'acc':
                    @pl.when(i == 0)
                    def _(o_ref=o_ref, r=r):
                        o_ref[...] = r

                    @pl.when(i > 0)
                    def _(o_ref=o_ref, r=r):
                        o_ref[...] += r
                else:
                    o_ref[...] = r.astype(o_ref.dtype)

        if nk == 1:
            finish(product(a_segs[0][1], seg_refs[0]))
            return
        acc_ref = refs[-1]

        @pl.when(k == 0)
        def _():
            acc_ref[...] = jnp.zeros_like(acc_ref)

        for s, ((_, fn), rs) in enumerate(zip(a_segs, seg_refs)):
            def accumulate(fn=fn, rs=rs, s=s):
                acc_ref[...] += product(fn, rs, s)
            if len(a_segs) == 1:
                accumulate()
            else:
                pl.when(jnp.logical_and(k >= koff[s], k < koff[s] + nkb[s]))(accumulate)

        @pl.when(k == nk - 1)
        def _():
            finish(acc_ref[...])

    return pl.pallas_call(
        body, name=name, grid=(ni, nj, nk), in_specs=in_specs, out_specs=out_specs, out_shape=out_shape,
        scratch_shapes=[] if nk == 1 else [pltpu.VMEM((tm, tn), F32)], compiler_params=_params(3),
    )(*operands)


def _tn_mm(name, a_arrs, a_fn, b_segs, *, tn, tk, out_dtype, shard_major=False, colsum=False):
    T, M = a_arrs[0].shape
    nbj = [arrs[0].shape[1] // tn for arrs, _ in b_segs]
    joff = [sum(nbj[:s]) for s in range(len(nbj))]
    nj, nk = sum(nbj), T // tk
    N = nj * tn
    assert T % tk == 0 and all(arrs[0].shape[1] % tn == 0 for arrs, _ in b_segs), name

    in_specs = [pl.BlockSpec((tk, M), lambda j, k: (k, 0)) for _ in a_arrs]
    operands = list(a_arrs)
    for s, (arrs, _) in enumerate(b_segs):
        for arr in arrs:
            in_specs.append(pl.BlockSpec(
                (tk, tn), lambda j, k, s=s: (k, jnp.clip(j - joff[s], 0, nbj[s] - 1))))
            operands.append(arr)
    if shard_major:
        per = (N // N_CHIPS) // tn
        assert per * tn * N_CHIPS == N, name
        out_shape = [jax.ShapeDtypeStruct((N_CHIPS, M, N // N_CHIPS), out_dtype)]
        out_specs = [pl.BlockSpec((1, M, tn), lambda j, k: (j // per, 0, j % per))]
    else:
        out_shape = [jax.ShapeDtypeStruct((M, N), out_dtype)]
        out_specs = [pl.BlockSpec((M, tn), lambda j, k: (0, j))]
    if colsum:
        out_shape.append(jax.ShapeDtypeStruct((1, N), F32))
        out_specs.append(pl.BlockSpec((1, tn), lambda j, k: (0, j)))
    n_a = len(a_arrs)
    n_seg_refs = [len(arrs) for arrs, _ in b_segs]

    def body(*refs):
        a_refs = refs[:n_a]
        pos = n_a
        seg_refs = []
        for n in n_seg_refs:
            seg_refs.append(refs[pos:pos + n])
            pos += n
        o_ref = refs[pos]
        cs_ref = refs[pos + 1] if colsum else None
        acc_ref = refs[-1]
        j, k = pl.program_id(0), pl.program_id(1)

        @pl.when(k == 0)
        def _():
            acc_ref[...] = jnp.zeros_like(acc_ref)
            if colsum:
                cs_ref[...] = jnp.zeros_like(cs_ref)

        for s, ((_, fn), rs) in enumerate(zip(b_segs, seg_refs)):
            def accumulate(fn=fn, rs=rs):
                a = a_refs[0][...] if a_fn is None else a_fn(*[r[...] for r in a_refs])
                bt = rs[0][...] if fn is None else fn(*[r[...] for r in rs])
                acc_ref[...] += _dot(a, bt, TN)
                if colsum:
                    cs_ref[...] += jnp.sum(bt.astype(F32), axis=0, keepdims=True)
            if len(b_segs) == 1:
                accumulate()
            else:
                pl.when(jnp.logical_and(j >= joff[s], j < joff[s] + nbj[s]))(accumulate)

        @pl.when(k == nk - 1)
        def _():
            if shard_major:
                o_ref[0] = acc_ref[...].astype(o_ref.dtype)
            else:
                o_ref[...] = acc_ref[...].astype(o_ref.dtype)

    res = pl.pallas_call(
        body, name=name, grid=(nj, nk), in_specs=in_specs, out_specs=out_specs, out_shape=out_shape,
        scratch_shapes=[pltpu.VMEM((M, tn), F32)], compiler_params=_params(2),
    )(*operands)
    return res if colsum else res[0]


def _silu(z):
    return z * _sig(z)


def _silu_and_grad(z):
    s = _sig(z)
    return z * s, s * (1.0 + z * (1.0 - s))


def _gate(o, z):
    return o.astype(F32) * _silu(z.astype(F32))


def _ln_stats(r):
    mu = jnp.mean(r, axis=1, keepdims=True)
    xc = r - mu
    var = jnp.mean(xc * xc, axis=1, keepdims=True)
    rstd = lax.rsqrt(var + LN_EPS)
    return xc * rstd, rstd


def _ln_bwd(dy, xhat, rstd, g):
    dxh = dy * g
    m1 = jnp.mean(dxh, axis=1, keepdims=True)
    m2 = jnp.mean(dxh * xhat, axis=1, keepdims=True)
    return (rstd * (dxh - m1 - xhat * m2), jnp.sum(dy * xhat, axis=0, keepdims=True),
            jnp.sum(dy, axis=0, keepdims=True))


def _rms_fwd(x, g):
    rstd = lax.rsqrt(jnp.mean(x * x, axis=1, keepdims=True) + RMS_EPS)
    return x * rstd * g


def _rms_bwd(dy, x, g):
    rstd = lax.rsqrt(jnp.mean(x * x, axis=1, keepdims=True) + RMS_EPS)
    xn = x * rstd
    dxn = dy * g
    return rstd * (dxn - xn * jnp.mean(dxn * xn, axis=1, keepdims=True)), jnp.sum(dy * xn, axis=0, keepdims=True)


def _rope(x, cos, sin, transpose=False):
    parts = []
    for g in range(x.shape[1] // LANES):
        xg = x[:, g * LANES:(g + 1) * LANES]
        if transpose:
            parts.append(xg * cos + pltpu.roll(xg * sin, LANES // 2, 1))
        else:
            parts.append(xg * cos + pltpu.roll(xg, LANES // 2, 1) * sin)
    return parts[0] if len(parts) == 1 else jnp.concatenate(parts, axis=1)


def _shifted_rows(window, rc):
    n = window.shape[0]
    for b in range(SUBLANES):
        rolled = window if b == 0 else pltpu.roll(window, n - b, 0)
        for a8 in range(0, n - rc - b + 1, SUBLANES):
            yield a8 + b, rolled[a8:a8 + rc]


def _conv_fwd(proj, conv_w, conv_b, E, tm):
    T = proj.shape[0]
    kc = conv_w.shape[0]
    lc, hb, rc = min(CONV_LC, E), CONV_HALO, min(CONV_RC, tm)
    nl, ni, ratio = E // lc, T // tm, tm // hb
    gate_off = E // lc

    sub = min(CONV_SUB, lc)
    base = hb - (kc - 1)

    def body(val_ref, gate_ref, valh_ref, gateh_ref, w_ref, cb_ref, u1_ref, ubuf):
        i = pl.program_id(1)
        ubuf[hb:, :] = val_ref[...] * _sig(gate_ref[...])
        halo = valh_ref[...] * _sig(gateh_ref[...])
        ubuf[0:hb, :] = jnp.where(i > 0, halo, 0.0)
        for l0 in range(0, lc, sub):
            ls = slice(l0, l0 + sub)
            for r0 in range(0, tm, rc):
                acc = jnp.zeros((rc, sub), F32) + cb_ref[:, ls]
                for off, rows in _shifted_rows(ubuf[r0:r0 + hb + rc, ls], rc):
                    if 0 <= off - base < kc:
                        acc += w_ref[off - base:off - base + 1, ls] * rows
                u1_ref[r0:r0 + rc, ls] = acc

    return pl.pallas_call(
        body, name="conv_fwd", grid=(nl, ni),
        in_specs=[
            pl.BlockSpec((tm, lc), lambda l, i: (i, l)),
            pl.BlockSpec((tm, lc), lambda l, i: (i, gate_off + l)),
            pl.BlockSpec((hb, lc), lambda l, i: (jnp.maximum(i * ratio - 1, 0), l)),
            pl.BlockSpec((hb, lc), lambda l, i: (jnp.maximum(i * ratio - 1, 0), gate_off + l)),
            pl.BlockSpec((kc, lc), lambda l, i: (0, l)),
            pl.BlockSpec((1, lc), lambda l, i: (0, l)),
        ],
        out_specs=pl.BlockSpec((tm, lc), lambda l, i: (i, l)),
        out_shape=jax.ShapeDtypeStruct((T, E), F32),
        scratch_shapes=[pltpu.VMEM((hb + tm, lc), F32)], compiler_params=_params(2),
    )(proj, proj, proj, proj, conv_w, conv_b)


def _conv_post(u1, proj, norm_g, norm_b, E, tm):
    T = u1.shape[0]

    def body(u1_ref, z_ref, g_ref, b_ref, u4_ref):
        xhat, _ = _ln_stats(u1_ref[...])
        u4_ref[...] = (_silu(xhat * g_ref[...] + b_ref[...]) * _silu(z_ref[...])).astype(BF16)

    return pl.pallas_call(
        body, name="conv_post", grid=(T // tm,),
        in_specs=[pl.BlockSpec((tm, E), lambda i: (i, 0)), pl.BlockSpec((tm, E), lambda i: (i, 2)),
                  pl.BlockSpec((1, E), lambda i: (0, 0)), pl.BlockSpec((1, E), lambda i: (0, 0))],
        out_specs=pl.BlockSpec((tm, E), lambda i: (i, 0)),
        out_shape=jax.ShapeDtypeStruct((T, E), BF16), compiler_params=_params(1),
    )(u1, proj, norm_g, norm_b)


def _conv_bwd(du1, proj, conv_w, E, tm):
    T = du1.shape[0]
    kc = conv_w.shape[0]
    lc, hb, rc = min(CONV_LC, E), CONV_HALO, min(CONV_RC, tm)
    nl, ni, ratio = E // lc, T // tm, tm // hb
    gate_off = E // lc
    last_halo = T // hb - 1

    sub = min(CONV_SUB, lc)
    base = hb - (kc - 1)

    def body(du_ref, dun_ref, val_ref, gate_ref, valh_ref, gateh_ref, w_ref,
             dval_ref, dgate_ref, dw_ref, db_ref, ubuf, dbuf, sbuf, dw_sc):
        i = pl.program_id(1)
        sbuf[...] = _sig(gate_ref[...])
        ubuf[hb:, :] = val_ref[...] * sbuf[...]
        halo = valh_ref[...] * _sig(gateh_ref[...])
        ubuf[0:hb, :] = jnp.where(i > 0, halo, 0.0)
        dbuf[0:tm, :] = du_ref[...]
        dbuf[tm:, :] = jnp.where(i < ni - 1, dun_ref[...], 0.0)

        @pl.when(i == 0)
        def _():
            dw_sc[...] = jnp.zeros_like(dw_sc)
            db_ref[...] = jnp.zeros_like(db_ref)

        db_ref[...] += jnp.sum(du_ref[...], axis=0, keepdims=True)
        for l0 in range(0, lc, sub):
            ls = slice(l0, l0 + sub)
            for r0 in range(0, tm, rc):
                dwin = dbuf[r0:r0 + rc + hb, ls]
                dchunk = dwin[0:rc]
                for off, rows in _shifted_rows(ubuf[r0:r0 + hb + rc, ls], rc):
                    k = off - base
                    if 0 <= k < kc:
                        prod = rows * dchunk
                        part = prod[0:SUBLANES]
                        for s8 in range(SUBLANES, rc, SUBLANES):
                            part = part + prod[s8:s8 + SUBLANES]
                        dw_sc[k, :, ls] += part
                acc = jnp.zeros((rc, sub), F32)
                for off, rows in _shifted_rows(dwin, rc):
                    k = (kc - 1) - off
                    if 0 <= k < kc:
                        acc += w_ref[k:k + 1, ls] * rows
                v, s = val_ref[r0:r0 + rc, ls], sbuf[r0:r0 + rc, ls]
                dval_ref[r0:r0 + rc, ls] = (acc * s).astype(BF16)
                dgate_ref[r0:r0 + rc, ls] = (acc * v * s * (1.0 - s)).astype(BF16)

        @pl.when(i == ni - 1)
        def _():
            for k in range(kc):
                dw_ref[k:k + 1, :] = jnp.sum(dw_sc[k], axis=0, keepdims=True)

    return pl.pallas_call(
        body, name="conv_bwd", grid=(nl, ni),
        in_specs=[
            pl.BlockSpec((tm, lc), lambda l, i: (i, l)),
            pl.BlockSpec((hb, lc), lambda l, i: (jnp.minimum((i + 1) * ratio, last_halo), l)),
            pl.BlockSpec((tm, lc), lambda l, i: (i, l)),
            pl.BlockSpec((tm, lc), lambda l, i: (i, gate_off + l)),
            pl.BlockSpec((hb, lc), lambda l, i: (jnp.maximum(i * ratio - 1, 0), l)),
            pl.BlockSpec((hb, lc), lambda l, i: (jnp.maximum(i * ratio - 1, 0), gate_off + l)),
            pl.BlockSpec((kc, lc), lambda l, i: (0, l)),
        ],
        out_specs=[pl.BlockSpec((tm, lc), lambda l, i: (i, l)), pl.BlockSpec((tm, lc), lambda l, i: (i, l)),
                   pl.BlockSpec((kc, lc), lambda l, i: (0, l)), pl.BlockSpec((1, lc), lambda l, i: (0, l))],
        out_shape=[jax.ShapeDtypeStruct((T, E), BF16), jax.ShapeDtypeStruct((T, E), BF16),
                   jax.ShapeDtypeStruct((kc, E), F32), jax.ShapeDtypeStruct((1, E), F32)],
        scratch_shapes=[pltpu.VMEM((hb + tm, lc), F32), pltpu.VMEM((tm + hb, lc), F32),
                        pltpu.VMEM((tm, lc), F32), pltpu.VMEM((kc, SUBLANES, lc), F32)],
        compiler_params=_params(2),
    )(du1, du1, proj, proj, proj, proj, conv_w)


def _attn_fwd(q_all, kv, kr, H, tq, scale):
    T = q_all.shape[0]
    nq = T // tq
    pair = 2
    W = pair * LANES
    assert H % pair == 0
    hp_n = H // pair

    def body(qn_ref, qr_ref, kn_ref, kr_ref, v_ref, o_ref, lse_ref, *scratch):
        qi = pl.program_id(1)
        chains = [scratch[4 * a:4 * a + 4] for a in range(pair)]
        lanes = [slice(a * LANES, (a + 1) * LANES) for a in range(pair)]
        groups = [slice(c * LANES, (c + 1) * LANES) for c in range(tq // LANES)]

        def fold(x, op):
            r = x[:, groups[0]]
            for gsl in groups[1:]:
                r = op(r, x[:, gsl])
            return r

        for _, m_sc, l_sc, acc_sc in chains:
            m_sc[...] = jnp.full_like(m_sc, MASK_VALUE)
            l_sc[...] = jnp.zeros_like(l_sc)
            acc_sc[...] = jnp.zeros_like(acc_sc)

        def scores(j, masked):
            rows = pl.ds(pl.multiple_of(j * tq, tq), tq)
            krope = kr_ref[rows, :]
            for a, (s_sc, m_sc, _, _) in enumerate(chains):
                q = jnp.concatenate([qn_ref[:, lanes[a]], qr_ref[:, lanes[a]]], axis=1)
                k = jnp.concatenate([kn_ref[rows, lanes[a]], krope], axis=1)
                s = _dot(q, k, NT) * (scale * LOG2_E)
                if masked:
                    row = lax.broadcasted_iota(jnp.int32, s.shape, 0)
                    col = lax.broadcasted_iota(jnp.int32, s.shape, 1)
                    s = jnp.where(col <= row, s, MASK_VALUE)
                s_sc[j] = s
                m_sc[...] = jnp.maximum(m_sc[...], fold(s, jnp.maximum))

        def two_per_trip(fn, count):
            def two(p, carry):
                fn(2 * p)
                fn(2 * p + 1)
                return carry

            lax.fori_loop(0, count // 2, two, 0)

            @pl.when(count % 2 == 1)
            def _():
                fn(count - 1)

        two_per_trip(functools.partial(scores, masked=False), qi)
        scores(qi, True)
        for _, m_sc, _, _ in chains:
            m_sc[...] = jnp.broadcast_to(jnp.max(m_sc[...], axis=1, keepdims=True), m_sc.shape)

        def weigh(j):
            rows = pl.ds(pl.multiple_of(j * tq, tq), tq)
            for a, (s_sc, m_sc, l_sc, acc_sc) in enumerate(chains):
                s, m = s_sc[j], m_sc[...]
                p = jnp.concatenate([jnp.exp2(s[:, gsl] - m) for gsl in groups], axis=1)
                l_sc[...] += fold(p, jnp.add)
                acc_sc[...] += _dot(p, v_ref[rows, lanes[a]], NN)

        two_per_trip(weigh, qi + 1)
        for a, (_, m_sc, l_sc, acc_sc) in enumerate(chains):
            l = jnp.sum(l_sc[...], axis=1, keepdims=True)
            o_ref[:, lanes[a]] = (acc_sc[...] / l).astype(BF16)
            lse_ref[a] = m_sc[:, 0:1] * (1.0 / LOG2_E) + jnp.log(l)

    chain_scratch = [pltpu.VMEM((nq, tq, tq), F32), pltpu.VMEM((tq, LANES), F32), pltpu.VMEM((tq, LANES), F32),
                     pltpu.VMEM((tq, LANES), F32)]
    return pl.pallas_call(
        body, name="attn_fwd", grid=(hp_n, nq),
        in_specs=[pl.BlockSpec((tq, W), lambda hp, qi: (qi, hp)),
                  pl.BlockSpec((tq, W), lambda hp, qi: (qi, hp_n + hp)),
                  pl.BlockSpec((T, W), lambda hp, qi: (0, hp)),
                  pl.BlockSpec((T, LANES), lambda hp, qi: (0, 0)),
                  pl.BlockSpec((T, W), lambda hp, qi: (0, hp_n + hp))],
        out_specs=[pl.BlockSpec((tq, W), lambda hp, qi: (qi, hp)),
                   pl.BlockSpec((pair, tq, 1), lambda hp, qi: (hp, qi, 0))],
        out_shape=[jax.ShapeDtypeStruct((T, H * LANES), BF16), jax.ShapeDtypeStruct((H, T, 1), F32)],
        scratch_shapes=chain_scratch * pair, compiler_params=_params(2),
    )(q_all, q_all, kv, kr, kv)


def _attn_bwd(q_all, kv, kr, do, o, lse, cos, sin, H, tq, scale):
    T = q_all.shape[0]
    nq = T // tq
    HV = H * LANES
    pair = 2
    tk2 = pair * tq
    ng = T // tk2
    assert ng * tk2 == T and pair == 2

    def body(qn_ref, qr_ref, kn_ref, kr_ref, v_ref, do_ref, o_ref, lse_ref, cos_ref, sin_ref,
             dqn_ref, dqr_ref, dkn_ref, dkr_ref, dv_ref, dq_sc, dk_sc, dv_sc):
        g = pl.program_id(1)

        @pl.when(g == 0)
        def _():
            dq_sc[...] = jnp.zeros_like(dq_sc)

        key_rows = [slice(c * tq, (c + 1) * tq) for c in range(pair)]

        def block(qi, modes):
            rows = pl.ds(pl.multiple_of(qi * tq, tq), tq)
            q = jnp.concatenate([qn_ref[rows, :], qr_ref[rows, :]], axis=1)
            dov = do_ref[rows, :]
            delta = jnp.sum(dov.astype(F32) * o_ref[rows, :].astype(F32), axis=1, keepdims=True)
            lse_q = lse_ref[0, rows, :]
            dq, dkv = None, []
            for kr_, masked in zip(key_rows, modes):
                if masked is None:
                    dkv.append(None)
                    continue
                k = jnp.concatenate([kn_ref[kr_, :], kr_ref[kr_, :]], axis=1)
                s = _dot(q, k, NT) * scale
                if masked:
                    row = lax.broadcasted_iota(jnp.int32, s.shape, 0)
                    col = lax.broadcasted_iota(jnp.int32, s.shape, 1)
                    s = jnp.where(col <= row, s, MASK_VALUE)
                p = jnp.exp(s - lse_q)
                dv = _dot(p, dov, TN)
                dp = _dot(dov, v_ref[kr_, :], NT)
                ds = (p * (dp - delta) * scale).astype(BF16)
                dkv.append((_dot(ds, q, TN), dv))
                part = _dot(ds, k, NN)
                dq = part if dq is None else dq + part
            return rows, dq, dkv

        rows_a, dq_a, (kv_a0, _) = block(pair * g, (True, None))
        rows_b, dq_b, (kv_b0, kv_b1) = block(pair * g + 1, (False, True))
        dk_sc[key_rows[0], :] = kv_a0[0] + kv_b0[0]
        dv_sc[key_rows[0], :] = kv_a0[1] + kv_b0[1]
        dk_sc[key_rows[1], :] = kv_b1[0]
        dv_sc[key_rows[1], :] = kv_b1[1]
        dq_sc[rows_a, :] += dq_a
        dq_sc[rows_b, :] += dq_b

        def below(trip, carry):
            for qi in (pair * g + pair + 2 * trip, pair * g + pair + 2 * trip + 1):
                rows, dq, dkv = block(qi, (False, False))
                for kr_, (dk, dv) in zip(key_rows, dkv):
                    dk_sc[kr_, :] += dk
                    dv_sc[kr_, :] += dv
                dq_sc[rows, :] += dq
            return carry

        lax.fori_loop(0, (nq - pair * g - pair) // 2, below, 0)
        dkn_ref[...] = dk_sc[:, :LANES].astype(BF16)
        dkr_ref[...] = dk_sc[:, LANES:]
        dv_ref[...] = dv_sc[...].astype(BF16)

        @pl.when(g == ng - 1)
        def _():
            dqn_ref[...] = dq_sc[:, :LANES].astype(BF16)
            dqr_ref[...] = _rope(dq_sc[:, LANES:], cos_ref[...], sin_ref[...], transpose=True).astype(BF16)

    whole = lambda col: pl.BlockSpec((T, LANES), col)
    tile = lambda col: pl.BlockSpec((tk2, LANES), col)
    return pl.pallas_call(
        body, name="attn_bwd", grid=(H, ng),
        in_specs=[whole(lambda h, g: (0, h)), whole(lambda h, g: (0, H + h)),
                  tile(lambda h, g: (g, h)), tile(lambda h, g: (g, 0)), tile(lambda h, g: (g, H + h)),
                  whole(lambda h, g: (0, h)), whole(lambda h, g: (0, h)),
                  pl.BlockSpec((1, T, 1), lambda h, g: (h, 0, 0)),
                  whole(lambda h, g: (0, 0)), whole(lambda h, g: (0, 0))],
        out_specs=[whole(lambda h, g: (0, h)), whole(lambda h, g: (0, h)),
                   tile(lambda h, g: (g, h)), tile(lambda h, g: (g, h)), tile(lambda h, g: (g, h))],
        out_shape=[jax.ShapeDtypeStruct((T, HV), BF16), jax.ShapeDtypeStruct((T, HV), BF16),
                   jax.ShapeDtypeStruct((T, HV), BF16), jax.ShapeDtypeStruct((T, HV), F32),
                   jax.ShapeDtypeStruct((T, HV), BF16)],
        scratch_shapes=[pltpu.VMEM((T, 2 * LANES), F32), pltpu.VMEM((tk2, 2 * LANES), F32),
                        pltpu.VMEM((tk2, LANES), F32)],
        compiler_params=_params(2),
    )(q_all, q_all, kv, kr, kv, do, o, lse, cos, sin)


def _key_rope_bwd(dkr_heads, cos, sin, H, tm):
    T, HV = dkr_heads.shape

    def body(dkr_ref, cos_ref, sin_ref, dk_ref):
        dk = dkr_ref[...]
        tot = dk[:, 0:LANES]
        for h in range(1, H):
            tot = tot + dk[:, h * LANES:(h + 1) * LANES]
        dk_ref[...] = _rope(tot, cos_ref[...], sin_ref[...], transpose=True)

    return pl.pallas_call(
        body, name="key_rope_bwd", grid=(T // tm,),
        in_specs=[pl.BlockSpec((tm, HV), lambda i: (i, 0)),
                  pl.BlockSpec((tm, LANES), lambda i: (i, 0)), pl.BlockSpec((tm, LANES), lambda i: (i, 0))],
        out_specs=pl.BlockSpec((tm, LANES), lambda i: (i, 0)),
        out_shape=jax.ShapeDtypeStruct((T, LANES), F32), compiler_params=_params(1),
    )(dkr_heads, cos, sin)


def _place():
    x, y, c = lax.axis_index("x"), lax.axis_index("y"), lax.axis_index("c")
    chips = [(1 - x, y), (x, 1 - y), (1 - x, 1 - y)]
    return x, y, c, chips


def _all_gather_chips(arrs):
    n = len(arrs)
    halves = [a.shape[0] // 2 for a in arrs]
    assert all(h * 2 == a.shape[0] and h % BF16_ROWS == 0 for h, a in zip(halves, arrs))

    def body(*refs):
        w_refs, out_refs = refs[:n], refs[n:2 * n]
        send_sems, recv_sems, local_sems = refs[2 * n:]
        x, y, c, chips = _place()
        sibling = (x, y, 1 - c)
        waits = []
        for w, (w_ref, out_ref, half) in enumerate(zip(w_refs, out_refs, halves)):
            def region(px, py, pc, out_ref=out_ref, half=half):
                return out_ref.at[2 * px + py, pl.ds(pc * half, half), :]

            def copy(k, block, to, src=None, w=w, region=region):
                return pltpu.make_async_remote_copy(
                    src_ref=region(*block) if src is None else src, dst_ref=region(*block),
                    send_sem=send_sems.at[6 * w + k], recv_sem=recv_sems.at[6 * w + k],
                    device_id=to, device_id_type=MESH)

            mine = pltpu.make_async_copy(w_ref, out_ref.at[2 * x + y], local_sems.at[w])
            mine.start()
            my_half = w_ref.at[pl.ds(c * half, half), :]
            first = [copy(j, (x, y, c), (*chip, c), src=my_half) for j, chip in enumerate(chips)]
            for cp in first:
                cp.start()
            waits.append((copy, mine, first))
        for copy, mine, first in waits:
            passed = [copy(3 + j, (*chip, c), sibling) for j, chip in enumerate(chips)]
            for j, chip in enumerate(chips):
                copy(j, (*chip, c), (x, y, c)).wait_recv()
                passed[j].start()
            for j, chip in enumerate(chips):
                copy(3 + j, (*chip, 1 - c), (x, y, c)).wait_recv()
            for cp in first + passed:
                cp.wait_send()
            mine.wait()

    return pl.pallas_call(
        body, name="gather_weights", in_specs=[ANY] * n, out_specs=[ANY] * n,
        out_shape=[jax.ShapeDtypeStruct((N_CHIPS,) + a.shape, a.dtype) for a in arrs],
        scratch_shapes=[pltpu.SemaphoreType.DMA((6 * n,)), pltpu.SemaphoreType.DMA((6 * n,)),
                        pltpu.SemaphoreType.DMA((n,))],
    )(*arrs)


def _swap_cores(name, parts):
    n = len(parts)

    def body(*refs):
        p_refs, r_refs = refs[:n], refs[n:2 * n]
        send_sems, recv_sems = refs[2 * n:]
        x, y, c, _ = _place()
        copies = [pltpu.make_async_remote_copy(
            src_ref=p_refs[w], dst_ref=r_refs[w], send_sem=send_sems.at[w], recv_sem=recv_sems.at[w],
            device_id=(x, y, 1 - c), device_id_type=MESH) for w in range(n)]
        for cp in copies:
            cp.start()
        for cp in copies:
            cp.wait()

    return pl.pallas_call(
        body, name=name, in_specs=[ANY] * n, out_specs=[ANY] * n,
        out_shape=[jax.ShapeDtypeStruct(p.shape, p.dtype) for p in parts],
        scratch_shapes=[pltpu.SemaphoreType.DMA((n,)), pltpu.SemaphoreType.DMA((n,))],
    )(*parts)


HBM = pl.BlockSpec(memory_space=pltpu.HBM)
SEM = pl.BlockSpec(memory_space=pltpu.SEMAPHORE)
EFFECT = pltpu.SideEffectType.DATAFLOW_SIDE_EFFECTING


def _push_copies(a_refs, l_refs, send_sems, recv_sems, by_target):
    x, y, c, chips = _place()
    me = 2 * x + y

    def part(a_ref, q):
        if by_target == 'cols':
            n = a_ref.shape[-1] // N_CHIPS
            return a_ref.at[(slice(None),) * (len(a_ref.shape) - 1) + (pl.ds(pl.multiple_of(q * n, LANES), n),)]
        return a_ref.at[q] if by_target else a_ref

    out = []
    for w, (a_ref, l_ref) in enumerate(zip(a_refs, l_refs)):
        for j, (px, py) in enumerate(chips):
            peer = 2 * px + py
            out.append((
                pltpu.make_async_remote_copy(
                    src_ref=part(a_ref, peer), dst_ref=l_ref.at[me],
                    send_sem=send_sems.at[3 * w + j], recv_sem=recv_sems.at[3 * w + j],
                    device_id=(px, py, c), device_id_type=MESH),
                pltpu.make_async_remote_copy(
                    src_ref=part(a_ref, me), dst_ref=l_ref.at[peer],
                    send_sem=send_sems.at[3 * w + j], recv_sem=recv_sems.at[3 * w + j],
                    device_id=(px, py, c), device_id_type=MESH)))
    return out


def _landing_shape(a, by_target):
    if by_target == 'cols':
        return (N_CHIPS,) + a.shape[:-1] + (a.shape[-1] // N_CHIPS,)
    return (N_CHIPS,) + (a.shape[1:] if by_target else a.shape)


def _push_start(name, arrs, by_target):
    n = len(arrs)
    lands = [lax.empty(_landing_shape(a, by_target), a.dtype) for a in arrs]

    def body(*refs):
        a_refs, l_refs = refs[:n], refs[n:2 * n]
        send_sems, recv_sems = refs[2 * n], refs[2 * n + 1]
        token = refs[-1]
        for send, _ in _push_copies(a_refs, l_refs, send_sems, recv_sems, by_target):
            send.start()
        token[...] = jnp.zeros_like(token)

    res = pl.pallas_call(
        body, name=name,
        out_shape=(pltpu.SemaphoreType.DMA((3 * n,)), pltpu.SemaphoreType.DMA((3 * n,)),
                   *[pltpu.HBM(a.shape, a.dtype) for a in arrs], *[pltpu.HBM(l.shape, l.dtype) for l in lands],
                   jax.ShapeDtypeStruct((8, LANES), F32)),
        in_specs=[HBM] * (2 * n), out_specs=(SEM, SEM, *[HBM] * (2 * n), pl.BlockSpec(memory_space=pltpu.VMEM)),
        input_output_aliases={i: 2 + i for i in range(2 * n)},
        compiler_params=pltpu.CompilerParams(has_side_effects=EFFECT),
    )(*[pltpu.with_memory_space_constraint(a, pltpu.HBM) for a in list(arrs) + lands])
    return res[0], res[1], list(res[2:2 + n]), list(res[2 + n:2 + 2 * n]), res[-1]


def _push_wait(name, send_sems, recv_sems, arrs, lands, after, by_target):
    n = len(arrs)

    def body(*refs):
        a_refs, l_refs = refs[:n], refs[n:2 * n]
        s_sems, r_sems = refs[2 * n], refs[2 * n + 1]
        for send, recv in _push_copies(a_refs, l_refs, s_sems, r_sems, by_target):
            send.wait_send()
            recv.wait_recv()

    res = pl.pallas_call(
        body, name=name,
        out_shape=[pltpu.HBM(a.shape, a.dtype) for a in list(arrs) + list(lands)],
        in_specs=[HBM] * (2 * n) + [SEM, SEM] + [ANY] * len(after), out_specs=[HBM] * (2 * n),
        input_output_aliases={i: i for i in range(2 * n)},
        compiler_params=pltpu.CompilerParams(has_side_effects=EFFECT),
    )(*arrs, *lands, send_sems, recv_sems, *after)
    return list(res[:n]), list(res[n:])


def _all_reduce_small(part):
    def body(p_ref, out_ref, sib_buf, chip_buf, send_sems, recv_sems):
        x, y, c, chips = _place()
        me = 2 * x + y
        swap = pltpu.make_async_remote_copy(
            src_ref=p_ref, dst_ref=sib_buf, send_sem=send_sems.at[0], recv_sem=recv_sems.at[0],
            device_id=(x, y, 1 - c), device_id_type=MESH)
        swap.start()
        swap.wait()
        chip_buf[me] = p_ref[...] + sib_buf[...]
        copies = []
        for j, (px, py) in enumerate(chips):
            cp = pltpu.make_async_remote_copy(
                src_ref=chip_buf.at[me], dst_ref=chip_buf.at[me], send_sem=send_sems.at[1 + j],
                recv_sem=recv_sems.at[1 + j], device_id=(px, py, c), device_id_type=MESH)
            cp.start()
            copies.append(cp)
        for j, (px, py) in enumerate(chips):
            pltpu.make_async_remote_copy(
                src_ref=chip_buf.at[me], dst_ref=chip_buf.at[2 * px + py], send_sem=send_sems.at[1 + j],
                recv_sem=recv_sems.at[1 + j], device_id=(px, py, c), device_id_type=MESH).wait_recv()
        for cp in copies:
            cp.wait_send()
        tot = chip_buf[0]
        for q in range(1, N_CHIPS):
            tot = tot + chip_buf[q]
        out_ref[...] = tot

    vm = pl.BlockSpec(memory_space=pltpu.VMEM)
    return pl.pallas_call(
        body, name="all_reduce_small", in_specs=[vm], out_specs=vm,
        out_shape=jax.ShapeDtypeStruct(part.shape, F32),
        scratch_shapes=[pltpu.VMEM(part.shape, F32), pltpu.VMEM((N_CHIPS,) + part.shape, F32),
                        pltpu.SemaphoreType.DMA((N_CHIPS,)), pltpu.SemaphoreType.DMA((N_CHIPS,))],
    )(part)


def _row_tiles(shape):
    ax = next(d for d, s in enumerate(shape) if s > 1)
    tr = _gcd(ADAM_ROWS, shape[ax])
    block = tuple(tr if d == ax else s for d, s in enumerate(shape))
    return shape[ax] // tr, block, lambda i: tuple(i if d == ax else 0 for d in range(len(shape)))


def _sum_chips(name, landed, sent, chip, by_target):
    shape = landed.shape[1:]
    steps, block, index = _row_tiles(shape)
    if by_target == 'cols':
        own_spec = pl.BlockSpec(block, lambda i, c: index(i)[:-1] + (c[0],))
    else:
        own_spec = pl.BlockSpec((1,) + block, lambda i, c: (c[0],) + index(i))

    def body(chip_ref, l_ref, s_ref, o_ref):
        own = (s_ref[...] if by_target == 'cols' else s_ref[0]).astype(F32)
        tot = None
        for q in range(N_CHIPS):
            term = jnp.where(chip_ref[0] == q, own, l_ref[q].astype(F32))
            tot = term if tot is None else tot + term
        o_ref[...] = tot

    return pl.pallas_call(
        body, name=name,
        grid_spec=pltpu.PrefetchScalarGridSpec(
            num_scalar_prefetch=1, grid=(steps,),
            in_specs=[pl.BlockSpec((N_CHIPS,) + block, lambda i, c: (0,) + index(i)), own_spec],
            out_specs=pl.BlockSpec(block, lambda i, c: index(i))),
        out_shape=jax.ShapeDtypeStruct(shape, F32), compiler_params=_params(1),
    )(chip, landed, sent)


def _adamw_math(g, w, m, v):
    mn = ADAM_B1 * m + (1.0 - ADAM_B1) * g
    vn = ADAM_B2 * v + (1.0 - ADAM_B2) * jnp.square(g)
    m_hat = mn / (1.0 - ADAM_B1 ** ADAM_STEP)
    v_hat = vn / (1.0 - ADAM_B2 ** ADAM_STEP)
    return -ADAM_LR * (m_hat / (jnp.sqrt(v_hat) + ADAM_EPS) + ADAM_WD * w), mn, vn


def _adamw(name, g_parts, w, m, v):
    steps, block, index = _row_tiles(w.shape)
    n = len(g_parts)

    def body(*refs):
        g = refs[0][...]
        for r in refs[1:n]:
            g = g + r[...]
        w_ref, m_ref, v_ref, go_ref, d_ref, mo_ref, vo_ref = refs[n:]
        go_ref[...] = g
        d_ref[...], mo_ref[...], vo_ref[...] = _adamw_math(g, w_ref[...], m_ref[...], v_ref[...])

    spec = pl.BlockSpec(block, index)
    return pl.pallas_call(
        body, name=name, grid=(steps,), in_specs=[spec] * (n + 3), out_specs=[spec] * 4,
        out_shape=[jax.ShapeDtypeStruct(w.shape, F32)] * 4, compiler_params=_params(1),
    )(*g_parts, w, m, v)


def _adamw_vectors(items):
    n = len(items)

    def body(*refs):
        ins, outs = refs[:4 * n], refs[4 * n:]
        for k in range(n):
            g, w, m, v = (r[...] for r in ins[4 * k:4 * k + 4])
            outs[3 * k][...], outs[3 * k + 1][...], outs[3 * k + 2][...] = _adamw_math(g, w, m, v)

    vm = pl.BlockSpec(memory_space=pltpu.VMEM)
    res = pl.pallas_call(
        body, name="adamw_vectors", in_specs=[vm] * (4 * n), out_specs=[vm] * (3 * n),
        out_shape=[jax.ShapeDtypeStruct(it[1].shape, F32) for it in items for _ in range(3)],
    )(*[a for it in items for a in it])
    return [res[3 * k:3 * k + 3] for k in range(n)]


def _pack_rows(flat, dtype, multiple):
    n = flat.shape[0]
    total = -(-n // multiple) * multiple
    return jnp.pad(flat, (0, total - n)).astype(dtype).reshape(total // LANES, LANES)


def kernel(x, positions, ln_g, ln_b, a_w_in, a_b_in, a_conv_w, a_conv_b, a_norm_g, a_norm_b, a_w_out, a_b_out, kv_w_down, kv_norm_g, kv_w_uk, kv_w_uv, b_w_in, b_q_norm_g, b_w_uq, b_w_out, loss_target, m_ln_g, m_ln_b, m_a_w_in, m_a_b_in, m_a_conv_w, m_a_conv_b, m_a_norm_g, m_a_norm_b, m_a_w_out, m_a_b_out, m_kv_w_down, m_kv_norm_g, m_kv_w_uk, m_kv_w_uv, m_b_w_in, m_b_q_norm_g, m_b_w_uq, m_b_w_out, v_ln_g, v_ln_b, v_a_w_in, v_a_b_in, v_a_conv_w, v_a_conv_b, v_a_norm_g, v_a_norm_b, v_a_w_out, v_a_b_out, v_kv_w_down, v_kv_norm_g, v_kv_w_uk, v_kv_w_uv, v_b_w_in, v_b_q_norm_g, v_b_w_uq, v_b_w_out):
    T, D = x.shape[1], x.shape[2]
    E = N_CHIPS * a_w_out.shape[1]
    KC = a_conv_w.shape[1]
    RKV = kv_norm_g.shape[0]
    H, DN = kv_w_uk.shape[1], kv_w_uk.shape[2]
    RQ = b_q_norm_g.shape[1]
    HV = N_CHIPS * b_w_out.shape[1]
    assert DN == LANES and kv_w_uv.shape[2] == LANES and HV == H * LANES
    assert kv_w_down.shape[1] == RKV + ROPE_DIM and b_w_uq.shape[3] == DN + ROPE_DIM
    assert ln_g.shape[0] == 2 and a_w_in.shape[0] == 1 and b_w_in.shape[0] == 1
    alpha = (2.0 * ln_g.shape[0]) ** 0.25
    scale = 1.0 / math.sqrt(DN + ROPE_DIM)
    WK = -(-(RKV + LANES) // 256) * 256
    assert WK % RQ == 0
    Z_OFF = WK + RQ
    tmw, tq = min(TM_WIDE, T), min(TQ, T)
    t512, t1024 = _fit(512, T), _fit(1024, T)
    xs = x[0]
    tgt = loss_target[0]
    px, py = lax.axis_index("x"), lax.axis_index("y")
    chip = 2 * px + py

    mats = [a_w_out[0], kv_w_down, kv_w_uk, kv_w_uv, b_w_in[0], b_w_uq[0], b_w_out[0]]
    vecs = [a_b_in[0], a_conv_w[0], a_conv_b[0], a_norm_g[0], a_norm_b[0], a_b_out[0]]
    vec_bits = jnp.concatenate([lax.bitcast_convert_type(w.reshape(-1), BF16).reshape(-1) for w in vecs])
    rest = [w.astype(BF16) for w in mats]
    g_win, gathered = _all_gather_chips(
        [a_w_in[0].astype(BF16), _pack_rows(vec_bits, BF16, 2 * BF16_ROWS * LANES)])
    gathered = gathered.reshape(N_CHIPS, -1)
    gathered, rest = lax.optimization_barrier((gathered, rest))
    rest_sems = _push_start("gather_rest_start", rest, by_target=False)
    off = 0
    fvec = []
    for w in vecs:
        bits = gathered[:, off:off + 2 * w.size].reshape((N_CHIPS,) + w.shape + (2,))
        fvec.append(lax.bitcast_convert_type(bits, F32))
        off += 2 * w.size
    cols = lambda g: jnp.moveaxis(g, 0, -2).reshape(g.shape[1:-1] + (N_CHIPS * g.shape[-1],))
    b_in = cols(fvec[0][:, None, :])
    conv_w = cols(fvec[1])
    conv_b, norm_g, norm_b, b_out = (cols(f[:, None, :]) for f in fvec[2:])
    row = lambda a: a.reshape(1, -1)
    g0, b0, g1, b1 = row(ln_g[0]), row(ln_b[0]), row(ln_g[1]), row(ln_b[1])
    kv_g, q_g = row(kv_norm_g), row(b_q_norm_g[0])
    plain = lambda acc, ins, i, j: [acc]

    b_in = b_in + rest_sems[4][0, 0]
    (proj,) = _row_mm("a_in", [((xs,), None)], g_win, nt=False, tm=t1024, tn=3 * E // N_CHIPS, tk=D,
                      outs=[((T, 3 * E), F32, 'tile')], epi=lambda acc, ins, i, j: [acc + ins[0]],
                      epi_ins=[(b_in, 'col')])
    u1 = _conv_fwd(proj, conv_w, conv_b, E, tmw)
    u4 = _conv_post(u1, proj, norm_g, norm_b, E, tmw)

    rest, landed = _push_wait("gather_rest_wait", *rest_sems[:4], after=[u4], by_target=False)
    g_wout, g_wd, g_uk, g_uv, g_wbin, g_wuq, g_wbout = [
        lax.dynamic_update_slice(l, w[None], (chip,) + (0,) * w.ndim) for w, l in zip(rest, landed)]
    w_out = g_wout.reshape(E, D)
    wd = g_wd.reshape(D, RKV + ROPE_DIM)
    zpad = jnp.zeros((D, ROPE_HALF), BF16)
    wd_p = jnp.concatenate(
        [wd[:, :RKV], wd[:, RKV:RKV + ROPE_HALF], zpad, wd[:, RKV + ROPE_HALF:], zpad,
         jnp.zeros((D, WK - RKV - LANES), BF16)], axis=1)
    w_bin = cols(g_wbin)
    w_z = w_bin[:, RQ:]
    wb_small = jnp.concatenate([wd_p, w_bin[:, :RQ]], axis=1)
    wb_all = jnp.concatenate([wd_p, w_bin], axis=1)
    w_kv = jnp.concatenate([g_uk.reshape(RKV, HV), g_uv.reshape(RKV, HV)], axis=1)
    wuq = g_wuq.reshape(RQ, H, DN + ROPE_DIM)
    zq = jnp.zeros((RQ, H, ROPE_HALF), BF16)
    w_qr = jnp.concatenate([wuq[:, :, DN:DN + ROPE_HALF], zq, wuq[:, :, DN + ROPE_HALF:], zq], axis=2)
    w_q = jnp.concatenate([wuq[:, :, :DN].reshape(RQ, HV), w_qr.reshape(RQ, HV)], axis=1)
    w_bout = g_wbout.reshape(HV, D)

    freqs = ROPE_THETA ** (-jnp.arange(0, ROPE_DIM, 2, dtype=F32) / ROPE_DIM)
    ang = positions[0].astype(F32)[:, None] * freqs
    cs, sn = jnp.cos(ang), jnp.sin(ang)
    ones, zeros = jnp.ones_like(cs), jnp.zeros_like(cs)
    cos_t = jnp.concatenate([cs, ones, cs, ones], axis=1)
    sin_t = jnp.concatenate([-sn, zeros, sn, zeros], axis=1)

    def ln_epi(acc, ins, i, j):
        bias, res, g, b = ins
        xhat, rstd = _ln_stats(alpha * res + acc + bias)
        h = xhat * g + b
        return [h, h, xhat, rstd]

    h1, h1b, xhat1, rstd1 = _row_mm(
        "a_out", [((u4,), None)], w_out, nt=False, tm=t512, tn=D, tk=_fit(2048, E),
        outs=[((T, D), F32, 'tile'), ((T, D), BF16, 'tile'), ((T, D), F32, 'tile'), ((T, 1), F32, 'row')],
        epi=ln_epi, epi_ins=[(b_out, 'col'), (xs, 'tile'), (g0, 'col'), (b0, 'col')])

    tkb = _fit(512, _gcd(WK, RQ, HV))
    def latents_epi(acc, ins, i, j):
        kg, qg, cos_, sin_ = ins
        return [acc, _rms_fwd(acc[:, :RKV], kg), _rope(acc[:, RKV:RKV + LANES], cos_, sin_),
                _rms_fwd(acc[:, WK:WK + RQ], qg)]

    whole_row = lambda a: (a, pl.BlockSpec(a.shape, lambda i, j, k: (0, 0)))
    pb, c_lat, kr, cqn = _row_mm(
        "b_in", [((h1b,), None)], wb_small, nt=False, tm=t512, tn=Z_OFF, tk=_fit(1024, D),
        outs=[((T, Z_OFF), F32, 'tile'), ((T, RKV), BF16, 'row'), ((T, LANES), BF16, 'row'),
              ((T, RQ), BF16, 'row')],
        epi=latents_epi, epi_ins=[whole_row(kv_g), whole_row(q_g), (cos_t, 'row'), (sin_t, 'row')])
    (zb,) = _row_mm("b_in_gate", [((h1b,), None)], w_z, nt=False, tm=t1024, tn=_fit(2048, HV),
                    tk=_fit(1024, D), outs=[((T, HV), BF16, 'tile')], epi=plain)
    (kv,) = _row_mm("kv_up", [((c_lat,), None)], w_kv, nt=False, tm=t1024, tn=_fit(2048, HV),
                    tk=_fit(1024, RKV), outs=[((T, 2 * HV), BF16, 'tile')], epi=plain)
    tnq = _fit(2048, HV)
    half_q = HV // tnq

    def q_epi(acc, ins, i, j):
        return [jnp.where(j >= half_q, _rope(acc, ins[0], ins[1]), acc)]

    (q_all,) = _row_mm("q_up", [((cqn,), None)], w_q, nt=False, tm=t1024, tn=tnq, tk=_fit(1024, RQ),
                       outs=[((T, 2 * HV), BF16, 'tile')], epi=q_epi,
                       epi_ins=[(cos_t, 'row'), (sin_t, 'row')])
    o, lse = _attn_fwd(q_all, kv, kr, H, tq, scale)

    def loss_epi(acc, ins, i, j):
        res, g, b, target = ins
        xhat, rstd = _ln_stats(alpha * res + acc)
        diff = xhat * g + b - target
        dr, dg, db = _ln_bwd(diff / D, xhat, rstd, g)
        return [dr, 0.5 * jnp.sum(diff * diff, keepdims=True) / D, dg, db]

    dr1, loss_part, dg1, db1 = _row_mm(
        "b_out", [((o, zb), _gate)], w_bout, nt=False, tm=t512, tn=D, tk=_fit(2048, HV),
        outs=[((T, D), F32, 'tile'), ((1, 1), F32, 'acc'), ((1, D), F32, 'acc'), ((1, D), F32, 'acc')],
        epi=loss_epi, epi_ins=[(h1, 'tile'), (g1, 'col'), (b1, 'col'), (tgt, 'tile')])

    def gate_bwd_epi(acc, ins, i, j):
        gate, gate_grad = _silu_and_grad(ins[1].astype(F32))
        return [acc * gate, acc * ins[0].astype(F32) * gate_grad]

    do, dz = _row_mm(
        "b_out_bwd", [((dr1,), None)], w_bout, nt=True, tm=t512, tn=_fit(2048, HV), tk=_fit(1024, D),
        outs=[((T, HV), BF16, 'tile'), ((T, HV), BF16, 'tile')], epi=gate_bwd_epi,
        epi_ins=[(o, 'tile'), (zb, 'tile')])
    gw_bout = _tn_mm("dw_b_out", (o, zb), _gate, [((dr1,), None)], tn=_fit(1024, D), tk=t512, out_dtype=BF16)
    dqn, dqr_pre, dkn, dkr_h, dv = _attn_bwd(q_all, kv, kr, do, o, lse, cos_t, sin_t, H, tq, scale)
    dkr_pre = _key_rope_bwd(dkr_h, cos_t, sin_t, H, tmw)

    def cq_bwd_epi(acc, ins, i, j):
        dx, dg = _rms_bwd(acc, ins[0], ins[1])
        return [dx, dg]

    dcq, dqg = _row_mm(
        "q_up_bwd", [((dqn,), None), ((dqr_pre,), None)], w_q, nt=True, tm=t1024, tn=RQ, tk=_fit(2048, HV),
        outs=[((T, RQ), BF16, 'tile'), ((1, RQ), F32, 'acc')], epi=cq_bwd_epi,
        epi_ins=[(pb, pl.BlockSpec((t1024, RQ), lambda i, j, k: (i, WK // RQ))), (q_g, 'col')])
    gw_q = _tn_mm("dw_q_up", (cqn,), None, [((dqn,), None), ((dqr_pre,), None)],
                  tn=_fit(2048, HV), tk=t1024, out_dtype=BF16)

    def ckv_bwd_epi(acc, ins, i, j):
        blk, dkr_t, g = ins
        dx, dg = _rms_bwd(acc, blk[:, :RKV], g)
        parts = [dx, dkr_t]
        if WK > RKV + LANES:
            parts.append(jnp.zeros((dx.shape[0], WK - RKV - LANES), F32))
        return [jnp.concatenate(parts, axis=1), dg]

    dckv, dkvg = _row_mm(
        "kv_up_bwd", [((dkn,), None), ((dv,), None)], w_kv, nt=True, tm=t1024, tn=RKV, tk=_fit(2048, HV),
        outs=[((T, WK), BF16, pl.BlockSpec((t1024, WK), lambda i, j, k: (i, 0))), ((1, RKV), F32, 'acc')],
        epi=ckv_bwd_epi,
        epi_ins=[(pb, pl.BlockSpec((t1024, WK), lambda i, j, k: (i, 0))), (dkr_pre, 'row'), (kv_g, 'col')])
    gw_kv = _tn_mm("dw_kv_up", (c_lat,), None, [((dkn,), None), ((dv,), None)],
                   tn=_fit(2048, HV), tk=t1024, out_dtype=BF16)

    def ln1_bwd_epi(acc, ins, i, j):
        dr_up, xhat, rstd, g = ins
        dr, dg, db = _ln_bwd(alpha * dr_up + acc, xhat, rstd, g)
        return [dr, dg, db]

    dp_segs = [((dckv,), None), ((dcq,), None), ((dz,), None)]
    gw_ball = _tn_mm("dw_b_in", (h1b,), None, dp_segs, tn=tkb, tk=t1024, out_dtype=BF16)

    shard_cols = lambda g: jnp.moveaxis(g.reshape(g.shape[0], N_CHIPS, -1), 1, 0)
    shard_rows = lambda g: g.reshape(N_CHIPS, g.shape[0] // N_CHIPS, g.shape[1])
    gq = gw_q.reshape(RQ, 2, H, LANES)
    g_uq = jnp.concatenate(
        [gq[:, 0], gq[:, 1, :, :ROPE_HALF], gq[:, 1, :, 2 * ROPE_HALF:3 * ROPE_HALF]], axis=2)
    g_wd_full = jnp.concatenate(
        [gw_ball[:, :RKV], gw_ball[:, RKV:RKV + ROPE_HALF],
         gw_ball[:, RKV + 2 * ROPE_HALF:RKV + 3 * ROPE_HALF]], axis=1)
    late_names = ["kv_w_down", "kv_w_uk", "kv_w_uv", "b_w_in", "b_w_uq", "b_w_out"]
    late_w = [kv_w_down, kv_w_uk, kv_w_uv, b_w_in, b_w_uq, b_w_out]
    chip_major = [g_wd_full, gw_kv[:, :HV], gw_kv[:, HV:], shard_cols(gw_ball[:, WK:]), g_uq, gw_bout]
    late_grads = [g.reshape((N_CHIPS,) + w.shape) for g, w in zip(chip_major, late_w)]
    late_sems = _push_start("scatter_late_start", late_grads, by_target=True)

    dr0, dg0, db0 = _row_mm(
        "b_in_bwd", dp_segs, wb_all, nt=True, tm=t1024, tn=D, tk=tkb,
        outs=[((T, D), F32, 'tile'), ((1, D), F32, 'acc'), ((1, D), F32, 'acc')], epi=ln1_bwd_epi,
        epi_ins=[(dr1, 'tile'), (xhat1, 'tile'), (rstd1, 'row'), (g0 + late_sems[4][0, 0], 'col')])

    def conv_branch_bwd_epi(acc, ins, i, j):
        u1_t, z, g, b = ins
        xhat, rstd = _ln_stats(u1_t)
        u2 = xhat * g + b
        gate, gate_grad = _silu_and_grad(z)
        act, act_grad = _silu_and_grad(u2)
        dz_a = acc * act * gate_grad
        du1, dg, db = _ln_bwd(acc * gate * act_grad, xhat, rstd, g)
        return [du1, dz_a, dg, db]

    du1, dz_a, dng, dnb = _row_mm(
        "a_out_bwd", [((dr0,), None)], w_out, nt=True, tm=tmw, tn=E, tk=_fit(1024, D),
        outs=[((T, E), F32, 'tile'), ((T, E), BF16, 'tile'), ((1, E), F32, 'acc'), ((1, E), F32, 'acc')],
        epi=conv_branch_bwd_epi,
        epi_ins=[(u1, 'tile'), (proj, pl.BlockSpec((tmw, E), lambda i, j, k: (i, 2))), (norm_g, 'col'),
                 (norm_b, 'col')])
    gw_out, dbo = _tn_mm("dw_a_out", (u4,), None, [((dr0,), None)], tn=_fit(1024, D), tk=t1024,
                         out_dtype=BF16, colsum=True)
    mid_sems = _push_start("scatter_mid_start", [gw_out.reshape((N_CHIPS,) + a_w_out.shape)], by_target=True)
    dval, dgate, dcw, dcb = _conv_bwd(du1, proj, conv_w + mid_sems[4][0, 0], E, tmw)
    dproj_segs = [((dval,), None), ((dgate,), None), ((dz_a,), None)]
    gw_in, dbi = _tn_mm("dw_a_in", (xs,), None, dproj_segs, tn=_fit(2048, E), tk=t1024, out_dtype=BF16,
                        colsum=True)

    chip_word = chip.reshape(1).astype(jnp.int32)

    def reduce_and_update(tag, names_, sent, landed, w_, m_, v_, by_target=True):
        sums = [_sum_chips("sum_" + n, l, s, chip_word, by_target) for n, s, l in zip(names_, sent, landed)]
        theirs = _swap_cores("swap_cores_" + tag, sums)
        return {n: _adamw("adamw_" + n, [mine, other], w, m, v)
                for n, mine, other, w, m, v in zip(names_, sums, theirs, w_, m_, v_)}

    late_sent, late_landed = _push_wait("scatter_late_wait", *late_sems[:4], after=[dbi], by_target=True)
    mid_sent, mid_landed = _push_wait("scatter_mid_wait", *mid_sems[:4], after=[dbi], by_target=True)
    early_sems = _push_start("scatter_early_start", [gw_in.reshape(a_w_in.shape[:-1] + (3 * E,))],
                             by_target='cols')
    (grad_x,) = _row_mm(
        "a_in_bwd", dproj_segs, g_win, nt=True, tm=t512, tn=D, tk=E, b_whole=True,
        outs=[((T, D), F32, 'tile')], epi=lambda acc, ins, i, j: [alpha * ins[0] + acc + ins[1]],
        epi_ins=[(dr0, 'tile'), (jnp.zeros((1, D), F32) + early_sems[4][0, 0], 'col')])
    big_out = reduce_and_update(
        "late", ["a_w_out"] + late_names, mid_sent + late_sent, mid_landed + late_landed,
        [a_w_out] + late_w, [m_a_w_out, m_kv_w_down, m_kv_w_uk, m_kv_w_uv, m_b_w_in, m_b_w_uq, m_b_w_out],
        [v_a_w_out, v_kv_w_down, v_kv_w_uk, v_kv_w_uv, v_b_w_in, v_b_w_uq, v_b_w_out])

    small_full = [jnp.concatenate([dg0, dg1]), jnp.concatenate([db0, db1]), dbi, dcw, dcb, dng, dnb, dbo,
                  dkvg, dqg]
    sflat = jnp.concatenate([g.reshape(-1) for g in small_full])
    summed = _all_reduce_small(_pack_rows(sflat, F32, 8 * LANES)).reshape(-1)
    soff = 0
    sgrads = []
    for g in small_full:
        sgrads.append(summed[soff:soff + g.size].reshape(g.shape))
        soff += g.size
    local_cols = lambda g, n: lax.dynamic_slice_in_dim(g, chip * n, n, axis=g.ndim - 1)
    snames = ["ln_g", "ln_b", "a_b_in", "a_conv_w", "a_conv_b", "a_norm_g", "a_norm_b", "a_b_out",
              "kv_norm_g", "b_q_norm_g"]
    small_w = [ln_g, ln_b, a_b_in, a_conv_w, a_conv_b, a_norm_g, a_norm_b, a_b_out, kv_norm_g, b_q_norm_g]
    small_m = [m_ln_g, m_ln_b, m_a_b_in, m_a_conv_w, m_a_conv_b, m_a_norm_g, m_a_norm_b, m_a_b_out,
               m_kv_norm_g, m_b_q_norm_g]
    small_v = [v_ln_g, v_ln_b, v_a_b_in, v_a_conv_w, v_a_conv_b, v_a_norm_g, v_a_norm_b, v_a_b_out,
               v_kv_norm_g, v_b_q_norm_g]
    sharded = {"a_b_in", "a_conv_w", "a_conv_b", "a_norm_g", "a_norm_b", "a_b_out"}
    local_g = [(local_cols(g, w.shape[-1]) if n in sharded else g).reshape(w.shape)
               for n, g, w in zip(snames, sgrads, small_w)]
    at_least_2d = lambda a: a.reshape((1,) + a.shape) if a.ndim == 1 else a
    sres = _adamw_vectors([tuple(at_least_2d(a) for a in item)
                           for item in zip(local_g, small_w, small_m, small_v)])
    small_out = {n: [g] + [r.reshape(w.shape) for r in res]
                 for n, g, w, res in zip(snames, local_g, small_w, sres)}

    early_sent, early_landed = _push_wait(
        "scatter_early_wait", *early_sems[:4], by_target='cols',
        after=[grad_x, big_out["b_w_out"][1], small_out["b_q_norm_g"][1]])
    big_out.update(reduce_and_update("early", ["a_w_in"], early_sent, early_landed,
                                     [a_w_in], [m_a_w_in], [v_a_w_in], by_target='cols'))

    loss = lax.psum(loss_part[0, 0], ("x", "y", "c"))
    order = ["ln_g", "ln_b", "a_w_in", "a_b_in", "a_conv_w", "a_conv_b", "a_norm_g", "a_norm_b", "a_w_out",
             "a_b_out", "kv_w_down", "kv_norm_g", "kv_w_uk", "kv_w_uv", "b_w_in", "b_q_norm_g", "b_w_uq",
             "b_w_out"]
    outs = {**big_out, **small_out}
    result = [loss, grad_x[None]]
    for part in range(4):
        result += [outs[n][part] for n in order]
    return tuple(result)
```

```python
import functools
import math

import jax
import jax.numpy as jnp
from jax import lax
from jax.experimental import pallas as pl
from jax.experimental.pallas import tpu as pltpu

F32, BF16 = jnp.float32, jnp.bfloat16
NN = (((1,), (0,)), ((), ()))
NT = (((1,), (1,)), ((), ()))
TN = (((0,), (0,)), ((), ()))
MESH = pl.DeviceIdType.MESH
ANY = pl.BlockSpec(memory_space=pl.ANY)

LANES = 128
BF16_ROWS = 16
VMEM_LIMIT = 56 * 1024 * 1024
N_CHIPS = 4
LN_EPS = 1e-5
RMS_EPS = 1e-6
MASK_VALUE = -1e30
LOG2_E = math.log2(math.e)
ROPE_THETA = 10000.0
ROPE_DIM = 64
ROPE_HALF = ROPE_DIM // 2
ADAM_LR, ADAM_B1, ADAM_B2, ADAM_EPS, ADAM_WD, ADAM_STEP = 0.001, 0.9, 0.999, 1e-08, 0.01, 10

MAX_TILE = 2048
TM_WIDE = 256
TQ = 512
CONV_HALO = 32
CONV_LC = 1024
CONV_SUB = 256
SUBLANES = 8
CONV_RC = 32
ADAM_ROWS = 128


def _dot(a, b, dims):
    return lax.dot_general(a.astype(BF16), b.astype(BF16), dims, preferred_element_type=F32)


def _sig(x):
    return 0.5 * jnp.tanh(0.5 * x) + 0.5


def _params(n_axes):
    return pltpu.CompilerParams(dimension_semantics=("arbitrary",) * n_axes, vmem_limit_bytes=VMEM_LIMIT)


def _gcd(*v):
    return functools.reduce(math.gcd, v)


def _fit(want, dim):
    return math.gcd(min(want, MAX_TILE), dim)


def _row_mm(name, a_segs, b, *, nt, tm, tn, tk, outs, epi, epi_ins=(), b_whole=False):
    M = a_segs[0][0][0].shape[0]
    stacked = b.ndim == 3
    if stacked:
        n_blk = b.shape[2]
        N = b.shape[1] if nt else N_CHIPS * n_blk
        assert (nt and b_whole) or (not nt and tn == n_blk and tk == b.shape[1]), name
    else:
        N = b.shape[0] if nt else b.shape[1]
    nkb = [arrs[0].shape[1] // tk for arrs, _ in a_segs]
    koff = [sum(nkb[:s]) for s in range(len(nkb))]
    ni, nj, nk = M // tm, N // tn, sum(nkb)
    assert M % tm == 0 and N % tn == 0 and all(arrs[0].shape[1] % tk == 0 for arrs, _ in a_segs), name
    assert stacked or (b.shape[1] if nt else b.shape[0]) == nk * tk, name

    def spec_of(shape, kind):
        if isinstance(kind, pl.BlockSpec):
            return kind
        if kind == 'tile':
            return pl.BlockSpec((tm, tn), lambda i, j, k: (i, j))
        if kind == 'row':
            return pl.BlockSpec((tm, shape[1]), lambda i, j, k: (i, 0))
        if kind == 'col':
            return pl.BlockSpec((1, tn), lambda i, j, k: (0, j))
        assert kind == 'acc' and nj == 1, name
        return pl.BlockSpec(shape, lambda i, j, k: (0,) * len(shape))

    in_specs, operands = [], []
    for s, (arrs, _) in enumerate(a_segs):
        for arr in arrs:
            in_specs.append(pl.BlockSpec(
                (tm, tk), lambda i, j, k, s=s: (i, jnp.clip(k - koff[s], 0, nkb[s] - 1))))
            operands.append(arr)
    if b_whole:
        assert nt and nj == 1 and all(n == 1 for n in nkb), name
        in_specs.append(pl.BlockSpec(b.shape, lambda i, j, k: (0,) * b.ndim))
    elif stacked:
        in_specs.append(pl.BlockSpec((1, tk, tn), lambda i, j, k: (j, 0, 0)))
    else:
        in_specs.append(pl.BlockSpec((tn, tk), lambda i, j, k: (j, k)) if nt
                        else pl.BlockSpec((tk, tn), lambda i, j, k: (k, j)))
    operands.append(b)
    for arr, kind in epi_ins:
        in_specs.append(spec_of(arr.shape, kind))
        operands.append(arr)
    out_specs = [spec_of(shape, kind) for shape, _, kind in outs]
    out_shape = [jax.ShapeDtypeStruct(shape, dtype) for shape, dtype, _ in outs]
    n_seg_refs = [len(arrs) for arrs, _ in a_segs]

    def body(*refs):
        pos = 0
        seg_refs = []
        for n in n_seg_refs:
            seg_refs.append(refs[pos:pos + n])
            pos += n
        b_ref = refs[pos]
        e_refs = refs[pos + 1:pos + 1 + len(epi_ins)]
        o_refs = refs[pos + 1 + len(epi_ins):pos + 1 + len(epi_ins) + len(outs)]
        i, j, k = pl.program_id(0), pl.program_id(1), pl.program_id(2)

        def product(fn, rs, s=0):
            a = rs[0][...] if fn is None else fn(*[r[...] for r in rs])
            if b_whole and stacked:
                lo, hi, tot = koff[s] * tk, (koff[s] + 1) * tk, None
                for q in range(lo // n_blk, (hi - 1) // n_blk + 1):
                    c0, c1 = max(lo, q * n_blk), min(hi, (q + 1) * n_blk)
                    part = _dot(a[:, c0 - lo:c1 - lo], b_ref[q, :, c0 - q * n_blk:c1 - q * n_blk], NT)
                    tot = part if tot is None else tot + part
                return tot
            if b_whole:
                return _dot(a, b_ref[:, koff[s] * tk:(koff[s] + 1) * tk], NT)
            return _dot(a, b_ref[0] if stacked else b_ref[...], NT if nt else NN)

        def finish(acc):
            res = epi(acc, [r[...] for r in e_refs], i, j)
            for o_ref, (_, _, kind), r in zip(o_refs, outs, res):
                if isinstance(kind, str) and kind == 'acc':
                    @pl.when(i == 0)
                    def _(o_ref=o_ref, r=r):
                        o_ref[...] = r

                    @pl.when(i > 0)
                    def _(o_ref=o_ref, r=r):
                        o_ref[...] += r
                else:
                    o_ref[...] = r.astype(o_ref.dtype)

        if nk == 1:
            finish(product(a_segs[0][1], seg_refs[0]))
            return
        acc_ref = refs[-1]

        @pl.when(k == 0)
        def _():
            acc_ref[...] = jnp.zeros_like(acc_ref)

        for s, ((_, fn), rs) in enumerate(zip(a_segs, seg_refs)):
            def accumulate(fn=fn, rs=rs, s=s):
                acc_ref[...] += product(fn, rs, s)
            if len(a_segs) == 1:
                accumulate()
            else:
                pl.when(jnp.logical_and(k >= koff[s], k < koff[s] + nkb[s]))(accumulate)

        @pl.when(k == nk - 1)
        def _():
            finish(acc_ref[...])

    return pl.pallas_call(
        body, name=name, grid=(ni, nj, nk), in_specs=in_specs, out_specs=out_specs, out_shape=out_shape,
        scratch_shapes=[] if nk == 1 else [pltpu.VMEM((tm, tn), F32)], compiler_params=_params(3),
    )(*operands)


def _tn_mm(name, a_arrs, a_fn, b_segs, *, tn, tk, out_dtype, shard_major=False, colsum=False):
    T, M = a_arrs[0].shape
    nbj = [arrs[0].shape[1] // tn for arrs, _ in b_segs]
    joff = [sum(nbj[:s]) for s in range(len(nbj))]
    nj, nk = sum(nbj), T // tk
    N = nj * tn
    assert T % tk == 0 and all(arrs[0].shape[1] % tn == 0 for arrs, _ in b_segs), name

    in_specs = [pl.BlockSpec((tk, M), lambda j, k: (k, 0)) for _ in a_arrs]
    operands = list(a_arrs)
    for s, (arrs, _) in enumerate(b_segs):
        for arr in arrs:
            in_specs.append(pl.BlockSpec(
                (tk, tn), lambda j, k, s=s: (k, jnp.clip(j - joff[s], 0, nbj[s] - 1))))
            operands.append(arr)
    if shard_major:
        per = (N // N_CHIPS) // tn
        assert per * tn * N_CHIPS == N, name
        out_shape = [jax.ShapeDtypeStruct((N_CHIPS, M, N // N_CHIPS), out_dtype)]
        out_specs = [pl.BlockSpec((1, M, tn), lambda j, k: (j // per, 0, j % per))]
    else:
        out_shape = [jax.ShapeDtypeStruct((M, N), out_dtype)]
        out_specs = [pl.BlockSpec((M, tn), lambda j, k: (0, j))]
    if colsum:
        out_shape.append(jax.ShapeDtypeStruct((1, N), F32))
        out_specs.append(pl.BlockSpec((1, tn), lambda j, k: (0, j)))
    n_a = len(a_arrs)
    n_seg_refs = [len(arrs) for arrs, _ in b_segs]

    def body(*refs):
        a_refs = refs[:n_a]
        pos = n_a
        seg_refs = []
        for n in n_seg_refs:
            seg_refs.append(refs[pos:pos + n])
            pos += n
        o_ref = refs[pos]
        cs_ref = refs[pos + 1] if colsum else None
        acc_ref = refs[-1]
        j, k = pl.program_id(0), pl.program_id(1)

        @pl.when(k == 0)
        def _():
            acc_ref[...] = jnp.zeros_like(acc_ref)
            if colsum:
                cs_ref[...] = jnp.zeros_like(cs_ref)

        for s, ((_, fn), rs) in enumerate(zip(b_segs, seg_refs)):
            def accumulate(fn=fn, rs=rs):
                a = a_refs[0][...] if a_fn is None else a_fn(*[r[...] for r in a_refs])
                bt = rs[0][...] if fn is None else fn(*[r[...] for r in rs])
                acc_ref[...] += _dot(a, bt, TN)
                if colsum:
                    cs_ref[...] += jnp.sum(bt.astype(F32), axis=0, keepdims=True)
            if len(b_segs) == 1:
                accumulate()
            else:
                pl.when(jnp.logical_and(j >= joff[s], j < joff[s] + nbj[s]))(accumulate)

        @pl.when(k == nk - 1)
        def _():
            if shard_major:
                o_ref[0] = acc_ref[...].astype(o_ref.dtype)
            else:
                o_ref[...] = acc_ref[...].astype(o_ref.dtype)

    res = pl.pallas_call(
        body, name=name, grid=(nj, nk), in_specs=in_specs, out_specs=out_specs, out_shape=out_shape,
        scratch_shapes=[pltpu.VMEM((M, tn), F32)], compiler_params=_params(2),
    )(*operands)
    return res if colsum else res[0]


def _silu(z):
    return z * _sig(z)


def _silu_and_grad(z):
    s = _sig(z)
    return z * s, s * (1.0 + z * (1.0 - s))


def _gate(o, z):
    return o.astype(F32) * _silu(z.astype(F32))


def _ln_stats(r):
    mu = jnp.mean(r, axis=1, keepdims=True)
    xc = r - mu
    var = jnp.mean(xc * xc, axis=1, keepdims=True)
    rstd = lax.rsqrt(var + LN_EPS)
    return xc * rstd, rstd


def _ln_bwd(dy, xhat, rstd, g):
    dxh = dy * g
    m1 = jnp.mean(dxh, axis=1, keepdims=True)
    m2 = jnp.mean(dxh * xhat, axis=1, keepdims=True)
    return (rstd * (dxh - m1 - xhat * m2), jnp.sum(dy * xhat, axis=0, keepdims=True),
            jnp.sum(dy, axis=0, keepdims=True))


def _rms_fwd(x, g):
    rstd = lax.rsqrt(jnp.mean(x * x, axis=1, keepdims=True) + RMS_EPS)
    return x * rstd * g


def _rms_bwd(dy, x, g):
    rstd = lax.rsqrt(jnp.mean(x * x, axis=1, keepdims=True) + RMS_EPS)
    xn = x * rstd
    dxn = dy * g
    return rstd * (dxn - xn * jnp.mean(dxn * xn, axis=1, keepdims=True)), jnp.sum(dy * xn, axis=0, keepdims=True)


def _rope(x, cos, sin, transpose=False):
    parts = []
    for g in range(x.shape[1] // LANES):
        xg = x[:, g * LANES:(g + 1) * LANES]
        if transpose:
            parts.append(xg * cos + pltpu.roll(xg * sin, LANES // 2, 1))
        else:
            parts.append(xg * cos + pltpu.roll(xg, LANES // 2, 1) * sin)
    return parts[0] if len(parts) == 1 else jnp.concatenate(parts, axis=1)


def _shifted_rows(window, rc):
    n = window.shape[0]
    for b in range(SUBLANES):
        rolled = window if b == 0 else pltpu.roll(window, n - b, 0)
        for a8 in range(0, n - rc - b + 1, SUBLANES):
            yield a8 + b, rolled[a8:a8 + rc]


def _conv_fwd(proj, conv_w, conv_b, norm_g, norm_b, E, tm):
    T = proj.shape[0]
    kc = conv_w.shape[0]
    hb, rc = CONV_HALO, min(CONV_RC, tm)
    ni, ratio = T // tm, tm // hb
    sub = min(CONV_SUB, E)
    base = hb - (kc - 1)

    def body(val_ref, gate_ref, z_ref, valh_ref, gateh_ref, w_ref, cb_ref, g_ref, b_ref, u1_ref, u4_ref, ubuf):
        i = pl.program_id(0)
        ubuf[hb:, :] = val_ref[...] * _sig(gate_ref[...])
        halo = valh_ref[...] * _sig(gateh_ref[...])
        ubuf[0:hb, :] = jnp.where(i > 0, halo, 0.0)
        for l0 in range(0, E, sub):
            ls = slice(l0, l0 + sub)
            for r0 in range(0, tm, rc):
                acc = jnp.zeros((rc, sub), F32) + cb_ref[:, ls]
                for off, rows in _shifted_rows(ubuf[r0:r0 + hb + rc, ls], rc):
                    if 0 <= off - base < kc:
                        acc += w_ref[off - base:off - base + 1, ls] * rows
                u1_ref[r0:r0 + rc, ls] = acc
        xhat, _ = _ln_stats(u1_ref[...])
        u4_ref[...] = (_silu(xhat * g_ref[...] + b_ref[...]) * _silu(z_ref[...])).astype(BF16)

    main = lambda col: pl.BlockSpec((tm, E), lambda i: (i, col))
    halo = lambda col: pl.BlockSpec((hb, E), lambda i: (jnp.maximum(i * ratio - 1, 0), col))
    whole = lambda a: pl.BlockSpec(a.shape, lambda i: (0, 0))
    return pl.pallas_call(
        body, name="conv_fwd", grid=(ni,),
        in_specs=[main(0), main(1), main(2), halo(0), halo(1), whole(conv_w), whole(conv_b), whole(norm_g),
                  whole(norm_b)],
        out_specs=[main(0), main(0)],
        out_shape=[jax.ShapeDtypeStruct((T, E), F32), jax.ShapeDtypeStruct((T, E), BF16)],
        scratch_shapes=[pltpu.VMEM((hb + tm, E), F32)], compiler_params=_params(1),
    )(proj, proj, proj, proj, proj, conv_w, conv_b, norm_g, norm_b)


def _conv_bwd(du1, proj, conv_w, E, tm):
    T = du1.shape[0]
    kc = conv_w.shape[0]
    lc, hb, rc = min(CONV_LC, E), CONV_HALO, min(CONV_RC, tm)
    nl, ni, ratio = E // lc, T // tm, tm // hb
    gate_off = E // lc
    last_halo = T // hb - 1

    sub = min(CONV_SUB, lc)
    base = hb - (kc - 1)

    def body(du_ref, dun_ref, val_ref, gate_ref, valh_ref, gateh_ref, w_ref,
             dval_ref, dgate_ref, dw_ref, db_ref, ubuf, dbuf, sbuf, dw_sc):
        i = pl.program_id(1)
        sbuf[...] = _sig(gate_ref[...])
        ubuf[hb:, :] = val_ref[...] * sbuf[...]
        halo = valh_ref[...] * _sig(gateh_ref[...])
        ubuf[0:hb, :] = jnp.where(i > 0, halo, 0.0)
        dbuf[0:tm, :] = du_ref[...]
        dbuf[tm:, :] = jnp.where(i < ni - 1, dun_ref[...], 0.0)

        @pl.when(i == 0)
        def _():
            dw_sc[...] = jnp.zeros_like(dw_sc)
            db_ref[...] = jnp.zeros_like(db_ref)

        db_ref[...] += jnp.sum(du_ref[...], axis=0, keepdims=True)
        for l0 in range(0, lc, sub):
            ls = slice(l0, l0 + sub)
            for r0 in range(0, tm, rc):
                dwin = dbuf[r0:r0 + rc + hb, ls]
                dchunk = dwin[0:rc]
                for off, rows in _shifted_rows(ubuf[r0:r0 + hb + rc, ls], rc):
                    k = off - base
                    if 0 <= k < kc:
                        prod = rows * dchunk
                        part = prod[0:SUBLANES]
                        for s8 in range(SUBLANES, rc, SUBLANES):
                            part = part + prod[s8:s8 + SUBLANES]
                        dw_sc[k, :, ls] += part
                acc = jnp.zeros((rc, sub), F32)
                for off, rows in _shifted_rows(dwin, rc):
                    k = (kc - 1) - off
                    if 0 <= k < kc:
                        acc += w_ref[k:k + 1, ls] * rows
                v, s = val_ref[r0:r0 + rc, ls], sbuf[r0:r0 + rc, ls]
                dval_ref[r0:r0 + rc, ls] = (acc * s).astype(BF16)
                dgate_ref[r0:r0 + rc, ls] = (acc * v * s * (1.0 - s)).astype(BF16)

        @pl.when(i == ni - 1)
        def _():
            for k in range(kc):
                dw_ref[k:k + 1, :] = jnp.sum(dw_sc[k], axis=0, keepdims=True)

    return pl.pallas_call(
        body, name="conv_bwd", grid=(nl, ni),
        in_specs=[
            pl.BlockSpec((tm, lc), lambda l, i: (i, l)),
            pl.BlockSpec((hb, lc), lambda l, i: (jnp.minimum((i + 1) * ratio, last_halo), l)),
            pl.BlockSpec((tm, lc), lambda l, i: (i, l)),
            pl.BlockSpec((tm, lc), lambda l, i: (i, gate_off + l)),
            pl.BlockSpec((hb, lc), lambda l, i: (jnp.maximum(i * ratio - 1, 0), l)),
            pl.BlockSpec((hb, lc), lambda l, i: (jnp.maximum(i * ratio - 1, 0), gate_off + l)),
            pl.BlockSpec((kc, lc), lambda l, i: (0, l)),
        ],
        out_specs=[pl.BlockSpec((tm, lc), lambda l, i: (i, l)), pl.BlockSpec((tm, lc), lambda l, i: (i, l)),
                   pl.BlockSpec((kc, lc), lambda l, i: (0, l)), pl.BlockSpec((1, lc), lambda l, i: (0, l))],
        out_shape=[jax.ShapeDtypeStruct((T, E), BF16), jax.ShapeDtypeStruct((T, E), BF16),
                   jax.ShapeDtypeStruct((kc, E), F32), jax.ShapeDtypeStruct((1, E), F32)],
        scratch_shapes=[pltpu.VMEM((hb + tm, lc), F32), pltpu.VMEM((tm + hb, lc), F32),
                        pltpu.VMEM((tm, lc), F32), pltpu.VMEM((kc, SUBLANES, lc), F32)],
        compiler_params=_params(2),
    )(du1, du1, proj, proj, proj, proj, conv_w)


def _attn_fwd(q_all, kv, kr, H, tq, scale):
    T = q_all.shape[0]
    nq = T // tq
    pair = 2
    W = pair * LANES
    assert H % pair == 0
    hp_n = H // pair

    def body(qn_ref, qr_ref, kn_ref, kr_ref, v_ref, o_ref, lse_ref, *scratch):
        qi = pl.program_id(1)
        chains = [scratch[4 * a:4 * a + 4] for a in range(pair)]
        lanes = [slice(a * LANES, (a + 1) * LANES) for a in range(pair)]
        groups = [slice(c * LANES, (c + 1) * LANES) for c in range(tq // LANES)]

        def fold(x, op):
            r = x[:, groups[0]]
            for gsl in groups[1:]:
                r = op(r, x[:, gsl])
            return r

        for _, m_sc, l_sc, acc_sc in chains:
            m_sc[...] = jnp.full_like(m_sc, MASK_VALUE)
            l_sc[...] = jnp.zeros_like(l_sc)
            acc_sc[...] = jnp.zeros_like(acc_sc)

        def scores(j, masked):
            rows = pl.ds(pl.multiple_of(j * tq, tq), tq)
            krope = kr_ref[rows, :]
            for a, (s_sc, m_sc, _, _) in enumerate(chains):
                q = jnp.concatenate([qn_ref[:, lanes[a]], qr_ref[:, lanes[a]]], axis=1)
                k = jnp.concatenate([kn_ref[rows, lanes[a]], krope], axis=1)
                s = _dot(q, k, NT) * (scale * LOG2_E)
                if masked:
                    row = lax.broadcasted_iota(jnp.int32, s.shape, 0)
                    col = lax.broadcasted_iota(jnp.int32, s.shape, 1)
                    s = jnp.where(col <= row, s, MASK_VALUE)
                s_sc[j] = s
                m_sc[...] = jnp.maximum(m_sc[...], fold(s, jnp.maximum))

        def two_per_trip(fn, count):
            def two(p, carry):
                fn(2 * p)
                fn(2 * p + 1)
                return carry

            lax.fori_loop(0, count // 2, two, 0)

            @pl.when(count % 2 == 1)
            def _():
                fn(count - 1)

        two_per_trip(functools.partial(scores, masked=False), qi)
        scores(qi, True)
        for _, m_sc, _, _ in chains:
            m_sc[...] = jnp.broadcast_to(jnp.max(m_sc[...], axis=1, keepdims=True), m_sc.shape)

        def weigh(j):
            rows = pl.ds(pl.multiple_of(j * tq, tq), tq)
            for a, (s_sc, m_sc, l_sc, acc_sc) in enumerate(chains):
                s, m = s_sc[j], m_sc[...]
                p = jnp.concatenate([jnp.exp2(s[:, gsl] - m) for gsl in groups], axis=1)
                l_sc[...] += fold(p, jnp.add)
                acc_sc[...] += _dot(p, v_ref[rows, lanes[a]], NN)

        two_per_trip(weigh, qi + 1)
        for a, (_, m_sc, l_sc, acc_sc) in enumerate(chains):
            l = jnp.sum(l_sc[...], axis=1, keepdims=True)
            o_ref[:, lanes[a]] = (acc_sc[...] / l).astype(BF16)
            lse_ref[a] = m_sc[:, 0:1] * (1.0 / LOG2_E) + jnp.log(l)

    chain_scratch = [pltpu.VMEM((nq, tq, tq), F32), pltpu.VMEM((tq, LANES), F32), pltpu.VMEM((tq, LANES), F32),
                     pltpu.VMEM((tq, LANES), F32)]
    return pl.pallas_call(
        body, name="attn_fwd", grid=(hp_n, nq),
        in_specs=[pl.BlockSpec((tq, W), lambda hp, qi: (qi, hp)),
                  pl.BlockSpec((tq, W), lambda hp, qi: (qi, hp_n + hp)),
                  pl.BlockSpec((T, W), lambda hp, qi: (0, hp)),
                  pl.BlockSpec((T, LANES), lambda hp, qi: (0, 0)),
                  pl.BlockSpec((T, W), lambda hp, qi: (0, hp_n + hp))],
        out_specs=[pl.BlockSpec((tq, W), lambda hp, qi: (qi, hp)),
                   pl.BlockSpec((pair, tq, 1), lambda hp, qi: (hp, qi, 0))],
        out_shape=[jax.ShapeDtypeStruct((T, H * LANES), BF16), jax.ShapeDtypeStruct((H, T, 1), F32)],
        scratch_shapes=chain_scratch * pair, compiler_params=_params(2),
    )(q_all, q_all, kv, kr, kv)


def _attn_bwd(q_all, kv, kr, do, o, lse, cos, sin, H, tq, scale):
    T = q_all.shape[0]
    nq = T // tq
    HV = H * LANES
    pair = 2
    tk2 = pair * tq
    ng = T // tk2
    assert ng * tk2 == T and pair == 2

    def body(qn_ref, qr_ref, kn_ref, kr_ref, v_ref, do_ref, o_ref, lse_ref, cos_ref, sin_ref,
             dqn_ref, dqr_ref, dkn_ref, dkr_ref, dv_ref, dq_sc, dk_sc, dv_sc):
        g = pl.program_id(1)

        @pl.when(g == 0)
        def _():
            dq_sc[...] = jnp.zeros_like(dq_sc)

        key_rows = [slice(c * tq, (c + 1) * tq) for c in range(pair)]

        def block(qi, modes):
            rows = pl.ds(pl.multiple_of(qi * tq, tq), tq)
            q = jnp.concatenate([qn_ref[rows, :], qr_ref[rows, :]], axis=1)
            dov = do_ref[rows, :]
            delta = jnp.sum(dov.astype(F32) * o_ref[rows, :].astype(F32), axis=1, keepdims=True)
            lse_q = lse_ref[0, rows, :]
            dq, dkv = None, []
            for kr_, masked in zip(key_rows, modes):
                if masked is None:
                    dkv.append(None)
                    continue
                k = jnp.concatenate([kn_ref[kr_, :], kr_ref[kr_, :]], axis=1)
                s = _dot(q, k, NT) * scale
                if masked:
                    row = lax.broadcasted_iota(jnp.int32, s.shape, 0)
                    col = lax.broadcasted_iota(jnp.int32, s.shape, 1)
                    s = jnp.where(col <= row, s, MASK_VALUE)
                p = jnp.exp(s - lse_q)
                dv = _dot(p, dov, TN)
                dp = _dot(dov, v_ref[kr_, :], NT)
                ds = (p * (dp - delta) * scale).astype(BF16)
                dkv.append((_dot(ds, q, TN), dv))
                part = _dot(ds, k, NN)
                dq = part if dq is None else dq + part
            return rows, dq, dkv

        rows_a, dq_a, (kv_a0, _) = block(pair * g, (True, None))
        rows_b, dq_b, (kv_b0, kv_b1) = block(pair * g + 1, (False, True))
        dk_sc[key_rows[0], :] = kv_a0[0] + kv_b0[0]
        dv_sc[key_rows[0], :] = kv_a0[1] + kv_b0[1]
        dk_sc[key_rows[1], :] = kv_b1[0]
        dv_sc[key_rows[1], :] = kv_b1[1]
        dq_sc[rows_a, :] += dq_a
        dq_sc[rows_b, :] += dq_b

        def below(trip, carry):
            for qi in (pair * g + pair + 2 * trip, pair * g + pair + 2 * trip + 1):
                rows, dq, dkv = block(qi, (False, False))
                for kr_, (dk, dv) in zip(key_rows, dkv):
                    dk_sc[kr_, :] += dk
                    dv_sc[kr_, :] += dv
                dq_sc[rows, :] += dq
            return carry

        lax.fori_loop(0, (nq - pair * g - pair) // 2, below, 0)
        dkn_ref[...] = dk_sc[:, :LANES].astype(BF16)
        dkr_ref[...] = dk_sc[:, LANES:]
        dv_ref[...] = dv_sc[...].astype(BF16)

        @pl.when(g == ng - 1)
        def _():
            dqn_ref[...] = dq_sc[:, :LANES].astype(BF16)
            dqr_ref[...] = _rope(dq_sc[:, LANES:], cos_ref[...], sin_ref[...], transpose=True).astype(BF16)

    whole = lambda col: pl.BlockSpec((T, LANES), col)
    tile = lambda col: pl.BlockSpec((tk2, LANES), col)
    return pl.pallas_call(
        body, name="attn_bwd", grid=(H, ng),
        in_specs=[whole(lambda h, g: (0, h)), whole(lambda h, g: (0, H + h)),
                  tile(lambda h, g: (g, h)), tile(lambda h, g: (g, 0)), tile(lambda h, g: (g, H + h)),
                  whole(lambda h, g: (0, h)), whole(lambda h, g: (0, h)),
                  pl.BlockSpec((1, T, 1), lambda h, g: (h, 0, 0)),
                  whole(lambda h, g: (0, 0)), whole(lambda h, g: (0, 0))],
        out_specs=[whole(lambda h, g: (0, h)), whole(lambda h, g: (0, h)),
                   tile(lambda h, g: (g, h)), tile(lambda h, g: (g, h)), tile(lambda h, g: (g, h))],
        out_shape=[jax.ShapeDtypeStruct((T, HV), BF16), jax.ShapeDtypeStruct((T, HV), BF16),
                   jax.ShapeDtypeStruct((T, HV), BF16), jax.ShapeDtypeStruct((T, HV), F32),
                   jax.ShapeDtypeStruct((T, HV), BF16)],
        scratch_shapes=[pltpu.VMEM((T, 2 * LANES), F32), pltpu.VMEM((tk2, 2 * LANES), F32),
                        pltpu.VMEM((tk2, LANES), F32)],
        compiler_params=_params(2),
    )(q_all, q_all, kv, kr, kv, do, o, lse, cos, sin)


def _key_rope_bwd(dkr_heads, cos, sin, H, tm):
    T, HV = dkr_heads.shape

    def body(dkr_ref, cos_ref, sin_ref, dk_ref):
        dk = dkr_ref[...]
        tot = dk[:, 0:LANES]
        for h in range(1, H):
            tot = tot + dk[:, h * LANES:(h + 1) * LANES]
        dk_ref[...] = _rope(tot, cos_ref[...], sin_ref[...], transpose=True)

    return pl.pallas_call(
        body, name="key_rope_bwd", grid=(T // tm,),
        in_specs=[pl.BlockSpec((tm, HV), lambda i: (i, 0)),
                  pl.BlockSpec((tm, LANES), lambda i: (i, 0)), pl.BlockSpec((tm, LANES), lambda i: (i, 0))],
        out_specs=pl.BlockSpec((tm, LANES), lambda i: (i, 0)),
        out_shape=jax.ShapeDtypeStruct((T, LANES), F32), compiler_params=_params(1),
    )(dkr_heads, cos, sin)


def _place():
    x, y, c = lax.axis_index("x"), lax.axis_index("y"), lax.axis_index("c")
    chips = [(1 - x, y), (x, 1 - y), (1 - x, 1 - y)]
    return x, y, c, chips


def _all_gather_chips(arrs):
    n = len(arrs)
    halves = [a.shape[0] // 2 for a in arrs]
    assert all(h * 2 == a.shape[0] and h % BF16_ROWS == 0 for h, a in zip(halves, arrs))

    def body(*refs):
        w_refs, out_refs = refs[:n], refs[n:2 * n]
        send_sems, recv_sems, local_sems = refs[2 * n:]
        x, y, c, chips = _place()
        sibling = (x, y, 1 - c)
        waits = []
        for w, (w_ref, out_ref, half) in enumerate(zip(w_refs, out_refs, halves)):
            def region(px, py, pc, out_ref=out_ref, half=half):
                return out_ref.at[2 * px + py, pl.ds(pc * half, half), :]

            def copy(k, block, to, src=None, w=w, region=region):
                return pltpu.make_async_remote_copy(
                    src_ref=region(*block) if src is None else src, dst_ref=region(*block),
                    send_sem=send_sems.at[6 * w + k], recv_sem=recv_sems.at[6 * w + k],
                    device_id=to, device_id_type=MESH)

            mine = pltpu.make_async_copy(w_ref, out_ref.at[2 * x + y], local_sems.at[w])
            mine.start()
            my_half = w_ref.at[pl.ds(c * half, half), :]
            first = [copy(j, (x, y, c), (*chip, c), src=my_half) for j, chip in enumerate(chips)]
            for cp in first:
                cp.start()
            waits.append((copy, mine, first))
        for copy, mine, first in waits:
            passed = [copy(3 + j, (*chip, c), sibling) for j, chip in enumerate(chips)]
            for j, chip in enumerate(chips):
                copy(j, (*chip, c), (x, y, c)).wait_recv()
                passed[j].start()
            for j, chip in enumerate(chips):
                copy(3 + j, (*chip, 1 - c), (x, y, c)).wait_recv()
            for cp in first + passed:
                cp.wait_send()
            mine.wait()

    return pl.pallas_call(
        body, name="gather_weights", in_specs=[ANY] * n, out_specs=[ANY] * n,
        out_shape=[jax.ShapeDtypeStruct((N_CHIPS,) + a.shape, a.dtype) for a in arrs],
        scratch_shapes=[pltpu.SemaphoreType.DMA((6 * n,)), pltpu.SemaphoreType.DMA((6 * n,)),
                        pltpu.SemaphoreType.DMA((n,))],
    )(*arrs)


def _swap_cores(name, parts):
    n = len(parts)

    def body(*refs):
        p_refs, r_refs = refs[:n], refs[n:2 * n]
        send_sems, recv_sems = refs[2 * n:]
        x, y, c, _ = _place()
        copies = [pltpu.make_async_remote_copy(
            src_ref=p_refs[w], dst_ref=r_refs[w], send_sem=send_sems.at[w], recv_sem=recv_sems.at[w],
            device_id=(x, y, 1 - c), device_id_type=MESH) for w in range(n)]
        for cp in copies:
            cp.start()
        for cp in copies:
            cp.wait()

    return pl.pallas_call(
        body, name=name, in_specs=[ANY] * n, out_specs=[ANY] * n,
        out_shape=[jax.ShapeDtypeStruct(p.shape, p.dtype) for p in parts],
        scratch_shapes=[pltpu.SemaphoreType.DMA((n,)), pltpu.SemaphoreType.DMA((n,))],
    )(*parts)


HBM = pl.BlockSpec(memory_space=pltpu.HBM)
SEM = pl.BlockSpec(memory_space=pltpu.SEMAPHORE)
EFFECT = pltpu.SideEffectType.DATAFLOW_SIDE_EFFECTING


def _push_copies(a_refs, l_refs, send_sems, recv_sems, by_target):
    x, y, c, chips = _place()
    me = 2 * x + y

    def part(a_ref, q):
        if by_target == 'cols':
            n = a_ref.shape[-1] // N_CHIPS
            return a_ref.at[(slice(None),) * (len(a_ref.shape) - 1) + (pl.ds(pl.multiple_of(q * n, LANES), n),)]
        return a_ref.at[q] if by_target else a_ref

    out = []
    for w, (a_ref, l_ref) in enumerate(zip(a_refs, l_refs)):
        for j, (px, py) in enumerate(chips):
            peer = 2 * px + py
            out.append((
                pltpu.make_async_remote_copy(
                    src_ref=part(a_ref, peer), dst_ref=l_ref.at[me],
                    send_sem=send_sems.at[3 * w + j], recv_sem=recv_sems.at[3 * w + j],
                    device_id=(px, py, c), device_id_type=MESH),
                pltpu.make_async_remote_copy(
                    src_ref=part(a_ref, me), dst_ref=l_ref.at[peer],
                    send_sem=send_sems.at[3 * w + j], recv_sem=recv_sems.at[3 * w + j],
                    device_id=(px, py, c), device_id_type=MESH)))
    return out


def _landing_shape(a, by_target):
    if by_target == 'cols':
        return (N_CHIPS,) + a.shape[:-1] + (a.shape[-1] // N_CHIPS,)
    return (N_CHIPS,) + (a.shape[1:] if by_target else a.shape)


def _push_start(name, arrs, by_target):
    n = len(arrs)
    lands = [lax.empty(_landing_shape(a, by_target), a.dtype) for a in arrs]

    def body(*refs):
        a_refs, l_refs = refs[:n], refs[n:2 * n]
        send_sems, recv_sems = refs[2 * n], refs[2 * n + 1]
        token = refs[-1]
        for send, _ in _push_copies(a_refs, l_refs, send_sems, recv_sems, by_target):
            send.start()
        token[...] = jnp.zeros_like(token)

    res = pl.pallas_call(
        body, name=name,
        out_shape=(pltpu.SemaphoreType.DMA((3 * n,)), pltpu.SemaphoreType.DMA((3 * n,)),
                   *[pltpu.HBM(a.shape, a.dtype) for a in arrs], *[pltpu.HBM(l.shape, l.dtype) for l in lands],
                   jax.ShapeDtypeStruct((8, LANES), F32)),
        in_specs=[HBM] * (2 * n), out_specs=(SEM, SEM, *[HBM] * (2 * n), pl.BlockSpec(memory_space=pltpu.VMEM)),
        input_output_aliases={i: 2 + i for i in range(2 * n)},
        compiler_params=pltpu.CompilerParams(has_side_effects=EFFECT),
    )(*[pltpu.with_memory_space_constraint(a, pltpu.HBM) for a in list(arrs) + lands])
    return res[0], res[1], list(res[2:2 + n]), list(res[2 + n:2 + 2 * n]), res[-1]


def _push_wait(name, send_sems, recv_sems, arrs, lands, after, by_target):
    n = len(arrs)

    def body(*refs):
        a_refs, l_refs = refs[:n], refs[n:2 * n]
        s_sems, r_sems = refs[2 * n], refs[2 * n + 1]
        for send, recv in _push_copies(a_refs, l_refs, s_sems, r_sems, by_target):
            send.wait_send()
            recv.wait_recv()

    res = pl.pallas_call(
        body, name=name,
        out_shape=[pltpu.HBM(a.shape, a.dtype) for a in list(arrs) + list(lands)],
        in_specs=[HBM] * (2 * n) + [SEM, SEM] + [ANY] * len(after), out_specs=[HBM] * (2 * n),
        input_output_aliases={i: i for i in range(2 * n)},
        compiler_params=pltpu.CompilerParams(has_side_effects=EFFECT),
    )(*arrs, *lands, send_sems, recv_sems, *after)
    return list(res[:n]), list(res[n:])


def _all_reduce_small(part):
    def body(p_ref, out_ref, sib_buf, chip_buf, send_sems, recv_sems):
        x, y, c, chips = _place()
        me = 2 * x + y
        swap = pltpu.make_async_remote_copy(
            src_ref=p_ref, dst_ref=sib_buf, send_sem=send_sems.at[0], recv_sem=recv_sems.at[0],
            device_id=(x, y, 1 - c), device_id_type=MESH)
        swap.start()
        swap.wait()
        chip_buf[me] = p_ref[...] + sib_buf[...]
        copies = []
        for j, (px, py) in enumerate(chips):
            cp = pltpu.make_async_remote_copy(
                src_ref=chip_buf.at[me], dst_ref=chip_buf.at[me], send_sem=send_sems.at[1 + j],
                recv_sem=recv_sems.at[1 + j], device_id=(px, py, c), device_id_type=MESH)
            cp.start()
            copies.append(cp)
        for j, (px, py) in enumerate(chips):
            pltpu.make_async_remote_copy(
                src_ref=chip_buf.at[me], dst_ref=chip_buf.at[2 * px + py], send_sem=send_sems.at[1 + j],
                recv_sem=recv_sems.at[1 + j], device_id=(px, py, c), device_id_type=MESH).wait_recv()
        for cp in copies:
            cp.wait_send()
        tot = chip_buf[0]
        for q in range(1, N_CHIPS):
            tot = tot + chip_buf[q]
        out_ref[...] = tot

    vm = pl.BlockSpec(memory_space=pltpu.VMEM)
    return pl.pallas_call(
        body, name="all_reduce_small", in_specs=[vm], out_specs=vm,
        out_shape=jax.ShapeDtypeStruct(part.shape, F32),
        scratch_shapes=[pltpu.VMEM(part.shape, F32), pltpu.VMEM((N_CHIPS,) + part.shape, F32),
                        pltpu.SemaphoreType.DMA((N_CHIPS,)), pltpu.SemaphoreType.DMA((N_CHIPS,))],
    )(part)


def _row_tiles(shape):
    ax = next(d for d, s in enumerate(shape) if s > 1)
    tr = _gcd(ADAM_ROWS, shape[ax])
    block = tuple(tr if d == ax else s for d, s in enumerate(shape))
    return shape[ax] // tr, block, lambda i: tuple(i if d == ax else 0 for d in range(len(shape)))


def _sum_chips(name, landed, sent, chip, by_target):
    shape = landed.shape[1:]
    steps, block, index = _row_tiles(shape)
    if by_target == 'cols':
        own_spec = pl.BlockSpec(block, lambda i, c: index(i)[:-1] + (c[0],))
    else:
        own_spec = pl.BlockSpec((1,) + block, lambda i, c: (c[0],) + index(i))

    def body(chip_ref, l_ref, s_ref, o_ref):
        own = (s_ref[...] if by_target == 'cols' else s_ref[0]).astype(F32)
        tot = None
        for q in range(N_CHIPS):
            term = jnp.where(chip_ref[0] == q, own, l_ref[q].astype(F32))
            tot = term if tot is None else tot + term
        o_ref[...] = tot

    return pl.pallas_call(
        body, name=name,
        grid_spec=pltpu.PrefetchScalarGridSpec(
            num_scalar_prefetch=1, grid=(steps,),
            in_specs=[pl.BlockSpec((N_CHIPS,) + block, lambda i, c: (0,) + index(i)), own_spec],
            out_specs=pl.BlockSpec(block, lambda i, c: index(i))),
        out_shape=jax.ShapeDtypeStruct(shape, F32), compiler_params=_params(1),
    )(chip, landed, sent)


def _adamw_math(g, w, m, v):
    mn = ADAM_B1 * m + (1.0 - ADAM_B1) * g
    vn = ADAM_B2 * v + (1.0 - ADAM_B2) * jnp.square(g)
    m_hat = mn / (1.0 - ADAM_B1 ** ADAM_STEP)
    v_hat = vn / (1.0 - ADAM_B2 ** ADAM_STEP)
    return -ADAM_LR * (m_hat / (jnp.sqrt(v_hat) + ADAM_EPS) + ADAM_WD * w), mn, vn


def _adamw(name, g_parts, w, m, v):
    steps, block, index = _row_tiles(w.shape)
    n = len(g_parts)

    def body(*refs):
        g = refs[0][...]
        for r in refs[1:n]:
            g = g + r[...]
        w_ref, m_ref, v_ref, go_ref, d_ref, mo_ref, vo_ref = refs[n:]
        go_ref[...] = g
        d_ref[...], mo_ref[...], vo_ref[...] = _adamw_math(g, w_ref[...], m_ref[...], v_ref[...])

    spec = pl.BlockSpec(block, index)
    return pl.pallas_call(
        body, name=name, grid=(steps,), in_specs=[spec] * (n + 3), out_specs=[spec] * 4,
        out_shape=[jax.ShapeDtypeStruct(w.shape, F32)] * 4, compiler_params=_params(1),
    )(*g_parts, w, m, v)


def _adamw_vectors(items):
    n = len(items)

    def body(*refs):
        ins, outs = refs[:4 * n], refs[4 * n:]
        for k in range(n):
            g, w, m, v = (r[...] for r in ins[4 * k:4 * k + 4])
            outs[3 * k][...], outs[3 * k + 1][...], outs[3 * k + 2][...] = _adamw_math(g, w, m, v)

    vm = pl.BlockSpec(memory_space=pltpu.VMEM)
    res = pl.pallas_call(
        body, name="adamw_vectors", in_specs=[vm] * (4 * n), out_specs=[vm] * (3 * n),
        out_shape=[jax.ShapeDtypeStruct(it[1].shape, F32) for it in items for _ in range(3)],
    )(*[a for it in items for a in it])
    return [res[3 * k:3 * k + 3] for k in range(n)]


def _pack_rows(flat, dtype, multiple):
    n = flat.shape[0]
    total = -(-n // multiple) * multiple
    return jnp.pad(flat, (0, total - n)).astype(dtype).reshape(total // LANES, LANES)


def kernel(x, positions, ln_g, ln_b, a_w_in, a_b_in, a_conv_w, a_conv_b, a_norm_g, a_norm_b, a_w_out, a_b_out, kv_w_down, kv_norm_g, kv_w_uk, kv_w_uv, b_w_in, b_q_norm_g, b_w_uq, b_w_out, loss_target, m_ln_g, m_ln_b, m_a_w_in, m_a_b_in, m_a_conv_w, m_a_conv_b, m_a_norm_g, m_a_norm_b, m_a_w_out, m_a_b_out, m_kv_w_down, m_kv_norm_g, m_kv_w_uk, m_kv_w_uv, m_b_w_in, m_b_q_norm_g, m_b_w_uq, m_b_w_out, v_ln_g, v_ln_b, v_a_w_in, v_a_b_in, v_a_conv_w, v_a_conv_b, v_a_norm_g, v_a_norm_b, v_a_w_out, v_a_b_out, v_kv_w_down, v_kv_norm_g, v_kv_w_uk, v_kv_w_uv, v_b_w_in, v_b_q_norm_g, v_b_w_uq, v_b_w_out):
    T, D = x.shape[1], x.shape[2]
    E = N_CHIPS * a_w_out.shape[1]
    KC = a_conv_w.shape[1]
    RKV = kv_norm_g.shape[0]
    H, DN = kv_w_uk.shape[1], kv_w_uk.shape[2]
    RQ = b_q_norm_g.shape[1]
    HV = N_CHIPS * b_w_out.shape[1]
    assert DN == LANES and kv_w_uv.shape[2] == LANES and HV == H * LANES
    assert kv_w_down.shape[1] == RKV + ROPE_DIM and b_w_uq.shape[3] == DN + ROPE_DIM
    assert ln_g.shape[0] == 2 and a_w_in.shape[0] == 1 and b_w_in.shape[0] == 1
    alpha = (2.0 * ln_g.shape[0]) ** 0.25
    scale = 1.0 / math.sqrt(DN + ROPE_DIM)
    WK = -(-(RKV + LANES) // 256) * 256
    assert WK % RQ == 0
    Z_OFF = WK + RQ
    tmw, tq = min(TM_WIDE, T), min(TQ, T)
    t512, t1024 = _fit(512, T), _fit(1024, T)
    xs = x[0]
    tgt = loss_target[0]
    px, py = lax.axis_index("x"), lax.axis_index("y")
    chip = 2 * px + py

    mats = [a_w_out[0], kv_w_down, kv_w_uk, kv_w_uv, b_w_in[0], b_w_uq[0], b_w_out[0]]
    vecs = [a_b_in[0], a_conv_w[0], a_conv_b[0], a_norm_g[0], a_norm_b[0], a_b_out[0]]
    vec_bits = jnp.concatenate([lax.bitcast_convert_type(w.reshape(-1), BF16).reshape(-1) for w in vecs])
    rest = [w.astype(BF16) for w in mats]
    g_win, gathered = _all_gather_chips(
        [a_w_in[0].astype(BF16), _pack_rows(vec_bits, BF16, 2 * BF16_ROWS * LANES)])
    gathered = gathered.reshape(N_CHIPS, -1)
    gathered, rest = lax.optimization_barrier((gathered, rest))
    rest_sems = _push_start("gather_rest_start", rest, by_target=False)
    off = 0
    fvec = []
    for w in vecs:
        bits = gathered[:, off:off + 2 * w.size].reshape((N_CHIPS,) + w.shape + (2,))
        fvec.append(lax.bitcast_convert_type(bits, F32))
        off += 2 * w.size
    cols = lambda g: jnp.moveaxis(g, 0, -2).reshape(g.shape[1:-1] + (N_CHIPS * g.shape[-1],))
    b_in = cols(fvec[0][:, None, :])
    conv_w = cols(fvec[1])
    conv_b, norm_g, norm_b, b_out = (cols(f[:, None, :]) for f in fvec[2:])
    row = lambda a: a.reshape(1, -1)
    g0, b0, g1, b1 = row(ln_g[0]), row(ln_b[0]), row(ln_g[1]), row(ln_b[1])
    kv_g, q_g = row(kv_norm_g), row(b_q_norm_g[0])
    plain = lambda acc, ins, i, j: [acc]

    b_in = b_in + rest_sems[4][0, 0]
    (proj,) = _row_mm("a_in", [((xs,), None)], g_win, nt=False, tm=t1024, tn=3 * E // N_CHIPS, tk=D,
                      outs=[((T, 3 * E), F32, 'tile')], epi=lambda acc, ins, i, j: [acc + ins[0]],
                      epi_ins=[(b_in, 'col')])
    u1, u4 = _conv_fwd(proj, conv_w, conv_b, norm_g, norm_b, E, tmw)

    rest, landed = _push_wait("gather_rest_wait", *rest_sems[:4], after=[u4], by_target=False)
    g_wout, g_wd, g_uk, g_uv, g_wbin, g_wuq, g_wbout = [
        lax.dynamic_update_slice(l, w[None], (chip,) + (0,) * w.ndim) for w, l in zip(rest, landed)]
    w_out = g_wout.reshape(E, D)
    wd = g_wd.reshape(D, RKV + ROPE_DIM)
    zpad = jnp.zeros((D, ROPE_HALF), BF16)
    wd_p = jnp.concatenate(
        [wd[:, :RKV], wd[:, RKV:RKV + ROPE_HALF], zpad, wd[:, RKV + ROPE_HALF:], zpad,
         jnp.zeros((D, WK - RKV - LANES), BF16)], axis=1)
    w_bin = cols(g_wbin)
    w_z = w_bin[:, RQ:]
    wb_small = jnp.concatenate([wd_p, w_bin[:, :RQ]], axis=1)
    wb_all = jnp.concatenate([wd_p, w_bin], axis=1)
    w_kv = jnp.concatenate([g_uk.reshape(RKV, HV), g_uv.reshape(RKV, HV)], axis=1)
    wuq = g_wuq.reshape(RQ, H, DN + ROPE_DIM)
    zq = jnp.zeros((RQ, H, ROPE_HALF), BF16)
    w_qr = jnp.concatenate([wuq[:, :, DN:DN + ROPE_HALF], zq, wuq[:, :, DN + ROPE_HALF:], zq], axis=2)
    w_q = jnp.concatenate([wuq[:, :, :DN].reshape(RQ, HV), w_qr.reshape(RQ, HV)], axis=1)
    w_bout = g_wbout.reshape(HV, D)

    freqs = ROPE_THETA ** (-jnp.arange(0, ROPE_DIM, 2, dtype=F32) / ROPE_DIM)
    ang = positions[0].astype(F32)[:, None] * freqs
    cs, sn = jnp.cos(ang), jnp.sin(ang)
    ones, zeros = jnp.ones_like(cs), jnp.zeros_like(cs)
    cos_t = jnp.concatenate([cs, ones, cs, ones], axis=1)
    sin_t = jnp.concatenate([-sn, zeros, sn, zeros], axis=1)

    def ln_epi(acc, ins, i, j):
        bias, res, g, b = ins
        xhat, rstd = _ln_stats(alpha * res + acc + bias)
        h = xhat * g + b
        return [h, h, xhat, rstd]

    h1, h1b, xhat1, rstd1 = _row_mm(
        "a_out", [((u4,), None)], w_out, nt=False, tm=t512, tn=D, tk=_fit(2048, E),
        outs=[((T, D), F32, 'tile'), ((T, D), BF16, 'tile'), ((T, D), F32, 'tile'), ((T, 1), F32, 'row')],
        epi=ln_epi, epi_ins=[(b_out, 'col'), (xs, 'tile'), (g0, 'col'), (b0, 'col')])

    tkb = _fit(512, _gcd(WK, RQ, HV))
    def latents_epi(acc, ins, i, j):
        kg, qg, cos_, sin_ = ins
        return [acc, _rms_fwd(acc[:, :RKV], kg), _rope(acc[:, RKV:RKV + LANES], cos_, sin_),
                _rms_fwd(acc[:, WK:WK + RQ], qg)]

    whole_row = lambda a: (a, pl.BlockSpec(a.shape, lambda i, j, k: (0, 0)))
    pb, c_lat, kr, cqn = _row_mm(
        "b_in", [((h1b,), None)], wb_small, nt=False, tm=t512, tn=Z_OFF, tk=_fit(1024, D),
        outs=[((T, Z_OFF), F32, 'tile'), ((T, RKV), BF16, 'row'), ((T, LANES), BF16, 'row'),
              ((T, RQ), BF16, 'row')],
        epi=latents_epi, epi_ins=[whole_row(kv_g), whole_row(q_g), (cos_t, 'row'), (sin_t, 'row')])
    (zb,) = _row_mm("b_in_gate", [((h1b,), None)], w_z, nt=False, tm=t1024, tn=_fit(2048, HV),
                    tk=_fit(1024, D), outs=[((T, HV), BF16, 'tile')], epi=plain)
    (kv,) = _row_mm("kv_up", [((c_lat,), None)], w_kv, nt=False, tm=t1024, tn=_fit(2048, HV),
                    tk=_fit(1024, RKV), outs=[((T, 2 * HV), BF16, 'tile')], epi=plain)
    tnq = _fit(2048, HV)
    half_q = HV // tnq

    def q_epi(acc, ins, i, j):
        return [jnp.where(j >= half_q, _rope(acc, ins[0], ins[1]), acc)]

    (q_all,) = _row_mm("q_up", [((cqn,), None)], w_q, nt=False, tm=t1024, tn=tnq, tk=_fit(1024, RQ),
                       outs=[((T, 2 * HV), BF16, 'tile')], epi=q_epi,
                       epi_ins=[(cos_t, 'row'), (sin_t, 'row')])
    o, lse = _attn_fwd(q_all, kv, kr, H, tq, scale)

    def loss_epi(acc, ins, i, j):
        res, g, b, target = ins
        xhat, rstd = _ln_stats(alpha * res + acc)
        diff = xhat * g + b - target
        dr, dg, db = _ln_bwd(diff / D, xhat, rstd, g)
        return [dr, 0.5 * jnp.sum(diff * diff, keepdims=True) / D, dg, db]

    dr1, loss_part, dg1, db1 = _row_mm(
        "b_out", [((o, zb), _gate)], w_bout, nt=False, tm=t512, tn=D, tk=_fit(2048, HV),
        outs=[((T, D), F32, 'tile'), ((1, 1), F32, 'acc'), ((1, D), F32, 'acc'), ((1, D), F32, 'acc')],
        epi=loss_epi, epi_ins=[(h1, 'tile'), (g1, 'col'), (b1, 'col'), (tgt, 'tile')])

    def gate_bwd_epi(acc, ins, i, j):
        gate, gate_grad = _silu_and_grad(ins[1].astype(F32))
        return [acc * gate, acc * ins[0].astype(F32) * gate_grad]

    do, dz = _row_mm(
        "b_out_bwd", [((dr1,), None)], w_bout, nt=True, tm=t512, tn=_fit(2048, HV), tk=_fit(1024, D),
        outs=[((T, HV), BF16, 'tile'), ((T, HV), BF16, 'tile')], epi=gate_bwd_epi,
        epi_ins=[(o, 'tile'), (zb, 'tile')])
    gw_bout = _tn_mm("dw_b_out", (o, zb), _gate, [((dr1,), None)], tn=_fit(1024, D), tk=t512, out_dtype=BF16)
    dqn, dqr_pre, dkn, dkr_h, dv = _attn_bwd(q_all, kv, kr, do, o, lse, cos_t, sin_t, H, tq, scale)
    dkr_pre = _key_rope_bwd(dkr_h, cos_t, sin_t, H, tmw)

    def cq_bwd_epi(acc, ins, i, j):
        dx, dg = _rms_bwd(acc, ins[0], ins[1])
        return [dx, dg]

    dcq, dqg = _row_mm(
        "q_up_bwd", [((dqn,), None), ((dqr_pre,), None)], w_q, nt=True, tm=t1024, tn=RQ, tk=_fit(2048, HV),
        outs=[((T, RQ), BF16, 'tile'), ((1, RQ), F32, 'acc')], epi=cq_bwd_epi,
        epi_ins=[(pb, pl.BlockSpec((t1024, RQ), lambda i, j, k: (i, WK // RQ))), (q_g, 'col')])
    gw_q = _tn_mm("dw_q_up", (cqn,), None, [((dqn,), None), ((dqr_pre,), None)],
                  tn=_fit(2048, HV), tk=t1024, out_dtype=BF16)

    def ckv_bwd_epi(acc, ins, i, j):
        blk, dkr_t, g = ins
        dx, dg = _rms_bwd(acc, blk[:, :RKV], g)
        parts = [dx, dkr_t]
        if WK > RKV + LANES:
            parts.append(jnp.zeros((dx.shape[0], WK - RKV - LANES), F32))
        return [jnp.concatenate(parts, axis=1), dg]

    dckv, dkvg = _row_mm(
        "kv_up_bwd", [((dkn,), None), ((dv,), None)], w_kv, nt=True, tm=t1024, tn=RKV, tk=_fit(2048, HV),
        outs=[((T, WK), BF16, pl.BlockSpec((t1024, WK), lambda i, j, k: (i, 0))), ((1, RKV), F32, 'acc')],
        epi=ckv_bwd_epi,
        epi_ins=[(pb, pl.BlockSpec((t1024, WK), lambda i, j, k: (i, 0))), (dkr_pre, 'row'), (kv_g, 'col')])
    gw_kv = _tn_mm("dw_kv_up", (c_lat,), None, [((dkn,), None), ((dv,), None)],
                   tn=_fit(2048, HV), tk=t1024, out_dtype=BF16)

    def ln1_bwd_epi(acc, ins, i, j):
        dr_up, xhat, rstd, g = ins
        dr, dg, db = _ln_bwd(alpha * dr_up + acc, xhat, rstd, g)
        return [dr, dg, db]

    dp_segs = [((dckv,), None), ((dcq,), None), ((dz,), None)]
    gw_lat = _tn_mm("dw_b_in", (h1b,), None, dp_segs[:2], tn=tkb, tk=t1024, out_dtype=BF16)
    gw_z = _tn_mm("dw_b_in_gate", (h1b,), None, dp_segs[2:], tn=_fit(2048, HV), tk=t1024, out_dtype=BF16)

    shard_cols = lambda g: jnp.moveaxis(g.reshape(g.shape[0], N_CHIPS, -1), 1, 0)
    shard_rows = lambda g: g.reshape(N_CHIPS, g.shape[0] // N_CHIPS, g.shape[1])
    gq = gw_q.reshape(RQ, 2, H, LANES)
    g_uq = jnp.concatenate(
        [gq[:, 0], gq[:, 1, :, :ROPE_HALF], gq[:, 1, :, 2 * ROPE_HALF:3 * ROPE_HALF]], axis=2)
    g_wd_full = jnp.concatenate(
        [gw_lat[:, :RKV], gw_lat[:, RKV:RKV + ROPE_HALF],
         gw_lat[:, RKV + 2 * ROPE_HALF:RKV + 3 * ROPE_HALF]], axis=1)
    late_names = ["kv_w_down", "kv_w_uk", "kv_w_uv", "b_w_in", "b_w_uq", "b_w_out"]
    late_w = [kv_w_down, kv_w_uk, kv_w_uv, b_w_in, b_w_uq, b_w_out]
    gw_bin = jnp.concatenate([gw_lat[:, WK:], gw_z], axis=1)
    chip_major = [g_wd_full, gw_kv[:, :HV], gw_kv[:, HV:], shard_cols(gw_bin), g_uq, gw_bout]
    late_grads = [g.reshape((N_CHIPS,) + w.shape) for g, w in zip(chip_major, late_w)]
    late_sems = _push_start("scatter_late_start", late_grads, by_target=True)

    dr0, dg0, db0 = _row_mm(
        "b_in_bwd", dp_segs, wb_all, nt=True, tm=t1024, tn=D, tk=tkb,
        outs=[((T, D), F32, 'tile'), ((1, D), F32, 'acc'), ((1, D), F32, 'acc')], epi=ln1_bwd_epi,
        epi_ins=[(dr1, 'tile'), (xhat1, 'tile'), (rstd1, 'row'), (g0 + late_sems[4][0, 0], 'col')])

    def conv_branch_bwd_epi(acc, ins, i, j):
        u1_t, z, g, b = ins
        xhat, rstd = _ln_stats(u1_t)
        u2 = xhat * g + b
        gate, gate_grad = _silu_and_grad(z)
        act, act_grad = _silu_and_grad(u2)
        dz_a = acc * act * gate_grad
        du1, dg, db = _ln_bwd(acc * gate * act_grad, xhat, rstd, g)
        return [du1, dz_a, dg, db]

    du1, dz_a, dng, dnb = _row_mm(
        "a_out_bwd", [((dr0,), None)], w_out, nt=True, tm=tmw, tn=E, tk=_fit(1024, D),
        outs=[((T, E), F32, 'tile'), ((T, E), BF16, 'tile'), ((1, E), F32, 'acc'), ((1, E), F32, 'acc')],
        epi=conv_branch_bwd_epi,
        epi_ins=[(u1, 'tile'), (proj, pl.BlockSpec((tmw, E), lambda i, j, k: (i, 2))), (norm_g, 'col'),
                 (norm_b, 'col')])
    gw_out, dbo = _tn_mm("dw_a_out", (u4,), None, [((dr0,), None)], tn=_fit(1024, D), tk=t1024,
                         out_dtype=BF16, colsum=True)
    mid_sems = _push_start("scatter_mid_start", [gw_out.reshape((N_CHIPS,) + a_w_out.shape)], by_target=True)
    dval, dgate, dcw, dcb = _conv_bwd(du1, proj, conv_w + mid_sems[4][0, 0], E, tmw)
    dproj_segs = [((dval,), None), ((dgate,), None), ((dz_a,), None)]
    gw_in, dbi = _tn_mm("dw_a_in", (xs,), None, dproj_segs, tn=_fit(2048, E), tk=t1024, out_dtype=BF16,
                        colsum=True)

    chip_word = chip.reshape(1).astype(jnp.int32)

    def reduce_and_update(tag, names_, sent, landed, w_, m_, v_, by_target=True):
        sums = [_sum_chips("sum_" + n, l, s, chip_word, by_target) for n, s, l in zip(names_, sent, landed)]
        theirs = _swap_cores("swap_cores_" + tag, sums)
        return {n: _adamw("adamw_" + n, [mine, other], w, m, v)
                for n, mine, other, w, m, v in zip(names_, sums, theirs, w_, m_, v_)}

    late_sent, late_landed = _push_wait("scatter_late_wait", *late_sems[:4], after=[dbi], by_target=True)
    mid_sent, mid_landed = _push_wait("scatter_mid_wait", *mid_sems[:4], after=[dbi], by_target=True)
    early_sems = _push_start("scatter_early_start", [gw_in.reshape(a_w_in.shape[:-1] + (3 * E,))],
                             by_target='cols')
    (grad_x,) = _row_mm(
        "a_in_bwd", dproj_segs, g_win, nt=True, tm=t512, tn=D, tk=E, b_whole=True,
        outs=[((T, D), F32, 'tile')], epi=lambda acc, ins, i, j: [alpha * ins[0] + acc + ins[1]],
        epi_ins=[(dr0, 'tile'), (jnp.zeros((1, D), F32) + early_sems[4][0, 0], 'col')])
    big_out = reduce_and_update(
        "late", ["a_w_out"] + late_names, mid_sent + late_sent, mid_landed + late_landed,
        [a_w_out] + late_w, [m_a_w_out, m_kv_w_down, m_kv_w_uk, m_kv_w_uv, m_b_w_in, m_b_w_uq, m_b_w_out],
        [v_a_w_out, v_kv_w_down, v_kv_w_uk, v_kv_w_uv, v_b_w_in, v_b_w_uq, v_b_w_out])

    small_full = [jnp.concatenate([dg0, dg1]), jnp.concatenate([db0, db1]), dbi, dcw, dcb, dng, dnb, dbo,
                  dkvg, dqg]
    sflat = jnp.concatenate([g.reshape(-1) for g in small_full])
    summed = _all_reduce_small(_pack_rows(sflat, F32, 8 * LANES)).reshape(-1)
    soff = 0
    sgrads = []
    for g in small_full:
        sgrads.append(summed[soff:soff + g.size].reshape(g.shape))
        soff += g.size
    local_cols = lambda g, n: lax.dynamic_slice_in_dim(g, chip * n, n, axis=g.ndim - 1)
    snames = ["ln_g", "ln_b", "a_b_in", "a_conv_w", "a_conv_b", "a_norm_g", "a_norm_b", "a_b_out",
              "kv_norm_g", "b_q_norm_g"]
    small_w = [ln_g, ln_b, a_b_in, a_conv_w, a_conv_b, a_norm_g, a_norm_b, a_b_out, kv_norm_g, b_q_norm_g]
    small_m = [m_ln_g, m_ln_b, m_a_b_in, m_a_conv_w, m_a_conv_b, m_a_norm_g, m_a_norm_b, m_a_b_out,
               m_kv_norm_g, m_b_q_norm_g]
    small_v = [v_ln_g, v_ln_b, v_a_b_in, v_a_conv_w, v_a_conv_b, v_a_norm_g, v_a_norm_b, v_a_b_out,
               v_kv_norm_g, v_b_q_norm_g]
    sharded = {"a_b_in", "a_conv_w", "a_conv_b", "a_norm_g", "a_norm_b", "a_b_out"}
    local_g = [(local_cols(g, w.shape[-1]) if n in sharded else g).reshape(w.shape)
               for n, g, w in zip(snames, sgrads, small_w)]
    at_least_2d = lambda a: a.reshape((1,) + a.shape) if a.ndim == 1 else a
    sres = _adamw_vectors([tuple(at_least_2d(a) for a in item)
                           for item in zip(local_g, small_w, small_m, small_v)])
    small_out = {n: [g] + [r.reshape(w.shape) for r in res]
                 for n, g, w, res in zip(snames, local_g, small_w, sres)}

    early_sent, early_landed = _push_wait(
        "scatter_early_wait", *early_sems[:4], by_target='cols',
        after=[grad_x, big_out["b_w_out"][1], small_out["b_q_norm_g"][1]])
    big_out.update(reduce_and_update("early", ["a_w_in"], early_sent, early_landed,
                                     [a_w_in], [m_a_w_in], [v_a_w_in], by_target='cols'))

    loss = lax.psum(loss_part[0, 0], ("x", "y", "c"))
    order = ["ln_g", "ln_b", "a_w_in", "a_b_in", "a_conv_w", "a_conv_b", "a_norm_g", "a_norm_b", "a_w_out",
             "a_b_out", "kv_w_down", "kv_norm_g", "kv_w_uk", "kv_w_uv", "b_w_in", "b_q_norm_g", "b_w_uq",
             "b_w_out"]
    outs = {**big_out, **small_out}
    result = [loss, grad_x[None]]
    for part in range(4):
        result += [outs[n][part] for n in order]
    return tuple(result)
```

```python
import functools
import math

import jax
import jax.numpy as jnp
from jax import lax
from jax.experimental import pallas as pl
from jax.experimental.pallas import tpu as pltpu

F32, BF16 = jnp.float32, jnp.bfloat16
NN = (((1,), (0,)), ((), ()))
NT = (((1,), (1,)), ((), ()))
TN = (((0,), (0,)), ((), ()))
MESH = pl.DeviceIdType.MESH
ANY = pl.BlockSpec(memory_space=pl.ANY)

LANES = 128
BF16_ROWS = 16
VMEM_LIMIT = 56 * 1024 * 1024
N_CHIPS = 4
LN_EPS = 1e-5
RMS_EPS = 1e-6
MASK_VALUE = -1e30
LOG2_E = math.log2(math.e)
ROPE_THETA = 10000.0
ROPE_DIM = 64
ROPE_HALF = ROPE_DIM // 2
ADAM_LR, ADAM_B1, ADAM_B2, ADAM_EPS, ADAM_WD, ADAM_STEP = 0.001, 0.9, 0.999, 1e-08, 0.01, 10

MAX_TILE = 2048
TM_WIDE = 256
TQ = 512
CONV_HALO = 32
CONV_LC = 1024
CONV_SUB = 256
SUBLANES = 8
CONV_RC = 32
ADAM_ROWS = 128


def _dot(a, b, dims):
    return lax.dot_general(a.astype(BF16), b.astype(BF16), dims, preferred_element_type=F32)


def _sig(x):
    return 0.5 * jnp.tanh(0.5 * x) + 0.5


def _params(n_axes):
    return pltpu.CompilerParams(dimension_semantics=("arbitrary",) * n_axes, vmem_limit_bytes=VMEM_LIMIT)


def _gcd(*v):
    return functools.reduce(math.gcd, v)


def _fit(want, dim):
    return math.gcd(min(want, MAX_TILE), dim)


def _row_mm(name, a_segs, b, *, nt, tm, tn, tk, outs, epi, epi_ins=(), b_whole=False):
    M = a_segs[0][0][0].shape[0]
    stacked = b.ndim == 3
    if stacked:
        n_blk = b.shape[2]
        N = b.shape[1] if nt else N_CHIPS * n_blk
        assert (nt and b_whole) or (not nt and tn == n_blk and tk == b.shape[1]), name
    else:
        N = b.shape[0] if nt else b.shape[1]
    nkb = [arrs[0].shape[1] // tk for arrs, _ in a_segs]
    koff = [sum(nkb[:s]) for s in range(len(nkb))]
    ni, nj, nk = M // tm, N // tn, sum(nkb)
    assert M % tm == 0 and N % tn == 0 and all(arrs[0].shape[1] % tk == 0 for arrs, _ in a_segs), name
    assert stacked or (b.shape[1] if nt else b.shape[0]) == nk * tk, name

    def spec_of(shape, kind):
        if isinstance(kind, pl.BlockSpec):
            return kind
        if kind == 'tile':
            return pl.BlockSpec((tm, tn), lambda i, j, k: (i, j))
        if kind == 'row':
            return pl.BlockSpec((tm, shape[1]), lambda i, j, k: (i, 0))
        if kind == 'col':
            return pl.BlockSpec((1, tn), lambda i, j, k: (0, j))
        assert kind == 'acc' and nj == 1, name
        return pl.BlockSpec(shape, lambda i, j, k: (0,) * len(shape))

    in_specs, operands = [], []
    for s, (arrs, _) in enumerate(a_segs):
        for arr in arrs:
            in_specs.append(pl.BlockSpec(
                (tm, tk), lambda i, j, k, s=s: (i, jnp.clip(k - koff[s], 0, nkb[s] - 1))))
            operands.append(arr)
    if b_whole:
        assert nt and nj == 1 and all(n == 1 for n in nkb), name
        in_specs.append(pl.BlockSpec(b.shape, lambda i, j, k: (0,) * b.ndim))
    elif stacked:
        in_specs.append(pl.BlockSpec((1, tk, tn), lambda i, j, k: (j, 0, 0)))
    else:
        in_specs.append(pl.BlockSpec((tn, tk), lambda i, j, k: (j, k)) if nt
                        else pl.BlockSpec((tk, tn), lambda i, j, k: (k, j)))
    operands.append(b)
    for arr, kind in epi_ins:
        in_specs.append(spec_of(arr.shape, kind))
        operands.append(arr)
    out_specs = [spec_of(shape, kind) for shape, _, kind in outs]
    out_shape = [jax.ShapeDtypeStruct(shape, dtype) for shape, dtype, _ in outs]
    n_seg_refs = [len(arrs) for arrs, _ in a_segs]

    def body(*refs):
        pos = 0
        seg_refs = []
        for n in n_seg_refs:
            seg_refs.append(refs[pos:pos + n])
            pos += n
        b_ref = refs[pos]
        e_refs = refs[pos + 1:pos + 1 + len(epi_ins)]
        o_refs = refs[pos + 1 + len(epi_ins):pos + 1 + len(epi_ins) + len(outs)]
        i, j, k = pl.program_id(0), pl.program_id(1), pl.program_id(2)

        def product(fn, rs, s=0):
            a = rs[0][...] if fn is None else fn(*[r[...] for r in rs])
            if b_whole and stacked:
                lo, hi, tot = koff[s] * tk, (koff[s] + 1) * tk, None
                for q in range(lo // n_blk, (hi - 1) // n_blk + 1):
                    c0, c1 = max(lo, q * n_blk), min(hi, (q + 1) * n_blk)
                    part = _dot(a[:, c0 - lo:c1 - lo], b_ref[q, :, c0 - q * n_blk:c1 - q * n_blk], NT)
                    tot = part if tot is None else tot + part
                return tot
            if b_whole:
                return _dot(a, b_ref[:, koff[s] * tk:(koff[s] + 1) * tk], NT)
            return _dot(a, b_ref[0] if stacked else b_ref[...], NT if nt else NN)

        def finish(acc):
            res = epi(acc, [r[...] for r in e_refs], i, j)
            for o_ref, (_, _, kind), r in zip(o_refs, outs, res):
                if isinstance(kind, str) and kind == 'acc':
                    @pl.when(i == 0)
                    def _(o_ref=o_ref, r=r):
                        o_ref[...] = r

                    @pl.when(i > 0)
                    def _(o_ref=o_ref, r=r):
                        o_ref[...] += r
                else:
                    o_ref[...] = r.astype(o_ref.dtype)

        if nk == 1:
            finish(product(a_segs[0][1], seg_refs[0]))
            return
        acc_ref = refs[-1]

        @pl.when(k == 0)
        def _():
            acc_ref[...] = jnp.zeros_like(acc_ref)

        for s, ((_, fn), rs) in enumerate(zip(a_segs, seg_refs)):
            def accumulate(fn=fn, rs=rs, s=s):
                acc_ref[...] += product(fn, rs, s)
            if len(a_segs) == 1:
                accumulate()
            else:
                pl.when(jnp.logical_and(k >= koff[s], k < koff[s] + nkb[s]))(accumulate)

        @pl.when(k == nk - 1)
        def _():
            finish(acc_ref[...])

    return pl.pallas_call(
        body, name=name, grid=(ni, nj, nk), in_specs=in_specs, out_specs=out_specs, out_shape=out_shape,
        scratch_shapes=[] if nk == 1 else [pltpu.VMEM((tm, tn), F32)], compiler_params=_params(3),
    )(*operands)


def _tn_mm(name, a_arrs, a_fn, b_segs, *, tn, tk, out_dtype, shard_major=False, colsum=False):
    T, M = a_arrs[0].shape
    nbj = [arrs[0].shape[1] // tn for arrs, _ in b_segs]
    joff = [sum(nbj[:s]) for s in range(len(nbj))]
    nj, nk = sum(nbj), T // tk
    N = nj * tn
    assert T % tk == 0 and all(arrs[0].shape[1] % tn == 0 for arrs, _ in b_segs), name

    in_specs = [pl.BlockSpec((tk, M), lambda j, k: (k, 0)) for _ in a_arrs]
    operands = list(a_arrs)
    for s, (arrs, _) in enumerate(b_segs):
        for arr in arrs:
            in_specs.append(pl.BlockSpec(
                (tk, tn), lambda j, k, s=s: (k, jnp.clip(j - joff[s], 0, nbj[s] - 1))))
            operands.append(arr)
    if shard_major:
        per = (N // N_CHIPS) // tn
        assert per * tn * N_CHIPS == N, name
        out_shape = [jax.ShapeDtypeStruct((N_CHIPS, M, N // N_CHIPS), out_dtype)]
        out_specs = [pl.BlockSpec((1, M, tn), lambda j, k: (j // per, 0, j % per))]
    else:
        out_shape = [jax.ShapeDtypeStruct((M, N), out_dtype)]
        out_specs = [pl.BlockSpec((M, tn), lambda j, k: (0, j))]
    if colsum:
        out_shape.append(jax.ShapeDtypeStruct((1, N), F32))
        out_specs.append(pl.BlockSpec((1, tn), lambda j, k: (0, j)))
    n_a = len(a_arrs)
    n_seg_refs = [len(arrs) for arrs, _ in b_segs]

    def body(*refs):
        a_refs = refs[:n_a]
        pos = n_a
        seg_refs = []
        for n in n_seg_refs:
            seg_refs.append(refs[pos:pos + n])
            pos += n
        o_ref = refs[pos]
        cs_ref = refs[pos + 1] if colsum else None
        acc_ref = refs[-1]
        j, k = pl.program_id(0), pl.program_id(1)

        @pl.when(k == 0)
        def _():
            acc_ref[...] = jnp.zeros_like(acc_ref)
            if colsum:
                cs_ref[...] = jnp.zeros_like(cs_ref)

        for s, ((_, fn), rs) in enumerate(zip(b_segs, seg_refs)):
            def accumulate(fn=fn, rs=rs):
                a = a_refs[0][...] if a_fn is None else a_fn(*[r[...] for r in a_refs])
                bt = rs[0][...] if fn is None else fn(*[r[...] for r in rs])
                acc_ref[...] += _dot(a, bt, TN)
                if colsum:
                    cs_ref[...] += jnp.sum(bt.astype(F32), axis=0, keepdims=True)
            if len(b_segs) == 1:
                accumulate()
            else:
                pl.when(jnp.logical_and(j >= joff[s], j < joff[s] + nbj[s]))(accumulate)

        @pl.when(k == nk - 1)
        def _():
            if shard_major:
                o_ref[0] = acc_ref[...].astype(o_ref.dtype)
            else:
                o_ref[...] = acc_ref[...].astype(o_ref.dtype)

    res = pl.pallas_call(
        body, name=name, grid=(nj, nk), in_specs=in_specs, out_specs=out_specs, out_shape=out_shape,
        scratch_shapes=[pltpu.VMEM((M, tn), F32)], compiler_params=_params(2),
    )(*operands)
    return res if colsum else res[0]


def _silu(z):
    return z * _sig(z)


def _silu_and_grad(z):
    s = _sig(z)
    return z * s, s * (1.0 + z * (1.0 - s))


def _gate(o, z):
    return o.astype(F32) * _silu(z.astype(F32))


def _ln_stats(r):
    mu = jnp.mean(r, axis=1, keepdims=True)
    xc = r - mu
    var = jnp.mean(xc * xc, axis=1, keepdims=True)
    rstd = lax.rsqrt(var + LN_EPS)
    return xc * rstd, rstd


def _ln_bwd(dy, xhat, rstd, g):
    dxh = dy * g
    m1 = jnp.mean(dxh, axis=1, keepdims=True)
    m2 = jnp.mean(dxh * xhat, axis=1, keepdims=True)
    return (rstd * (dxh - m1 - xhat * m2), jnp.sum(dy * xhat, axis=0, keepdims=True),
            jnp.sum(dy, axis=0, keepdims=True))


def _rms_fwd(x, g):
    rstd = lax.rsqrt(jnp.mean(x * x, axis=1, keepdims=True) + RMS_EPS)
    return x * rstd * g


def _rms_bwd(dy, x, g):
    rstd = lax.rsqrt(jnp.mean(x * x, axis=1, keepdims=True) + RMS_EPS)
    xn = x * rstd
    dxn = dy * g
    return rstd * (dxn - xn * jnp.mean(dxn * xn, axis=1, keepdims=True)), jnp.sum(dy * xn, axis=0, keepdims=True)


def _rope(x, cos, sin, transpose=False):
    parts = []
    for g in range(x.shape[1] // LANES):
        xg = x[:, g * LANES:(g + 1) * LANES]
        if transpose:
            parts.append(xg * cos + pltpu.roll(xg * sin, LANES // 2, 1))
        else:
            parts.append(xg * cos + pltpu.roll(xg, LANES // 2, 1) * sin)
    return parts[0] if len(parts) == 1 else jnp.concatenate(parts, axis=1)


def _shifted_rows(window, rc):
    n = window.shape[0]
    for b in range(SUBLANES):
        rolled = window if b == 0 else pltpu.roll(window, n - b, 0)
        for a8 in range(0, n - rc - b + 1, SUBLANES):
            yield a8 + b, rolled[a8:a8 + rc]


def _conv_fwd(proj, conv_w, conv_b, norm_g, norm_b, E, tm):
    T = proj.shape[0]
    kc = conv_w.shape[0]
    hb, rc = CONV_HALO, min(CONV_RC, tm)
    ni, ratio = T // tm, tm // hb
    sub = min(CONV_SUB, E)
    base = hb - (kc - 1)

    def body(val_ref, gate_ref, z_ref, valh_ref, gateh_ref, w_ref, cb_ref, g_ref, b_ref, u1_ref, u4_ref, ubuf):
        i = pl.program_id(0)
        ubuf[hb:, :] = val_ref[...] * _sig(gate_ref[...])
        halo = valh_ref[...] * _sig(gateh_ref[...])
        ubuf[0:hb, :] = jnp.where(i > 0, halo, 0.0)
        for l0 in range(0, E, sub):
            ls = slice(l0, l0 + sub)
            for r0 in range(0, tm, rc):
                acc = jnp.zeros((rc, sub), F32) + cb_ref[:, ls]
                for off, rows in _shifted_rows(ubuf[r0:r0 + hb + rc, ls], rc):
                    if 0 <= off - base < kc:
                        acc += w_ref[off - base:off - base + 1, ls] * rows
                u1_ref[r0:r0 + rc, ls] = acc
        xhat, _ = _ln_stats(u1_ref[...])
        u4_ref[...] = (_silu(xhat * g_ref[...] + b_ref[...]) * _silu(z_ref[...])).astype(BF16)

    main = lambda col: pl.BlockSpec((tm, E), lambda i: (i, col))
    halo = lambda col: pl.BlockSpec((hb, E), lambda i: (jnp.maximum(i * ratio - 1, 0), col))
    whole = lambda a: pl.BlockSpec(a.shape, lambda i: (0, 0))
    return pl.pallas_call(
        body, name="conv_fwd", grid=(ni,),
        in_specs=[main(0), main(1), main(2), halo(0), halo(1), whole(conv_w), whole(conv_b), whole(norm_g),
                  whole(norm_b)],
        out_specs=[main(0), main(0)],
        out_shape=[jax.ShapeDtypeStruct((T, E), F32), jax.ShapeDtypeStruct((T, E), BF16)],
        scratch_shapes=[pltpu.VMEM((hb + tm, E), F32)], compiler_params=_params(1),
    )(proj, proj, proj, proj, proj, conv_w, conv_b, norm_g, norm_b)


def _conv_bwd(du1, proj, conv_w, E, tm):
    T = du1.shape[0]
    kc = conv_w.shape[0]
    lc, hb, rc = min(CONV_LC, E), CONV_HALO, min(CONV_RC, tm)
    nl, ni, ratio = E // lc, T // tm, tm // hb
    gate_off = E // lc
    last_halo = T // hb - 1

    sub = min(CONV_SUB, lc)
    base = hb - (kc - 1)

    def body(du_ref, dun_ref, val_ref, gate_ref, valh_ref, gateh_ref, w_ref,
             dval_ref, dgate_ref, dw_ref, db_ref, ubuf, dbuf, sbuf, dw_sc):
        i = pl.program_id(1)
        sbuf[...] = _sig(gate_ref[...])
        ubuf[hb:, :] = val_ref[...] * sbuf[...]
        halo = valh_ref[...] * _sig(gateh_ref[...])
        ubuf[0:hb, :] = jnp.where(i > 0, halo, 0.0)
        dbuf[0:tm, :] = du_ref[...]
        dbuf[tm:, :] = jnp.where(i < ni - 1, dun_ref[...], 0.0)

        @pl.when(i == 0)
        def _():
            dw_sc[...] = jnp.zeros_like(dw_sc)
            db_ref[...] = jnp.zeros_like(db_ref)

        db_ref[...] += jnp.sum(du_ref[...], axis=0, keepdims=True)
        for l0 in range(0, lc, sub):
            ls = slice(l0, l0 + sub)
            for r0 in range(0, tm, rc):
                dwin = dbuf[r0:r0 + rc + hb, ls]
                dchunk = dwin[0:rc]
                for off, rows in _shifted_rows(ubuf[r0:r0 + hb + rc, ls], rc):
                    k = off - base
                    if 0 <= k < kc:
                        prod = rows * dchunk
                        part = prod[0:SUBLANES]
                        for s8 in range(SUBLANES, rc, SUBLANES):
                            part = part + prod[s8:s8 + SUBLANES]
                        dw_sc[k, :, ls] += part
                acc = jnp.zeros((rc, sub), F32)
                for off, rows in _shifted_rows(dwin, rc):
                    k = (kc - 1) - off
                    if 0 <= k < kc:
                        acc += w_ref[k:k + 1, ls] * rows
                v, s = val_ref[r0:r0 + rc, ls], sbuf[r0:r0 + rc, ls]
                dval_ref[r0:r0 + rc, ls] = (acc * s).astype(BF16)
                dgate_ref[r0:r0 + rc, ls] = (acc * v * s * (1.0 - s)).astype(BF16)

        @pl.when(i == ni - 1)
        def _():
            for k in range(kc):
                dw_ref[k:k + 1, :] = jnp.sum(dw_sc[k], axis=0, keepdims=True)

    return pl.pallas_call(
        body, name="conv_bwd", grid=(nl, ni),
        in_specs=[
            pl.BlockSpec((tm, lc), lambda l, i: (i, l)),
            pl.BlockSpec((hb, lc), lambda l, i: (jnp.minimum((i + 1) * ratio, last_halo), l)),
            pl.BlockSpec((tm, lc), lambda l, i: (i, l)),
            pl.BlockSpec((tm, lc), lambda l, i: (i, gate_off + l)),
            pl.BlockSpec((hb, lc), lambda l, i: (jnp.maximum(i * ratio - 1, 0), l)),
            pl.BlockSpec((hb, lc), lambda l, i: (jnp.maximum(i * ratio - 1, 0), gate_off + l)),
            pl.BlockSpec((kc, lc), lambda l, i: (0, l)),
        ],
        out_specs=[pl.BlockSpec((tm, lc), lambda l, i: (i, l)), pl.BlockSpec((tm, lc), lambda l, i: (i, l)),
                   pl.BlockSpec((kc, lc), lambda l, i: (0, l)), pl.BlockSpec((1, lc), lambda l, i: (0, l))],
        out_shape=[jax.ShapeDtypeStruct((T, E), BF16), jax.ShapeDtypeStruct((T, E), BF16),
                   jax.ShapeDtypeStruct((kc, E), F32), jax.ShapeDtypeStruct((1, E), F32)],
        scratch_shapes=[pltpu.VMEM((hb + tm, lc), F32), pltpu.VMEM((tm + hb, lc), F32),
                        pltpu.VMEM((tm, lc), F32), pltpu.VMEM((kc, SUBLANES, lc), F32)],
        compiler_params=_params(2),
    )(du1, du1, proj, proj, proj, proj, conv_w)


def _attn_fwd(q_all, kv, kr, H, tq, scale):
    T = q_all.shape[0]
    nq = T // tq
    pair = 2
    W = pair * LANES
    assert H % pair == 0
    hp_n = H // pair

    def body(qn_ref, qr_ref, kn_ref, kr_ref, v_ref, o_ref, lse_ref, *scratch):
        qi = pl.program_id(1)
        chains = [scratch[4 * a:4 * a + 4] for a in range(pair)]
        lanes = [slice(a * LANES, (a + 1) * LANES) for a in range(pair)]
        groups = [slice(c * LANES, (c + 1) * LANES) for c in range(tq // LANES)]

        def fold(x, op):
            r = x[:, groups[0]]
            for gsl in groups[1:]:
                r = op(r, x[:, gsl])
            return r

        for _, m_sc, l_sc, acc_sc in chains:
            m_sc[...] = jnp.full_like(m_sc, MASK_VALUE)
            l_sc[...] = jnp.zeros_like(l_sc)
            acc_sc[...] = jnp.zeros_like(acc_sc)

        def scores(j, masked):
            rows = pl.ds(pl.multiple_of(j * tq, tq), tq)
            krope = kr_ref[rows, :]
            for a, (s_sc, m_sc, _, _) in enumerate(chains):
                q = jnp.concatenate([qn_ref[:, lanes[a]], qr_ref[:, lanes[a]]], axis=1)
                k = jnp.concatenate([kn_ref[rows, lanes[a]], krope], axis=1)
                s = _dot(q, k, NT) * (scale * LOG2_E)
                if masked:
                    row = lax.broadcasted_iota(jnp.int32, s.shape, 0)
                    col = lax.broadcasted_iota(jnp.int32, s.shape, 1)
                    s = jnp.where(col <= row, s, MASK_VALUE)
                s_sc[j] = s
                m_sc[...] = jnp.maximum(m_sc[...], fold(s, jnp.maximum))

        def two_per_trip(fn, count):
            def two(p, carry):
                fn(2 * p)
                fn(2 * p + 1)
                return carry

            lax.fori_loop(0, count // 2, two, 0)

            @pl.when(count % 2 == 1)
            def _():
                fn(count - 1)

        two_per_trip(functools.partial(scores, masked=False), qi)
        scores(qi, True)
        for _, m_sc, _, _ in chains:
            m_sc[...] = jnp.broadcast_to(jnp.max(m_sc[...], axis=1, keepdims=True), m_sc.shape)

        def weigh(j):
            rows = pl.ds(pl.multiple_of(j * tq, tq), tq)
            for a, (s_sc, m_sc, l_sc, acc_sc) in enumerate(chains):
                s, m = s_sc[j], m_sc[...]
                p = jnp.concatenate([jnp.exp2(s[:, gsl] - m) for gsl in groups], axis=1)
                l_sc[...] += fold(p, jnp.add)
                acc_sc[...] += _dot(p, v_ref[rows, lanes[a]], NN)

        two_per_trip(weigh, qi + 1)
        for a, (_, m_sc, l_sc, acc_sc) in enumerate(chains):
            l = jnp.sum(l_sc[...], axis=1, keepdims=True)
            o_ref[:, lanes[a]] = (acc_sc[...] / l).astype(BF16)
            lse_ref[a] = m_sc[:, 0:1] * (1.0 / LOG2_E) + jnp.log(l)

    chain_scratch = [pltpu.VMEM((nq, tq, tq), F32), pltpu.VMEM((tq, LANES), F32), pltpu.VMEM((tq, LANES), F32),
                     pltpu.VMEM((tq, LANES), F32)]
    return pl.pallas_call(
        body, name="attn_fwd", grid=(hp_n, nq),
        in_specs=[pl.BlockSpec((tq, W), lambda hp, qi: (qi, hp)),
                  pl.BlockSpec((tq, W), lambda hp, qi: (qi, hp_n + hp)),
                  pl.BlockSpec((T, W), lambda hp, qi: (0, hp)),
                  pl.BlockSpec((T, LANES), lambda hp, qi: (0, 0)),
                  pl.BlockSpec((T, W), lambda hp, qi: (0, hp_n + hp))],
        out_specs=[pl.BlockSpec((tq, W), lambda hp, qi: (qi, hp)),
                   pl.BlockSpec((pair, tq, 1), lambda hp, qi: (hp, qi, 0))],
        out_shape=[jax.ShapeDtypeStruct((T, H * LANES), BF16), jax.ShapeDtypeStruct((H, T, 1), F32)],
        scratch_shapes=chain_scratch * pair, compiler_params=_params(2),
    )(q_all, q_all, kv, kr, kv)


def _attn_bwd(q_all, kv, kr, do, o, lse, cos, sin, H, tq, scale):
    T = q_all.shape[0]
    nq = T // tq
    HV = H * LANES
    pair = 2
    tk2 = pair * tq
    ng = T // tk2
    assert ng * tk2 == T and pair == 2

    def body(qn_ref, qr_ref, kn_ref, kr_ref, v_ref, do_ref, o_ref, lse_ref, cos_ref, sin_ref,
             dqn_ref, dqr_ref, dkn_ref, dkr_ref, dv_ref, dq_sc, dk_sc, dv_sc):
        g = pl.program_id(1)

        @pl.when(g == 0)
        def _():
            dq_sc[...] = jnp.zeros_like(dq_sc)

        key_rows = [slice(c * tq, (c + 1) * tq) for c in range(pair)]

        def block(qi, modes):
            rows = pl.ds(pl.multiple_of(qi * tq, tq), tq)
            q = jnp.concatenate([qn_ref[rows, :], qr_ref[rows, :]], axis=1)
            dov = do_ref[rows, :]
            delta = jnp.sum(dov.astype(F32) * o_ref[rows, :].astype(F32), axis=1, keepdims=True)
            lse_q = lse_ref[0, rows, :]
            dq, dkv = None, []
            for kr_, masked in zip(key_rows, modes):
                if masked is None:
                    dkv.append(None)
                    continue
                k = jnp.concatenate([kn_ref[kr_, :], kr_ref[kr_, :]], axis=1)
                s = _dot(q, k, NT) * scale
                if masked:
                    row = lax.broadcasted_iota(jnp.int32, s.shape, 0)
                    col = lax.broadcasted_iota(jnp.int32, s.shape, 1)
                    s = jnp.where(col <= row, s, MASK_VALUE)
                p = jnp.exp(s - lse_q)
                dv = _dot(p, dov, TN)
                dp = _dot(dov, v_ref[kr_, :], NT)
                ds = (p * (dp - delta) * scale).astype(BF16)
                dkv.append((_dot(ds, q, TN), dv))
                part = _dot(ds, k, NN)
                dq = part if dq is None else dq + part
            return rows, dq, dkv

        rows_a, dq_a, (kv_a0, _) = block(pair * g, (True, None))
        rows_b, dq_b, (kv_b0, kv_b1) = block(pair * g + 1, (False, True))
        dk_sc[key_rows[0], :] = kv_a0[0] + kv_b0[0]
        dv_sc[key_rows[0], :] = kv_a0[1] + kv_b0[1]
        dk_sc[key_rows[1], :] = kv_b1[0]
        dv_sc[key_rows[1], :] = kv_b1[1]
        dq_sc[rows_a, :] += dq_a
        dq_sc[rows_b, :] += dq_b

        def below(trip, carry):
            for qi in (pair * g + pair + 2 * trip, pair * g + pair + 2 * trip + 1):
                rows, dq, dkv = block(qi, (False, False))
                for kr_, (dk, dv) in zip(key_rows, dkv):
                    dk_sc[kr_, :] += dk
                    dv_sc[kr_, :] += dv
                dq_sc[rows, :] += dq
            return carry

        lax.fori_loop(0, (nq - pair * g - pair) // 2, below, 0)
        dkn_ref[...] = dk_sc[:, :LANES].astype(BF16)
        dkr_ref[...] = dk_sc[:, LANES:].astype(BF16)
        dv_ref[...] = dv_sc[...].astype(BF16)

        @pl.when(g == ng - 1)
        def _():
            dqn_ref[...] = dq_sc[:, :LANES].astype(BF16)
            dqr_ref[...] = _rope(dq_sc[:, LANES:], cos_ref[...], sin_ref[...], transpose=True).astype(BF16)

    whole = lambda col: pl.BlockSpec((T, LANES), col)
    tile = lambda col: pl.BlockSpec((tk2, LANES), col)
    return pl.pallas_call(
        body, name="attn_bwd", grid=(H, ng),
        in_specs=[whole(lambda h, g: (0, h)), whole(lambda h, g: (0, H + h)),
                  tile(lambda h, g: (g, h)), tile(lambda h, g: (g, 0)), tile(lambda h, g: (g, H + h)),
                  whole(lambda h, g: (0, h)), whole(lambda h, g: (0, h)),
                  pl.BlockSpec((1, T, 1), lambda h, g: (h, 0, 0)),
                  whole(lambda h, g: (0, 0)), whole(lambda h, g: (0, 0))],
        out_specs=[whole(lambda h, g: (0, h)), whole(lambda h, g: (0, h)),
                   tile(lambda h, g: (g, h)), tile(lambda h, g: (g, h)), tile(lambda h, g: (g, h))],
        out_shape=[jax.ShapeDtypeStruct((T, HV), BF16), jax.ShapeDtypeStruct((T, HV), BF16),
                   jax.ShapeDtypeStruct((T, HV), BF16), jax.ShapeDtypeStruct((T, HV), BF16),
                   jax.ShapeDtypeStruct((T, HV), BF16)],
        scratch_shapes=[pltpu.VMEM((T, 2 * LANES), F32), pltpu.VMEM((tk2, 2 * LANES), F32),
                        pltpu.VMEM((tk2, LANES), F32)],
        compiler_params=_params(2),
    )(q_all, q_all, kv, kr, kv, do, o, lse, cos, sin)


def _key_rope_bwd(dkr_heads, cos, sin, H, tm):
    T, HV = dkr_heads.shape

    def body(dkr_ref, cos_ref, sin_ref, dk_ref):
        dk = dkr_ref[...].astype(F32)
        tot = dk[:, 0:LANES]
        for h in range(1, H):
            tot = tot + dk[:, h * LANES:(h + 1) * LANES]
        dk_ref[...] = _rope(tot, cos_ref[...], sin_ref[...], transpose=True)

    return pl.pallas_call(
        body, name="key_rope_bwd", grid=(T // tm,),
        in_specs=[pl.BlockSpec((tm, HV), lambda i: (i, 0)),
                  pl.BlockSpec((tm, LANES), lambda i: (i, 0)), pl.BlockSpec((tm, LANES), lambda i: (i, 0))],
        out_specs=pl.BlockSpec((tm, LANES), lambda i: (i, 0)),
        out_shape=jax.ShapeDtypeStruct((T, LANES), F32), compiler_params=_params(1),
    )(dkr_heads, cos, sin)


def _place():
    x, y, c = lax.axis_index("x"), lax.axis_index("y"), lax.axis_index("c")
    chips = [(1 - x, y), (x, 1 - y), (1 - x, 1 - y)]
    return x, y, c, chips


def _all_gather_chips(arrs):
    n = len(arrs)
    halves = [a.shape[0] // 2 for a in arrs]
    assert all(h * 2 == a.shape[0] and h % BF16_ROWS == 0 for h, a in zip(halves, arrs))

    def body(*refs):
        w_refs, out_refs = refs[:n], refs[n:2 * n]
        send_sems, recv_sems, local_sems = refs[2 * n:]
        x, y, c, chips = _place()
        sibling = (x, y, 1 - c)
        waits = []
        for w, (w_ref, out_ref, half) in enumerate(zip(w_refs, out_refs, halves)):
            def region(px, py, pc, out_ref=out_ref, half=half):
                return out_ref.at[2 * px + py, pl.ds(pc * half, half), :]

            def copy(k, block, to, src=None, w=w, region=region):
                return pltpu.make_async_remote_copy(
                    src_ref=region(*block) if src is None else src, dst_ref=region(*block),
                    send_sem=send_sems.at[6 * w + k], recv_sem=recv_sems.at[6 * w + k],
                    device_id=to, device_id_type=MESH)

            mine = pltpu.make_async_copy(w_ref, out_ref.at[2 * x + y], local_sems.at[w])
            mine.start()
            my_half = w_ref.at[pl.ds(c * half, half), :]
            first = [copy(j, (x, y, c), (*chip, c), src=my_half) for j, chip in enumerate(chips)]
            for cp in first:
                cp.start()
            waits.append((copy, mine, first))
        for copy, mine, first in waits:
            passed = [copy(3 + j, (*chip, c), sibling) for j, chip in enumerate(chips)]
            for j, chip in enumerate(chips):
                copy(j, (*chip, c), (x, y, c)).wait_recv()
                passed[j].start()
            for j, chip in enumerate(chips):
                copy(3 + j, (*chip, 1 - c), (x, y, c)).wait_recv()
            for cp in first + passed:
                cp.wait_send()
            mine.wait()

    return pl.pallas_call(
        body, name="gather_weights", in_specs=[ANY] * n, out_specs=[ANY] * n,
        out_shape=[jax.ShapeDtypeStruct((N_CHIPS,) + a.shape, a.dtype) for a in arrs],
        scratch_shapes=[pltpu.SemaphoreType.DMA((6 * n,)), pltpu.SemaphoreType.DMA((6 * n,)),
                        pltpu.SemaphoreType.DMA((n,))],
    )(*arrs)


def _swap_cores(name, parts):
    n = len(parts)

    def body(*refs):
        p_refs, r_refs = refs[:n], refs[n:2 * n]
        send_sems, recv_sems = refs[2 * n:]
        x, y, c, _ = _place()
        copies = [pltpu.make_async_remote_copy(
            src_ref=p_refs[w], dst_ref=r_refs[w], send_sem=send_sems.at[w], recv_sem=recv_sems.at[w],
            device_id=(x, y, 1 - c), device_id_type=MESH) for w in range(n)]
        for cp in copies:
            cp.start()
        for cp in copies:
            cp.wait()

    return pl.pallas_call(
        body, name=name, in_specs=[ANY] * n, out_specs=[ANY] * n,
        out_shape=[jax.ShapeDtypeStruct(p.shape, p.dtype) for p in parts],
        scratch_shapes=[pltpu.SemaphoreType.DMA((n,)), pltpu.SemaphoreType.DMA((n,))],
    )(*parts)


HBM = pl.BlockSpec(memory_space=pltpu.HBM)
SEM = pl.BlockSpec(memory_space=pltpu.SEMAPHORE)
EFFECT = pltpu.SideEffectType.DATAFLOW_SIDE_EFFECTING


def _push_copies(a_refs, l_refs, send_sems, recv_sems, by_target):
    x, y, c, chips = _place()
    me = 2 * x + y

    def part(a_ref, q):
        if by_target == 'cols':
            n = a_ref.shape[-1] // N_CHIPS
            return a_ref.at[(slice(None),) * (len(a_ref.shape) - 1) + (pl.ds(pl.multiple_of(q * n, LANES), n),)]
        return a_ref.at[q] if by_target else a_ref

    out = []
    for w, (a_ref, l_ref) in enumerate(zip(a_refs, l_refs)):
        for j, (px, py) in enumerate(chips):
            peer = 2 * px + py
            out.append((
                pltpu.make_async_remote_copy(
                    src_ref=part(a_ref, peer), dst_ref=l_ref.at[me],
                    send_sem=send_sems.at[3 * w + j], recv_sem=recv_sems.at[3 * w + j],
                    device_id=(px, py, c), device_id_type=MESH),
                pltpu.make_async_remote_copy(
                    src_ref=part(a_ref, me), dst_ref=l_ref.at[peer],
                    send_sem=send_sems.at[3 * w + j], recv_sem=recv_sems.at[3 * w + j],
                    device_id=(px, py, c), device_id_type=MESH)))
    return out


def _landing_shape(a, by_target):
    if by_target == 'cols':
        return (N_CHIPS,) + a.shape[:-1] + (a.shape[-1] // N_CHIPS,)
    return (N_CHIPS,) + (a.shape[1:] if by_target else a.shape)


def _push_start(name, arrs, by_target):
    n = len(arrs)
    lands = [lax.empty(_landing_shape(a, by_target), a.dtype) for a in arrs]

    def body(*refs):
        a_refs, l_refs = refs[:n], refs[n:2 * n]
        send_sems, recv_sems = refs[2 * n], refs[2 * n + 1]
        token = refs[-1]
        for send, _ in _push_copies(a_refs, l_refs, send_sems, recv_sems, by_target):
            send.start()
        token[...] = jnp.zeros_like(token)

    res = pl.pallas_call(
        body, name=name,
        out_shape=(pltpu.SemaphoreType.DMA((3 * n,)), pltpu.SemaphoreType.DMA((3 * n,)),
                   *[pltpu.HBM(a.shape, a.dtype) for a in arrs], *[pltpu.HBM(l.shape, l.dtype) for l in lands],
                   jax.ShapeDtypeStruct((8, LANES), F32)),
        in_specs=[HBM] * (2 * n), out_specs=(SEM, SEM, *[HBM] * (2 * n), pl.BlockSpec(memory_space=pltpu.VMEM)),
        input_output_aliases={i: 2 + i for i in range(2 * n)},
        compiler_params=pltpu.CompilerParams(has_side_effects=EFFECT),
    )(*[pltpu.with_memory_space_constraint(a, pltpu.HBM) for a in list(arrs) + lands])
    return res[0], res[1], list(res[2:2 + n]), list(res[2 + n:2 + 2 * n]), res[-1]


def _push_wait(name, send_sems, recv_sems, arrs, lands, after, by_target):
    n = len(arrs)

    def body(*refs):
        a_refs, l_refs = refs[:n], refs[n:2 * n]
        s_sems, r_sems = refs[2 * n], refs[2 * n + 1]
        for send, recv in _push_copies(a_refs, l_refs, s_sems, r_sems, by_target):
            send.wait_send()
            recv.wait_recv()

    res = pl.pallas_call(
        body, name=name,
        out_shape=[pltpu.HBM(a.shape, a.dtype) for a in list(arrs) + list(lands)],
        in_specs=[HBM] * (2 * n) + [SEM, SEM] + [ANY] * len(after), out_specs=[HBM] * (2 * n),
        input_output_aliases={i: i for i in range(2 * n)},
        compiler_params=pltpu.CompilerParams(has_side_effects=EFFECT),
    )(*arrs, *lands, send_sems, recv_sems, *after)
    return list(res[:n]), list(res[n:])


def _all_reduce_small(part):
    def body(p_ref, out_ref, sib_buf, chip_buf, send_sems, recv_sems):
        x, y, c, chips = _place()
        me = 2 * x + y
        swap = pltpu.make_async_remote_copy(
            src_ref=p_ref, dst_ref=sib_buf, send_sem=send_sems.at[0], recv_sem=recv_sems.at[0],
            device_id=(x, y, 1 - c), device_id_type=MESH)
        swap.start()
        swap.wait()
        chip_buf[me] = p_ref[...] + sib_buf[...]
        copies = []
        for j, (px, py) in enumerate(chips):
            cp = pltpu.make_async_remote_copy(
                src_ref=chip_buf.at[me], dst_ref=chip_buf.at[me], send_sem=send_sems.at[1 + j],
                recv_sem=recv_sems.at[1 + j], device_id=(px, py, c), device_id_type=MESH)
            cp.start()
            copies.append(cp)
        for j, (px, py) in enumerate(chips):
            pltpu.make_async_remote_copy(
                src_ref=chip_buf.at[me], dst_ref=chip_buf.at[2 * px + py], send_sem=send_sems.at[1 + j],
                recv_sem=recv_sems.at[1 + j], device_id=(px, py, c), device_id_type=MESH).wait_recv()
        for cp in copies:
            cp.wait_send()
        tot = chip_buf[0]
        for q in range(1, N_CHIPS):
            tot = tot + chip_buf[q]
        out_ref[...] = tot

    vm = pl.BlockSpec(memory_space=pltpu.VMEM)
    return pl.pallas_call(
        body, name="all_reduce_small", in_specs=[vm], out_specs=vm,
        out_shape=jax.ShapeDtypeStruct(part.shape, F32),
        scratch_shapes=[pltpu.VMEM(part.shape, F32), pltpu.VMEM((N_CHIPS,) + part.shape, F32),
                        pltpu.SemaphoreType.DMA((N_CHIPS,)), pltpu.SemaphoreType.DMA((N_CHIPS,))],
    )(part)


def _row_tiles(shape):
    ax = next(d for d, s in enumerate(shape) if s > 1)
    tr = _gcd(ADAM_ROWS, shape[ax])
    block = tuple(tr if d == ax else s for d, s in enumerate(shape))
    return shape[ax] // tr, block, lambda i: tuple(i if d == ax else 0 for d in range(len(shape)))


def _sum_chips(name, landed, sent, chip, by_target):
    shape = landed.shape[1:]
    steps, block, index = _row_tiles(shape)
    if by_target == 'cols':
        own_spec = pl.BlockSpec(block, lambda i, c: index(i)[:-1] + (c[0],))
    else:
        own_spec = pl.BlockSpec((1,) + block, lambda i, c: (c[0],) + index(i))

    def body(chip_ref, l_ref, s_ref, o_ref):
        own = (s_ref[...] if by_target == 'cols' else s_ref[0]).astype(F32)
        tot = None
        for q in range(N_CHIPS):
            term = jnp.where(chip_ref[0] == q, own, l_ref[q].astype(F32))
            tot = term if tot is None else tot + term
        o_ref[...] = tot

    return pl.pallas_call(
        body, name=name,
        grid_spec=pltpu.PrefetchScalarGridSpec(
            num_scalar_prefetch=1, grid=(steps,),
            in_specs=[pl.BlockSpec((N_CHIPS,) + block, lambda i, c: (0,) + index(i)), own_spec],
            out_specs=pl.BlockSpec(block, lambda i, c: index(i))),
        out_shape=jax.ShapeDtypeStruct(shape, F32), compiler_params=_params(1),
    )(chip, landed, sent)


def _adamw_math(g, w, m, v):
    mn = ADAM_B1 * m + (1.0 - ADAM_B1) * g
    vn = ADAM_B2 * v + (1.0 - ADAM_B2) * jnp.square(g)
    m_hat = mn / (1.0 - ADAM_B1 ** ADAM_STEP)
    v_hat = vn / (1.0 - ADAM_B2 ** ADAM_STEP)
    return -ADAM_LR * (m_hat / (jnp.sqrt(v_hat) + ADAM_EPS) + ADAM_WD * w), mn, vn


def _adamw(name, g_parts, w, m, v):
    steps, block, index = _row_tiles(w.shape)
    n = len(g_parts)

    def body(*refs):
        g = refs[0][...]
        for r in refs[1:n]:
            g = g + r[...]
        w_ref, m_ref, v_ref, go_ref, d_ref, mo_ref, vo_ref = refs[n:]
        go_ref[...] = g
        d_ref[...], mo_ref[...], vo_ref[...] = _adamw_math(g, w_ref[...], m_ref[...], v_ref[...])

    spec = pl.BlockSpec(block, index)
    return pl.pallas_call(
        body, name=name, grid=(steps,), in_specs=[spec] * (n + 3), out_specs=[spec] * 4,
        out_shape=[jax.ShapeDtypeStruct(w.shape, F32)] * 4, compiler_params=_params(1),
    )(*g_parts, w, m, v)


def _adamw_vectors(items):
    n = len(items)

    def body(*refs):
        ins, outs = refs[:4 * n], refs[4 * n:]
        for k in range(n):
            g, w, m, v = (r[...] for r in ins[4 * k:4 * k + 4])
            outs[3 * k][...], outs[3 * k + 1][...], outs[3 * k + 2][...] = _adamw_math(g, w, m, v)

    vm = pl.BlockSpec(memory_space=pltpu.VMEM)
    res = pl.pallas_call(
        body, name="adamw_vectors", in_specs=[vm] * (4 * n), out_specs=[vm] * (3 * n),
        out_shape=[jax.ShapeDtypeStruct(it[1].shape, F32) for it in items for _ in range(3)],
    )(*[a for it in items for a in it])
    return [res[3 * k:3 * k + 3] for k in range(n)]


def _pack_rows(flat, dtype, multiple):
    n = flat.shape[0]
    total = -(-n // multiple) * multiple
    return jnp.pad(flat, (0, total - n)).astype(dtype).reshape(total // LANES, LANES)


def kernel(x, positions, ln_g, ln_b, a_w_in, a_b_in, a_conv_w, a_conv_b, a_norm_g, a_norm_b, a_w_out, a_b_out, kv_w_down, kv_norm_g, kv_w_uk, kv_w_uv, b_w_in, b_q_norm_g, b_w_uq, b_w_out, loss_target, m_ln_g, m_ln_b, m_a_w_in, m_a_b_in, m_a_conv_w, m_a_conv_b, m_a_norm_g, m_a_norm_b, m_a_w_out, m_a_b_out, m_kv_w_down, m_kv_norm_g, m_kv_w_uk, m_kv_w_uv, m_b_w_in, m_b_q_norm_g, m_b_w_uq, m_b_w_out, v_ln_g, v_ln_b, v_a_w_in, v_a_b_in, v_a_conv_w, v_a_conv_b, v_a_norm_g, v_a_norm_b, v_a_w_out, v_a_b_out, v_kv_w_down, v_kv_norm_g, v_kv_w_uk, v_kv_w_uv, v_b_w_in, v_b_q_norm_g, v_b_w_uq, v_b_w_out):
    T, D = x.shape[1], x.shape[2]
    E = N_CHIPS * a_w_out.shape[1]
    KC = a_conv_w.shape[1]
    RKV = kv_norm_g.shape[0]
    H, DN = kv_w_uk.shape[1], kv_w_uk.shape[2]
    RQ = b_q_norm_g.shape[1]
    HV = N_CHIPS * b_w_out.shape[1]
    assert DN == LANES and kv_w_uv.shape[2] == LANES and HV == H * LANES
    assert kv_w_down.shape[1] == RKV + ROPE_DIM and b_w_uq.shape[3] == DN + ROPE_DIM
    assert ln_g.shape[0] == 2 and a_w_in.shape[0] == 1 and b_w_in.shape[0] == 1
    alpha = (2.0 * ln_g.shape[0]) ** 0.25
    scale = 1.0 / math.sqrt(DN + ROPE_DIM)
    WK = -(-(RKV + LANES) // 256) * 256
    assert WK % RQ == 0
    Z_OFF = WK + RQ
    tmw, tq = min(TM_WIDE, T), min(TQ, T)
    t512, t1024 = _fit(512, T), _fit(1024, T)
    xs = x[0]
    tgt = loss_target[0]
    px, py = lax.axis_index("x"), lax.axis_index("y")
    chip = 2 * px + py

    mats = [a_w_out[0], kv_w_down, kv_w_uk, kv_w_uv, b_w_in[0], b_w_uq[0], b_w_out[0]]
    vecs = [a_b_in[0], a_conv_w[0], a_conv_b[0], a_norm_g[0], a_norm_b[0], a_b_out[0]]
    vec_bits = jnp.concatenate([lax.bitcast_convert_type(w.reshape(-1), BF16).reshape(-1) for w in vecs])
    rest = [w.astype(BF16) for w in mats]
    g_win, gathered = _all_gather_chips(
        [a_w_in[0].astype(BF16), _pack_rows(vec_bits, BF16, 2 * BF16_ROWS * LANES)])
    gathered = gathered.reshape(N_CHIPS, -1)
    gathered, rest = lax.optimization_barrier((gathered, rest))
    rest_sems = _push_start("gather_rest_start", rest, by_target=False)
    off = 0
    fvec = []
    for w in vecs:
        bits = gathered[:, off:off + 2 * w.size].reshape((N_CHIPS,) + w.shape + (2,))
        fvec.append(lax.bitcast_convert_type(bits, F32))
        off += 2 * w.size
    cols = lambda g: jnp.moveaxis(g, 0, -2).reshape(g.shape[1:-1] + (N_CHIPS * g.shape[-1],))
    b_in = cols(fvec[0][:, None, :])
    conv_w = cols(fvec[1])
    conv_b, norm_g, norm_b, b_out = (cols(f[:, None, :]) for f in fvec[2:])
    row = lambda a: a.reshape(1, -1)
    g0, b0, g1, b1 = row(ln_g[0]), row(ln_b[0]), row(ln_g[1]), row(ln_b[1])
    kv_g, q_g = row(kv_norm_g), row(b_q_norm_g[0])
    plain = lambda acc, ins, i, j: [acc]

    b_in = b_in + rest_sems[4][0, 0]
    (proj,) = _row_mm("a_in", [((xs,), None)], g_win, nt=False, tm=t1024, tn=3 * E // N_CHIPS, tk=D,
                      outs=[((T, 3 * E), F32, 'tile')], epi=lambda acc, ins, i, j: [acc + ins[0]],
                      epi_ins=[(b_in, 'col')])
    u1, u4 = _conv_fwd(proj, conv_w, conv_b, norm_g, norm_b, E, tmw)

    rest, landed = _push_wait("gather_rest_wait", *rest_sems[:4], after=[u4], by_target=False)
    g_wout, g_wd, g_uk, g_uv, g_wbin, g_wuq, g_wbout = [
        lax.dynamic_update_slice(l, w[None], (chip,) + (0,) * w.ndim) for w, l in zip(rest, landed)]
    w_out = g_wout.reshape(E, D)
    wd = g_wd.reshape(D, RKV + ROPE_DIM)
    zpad = jnp.zeros((D, ROPE_HALF), BF16)
    wd_p = jnp.concatenate(
        [wd[:, :RKV], wd[:, RKV:RKV + ROPE_HALF], zpad, wd[:, RKV + ROPE_HALF:], zpad,
         jnp.zeros((D, WK - RKV - LANES), BF16)], axis=1)
    w_bin = cols(g_wbin)
    w_z = w_bin[:, RQ:]
    wb_small = jnp.concatenate([wd_p, w_bin[:, :RQ]], axis=1)
    wb_all = jnp.concatenate([wd_p, w_bin], axis=1)
    w_kv = jnp.concatenate([g_uk.reshape(RKV, HV), g_uv.reshape(RKV, HV)], axis=1)
    wuq = g_wuq.reshape(RQ, H, DN + ROPE_DIM)
    zq = jnp.zeros((RQ, H, ROPE_HALF), BF16)
    w_qr = jnp.concatenate([wuq[:, :, DN:DN + ROPE_HALF], zq, wuq[:, :, DN + ROPE_HALF:], zq], axis=2)
    w_q = jnp.concatenate([wuq[:, :, :DN].reshape(RQ, HV), w_qr.reshape(RQ, HV)], axis=1)
    w_bout = g_wbout.reshape(HV, D)

    freqs = ROPE_THETA ** (-jnp.arange(0, ROPE_DIM, 2, dtype=F32) / ROPE_DIM)
    ang = positions[0].astype(F32)[:, None] * freqs
    cs, sn = jnp.cos(ang), jnp.sin(ang)
    ones, zeros = jnp.ones_like(cs), jnp.zeros_like(cs)
    cos_t = jnp.concatenate([cs, ones, cs, ones], axis=1)
    sin_t = jnp.concatenate([-sn, zeros, sn, zeros], axis=1)

    def ln_epi(acc, ins, i, j):
        bias, res, g, b = ins
        xhat, rstd = _ln_stats(alpha * res + acc + bias)
        h = xhat * g + b
        return [h, h, xhat, rstd]

    h1, h1b, xhat1, rstd1 = _row_mm(
        "a_out", [((u4,), None)], w_out, nt=False, tm=t512, tn=D, tk=_fit(2048, E),
        outs=[((T, D), F32, 'tile'), ((T, D), BF16, 'tile'), ((T, D), F32, 'tile'), ((T, 1), F32, 'row')],
        epi=ln_epi, epi_ins=[(b_out, 'col'), (xs, 'tile'), (g0, 'col'), (b0, 'col')])

    tkb = _fit(512, _gcd(WK, RQ, HV))
    def latents_epi(acc, ins, i, j):
        kg, qg, cos_, sin_ = ins
        return [acc, _rms_fwd(acc[:, :RKV], kg), _rope(acc[:, RKV:RKV + LANES], cos_, sin_),
                _rms_fwd(acc[:, WK:WK + RQ], qg)]

    whole_row = lambda a: (a, pl.BlockSpec(a.shape, lambda i, j, k: (0, 0)))
    pb, c_lat, kr, cqn = _row_mm(
        "b_in", [((h1b,), None)], wb_small, nt=False, tm=t512, tn=Z_OFF, tk=_fit(1024, D),
        outs=[((T, Z_OFF), F32, 'tile'), ((T, RKV), BF16, 'row'), ((T, LANES), BF16, 'row'),
              ((T, RQ), BF16, 'row')],
        epi=latents_epi, epi_ins=[whole_row(kv_g), whole_row(q_g), (cos_t, 'row'), (sin_t, 'row')])
    (zb,) = _row_mm("b_in_gate", [((h1b,), None)], w_z, nt=False, tm=t1024, tn=_fit(2048, HV),
                    tk=_fit(1024, D), outs=[((T, HV), BF16, 'tile')], epi=plain)
    (kv,) = _row_mm("kv_up", [((c_lat,), None)], w_kv, nt=False, tm=t1024, tn=_fit(2048, HV),
                    tk=_fit(1024, RKV), outs=[((T, 2 * HV), BF16, 'tile')], epi=plain)
    tnq = _fit(2048, HV)
    half_q = HV // tnq

    def q_epi(acc, ins, i, j):
        return [jnp.where(j >= half_q, _rope(acc, ins[0], ins[1]), acc)]

    (q_all,) = _row_mm("q_up", [((cqn,), None)], w_q, nt=False, tm=t1024, tn=tnq, tk=_fit(1024, RQ),
                       outs=[((T, 2 * HV), BF16, 'tile')], epi=q_epi,
                       epi_ins=[(cos_t, 'row'), (sin_t, 'row')])
    o, lse = _attn_fwd(q_all, kv, kr, H, tq, scale)

    def loss_epi(acc, ins, i, j):
        res, g, b, target = ins
        xhat, rstd = _ln_stats(alpha * res + acc)
        diff = xhat * g + b - target
        dr, dg, db = _ln_bwd(diff / D, xhat, rstd, g)
        return [dr, 0.5 * jnp.sum(diff * diff, keepdims=True) / D, dg, db]

    dr1, loss_part, dg1, db1 = _row_mm(
        "b_out", [((o, zb), _gate)], w_bout, nt=False, tm=t512, tn=D, tk=_fit(2048, HV),
        outs=[((T, D), F32, 'tile'), ((1, 1), F32, 'acc'), ((1, D), F32, 'acc'), ((1, D), F32, 'acc')],
        epi=loss_epi, epi_ins=[(h1, 'tile'), (g1, 'col'), (b1, 'col'), (tgt, 'tile')])

    def gate_bwd_epi(acc, ins, i, j):
        gate, gate_grad = _silu_and_grad(ins[1].astype(F32))
        return [acc * gate, acc * ins[0].astype(F32) * gate_grad]

    do, dz = _row_mm(
        "b_out_bwd", [((dr1,), None)], w_bout, nt=True, tm=t512, tn=_fit(2048, HV), tk=_fit(1024, D),
        outs=[((T, HV), BF16, 'tile'), ((T, HV), BF16, 'tile')], epi=gate_bwd_epi,
        epi_ins=[(o, 'tile'), (zb, 'tile')])
    gw_bout = _tn_mm("dw_b_out", (o, zb), _gate, [((dr1,), None)], tn=_fit(1024, D), tk=t512, out_dtype=BF16)
    dqn, dqr_pre, dkn, dkr_h, dv = _attn_bwd(q_all, kv, kr, do, o, lse, cos_t, sin_t, H, tq, scale)
    dkr_pre = _key_rope_bwd(dkr_h, cos_t, sin_t, H, tmw)

    def cq_bwd_epi(acc, ins, i, j):
        dx, dg = _rms_bwd(acc, ins[0], ins[1])
        return [dx, dg]

    dcq, dqg = _row_mm(
        "q_up_bwd", [((dqn,), None), ((dqr_pre,), None)], w_q, nt=True, tm=t1024, tn=RQ, tk=_fit(2048, HV),
        outs=[((T, RQ), BF16, 'tile'), ((1, RQ), F32, 'acc')], epi=cq_bwd_epi,
        epi_ins=[(pb, pl.BlockSpec((t1024, RQ), lambda i, j, k: (i, WK // RQ))), (q_g, 'col')])
    gw_q = _tn_mm("dw_q_up", (cqn,), None, [((dqn,), None), ((dqr_pre,), None)],
                  tn=_fit(2048, HV), tk=t1024, out_dtype=BF16)

    def ckv_bwd_epi(acc, ins, i, j):
        blk, dkr_t, g = ins
        dx, dg = _rms_bwd(acc, blk[:, :RKV], g)
        parts = [dx, dkr_t]
        if WK > RKV + LANES:
            parts.append(jnp.zeros((dx.shape[0], WK - RKV - LANES), F32))
        return [jnp.concatenate(parts, axis=1), dg]

    dckv, dkvg = _row_mm(
        "kv_up_bwd", [((dkn,), None), ((dv,), None)], w_kv, nt=True, tm=t1024, tn=RKV, tk=_fit(2048, HV),
        outs=[((T, WK), BF16, pl.BlockSpec((t1024, WK), lambda i, j, k: (i, 0))), ((1, RKV), F32, 'acc')],
        epi=ckv_bwd_epi,
        epi_ins=[(pb, pl.BlockSpec((t1024, WK), lambda i, j, k: (i, 0))), (dkr_pre, 'row'), (kv_g, 'col')])
    gw_kv = _tn_mm("dw_kv_up", (c_lat,), None, [((dkn,), None), ((dv,), None)],
                   tn=_fit(2048, HV), tk=t1024, out_dtype=BF16)

    def ln1_bwd_epi(acc, ins, i, j):
        dr_up, xhat, rstd, g = ins
        dr, dg, db = _ln_bwd(alpha * dr_up + acc, xhat, rstd, g)
        return [dr, dg, db]

    dp_segs = [((dckv,), None), ((dcq,), None), ((dz,), None)]
    gw_lat = _tn_mm("dw_b_in", (h1b,), None, dp_segs[:2], tn=tkb, tk=t1024, out_dtype=BF16)
    gw_z = _tn_mm("dw_b_in_gate", (h1b,), None, dp_segs[2:], tn=_fit(2048, HV), tk=t1024, out_dtype=BF16)

    shard_cols = lambda g: jnp.moveaxis(g.reshape(g.shape[0], N_CHIPS, -1), 1, 0)
    shard_rows = lambda g: g.reshape(N_CHIPS, g.shape[0] // N_CHIPS, g.shape[1])
    gq = gw_q.reshape(RQ, 2, H, LANES)
    g_uq = jnp.concatenate(
        [gq[:, 0], gq[:, 1, :, :ROPE_HALF], gq[:, 1, :, 2 * ROPE_HALF:3 * ROPE_HALF]], axis=2)
    g_wd_full = jnp.concatenate(
        [gw_lat[:, :RKV], gw_lat[:, RKV:RKV + ROPE_HALF],
         gw_lat[:, RKV + 2 * ROPE_HALF:RKV + 3 * ROPE_HALF]], axis=1)
    late_names = ["kv_w_down", "kv_w_uk", "kv_w_uv", "b_w_in", "b_w_uq", "b_w_out"]
    late_w = [kv_w_down, kv_w_uk, kv_w_uv, b_w_in, b_w_uq, b_w_out]
    gw_bin = jnp.concatenate([gw_lat[:, WK:], gw_z], axis=1)
    chip_major = [g_wd_full, gw_kv[:, :HV], gw_kv[:, HV:], shard_cols(gw_bin), g_uq, gw_bout]
    late_grads = [g.reshape((N_CHIPS,) + w.shape) for g, w in zip(chip_major, late_w)]
    late_sems = _push_start("scatter_late_start", late_grads, by_target=True)

    dr0, dg0, db0 = _row_mm(
        "b_in_bwd", dp_segs, wb_all, nt=True, tm=t1024, tn=D, tk=tkb,
        outs=[((T, D), F32, 'tile'), ((1, D), F32, 'acc'), ((1, D), F32, 'acc')], epi=ln1_bwd_epi,
        epi_ins=[(dr1, 'tile'), (xhat1, 'tile'), (rstd1, 'row'), (g0 + late_sems[4][0, 0], 'col')])

    def conv_branch_bwd_epi(acc, ins, i, j):
        u1_t, z, g, b = ins
        xhat, rstd = _ln_stats(u1_t)
        u2 = xhat * g + b
        gate, gate_grad = _silu_and_grad(z)
        act, act_grad = _silu_and_grad(u2)
        dz_a = acc * act * gate_grad
        du1, dg, db = _ln_bwd(acc * gate * act_grad, xhat, rstd, g)
        return [du1, dz_a, dg, db]

    du1, dz_a, dng, dnb = _row_mm(
        "a_out_bwd", [((dr0,), None)], w_out, nt=True, tm=tmw, tn=E, tk=_fit(1024, D),
        outs=[((T, E), F32, 'tile'), ((T, E), BF16, 'tile'), ((1, E), F32, 'acc'), ((1, E), F32, 'acc')],
        epi=conv_branch_bwd_epi,
        epi_ins=[(u1, 'tile'), (proj, pl.BlockSpec((tmw, E), lambda i, j, k: (i, 2))), (norm_g, 'col'),
                 (norm_b, 'col')])
    gw_out, dbo = _tn_mm("dw_a_out", (u4,), None, [((dr0,), None)], tn=_fit(1024, D), tk=t1024,
                         out_dtype=BF16, colsum=True)
    mid_sems = _push_start("scatter_mid_start", [gw_out.reshape((N_CHIPS,) + a_w_out.shape)], by_target=True)
    dval, dgate, dcw, dcb = _conv_bwd(du1, proj, conv_w + mid_sems[4][0, 0], E, tmw)
    dproj_segs = [((dval,), None), ((dgate,), None), ((dz_a,), None)]
    gw_in, dbi = _tn_mm("dw_a_in", (xs,), None, dproj_segs, tn=_fit(2048, E), tk=t1024, out_dtype=BF16,
                        colsum=True)

    chip_word = chip.reshape(1).astype(jnp.int32)

    def reduce_and_update(tag, names_, sent, landed, w_, m_, v_, by_target=True):
        sums = [_sum_chips("sum_" + n, l, s, chip_word, by_target) for n, s, l in zip(names_, sent, landed)]
        theirs = _swap_cores("swap_cores_" + tag, sums)
        return {n: _adamw("adamw_" + n, [mine, other], w, m, v)
                for n, mine, other, w, m, v in zip(names_, sums, theirs, w_, m_, v_)}

    late_sent, late_landed = _push_wait("scatter_late_wait", *late_sems[:4], after=[dbi], by_target=True)
    mid_sent, mid_landed = _push_wait("scatter_mid_wait", *mid_sems[:4], after=[dbi], by_target=True)
    early_sems = _push_start("scatter_early_start", [gw_in.reshape(a_w_in.shape[:-1] + (3 * E,))],
                             by_target='cols')
    (grad_x,) = _row_mm(
        "a_in_bwd", dproj_segs, g_win, nt=True, tm=t512, tn=D, tk=E, b_whole=True,
        outs=[((T, D), F32, 'tile')], epi=lambda acc, ins, i, j: [alpha * ins[0] + acc + ins[1]],
        epi_ins=[(dr0, 'tile'), (jnp.zeros((1, D), F32) + early_sems[4][0, 0], 'col')])
    big_out = reduce_and_update(
        "late", ["a_w_out"] + late_names, mid_sent + late_sent, mid_landed + late_landed,
        [a_w_out] + late_w, [m_a_w_out, m_kv_w_down, m_kv_w_uk, m_kv_w_uv, m_b_w_in, m_b_w_uq, m_b_w_out],
        [v_a_w_out, v_kv_w_down, v_kv_w_uk, v_kv_w_uv, v_b_w_in, v_b_w_uq, v_b_w_out])

    small_full = [jnp.concatenate([dg0, dg1]), jnp.concatenate([db0, db1]), dbi, dcw, dcb, dng, dnb, dbo,
                  dkvg, dqg, loss_part]
    sflat = jnp.concatenate([g.reshape(-1) for g in small_full])
    summed = _all_reduce_small(_pack_rows(sflat, F32, 8 * LANES)).reshape(-1)
    soff = 0
    sgrads = []
    for g in small_full:
        sgrads.append(summed[soff:soff + g.size].reshape(g.shape))
        soff += g.size
    loss = sgrads.pop()[0, 0]
    local_cols = lambda g, n: lax.dynamic_slice_in_dim(g, chip * n, n, axis=g.ndim - 1)
    snames = ["ln_g", "ln_b", "a_b_in", "a_conv_w", "a_conv_b", "a_norm_g", "a_norm_b", "a_b_out",
              "kv_norm_g", "b_q_norm_g"]
    small_w = [ln_g, ln_b, a_b_in, a_conv_w, a_conv_b, a_norm_g, a_norm_b, a_b_out, kv_norm_g, b_q_norm_g]
    small_m = [m_ln_g, m_ln_b, m_a_b_in, m_a_conv_w, m_a_conv_b, m_a_norm_g, m_a_norm_b, m_a_b_out,
               m_kv_norm_g, m_b_q_norm_g]
    small_v = [v_ln_g, v_ln_b, v_a_b_in, v_a_conv_w, v_a_conv_b, v_a_norm_g, v_a_norm_b, v_a_b_out,
               v_kv_norm_g, v_b_q_norm_g]
    sharded = {"a_b_in", "a_conv_w", "a_conv_b", "a_norm_g", "a_norm_b", "a_b_out"}
    local_g = [(local_cols(g, w.shape[-1]) if n in sharded else g).reshape(w.shape)
               for n, g, w in zip(snames, sgrads, small_w)]
    at_least_2d = lambda a: a.reshape((1,) + a.shape) if a.ndim == 1 else a
    sres = _adamw_vectors([tuple(at_least_2d(a) for a in item)
                           for item in zip(local_g, small_w, small_m, small_v)])
    small_out = {n: [g] + [r.reshape(w.shape) for r in res]
                 for n, g, w, res in zip(snames, local_g, small_w, sres)}

    early_sent, early_landed = _push_wait(
        "scatter_early_wait", *early_sems[:4], by_target='cols',
        after=[grad_x, big_out["b_w_out"][1], small_out["b_q_norm_g"][1]])
    big_out.update(reduce_and_update("early", ["a_w_in"], early_sent, early_landed,
                                     [a_w_in], [m_a_w_in], [v_a_w_in], by_target='cols'))

    order =["ln_g", "ln_b", "a_w_in", "a_b_in", "a_conv_w", "a_conv_b", "a_norm_g", "a_norm_b", "a_w_out",
             "a_b_out", "kv_w_down", "kv_norm_g", "kv_w_uk", "kv_w_uv", "b_w_in", "b_q_norm_g", "b_w_uq",
             "b_w_out"]
    outs = {**big_out, **small_out}
    result = [loss, grad_x[None]]
    for part in range(4):
        result += [outs[n][part] for n in order]
    return tuple(result)
```

```python
import functools
import math

import jax
import jax.numpy as jnp
from jax import lax
from jax.experimental import pallas as pl
from jax.experimental.pallas import tpu as pltpu

F32, BF16 = jnp.float32, jnp.bfloat16
NN = (((1,), (0,)), ((), ()))
NT = (((1,), (1,)), ((), ()))
TN = (((0,), (0,)), ((), ()))
MESH = pl.DeviceIdType.MESH
ANY = pl.BlockSpec(memory_space=pl.ANY)

LANES = 128
BF16_ROWS = 16
VMEM_LIMIT = 56 * 1024 * 1024
N_CHIPS = 4
LN_EPS = 1e-5
RMS_EPS = 1e-6
MASK_VALUE = -1e30
LOG2_E = math.log2(math.e)
ROPE_THETA = 10000.0
ROPE_DIM = 64
ROPE_HALF = ROPE_DIM // 2
ADAM_LR, ADAM_B1, ADAM_B2, ADAM_EPS, ADAM_WD, ADAM_STEP = 0.001, 0.9, 0.999, 1e-08, 0.01, 10

MAX_TILE = 2048
TM_WIDE = 256
TQ = 512
CONV_HALO = 32
CONV_LC = 1024
CONV_SUB = 256
SUBLANES = 8
CONV_RC = 32
ADAM_ROWS = 128


def _dot(a, b, dims):
    return lax.dot_general(a.astype(BF16), b.astype(BF16), dims, preferred_element_type=F32)


def _sig(x):
    return 0.5 * jnp.tanh(0.5 * x) + 0.5


def _params(n_axes):
    return pltpu.CompilerParams(dimension_semantics=("arbitrary",) * n_axes, vmem_limit_bytes=VMEM_LIMIT)


def _gcd(*v):
    return functools.reduce(math.gcd, v)


def _fit(want, dim):
    return math.gcd(min(want, MAX_TILE), dim)


def _row_mm(name, a_segs, b, *, nt, tm, tn, tk, outs, epi, epi_ins=(), b_whole=False):
    M = a_segs[0][0][0].shape[0]
    stacked = b.ndim == 3
    if stacked:
        n_blk = b.shape[2]
        N = b.shape[1] if nt else N_CHIPS * n_blk
        assert (nt and b_whole) or (not nt and tn == n_blk and tk == b.shape[1]), name
    else:
        N = b.shape[0] if nt else b.shape[1]
    nkb = [arrs[0].shape[1] // tk for arrs, _ in a_segs]
    koff = [sum(nkb[:s]) for s in range(len(nkb))]
    ni, nj, nk = M // tm, N // tn, sum(nkb)
    assert M % tm == 0 and N % tn == 0 and all(arrs[0].shape[1] % tk == 0 for arrs, _ in a_segs), name
    assert stacked or (b.shape[1] if nt else b.shape[0]) == nk * tk, name

    def spec_of(shape, kind):
        if isinstance(kind, pl.BlockSpec):
            return kind
        if kind == 'tile':
            return pl.BlockSpec((tm, tn), lambda i, j, k: (i, j))
        if kind == 'row':
            return pl.BlockSpec((tm, shape[1]), lambda i, j, k: (i, 0))
        if kind == 'col':
            return pl.BlockSpec((1, tn), lambda i, j, k: (0, j))
        assert kind == 'acc' and nj == 1, name
        return pl.BlockSpec(shape, lambda i, j, k: (0,) * len(shape))

    in_specs, operands = [], []
    for s, (arrs, _) in enumerate(a_segs):
        for arr in arrs:
            in_specs.append(pl.BlockSpec(
                (tm, tk), lambda i, j, k, s=s: (i, jnp.clip(k - koff[s], 0, nkb[s] - 1))))
            operands.append(arr)
    if b_whole:
        assert nt and nj == 1 and all(n == 1 for n in nkb), name
        in_specs.append(pl.BlockSpec(b.shape, lambda i, j, k: (0,) * b.ndim))
    elif stacked:
        in_specs.append(pl.BlockSpec((1, tk, tn), lambda i, j, k: (j, 0, 0)))
    else:
        in_specs.append(pl.BlockSpec((tn, tk), lambda i, j, k: (j, k)) if nt
                        else pl.BlockSpec((tk, tn), lambda i, j, k: (k, j)))
    operands.append(b)
    for arr, kind in epi_ins:
        in_specs.append(spec_of(arr.shape, kind))
        operands.append(arr)
    out_specs = [spec_of(shape, kind) for shape, _, kind in outs]
    out_shape = [jax.ShapeDtypeStruct(shape, dtype) for shape, dtype, _ in outs]
    n_seg_refs = [len(arrs) for arrs, _ in a_segs]

    def body(*refs):
        pos = 0
        seg_refs = []
        for n in n_seg_refs:
            seg_refs.append(refs[pos:pos + n])
            pos += n
        b_ref = refs[pos]
        e_refs = refs[pos + 1:pos + 1 + len(epi_ins)]
        o_refs = refs[pos + 1 + len(epi_ins):pos + 1 + len(epi_ins) + len(outs)]
        i, j, k = pl.program_id(0), pl.program_id(1), pl.program_id(2)

        def product(fn, rs, s=0):
            a = rs[0][...] if fn is None else fn(*[r[...] for r in rs])
            if b_whole and stacked:
                lo, hi, tot = koff[s] * tk, (koff[s] + 1) * tk, None
                for q in range(lo // n_blk, (hi - 1) // n_blk + 1):
                    c0, c1 = max(lo, q * n_blk), min(hi, (q + 1) * n_blk)
                    part = _dot(a[:, c0 - lo:c1 - lo], b_ref[q, :, c0 - q * n_blk:c1 - q * n_blk], NT)
                    tot = part if tot is None else tot + part
                return tot
            if b_whole:
                return _dot(a, b_ref[:, koff[s] * tk:(koff[s] + 1) * tk], NT)
            return _dot(a, b_ref[0] if stacked else b_ref[...], NT if nt else NN)

        def finish(acc):
            res = epi(acc, [r[...] for r in e_refs], i, j)
            for o_ref, (_, _, kind), r in zip(o_refs, outs, res):
                if isinstance(kind, str) and kind == 'acc':
                    @pl.when(i == 0)
                    def _(o_ref=o_ref, r=r):
                        o_ref[...] = r

                    @pl.when(i > 0)
                    def _(o_ref=o_ref, r=r):
                        o_ref[...] += r
                else:
                    o_ref[...] = r.astype(o_ref.dtype)

        if nk == 1:
            finish(product(a_segs[0][1], seg_refs[0]))
            return
        acc_ref = refs[-1]

        @pl.when(k == 0)
        def _():
            acc_ref[...] = jnp.zeros_like(acc_ref)

        for s, ((_, fn), rs) in enumerate(zip(a_segs, seg_refs)):
            def accumulate(fn=fn, rs=rs, s=s):
                acc_ref[...] += product(fn, rs, s)
            if len(a_segs) == 1:
                accumulate()
            else:
                pl.when(jnp.logical_and(k >= koff[s], k < koff[s] + nkb[s]))(accumulate)

        @pl.when(k == nk - 1)
        def _():
            finish(acc_ref[...])

    return pl.pallas_call(
        body, name=name, grid=(ni, nj, nk), in_specs=in_specs, out_specs=out_specs, out_shape=out_shape,
        scratch_shapes=[] if nk == 1 else [pltpu.VMEM((tm, tn), F32)], compiler_params=_params(3),
    )(*operands)


def _tn_mm(name, a_arrs, a_fn, b_segs, *, tn, tk, out_dtype, shard_major=False, colsum=False):
    T, M = a_arrs[0].shape
    nbj = [arrs[0].shape[1] // tn for arrs, _ in b_segs]
    joff = [sum(nbj[:s]) for s in range(len(nbj))]
    nj, nk = sum(nbj), T // tk
    N = nj * tn
    assert T % tk == 0 and all(arrs[0].shape[1] % tn == 0 for arrs, _ in b_segs), name

    in_specs = [pl.BlockSpec((tk, M), lambda j, k: (k, 0)) for _ in a_arrs]
    operands = list(a_arrs)
    for s, (arrs, _) in enumerate(b_segs):
        for arr in arrs:
            in_specs.append(pl.BlockSpec(
                (tk, tn), lambda j, k, s=s: (k, jnp.clip(j - joff[s], 0, nbj[s] - 1))))
            operands.append(arr)
    if shard_major:
        per = (N // N_CHIPS) // tn
        assert per * tn * N_CHIPS == N, name
        out_shape = [jax.ShapeDtypeStruct((N_CHIPS, M, N // N_CHIPS), out_dtype)]
        out_specs = [pl.BlockSpec((1, M, tn), lambda j, k: (j // per, 0, j % per))]
    else:
        out_shape = [jax.ShapeDtypeStruct((M, N), out_dtype)]
        out_specs = [pl.BlockSpec((M, tn), lambda j, k: (0, j))]
    if colsum:
        out_shape.append(jax.ShapeDtypeStruct((1, N), F32))
        out_specs.append(pl.BlockSpec((1, tn), lambda j, k: (0, j)))
    n_a = len(a_arrs)
    n_seg_refs = [len(arrs) for arrs, _ in b_segs]

    def body(*refs):
        a_refs = refs[:n_a]
        pos = n_a
        seg_refs = []
        for n in n_seg_refs:
            seg_refs.append(refs[pos:pos + n])
            pos += n
        o_ref = refs[pos]
        cs_ref = refs[pos + 1] if colsum else None
        acc_ref = refs[-1]
        j, k = pl.program_id(0), pl.program_id(1)

        @pl.when(k == 0)
        def _():
            acc_ref[...] = jnp.zeros_like(acc_ref)
            if colsum:
                cs_ref[...] = jnp.zeros_like(cs_ref)

        for s, ((_, fn), rs) in enumerate(zip(b_segs, seg_refs)):
            def accumulate(fn=fn, rs=rs):
                a = a_refs[0][...] if a_fn is None else a_fn(*[r[...] for r in a_refs])
                bt = rs[0][...] if fn is None else fn(*[r[...] for r in rs])
                acc_ref[...] += _dot(a, bt, TN)
                if colsum:
                    cs_ref[...] += jnp.sum(bt.astype(F32), axis=0, keepdims=True)
            if len(b_segs) == 1:
                accumulate()
            else:
                pl.when(jnp.logical_and(j >= joff[s], j < joff[s] + nbj[s]))(accumulate)

        @pl.when(k == nk - 1)
        def _():
            if shard_major:
                o_ref[0] = acc_ref[...].astype(o_ref.dtype)
            else:
                o_ref[...] = acc_ref[...].astype(o_ref.dtype)

    res = pl.pallas_call(
        body, name=name, grid=(nj, nk), in_specs=in_specs, out_specs=out_specs, out_shape=out_shape,
        scratch_shapes=[pltpu.VMEM((M, tn), F32)], compiler_params=_params(2),
    )(*operands)
    return res if colsum else res[0]


def _silu(z):
    return z * _sig(z)


def _silu_and_grad(z):
    s = _sig(z)
    return z * s, s * (1.0 + z * (1.0 - s))


def _gate(o, z):
    return o.astype(F32) * _silu(z.astype(F32))


def _ln_stats(r):
    mu = jnp.mean(r, axis=1, keepdims=True)
    xc = r - mu
    var = jnp.mean(xc * xc, axis=1, keepdims=True)
    rstd = lax.rsqrt(var + LN_EPS)
    return xc * rstd, rstd


def _ln_bwd(dy, xhat, rstd, g):
    dxh = dy * g
    m1 = jnp.mean(dxh, axis=1, keepdims=True)
    m2 = jnp.mean(dxh * xhat, axis=1, keepdims=True)
    return (rstd * (dxh - m1 - xhat * m2), jnp.sum(dy * xhat, axis=0, keepdims=True),
            jnp.sum(dy, axis=0, keepdims=True))


def _rms_fwd(x, g):
    rstd = lax.rsqrt(jnp.mean(x * x, axis=1, keepdims=True) + RMS_EPS)
    return x * rstd * g


def _rms_bwd(dy, x, g):
    rstd = lax.rsqrt(jnp.mean(x * x, axis=1, keepdims=True) + RMS_EPS)
    xn = x * rstd
    dxn = dy * g
    return rstd * (dxn - xn * jnp.mean(dxn * xn, axis=1, keepdims=True)), jnp.sum(dy * xn, axis=0, keepdims=True)


def _rope(x, cos, sin, transpose=False):
    parts = []
    for g in range(x.shape[1] // LANES):
        xg = x[:, g * LANES:(g + 1) * LANES]
        if transpose:
            parts.append(xg * cos + pltpu.roll(xg * sin, LANES // 2, 1))
        else:
            parts.append(xg * cos + pltpu.roll(xg, LANES // 2, 1) * sin)
    return parts[0] if len(parts) == 1 else jnp.concatenate(parts, axis=1)


def _shifted_rows(window, rc):
    n = window.shape[0]
    for b in range(SUBLANES):
        rolled = window if b == 0 else pltpu.roll(window, n - b, 0)
        for a8 in range(0, n - rc - b + 1, SUBLANES):
            yield a8 + b, rolled[a8:a8 + rc]


def _conv_fwd(proj, conv_w, conv_b, norm_g, norm_b, E, tm):
    T = proj.shape[0]
    kc = conv_w.shape[0]
    hb, rc = CONV_HALO, min(CONV_RC, tm)
    ni, ratio = T // tm, tm // hb
    sub = min(CONV_SUB, E)
    base = hb - (kc - 1)

    def body(val_ref, gate_ref, z_ref, valh_ref, gateh_ref, w_ref, cb_ref, g_ref, b_ref, u1_ref, u4_ref, ubuf):
        i = pl.program_id(0)
        ubuf[hb:, :] = val_ref[...] * _sig(gate_ref[...])
        halo = valh_ref[...] * _sig(gateh_ref[...])
        ubuf[0:hb, :] = jnp.where(i > 0, halo, 0.0)
        for l0 in range(0, E, sub):
            ls = slice(l0, l0 + sub)
            for r0 in range(0, tm, rc):
                acc = jnp.zeros((rc, sub), F32) + cb_ref[:, ls]
                for off, rows in _shifted_rows(ubuf[r0:r0 + hb + rc, ls], rc):
                    if 0 <= off - base < kc:
                        acc += w_ref[off - base:off - base + 1, ls] * rows
                u1_ref[r0:r0 + rc, ls] = acc
        xhat, _ = _ln_stats(u1_ref[...])
        u4_ref[...] = (_silu(xhat * g_ref[...] + b_ref[...]) * _silu(z_ref[...])).astype(BF16)

    main = lambda col: pl.BlockSpec((tm, E), lambda i: (i, col))
    halo = lambda col: pl.BlockSpec((hb, E), lambda i: (jnp.maximum(i * ratio - 1, 0), col))
    whole = lambda a: pl.BlockSpec(a.shape, lambda i: (0, 0))
    return pl.pallas_call(
        body, name="conv_fwd", grid=(ni,),
        in_specs=[main(0), main(1), main(2), halo(0), halo(1), whole(conv_w), whole(conv_b), whole(norm_g),
                  whole(norm_b)],
        out_specs=[main(0), main(0)],
        out_shape=[jax.ShapeDtypeStruct((T, E), F32), jax.ShapeDtypeStruct((T, E), BF16)],
        scratch_shapes=[pltpu.VMEM((hb + tm, E), F32)], compiler_params=_params(1),
    )(proj, proj, proj, proj, proj, conv_w, conv_b, norm_g, norm_b)


def _conv_bwd(du1, proj, conv_w, E, tm):
    T = du1.shape[0]
    kc = conv_w.shape[0]
    lc, hb, rc = min(CONV_LC, E), CONV_HALO, min(CONV_RC, tm)
    nl, ni, ratio = E // lc, T // tm, tm // hb
    gate_off = E // lc
    last_halo = T // hb - 1

    sub = min(CONV_SUB, lc)
    base = hb - (kc - 1)

    def body(du_ref, dun_ref, val_ref, gate_ref, valh_ref, gateh_ref, w_ref,
             dval_ref, dgate_ref, dw_ref, db_ref, ubuf, dbuf, sbuf, dw_sc):
        i = pl.program_id(1)
        sbuf[...] = _sig(gate_ref[...])
        ubuf[hb:, :] = val_ref[...] * sbuf[...]
        halo = valh_ref[...] * _sig(gateh_ref[...])
        ubuf[0:hb, :] = jnp.where(i > 0, halo, 0.0)
        dbuf[0:tm, :] = du_ref[...]
        dbuf[tm:, :] = jnp.where(i < ni - 1, dun_ref[...], 0.0)

        @pl.when(i == 0)
        def _():
            dw_sc[...] = jnp.zeros_like(dw_sc)
            db_ref[...] = jnp.zeros_like(db_ref)

        db_ref[...] += jnp.sum(du_ref[...], axis=0, keepdims=True)
        for l0 in range(0, lc, sub):
            ls = slice(l0, l0 + sub)
            for r0 in range(0, tm, rc):
                dwin = dbuf[r0:r0 + rc + hb, ls]
                dchunk = dwin[0:rc]
                for off, rows in _shifted_rows(ubuf[r0:r0 + hb + rc, ls], rc):
                    k = off - base
                    if 0 <= k < kc:
                        prod = rows * dchunk
                        part = prod[0:SUBLANES]
                        for s8 in range(SUBLANES, rc, SUBLANES):
                            part = part + prod[s8:s8 + SUBLANES]
                        dw_sc[k, :, ls] += part
                acc = jnp.zeros((rc, sub), F32)
                for off, rows in _shifted_rows(dwin, rc):
                    k = (kc - 1) - off
                    if 0 <= k < kc:
                        acc += w_ref[k:k + 1, ls] * rows
                v, s = val_ref[r0:r0 + rc, ls], sbuf[r0:r0 + rc, ls]
                dval_ref[r0:r0 + rc, ls] = (acc * s).astype(BF16)
                dgate_ref[r0:r0 + rc, ls] = (acc * v * s * (1.0 - s)).astype(BF16)

        @pl.when(i == ni - 1)
        def _():
            for k in range(kc):
                dw_ref[k:k + 1, :] = jnp.sum(dw_sc[k], axis=0, keepdims=True)

    return pl.pallas_call(
        body, name="conv_bwd", grid=(nl, ni),
        in_specs=[
            pl.BlockSpec((tm, lc), lambda l, i: (i, l)),
            pl.BlockSpec((hb, lc), lambda l, i: (jnp.minimum((i + 1) * ratio, last_halo), l)),
            pl.BlockSpec((tm, lc), lambda l, i: (i, l)),
            pl.BlockSpec((tm, lc), lambda l, i: (i, gate_off + l)),
            pl.BlockSpec((hb, lc), lambda l, i: (jnp.maximum(i * ratio - 1, 0), l)),
            pl.BlockSpec((hb, lc), lambda l, i: (jnp.maximum(i * ratio - 1, 0), gate_off + l)),
            pl.BlockSpec((kc, lc), lambda l, i: (0, l)),
        ],
        out_specs=[pl.BlockSpec((tm, lc), lambda l, i: (i, l)), pl.BlockSpec((tm, lc), lambda l, i: (i, l)),
                   pl.BlockSpec((kc, lc), lambda l, i: (0, l)), pl.BlockSpec((1, lc), lambda l, i: (0, l))],
        out_shape=[jax.ShapeDtypeStruct((T, E), BF16), jax.ShapeDtypeStruct((T, E), BF16),
                   jax.ShapeDtypeStruct((kc, E), F32), jax.ShapeDtypeStruct((1, E), F32)],
        scratch_shapes=[pltpu.VMEM((hb + tm, lc), F32), pltpu.VMEM((tm + hb, lc), F32),
                        pltpu.VMEM((tm, lc), F32), pltpu.VMEM((kc, SUBLANES, lc), F32)],
        compiler_params=_params(2),
    )(du1, du1, proj, proj, proj, proj, conv_w)


def _attn_fwd(q_all, kv, kr, H, tq, scale):
    T = q_all.shape[0]
    nq = T // tq
    pair = 2
    W = pair * LANES
    assert H % pair == 0
    hp_n = H // pair

    def body(qn_ref, qr_ref, kn_ref, kr_ref, v_ref, o_ref, lse_ref, *scratch):
        qi = pl.program_id(1)
        chains = [scratch[4 * a:4 * a + 4] for a in range(pair)]
        lanes = [slice(a * LANES, (a + 1) * LANES) for a in range(pair)]
        groups = [slice(c * LANES, (c + 1) * LANES) for c in range(tq // LANES)]

        def fold(x, op):
            r = x[:, groups[0]]
            for gsl in groups[1:]:
                r = op(r, x[:, gsl])
            return r

        for _, m_sc, l_sc, acc_sc in chains:
            m_sc[...] = jnp.full_like(m_sc, MASK_VALUE)
            l_sc[...] = jnp.zeros_like(l_sc)
            acc_sc[...] = jnp.zeros_like(acc_sc)

        def scores(j, masked):
            rows = pl.ds(pl.multiple_of(j * tq, tq), tq)
            krope = kr_ref[rows, :]
            for a, (s_sc, m_sc, _, _) in enumerate(chains):
                q = jnp.concatenate([qn_ref[:, lanes[a]], qr_ref[:, lanes[a]]], axis=1)
                k = jnp.concatenate([kn_ref[rows, lanes[a]], krope], axis=1)
                s = _dot(q, k, NT) * (scale * LOG2_E)
                if masked:
                    row = lax.broadcasted_iota(jnp.int32, s.shape, 0)
                    col = lax.broadcasted_iota(jnp.int32, s.shape, 1)
                    s = jnp.where(col <= row, s, MASK_VALUE)
                s_sc[j] = s
                m_sc[...] = jnp.maximum(m_sc[...], fold(s, jnp.maximum))

        def two_per_trip(fn, count):
            def two(p, carry):
                fn(2 * p)
                fn(2 * p + 1)
                return carry

            lax.fori_loop(0, count // 2, two, 0)

            @pl.when(count % 2 == 1)
            def _():
                fn(count - 1)

        two_per_trip(functools.partial(scores, masked=False), qi)
        scores(qi, True)
        for _, m_sc, _, _ in chains:
            m_sc[...] = jnp.broadcast_to(jnp.max(m_sc[...], axis=1, keepdims=True), m_sc.shape)

        def weigh(j):
            rows = pl.ds(pl.multiple_of(j * tq, tq), tq)
            for a, (s_sc, m_sc, l_sc, acc_sc) in enumerate(chains):
                s, m = s_sc[j], m_sc[...]
                p = jnp.concatenate([jnp.exp2(s[:, gsl] - m) for gsl in groups], axis=1)
                l_sc[...] += fold(p, jnp.add)
                acc_sc[...] += _dot(p, v_ref[rows, lanes[a]], NN)

        two_per_trip(weigh, qi + 1)
        for a, (_, m_sc, l_sc, acc_sc) in enumerate(chains):
            l = jnp.sum(l_sc[...], axis=1, keepdims=True)
            o_ref[:, lanes[a]] = (acc_sc[...] / l).astype(BF16)
            lse_ref[a] = m_sc[:, 0:1] * (1.0 / LOG2_E) + jnp.log(l)

    chain_scratch = [pltpu.VMEM((nq, tq, tq), F32), pltpu.VMEM((tq, LANES), F32), pltpu.VMEM((tq, LANES), F32),
                     pltpu.VMEM((tq, LANES), F32)]
    return pl.pallas_call(
        body, name="attn_fwd", grid=(hp_n, nq),
        in_specs=[pl.BlockSpec((tq, W), lambda hp, qi: (qi, hp)),
                  pl.BlockSpec((tq, W), lambda hp, qi: (qi, hp_n + hp)),
                  pl.BlockSpec((T, W), lambda hp, qi: (0, hp)),
                  pl.BlockSpec((T, LANES), lambda hp, qi: (0, 0)),
                  pl.BlockSpec((T, W), lambda hp, qi: (0, hp_n + hp))],
        out_specs=[pl.BlockSpec((tq, W), lambda hp, qi: (qi, hp)),
                   pl.BlockSpec((pair, tq, 1), lambda hp, qi: (hp, qi, 0))],
        out_shape=[jax.ShapeDtypeStruct((T, H * LANES), BF16), jax.ShapeDtypeStruct((H, T, 1), F32)],
        scratch_shapes=chain_scratch * pair, compiler_params=_params(2),
    )(q_all, q_all, kv, kr, kv)


def _attn_bwd(q_all, kv, kr, do, o, lse, cos, sin, H, tq, scale):
    T = q_all.shape[0]
    nq = T // tq
    HV = H * LANES
    pair = 2
    tk2 = pair * tq
    ng = T // tk2
    assert ng * tk2 == T and pair == 2

    def body(qn_ref, qr_ref, kn_ref, kr_ref, v_ref, do_ref, o_ref, lse_ref, cos_ref, sin_ref,
             dqn_ref, dqr_ref, dkn_ref, dkr_ref, dv_ref, dq_sc, dk_sc, dv_sc):
        g = pl.program_id(1)

        @pl.when(g == 0)
        def _():
            dq_sc[...] = jnp.zeros_like(dq_sc)

        key_rows = [slice(c * tq, (c + 1) * tq) for c in range(pair)]

        def block(qi, modes):
            rows = pl.ds(pl.multiple_of(qi * tq, tq), tq)
            q = jnp.concatenate([qn_ref[rows, :], qr_ref[rows, :]], axis=1)
            dov = do_ref[rows, :]
            delta = jnp.sum(dov.astype(F32) * o_ref[rows, :].astype(F32), axis=1, keepdims=True)
            lse_q = lse_ref[0, rows, :]
            dq, dkv = None, []
            for kr_, masked in zip(key_rows, modes):
                if masked is None:
                    dkv.append(None)
                    continue
                k = jnp.concatenate([kn_ref[kr_, :], kr_ref[kr_, :]], axis=1)
                s = _dot(q, k, NT) * scale
                if masked:
                    row = lax.broadcasted_iota(jnp.int32, s.shape, 0)
                    col = lax.broadcasted_iota(jnp.int32, s.shape, 1)
                    s = jnp.where(col <= row, s, MASK_VALUE)
                p = jnp.exp(s - lse_q)
                dv = _dot(p, dov, TN)
                dp = _dot(dov, v_ref[kr_, :], NT)
                ds = (p * (dp - delta) * scale).astype(BF16)
                dkv.append((_dot(ds, q, TN), dv))
                part = _dot(ds, k, NN)
                dq = part if dq is None else dq + part
            return rows, dq, dkv

        rows_a, dq_a, (kv_a0, _) = block(pair * g, (True, None))
        rows_b, dq_b, (kv_b0, kv_b1) = block(pair * g + 1, (False, True))
        dk_sc[key_rows[0], :] = kv_a0[0] + kv_b0[0]
        dv_sc[key_rows[0], :] = kv_a0[1] + kv_b0[1]
        dk_sc[key_rows[1], :] = kv_b1[0]
        dv_sc[key_rows[1], :] = kv_b1[1]
        dq_sc[rows_a, :] += dq_a
        dq_sc[rows_b, :] += dq_b

        def below(trip, carry):
            for qi in (pair * g + pair + 2 * trip, pair * g + pair + 2 * trip + 1):
                rows, dq, dkv = block(qi, (False, False))
                for kr_, (dk, dv) in zip(key_rows, dkv):
                    dk_sc[kr_, :] += dk
                    dv_sc[kr_, :] += dv
                dq_sc[rows, :] += dq
            return carry

        lax.fori_loop(0, (nq - pair * g - pair) // 2, below, 0)
        dkn_ref[...] = dk_sc[:, :LANES].astype(BF16)
        dkr_ref[...] = dk_sc[:, LANES:].astype(BF16)
        dv_ref[...] = dv_sc[...].astype(BF16)

        @pl.when(g == ng - 1)
        def _():
            dqn_ref[...] = dq_sc[:, :LANES].astype(BF16)
            dqr_ref[...] = _rope(dq_sc[:, LANES:], cos_ref[...], sin_ref[...], transpose=True).astype(BF16)

    whole = lambda col: pl.BlockSpec((T, LANES), col)
    tile = lambda col: pl.BlockSpec((tk2, LANES), col)
    return pl.pallas_call(
        body, name="attn_bwd", grid=(H, ng),
        in_specs=[whole(lambda h, g: (0, h)), whole(lambda h, g: (0, H + h)),
                  tile(lambda h, g: (g, h)), tile(lambda h, g: (g, 0)), tile(lambda h, g: (g, H + h)),
                  whole(lambda h, g: (0, h)), whole(lambda h, g: (0, h)),
                  pl.BlockSpec((1, T, 1), lambda h, g: (h, 0, 0)),
                  whole(lambda h, g: (0, 0)), whole(lambda h, g: (0, 0))],
        out_specs=[whole(lambda h, g: (0, h)), whole(lambda h, g: (0, h)),
                   tile(lambda h, g: (g, h)), tile(lambda h, g: (g, h)), tile(lambda h, g: (g, h))],
        out_shape=[jax.ShapeDtypeStruct((T, HV), BF16), jax.ShapeDtypeStruct((T, HV), BF16),
                   jax.ShapeDtypeStruct((T, HV), BF16), jax.ShapeDtypeStruct((T, HV), BF16),
                   jax.ShapeDtypeStruct((T, HV), BF16)],
        scratch_shapes=[pltpu.VMEM((T, 2 * LANES), F32), pltpu.VMEM((tk2, 2 * LANES), F32),
                        pltpu.VMEM((tk2, LANES), F32)],
        compiler_params=_params(2),
    )(q_all, q_all, kv, kr, kv, do, o, lse, cos, sin)


def _key_rope_bwd(dkr_heads, cos, sin, H, tm):
    T, HV = dkr_heads.shape

    def body(dkr_ref, cos_ref, sin_ref, dk_ref):
        dk = dkr_ref[...].astype(F32)
        tot = dk[:, 0:LANES]
        for h in range(1, H):
            tot = tot + dk[:, h * LANES:(h + 1) * LANES]
        dk_ref[...] = _rope(tot, cos_ref[...], sin_ref[...], transpose=True)

    return pl.pallas_call(
        body, name="key_rope_bwd", grid=(T // tm,),
        in_specs=[pl.BlockSpec((tm, HV), lambda i: (i, 0)),
                  pl.BlockSpec((tm, LANES), lambda i: (i, 0)), pl.BlockSpec((tm, LANES), lambda i: (i, 0))],
        out_specs=pl.BlockSpec((tm, LANES), lambda i: (i, 0)),
        out_shape=jax.ShapeDtypeStruct((T, LANES), F32), compiler_params=_params(1),
    )(dkr_heads, cos, sin)


def _place():
    x, y, c = lax.axis_index("x"), lax.axis_index("y"), lax.axis_index("c")
    chips = [(1 - x, y), (x, 1 - y), (1 - x, 1 - y)]
    return x, y, c, chips


def _all_gather_chips(arrs):
    n = len(arrs)
    halves = [a.shape[0] // 2 for a in arrs]
    assert all(h * 2 == a.shape[0] and h % BF16_ROWS == 0 for h, a in zip(halves, arrs))

    def body(*refs):
        w_refs, out_refs = refs[:n], refs[n:2 * n]
        send_sems, recv_sems, local_sems = refs[2 * n:]
        x, y, c, chips = _place()
        sibling = (x, y, 1 - c)
        waits = []
        for w, (w_ref, out_ref, half) in enumerate(zip(w_refs, out_refs, halves)):
            def region(px, py, pc, out_ref=out_ref, half=half):
                return out_ref.at[2 * px + py, pl.ds(pc * half, half), :]

            def copy(k, block, to, src=None, w=w, region=region):
                return pltpu.make_async_remote_copy(
                    src_ref=region(*block) if src is None else src, dst_ref=region(*block),
                    send_sem=send_sems.at[6 * w + k], recv_sem=recv_sems.at[6 * w + k],
                    device_id=to, device_id_type=MESH)

            mine = pltpu.make_async_copy(w_ref, out_ref.at[2 * x + y], local_sems.at[w])
            mine.start()
            my_half = w_ref.at[pl.ds(c * half, half), :]
            first = [copy(j, (x, y, c), (*chip, c), src=my_half) for j, chip in enumerate(chips)]
            for cp in first:
                cp.start()
            waits.append((copy, mine, first))
        for copy, mine, first in waits:
            passed = [copy(3 + j, (*chip, c), sibling) for j, chip in enumerate(chips)]
            for j, chip in enumerate(chips):
                copy(j, (*chip, c), (x, y, c)).wait_recv()
                passed[j].start()
            for j, chip in enumerate(chips):
                copy(3 + j, (*chip, 1 - c), (x, y, c)).wait_recv()
            for cp in first + passed:
                cp.wait_send()
            mine.wait()

    return pl.pallas_call(
        body, name="gather_weights", in_specs=[ANY] * n, out_specs=[ANY] * n,
        out_shape=[jax.ShapeDtypeStruct((N_CHIPS,) + a.shape, a.dtype) for a in arrs],
        scratch_shapes=[pltpu.SemaphoreType.DMA((6 * n,)), pltpu.SemaphoreType.DMA((6 * n,)),
                        pltpu.SemaphoreType.DMA((n,))],
    )(*arrs)


def _swap_cores(name, parts):
    n = len(parts)

    def body(*refs):
        p_refs, r_refs = refs[:n], refs[n:2 * n]
        send_sems, recv_sems = refs[2 * n:]
        x, y, c, _ = _place()
        copies = [pltpu.make_async_remote_copy(
            src_ref=p_refs[w], dst_ref=r_refs[w], send_sem=send_sems.at[w], recv_sem=recv_sems.at[w],
            device_id=(x, y, 1 - c), device_id_type=MESH) for w in range(n)]
        for cp in copies:
            cp.start()
        for cp in copies:
            cp.wait()

    return pl.pallas_call(
        body, name=name, in_specs=[ANY] * n, out_specs=[ANY] * n,
        out_shape=[jax.ShapeDtypeStruct(p.shape, p.dtype) for p in parts],
        scratch_shapes=[pltpu.SemaphoreType.DMA((n,)), pltpu.SemaphoreType.DMA((n,))],
    )(*parts)


HBM = pl.BlockSpec(memory_space=pltpu.HBM)
SEM = pl.BlockSpec(memory_space=pltpu.SEMAPHORE)
EFFECT = pltpu.SideEffectType.DATAFLOW_SIDE_EFFECTING


def _push_copies(a_refs, l_refs, send_sems, recv_sems, by_target):
    x, y, c, chips = _place()
    me = 2 * x + y

    def part(a_ref, q):
        if by_target == 'cols':
            n = a_ref.shape[-1] // N_CHIPS
            return a_ref.at[(slice(None),) * (len(a_ref.shape) - 1) + (pl.ds(pl.multiple_of(q * n, LANES), n),)]
        return a_ref.at[q] if by_target else a_ref

    out = []
    for w, (a_ref, l_ref) in enumerate(zip(a_refs, l_refs)):
        for j, (px, py) in enumerate(chips):
            peer = 2 * px + py
            out.append((
                pltpu.make_async_remote_copy(
                    src_ref=part(a_ref, peer), dst_ref=l_ref.at[me],
                    send_sem=send_sems.at[3 * w + j], recv_sem=recv_sems.at[3 * w + j],
                    device_id=(px, py, c), device_id_type=MESH),
                pltpu.make_async_remote_copy(
                    src_ref=part(a_ref, me), dst_ref=l_ref.at[peer],
                    send_sem=send_sems.at[3 * w + j], recv_sem=recv_sems.at[3 * w + j],
                    device_id=(px, py, c), device_id_type=MESH)))
    return out


def _landing_shape(a, by_target):
    if by_target == 'cols':
        return (N_CHIPS,) + a.shape[:-1] + (a.shape[-1] // N_CHIPS,)
    return (N_CHIPS,) + (a.shape[1:] if by_target else a.shape)


def _push_start(name, arrs, by_target):
    n = len(arrs)
    lands = [lax.empty(_landing_shape(a, by_target), a.dtype) for a in arrs]

    def body(*refs):
        a_refs, l_refs = refs[:n], refs[n:2 * n]
        send_sems, recv_sems = refs[2 * n], refs[2 * n + 1]
        token = refs[-1]
        for send, _ in _push_copies(a_refs, l_refs, send_sems, recv_sems, by_target):
            send.start()
        token[...] = jnp.zeros_like(token)

    res = pl.pallas_call(
        body, name=name,
        out_shape=(pltpu.SemaphoreType.DMA((3 * n,)), pltpu.SemaphoreType.DMA((3 * n,)),
                   *[pltpu.HBM(a.shape, a.dtype) for a in arrs], *[pltpu.HBM(l.shape, l.dtype) for l in lands],
                   jax.ShapeDtypeStruct((8, LANES), F32)),
        in_specs=[HBM] * (2 * n), out_specs=(SEM, SEM, *[HBM] * (2 * n), pl.BlockSpec(memory_space=pltpu.VMEM)),
        input_output_aliases={i: 2 + i for i in range(2 * n)},
        compiler_params=pltpu.CompilerParams(has_side_effects=EFFECT),
    )(*[pltpu.with_memory_space_constraint(a, pltpu.HBM) for a in list(arrs) + lands])
    return res[0], res[1], list(res[2:2 + n]), list(res[2 + n:2 + 2 * n]), res[-1]


def _push_wait(name, send_sems, recv_sems, arrs, lands, after, by_target):
    n = len(arrs)

    def body(*refs):
        a_refs, l_refs = refs[:n], refs[n:2 * n]
        s_sems, r_sems = refs[2 * n], refs[2 * n + 1]
        for send, recv in _push_copies(a_refs, l_refs, s_sems, r_sems, by_target):
            send.wait_send()
            recv.wait_recv()

    res = pl.pallas_call(
        body, name=name,
        out_shape=[pltpu.HBM(a.shape, a.dtype) for a in list(arrs) + list(lands)],
        in_specs=[HBM] * (2 * n) + [SEM, SEM] + [ANY] * len(after), out_specs=[HBM] * (2 * n),
        input_output_aliases={i: i for i in range(2 * n)},
        compiler_params=pltpu.CompilerParams(has_side_effects=EFFECT),
    )(*arrs, *lands, send_sems, recv_sems, *after)
    return list(res[:n]), list(res[n:])


def _all_reduce_small(part):
    def body(p_ref, out_ref, sib_buf, chip_buf, send_sems, recv_sems):
        x, y, c, chips = _place()
        me = 2 * x + y
        swap = pltpu.make_async_remote_copy(
            src_ref=p_ref, dst_ref=sib_buf, send_sem=send_sems.at[0], recv_sem=recv_sems.at[0],
            device_id=(x, y, 1 - c), device_id_type=MESH)
        swap.start()
        swap.wait()
        chip_buf[me] = p_ref[...] + sib_buf[...]
        copies = []
        for j, (px, py) in enumerate(chips):
            cp = pltpu.make_async_remote_copy(
                src_ref=chip_buf.at[me], dst_ref=chip_buf.at[me], send_sem=send_sems.at[1 + j],
                recv_sem=recv_sems.at[1 + j], device_id=(px, py, c), device_id_type=MESH)
            cp.start()
            copies.append(cp)
        for j, (px, py) in enumerate(chips):
            pltpu.make_async_remote_copy(
                src_ref=chip_buf.at[me], dst_ref=chip_buf.at[2 * px + py], send_sem=send_sems.at[1 + j],
                recv_sem=recv_sems.at[1 + j], device_id=(px, py, c), device_id_type=MESH).wait_recv()
        for cp in copies:
            cp.wait_send()
        tot = chip_buf[0]
        for q in range(1, N_CHIPS):
            tot = tot + chip_buf[q]
        out_ref[...] = tot

    vm = pl.BlockSpec(memory_space=pltpu.VMEM)
    return pl.pallas_call(
        body, name="all_reduce_small", in_specs=[vm], out_specs=vm,
        out_shape=jax.ShapeDtypeStruct(part.shape, F32),
        scratch_shapes=[pltpu.VMEM(part.shape, F32), pltpu.VMEM((N_CHIPS,) + part.shape, F32),
                        pltpu.SemaphoreType.DMA((N_CHIPS,)), pltpu.SemaphoreType.DMA((N_CHIPS,))],
    )(part)


def _row_tiles(shape):
    ax = next(d for d, s in enumerate(shape) if s > 1)
    tr = _gcd(ADAM_ROWS, shape[ax])
    block = tuple(tr if d == ax else s for d, s in enumerate(shape))
    return shape[ax] // tr, block, lambda i: tuple(i if d == ax else 0 for d in range(len(shape)))


def _sum_chips(name, landed, sent, chip, by_target):
    shape = landed.shape[1:]
    steps, block, index = _row_tiles(shape)
    if by_target == 'cols':
        own_spec = pl.BlockSpec(block, lambda i, c: index(i)[:-1] + (c[0],))
    else:
        own_spec = pl.BlockSpec((1,) + block, lambda i, c: (c[0],) + index(i))

    def body(chip_ref, l_ref, s_ref, o_ref):
        own = (s_ref[...] if by_target == 'cols' else s_ref[0]).astype(F32)
        tot = None
        for q in range(N_CHIPS):
            term = jnp.where(chip_ref[0] == q, own, l_ref[q].astype(F32))
            tot = term if tot is None else tot + term
        o_ref[...] = tot

    return pl.pallas_call(
        body, name=name,
        grid_spec=pltpu.PrefetchScalarGridSpec(
            num_scalar_prefetch=1, grid=(steps,),
            in_specs=[pl.BlockSpec((N_CHIPS,) + block, lambda i, c: (0,) + index(i)), own_spec],
            out_specs=pl.BlockSpec(block, lambda i, c: index(i))),
        out_shape=jax.ShapeDtypeStruct(shape, F32), compiler_params=_params(1),
    )(chip, landed, sent)


def _adamw_math(g, w, m, v):
    mn = ADAM_B1 * m + (1.0 - ADAM_B1) * g
    vn = ADAM_B2 * v + (1.0 - ADAM_B2) * jnp.square(g)
    m_hat = mn / (1.0 - ADAM_B1 ** ADAM_STEP)
    v_hat = vn / (1.0 - ADAM_B2 ** ADAM_STEP)
    return -ADAM_LR * (m_hat / (jnp.sqrt(v_hat) + ADAM_EPS) + ADAM_WD * w), mn, vn


def _adamw(name, g_parts, w, m, v):
    steps, block, index = _row_tiles(w.shape)
    n = len(g_parts)

    def body(*refs):
        g = refs[0][...]
        for r in refs[1:n]:
            g = g + r[...]
        w_ref, m_ref, v_ref, go_ref, d_ref, mo_ref, vo_ref = refs[n:]
        go_ref[...] = g
        d_ref[...], mo_ref[...], vo_ref[...] = _adamw_math(g, w_ref[...], m_ref[...], v_ref[...])

    spec = pl.BlockSpec(block, index)
    return pl.pallas_call(
        body, name=name, grid=(steps,), in_specs=[spec] * (n + 3), out_specs=[spec] * 4,
        out_shape=[jax.ShapeDtypeStruct(w.shape, F32)] * 4, compiler_params=_params(1),
    )(*g_parts, w, m, v)


def _adamw_vectors(items):
    n = len(items)

    def body(*refs):
        ins, outs = refs[:4 * n], refs[4 * n:]
        for k in range(n):
            g, w, m, v = (r[...] for r in ins[4 * k:4 * k + 4])
            outs[3 * k][...], outs[3 * k + 1][...], outs[3 * k + 2][...] = _adamw_math(g, w, m, v)

    vm = pl.BlockSpec(memory_space=pltpu.VMEM)
    res = pl.pallas_call(
        body, name="adamw_vectors", in_specs=[vm] * (4 * n), out_specs=[vm] * (3 * n),
        out_shape=[jax.ShapeDtypeStruct(it[1].shape, F32) for it in items for _ in range(3)],
    )(*[a for it in items for a in it])
    return [res[3 * k:3 * k + 3] for k in range(n)]


def _pack_rows(flat, dtype, multiple):
    n = flat.shape[0]
    total = -(-n // multiple) * multiple
    return jnp.pad(flat, (0, total - n)).astype(dtype).reshape(total // LANES, LANES)


def kernel(x, positions, ln_g, ln_b, a_w_in, a_b_in, a_conv_w, a_conv_b, a_norm_g, a_norm_b, a_w_out, a_b_out, kv_w_down, kv_norm_g, kv_w_uk, kv_w_uv, b_w_in, b_q_norm_g, b_w_uq, b_w_out, loss_target, m_ln_g, m_ln_b, m_a_w_in, m_a_b_in, m_a_conv_w, m_a_conv_b, m_a_norm_g, m_a_norm_b, m_a_w_out, m_a_b_out, m_kv_w_down, m_kv_norm_g, m_kv_w_uk, m_kv_w_uv, m_b_w_in, m_b_q_norm_g, m_b_w_uq, m_b_w_out, v_ln_g, v_ln_b, v_a_w_in, v_a_b_in, v_a_conv_w, v_a_conv_b, v_a_norm_g, v_a_norm_b, v_a_w_out, v_a_b_out, v_kv_w_down, v_kv_norm_g, v_kv_w_uk, v_kv_w_uv, v_b_w_in, v_b_q_norm_g, v_b_w_uq, v_b_w_out):
    T, D = x.shape[1], x.shape[2]
    E = N_CHIPS * a_w_out.shape[1]
    RKV = kv_norm_g.shape[0]
    H, DN = kv_w_uk.shape[1], kv_w_uk.shape[2]
    RQ = b_q_norm_g.shape[1]
    HV = N_CHIPS * b_w_out.shape[1]
    assert DN == LANES and kv_w_uv.shape[2] == LANES and HV == H * LANES
    assert kv_w_down.shape[1] == RKV + ROPE_DIM and b_w_uq.shape[3] == DN + ROPE_DIM
    assert ln_g.shape[0] == 2 and a_w_in.shape[0] == 1 and b_w_in.shape[0] == 1
    alpha = (2.0 * ln_g.shape[0]) ** 0.25
    scale = 1.0 / math.sqrt(DN + ROPE_DIM)
    WK = -(-(RKV + LANES) // 256) * 256
    assert WK % RQ == 0
    Z_OFF = WK + RQ
    tmw, tq = min(TM_WIDE, T), min(TQ, T)
    t512, t1024 = _fit(512, T), _fit(1024, T)
    xs = x[0]
    tgt = loss_target[0]
    px, py = lax.axis_index("x"), lax.axis_index("y")
    chip = 2 * px + py

    mats = [a_w_out[0], kv_w_down, kv_w_uk, kv_w_uv, b_w_in[0], b_w_uq[0], b_w_out[0]]
    vecs = [a_b_in[0], a_conv_w[0], a_conv_b[0], a_norm_g[0], a_norm_b[0], a_b_out[0]]
    vec_bits = jnp.concatenate([lax.bitcast_convert_type(w.reshape(-1), BF16).reshape(-1) for w in vecs])
    rest = [w.astype(BF16) for w in mats]
    g_win, gathered = _all_gather_chips(
        [a_w_in[0].astype(BF16), _pack_rows(vec_bits, BF16, 2 * BF16_ROWS * LANES)])
    gathered = gathered.reshape(N_CHIPS, -1)
    gathered, rest = lax.optimization_barrier((gathered, rest))
    rest_sems = _push_start("gather_rest_start", rest, by_target=False)
    off = 0
    fvec = []
    for w in vecs:
        bits = gathered[:, off:off + 2 * w.size].reshape((N_CHIPS,) + w.shape + (2,))
        fvec.append(lax.bitcast_convert_type(bits, F32))
        off += 2 * w.size
    cols = lambda g: jnp.moveaxis(g, 0, -2).reshape(g.shape[1:-1] + (N_CHIPS * g.shape[-1],))
    b_in = cols(fvec[0][:, None, :])
    conv_w = cols(fvec[1])
    conv_b, norm_g, norm_b, b_out = (cols(f[:, None, :]) for f in fvec[2:])
    row = lambda a: a.reshape(1, -1)
    g0, b0, g1, b1 = row(ln_g[0]), row(ln_b[0]), row(ln_g[1]), row(ln_b[1])
    kv_g, q_g = row(kv_norm_g), row(b_q_norm_g[0])
    plain = lambda acc, ins, i, j: [acc]

    b_in = b_in + rest_sems[4][0, 0]
    (proj,) = _row_mm("a_in", [((xs,), None)], g_win, nt=False, tm=t1024, tn=3 * E // N_CHIPS, tk=D,
                      outs=[((T, 3 * E), F32, 'tile')], epi=lambda acc, ins, i, j: [acc + ins[0]],
                      epi_ins=[(b_in, 'col')])
    u1, u4 = _conv_fwd(proj, conv_w, conv_b, norm_g, norm_b, E, tmw)

    rest, landed = _push_wait("gather_rest_wait", *rest_sems[:4], after=[u4], by_target=False)
    g_wout, g_wd, g_uk, g_uv, g_wbin, g_wuq, g_wbout = [
        lax.dynamic_update_slice(l, w[None], (chip,) + (0,) * w.ndim) for w, l in zip(rest, landed)]
    w_out = g_wout.reshape(E, D)
    wd = g_wd.reshape(D, RKV + ROPE_DIM)
    zpad = jnp.zeros((D, ROPE_HALF), BF16)
    wd_p = jnp.concatenate(
        [wd[:, :RKV], wd[:, RKV:RKV + ROPE_HALF], zpad, wd[:, RKV + ROPE_HALF:], zpad,
         jnp.zeros((D, WK - RKV - LANES), BF16)], axis=1)
    w_bin = cols(g_wbin)
    w_z = w_bin[:, RQ:]
    wb_small = jnp.concatenate([wd_p, w_bin[:, :RQ]], axis=1)
    wb_all = jnp.concatenate([wd_p, w_bin], axis=1)
    w_kv = jnp.concatenate([g_uk.reshape(RKV, HV), g_uv.reshape(RKV, HV)], axis=1)
    wuq = g_wuq.reshape(RQ, H, DN + ROPE_DIM)
    zq = jnp.zeros((RQ, H, ROPE_HALF), BF16)
    w_qr = jnp.concatenate([wuq[:, :, DN:DN + ROPE_HALF], zq, wuq[:, :, DN + ROPE_HALF:], zq], axis=2)
    w_q = jnp.concatenate([wuq[:, :, :DN].reshape(RQ, HV), w_qr.reshape(RQ, HV)], axis=1)
    w_bout = g_wbout.reshape(HV, D)

    freqs = ROPE_THETA ** (-jnp.arange(0, ROPE_DIM, 2, dtype=F32) / ROPE_DIM)
    ang = positions[0].astype(F32)[:, None] * freqs
    cs, sn = jnp.cos(ang), jnp.sin(ang)
    ones, zeros = jnp.ones_like(cs), jnp.zeros_like(cs)
    cos_t = jnp.concatenate([cs, ones, cs, ones], axis=1)
    sin_t = jnp.concatenate([-sn, zeros, sn, zeros], axis=1)

    def ln_epi(acc, ins, i, j):
        bias, res, g, b = ins
        xhat, rstd = _ln_stats(alpha * res + acc + bias)
        h = xhat * g + b
        return [h, h, xhat, rstd]

    h1, h1b, xhat1, rstd1 = _row_mm(
        "a_out", [((u4,), None)], w_out, nt=False, tm=t512, tn=D, tk=_fit(2048, E),
        outs=[((T, D), F32, 'tile'), ((T, D), BF16, 'tile'), ((T, D), F32, 'tile'), ((T, 1), F32, 'row')],
        epi=ln_epi, epi_ins=[(b_out, 'col'), (xs, 'tile'), (g0, 'col'), (b0, 'col')])

    tkb = _fit(512, _gcd(WK, RQ, HV))
    def latents_epi(acc, ins, i, j):
        kg, qg, cos_, sin_ = ins
        return [acc, _rms_fwd(acc[:, :RKV], kg), _rope(acc[:, RKV:RKV + LANES], cos_, sin_),
                _rms_fwd(acc[:, WK:WK + RQ], qg)]

    whole_row = lambda a: (a, pl.BlockSpec(a.shape, lambda i, j, k: (0, 0)))
    pb, c_lat, kr, cqn = _row_mm(
        "b_in", [((h1b,), None)], wb_small, nt=False, tm=t512, tn=Z_OFF, tk=_fit(1024, D),
        outs=[((T, Z_OFF), F32, 'tile'), ((T, RKV), BF16, 'row'), ((T, LANES), BF16, 'row'),
              ((T, RQ), BF16, 'row')],
        epi=latents_epi, epi_ins=[whole_row(kv_g), whole_row(q_g), (cos_t, 'row'), (sin_t, 'row')])
    (zb,) = _row_mm("b_in_gate", [((h1b,), None)], w_z, nt=False, tm=t1024, tn=_fit(2048, HV),
                    tk=_fit(1024, D), outs=[((T, HV), BF16, 'tile')], epi=plain)
    (kv,) = _row_mm("kv_up", [((c_lat,), None)], w_kv, nt=False, tm=t1024, tn=_fit(2048, HV),
                    tk=_fit(1024, RKV), outs=[((T, 2 * HV), BF16, 'tile')], epi=plain)
    tnq = _fit(2048, HV)
    half_q = HV // tnq

    def q_epi(acc, ins, i, j):
        return [jnp.where(j >= half_q, _rope(acc, ins[0], ins[1]), acc)]

    (q_all,) = _row_mm("q_up", [((cqn,), None)], w_q, nt=False, tm=t1024, tn=tnq, tk=_fit(1024, RQ),
                       outs=[((T, 2 * HV), BF16, 'tile')], epi=q_epi,
                       epi_ins=[(cos_t, 'row'), (sin_t, 'row')])
    o, lse = _attn_fwd(q_all, kv, kr, H, tq, scale)

    def loss_epi(acc, ins, i, j):
        res, g, b, target = ins
        xhat, rstd = _ln_stats(alpha * res + acc)
        diff = xhat * g + b - target
        dr, dg, db = _ln_bwd(diff / D, xhat, rstd, g)
        return [dr, 0.5 * jnp.sum(diff * diff, keepdims=True) / D, dg, db]

    dr1, loss_part, dg1, db1 = _row_mm(
        "b_out", [((o, zb), _gate)], w_bout, nt=False, tm=t512, tn=D, tk=_fit(2048, HV),
        outs=[((T, D), F32, 'tile'), ((1, 1), F32, 'acc'), ((1, D), F32, 'acc'), ((1, D), F32, 'acc')],
        epi=loss_epi, epi_ins=[(h1, 'tile'), (g1, 'col'), (b1, 'col'), (tgt, 'tile')])

    def gate_bwd_epi(acc, ins, i, j):
        gate, gate_grad = _silu_and_grad(ins[1].astype(F32))
        return [acc * gate, acc * ins[0].astype(F32) * gate_grad]

    do, dz = _row_mm(
        "b_out_bwd", [((dr1,), None)], w_bout, nt=True, tm=t512, tn=_fit(2048, HV), tk=_fit(1024, D),
        outs=[((T, HV), BF16, 'tile'), ((T, HV), BF16, 'tile')], epi=gate_bwd_epi,
        epi_ins=[(o, 'tile'), (zb, 'tile')])
    gw_bout = _tn_mm("dw_b_out", (o, zb), _gate, [((dr1,), None)], tn=_fit(1024, D), tk=t512, out_dtype=BF16)
    dqn, dqr_pre, dkn, dkr_h, dv = _attn_bwd(q_all, kv, kr, do, o, lse, cos_t, sin_t, H, tq, scale)
    dkr_pre = _key_rope_bwd(dkr_h, cos_t, sin_t, H, t512)

    def cq_bwd_epi(acc, ins, i, j):
        dx, dg = _rms_bwd(acc, ins[0], ins[1])
        return [dx, dg]

    dcq, dqg = _row_mm(
        "q_up_bwd", [((dqn,), None), ((dqr_pre,), None)], w_q, nt=True, tm=t1024, tn=RQ, tk=_fit(2048, HV),
        outs=[((T, RQ), BF16, 'tile'), ((1, RQ), F32, 'acc')], epi=cq_bwd_epi,
        epi_ins=[(pb, pl.BlockSpec((t1024, RQ), lambda i, j, k: (i, WK // RQ))), (q_g, 'col')])
    gw_q = _tn_mm("dw_q_up", (cqn,), None, [((dqn,), None), ((dqr_pre,), None)],
                  tn=_fit(2048, HV), tk=t1024, out_dtype=BF16)

    def ckv_bwd_epi(acc, ins, i, j):
        blk, dkr_t, g = ins
        dx, dg = _rms_bwd(acc, blk[:, :RKV], g)
        parts = [dx, dkr_t]
        if WK > RKV + LANES:
            parts.append(jnp.zeros((dx.shape[0], WK - RKV - LANES), F32))
        return [jnp.concatenate(parts, axis=1), dg]

    dckv, dkvg = _row_mm(
        "kv_up_bwd", [((dkn,), None), ((dv,), None)], w_kv, nt=True, tm=t1024, tn=RKV, tk=_fit(2048, HV),
        outs=[((T, WK), BF16, pl.BlockSpec((t1024, WK), lambda i, j, k: (i, 0))), ((1, RKV), F32, 'acc')],
        epi=ckv_bwd_epi,
        epi_ins=[(pb, pl.BlockSpec((t1024, WK), lambda i, j, k: (i, 0))), (dkr_pre, 'row'), (kv_g, 'col')])
    gw_kv = _tn_mm("dw_kv_up", (c_lat,), None, [((dkn,), None), ((dv,), None)],
                   tn=_fit(2048, HV), tk=t1024, out_dtype=BF16)

    def ln1_bwd_epi(acc, ins, i, j):
        dr_up, xhat, rstd, g = ins
        dr, dg, db = _ln_bwd(alpha * dr_up + acc, xhat, rstd, g)
        return [dr, dg, db]

    dp_segs = [((dckv,), None), ((dcq,), None), ((dz,), None)]
    gw_lat = _tn_mm("dw_b_in", (h1b,), None, dp_segs[:2], tn=tkb, tk=t1024, out_dtype=BF16)
    gw_z = _tn_mm("dw_b_in_gate", (h1b,), None, dp_segs[2:], tn=_fit(2048, HV), tk=t1024, out_dtype=BF16)

    shard_cols = lambda g: jnp.moveaxis(g.reshape(g.shape[0], N_CHIPS, -1), 1, 0)
    gq = gw_q.reshape(RQ, 2, H, LANES)
    g_uq = jnp.concatenate(
        [gq[:, 0], gq[:, 1, :, :ROPE_HALF], gq[:, 1, :, 2 * ROPE_HALF:3 * ROPE_HALF]], axis=2)
    g_wd_full = jnp.concatenate(
        [gw_lat[:, :RKV], gw_lat[:, RKV:RKV + ROPE_HALF],
         gw_lat[:, RKV + 2 * ROPE_HALF:RKV + 3 * ROPE_HALF]], axis=1)
    late_names = ["kv_w_down", "kv_w_uk", "kv_w_uv", "b_w_in", "b_w_uq", "b_w_out"]
    late_w = [kv_w_down, kv_w_uk, kv_w_uv, b_w_in, b_w_uq, b_w_out]
    gw_bin = jnp.concatenate([gw_lat[:, WK:], gw_z], axis=1)
    chip_major = [g_wd_full, gw_kv[:, :HV], gw_kv[:, HV:], shard_cols(gw_bin), g_uq, gw_bout]
    late_grads = [g.reshape((N_CHIPS,) + w.shape) for g, w in zip(chip_major, late_w)]
    late_sems = _push_start("scatter_late_start", late_grads, by_target=True)

    dr0, dg0, db0 = _row_mm(
        "b_in_bwd", dp_segs, wb_all, nt=True, tm=t1024, tn=D, tk=tkb,
        outs=[((T, D), F32, 'tile'), ((1, D), F32, 'acc'), ((1, D), F32, 'acc')], epi=ln1_bwd_epi,
        epi_ins=[(dr1, 'tile'), (xhat1, 'tile'), (rstd1, 'row'), (g0 + late_sems[4][0, 0], 'col')])

    def conv_branch_bwd_epi(acc, ins, i, j):
        u1_t, z, g, b = ins
        xhat, rstd = _ln_stats(u1_t)
        u2 = xhat * g + b
        gate, gate_grad = _silu_and_grad(z)
        act, act_grad = _silu_and_grad(u2)
        dz_a = acc * act * gate_grad
        du1, dg, db = _ln_bwd(acc * gate * act_grad, xhat, rstd, g)
        return [du1, dz_a, dg, db]

    du1, dz_a, dng, dnb = _row_mm(
        "a_out_bwd", [((dr0,), None)], w_out, nt=True, tm=tmw, tn=E, tk=_fit(1024, D),
        outs=[((T, E), F32, 'tile'), ((T, E), BF16, 'tile'), ((1, E), F32, 'acc'), ((1, E), F32, 'acc')],
        epi=conv_branch_bwd_epi,
        epi_ins=[(u1, 'tile'), (proj, pl.BlockSpec((tmw, E), lambda i, j, k: (i, 2))), (norm_g, 'col'),
                 (norm_b, 'col')])
    gw_out, dbo = _tn_mm("dw_a_out", (u4,), None, [((dr0,), None)], tn=_fit(1024, D), tk=t1024,
                         out_dtype=BF16, colsum=True)
    mid_sems = _push_start("scatter_mid_start", [gw_out.reshape((N_CHIPS,) + a_w_out.shape)], by_target=True)
    dval, dgate, dcw, dcb = _conv_bwd(du1, proj, conv_w + mid_sems[4][0, 0], E, tmw)
    dproj_segs = [((dval,), None), ((dgate,), None), ((dz_a,), None)]
    gw_in, dbi = _tn_mm("dw_a_in", (xs,), None, dproj_segs, tn=_fit(2048, E), tk=t1024, out_dtype=BF16,
                        colsum=True)

    chip_word = chip.reshape(1).astype(jnp.int32)

    def reduce_and_update(tag, names_, sent, landed, w_, m_, v_, by_target=True):
        sums = [_sum_chips("sum_" + n, l, s, chip_word, by_target) for n, s, l in zip(names_, sent, landed)]
        theirs = _swap_cores("swap_cores_" + tag, sums)
        return {n: _adamw("adamw_" + n, [mine, other], w, m, v)
                for n, mine, other, w, m, v in zip(names_, sums, theirs, w_, m_, v_)}

    late_sent, late_landed = _push_wait("scatter_late_wait", *late_sems[:4], after=[dbi], by_target=True)
    mid_sent, mid_landed = _push_wait("scatter_mid_wait", *mid_sems[:4], after=[dbi], by_target=True)
    early_sems = _push_start("scatter_early_start", [gw_in.reshape(a_w_in.shape[:-1] + (3 * E,))],
                             by_target='cols')
    (grad_x,) = _row_mm(
        "a_in_bwd", dproj_segs, g_win, nt=True, tm=t512, tn=D, tk=E, b_whole=True,
        outs=[((T, D), F32, 'tile')], epi=lambda acc, ins, i, j: [alpha * ins[0] + acc + ins[1]],
        epi_ins=[(dr0, 'tile'), (jnp.zeros((1, D), F32) + early_sems[4][0, 0], 'col')])
    big_out = reduce_and_update(
        "late", ["a_w_out"] + late_names, mid_sent + late_sent, mid_landed + late_landed,
        [a_w_out] + late_w, [m_a_w_out, m_kv_w_down, m_kv_w_uk, m_kv_w_uv, m_b_w_in, m_b_w_uq, m_b_w_out],
        [v_a_w_out, v_kv_w_down, v_kv_w_uk, v_kv_w_uv, v_b_w_in, v_b_w_uq, v_b_w_out])

    small_full = [jnp.concatenate([dg0, dg1]), jnp.concatenate([db0, db1]), dbi, dcw, dcb, dng, dnb, dbo,
                  dkvg, dqg, loss_part]
    sflat = jnp.concatenate([g.reshape(-1) for g in small_full])
    summed = _all_reduce_small(_pack_rows(sflat, F32, 8 * LANES)).reshape(-1)
    soff = 0
    sgrads = []
    for g in small_full:
        sgrads.append(summed[soff:soff + g.size].reshape(g.shape))
        soff += g.size
    loss = sgrads.pop()[0, 0]
    local_cols = lambda g, n: lax.dynamic_slice_in_dim(g, chip * n, n, axis=g.ndim - 1)
    snames = ["ln_g", "ln_b", "a_b_in", "a_conv_w", "a_conv_b", "a_norm_g", "a_norm_b", "a_b_out",
              "kv_norm_g", "b_q_norm_g"]
    small_w = [ln_g, ln_b, a_b_in, a_conv_w, a_conv_b, a_norm_g, a_norm_b, a_b_out, kv_norm_g, b_q_norm_g]
    small_m = [m_ln_g, m_ln_b, m_a_b_in, m_a_conv_w, m_a_conv_b, m_a_norm_g, m_a_norm_b, m_a_b_out,
               m_kv_norm_g, m_b_q_norm_g]
    small_v = [v_ln_g, v_ln_b, v_a_b_in, v_a_conv_w, v_a_conv_b, v_a_norm_g, v_a_norm_b, v_a_b_out,
               v_kv_norm_g, v_b_q_norm_g]
    sharded = {"a_b_in", "a_conv_w", "a_conv_b", "a_norm_g", "a_norm_b", "a_b_out"}
    local_g = [(local_cols(g, w.shape[-1]) if n in sharded else g).reshape(w.shape)
               for n, g, w in zip(snames, sgrads, small_w)]
    at_least_2d = lambda a: a.reshape((1,) + a.shape) if a.ndim == 1 else a
    sres = _adamw_vectors([tuple(at_least_2d(a) for a in item)
                           for item in zip(local_g, small_w, small_m, small_v)])
    small_out = {n: [g] + [r.reshape(w.shape) for r in res]
                 for n, g, w, res in zip(snames, local_g, small_w, sres)}

    early_sent, early_landed = _push_wait(
        "scatter_early_wait", *early_sems[:4], by_target='cols',
        after=[grad_x, big_out["b_w_out"][1], small_out["b_q_norm_g"][1]])
    big_out.update(reduce_and_update("early", ["a_w_in"], early_sent, early_landed,
                                     [a_w_in], [m_a_w_in], [v_a_w_in], by_target='cols'))

    order =["ln_g", "ln_b", "a_w_in", "a_b_in", "a_conv_w", "a_conv_b", "a_norm_g", "a_norm_b", "a_w_out",
             "a_b_out", "kv_w_down", "kv_norm_g", "kv_w_uk", "kv_w_uv", "b_w_in", "b_q_norm_g", "b_w_uq",
             "b_w_out"]
    outs = {**big_out, **small_out}
    result = [loss, grad_x[None]]
    for part in range(4):
        result += [outs[n][part] for n in order]
    return tuple(result)
```

```python
import functools
import math

import jax
import jax.numpy as jnp
from jax import lax
from jax.experimental import pallas as pl
from jax.experimental.pallas import tpu as pltpu

F32, BF16 = jnp.float32, jnp.bfloat16
NN = (((1,), (0,)), ((), ()))
NT = (((1,), (1,)), ((), ()))
TN = (((0,), (0,)), ((), ()))
MESH = pl.DeviceIdType.MESH
ANY = pl.BlockSpec(memory_space=pl.ANY)

LANES = 128
BF16_ROWS = 16
VMEM_LIMIT = 56 * 1024 * 1024
N_CHIPS = 4
LN_EPS = 1e-5
RMS_EPS = 1e-6
MASK_VALUE = -1e30
LOG2_E = math.log2(math.e)
ROPE_THETA = 10000.0
ROPE_DIM = 64
ROPE_HALF = ROPE_DIM // 2
ADAM_LR, ADAM_B1, ADAM_B2, ADAM_EPS, ADAM_WD, ADAM_STEP = 0.001, 0.9, 0.999, 1e-08, 0.01, 10

MAX_TILE = 2048
TM_WIDE = 256
TQ = 512
CONV_HALO = 32
CONV_LC = 1024
CONV_SUB = 256
SUBLANES = 8
CONV_RC = 32
ADAM_ROWS = 128


def _dot(a, b, dims):
    return lax.dot_general(a.astype(BF16), b.astype(BF16), dims, preferred_element_type=F32)


def _sig(x):
    return 0.5 * jnp.tanh(0.5 * x) + 0.5


def _params(n_axes):
    return pltpu.CompilerParams(dimension_semantics=("arbitrary",) * n_axes, vmem_limit_bytes=VMEM_LIMIT)


def _gcd(*v):
    return functools.reduce(math.gcd, v)


def _fit(want, dim):
    return math.gcd(min(want, MAX_TILE), dim)


def _row_mm(name, a_segs, b, *, nt, tm, tn, tk, outs, epi, epi_ins=(), b_whole=False):
    M = a_segs[0][0][0].shape[0]
    stacked = b.ndim == 3
    if stacked:
        n_blk = b.shape[2]
        N = b.shape[1] if nt else N_CHIPS * n_blk
        assert (nt and b_whole) or (not nt and tn == n_blk and tk == b.shape[1]), name
    else:
        N = b.shape[0] if nt else b.shape[1]
    nkb = [arrs[0].shape[1] // tk for arrs, _ in a_segs]
    koff = [sum(nkb[:s]) for s in range(len(nkb))]
    ni, nj, nk = M // tm, N // tn, sum(nkb)
    assert M % tm == 0 and N % tn == 0 and all(arrs[0].shape[1] % tk == 0 for arrs, _ in a_segs), name
    assert stacked or (b.shape[1] if nt else b.shape[0]) == nk * tk, name

    def spec_of(shape, kind):
        if isinstance(kind, pl.BlockSpec):
            return kind
        if kind == 'tile':
            return pl.BlockSpec((tm, tn), lambda i, j, k: (i, j))
        if kind == 'row':
            return pl.BlockSpec((tm, shape[1]), lambda i, j, k: (i, 0))
        if kind == 'col':
            return pl.BlockSpec((1, tn), lambda i, j, k: (0, j))
        assert kind == 'acc' and nj == 1, name
        return pl.BlockSpec(shape, lambda i, j, k: (0,) * len(shape))

    in_specs, operands = [], []
    for s, (arrs, _) in enumerate(a_segs):
        for arr in arrs:
            in_specs.append(pl.BlockSpec(
                (tm, tk), lambda i, j, k, s=s: (i, jnp.clip(k - koff[s], 0, nkb[s] - 1))))
            operands.append(arr)
    if b_whole:
        assert nt and nj == 1 and all(n == 1 for n in nkb), name
        in_specs.append(pl.BlockSpec(b.shape, lambda i, j, k: (0,) * b.ndim))
    elif stacked:
        in_specs.append(pl.BlockSpec((1, tk, tn), lambda i, j, k: (j, 0, 0)))
    else:
        in_specs.append(pl.BlockSpec((tn, tk), lambda i, j, k: (j, k)) if nt
                        else pl.BlockSpec((tk, tn), lambda i, j, k: (k, j)))
    operands.append(b)
    for arr, kind in epi_ins:
        in_specs.append(spec_of(arr.shape, kind))
        operands.append(arr)
    out_specs = [spec_of(shape, kind) for shape, _, kind in outs]
    out_shape = [jax.ShapeDtypeStruct(shape, dtype) for shape, dtype, _ in outs]
    n_seg_refs = [len(arrs) for arrs, _ in a_segs]

    def body(*refs):
        pos = 0
        seg_refs = []
        for n in n_seg_refs:
            seg_refs.append(refs[pos:pos + n])
            pos += n
        b_ref = refs[pos]
        e_refs = refs[pos + 1:pos + 1 + len(epi_ins)]
        o_refs = refs[pos + 1 + len(epi_ins):pos + 1 + len(epi_ins) + len(outs)]
        i, j, k = pl.program_id(0), pl.program_id(1), pl.program_id(2)

        def product(fn, rs, s=0):
            a = rs[0][...] if fn is None else fn(*[r[...] for r in rs])
            if b_whole and stacked:
                lo, hi, tot = koff[s] * tk, (koff[s] + 1) * tk, None
                for q in range(lo // n_blk, (hi - 1) // n_blk + 1):
                    c0, c1 = max(lo, q * n_blk), min(hi, (q + 1) * n_blk)
                    part = _dot(a[:, c0 - lo:c1 - lo], b_ref[q, :, c0 - q * n_blk:c1 - q * n_blk], NT)
                    tot = part if tot is None else tot + part
                return tot
            if b_whole:
                return _dot(a, b_ref[:, koff[s] * tk:(koff[s] + 1) * tk], NT)
            return _dot(a, b_ref[0] if stacked else b_ref[...], NT if nt else NN)

        def finish(acc):
            res = epi(acc, [r[...] for r in e_refs], i, j)
            for o_ref, (_, _, kind), r in zip(o_refs, outs, res):
                if isinstance(kind, str) and kind == 'acc':
                    @pl.when(i == 0)
                    def _(o_ref=o_ref, r=r):
                        o_ref[...] = r

                    @pl.when(i > 0)
                    def _(o_ref=o_ref, r=r):
                        o_ref[...] += r
                else:
                    o_ref[...] = r.astype(o_ref.dtype)

        if nk == 1:
            finish(product(a_segs[0][1], seg_refs[0]))
            return
        acc_ref = refs[-1]

        @pl.when(k == 0)
        def _():
            acc_ref[...] = jnp.zeros_like(acc_ref)

        for s, ((_, fn), rs) in enumerate(zip(a_segs, seg_refs)):
            def accumulate(fn=fn, rs=rs, s=s):
                acc_ref[...] += product(fn, rs, s)
            if len(a_segs) == 1:
                accumulate()
            else:
                pl.when(jnp.logical_and(k >= koff[s], k < koff[s] + nkb[s]))(accumulate)

        @pl.when(k == nk - 1)
        def _():
            finish(acc_ref[...])

    return pl.pallas_call(
        body, name=name, grid=(ni, nj, nk), in_specs=in_specs, out_specs=out_specs, out_shape=out_shape,
        scratch_shapes=[] if nk == 1 else [pltpu.VMEM((tm, tn), F32)], compiler_params=_params(3),
    )(*operands)


def _tn_mm(name, a_arrs, a_fn, b_segs, *, tn, tk, out_dtype, shard_major=False, colsum=False):
    T, M = a_arrs[0].shape
    nbj = [arrs[0].shape[1] // tn for arrs, _ in b_segs]
    joff = [sum(nbj[:s]) for s in range(len(nbj))]
    nj, nk = sum(nbj), T // tk
    N = nj * tn
    assert T % tk == 0 and all(arrs[0].shape[1] % tn == 0 for arrs, _ in b_segs), name

    in_specs = [pl.BlockSpec((tk, M), lambda j, k: (k, 0)) for _ in a_arrs]
    operands = list(a_arrs)
    for s, (arrs, _) in enumerate(b_segs):
        for arr in arrs:
            in_specs.append(pl.BlockSpec(
                (tk, tn), lambda j, k, s=s: (k, jnp.clip(j - joff[s], 0, nbj[s] - 1))))
            operands.append(arr)
    if shard_major:
        per = (N // N_CHIPS) // tn
        assert per * tn * N_CHIPS == N, name
        out_shape = [jax.ShapeDtypeStruct((N_CHIPS, M, N // N_CHIPS), out_dtype)]
        out_specs = [pl.BlockSpec((1, M, tn), lambda j, k: (j // per, 0, j % per))]
    else:
        out_shape = [jax.ShapeDtypeStruct((M, N), out_dtype)]
        out_specs = [pl.BlockSpec((M, tn), lambda j, k: (0, j))]
    if colsum:
        out_shape.append(jax.ShapeDtypeStruct((1, N), F32))
        out_specs.append(pl.BlockSpec((1, tn), lambda j, k: (0, j)))
    n_a = len(a_arrs)
    n_seg_refs = [len(arrs) for arrs, _ in b_segs]

    def body(*refs):
        a_refs = refs[:n_a]
        pos = n_a
        seg_refs = []
        for n in n_seg_refs:
            seg_refs.append(refs[pos:pos + n])
            pos += n
        o_ref = refs[pos]
        cs_ref = refs[pos + 1] if colsum else None
        acc_ref = refs[-1]
        j, k = pl.program_id(0), pl.program_id(1)

        @pl.when(k == 0)
        def _():
            acc_ref[...] = jnp.zeros_like(acc_ref)
            if colsum:
                cs_ref[...] = jnp.zeros_like(cs_ref)

        for s, ((_, fn), rs) in enumerate(zip(b_segs, seg_refs)):
            def accumulate(fn=fn, rs=rs):
                a = a_refs[0][...] if a_fn is None else a_fn(*[r[...] for r in a_refs])
                bt = rs[0][...] if fn is None else fn(*[r[...] for r in rs])
                acc_ref[...] += _dot(a, bt, TN)
                if colsum:
                    cs_ref[...] += jnp.sum(bt.astype(F32), axis=0, keepdims=True)
            if len(b_segs) == 1:
                accumulate()
            else:
                pl.when(jnp.logical_and(j >= joff[s], j < joff[s] + nbj[s]))(accumulate)

        @pl.when(k == nk - 1)
        def _():
            if shard_major:
                o_ref[0] = acc_ref[...].astype(o_ref.dtype)
            else:
                o_ref[...] = acc_ref[...].astype(o_ref.dtype)

    res = pl.pallas_call(
        body, name=name, grid=(nj, nk), in_specs=in_specs, out_specs=out_specs, out_shape=out_shape,
        scratch_shapes=[pltpu.VMEM((M, tn), F32)], compiler_params=_params(2),
    )(*operands)
    return res if colsum else res[0]


def _silu(z):
    return z * _sig(z)


def _silu_and_grad(z):
    s = _sig(z)
    return z * s, s * (1.0 + z * (1.0 - s))


def _gate(o, z):
    return o.astype(F32) * _silu(z.astype(F32))


def _ln_stats(r):
    mu = jnp.mean(r, axis=1, keepdims=True)
    xc = r - mu
    var = jnp.mean(xc * xc, axis=1, keepdims=True)
    rstd = lax.rsqrt(var + LN_EPS)
    return xc * rstd, rstd


def _ln_bwd(dy, xhat, rstd, g):
    dxh = dy * g
    m1 = jnp.mean(dxh, axis=1, keepdims=True)
    m2 = jnp.mean(dxh * xhat, axis=1, keepdims=True)
    return (rstd * (dxh - m1 - xhat * m2), jnp.sum(dy * xhat, axis=0, keepdims=True),
            jnp.sum(dy, axis=0, keepdims=True))


def _rms_fwd(x, g):
    rstd = lax.rsqrt(jnp.mean(x * x, axis=1, keepdims=True) + RMS_EPS)
    return x * rstd * g


def _rms_bwd(dy, x, g):
    rstd = lax.rsqrt(jnp.mean(x * x, axis=1, keepdims=True) + RMS_EPS)
    xn = x * rstd
    dxn = dy * g
    return rstd * (dxn - xn * jnp.mean(dxn * xn, axis=1, keepdims=True)), jnp.sum(dy * xn, axis=0, keepdims=True)


def _rope(x, cos, sin, transpose=False):
    parts = []
    for g in range(x.shape[1] // LANES):
        xg = x[:, g * LANES:(g + 1) * LANES]
        if transpose:
            parts.append(xg * cos + pltpu.roll(xg * sin, LANES // 2, 1))
        else:
            parts.append(xg * cos + pltpu.roll(xg, LANES // 2, 1) * sin)
    return parts[0] if len(parts) == 1 else jnp.concatenate(parts, axis=1)


def _shifted_rows(window, rc):
    n = window.shape[0]
    for b in range(SUBLANES):
        rolled = window if b == 0 else pltpu.roll(window, n - b, 0)
        for a8 in range(0, n - rc - b + 1, SUBLANES):
            yield a8 + b, rolled[a8:a8 + rc]


def _conv_fwd(proj, conv_w, conv_b, norm_g, norm_b, E, tm):
    T = proj.shape[0]
    kc = conv_w.shape[0]
    hb, rc = CONV_HALO, min(CONV_RC, tm)
    ni, ratio = T // tm, tm // hb
    sub = min(CONV_SUB, E)
    base = hb - (kc - 1)

    def body(val_ref, gate_ref, z_ref, valh_ref, gateh_ref, w_ref, cb_ref, g_ref, b_ref, u1_ref, u4_ref, ubuf):
        i = pl.program_id(0)
        ubuf[hb:, :] = val_ref[...] * _sig(gate_ref[...])
        halo = valh_ref[...] * _sig(gateh_ref[...])
        ubuf[0:hb, :] = jnp.where(i > 0, halo, 0.0)
        for l0 in range(0, E, sub):
            ls = slice(l0, l0 + sub)
            for r0 in range(0, tm, rc):
                acc = jnp.zeros((rc, sub), F32) + cb_ref[:, ls]
                for off, rows in _shifted_rows(ubuf[r0:r0 + hb + rc, ls], rc):
                    if 0 <= off - base < kc:
                        acc += w_ref[off - base:off - base + 1, ls] * rows
                u1_ref[r0:r0 + rc, ls] = acc
        xhat, _ = _ln_stats(u1_ref[...])
        u4_ref[...] = (_silu(xhat * g_ref[...] + b_ref[...]) * _silu(z_ref[...])).astype(BF16)

    main = lambda col: pl.BlockSpec((tm, E), lambda i: (i, col))
    halo = lambda col: pl.BlockSpec((hb, E), lambda i: (jnp.maximum(i * ratio - 1, 0), col))
    whole = lambda a: pl.BlockSpec(a.shape, lambda i: (0, 0))
    return pl.pallas_call(
        body, name="conv_fwd", grid=(ni,),
        in_specs=[main(0), main(1), main(2), halo(0), halo(1), whole(conv_w), whole(conv_b), whole(norm_g),
                  whole(norm_b)],
        out_specs=[main(0), main(0)],
        out_shape=[jax.ShapeDtypeStruct((T, E), F32), jax.ShapeDtypeStruct((T, E), BF16)],
        scratch_shapes=[pltpu.VMEM((hb + tm, E), F32)], compiler_params=_params(1),
    )(proj, proj, proj, proj, proj, conv_w, conv_b, norm_g, norm_b)


def _conv_bwd(du1, proj, conv_w, E, tm):
    T = du1.shape[0]
    kc = conv_w.shape[0]
    lc, hb, rc = min(CONV_LC, E), CONV_HALO, min(CONV_RC, tm)
    nl, ni, ratio = E // lc, T // tm, tm // hb
    gate_off = E // lc
    last_halo = T // hb - 1

    sub = min(CONV_SUB, lc)
    base = hb - (kc - 1)

    def body(du_ref, dun_ref, val_ref, gate_ref, valh_ref, gateh_ref, w_ref,
             dval_ref, dgate_ref, dw_ref, db_ref, ubuf, dbuf, sbuf, dw_sc):
        i = pl.program_id(1)
        sbuf[...] = _sig(gate_ref[...])
        ubuf[hb:, :] = val_ref[...] * sbuf[...]
        halo = valh_ref[...] * _sig(gateh_ref[...])
        ubuf[0:hb, :] = jnp.where(i > 0, halo, 0.0)
        dbuf[0:tm, :] = du_ref[...]
        dbuf[tm:, :] = jnp.where(i < ni - 1, dun_ref[...], 0.0)

        @pl.when(i == 0)
        def _():
            dw_sc[...] = jnp.zeros_like(dw_sc)
            db_ref[...] = jnp.zeros_like(db_ref)

        db_ref[...] += jnp.sum(du_ref[...], axis=0, keepdims=True)
        for l0 in range(0, lc, sub):
            ls = slice(l0, l0 + sub)
            for r0 in range(0, tm, rc):
                dwin = dbuf[r0:r0 + rc + hb, ls]
                dchunk = dwin[0:rc]
                for off, rows in _shifted_rows(ubuf[r0:r0 + hb + rc, ls], rc):
                    k = off - base
                    if 0 <= k < kc:
                        prod = rows * dchunk
                        part = prod[0:SUBLANES]
                        for s8 in range(SUBLANES, rc, SUBLANES):
                            part = part + prod[s8:s8 + SUBLANES]
                        dw_sc[k, :, ls] += part
                acc = jnp.zeros((rc, sub), F32)
                for off, rows in _shifted_rows(dwin, rc):
                    k = (kc - 1) - off
                    if 0 <= k < kc:
                        acc += w_ref[k:k + 1, ls] * rows
                v, s = val_ref[r0:r0 + rc, ls], sbuf[r0:r0 + rc, ls]
                dval_ref[r0:r0 + rc, ls] = (acc * s).astype(BF16)
                dgate_ref[r0:r0 + rc, ls] = (acc * v * s * (1.0 - s)).astype(BF16)

        @pl.when(i == ni - 1)
        def _():
            for k in range(kc):
                dw_ref[k:k + 1, :] = jnp.sum(dw_sc[k], axis=0, keepdims=True)

    return pl.pallas_call(
        body, name="conv_bwd", grid=(nl, ni),
        in_specs=[
            pl.BlockSpec((tm, lc), lambda l, i: (i, l)),
            pl.BlockSpec((hb, lc), lambda l, i: (jnp.minimum((i + 1) * ratio, last_halo), l)),
            pl.BlockSpec((tm, lc), lambda l, i: (i, l)),
            pl.BlockSpec((tm, lc), lambda l, i: (i, gate_off + l)),
            pl.BlockSpec((hb, lc), lambda l, i: (jnp.maximum(i * ratio - 1, 0), l)),
            pl.BlockSpec((hb, lc), lambda l, i: (jnp.maximum(i * ratio - 1, 0), gate_off + l)),
            pl.BlockSpec((kc, lc), lambda l, i: (0, l)),
        ],
        out_specs=[pl.BlockSpec((tm, lc), lambda l, i: (i, l)), pl.BlockSpec((tm, lc), lambda l, i: (i, l)),
                   pl.BlockSpec((kc, lc), lambda l, i: (0, l)), pl.BlockSpec((1, lc), lambda l, i: (0, l))],
        out_shape=[jax.ShapeDtypeStruct((T, E), BF16), jax.ShapeDtypeStruct((T, E), BF16),
                   jax.ShapeDtypeStruct((kc, E), F32), jax.ShapeDtypeStruct((1, E), F32)],
        scratch_shapes=[pltpu.VMEM((hb + tm, lc), F32), pltpu.VMEM((tm + hb, lc), F32),
                        pltpu.VMEM((tm, lc), F32), pltpu.VMEM((kc, SUBLANES, lc), F32)],
        compiler_params=_params(2),
    )(du1, du1, proj, proj, proj, proj, conv_w)


def _attn_fwd(q_all, kv, kr, H, tq, scale):
    T = q_all.shape[0]
    nq = T // tq
    pair = 2
    W = pair * LANES
    assert H % pair == 0
    hp_n = H // pair

    def body(qn_ref, qr_ref, kn_ref, kr_ref, v_ref, o_ref, lse_ref, *scratch):
        qi = pl.program_id(1)
        chains = [scratch[4 * a:4 * a + 4] for a in range(pair)]
        lanes = [slice(a * LANES, (a + 1) * LANES) for a in range(pair)]
        groups = [slice(c * LANES, (c + 1) * LANES) for c in range(tq // LANES)]

        def fold(x, op):
            r = x[:, groups[0]]
            for gsl in groups[1:]:
                r = op(r, x[:, gsl])
            return r

        for _, m_sc, l_sc, acc_sc in chains:
            m_sc[...] = jnp.full_like(m_sc, MASK_VALUE)
            l_sc[...] = jnp.zeros_like(l_sc)
            acc_sc[...] = jnp.zeros_like(acc_sc)

        def scores(j, masked):
            rows = pl.ds(pl.multiple_of(j * tq, tq), tq)
            krope = kr_ref[rows, :]
            for a, (s_sc, m_sc, _, _) in enumerate(chains):
                q = jnp.concatenate([qn_ref[:, lanes[a]], qr_ref[:, lanes[a]]], axis=1)
                k = jnp.concatenate([kn_ref[rows, lanes[a]], krope], axis=1)
                s = _dot(q, k, NT) * (scale * LOG2_E)
                if masked:
                    row = lax.broadcasted_iota(jnp.int32, s.shape, 0)
                    col = lax.broadcasted_iota(jnp.int32, s.shape, 1)
                    s = jnp.where(col <= row, s, MASK_VALUE)
                s_sc[j] = s
                m_sc[...] = jnp.maximum(m_sc[...], fold(s, jnp.maximum))

        def two_per_trip(fn, count):
            def two(p, carry):
                fn(2 * p)
                fn(2 * p + 1)
                return carry

            lax.fori_loop(0, count // 2, two, 0)

            @pl.when(count % 2 == 1)
            def _():
                fn(count - 1)

        two_per_trip(functools.partial(scores, masked=False), qi)
        scores(qi, True)
        for _, m_sc, _, _ in chains:
            m_sc[...] = jnp.broadcast_to(jnp.max(m_sc[...], axis=1, keepdims=True), m_sc.shape)

        def weigh(j):
            rows = pl.ds(pl.multiple_of(j * tq, tq), tq)
            for a, (s_sc, m_sc, l_sc, acc_sc) in enumerate(chains):
                s, m = s_sc[j], m_sc[...]
                p = jnp.concatenate([jnp.exp2(s[:, gsl] - m) for gsl in groups], axis=1)
                l_sc[...] += fold(p, jnp.add)
                acc_sc[...] += _dot(p, v_ref[rows, lanes[a]], NN)

        two_per_trip(weigh, qi + 1)
        for a, (_, m_sc, l_sc, acc_sc) in enumerate(chains):
            l = jnp.sum(l_sc[...], axis=1, keepdims=True)
            o_ref[:, lanes[a]] = (acc_sc[...] / l).astype(BF16)
            lse_ref[a] = m_sc[:, 0:1] * (1.0 / LOG2_E) + jnp.log(l)

    chain_scratch = [pltpu.VMEM((nq, tq, tq), F32), pltpu.VMEM((tq, LANES), F32), pltpu.VMEM((tq, LANES), F32),
                     pltpu.VMEM((tq, LANES), F32)]
    return pl.pallas_call(
        body, name="attn_fwd", grid=(hp_n, nq),
        in_specs=[pl.BlockSpec((tq, W), lambda hp, qi: (qi, hp)),
                  pl.BlockSpec((tq, W), lambda hp, qi: (qi, hp_n + hp)),
                  pl.BlockSpec((T, W), lambda hp, qi: (0, hp)),
                  pl.BlockSpec((T, LANES), lambda hp, qi: (0, 0)),
                  pl.BlockSpec((T, W), lambda hp, qi: (0, hp_n + hp))],
        out_specs=[pl.BlockSpec((tq, W), lambda hp, qi: (qi, hp)),
                   pl.BlockSpec((pair, tq, 1), lambda hp, qi: (hp, qi, 0))],
        out_shape=[jax.ShapeDtypeStruct((T, H * LANES), BF16), jax.ShapeDtypeStruct((H, T, 1), F32)],
        scratch_shapes=chain_scratch * pair, compiler_params=_params(2),
    )(q_all, q_all, kv, kr, kv)


def _attn_bwd(q_all, kv, kr, do, o, lse, cos, sin, H, tq, scale):
    T = q_all.shape[0]
    nq = T // tq
    HV = H * LANES
    pair = 2
    tk2 = pair * tq
    ng = T // tk2
    assert ng * tk2 == T and pair == 2

    def body(qn_ref, qr_ref, kn_ref, kr_ref, v_ref, do_ref, o_ref, lse_ref, cos_ref, sin_ref,
             dqn_ref, dqr_ref, dkn_ref, dkr_ref, dv_ref, dq_sc, dk_sc, dv_sc):
        g = pl.program_id(1)

        @pl.when(g == 0)
        def _():
            dq_sc[...] = jnp.zeros_like(dq_sc)

        key_rows = [slice(c * tq, (c + 1) * tq) for c in range(pair)]

        def block(qi, modes):
            rows = pl.ds(pl.multiple_of(qi * tq, tq), tq)
            q = jnp.concatenate([qn_ref[rows, :], qr_ref[rows, :]], axis=1)
            dov = do_ref[rows, :]
            delta = jnp.sum(dov.astype(F32) * o_ref[rows, :].astype(F32), axis=1, keepdims=True)
            lse_q = lse_ref[0, rows, :]
            dq, dkv = None, []
            for kr_, masked in zip(key_rows, modes):
                if masked is None:
                    dkv.append(None)
                    continue
                k = jnp.concatenate([kn_ref[kr_, :], kr_ref[kr_, :]], axis=1)
                s = _dot(q, k, NT) * scale
                if masked:
                    row = lax.broadcasted_iota(jnp.int32, s.shape, 0)
                    col = lax.broadcasted_iota(jnp.int32, s.shape, 1)
                    s = jnp.where(col <= row, s, MASK_VALUE)
                p = jnp.exp(s - lse_q)
                dv = _dot(p, dov, TN)
                dp = _dot(dov, v_ref[kr_, :], NT)
                ds = (p * (dp - delta) * scale).astype(BF16)
                dkv.append((_dot(ds, q, TN), dv))
                part = _dot(ds, k, NN)
                dq = part if dq is None else dq + part
            return rows, dq, dkv

        rows_a, dq_a, (kv_a0, _) = block(pair * g, (True, None))
        rows_b, dq_b, (kv_b0, kv_b1) = block(pair * g + 1, (False, True))
        dk_sc[key_rows[0], :] = kv_a0[0] + kv_b0[0]
        dv_sc[key_rows[0], :] = kv_a0[1] + kv_b0[1]
        dk_sc[key_rows[1], :] = kv_b1[0]
        dv_sc[key_rows[1], :] = kv_b1[1]
        dq_sc[rows_a, :] += dq_a
        dq_sc[rows_b, :] += dq_b

        def below(trip, carry):
            for qi in (pair * g + pair + 2 * trip, pair * g + pair + 2 * trip + 1):
                rows, dq, dkv = block(qi, (False, False))
                for kr_, (dk, dv) in zip(key_rows, dkv):
                    dk_sc[kr_, :] += dk
                    dv_sc[kr_, :] += dv
                dq_sc[rows, :] += dq
            return carry

        lax.fori_loop(0, (nq - pair * g - pair) // 2, below, 0)
        dkn_ref[...] = dk_sc[:, :LANES].astype(BF16)
        dkr_ref[...] = dk_sc[:, LANES:].astype(BF16)
        dv_ref[...] = dv_sc[...].astype(BF16)

        @pl.when(g == ng - 1)
        def _():
            dqn_ref[...] = dq_sc[:, :LANES].astype(BF16)
            dqr_ref[...] = _rope(dq_sc[:, LANES:], cos_ref[...], sin_ref[...], transpose=True).astype(BF16)

    whole = lambda col: pl.BlockSpec((T, LANES), col)
    tile = lambda col: pl.BlockSpec((tk2, LANES), col)
    return pl.pallas_call(
        body, name="attn_bwd", grid=(H, ng),
        in_specs=[whole(lambda h, g: (0, h)), whole(lambda h, g: (0, H + h)),
                  tile(lambda h, g: (g, h)), tile(lambda h, g: (g, 0)), tile(lambda h, g: (g, H + h)),
                  whole(lambda h, g: (0, h)), whole(lambda h, g: (0, h)),
                  pl.BlockSpec((1, T, 1), lambda h, g: (h, 0, 0)),
                  whole(lambda h, g: (0, 0)), whole(lambda h, g: (0, 0))],
        out_specs=[whole(lambda h, g: (0, h)), whole(lambda h, g: (0, h)),
                   tile(lambda h, g: (g, h)), tile(lambda h, g: (g, h)), tile(lambda h, g: (g, h))],
        out_shape=[jax.ShapeDtypeStruct((T, HV), BF16), jax.ShapeDtypeStruct((T, HV), BF16),
                   jax.ShapeDtypeStruct((T, HV), BF16), jax.ShapeDtypeStruct((T, HV), BF16),
                   jax.ShapeDtypeStruct((T, HV), BF16)],
        scratch_shapes=[pltpu.VMEM((T, 2 * LANES), F32), pltpu.VMEM((tk2, 2 * LANES), F32),
                        pltpu.VMEM((tk2, LANES), F32)],
        compiler_params=_params(2),
    )(q_all, q_all, kv, kr, kv, do, o, lse, cos, sin)


def _key_rope_bwd(dkr_heads, cos, sin, H, tm):
    T, HV = dkr_heads.shape

    def body(dkr_ref, cos_ref, sin_ref, dk_ref):
        dk = dkr_ref[...].astype(F32)
        tot = dk[:, 0:LANES]
        for h in range(1, H):
            tot = tot + dk[:, h * LANES:(h + 1) * LANES]
        dk_ref[...] = _rope(tot, cos_ref[...], sin_ref[...], transpose=True)

    return pl.pallas_call(
        body, name="key_rope_bwd", grid=(T // tm,),
        in_specs=[pl.BlockSpec((tm, HV), lambda i: (i, 0)),
                  pl.BlockSpec((tm, LANES), lambda i: (i, 0)), pl.BlockSpec((tm, LANES), lambda i: (i, 0))],
        out_specs=pl.BlockSpec((tm, LANES), lambda i: (i, 0)),
        out_shape=jax.ShapeDtypeStruct((T, LANES), F32), compiler_params=_params(1),
    )(dkr_heads, cos, sin)


def _place():
    x, y, c = lax.axis_index("x"), lax.axis_index("y"), lax.axis_index("c")
    chips = [(1 - x, y), (x, 1 - y), (1 - x, 1 - y)]
    return x, y, c, chips


def _all_gather_chips(arrs):
    n = len(arrs)
    halves = [a.shape[0] // 2 for a in arrs]
    assert all(h * 2 == a.shape[0] and h % (2 * BF16_ROWS) == 0 for h, a in zip(halves, arrs))
    per = 8

    def body(*refs):
        w_refs, out_refs = refs[:n], refs[n:2 * n]
        send_sems, recv_sems, local_sems = refs[2 * n:]
        x, y, c, _ = _place()
        sibling = (x, y, 1 - c)
        xn, yn, dg = (1 - x, y), (x, 1 - y), (1 - x, 1 - y)
        chains = []
        for w, (w_ref, out_ref, half) in enumerate(zip(w_refs, out_refs, halves)):
            def rows(chip, pc, quarter=None, out_ref=out_ref, half=half):
                start = pc * half if quarter is None else pc * half + quarter * (half // 2)
                return out_ref.at[2 * chip[0] + chip[1], pl.ds(start, half if quarter is None else half // 2), :]

            def copy(k, region, to, src=None, w=w):
                return pltpu.make_async_remote_copy(
                    src_ref=region if src is None else src, dst_ref=region,
                    send_sem=send_sems.at[per * w + k], recv_sem=recv_sems.at[per * w + k],
                    device_id=to, device_id_type=MESH)

            mine = pltpu.make_async_copy(w_ref, out_ref.at[2 * x + y], local_sems.at[w])
            mine.start()
            my_half = w_ref.at[pl.ds(c * half, half), :]
            own = [copy(0, rows((x, y), c), (*xn, c), src=my_half), copy(1, rows((x, y), c), (*yn, c), src=my_half)]
            for cp in own:
                cp.start()
            chains.append((rows, copy, mine, own))
        for rows, copy, mine, own in chains:
            landed = [rows(xn, c), rows(yn, c), rows(dg, c, 0), rows(dg, c, 1)]
            sent = list(own)
            for k, region in enumerate(landed):
                copy(k, region, (x, y, c)).wait_recv()
                if k == 0:
                    sent.append(copy(2, rows(xn, c, 0), (*yn, c)))
                elif k == 1:
                    sent.append(copy(3, rows(yn, c, 1), (*xn, c)))
                if k < 2:
                    sent[-1].start()
                sent.append(copy(4 + k, region, sibling))
                sent[-1].start()
            for k, region in enumerate([rows(xn, 1 - c), rows(yn, 1 - c), rows(dg, 1 - c, 0), rows(dg, 1 - c, 1)]):
                copy(4 + k, region, (x, y, c)).wait_recv()
            for cp in sent:
                cp.wait_send()
            mine.wait()

    return pl.pallas_call(
        body, name="gather_weights", in_specs=[ANY] * n, out_specs=[ANY] * n,
        out_shape=[jax.ShapeDtypeStruct((N_CHIPS,) + a.shape, a.dtype) for a in arrs],
        scratch_shapes=[pltpu.SemaphoreType.DMA((per * n,)), pltpu.SemaphoreType.DMA((per * n,)),
                        pltpu.SemaphoreType.DMA((n,))],
    )(*arrs)


def _swap_cores(name, parts):
    n = len(parts)

    def body(*refs):
        p_refs, r_refs = refs[:n], refs[n:2 * n]
        send_sems, recv_sems = refs[2 * n:]
        x, y, c, _ = _place()
        copies = [pltpu.make_async_remote_copy(
            src_ref=p_refs[w], dst_ref=r_refs[w], send_sem=send_sems.at[w], recv_sem=recv_sems.at[w],
            device_id=(x, y, 1 - c), device_id_type=MESH) for w in range(n)]
        for cp in copies:
            cp.start()
        for cp in copies:
            cp.wait()

    return pl.pallas_call(
        body, name=name, in_specs=[ANY] * n, out_specs=[ANY] * n,
        out_shape=[jax.ShapeDtypeStruct(p.shape, p.dtype) for p in parts],
        scratch_shapes=[pltpu.SemaphoreType.DMA((n,)), pltpu.SemaphoreType.DMA((n,))],
    )(*parts)


HBM = pl.BlockSpec(memory_space=pltpu.HBM)
SEM = pl.BlockSpec(memory_space=pltpu.SEMAPHORE)
EFFECT = pltpu.SideEffectType.DATAFLOW_SIDE_EFFECTING


def _push_copies(a_refs, l_refs, send_sems, recv_sems, by_target):
    x, y, c, chips = _place()
    me = 2 * x + y

    def part(a_ref, q):
        if by_target == 'cols':
            n = a_ref.shape[-1] // N_CHIPS
            return a_ref.at[(slice(None),) * (len(a_ref.shape) - 1) + (pl.ds(pl.multiple_of(q * n, LANES), n),)]
        return a_ref.at[q] if by_target else a_ref

    out = []
    for w, (a_ref, l_ref) in enumerate(zip(a_refs, l_refs)):
        for j, (px, py) in enumerate(chips):
            peer = 2 * px + py
            out.append((
                pltpu.make_async_remote_copy(
                    src_ref=part(a_ref, peer), dst_ref=l_ref.at[me],
                    send_sem=send_sems.at[3 * w + j], recv_sem=recv_sems.at[3 * w + j],
                    device_id=(px, py, c), device_id_type=MESH),
                pltpu.make_async_remote_copy(
                    src_ref=part(a_ref, me), dst_ref=l_ref.at[peer],
                    send_sem=send_sems.at[3 * w + j], recv_sem=recv_sems.at[3 * w + j],
                    device_id=(px, py, c), device_id_type=MESH)))
    return out


def _landing_shape(a, by_target):
    if by_target == 'cols':
        return (N_CHIPS,) + a.shape[:-1] + (a.shape[-1] // N_CHIPS,)
    return (N_CHIPS,) + (a.shape[1:] if by_target else a.shape)


def _push_start(name, arrs, by_target):
    n = len(arrs)
    lands = [lax.empty(_landing_shape(a, by_target), a.dtype) for a in arrs]

    def body(*refs):
        a_refs, l_refs = refs[:n], refs[n:2 * n]
        send_sems, recv_sems = refs[2 * n], refs[2 * n + 1]
        token = refs[-1]
        for send, _ in _push_copies(a_refs, l_refs, send_sems, recv_sems, by_target):
            send.start()
        token[...] = jnp.zeros_like(token)

    res = pl.pallas_call(
        body, name=name,
        out_shape=(pltpu.SemaphoreType.DMA((3 * n,)), pltpu.SemaphoreType.DMA((3 * n,)),
                   *[pltpu.HBM(a.shape, a.dtype) for a in arrs], *[pltpu.HBM(l.shape, l.dtype) for l in lands],
                   jax.ShapeDtypeStruct((8, LANES), F32)),
        in_specs=[HBM] * (2 * n), out_specs=(SEM, SEM, *[HBM] * (2 * n), pl.BlockSpec(memory_space=pltpu.VMEM)),
        input_output_aliases={i: 2 + i for i in range(2 * n)},
        compiler_params=pltpu.CompilerParams(has_side_effects=EFFECT),
    )(*[pltpu.with_memory_space_constraint(a, pltpu.HBM) for a in list(arrs) + lands])
    return res[0], res[1], list(res[2:2 + n]), list(res[2 + n:2 + 2 * n]), res[-1]


def _push_wait(name, send_sems, recv_sems, arrs, lands, after, by_target):
    n = len(arrs)

    def body(*refs):
        a_refs, l_refs = refs[:n], refs[n:2 * n]
        s_sems, r_sems = refs[2 * n], refs[2 * n + 1]
        for send, recv in _push_copies(a_refs, l_refs, s_sems, r_sems, by_target):
            send.wait_send()
            recv.wait_recv()

    res = pl.pallas_call(
        body, name=name,
        out_shape=[pltpu.HBM(a.shape, a.dtype) for a in list(arrs) + list(lands)],
        in_specs=[HBM] * (2 * n) + [SEM, SEM] + [ANY] * len(after), out_specs=[HBM] * (2 * n),
        input_output_aliases={i: i for i in range(2 * n)},
        compiler_params=pltpu.CompilerParams(has_side_effects=EFFECT),
    )(*arrs, *lands, send_sems, recv_sems, *after)
    return list(res[:n]), list(res[n:])


def _all_reduce_small(part):
    def body(p_ref, out_ref, sib_buf, chip_buf, send_sems, recv_sems):
        x, y, c, chips = _place()
        me = 2 * x + y
        swap = pltpu.make_async_remote_copy(
            src_ref=p_ref, dst_ref=sib_buf, send_sem=send_sems.at[0], recv_sem=recv_sems.at[0],
            device_id=(x, y, 1 - c), device_id_type=MESH)
        swap.start()
        swap.wait()
        chip_buf[me] = p_ref[...] + sib_buf[...]
        copies = []
        for j, (px, py) in enumerate(chips):
            cp = pltpu.make_async_remote_copy(
                src_ref=chip_buf.at[me], dst_ref=chip_buf.at[me], send_sem=send_sems.at[1 + j],
                recv_sem=recv_sems.at[1 + j], device_id=(px, py, c), device_id_type=MESH)
            cp.start()
            copies.append(cp)
        for j, (px, py) in enumerate(chips):
            pltpu.make_async_remote_copy(
                src_ref=chip_buf.at[me], dst_ref=chip_buf.at[2 * px + py], send_sem=send_sems.at[1 + j],
                recv_sem=recv_sems.at[1 + j], device_id=(px, py, c), device_id_type=MESH).wait_recv()
        for cp in copies:
            cp.wait_send()
        tot = chip_buf[0]
        for q in range(1, N_CHIPS):
            tot = tot + chip_buf[q]
        out_ref[...] = tot

    vm = pl.BlockSpec(memory_space=pltpu.VMEM)
    return pl.pallas_call(
        body, name="all_reduce_small", in_specs=[vm], out_specs=vm,
        out_shape=jax.ShapeDtypeStruct(part.shape, F32),
        scratch_shapes=[pltpu.VMEM(part.shape, F32), pltpu.VMEM((N_CHIPS,) + part.shape, F32),
                        pltpu.SemaphoreType.DMA((N_CHIPS,)), pltpu.SemaphoreType.DMA((N_CHIPS,))],
    )(part)


def _row_tiles(shape):
    ax = next(d for d, s in enumerate(shape) if s > 1)
    tr = _gcd(ADAM_ROWS, shape[ax])
    block = tuple(tr if d == ax else s for d, s in enumerate(shape))
    return shape[ax] // tr, block, lambda i: tuple(i if d == ax else 0 for d in range(len(shape)))


def _sum_chips(name, landed, sent, chip, by_target):
    shape = landed.shape[1:]
    steps, block, index = _row_tiles(shape)
    if by_target == 'cols':
        own_spec = pl.BlockSpec(block, lambda i, c: index(i)[:-1] + (c[0],))
    else:
        own_spec = pl.BlockSpec((1,) + block, lambda i, c: (c[0],) + index(i))

    def body(chip_ref, l_ref, s_ref, o_ref):
        own = (s_ref[...] if by_target == 'cols' else s_ref[0]).astype(F32)
        tot = None
        for q in range(N_CHIPS):
            term = jnp.where(chip_ref[0] == q, own, l_ref[q].astype(F32))
            tot = term if tot is None else tot + term
        o_ref[...] = tot

    return pl.pallas_call(
        body, name=name,
        grid_spec=pltpu.PrefetchScalarGridSpec(
            num_scalar_prefetch=1, grid=(steps,),
            in_specs=[pl.BlockSpec((N_CHIPS,) + block, lambda i, c: (0,) + index(i)), own_spec],
            out_specs=pl.BlockSpec(block, lambda i, c: index(i))),
        out_shape=jax.ShapeDtypeStruct(shape, F32), compiler_params=_params(1),
    )(chip, landed, sent)


def _adamw_math(g, w, m, v):
    mn = ADAM_B1 * m + (1.0 - ADAM_B1) * g
    vn = ADAM_B2 * v + (1.0 - ADAM_B2) * jnp.square(g)
    m_hat = mn / (1.0 - ADAM_B1 ** ADAM_STEP)
    v_hat = vn / (1.0 - ADAM_B2 ** ADAM_STEP)
    return -ADAM_LR * (m_hat / (jnp.sqrt(v_hat) + ADAM_EPS) + ADAM_WD * w), mn, vn


def _adamw(name, g_parts, w, m, v):
    steps, block, index = _row_tiles(w.shape)
    n = len(g_parts)

    def body(*refs):
        g = refs[0][...]
        for r in refs[1:n]:
            g = g + r[...]
        w_ref, m_ref, v_ref, go_ref, d_ref, mo_ref, vo_ref = refs[n:]
        go_ref[...] = g
        d_ref[...], mo_ref[...], vo_ref[...] = _adamw_math(g, w_ref[...], m_ref[...], v_ref[...])

    spec = pl.BlockSpec(block, index)
    return pl.pallas_call(
        body, name=name, grid=(steps,), in_specs=[spec] * (n + 3), out_specs=[spec] * 4,
        out_shape=[jax.ShapeDtypeStruct(w.shape, F32)] * 4, compiler_params=_params(1),
    )(*g_parts, w, m, v)


def _adamw_vectors(items):
    n = len(items)

    def body(*refs):
        ins, outs = refs[:4 * n], refs[4 * n:]
        for k in range(n):
            g, w, m, v = (r[...] for r in ins[4 * k:4 * k + 4])
            outs[3 * k][...], outs[3 * k + 1][...], outs[3 * k + 2][...] = _adamw_math(g, w, m, v)

    vm = pl.BlockSpec(memory_space=pltpu.VMEM)
    res = pl.pallas_call(
        body, name="adamw_vectors", in_specs=[vm] * (4 * n), out_specs=[vm] * (3 * n),
        out_shape=[jax.ShapeDtypeStruct(it[1].shape, F32) for it in items for _ in range(3)],
    )(*[a for it in items for a in it])
    return [res[3 * k:3 * k + 3] for k in range(n)]


def _pack_rows(flat, dtype, multiple):
    n = flat.shape[0]
    total = -(-n // multiple) * multiple
    return jnp.pad(flat, (0, total - n)).astype(dtype).reshape(total // LANES, LANES)


def kernel(x, positions, ln_g, ln_b, a_w_in, a_b_in, a_conv_w, a_conv_b, a_norm_g, a_norm_b, a_w_out, a_b_out, kv_w_down, kv_norm_g, kv_w_uk, kv_w_uv, b_w_in, b_q_norm_g, b_w_uq, b_w_out, loss_target, m_ln_g, m_ln_b, m_a_w_in, m_a_b_in, m_a_conv_w, m_a_conv_b, m_a_norm_g, m_a_norm_b, m_a_w_out, m_a_b_out, m_kv_w_down, m_kv_norm_g, m_kv_w_uk, m_kv_w_uv, m_b_w_in, m_b_q_norm_g, m_b_w_uq, m_b_w_out, v_ln_g, v_ln_b, v_a_w_in, v_a_b_in, v_a_conv_w, v_a_conv_b, v_a_norm_g, v_a_norm_b, v_a_w_out, v_a_b_out, v_kv_w_down, v_kv_norm_g, v_kv_w_uk, v_kv_w_uv, v_b_w_in, v_b_q_norm_g, v_b_w_uq, v_b_w_out):
    T, D = x.shape[1], x.shape[2]
    E = N_CHIPS * a_w_out.shape[1]
    RKV = kv_norm_g.shape[0]
    H, DN = kv_w_uk.shape[1], kv_w_uk.shape[2]
    RQ = b_q_norm_g.shape[1]
    HV = N_CHIPS * b_w_out.shape[1]
    assert DN == LANES and kv_w_uv.shape[2] == LANES and HV == H * LANES
    assert kv_w_down.shape[1] == RKV + ROPE_DIM and b_w_uq.shape[3] == DN + ROPE_DIM
    assert ln_g.shape[0] == 2 and a_w_in.shape[0] == 1 and b_w_in.shape[0] == 1
    alpha = (2.0 * ln_g.shape[0]) ** 0.25
    scale = 1.0 / math.sqrt(DN + ROPE_DIM)
    WK = -(-(RKV + LANES) // 256) * 256
    assert WK % RQ == 0
    Z_OFF = WK + RQ
    tmw, tq = min(TM_WIDE, T), min(TQ, T)
    t512, t1024 = _fit(512, T), _fit(1024, T)
    xs = x[0]
    tgt = loss_target[0]
    px, py = lax.axis_index("x"), lax.axis_index("y")
    chip = 2 * px + py

    mats = [a_w_out[0], kv_w_down, kv_w_uk, kv_w_uv, b_w_in[0], b_w_uq[0], b_w_out[0]]
    vecs = [a_b_in[0], a_conv_w[0], a_conv_b[0], a_norm_g[0], a_norm_b[0], a_b_out[0]]
    vec_bits = jnp.concatenate([lax.bitcast_convert_type(w.reshape(-1), BF16).reshape(-1) for w in vecs])
    rest = [w.astype(BF16) for w in mats]
    g_win, gathered = _all_gather_chips(
        [a_w_in[0].astype(BF16), _pack_rows(vec_bits, BF16, 4 * BF16_ROWS * LANES)])
    gathered = gathered.reshape(N_CHIPS, -1)
    gathered, rest = lax.optimization_barrier((gathered, rest))
    rest_sems = _push_start("gather_rest_start", rest, by_target=False)
    off = 0
    fvec = []
    for w in vecs:
        bits = gathered[:, off:off + 2 * w.size].reshape((N_CHIPS,) + w.shape + (2,))
        fvec.append(lax.bitcast_convert_type(bits, F32))
        off += 2 * w.size
    cols = lambda g: jnp.moveaxis(g, 0, -2).reshape(g.shape[1:-1] + (N_CHIPS * g.shape[-1],))
    b_in = cols(fvec[0][:, None, :])
    conv_w = cols(fvec[1])
    conv_b, norm_g, norm_b, b_out = (cols(f[:, None, :]) for f in fvec[2:])
    row = lambda a: a.reshape(1, -1)
    g0, b0, g1, b1 = row(ln_g[0]), row(ln_b[0]), row(ln_g[1]), row(ln_b[1])
    kv_g, q_g = row(kv_norm_g), row(b_q_norm_g[0])
    plain = lambda acc, ins, i, j: [acc]

    b_in = b_in + rest_sems[4][0, 0]
    (proj,) = _row_mm("a_in", [((xs,), None)], g_win, nt=False, tm=t1024, tn=3 * E // N_CHIPS, tk=D,
                      outs=[((T, 3 * E), F32, 'tile')], epi=lambda acc, ins, i, j: [acc + ins[0]],
                      epi_ins=[(b_in, 'col')])
    u1, u4 = _conv_fwd(proj, conv_w, conv_b, norm_g, norm_b, E, tmw)

    rest, landed = _push_wait("gather_rest_wait", *rest_sems[:4], after=[u4], by_target=False)
    g_wout, g_wd, g_uk, g_uv, g_wbin, g_wuq, g_wbout = [
        lax.dynamic_update_slice(l, w[None], (chip,) + (0,) * w.ndim) for w, l in zip(rest, landed)]
    w_out = g_wout.reshape(E, D)
    wd = g_wd.reshape(D, RKV + ROPE_DIM)
    zpad = jnp.zeros((D, ROPE_HALF), BF16)
    wd_p = jnp.concatenate(
        [wd[:, :RKV], wd[:, RKV:RKV + ROPE_HALF], zpad, wd[:, RKV + ROPE_HALF:], zpad,
         jnp.zeros((D, WK - RKV - LANES), BF16)], axis=1)
    w_bin = cols(g_wbin)
    w_z = w_bin[:, RQ:]
    wb_small = jnp.concatenate([wd_p, w_bin[:, :RQ]], axis=1)
    wb_all = jnp.concatenate([wd_p, w_bin], axis=1)
    w_kv = jnp.concatenate([g_uk.reshape(RKV, HV), g_uv.reshape(RKV, HV)], axis=1)
    wuq = g_wuq.reshape(RQ, H, DN + ROPE_DIM)
    zq = jnp.zeros((RQ, H, ROPE_HALF), BF16)
    w_qr = jnp.concatenate([wuq[:, :, DN:DN + ROPE_HALF], zq, wuq[:, :, DN + ROPE_HALF:], zq], axis=2)
    w_q = jnp.concatenate([wuq[:, :, :DN].reshape(RQ, HV), w_qr.reshape(RQ, HV)], axis=1)
    w_bout = g_wbout.reshape(HV, D)

    freqs = ROPE_THETA ** (-jnp.arange(0, ROPE_DIM, 2, dtype=F32) / ROPE_DIM)
    ang = positions[0].astype(F32)[:, None] * freqs
    cs, sn = jnp.cos(ang), jnp.sin(ang)
    ones, zeros = jnp.ones_like(cs), jnp.zeros_like(cs)
    cos_t = jnp.concatenate([cs, ones, cs, ones], axis=1)
    sin_t = jnp.concatenate([-sn, zeros, sn, zeros], axis=1)

    def ln_epi(acc, ins, i, j):
        bias, res, g, b = ins
        xhat, rstd = _ln_stats(alpha * res + acc + bias)
        h = xhat * g + b
        return [h, h, xhat, rstd]

    h1, h1b, xhat1, rstd1 = _row_mm(
        "a_out", [((u4,), None)], w_out, nt=False, tm=t512, tn=D, tk=_fit(2048, E),
        outs=[((T, D), F32, 'tile'), ((T, D), BF16, 'tile'), ((T, D), F32, 'tile'), ((T, 1), F32, 'row')],
        epi=ln_epi, epi_ins=[(b_out, 'col'), (xs, 'tile'), (g0, 'col'), (b0, 'col')])

    tkb = _fit(512, _gcd(WK, RQ, HV))
    def latents_epi(acc, ins, i, j):
        kg, qg, cos_, sin_ = ins
        return [acc, _rms_fwd(acc[:, :RKV], kg), _rope(acc[:, RKV:RKV + LANES], cos_, sin_),
                _rms_fwd(acc[:, WK:WK + RQ], qg)]

    whole_row = lambda a: (a, pl.BlockSpec(a.shape, lambda i, j, k: (0, 0)))
    pb, c_lat, kr, cqn = _row_mm(
        "b_in", [((h1b,), None)], wb_small, nt=False, tm=t512, tn=Z_OFF, tk=_fit(1024, D),
        outs=[((T, Z_OFF), F32, 'tile'), ((T, RKV), BF16, 'row'), ((T, LANES), BF16, 'row'),
              ((T, RQ), BF16, 'row')],
        epi=latents_epi, epi_ins=[whole_row(kv_g), whole_row(q_g), (cos_t, 'row'), (sin_t, 'row')])
    (zb,) = _row_mm("b_in_gate", [((h1b,), None)], w_z, nt=False, tm=t1024, tn=_fit(2048, HV),
                    tk=_fit(1024, D), outs=[((T, HV), BF16, 'tile')], epi=plain)
    (kv,) = _row_mm("kv_up", [((c_lat,), None)], w_kv, nt=False, tm=t1024, tn=_fit(2048, HV),
                    tk=_fit(1024, RKV), outs=[((T, 2 * HV), BF16, 'tile')], epi=plain)
    tnq = _fit(2048, HV)
    half_q = HV // tnq

    def q_epi(acc, ins, i, j):
        return [jnp.where(j >= half_q, _rope(acc, ins[0], ins[1]), acc)]

    (q_all,) = _row_mm("q_up", [((cqn,), None)], w_q, nt=False, tm=t1024, tn=tnq, tk=_fit(1024, RQ),
                       outs=[((T, 2 * HV), BF16, 'tile')], epi=q_epi,
                       epi_ins=[(cos_t, 'row'), (sin_t, 'row')])
    o, lse = _attn_fwd(q_all, kv, kr, H, tq, scale)

    def loss_epi(acc, ins, i, j):
        res, g, b, target = ins
        xhat, rstd = _ln_stats(alpha * res + acc)
        diff = xhat * g + b - target
        dr, dg, db = _ln_bwd(diff / D, xhat, rstd, g)
        return [dr, 0.5 * jnp.sum(diff * diff, keepdims=True) / D, dg, db]

    dr1, loss_part, dg1, db1 = _row_mm(
        "b_out", [((o, zb), _gate)], w_bout, nt=False, tm=t512, tn=D, tk=_fit(2048, HV),
        outs=[((T, D), F32, 'tile'), ((1, 1), F32, 'acc'), ((1, D), F32, 'acc'), ((1, D), F32, 'acc')],
        epi=loss_epi, epi_ins=[(h1, 'tile'), (g1, 'col'), (b1, 'col'), (tgt, 'tile')])

    def gate_bwd_epi(acc, ins, i, j):
        gate, gate_grad = _silu_and_grad(ins[1].astype(F32))
        return [acc * gate, acc * ins[0].astype(F32) * gate_grad]

    do, dz = _row_mm(
        "b_out_bwd", [((dr1,), None)], w_bout, nt=True, tm=t512, tn=_fit(2048, HV), tk=_fit(1024, D),
        outs=[((T, HV), BF16, 'tile'), ((T, HV), BF16, 'tile')], epi=gate_bwd_epi,
        epi_ins=[(o, 'tile'), (zb, 'tile')])
    gw_bout = _tn_mm("dw_b_out", (o, zb), _gate, [((dr1,), None)], tn=_fit(1024, D), tk=t512, out_dtype=BF16)
    dqn, dqr_pre, dkn, dkr_h, dv = _attn_bwd(q_all, kv, kr, do, o, lse, cos_t, sin_t, H, tq, scale)
    dkr_pre = _key_rope_bwd(dkr_h, cos_t, sin_t, H, t512)

    def cq_bwd_epi(acc, ins, i, j):
        dx, dg = _rms_bwd(acc, ins[0], ins[1])
        return [dx, dg]

    dcq, dqg = _row_mm(
        "q_up_bwd", [((dqn,), None), ((dqr_pre,), None)], w_q, nt=True, tm=t1024, tn=RQ, tk=_fit(2048, HV),
        outs=[((T, RQ), BF16, 'tile'), ((1, RQ), F32, 'acc')], epi=cq_bwd_epi,
        epi_ins=[(pb, pl.BlockSpec((t1024, RQ), lambda i, j, k: (i, WK // RQ))), (q_g, 'col')])
    gw_q = _tn_mm("dw_q_up", (cqn,), None, [((dqn,), None), ((dqr_pre,), None)],
                  tn=_fit(2048, HV), tk=t1024, out_dtype=BF16)

    def ckv_bwd_epi(acc, ins, i, j):
        blk, dkr_t, g = ins
        dx, dg = _rms_bwd(acc, blk[:, :RKV], g)
        parts = [dx, dkr_t]
        if WK > RKV + LANES:
            parts.append(jnp.zeros((dx.shape[0], WK - RKV - LANES), F32))
        return [jnp.concatenate(parts, axis=1), dg]

    dckv, dkvg = _row_mm(
        "kv_up_bwd", [((dkn,), None), ((dv,), None)], w_kv, nt=True, tm=t1024, tn=RKV, tk=_fit(2048, HV),
        outs=[((T, WK), BF16, pl.BlockSpec((t1024, WK), lambda i, j, k: (i, 0))), ((1, RKV), F32, 'acc')],
        epi=ckv_bwd_epi,
        epi_ins=[(pb, pl.BlockSpec((t1024, WK), lambda i, j, k: (i, 0))), (dkr_pre, 'row'), (kv_g, 'col')])
    gw_kv = _tn_mm("dw_kv_up", (c_lat,), None, [((dkn,), None), ((dv,), None)],
                   tn=_fit(2048, HV), tk=t1024, out_dtype=BF16)

    def ln1_bwd_epi(acc, ins, i, j):
        dr_up, xhat, rstd, g = ins
        dr, dg, db = _ln_bwd(alpha * dr_up + acc, xhat, rstd, g)
        return [dr, dg, db]

    dp_segs = [((dckv,), None), ((dcq,), None), ((dz,), None)]
    gw_lat = _tn_mm("dw_b_in", (h1b,), None, dp_segs[:2], tn=tkb, tk=t1024, out_dtype=BF16)
    gw_z = _tn_mm("dw_b_in_gate", (h1b,), None, dp_segs[2:], tn=_fit(2048, HV), tk=t1024, out_dtype=BF16)

    shard_cols = lambda g: jnp.moveaxis(g.reshape(g.shape[0], N_CHIPS, -1), 1, 0)
    gq = gw_q.reshape(RQ, 2, H, LANES)
    g_uq = jnp.concatenate(
        [gq[:, 0], gq[:, 1, :, :ROPE_HALF], gq[:, 1, :, 2 * ROPE_HALF:3 * ROPE_HALF]], axis=2)
    g_wd_full = jnp.concatenate(
        [gw_lat[:, :RKV], gw_lat[:, RKV:RKV + ROPE_HALF],
         gw_lat[:, RKV + 2 * ROPE_HALF:RKV + 3 * ROPE_HALF]], axis=1)
    late_names = ["kv_w_down", "kv_w_uk", "kv_w_uv", "b_w_in", "b_w_uq", "b_w_out"]
    late_w = [kv_w_down, kv_w_uk, kv_w_uv, b_w_in, b_w_uq, b_w_out]
    gw_bin = jnp.concatenate([gw_lat[:, WK:], gw_z], axis=1)
    chip_major = [g_wd_full, gw_kv[:, :HV], gw_kv[:, HV:], shard_cols(gw_bin), g_uq, gw_bout]
    late_grads = [g.reshape((N_CHIPS,) + w.shape) for g, w in zip(chip_major, late_w)]
    late_sems = _push_start("scatter_late_start", late_grads, by_target=True)

    dr0, dg0, db0 = _row_mm(
        "b_in_bwd", dp_segs, wb_all, nt=True, tm=t1024, tn=D, tk=tkb,
        outs=[((T, D), F32, 'tile'), ((1, D), F32, 'acc'), ((1, D), F32, 'acc')], epi=ln1_bwd_epi,
        epi_ins=[(dr1, 'tile'), (xhat1, 'tile'), (rstd1, 'row'), (g0 + late_sems[4][0, 0], 'col')])

    def conv_branch_bwd_epi(acc, ins, i, j):
        u1_t, z, g, b = ins
        xhat, rstd = _ln_stats(u1_t)
        u2 = xhat * g + b
        gate, gate_grad = _silu_and_grad(z)
        act, act_grad = _silu_and_grad(u2)
        dz_a = acc * act * gate_grad
        du1, dg, db = _ln_bwd(acc * gate * act_grad, xhat, rstd, g)
        return [du1, dz_a, dg, db]

    du1, dz_a, dng, dnb = _row_mm(
        "a_out_bwd", [((dr0,), None)], w_out, nt=True, tm=tmw, tn=E, tk=_fit(1024, D),
        outs=[((T, E), F32, 'tile'), ((T, E), BF16, 'tile'), ((1, E), F32, 'acc'), ((1, E), F32, 'acc')],
        epi=conv_branch_bwd_epi,
        epi_ins=[(u1, 'tile'), (proj, pl.BlockSpec((tmw, E), lambda i, j, k: (i, 2))), (norm_g, 'col'),
                 (norm_b, 'col')])
    gw_out, dbo = _tn_mm("dw_a_out", (u4,), None, [((dr0,), None)], tn=_fit(1024, D), tk=t1024,
                         out_dtype=BF16, colsum=True)
    mid_sems = _push_start("scatter_mid_start", [gw_out.reshape((N_CHIPS,) + a_w_out.shape)], by_target=True)
    dval, dgate, dcw, dcb = _conv_bwd(du1, proj, conv_w + mid_sems[4][0, 0], E, tmw)
    dproj_segs = [((dval,), None), ((dgate,), None), ((dz_a,), None)]
    gw_in, dbi = _tn_mm("dw_a_in", (xs,), None, dproj_segs, tn=_fit(2048, E), tk=t1024, out_dtype=BF16,
                        colsum=True)

    chip_word = chip.reshape(1).astype(jnp.int32)

    def reduce_and_update(tag, names_, sent, landed, w_, m_, v_, by_target=True):
        sums = [_sum_chips("sum_" + n, l, s, chip_word, by_target) for n, s, l in zip(names_, sent, landed)]
        theirs = _swap_cores("swap_cores_" + tag, sums)
        return {n: _adamw("adamw_" + n, [mine, other], w, m, v)
                for n, mine, other, w, m, v in zip(names_, sums, theirs, w_, m_, v_)}

    late_sent, late_landed = _push_wait("scatter_late_wait", *late_sems[:4], after=[dbi], by_target=True)
    mid_sent, mid_landed = _push_wait("scatter_mid_wait", *mid_sems[:4], after=[dbi], by_target=True)
    early_sems = _push_start("scatter_early_start", [gw_in.reshape(a_w_in.shape[:-1] + (3 * E,))],
                             by_target='cols')
    (grad_x,) = _row_mm(
        "a_in_bwd", dproj_segs, g_win, nt=True, tm=t512, tn=D, tk=E, b_whole=True,
        outs=[((T, D), F32, 'tile')], epi=lambda acc, ins, i, j: [alpha * ins[0] + acc + ins[1]],
        epi_ins=[(dr0, 'tile'), (jnp.zeros((1, D), F32) + early_sems[4][0, 0], 'col')])
    big_out = reduce_and_update(
        "late", ["a_w_out"] + late_names, mid_sent + late_sent, mid_landed + late_landed,
        [a_w_out] + late_w, [m_a_w_out, m_kv_w_down, m_kv_w_uk, m_kv_w_uv, m_b_w_in, m_b_w_uq, m_b_w_out],
        [v_a_w_out, v_kv_w_down, v_kv_w_uk, v_kv_w_uv, v_b_w_in, v_b_w_uq, v_b_w_out])

    small_full = [jnp.concatenate([dg0, dg1]), jnp.concatenate([db0, db1]), dbi, dcw, dcb, dng, dnb, dbo,
                  dkvg, dqg, loss_part]
    sflat = jnp.concatenate([g.reshape(-1) for g in small_full])
    summed = _all_reduce_small(_pack_rows(sflat, F32, 8 * LANES)).reshape(-1)
    soff = 0
    sgrads = []
    for g in small_full:
        sgrads.append(summed[soff:soff + g.size].reshape(g.shape))
        soff += g.size
    loss = sgrads.pop()[0, 0]
    local_cols = lambda g, n: lax.dynamic_slice_in_dim(g, chip * n, n, axis=g.ndim - 1)
    snames = ["ln_g", "ln_b", "a_b_in", "a_conv_w", "a_conv_b", "a_norm_g", "a_norm_b", "a_b_out",
              "kv_norm_g", "b_q_norm_g"]
    small_w = [ln_g, ln_b, a_b_in, a_conv_w, a_conv_b, a_norm_g, a_norm_b, a_b_out, kv_norm_g, b_q_norm_g]
    small_m = [m_ln_g, m_ln_b, m_a_b_in, m_a_conv_w, m_a_conv_b, m_a_norm_g, m_a_norm_b, m_a_b_out,
               m_kv_norm_g, m_b_q_norm_g]
    small_v = [v_ln_g, v_ln_b, v_a_b_in, v_a_conv_w, v_a_conv_b, v_a_norm_g, v_a_norm_b, v_a_b_out,
               v_kv_norm_g, v_b_q_norm_g]
    sharded = {"a_b_in", "a_conv_w", "a_conv_b", "a_norm_g", "a_norm_b", "a_b_out"}
    local_g = [(local_cols(g, w.shape[-1]) if n in sharded else g).reshape(w.shape)
               for n, g, w in zip(snames, sgrads, small_w)]
    at_least_2d = lambda a: a.reshape((1,) + a.shape) if a.ndim == 1 else a
    sres = _adamw_vectors([tuple(at_least_2d(a) for a in item)
                           for item in zip(local_g, small_w, small_m, small_v)])
    small_out = {n: [g] + [r.reshape(w.shape) for r in res]
                 for n, g, w, res in zip(snames, local_g, small_w, sres)}

    early_sent, early_landed = _push_wait(
        "scatter_early_wait", *early_sems[:4], by_target='cols',
        after=[grad_x, big_out["b_w_out"][1], small_out["b_q_norm_g"][1]])
    big_out.update(reduce_and_update("early", ["a_w_in"], early_sent, early_landed,
                                     [a_w_in], [m_a_w_in], [v_a_w_in], by_target='cols'))

    order =["ln_g", "ln_b", "a_w_in", "a_b_in", "a_conv_w", "a_conv_b", "a_norm_g", "a_norm_b", "a_w_out",
             "a_b_out", "kv_w_down", "kv_norm_g", "kv_w_uk", "kv_w_uv", "b_w_in", "b_q_norm_g", "b_w_uq",
             "b_w_out"]
    outs = {**big_out, **small_out}
    result = [loss, grad_x[None]]
    for part in range(4):
        result += [outs[n][part] for n in order]
    return tuple(result)
```

```python
import functools
import math

import jax
import jax.numpy as jnp
from jax import lax
from jax.experimental import pallas as pl
from jax.experimental.pallas import tpu as pltpu

F32, BF16 = jnp.float32, jnp.bfloat16
NN = (((1,), (0,)), ((), ()))
NT = (((1,), (1,)), ((), ()))
TN = (((0,), (0,)), ((), ()))
MESH = pl.DeviceIdType.MESH
ANY = pl.BlockSpec(memory_space=pl.ANY)

LANES = 128
BF16_ROWS = 16
VMEM_LIMIT = 56 * 1024 * 1024
N_CHIPS = 4
LN_EPS = 1e-5
RMS_EPS = 1e-6
MASK_VALUE = -1e30
LOG2_E = math.log2(math.e)
ROPE_THETA = 10000.0
ROPE_DIM = 64
ROPE_HALF = ROPE_DIM // 2
ADAM_LR, ADAM_B1, ADAM_B2, ADAM_EPS, ADAM_WD, ADAM_STEP = 0.001, 0.9, 0.999, 1e-08, 0.01, 10

MAX_TILE = 2048
TM_WIDE = 256
TQ = 512
CONV_HALO = 32
CONV_LC = 1024
CONV_SUB = 256
SUBLANES = 8
CONV_RC = 32
ADAM_ROWS = 128


def _dot(a, b, dims):
    return lax.dot_general(a.astype(BF16), b.astype(BF16), dims, preferred_element_type=F32)


def _sig(x):
    return 0.5 * jnp.tanh(0.5 * x) + 0.5


def _params(n_axes):
    return pltpu.CompilerParams(dimension_semantics=("arbitrary",) * n_axes, vmem_limit_bytes=VMEM_LIMIT)


def _gcd(*v):
    return functools.reduce(math.gcd, v)


def _fit(want, dim):
    return math.gcd(min(want, MAX_TILE), dim)


def _row_mm(name, a_segs, b, *, nt, tm, tn, tk, outs, epi, epi_ins=(), b_whole=False):
    M = a_segs[0][0][0].shape[0]
    stacked = b.ndim == 3
    if stacked:
        n_blk = b.shape[2]
        N = b.shape[1] if nt else N_CHIPS * n_blk
        assert (nt and b_whole) or (not nt and tn == n_blk and tk == b.shape[1]), name
    else:
        N = b.shape[0] if nt else b.shape[1]
    nkb = [arrs[0].shape[1] // tk for arrs, _ in a_segs]
    koff = [sum(nkb[:s]) for s in range(len(nkb))]
    ni, nj, nk = M // tm, N // tn, sum(nkb)
    assert M % tm == 0 and N % tn == 0 and all(arrs[0].shape[1] % tk == 0 for arrs, _ in a_segs), name
    assert stacked or (b.shape[1] if nt else b.shape[0]) == nk * tk, name

    def spec_of(shape, kind):
        if isinstance(kind, pl.BlockSpec):
            return kind
        if kind == 'tile':
            return pl.BlockSpec((tm, tn), lambda i, j, k: (i, j))
        if kind == 'row':
            return pl.BlockSpec((tm, shape[1]), lambda i, j, k: (i, 0))
        if kind == 'col':
            return pl.BlockSpec((1, tn), lambda i, j, k: (0, j))
        assert kind == 'acc' and nj == 1, name
        return pl.BlockSpec(shape, lambda i, j, k: (0,) * len(shape))

    in_specs, operands = [], []
    for s, (arrs, _) in enumerate(a_segs):
        for arr in arrs:
            in_specs.append(pl.BlockSpec(
                (tm, tk), lambda i, j, k, s=s: (i, jnp.clip(k - koff[s], 0, nkb[s] - 1))))
            operands.append(arr)
    if b_whole:
        assert nt and nj == 1 and all(n == 1 for n in nkb), name
        in_specs.append(pl.BlockSpec(b.shape, lambda i, j, k: (0,) * b.ndim))
    elif stacked:
        in_specs.append(pl.BlockSpec((1, tk, tn), lambda i, j, k: (j, 0, 0)))
    else:
        in_specs.append(pl.BlockSpec((tn, tk), lambda i, j, k: (j, k)) if nt
                        else pl.BlockSpec((tk, tn), lambda i, j, k: (k, j)))
    operands.append(b)
    for arr, kind in epi_ins:
        in_specs.append(spec_of(arr.shape, kind))
        operands.append(arr)
    out_specs = [spec_of(shape, kind) for shape, _, kind in outs]
    out_shape = [jax.ShapeDtypeStruct(shape, dtype) for shape, dtype, _ in outs]
    n_seg_refs = [len(arrs) for arrs, _ in a_segs]

    def body(*refs):
        pos = 0
        seg_refs = []
        for n in n_seg_refs:
            seg_refs.append(refs[pos:pos + n])
            pos += n
        b_ref = refs[pos]
        e_refs = refs[pos + 1:pos + 1 + len(epi_ins)]
        o_refs = refs[pos + 1 + len(epi_ins):pos + 1 + len(epi_ins) + len(outs)]
        i, j, k = pl.program_id(0), pl.program_id(1), pl.program_id(2)

        def product(fn, rs, s=0):
            a = rs[0][...] if fn is None else fn(*[r[...] for r in rs])
            if b_whole and stacked:
                lo, hi, tot = koff[s] * tk, (koff[s] + 1) * tk, None
                for q in range(lo // n_blk, (hi - 1) // n_blk + 1):
                    c0, c1 = max(lo, q * n_blk), min(hi, (q + 1) * n_blk)
                    part = _dot(a[:, c0 - lo:c1 - lo], b_ref[q, :, c0 - q * n_blk:c1 - q * n_blk], NT)
                    tot = part if tot is None else tot + part
                return tot
            if b_whole:
                return _dot(a, b_ref[:, koff[s] * tk:(koff[s] + 1) * tk], NT)
            return _dot(a, b_ref[0] if stacked else b_ref[...], NT if nt else NN)

        def finish(acc):
            res = epi(acc, [r[...] for r in e_refs], i, j)
            for o_ref, (_, _, kind), r in zip(o_refs, outs, res):
                if isinstance(kind, str) and kind == 'acc':
                    @pl.when(i == 0)
                    def _(o_ref=o_ref, r=r):
                        o_ref[...] = r

                    @pl.when(i > 0)
                    def _(o_ref=o_ref, r=r):
                        o_ref[...] += r
                else:
                    o_ref[...] = r.astype(o_ref.dtype)

        if nk == 1:
            finish(product(a_segs[0][1], seg_refs[0]))
            return
        acc_ref = refs[-1]

        @pl.when(k == 0)
        def _():
            acc_ref[...] = jnp.zeros_like(acc_ref)

        for s, ((_, fn), rs) in enumerate(zip(a_segs, seg_refs)):
            def accumulate(fn=fn, rs=rs, s=s):
                acc_ref[...] += product(fn, rs, s)
            if len(a_segs) == 1:
                accumulate()
            else:
                pl.when(jnp.logical_and(k >= koff[s], k < koff[s] + nkb[s]))(accumulate)

        @pl.when(k == nk - 1)
        def _():
            finish(acc_ref[...])

    return pl.pallas_call(
        body, name=name, grid=(ni, nj, nk), in_specs=in_specs, out_specs=out_specs, out_shape=out_shape,
        scratch_shapes=[] if nk == 1 else [pltpu.VMEM((tm, tn), F32)], compiler_params=_params(3),
    )(*operands)


def _tn_mm(name, a_arrs, a_fn, b_segs, *, tn, tk, out_dtype, shard_major=False, colsum=False):
    T, M = a_arrs[0].shape
    nbj = [arrs[0].shape[1] // tn for arrs, _ in b_segs]
    joff = [sum(nbj[:s]) for s in range(len(nbj))]
    nj, nk = sum(nbj), T // tk
    N = nj * tn
    assert T % tk == 0 and all(arrs[0].shape[1] % tn == 0 for arrs, _ in b_segs), name

    in_specs = [pl.BlockSpec((tk, M), lambda j, k: (k, 0)) for _ in a_arrs]
    operands = list(a_arrs)
    for s, (arrs, _) in enumerate(b_segs):
        for arr in arrs:
            in_specs.append(pl.BlockSpec(
                (tk, tn), lambda j, k, s=s: (k, jnp.clip(j - joff[s], 0, nbj[s] - 1))))
            operands.append(arr)
    if shard_major:
        per = (N // N_CHIPS) // tn
        assert per * tn * N_CHIPS == N, name
        out_shape = [jax.ShapeDtypeStruct((N_CHIPS, M, N // N_CHIPS), out_dtype)]
        out_specs = [pl.BlockSpec((1, M, tn), lambda j, k: (j // per, 0, j % per))]
    else:
        out_shape = [jax.ShapeDtypeStruct((M, N), out_dtype)]
        out_specs = [pl.BlockSpec((M, tn), lambda j, k: (0, j))]
    if colsum:
        out_shape.append(jax.ShapeDtypeStruct((1, N), F32))
        out_specs.append(pl.BlockSpec((1, tn), lambda j, k: (0, j)))
    n_a = len(a_arrs)
    n_seg_refs = [len(arrs) for arrs, _ in b_segs]

    def body(*refs):
        a_refs = refs[:n_a]
        pos = n_a
        seg_refs = []
        for n in n_seg_refs:
            seg_refs.append(refs[pos:pos + n])
            pos += n
        o_ref = refs[pos]
        cs_ref = refs[pos + 1] if colsum else None
        acc_ref = refs[-1]
        j, k = pl.program_id(0), pl.program_id(1)

        @pl.when(k == 0)
        def _():
            acc_ref[...] = jnp.zeros_like(acc_ref)
            if colsum:
                cs_ref[...] = jnp.zeros_like(cs_ref)

        for s, ((_, fn), rs) in enumerate(zip(b_segs, seg_refs)):
            def accumulate(fn=fn, rs=rs):
                a = a_refs[0][...] if a_fn is None else a_fn(*[r[...] for r in a_refs])
                bt = rs[0][...] if fn is None else fn(*[r[...] for r in rs])
                acc_ref[...] += _dot(a, bt, TN)
                if colsum:
                    cs_ref[...] += jnp.sum(bt.astype(F32), axis=0, keepdims=True)
            if len(b_segs) == 1:
                accumulate()
            else:
                pl.when(jnp.logical_and(j >= joff[s], j < joff[s] + nbj[s]))(accumulate)

        @pl.when(k == nk - 1)
        def _():
            if shard_major:
                o_ref[0] = acc_ref[...].astype(o_ref.dtype)
            else:
                o_ref[...] = acc_ref[...].astype(o_ref.dtype)

    res = pl.pallas_call(
        body, name=name, grid=(nj, nk), in_specs=in_specs, out_specs=out_specs, out_shape=out_shape,
        scratch_shapes=[pltpu.VMEM((M, tn), F32)], compiler_params=_params(2),
    )(*operands)
    return res if colsum else res[0]


def _silu(z):
    return z * _sig(z)


def _silu_and_grad(z):
    s = _sig(z)
    return z * s, s * (1.0 + z * (1.0 - s))


def _gate(o, z):
    return o.astype(F32) * _silu(z.astype(F32))


def _ln_stats(r):
    mu = jnp.mean(r, axis=1, keepdims=True)
    xc = r - mu
    var = jnp.mean(xc * xc, axis=1, keepdims=True)
    rstd = lax.rsqrt(var + LN_EPS)
    return xc * rstd, rstd


def _ln_bwd(dy, xhat, rstd, g):
    dxh = dy * g
    m1 = jnp.mean(dxh, axis=1, keepdims=True)
    m2 = jnp.mean(dxh * xhat, axis=1, keepdims=True)
    return (rstd * (dxh - m1 - xhat * m2), jnp.sum(dy * xhat, axis=0, keepdims=True),
            jnp.sum(dy, axis=0, keepdims=True))


def _rms_fwd(x, g):
    rstd = lax.rsqrt(jnp.mean(x * x, axis=1, keepdims=True) + RMS_EPS)
    return x * rstd * g


def _rms_bwd(dy, x, g):
    rstd = lax.rsqrt(jnp.mean(x * x, axis=1, keepdims=True) + RMS_EPS)
    xn = x * rstd
    dxn = dy * g
    return rstd * (dxn - xn * jnp.mean(dxn * xn, axis=1, keepdims=True)), jnp.sum(dy * xn, axis=0, keepdims=True)


def _rope(x, cos, sin, transpose=False):
    parts = []
    for g in range(x.shape[1] // LANES):
        xg = x[:, g * LANES:(g + 1) * LANES]
        if transpose:
            parts.append(xg * cos + pltpu.roll(xg * sin, LANES // 2, 1))
        else:
            parts.append(xg * cos + pltpu.roll(xg, LANES // 2, 1) * sin)
    return parts[0] if len(parts) == 1 else jnp.concatenate(parts, axis=1)


def _shifted_rows(window, rc):
    n = window.shape[0]
    for b in range(SUBLANES):
        rolled = window if b == 0 else pltpu.roll(window, n - b, 0)
        for a8 in range(0, n - rc - b + 1, SUBLANES):
            yield a8 + b, rolled[a8:a8 + rc]


def _conv_fwd(proj, conv_w, conv_b, norm_g, norm_b, E, tm):
    T = proj.shape[0]
    kc = conv_w.shape[0]
    hb, rc = CONV_HALO, min(CONV_RC, tm)
    ni, ratio = T // tm, tm // hb
    sub = min(CONV_SUB, E)
    base = hb - (kc - 1)

    def body(val_ref, gate_ref, z_ref, valh_ref, gateh_ref, w_ref, cb_ref, g_ref, b_ref, u1_ref, u4_ref, ubuf):
        i = pl.program_id(0)
        ubuf[hb:, :] = val_ref[...] * _sig(gate_ref[...])
        halo = valh_ref[...] * _sig(gateh_ref[...])
        ubuf[0:hb, :] = jnp.where(i > 0, halo, 0.0)
        for l0 in range(0, E, sub):
            ls = slice(l0, l0 + sub)
            for r0 in range(0, tm, rc):
                acc = jnp.zeros((rc, sub), F32) + cb_ref[:, ls]
                for off, rows in _shifted_rows(ubuf[r0:r0 + hb + rc, ls], rc):
                    if 0 <= off - base < kc:
                        acc += w_ref[off - base:off - base + 1, ls] * rows
                u1_ref[r0:r0 + rc, ls] = acc
        xhat, _ = _ln_stats(u1_ref[...])
        u4_ref[...] = (_silu(xhat * g_ref[...] + b_ref[...]) * _silu(z_ref[...])).astype(BF16)

    main = lambda col: pl.BlockSpec((tm, E), lambda i: (i, col))
    halo = lambda col: pl.BlockSpec((hb, E), lambda i: (jnp.maximum(i * ratio - 1, 0), col))
    whole = lambda a: pl.BlockSpec(a.shape, lambda i: (0, 0))
    return pl.pallas_call(
        body, name="conv_fwd", grid=(ni,),
        in_specs=[main(0), main(1), main(2), halo(0), halo(1), whole(conv_w), whole(conv_b), whole(norm_g),
                  whole(norm_b)],
        out_specs=[main(0), main(0)],
        out_shape=[jax.ShapeDtypeStruct((T, E), F32), jax.ShapeDtypeStruct((T, E), BF16)],
        scratch_shapes=[pltpu.VMEM((hb + tm, E), F32)], compiler_params=_params(1),
    )(proj, proj, proj, proj, proj, conv_w, conv_b, norm_g, norm_b)


def _conv_bwd(du1, proj, conv_w, E, tm):
    T = du1.shape[0]
    kc = conv_w.shape[0]
    lc, hb, rc = min(CONV_LC, E), CONV_HALO, min(CONV_RC, tm)
    nl, ni, ratio = E // lc, T // tm, tm // hb
    gate_off = E // lc
    last_halo = T // hb - 1

    sub = min(CONV_SUB, lc)
    base = hb - (kc - 1)

    def body(du_ref, dun_ref, val_ref, gate_ref, valh_ref, gateh_ref, w_ref,
             dval_ref, dgate_ref, dw_ref, db_ref, ubuf, dbuf, sbuf, dw_sc):
        i = pl.program_id(1)
        sbuf[...] = _sig(gate_ref[...])
        ubuf[hb:, :] = val_ref[...] * sbuf[...]
        halo = valh_ref[...] * _sig(gateh_ref[...])
        ubuf[0:hb, :] = jnp.where(i > 0, halo, 0.0)
        dbuf[0:tm, :] = du_ref[...]
        dbuf[tm:, :] = jnp.where(i < ni - 1, dun_ref[...], 0.0)

        @pl.when(i == 0)
        def _():
            dw_sc[...] = jnp.zeros_like(dw_sc)
            db_ref[...] = jnp.zeros_like(db_ref)

        db_ref[...] += jnp.sum(du_ref[...], axis=0, keepdims=True)
        for l0 in range(0, lc, sub):
            ls = slice(l0, l0 + sub)
            for r0 in range(0, tm, rc):
                dwin = dbuf[r0:r0 + rc + hb, ls]
                dchunk = dwin[0:rc]
                for off, rows in _shifted_rows(ubuf[r0:r0 + hb + rc, ls], rc):
                    k = off - base
                    if 0 <= k < kc:
                        prod = rows * dchunk
                        part = prod[0:SUBLANES]
                        for s8 in range(SUBLANES, rc, SUBLANES):
                            part = part + prod[s8:s8 + SUBLANES]
                        dw_sc[k, :, ls] += part
                acc = jnp.zeros((rc, sub), F32)
                for off, rows in _shifted_rows(dwin, rc):
                    k = (kc - 1) - off
                    if 0 <= k < kc:
                        acc += w_ref[k:k + 1, ls] * rows
                v, s = val_ref[r0:r0 + rc, ls], sbuf[r0:r0 + rc, ls]
                dval_ref[r0:r0 + rc, ls] = (acc * s).astype(BF16)
                dgate_ref[r0:r0 + rc, ls] = (acc * v * s * (1.0 - s)).astype(BF16)

        @pl.when(i == ni - 1)
        def _():
            for k in range(kc):
                dw_ref[k:k + 1, :] = jnp.sum(dw_sc[k], axis=0, keepdims=True)

    return pl.pallas_call(
        body, name="conv_bwd", grid=(nl, ni),
        in_specs=[
            pl.BlockSpec((tm, lc), lambda l, i: (i, l)),
            pl.BlockSpec((hb, lc), lambda l, i: (jnp.minimum((i + 1) * ratio, last_halo), l)),
            pl.BlockSpec((tm, lc), lambda l, i: (i, l)),
            pl.BlockSpec((tm, lc), lambda l, i: (i, gate_off + l)),
            pl.BlockSpec((hb, lc), lambda l, i: (jnp.maximum(i * ratio - 1, 0), l)),
            pl.BlockSpec((hb, lc), lambda l, i: (jnp.maximum(i * ratio - 1, 0), gate_off + l)),
            pl.BlockSpec((kc, lc), lambda l, i: (0, l)),
        ],
        out_specs=[pl.BlockSpec((tm, lc), lambda l, i: (i, l)), pl.BlockSpec((tm, lc), lambda l, i: (i, l)),
                   pl.BlockSpec((kc, lc), lambda l, i: (0, l)), pl.BlockSpec((1, lc), lambda l, i: (0, l))],
        out_shape=[jax.ShapeDtypeStruct((T, E), BF16), jax.ShapeDtypeStruct((T, E), BF16),
                   jax.ShapeDtypeStruct((kc, E), F32), jax.ShapeDtypeStruct((1, E), F32)],
        scratch_shapes=[pltpu.VMEM((hb + tm, lc), F32), pltpu.VMEM((tm + hb, lc), F32),
                        pltpu.VMEM((tm, lc), F32), pltpu.VMEM((kc, SUBLANES, lc), F32)],
        compiler_params=_params(2),
    )(du1, du1, proj, proj, proj, proj, conv_w)


def _attn_fwd(q_all, kv, kr, H, tq, scale):
    T = q_all.shape[0]
    nq = T // tq
    pair = 2
    W = pair * LANES
    assert H % pair == 0
    hp_n = H // pair

    def body(qn_ref, qr_ref, kn_ref, kr_ref, v_ref, o_ref, lse_ref, *scratch):
        qi = pl.program_id(1)
        chains = [scratch[4 * a:4 * a + 4] for a in range(pair)]
        lanes = [slice(a * LANES, (a + 1) * LANES) for a in range(pair)]
        groups = [slice(c * LANES, (c + 1) * LANES) for c in range(tq // LANES)]

        def fold(x, op):
            r = x[:, groups[0]]
            for gsl in groups[1:]:
                r = op(r, x[:, gsl])
            return r

        for _, m_sc, l_sc, acc_sc in chains:
            m_sc[...] = jnp.full_like(m_sc, MASK_VALUE)
            l_sc[...] = jnp.zeros_like(l_sc)
            acc_sc[...] = jnp.zeros_like(acc_sc)

        def scores(j, masked):
            rows = pl.ds(pl.multiple_of(j * tq, tq), tq)
            krope = kr_ref[rows, :]
            for a, (s_sc, m_sc, _, _) in enumerate(chains):
                q = jnp.concatenate([qn_ref[:, lanes[a]], qr_ref[:, lanes[a]]], axis=1)
                k = jnp.concatenate([kn_ref[rows, lanes[a]], krope], axis=1)
                s = _dot(q, k, NT) * (scale * LOG2_E)
                if masked:
                    row = lax.broadcasted_iota(jnp.int32, s.shape, 0)
                    col = lax.broadcasted_iota(jnp.int32, s.shape, 1)
                    s = jnp.where(col <= row, s, MASK_VALUE)
                s_sc[j] = s
                m_sc[...] = jnp.maximum(m_sc[...], fold(s, jnp.maximum))

        def two_per_trip(fn, count):
            def two(p, carry):
                fn(2 * p)
                fn(2 * p + 1)
                return carry

            lax.fori_loop(0, count // 2, two, 0)

            @pl.when(count % 2 == 1)
            def _():
                fn(count - 1)

        two_per_trip(functools.partial(scores, masked=False), qi)
        scores(qi, True)
        for _, m_sc, _, _ in chains:
            m_sc[...] = jnp.broadcast_to(jnp.max(m_sc[...], axis=1, keepdims=True), m_sc.shape)

        def weigh(j):
            rows = pl.ds(pl.multiple_of(j * tq, tq), tq)
            for a, (s_sc, m_sc, l_sc, acc_sc) in enumerate(chains):
                s, m = s_sc[j], m_sc[...]
                p = jnp.concatenate([jnp.exp2(s[:, gsl] - m) for gsl in groups], axis=1)
                l_sc[...] += fold(p, jnp.add)
                acc_sc[...] += _dot(p, v_ref[rows, lanes[a]], NN)

        two_per_trip(weigh, qi + 1)
        for a, (_, m_sc, l_sc, acc_sc) in enumerate(chains):
            l = jnp.sum(l_sc[...], axis=1, keepdims=True)
            o_ref[:, lanes[a]] = (acc_sc[...] / l).astype(BF16)
            lse_ref[a] = m_sc[:, 0:1] * (1.0 / LOG2_E) + jnp.log(l)

    chain_scratch = [pltpu.VMEM((nq, tq, tq), F32), pltpu.VMEM((tq, LANES), F32), pltpu.VMEM((tq, LANES), F32),
                     pltpu.VMEM((tq, LANES), F32)]
    return pl.pallas_call(
        body, name="attn_fwd", grid=(hp_n, nq),
        in_specs=[pl.BlockSpec((tq, W), lambda hp, qi: (qi, hp)),
                  pl.BlockSpec((tq, W), lambda hp, qi: (qi, hp_n + hp)),
                  pl.BlockSpec((T, W), lambda hp, qi: (0, hp)),
                  pl.BlockSpec((T, LANES), lambda hp, qi: (0, 0)),
                  pl.BlockSpec((T, W), lambda hp, qi: (0, hp_n + hp))],
        out_specs=[pl.BlockSpec((tq, W), lambda hp, qi: (qi, hp)),
                   pl.BlockSpec((pair, tq, 1), lambda hp, qi: (hp, qi, 0))],
        out_shape=[jax.ShapeDtypeStruct((T, H * LANES), BF16), jax.ShapeDtypeStruct((H, T, 1), F32)],
        scratch_shapes=chain_scratch * pair, compiler_params=_params(2),
    )(q_all, q_all, kv, kr, kv)


def _attn_bwd(q_all, kv, kr, do, o, lse, cos, sin, H, tq, scale):
    T = q_all.shape[0]
    nq = T // tq
    HV = H * LANES
    pair = 2
    tk2 = pair * tq
    ng = T // tk2
    assert ng * tk2 == T and pair == 2

    def body(qn_ref, qr_ref, kn_ref, kr_ref, v_ref, do_ref, o_ref, lse_ref, cos_ref, sin_ref,
             dqn_ref, dqr_ref, dkn_ref, dkr_ref, dv_ref, dq_sc, dk_sc, dv_sc):
        g = pl.program_id(1)

        @pl.when(g == 0)
        def _():
            dq_sc[...] = jnp.zeros_like(dq_sc)

        key_rows = [slice(c * tq, (c + 1) * tq) for c in range(pair)]

        def block(qi, modes):
            rows = pl.ds(pl.multiple_of(qi * tq, tq), tq)
            q = jnp.concatenate([qn_ref[rows, :], qr_ref[rows, :]], axis=1)
            dov = do_ref[rows, :]
            delta = jnp.sum(dov.astype(F32) * o_ref[rows, :].astype(F32), axis=1, keepdims=True)
            lse_q = lse_ref[0, rows, :]
            dq, dkv = None, []
            for kr_, masked in zip(key_rows, modes):
                if masked is None:
                    dkv.append(None)
                    continue
                k = jnp.concatenate([kn_ref[kr_, :], kr_ref[kr_, :]], axis=1)
                s = _dot(q, k, NT) * scale
                if masked:
                    row = lax.broadcasted_iota(jnp.int32, s.shape, 0)
                    col = lax.broadcasted_iota(jnp.int32, s.shape, 1)
                    s = jnp.where(col <= row, s, MASK_VALUE)
                p = jnp.exp(s - lse_q)
                dv = _dot(p, dov, TN)
                dp = _dot(dov, v_ref[kr_, :], NT)
                ds = (p * (dp - delta) * scale).astype(BF16)
                dkv.append((_dot(ds, q, TN), dv))
                part = _dot(ds, k, NN)
                dq = part if dq is None else dq + part
            return rows, dq, dkv

        rows_a, dq_a, (kv_a0, _) = block(pair * g, (True, None))
        rows_b, dq_b, (kv_b0, kv_b1) = block(pair * g + 1, (False, True))
        dk_sc[key_rows[0], :] = kv_a0[0] + kv_b0[0]
        dv_sc[key_rows[0], :] = kv_a0[1] + kv_b0[1]
        dk_sc[key_rows[1], :] = kv_b1[0]
        dv_sc[key_rows[1], :] = kv_b1[1]
        dq_sc[rows_a, :] += dq_a
        dq_sc[rows_b, :] += dq_b

        def below(trip, carry):
            for qi in (pair * g + pair + 2 * trip, pair * g + pair + 2 * trip + 1):
                rows, dq, dkv = block(qi, (False, False))
                for kr_, (dk, dv) in zip(key_rows, dkv):
                    dk_sc[kr_, :] += dk
                    dv_sc[kr_, :] += dv
                dq_sc[rows, :] += dq
            return carry

        lax.fori_loop(0, (nq - pair * g - pair) // 2, below, 0)
        dkn_ref[...] = dk_sc[:, :LANES].astype(BF16)
        dkr_ref[...] = dk_sc[:, LANES:].astype(BF16)
        dv_ref[...] = dv_sc[...].astype(BF16)

        @pl.when(g == ng - 1)
        def _():
            dqn_ref[...] = dq_sc[:, :LANES].astype(BF16)
            dqr_ref[...] = _rope(dq_sc[:, LANES:], cos_ref[...], sin_ref[...], transpose=True).astype(BF16)

    whole = lambda col: pl.BlockSpec((T, LANES), col)
    tile = lambda col: pl.BlockSpec((tk2, LANES), col)
    return pl.pallas_call(
        body, name="attn_bwd", grid=(H, ng),
        in_specs=[whole(lambda h, g: (0, h)), whole(lambda h, g: (0, H + h)),
                  tile(lambda h, g: (g, h)), tile(lambda h, g: (g, 0)), tile(lambda h, g: (g, H + h)),
                  whole(lambda h, g: (0, h)), whole(lambda h, g: (0, h)),
                  pl.BlockSpec((1, T, 1), lambda h, g: (h, 0, 0)),
                  whole(lambda h, g: (0, 0)), whole(lambda h, g: (0, 0))],
        out_specs=[whole(lambda h, g: (0, h)), whole(lambda h, g: (0, h)),
                   tile(lambda h, g: (g, h)), tile(lambda h, g: (g, h)), tile(lambda h, g: (g, h))],
        out_shape=[jax.ShapeDtypeStruct((T, HV), BF16), jax.ShapeDtypeStruct((T, HV), BF16),
                   jax.ShapeDtypeStruct((T, HV), BF16), jax.ShapeDtypeStruct((T, HV), BF16),
                   jax.ShapeDtypeStruct((T, HV), BF16)],
        scratch_shapes=[pltpu.VMEM((T, 2 * LANES), F32), pltpu.VMEM((tk2, 2 * LANES), F32),
                        pltpu.VMEM((tk2, LANES), F32)],
        compiler_params=_params(2),
    )(q_all, q_all, kv, kr, kv, do, o, lse, cos, sin)


def _key_rope_bwd(dkr_heads, cos, sin, H, tm):
    T, HV = dkr_heads.shape

    def body(dkr_ref, cos_ref, sin_ref, dk_ref):
        dk = dkr_ref[...].astype(F32)
        tot = dk[:, 0:LANES]
        for h in range(1, H):
            tot = tot + dk[:, h * LANES:(h + 1) * LANES]
        dk_ref[...] = _rope(tot, cos_ref[...], sin_ref[...], transpose=True)

    return pl.pallas_call(
        body, name="key_rope_bwd", grid=(T // tm,),
        in_specs=[pl.BlockSpec((tm, HV), lambda i: (i, 0)),
                  pl.BlockSpec((tm, LANES), lambda i: (i, 0)), pl.BlockSpec((tm, LANES), lambda i: (i, 0))],
        out_specs=pl.BlockSpec((tm, LANES), lambda i: (i, 0)),
        out_shape=jax.ShapeDtypeStruct((T, LANES), F32), compiler_params=_params(1),
    )(dkr_heads, cos, sin)


def _place():
    x, y, c = lax.axis_index("x"), lax.axis_index("y"), lax.axis_index("c")
    chips = [(1 - x, y), (x, 1 - y), (1 - x, 1 - y)]
    return x, y, c, chips


def _all_gather_chips(arrs):
    n = len(arrs)
    halves = [a.shape[0] // 2 for a in arrs]
    assert all(h * 2 == a.shape[0] and h % BF16_ROWS == 0 for h, a in zip(halves, arrs))

    def body(*refs):
        w_refs, out_refs = refs[:n], refs[n:2 * n]
        send_sems, recv_sems, local_sems = refs[2 * n:]
        x, y, c, chips = _place()
        sibling = (x, y, 1 - c)
        waits = []
        for w, (w_ref, out_ref, half) in enumerate(zip(w_refs, out_refs, halves)):
            def region(px, py, pc, out_ref=out_ref, half=half):
                return out_ref.at[2 * px + py, pl.ds(pc * half, half), :]

            def copy(k, block, to, src=None, w=w, region=region):
                return pltpu.make_async_remote_copy(
                    src_ref=region(*block) if src is None else src, dst_ref=region(*block),
                    send_sem=send_sems.at[6 * w + k], recv_sem=recv_sems.at[6 * w + k],
                    device_id=to, device_id_type=MESH)

            mine = pltpu.make_async_copy(w_ref, out_ref.at[2 * x + y], local_sems.at[w])
            mine.start()
            my_half = w_ref.at[pl.ds(c * half, half), :]
            first = [copy(j, (x, y, c), (*chip, c), src=my_half) for j, chip in enumerate(chips)]
            for cp in first:
                cp.start()
            waits.append((copy, mine, first))
        for copy, mine, first in waits:
            passed = [copy(3 + j, (*chip, c), sibling) for j, chip in enumerate(chips)]
            for j, chip in enumerate(chips):
                copy(j, (*chip, c), (x, y, c)).wait_recv()
                passed[j].start()
            for j, chip in enumerate(chips):
                copy(3 + j, (*chip, 1 - c), (x, y, c)).wait_recv()
            for cp in first + passed:
                cp.wait_send()
            mine.wait()

    return pl.pallas_call(
        body, name="gather_weights", in_specs=[ANY] * n, out_specs=[ANY] * n,
        out_shape=[jax.ShapeDtypeStruct((N_CHIPS,) + a.shape, a.dtype) for a in arrs],
        scratch_shapes=[pltpu.SemaphoreType.DMA((6 * n,)), pltpu.SemaphoreType.DMA((6 * n,)),
                        pltpu.SemaphoreType.DMA((n,))],
    )(*arrs)


def _swap_cores(name, parts):
    n = len(parts)

    def body(*refs):
        p_refs, r_refs = refs[:n], refs[n:2 * n]
        send_sems, recv_sems = refs[2 * n:]
        x, y, c, _ = _place()
        copies = [pltpu.make_async_remote_copy(
            src_ref=p_refs[w], dst_ref=r_refs[w], send_sem=send_sems.at[w], recv_sem=recv_sems.at[w],
            device_id=(x, y, 1 - c), device_id_type=MESH) for w in range(n)]
        for cp in copies:
            cp.start()
        for cp in copies:
            cp.wait()

    return pl.pallas_call(
        body, name=name, in_specs=[ANY] * n, out_specs=[ANY] * n,
        out_shape=[jax.ShapeDtypeStruct(p.shape, p.dtype) for p in parts],
        scratch_shapes=[pltpu.SemaphoreType.DMA((n,)), pltpu.SemaphoreType.DMA((n,))],
    )(*parts)


def _chip_sum_of_my_half(name, a, core):
    _, rows, cols = a.shape
    half = rows // 2
    tr = _gcd(ADAM_ROWS, half)
    steps = half // tr

    def swap_body(a_ref, r_ref, send_sem, recv_sem):
        x, y, c, _ = _place()
        cp = pltpu.make_async_remote_copy(
            src_ref=a_ref.at[:, pl.ds((1 - c) * half, half), :], dst_ref=r_ref, send_sem=send_sem,
            recv_sem=recv_sem, device_id=(x, y, 1 - c), device_id_type=MESH)
        cp.start()
        cp.wait()

    theirs = pl.pallas_call(
        swap_body, name=name + "_swap", in_specs=[ANY], out_specs=ANY,
        out_shape=jax.ShapeDtypeStruct((1, half, cols), a.dtype),
        scratch_shapes=[pltpu.SemaphoreType.DMA, pltpu.SemaphoreType.DMA],
    )(a)

    def add_body(core_ref, a_ref, r_ref, o_ref):
        o_ref[...] = (a_ref[...].astype(F32) + r_ref[...].astype(F32)).astype(o_ref.dtype)

    return pl.pallas_call(
        add_body, name=name + "_add",
        grid_spec=pltpu.PrefetchScalarGridSpec(
            num_scalar_prefetch=1, grid=(steps,),
            in_specs=[pl.BlockSpec((1, tr, cols), lambda i, c: (0, c[0] * steps + i, 0)),
                      pl.BlockSpec((1, tr, cols), lambda i, c: (0, i, 0))],
            out_specs=pl.BlockSpec((1, tr, cols), lambda i, c: (0, i, 0))),
        out_shape=jax.ShapeDtypeStruct((1, half, cols), a.dtype), compiler_params=_params(1),
    )(core, a, theirs)


HBM = pl.BlockSpec(memory_space=pltpu.HBM)
SEM = pl.BlockSpec(memory_space=pltpu.SEMAPHORE)
EFFECT = pltpu.SideEffectType.DATAFLOW_SIDE_EFFECTING


def _push_copies(a_refs, l_refs, send_sems, recv_sems, by_target):
    x, y, c, chips = _place()
    me = 2 * x + y

    def part(a_ref, q):
        if by_target == 'cols':
            n = a_ref.shape[-1] // N_CHIPS
            return a_ref.at[(slice(None),) * (len(a_ref.shape) - 1) + (pl.ds(pl.multiple_of(q * n, LANES), n),)]
        return a_ref.at[q] if by_target else a_ref

    out = []
    for w, (a_ref, l_ref) in enumerate(zip(a_refs, l_refs)):
        for j, (px, py) in enumerate(chips):
            peer = 2 * px + py
            out.append((
                pltpu.make_async_remote_copy(
                    src_ref=part(a_ref, peer), dst_ref=l_ref.at[me],
                    send_sem=send_sems.at[3 * w + j], recv_sem=recv_sems.at[3 * w + j],
                    device_id=(px, py, c), device_id_type=MESH),
                pltpu.make_async_remote_copy(
                    src_ref=part(a_ref, me), dst_ref=l_ref.at[peer],
                    send_sem=send_sems.at[3 * w + j], recv_sem=recv_sems.at[3 * w + j],
                    device_id=(px, py, c), device_id_type=MESH)))
    return out


def _landing_shape(a, by_target):
    if by_target == 'cols':
        return (N_CHIPS,) + a.shape[:-1] + (a.shape[-1] // N_CHIPS,)
    return (N_CHIPS,) + (a.shape[1:] if by_target else a.shape)


def _push_start(name, arrs, by_target):
    n = len(arrs)
    lands = [lax.empty(_landing_shape(a, by_target), a.dtype) for a in arrs]

    def body(*refs):
        a_refs, l_refs = refs[:n], refs[n:2 * n]
        send_sems, recv_sems = refs[2 * n], refs[2 * n + 1]
        token = refs[-1]
        for send, _ in _push_copies(a_refs, l_refs, send_sems, recv_sems, by_target):
            send.start()
        token[...] = jnp.zeros_like(token)

    res = pl.pallas_call(
        body, name=name,
        out_shape=(pltpu.SemaphoreType.DMA((3 * n,)), pltpu.SemaphoreType.DMA((3 * n,)),
                   *[pltpu.HBM(a.shape, a.dtype) for a in arrs], *[pltpu.HBM(l.shape, l.dtype) for l in lands],
                   jax.ShapeDtypeStruct((8, LANES), F32)),
        in_specs=[HBM] * (2 * n), out_specs=(SEM, SEM, *[HBM] * (2 * n), pl.BlockSpec(memory_space=pltpu.VMEM)),
        input_output_aliases={i: 2 + i for i in range(2 * n)},
        compiler_params=pltpu.CompilerParams(has_side_effects=EFFECT),
    )(*[pltpu.with_memory_space_constraint(a, pltpu.HBM) for a in list(arrs) + lands])
    return res[0], res[1], list(res[2:2 + n]), list(res[2 + n:2 + 2 * n]), res[-1]


def _push_wait(name, send_sems, recv_sems, arrs, lands, after, by_target):
    n = len(arrs)

    def body(*refs):
        a_refs, l_refs = refs[:n], refs[n:2 * n]
        s_sems, r_sems = refs[2 * n], refs[2 * n + 1]
        for send, recv in _push_copies(a_refs, l_refs, s_sems, r_sems, by_target):
            send.wait_send()
            recv.wait_recv()

    res = pl.pallas_call(
        body, name=name,
        out_shape=[pltpu.HBM(a.shape, a.dtype) for a in list(arrs) + list(lands)],
        in_specs=[HBM] * (2 * n) + [SEM, SEM] + [ANY] * len(after), out_specs=[HBM] * (2 * n),
        input_output_aliases={i: i for i in range(2 * n)},
        compiler_params=pltpu.CompilerParams(has_side_effects=EFFECT),
    )(*arrs, *lands, send_sems, recv_sems, *after)
    return list(res[:n]), list(res[n:])


def _all_reduce_small(part):
    def body(p_ref, out_ref, sib_buf, chip_buf, send_sems, recv_sems):
        x, y, c, chips = _place()
        me = 2 * x + y
        swap = pltpu.make_async_remote_copy(
            src_ref=p_ref, dst_ref=sib_buf, send_sem=send_sems.at[0], recv_sem=recv_sems.at[0],
            device_id=(x, y, 1 - c), device_id_type=MESH)
        swap.start()
        swap.wait()
        chip_buf[me] = p_ref[...] + sib_buf[...]
        copies = []
        for j, (px, py) in enumerate(chips):
            cp = pltpu.make_async_remote_copy(
                src_ref=chip_buf.at[me], dst_ref=chip_buf.at[me], send_sem=send_sems.at[1 + j],
                recv_sem=recv_sems.at[1 + j], device_id=(px, py, c), device_id_type=MESH)
            cp.start()
            copies.append(cp)
        for j, (px, py) in enumerate(chips):
            pltpu.make_async_remote_copy(
                src_ref=chip_buf.at[me], dst_ref=chip_buf.at[2 * px + py], send_sem=send_sems.at[1 + j],
                recv_sem=recv_sems.at[1 + j], device_id=(px, py, c), device_id_type=MESH).wait_recv()
        for cp in copies:
            cp.wait_send()
        tot = chip_buf[0]
        for q in range(1, N_CHIPS):
            tot = tot + chip_buf[q]
        out_ref[...] = tot

    vm = pl.BlockSpec(memory_space=pltpu.VMEM)
    return pl.pallas_call(
        body, name="all_reduce_small", in_specs=[vm], out_specs=vm,
        out_shape=jax.ShapeDtypeStruct(part.shape, F32),
        scratch_shapes=[pltpu.VMEM(part.shape, F32), pltpu.VMEM((N_CHIPS,) + part.shape, F32),
                        pltpu.SemaphoreType.DMA((N_CHIPS,)), pltpu.SemaphoreType.DMA((N_CHIPS,))],
    )(part)


def _row_tiles(shape):
    ax = next(d for d, s in enumerate(shape) if s > 1)
    tr = _gcd(ADAM_ROWS, shape[ax])
    block = tuple(tr if d == ax else s for d, s in enumerate(shape))
    return shape[ax] // tr, block, lambda i: tuple(i if d == ax else 0 for d in range(len(shape)))


def _sum_chips(name, landed, sent, chip, by_target):
    shape = landed.shape[1:]
    steps, block, index = _row_tiles(shape)
    if by_target == 'cols':
        own_spec = pl.BlockSpec(block, lambda i, c: index(i)[:-1] + (c[0],))
    else:
        own_spec = pl.BlockSpec((1,) + block, lambda i, c: (c[0],) + index(i))

    def body(chip_ref, l_ref, s_ref, o_ref):
        own = (s_ref[...] if by_target == 'cols' else s_ref[0]).astype(F32)
        tot = None
        for q in range(N_CHIPS):
            term = jnp.where(chip_ref[0] == q, own, l_ref[q].astype(F32))
            tot = term if tot is None else tot + term
        o_ref[...] = tot

    return pl.pallas_call(
        body, name=name,
        grid_spec=pltpu.PrefetchScalarGridSpec(
            num_scalar_prefetch=1, grid=(steps,),
            in_specs=[pl.BlockSpec((N_CHIPS,) + block, lambda i, c: (0,) + index(i)), own_spec],
            out_specs=pl.BlockSpec(block, lambda i, c: index(i))),
        out_shape=jax.ShapeDtypeStruct(shape, F32), compiler_params=_params(1),
    )(chip, landed, sent)


def _adamw_math(g, w, m, v):
    mn = ADAM_B1 * m + (1.0 - ADAM_B1) * g
    vn = ADAM_B2 * v + (1.0 - ADAM_B2) * jnp.square(g)
    m_hat = mn / (1.0 - ADAM_B1 ** ADAM_STEP)
    v_hat = vn / (1.0 - ADAM_B2 ** ADAM_STEP)
    return -ADAM_LR * (m_hat / (jnp.sqrt(v_hat) + ADAM_EPS) + ADAM_WD * w), mn, vn


def _adamw(name, g_parts, w, m, v):
    steps, block, index = _row_tiles(w.shape)
    n = len(g_parts)

    def body(*refs):
        g = refs[0][...]
        for r in refs[1:n]:
            g = g + r[...]
        w_ref, m_ref, v_ref, go_ref, d_ref, mo_ref, vo_ref = refs[n:]
        go_ref[...] = g
        d_ref[...], mo_ref[...], vo_ref[...] = _adamw_math(g, w_ref[...], m_ref[...], v_ref[...])

    spec = pl.BlockSpec(block, index)
    return pl.pallas_call(
        body, name=name, grid=(steps,), in_specs=[spec] * (n + 3), out_specs=[spec] * 4,
        out_shape=[jax.ShapeDtypeStruct(w.shape, F32)] * 4, compiler_params=_params(1),
    )(*g_parts, w, m, v)


def _adamw_vectors(items):
    n = len(items)

    def body(*refs):
        ins, outs = refs[:4 * n], refs[4 * n:]
        for k in range(n):
            g, w, m, v = (r[...] for r in ins[4 * k:4 * k + 4])
            outs[3 * k][...], outs[3 * k + 1][...], outs[3 * k + 2][...] = _adamw_math(g, w, m, v)

    vm = pl.BlockSpec(memory_space=pltpu.VMEM)
    res = pl.pallas_call(
        body, name="adamw_vectors", in_specs=[vm] * (4 * n), out_specs=[vm] * (3 * n),
        out_shape=[jax.ShapeDtypeStruct(it[1].shape, F32) for it in items for _ in range(3)],
    )(*[a for it in items for a in it])
    return [res[3 * k:3 * k + 3] for k in range(n)]


def _pack_rows(flat, dtype, multiple):
    n = flat.shape[0]
    total = -(-n // multiple) * multiple
    return jnp.pad(flat, (0, total - n)).astype(dtype).reshape(total // LANES, LANES)


def kernel(x, positions, ln_g, ln_b, a_w_in, a_b_in, a_conv_w, a_conv_b, a_norm_g, a_norm_b, a_w_out, a_b_out, kv_w_down, kv_norm_g, kv_w_uk, kv_w_uv, b_w_in, b_q_norm_g, b_w_uq, b_w_out, loss_target, m_ln_g, m_ln_b, m_a_w_in, m_a_b_in, m_a_conv_w, m_a_conv_b, m_a_norm_g, m_a_norm_b, m_a_w_out, m_a_b_out, m_kv_w_down, m_kv_norm_g, m_kv_w_uk, m_kv_w_uv, m_b_w_in, m_b_q_norm_g, m_b_w_uq, m_b_w_out, v_ln_g, v_ln_b, v_a_w_in, v_a_b_in, v_a_conv_w, v_a_conv_b, v_a_norm_g, v_a_norm_b, v_a_w_out, v_a_b_out, v_kv_w_down, v_kv_norm_g, v_kv_w_uk, v_kv_w_uv, v_b_w_in, v_b_q_norm_g, v_b_w_uq, v_b_w_out):
    T, D = x.shape[1], x.shape[2]
    E = N_CHIPS * a_w_out.shape[1]
    RKV = kv_norm_g.shape[0]
    H, DN = kv_w_uk.shape[1], kv_w_uk.shape[2]
    RQ = b_q_norm_g.shape[1]
    HV = N_CHIPS * b_w_out.shape[1]
    assert DN == LANES and kv_w_uv.shape[2] == LANES and HV == H * LANES
    assert kv_w_down.shape[1] == RKV + ROPE_DIM and b_w_uq.shape[3] == DN + ROPE_DIM
    assert ln_g.shape[0] == 2 and a_w_in.shape[0] == 1 and b_w_in.shape[0] == 1
    alpha = (2.0 * ln_g.shape[0]) ** 0.25
    scale = 1.0 / math.sqrt(DN + ROPE_DIM)
    WK = -(-(RKV + LANES) // 256) * 256
    assert WK % RQ == 0
    Z_OFF = WK + RQ
    tmw, tq = min(TM_WIDE, T), min(TQ, T)
    t512, t1024 = _fit(512, T), _fit(1024, T)
    xs = x[0]
    tgt = loss_target[0]
    px, py = lax.axis_index("x"), lax.axis_index("y")
    chip = 2 * px + py

    mats = [a_w_out[0], kv_w_down, kv_w_uk, kv_w_uv, b_w_in[0], b_w_uq[0], b_w_out[0]]
    vecs = [a_b_in[0], a_conv_w[0], a_conv_b[0], a_norm_g[0], a_norm_b[0], a_b_out[0]]
    vec_bits = jnp.concatenate([lax.bitcast_convert_type(w.reshape(-1), BF16).reshape(-1) for w in vecs])
    rest = [w.astype(BF16) for w in mats]
    g_win, gathered = _all_gather_chips(
        [a_w_in[0].astype(BF16), _pack_rows(vec_bits, BF16, 2 * BF16_ROWS * LANES)])
    gathered = gathered.reshape(N_CHIPS, -1)
    gathered, rest = lax.optimization_barrier((gathered, rest))
    rest_sems = _push_start("gather_rest_start", rest, by_target=False)
    off = 0
    fvec = []
    for w in vecs:
        bits = gathered[:, off:off + 2 * w.size].reshape((N_CHIPS,) + w.shape + (2,))
        fvec.append(lax.bitcast_convert_type(bits, F32))
        off += 2 * w.size
    cols = lambda g: jnp.moveaxis(g, 0, -2).reshape(g.shape[1:-1] + (N_CHIPS * g.shape[-1],))
    b_in = cols(fvec[0][:, None, :])
    conv_w = cols(fvec[1])
    conv_b, norm_g, norm_b, b_out = (cols(f[:, None, :]) for f in fvec[2:])
    row = lambda a: a.reshape(1, -1)
    g0, b0, g1, b1 = row(ln_g[0]), row(ln_b[0]), row(ln_g[1]), row(ln_b[1])
    kv_g, q_g = row(kv_norm_g), row(b_q_norm_g[0])
    plain = lambda acc, ins, i, j: [acc]

    b_in = b_in + rest_sems[4][0, 0]
    (proj,) = _row_mm("a_in", [((xs,), None)], g_win, nt=False, tm=t1024, tn=3 * E // N_CHIPS, tk=D,
                      outs=[((T, 3 * E), F32, 'tile')], epi=lambda acc, ins, i, j: [acc + ins[0]],
                      epi_ins=[(b_in, 'col')])
    u1, u4 = _conv_fwd(proj, conv_w, conv_b, norm_g, norm_b, E, tmw)

    rest, landed = _push_wait("gather_rest_wait", *rest_sems[:4], after=[u4], by_target=False)
    g_wout, g_wd, g_uk, g_uv, g_wbin, g_wuq, g_wbout = [
        lax.dynamic_update_slice(l, w[None], (chip,) + (0,) * w.ndim) for w, l in zip(rest, landed)]
    w_out = g_wout.reshape(E, D)
    wd = g_wd.reshape(D, RKV + ROPE_DIM)
    zpad = jnp.zeros((D, ROPE_HALF), BF16)
    wd_p = jnp.concatenate(
        [wd[:, :RKV], wd[:, RKV:RKV + ROPE_HALF], zpad, wd[:, RKV + ROPE_HALF:], zpad,
         jnp.zeros((D, WK - RKV - LANES), BF16)], axis=1)
    w_bin = cols(g_wbin)
    w_z = w_bin[:, RQ:]
    wb_small = jnp.concatenate([wd_p, w_bin[:, :RQ]], axis=1)
    wb_all = jnp.concatenate([wd_p, w_bin], axis=1)
    w_kv = jnp.concatenate([g_uk.reshape(RKV, HV), g_uv.reshape(RKV, HV)], axis=1)
    wuq = g_wuq.reshape(RQ, H, DN + ROPE_DIM)
    zq = jnp.zeros((RQ, H, ROPE_HALF), BF16)
    w_qr = jnp.concatenate([wuq[:, :, DN:DN + ROPE_HALF], zq, wuq[:, :, DN + ROPE_HALF:], zq], axis=2)
    w_q = jnp.concatenate([wuq[:, :, :DN].reshape(RQ, HV), w_qr.reshape(RQ, HV)], axis=1)
    w_bout = g_wbout.reshape(HV, D)

    freqs = ROPE_THETA ** (-jnp.arange(0, ROPE_DIM, 2, dtype=F32) / ROPE_DIM)
    ang = positions[0].astype(F32)[:, None] * freqs
    cs, sn = jnp.cos(ang), jnp.sin(ang)
    ones, zeros = jnp.ones_like(cs), jnp.zeros_like(cs)
    cos_t = jnp.concatenate([cs, ones, cs, ones], axis=1)
    sin_t = jnp.concatenate([-sn, zeros, sn, zeros], axis=1)

    def ln_epi(acc, ins, i, j):
        bias, res, g, b = ins
        xhat, rstd = _ln_stats(alpha * res + acc + bias)
        h = xhat * g + b
        return [h, h, xhat, rstd]

    h1, h1b, xhat1, rstd1 = _row_mm(
        "a_out", [((u4,), None)], w_out, nt=False, tm=t512, tn=D, tk=_fit(2048, E),
        outs=[((T, D), F32, 'tile'), ((T, D), BF16, 'tile'), ((T, D), F32, 'tile'), ((T, 1), F32, 'row')],
        epi=ln_epi, epi_ins=[(b_out, 'col'), (xs, 'tile'), (g0, 'col'), (b0, 'col')])

    tkb = _fit(512, _gcd(WK, RQ, HV))
    def latents_epi(acc, ins, i, j):
        kg, qg, cos_, sin_ = ins
        return [acc, _rms_fwd(acc[:, :RKV], kg), _rope(acc[:, RKV:RKV + LANES], cos_, sin_),
                _rms_fwd(acc[:, WK:WK + RQ], qg)]

    whole_row = lambda a: (a, pl.BlockSpec(a.shape, lambda i, j, k: (0, 0)))
    pb, c_lat, kr, cqn = _row_mm(
        "b_in", [((h1b,), None)], wb_small, nt=False, tm=t512, tn=Z_OFF, tk=_fit(1024, D),
        outs=[((T, Z_OFF), F32, 'tile'), ((T, RKV), BF16, 'row'), ((T, LANES), BF16, 'row'),
              ((T, RQ), BF16, 'row')],
        epi=latents_epi, epi_ins=[whole_row(kv_g), whole_row(q_g), (cos_t, 'row'), (sin_t, 'row')])
    (zb,) = _row_mm("b_in_gate", [((h1b,), None)], w_z, nt=False, tm=t1024, tn=_fit(2048, HV),
                    tk=_fit(1024, D), outs=[((T, HV), BF16, 'tile')], epi=plain)
    (kv,) = _row_mm("kv_up", [((c_lat,), None)], w_kv, nt=False, tm=t1024, tn=_fit(2048, HV),
                    tk=_fit(1024, RKV), outs=[((T, 2 * HV), BF16, 'tile')], epi=plain)
    tnq = _fit(2048, HV)
    half_q = HV // tnq

    def q_epi(acc, ins, i, j):
        return [jnp.where(j >= half_q, _rope(acc, ins[0], ins[1]), acc)]

    (q_all,) = _row_mm("q_up", [((cqn,), None)], w_q, nt=False, tm=t1024, tn=tnq, tk=_fit(1024, RQ),
                       outs=[((T, 2 * HV), BF16, 'tile')], epi=q_epi,
                       epi_ins=[(cos_t, 'row'), (sin_t, 'row')])
    o, lse = _attn_fwd(q_all, kv, kr, H, tq, scale)

    def loss_epi(acc, ins, i, j):
        res, g, b, target = ins
        xhat, rstd = _ln_stats(alpha * res + acc)
        diff = xhat * g + b - target
        dr, dg, db = _ln_bwd(diff / D, xhat, rstd, g)
        return [dr, 0.5 * jnp.sum(diff * diff, keepdims=True) / D, dg, db]

    dr1, loss_part, dg1, db1 = _row_mm(
        "b_out", [((o, zb), _gate)], w_bout, nt=False, tm=t512, tn=D, tk=_fit(2048, HV),
        outs=[((T, D), F32, 'tile'), ((1, 1), F32, 'acc'), ((1, D), F32, 'acc'), ((1, D), F32, 'acc')],
        epi=loss_epi, epi_ins=[(h1, 'tile'), (g1, 'col'), (b1, 'col'), (tgt, 'tile')])

    def gate_bwd_epi(acc, ins, i, j):
        gate, gate_grad = _silu_and_grad(ins[1].astype(F32))
        return [acc * gate, acc * ins[0].astype(F32) * gate_grad]

    do, dz = _row_mm(
        "b_out_bwd", [((dr1,), None)], w_bout, nt=True, tm=t512, tn=_fit(2048, HV), tk=_fit(1024, D),
        outs=[((T, HV), BF16, 'tile'), ((T, HV), BF16, 'tile')], epi=gate_bwd_epi,
        epi_ins=[(o, 'tile'), (zb, 'tile')])
    gw_bout = _tn_mm("dw_b_out", (o, zb), _gate, [((dr1,), None)], tn=_fit(1024, D), tk=t512, out_dtype=BF16)
    dqn, dqr_pre, dkn, dkr_h, dv = _attn_bwd(q_all, kv, kr, do, o, lse, cos_t, sin_t, H, tq, scale)
    dkr_pre = _key_rope_bwd(dkr_h, cos_t, sin_t, H, t512)

    def cq_bwd_epi(acc, ins, i, j):
        dx, dg = _rms_bwd(acc, ins[0], ins[1])
        return [dx, dg]

    dcq, dqg = _row_mm(
        "q_up_bwd", [((dqn,), None), ((dqr_pre,), None)], w_q, nt=True, tm=t1024, tn=RQ, tk=_fit(2048, HV),
        outs=[((T, RQ), BF16, 'tile'), ((1, RQ), F32, 'acc')], epi=cq_bwd_epi,
        epi_ins=[(pb, pl.BlockSpec((t1024, RQ), lambda i, j, k: (i, WK // RQ))), (q_g, 'col')])
    gw_q = _tn_mm("dw_q_up", (cqn,), None, [((dqn,), None), ((dqr_pre,), None)],
                  tn=_fit(2048, HV), tk=t1024, out_dtype=BF16)

    def ckv_bwd_epi(acc, ins, i, j):
        blk, dkr_t, g = ins
        dx, dg = _rms_bwd(acc, blk[:, :RKV], g)
        parts = [dx, dkr_t]
        if WK > RKV + LANES:
            parts.append(jnp.zeros((dx.shape[0], WK - RKV - LANES), F32))
        return [jnp.concatenate(parts, axis=1), dg]

    dckv, dkvg = _row_mm(
        "kv_up_bwd", [((dkn,), None), ((dv,), None)], w_kv, nt=True, tm=t1024, tn=RKV, tk=_fit(2048, HV),
        outs=[((T, WK), BF16, pl.BlockSpec((t1024, WK), lambda i, j, k: (i, 0))), ((1, RKV), F32, 'acc')],
        epi=ckv_bwd_epi,
        epi_ins=[(pb, pl.BlockSpec((t1024, WK), lambda i, j, k: (i, 0))), (dkr_pre, 'row'), (kv_g, 'col')])
    gw_kv = _tn_mm("dw_kv_up", (c_lat,), None, [((dkn,), None), ((dv,), None)],
                   tn=_fit(2048, HV), tk=t1024, out_dtype=BF16)

    def ln1_bwd_epi(acc, ins, i, j):
        dr_up, xhat, rstd, g = ins
        dr, dg, db = _ln_bwd(alpha * dr_up + acc, xhat, rstd, g)
        return [dr, dg, db]

    dp_segs = [((dckv,), None), ((dcq,), None), ((dz,), None)]
    gw_lat = _tn_mm("dw_b_in", (h1b,), None, dp_segs[:2], tn=tkb, tk=t1024, out_dtype=BF16)
    gw_z = _tn_mm("dw_b_in_gate", (h1b,), None, dp_segs[2:], tn=_fit(2048, HV), tk=t1024, out_dtype=BF16)

    shard_cols = lambda g: jnp.moveaxis(g.reshape(g.shape[0], N_CHIPS, -1), 1, 0)
    gq = gw_q.reshape(RQ, 2, H, LANES)
    g_uq = jnp.concatenate(
        [gq[:, 0], gq[:, 1, :, :ROPE_HALF], gq[:, 1, :, 2 * ROPE_HALF:3 * ROPE_HALF]], axis=2)
    g_wd_full = jnp.concatenate(
        [gw_lat[:, :RKV], gw_lat[:, RKV:RKV + ROPE_HALF],
         gw_lat[:, RKV + 2 * ROPE_HALF:RKV + 3 * ROPE_HALF]], axis=1)
    late_names = ["kv_w_down", "kv_w_uk", "kv_w_uv", "b_w_in", "b_w_uq", "b_w_out"]
    late_w = [kv_w_down, kv_w_uk, kv_w_uv, b_w_in, b_w_uq, b_w_out]
    gw_bin = jnp.concatenate([gw_lat[:, WK:], gw_z], axis=1)
    chip_major = [g_wd_full, gw_kv[:, :HV], gw_kv[:, HV:], shard_cols(gw_bin), g_uq, gw_bout]
    late_grads = [g.reshape((N_CHIPS,) + w.shape) for g, w in zip(chip_major, late_w)]
    late_sems = _push_start("scatter_late_start", late_grads, by_target=True)

    dr0, dg0, db0 = _row_mm(
        "b_in_bwd", dp_segs, wb_all, nt=True, tm=t1024, tn=D, tk=tkb,
        outs=[((T, D), F32, 'tile'), ((1, D), F32, 'acc'), ((1, D), F32, 'acc')], epi=ln1_bwd_epi,
        epi_ins=[(dr1, 'tile'), (xhat1, 'tile'), (rstd1, 'row'), (g0 + late_sems[4][0, 0], 'col')])

    def conv_branch_bwd_epi(acc, ins, i, j):
        u1_t, z, g, b = ins
        xhat, rstd = _ln_stats(u1_t)
        u2 = xhat * g + b
        gate, gate_grad = _silu_and_grad(z)
        act, act_grad = _silu_and_grad(u2)
        dz_a = acc * act * gate_grad
        du1, dg, db = _ln_bwd(acc * gate * act_grad, xhat, rstd, g)
        return [du1, dz_a, dg, db]

    du1, dz_a, dng, dnb = _row_mm(
        "a_out_bwd", [((dr0,), None)], w_out, nt=True, tm=tmw, tn=E, tk=_fit(1024, D),
        outs=[((T, E), F32, 'tile'), ((T, E), BF16, 'tile'), ((1, E), F32, 'acc'), ((1, E), F32, 'acc')],
        epi=conv_branch_bwd_epi,
        epi_ins=[(u1, 'tile'), (proj, pl.BlockSpec((tmw, E), lambda i, j, k: (i, 2))), (norm_g, 'col'),
                 (norm_b, 'col')])
    gw_out, dbo = _tn_mm("dw_a_out", (u4,), None, [((dr0,), None)], tn=_fit(1024, D), tk=t1024,
                         out_dtype=BF16, colsum=True)
    mid_sems = _push_start("scatter_mid_start", [gw_out.reshape((N_CHIPS,) + a_w_out.shape)], by_target=True)
    dval, dgate, dcw, dcb = _conv_bwd(du1, proj, conv_w + mid_sems[4][0, 0], E, tmw)
    dproj_segs = [((dval,), None), ((dgate,), None), ((dz_a,), None)]
    gw_in, dbi = _tn_mm("dw_a_in", (xs,), None, dproj_segs, tn=_fit(2048, E), tk=t1024, out_dtype=BF16,
                        colsum=True)

    chip_word = chip.reshape(1).astype(jnp.int32)

    def reduce_and_update(tag, names_, sent, landed, w_, m_, v_, by_target=True):
        sums = [_sum_chips("sum_" + n, l, s, chip_word, by_target) for n, s, l in zip(names_, sent, landed)]
        theirs = _swap_cores("swap_cores_" + tag, sums)
        return {n: _adamw("adamw_" + n, [mine, other], w, m, v)
                for n, mine, other, w, m, v in zip(names_, sums, theirs, w_, m_, v_)}

    late_sent, late_landed = _push_wait("scatter_late_wait", *late_sems[:4], after=[dbi], by_target=True)
    mid_sent, mid_landed = _push_wait("scatter_mid_wait", *mid_sems[:4], after=[dbi], by_target=True)
    core = lax.axis_index("c")
    chip_half = _chip_sum_of_my_half("chip_sum_a_w_in", gw_in.reshape(a_w_in.shape[:-1] + (3 * E,)),
                                     core.reshape(1).astype(jnp.int32))
    early_sems = _push_start("scatter_early_start", [chip_half], by_target='cols')
    (grad_x,) = _row_mm(
        "a_in_bwd", dproj_segs, g_win, nt=True, tm=t512, tn=D, tk=E, b_whole=True,
        outs=[((T, D), F32, 'tile')], epi=lambda acc, ins, i, j: [alpha * ins[0] + acc + ins[1]],
        epi_ins=[(dr0, 'tile'), (jnp.zeros((1, D), F32) + early_sems[4][0, 0], 'col')])
    big_out = reduce_and_update(
        "late", ["a_w_out"] + late_names, mid_sent + late_sent, mid_landed + late_landed,
        [a_w_out] + late_w, [m_a_w_out, m_kv_w_down, m_kv_w_uk, m_kv_w_uv, m_b_w_in, m_b_w_uq, m_b_w_out],
        [v_a_w_out, v_kv_w_down, v_kv_w_uk, v_kv_w_uv, v_b_w_in, v_b_w_uq, v_b_w_out])

    small_full = [jnp.concatenate([dg0, dg1]), jnp.concatenate([db0, db1]), dbi, dcw, dcb, dng, dnb, dbo,
                  dkvg, dqg, loss_part]
    sflat = jnp.concatenate([g.reshape(-1) for g in small_full])
    summed = _all_reduce_small(_pack_rows(sflat, F32, 8 * LANES)).reshape(-1)
    soff = 0
    sgrads = []
    for g in small_full:
        sgrads.append(summed[soff:soff + g.size].reshape(g.shape))
        soff += g.size
    loss = sgrads.pop()[0, 0]
    local_cols = lambda g, n: lax.dynamic_slice_in_dim(g, chip * n, n, axis=g.ndim - 1)
    snames = ["ln_g", "ln_b", "a_b_in", "a_conv_w", "a_conv_b", "a_norm_g", "a_norm_b", "a_b_out",
              "kv_norm_g", "b_q_norm_g"]
    small_w = [ln_g, ln_b, a_b_in, a_conv_w, a_conv_b, a_norm_g, a_norm_b, a_b_out, kv_norm_g, b_q_norm_g]
    small_m = [m_ln_g, m_ln_b, m_a_b_in, m_a_conv_w, m_a_conv_b, m_a_norm_g, m_a_norm_b, m_a_b_out,
               m_kv_norm_g, m_b_q_norm_g]
    small_v = [v_ln_g, v_ln_b, v_a_b_in, v_a_conv_w, v_a_conv_b, v_a_norm_g, v_a_norm_b, v_a_b_out,
               v_kv_norm_g, v_b_q_norm_g]
    sharded = {"a_b_in", "a_conv_w", "a_conv_b", "a_norm_g", "a_norm_b", "a_b_out"}
    local_g = [(local_cols(g, w.shape[-1]) if n in sharded else g).reshape(w.shape)
               for n, g, w in zip(snames, sgrads, small_w)]
    at_least_2d = lambda a: a.reshape((1,) + a.shape) if a.ndim == 1 else a
    sres = _adamw_vectors([tuple(at_least_2d(a) for a in item)
                           for item in zip(local_g, small_w, small_m, small_v)])
    small_out = {n: [g] + [r.reshape(w.shape) for r in res]
                 for n, g, w, res in zip(snames, local_g, small_w, sres)}

    early_sent, early_landed = _push_wait(
        "scatter_early_wait", *early_sems[:4], by_target='cols',
        after=[grad_x, big_out["b_w_out"][1], small_out["b_q_norm_g"][1]])
    my_rows = _sum_chips("sum_a_w_in", early_landed[0], early_sent[0], chip_word, 'cols')
    (sibling_rows,) = _swap_cores("swap_cores_early", [my_rows])
    g_a_w_in = jnp.concatenate([jnp.where(core == 0, my_rows, sibling_rows),
                                jnp.where(core == 0, sibling_rows, my_rows)], axis=1)
    big_out["a_w_in"] = _adamw("adamw_a_w_in", [g_a_w_in], a_w_in, m_a_w_in, v_a_w_in)

    order =["ln_g", "ln_b", "a_w_in", "a_b_in", "a_conv_w", "a_conv_b", "a_norm_g", "a_norm_b", "a_w_out",
             "a_b_out", "kv_w_down", "kv_norm_g", "kv_w_uk", "kv_w_uv", "b_w_in", "b_q_norm_g", "b_w_uq",
             "b_w_out"]
    outs = {**big_out, **small_out}
    result = [loss, grad_x[None]]
    for part in range(4):
        result += [outs[n][part] for n in order]
    return tuple(result)
```

```python
import functools
import math

import jax
import jax.numpy as jnp
from jax import lax
from jax.experimental import pallas as pl
from jax.experimental.pallas import tpu as pltpu

F32, BF16 = jnp.float32, jnp.bfloat16
NN = (((1,), (0,)), ((), ()))
NT = (((1,), (1,)), ((), ()))
TN = (((0,), (0,)), ((), ()))
MESH = pl.DeviceIdType.MESH
ANY = pl.BlockSpec(memory_space=pl.ANY)

LANES = 128
BF16_ROWS = 16
VMEM_LIMIT = 56 * 1024 * 1024
N_CHIPS = 4
LN_EPS = 1e-5
RMS_EPS = 1e-6
MASK_VALUE = -1e30
LOG2_E = math.log2(math.e)
ROPE_THETA = 10000.0
ROPE_DIM = 64
ROPE_HALF = ROPE_DIM // 2
ADAM_LR, ADAM_B1, ADAM_B2, ADAM_EPS, ADAM_WD, ADAM_STEP = 0.001, 0.9, 0.999, 1e-08, 0.01, 10

MAX_TILE = 2048
TM_WIDE = 256
TQ = 512
CONV_HALO = 32
CONV_LC = 1024
CONV_SUB = 256
SUBLANES = 8
CONV_RC = 32
ADAM_ROWS = 128


def _dot(a, b, dims):
    return lax.dot_general(a.astype(BF16), b.astype(BF16), dims, preferred_element_type=F32)


def _sig(x):
    return 0.5 * jnp.tanh(0.5 * x) + 0.5


def _params(n_axes):
    return pltpu.CompilerParams(dimension_semantics=("arbitrary",) * n_axes, vmem_limit_bytes=VMEM_LIMIT)


def _gcd(*v):
    return functools.reduce(math.gcd, v)


def _fit(want, dim):
    return math.gcd(min(want, MAX_TILE), dim)


def _row_mm(name, a_segs, b, *, nt, tm, tn, tk, outs, epi, epi_ins=(), b_whole=False):
    M = a_segs[0][0][0].shape[0]
    stacked = b.ndim == 3
    if stacked:
        n_blk = b.shape[2]
        N = b.shape[1] if nt else N_CHIPS * n_blk
        assert (nt and b_whole) or (not nt and tn == n_blk and tk == b.shape[1]), name
    else:
        N = b.shape[0] if nt else b.shape[1]
    nkb = [arrs[0].shape[1] // tk for arrs, _ in a_segs]
    koff = [sum(nkb[:s]) for s in range(len(nkb))]
    ni, nj, nk = M // tm, N // tn, sum(nkb)
    assert M % tm == 0 and N % tn == 0 and all(arrs[0].shape[1] % tk == 0 for arrs, _ in a_segs), name
    assert stacked or (b.shape[1] if nt else b.shape[0]) == nk * tk, name

    def spec_of(shape, kind):
        if isinstance(kind, pl.BlockSpec):
            return kind
        if kind == 'tile':
            return pl.BlockSpec((tm, tn), lambda i, j, k: (i, j))
        if kind == 'row':
            return pl.BlockSpec((tm, shape[1]), lambda i, j, k: (i, 0))
        if kind == 'col':
            return pl.BlockSpec((1, tn), lambda i, j, k: (0, j))
        assert kind == 'acc' and nj == 1, name
        return pl.BlockSpec(shape, lambda i, j, k: (0,) * len(shape))

    in_specs, operands = [], []
    for s, (arrs, _) in enumerate(a_segs):
        for arr in arrs:
            in_specs.append(pl.BlockSpec(
                (tm, tk), lambda i, j, k, s=s: (i, jnp.clip(k - koff[s], 0, nkb[s] - 1))))
            operands.append(arr)
    if b_whole:
        assert nt and nj == 1 and all(n == 1 for n in nkb), name
        in_specs.append(pl.BlockSpec(b.shape, lambda i, j, k: (0,) * b.ndim))
    elif stacked:
        in_specs.append(pl.BlockSpec((1, tk, tn), lambda i, j, k: (j, 0, 0)))
    else:
        in_specs.append(pl.BlockSpec((tn, tk), lambda i, j, k: (j, k)) if nt
                        else pl.BlockSpec((tk, tn), lambda i, j, k: (k, j)))
    operands.append(b)
    for arr, kind in epi_ins:
        in_specs.append(spec_of(arr.shape, kind))
        operands.append(arr)
    out_specs = [spec_of(shape, kind) for shape, _, kind in outs]
    out_shape = [jax.ShapeDtypeStruct(shape, dtype) for shape, dtype, _ in outs]
    n_seg_refs = [len(arrs) for arrs, _ in a_segs]

    def body(*refs):
        pos = 0
        seg_refs = []
        for n in n_seg_refs:
            seg_refs.append(refs[pos:pos + n])
            pos += n
        b_ref = refs[pos]
        e_refs = refs[pos + 1:pos + 1 + len(epi_ins)]
        o_refs = refs[pos + 1 + len(epi_ins):pos + 1 + len(epi_ins) + len(outs)]
        i, j, k = pl.program_id(0), pl.program_id(1), pl.program_id(2)

        def product(fn, rs, s=0):
            a = rs[0][...] if fn is None else fn(*[r[...] for r in rs])
            if b_whole and stacked:
                lo, hi, tot = koff[s] * tk, (koff[s] + 1) * tk, None
                for q in range(lo // n_blk, (hi - 1) // n_blk + 1):
                    c0, c1 = max(lo, q * n_blk), min(hi, (q + 1) * n_blk)
                    part = _dot(a[:, c0 - lo:c1 - lo], b_ref[q, :, c0 - q * n_blk:c1 - q * n_blk], NT)
                    tot = part if tot is None else tot + part
                return tot
            if b_whole:
                return _dot(a, b_ref[:, koff[s] * tk:(koff[s] + 1) * tk], NT)
            return _dot(a, b_ref[0] if stacked else b_ref[...], NT if nt else NN)

        def finish(acc):
            res = epi(acc, [r[...] for r in e_refs], i, j)
            for o_ref, (_, _, kind), r in zip(o_refs, outs, res):
                if isinstance(kind, str) and kind == 'acc':
                    @pl.when(i == 0)
                    def _(o_ref=o_ref, r=r):
                        o_ref[...] = r

                    @pl.when(i > 0)
                    def _(o_ref=o_ref, r=r):
                        o_ref[...] += r
                else:
                    o_ref[...] = r.astype(o_ref.dtype)

        if nk == 1:
            finish(product(a_segs[0][1], seg_refs[0]))
            return
        acc_ref = refs[-1]

        @pl.when(k == 0)
        def _():
            acc_ref[...] = jnp.zeros_like(acc_ref)

        for s, ((_, fn), rs) in enumerate(zip(a_segs, seg_refs)):
            def accumulate(fn=fn, rs=rs, s=s):
                acc_ref[...] += product(fn, rs, s)
            if len(a_segs) == 1:
                accumulate()
            else:
                pl.when(jnp.logical_and(k >= koff[s], k < koff[s] + nkb[s]))(accumulate)

        @pl.when(k == nk - 1)
        def _():
            finish(acc_ref[...])

    return pl.pallas_call(
        body, name=name, grid=(ni, nj, nk), in_specs=in_specs, out_specs=out_specs, out_shape=out_shape,
        scratch_shapes=[] if nk == 1 else [pltpu.VMEM((tm, tn), F32)], compiler_params=_params(3),
    )(*operands)


def _tn_mm(name, a_arrs, a_fn, b_segs, *, tn, tk, out_dtype, shard_major=False, colsum=False):
    T, M = a_arrs[0].shape
    nbj = [arrs[0].shape[1] // tn for arrs, _ in b_segs]
    joff = [sum(nbj[:s]) for s in range(len(nbj))]
    nj, nk = sum(nbj), T // tk
    N = nj * tn
    assert T % tk == 0 and all(arrs[0].shape[1] % tn == 0 for arrs, _ in b_segs), name

    in_specs = [pl.BlockSpec((tk, M), lambda j, k: (k, 0)) for _ in a_arrs]
    operands = list(a_arrs)
    for s, (arrs, _) in enumerate(b_segs):
        for arr in arrs:
            in_specs.append(pl.BlockSpec(
                (tk, tn), lambda j, k, s=s: (k, jnp.clip(j - joff[s], 0, nbj[s] - 1))))
            operands.append(arr)
    if shard_major:
        per = (N // N_CHIPS) // tn
        assert per * tn * N_CHIPS == N, name
        out_shape = [jax.ShapeDtypeStruct((N_CHIPS, M, N // N_CHIPS), out_dtype)]
        out_specs = [pl.BlockSpec((1, M, tn), lambda j, k: (j // per, 0, j % per))]
    else:
        out_shape = [jax.ShapeDtypeStruct((M, N), out_dtype)]
        out_specs = [pl.BlockSpec((M, tn), lambda j, k: (0, j))]
    if colsum:
        out_shape.append(jax.ShapeDtypeStruct((1, N), F32))
        out_specs.append(pl.BlockSpec((1, tn), lambda j, k: (0, j)))
    n_a = len(a_arrs)
    n_seg_refs = [len(arrs) for arrs, _ in b_segs]

    def body(*refs):
        a_refs = refs[:n_a]
        pos = n_a
        seg_refs = []
        for n in n_seg_refs:
            seg_refs.append(refs[pos:pos + n])
            pos += n
        o_ref = refs[pos]
        cs_ref = refs[pos + 1] if colsum else None
        acc_ref = refs[-1]
        j, k = pl.program_id(0), pl.program_id(1)

        @pl.when(k == 0)
        def _():
            acc_ref[...] = jnp.zeros_like(acc_ref)
            if colsum:
                cs_ref[...] = jnp.zeros_like(cs_ref)

        for s, ((_, fn), rs) in enumerate(zip(b_segs, seg_refs)):
            def accumulate(fn=fn, rs=rs):
                a = a_refs[0][...] if a_fn is None else a_fn(*[r[...] for r in a_refs])
                bt = rs[0][...] if fn is None else fn(*[r[...] for r in rs])
                acc_ref[...] += _dot(a, bt, TN)
                if colsum:
                    cs_ref[...] += jnp.sum(bt.astype(F32), axis=0, keepdims=True)
            if len(b_segs) == 1:
                accumulate()
            else:
                pl.when(jnp.logical_and(j >= joff[s], j < joff[s] + nbj[s]))(accumulate)

        @pl.when(k == nk - 1)
        def _():
            if shard_major:
                o_ref[0] = acc_ref[...].astype(o_ref.dtype)
            else:
                o_ref[...] = acc_ref[...].astype(o_ref.dtype)

    res = pl.pallas_call(
        body, name=name, grid=(nj, nk), in_specs=in_specs, out_specs=out_specs, out_shape=out_shape,
        scratch_shapes=[pltpu.VMEM((M, tn), F32)], compiler_params=_params(2),
    )(*operands)
    return res if colsum else res[0]


def _silu(z):
    return z * _sig(z)


def _silu_and_grad(z):
    s = _sig(z)
    return z * s, s * (1.0 + z * (1.0 - s))


def _gate(o, z):
    return o.astype(F32) * _silu(z.astype(F32))


def _ln_stats(r):
    mu = jnp.mean(r, axis=1, keepdims=True)
    xc = r - mu
    var = jnp.mean(xc * xc, axis=1, keepdims=True)
    rstd = lax.rsqrt(var + LN_EPS)
    return xc * rstd, rstd


def _ln_bwd(dy, xhat, rstd, g):
    dxh = dy * g
    m1 = jnp.mean(dxh, axis=1, keepdims=True)
    m2 = jnp.mean(dxh * xhat, axis=1, keepdims=True)
    return (rstd * (dxh - m1 - xhat * m2), jnp.sum(dy * xhat, axis=0, keepdims=True),
            jnp.sum(dy, axis=0, keepdims=True))


def _rms_fwd(x, g):
    rstd = lax.rsqrt(jnp.mean(x * x, axis=1, keepdims=True) + RMS_EPS)
    return x * rstd * g


def _rms_bwd(dy, x, g):
    rstd = lax.rsqrt(jnp.mean(x * x, axis=1, keepdims=True) + RMS_EPS)
    xn = x * rstd
    dxn = dy * g
    return rstd * (dxn - xn * jnp.mean(dxn * xn, axis=1, keepdims=True)), jnp.sum(dy * xn, axis=0, keepdims=True)


def _rope(x, cos, sin, transpose=False):
    parts = []
    for g in range(x.shape[1] // LANES):
        xg = x[:, g * LANES:(g + 1) * LANES]
        if transpose:
            parts.append(xg * cos + pltpu.roll(xg * sin, LANES // 2, 1))
        else:
            parts.append(xg * cos + pltpu.roll(xg, LANES // 2, 1) * sin)
    return parts[0] if len(parts) == 1 else jnp.concatenate(parts, axis=1)


def _shifted_rows(window, rc):
    n = window.shape[0]
    for b in range(SUBLANES):
        rolled = window if b == 0 else pltpu.roll(window, n - b, 0)
        for a8 in range(0, n - rc - b + 1, SUBLANES):
            yield a8 + b, rolled[a8:a8 + rc]


def _conv_fwd(proj, conv_w, conv_b, norm_g, norm_b, E, tm):
    T = proj.shape[0]
    kc = conv_w.shape[0]
    hb, rc = CONV_HALO, min(CONV_RC, tm)
    ni, ratio = T // tm, tm // hb
    sub = min(CONV_SUB, E)
    base = hb - (kc - 1)

    def body(val_ref, gate_ref, z_ref, valh_ref, gateh_ref, w_ref, cb_ref, g_ref, b_ref, u1_ref, u4_ref, ubuf):
        i = pl.program_id(0)
        ubuf[hb:, :] = val_ref[...] * _sig(gate_ref[...])
        halo = valh_ref[...] * _sig(gateh_ref[...])
        ubuf[0:hb, :] = jnp.where(i > 0, halo, 0.0)
        for l0 in range(0, E, sub):
            ls = slice(l0, l0 + sub)
            for r0 in range(0, tm, rc):
                acc = jnp.zeros((rc, sub), F32) + cb_ref[:, ls]
                for off, rows in _shifted_rows(ubuf[r0:r0 + hb + rc, ls], rc):
                    if 0 <= off - base < kc:
                        acc += w_ref[off - base:off - base + 1, ls] * rows
                u1_ref[r0:r0 + rc, ls] = acc
        xhat, _ = _ln_stats(u1_ref[...])
        u4_ref[...] = (_silu(xhat * g_ref[...] + b_ref[...]) * _silu(z_ref[...])).astype(BF16)

    main = lambda col: pl.BlockSpec((tm, E), lambda i: (i, col))
    halo = lambda col: pl.BlockSpec((hb, E), lambda i: (jnp.maximum(i * ratio - 1, 0), col))
    whole = lambda a: pl.BlockSpec(a.shape, lambda i: (0, 0))
    return pl.pallas_call(
        body, name="conv_fwd", grid=(ni,),
        in_specs=[main(0), main(1), main(2), halo(0), halo(1), whole(conv_w), whole(conv_b), whole(norm_g),
                  whole(norm_b)],
        out_specs=[main(0), main(0)],
        out_shape=[jax.ShapeDtypeStruct((T, E), F32), jax.ShapeDtypeStruct((T, E), BF16)],
        scratch_shapes=[pltpu.VMEM((hb + tm, E), F32)], compiler_params=_params(1),
    )(proj, proj, proj, proj, proj, conv_w, conv_b, norm_g, norm_b)


def _conv_bwd(du1, proj, conv_w, E, tm):
    T = du1.shape[0]
    kc = conv_w.shape[0]
    lc, hb, rc = min(CONV_LC, E), CONV_HALO, min(CONV_RC, tm)
    nl, ni, ratio = E // lc, T // tm, tm // hb
    gate_off = E // lc
    last_halo = T // hb - 1

    sub = min(CONV_SUB, lc)
    base = hb - (kc - 1)

    def body(du_ref, dun_ref, val_ref, gate_ref, valh_ref, gateh_ref, w_ref,
             dval_ref, dgate_ref, dw_ref, db_ref, ubuf, dbuf, sbuf, dw_sc):
        i = pl.program_id(1)
        sbuf[...] = _sig(gate_ref[...])
        ubuf[hb:, :] = val_ref[...] * sbuf[...]
        halo = valh_ref[...] * _sig(gateh_ref[...])
        ubuf[0:hb, :] = jnp.where(i > 0, halo, 0.0)
        dbuf[0:tm, :] = du_ref[...]
        dbuf[tm:, :] = jnp.where(i < ni - 1, dun_ref[...], 0.0)

        @pl.when(i == 0)
        def _():
            dw_sc[...] = jnp.zeros_like(dw_sc)
            db_ref[...] = jnp.zeros_like(db_ref)

        db_ref[...] += jnp.sum(du_ref[...], axis=0, keepdims=True)
        for l0 in range(0, lc, sub):
            ls = slice(l0, l0 + sub)
            for r0 in range(0, tm, rc):
                dwin = dbuf[r0:r0 + rc + hb, ls]
                dchunk = dwin[0:rc]
                for off, rows in _shifted_rows(ubuf[r0:r0 + hb + rc, ls], rc):
                    k = off - base
                    if 0 <= k < kc:
                        prod = rows * dchunk
                        part = prod[0:SUBLANES]
                        for s8 in range(SUBLANES, rc, SUBLANES):
                            part = part + prod[s8:s8 + SUBLANES]
                        dw_sc[k, :, ls] += part
                acc = jnp.zeros((rc, sub), F32)
                for off, rows in _shifted_rows(dwin, rc):
                    k = (kc - 1) - off
                    if 0 <= k < kc:
                        acc += w_ref[k:k + 1, ls] * rows
                v, s = val_ref[r0:r0 + rc, ls], sbuf[r0:r0 + rc, ls]
                dval_ref[r0:r0 + rc, ls] = (acc * s).astype(BF16)
                dgate_ref[r0:r0 + rc, ls] = (acc * v * s * (1.0 - s)).astype(BF16)

        @pl.when(i == ni - 1)
        def _():
            for k in range(kc):
                dw_ref[k:k + 1, :] = jnp.sum(dw_sc[k], axis=0, keepdims=True)

    return pl.pallas_call(
        body, name="conv_bwd", grid=(nl, ni),
        in_specs=[
            pl.BlockSpec((tm, lc), lambda l, i: (i, l)),
            pl.BlockSpec((hb, lc), lambda l, i: (jnp.minimum((i + 1) * ratio, last_halo), l)),
            pl.BlockSpec((tm, lc), lambda l, i: (i, l)),
            pl.BlockSpec((tm, lc), lambda l, i: (i, gate_off + l)),
            pl.BlockSpec((hb, lc), lambda l, i: (jnp.maximum(i * ratio - 1, 0), l)),
            pl.BlockSpec((hb, lc), lambda l, i: (jnp.maximum(i * ratio - 1, 0), gate_off + l)),
            pl.BlockSpec((kc, lc), lambda l, i: (0, l)),
        ],
        out_specs=[pl.BlockSpec((tm, lc), lambda l, i: (i, l)), pl.BlockSpec((tm, lc), lambda l, i: (i, l)),
                   pl.BlockSpec((kc, lc), lambda l, i: (0, l)), pl.BlockSpec((1, lc), lambda l, i: (0, l))],
        out_shape=[jax.ShapeDtypeStruct((T, E), BF16), jax.ShapeDtypeStruct((T, E), BF16),
                   jax.ShapeDtypeStruct((kc, E), F32), jax.ShapeDtypeStruct((1, E), F32)],
        scratch_shapes=[pltpu.VMEM((hb + tm, lc), F32), pltpu.VMEM((tm + hb, lc), F32),
                        pltpu.VMEM((tm, lc), F32), pltpu.VMEM((kc, SUBLANES, lc), F32)],
        compiler_params=_params(2),
    )(du1, du1, proj, proj, proj, proj, conv_w)


def _attn_fwd(q_all, kv, kr, H, tq, scale):
    T = q_all.shape[0]
    nq = T // tq
    pair = 2
    W = pair * LANES
    assert H % pair == 0
    hp_n = H // pair

    def body(qn_ref, qr_ref, kn_ref, kr_ref, v_ref, o_ref, lse_ref, *scratch):
        qi = pl.program_id(1)
        chains = [scratch[4 * a:4 * a + 4] for a in range(pair)]
        lanes = [slice(a * LANES, (a + 1) * LANES) for a in range(pair)]
        groups = [slice(c * LANES, (c + 1) * LANES) for c in range(tq // LANES)]

        def fold(x, op):
            r = x[:, groups[0]]
            for gsl in groups[1:]:
                r = op(r, x[:, gsl])
            return r

        for _, m_sc, l_sc, acc_sc in chains:
            m_sc[...] = jnp.full_like(m_sc, MASK_VALUE)
            l_sc[...] = jnp.zeros_like(l_sc)
            acc_sc[...] = jnp.zeros_like(acc_sc)

        def scores(j, masked):
            rows = pl.ds(pl.multiple_of(j * tq, tq), tq)
            krope = kr_ref[rows, :]
            for a, (s_sc, m_sc, _, _) in enumerate(chains):
                q = jnp.concatenate([qn_ref[:, lanes[a]], qr_ref[:, lanes[a]]], axis=1)
                k = jnp.concatenate([kn_ref[rows, lanes[a]], krope], axis=1)
                s = _dot(q, k, NT) * (scale * LOG2_E)
                if masked:
                    row = lax.broadcasted_iota(jnp.int32, s.shape, 0)
                    col = lax.broadcasted_iota(jnp.int32, s.shape, 1)
                    s = jnp.where(col <= row, s, MASK_VALUE)
                s_sc[j] = s
                m_sc[...] = jnp.maximum(m_sc[...], fold(s, jnp.maximum))

        def two_per_trip(fn, count):
            def two(p, carry):
                fn(2 * p)
                fn(2 * p + 1)
                return carry

            lax.fori_loop(0, count // 2, two, 0)

            @pl.when(count % 2 == 1)
            def _():
                fn(count - 1)

        two_per_trip(functools.partial(scores, masked=False), qi)
        scores(qi, True)
        for _, m_sc, _, _ in chains:
            m_sc[...] = jnp.broadcast_to(jnp.max(m_sc[...], axis=1, keepdims=True), m_sc.shape)

        def weigh(j):
            rows = pl.ds(pl.multiple_of(j * tq, tq), tq)
            for a, (s_sc, m_sc, l_sc, acc_sc) in enumerate(chains):
                s, m = s_sc[j], m_sc[...]
                p = jnp.concatenate([jnp.exp2(s[:, gsl] - m) for gsl in groups], axis=1)
                l_sc[...] += fold(p, jnp.add)
                acc_sc[...] += _dot(p, v_ref[rows, lanes[a]], NN)

        two_per_trip(weigh, qi + 1)
        for a, (_, m_sc, l_sc, acc_sc) in enumerate(chains):
            l = jnp.sum(l_sc[...], axis=1, keepdims=True)
            o_ref[:, lanes[a]] = (acc_sc[...] / l).astype(BF16)
            lse_ref[a] = m_sc[:, 0:1] * (1.0 / LOG2_E) + jnp.log(l)

    chain_scratch = [pltpu.VMEM((nq, tq, tq), F32), pltpu.VMEM((tq, LANES), F32), pltpu.VMEM((tq, LANES), F32),
                     pltpu.VMEM((tq, LANES), F32)]
    return pl.pallas_call(
        body, name="attn_fwd", grid=(hp_n, nq),
        in_specs=[pl.BlockSpec((tq, W), lambda hp, qi: (qi, hp)),
                  pl.BlockSpec((tq, W), lambda hp, qi: (qi, hp_n + hp)),
                  pl.BlockSpec((T, W), lambda hp, qi: (0, hp)),
                  pl.BlockSpec((T, LANES), lambda hp, qi: (0, 0)),
                  pl.BlockSpec((T, W), lambda hp, qi: (0, hp_n + hp))],
        out_specs=[pl.BlockSpec((tq, W), lambda hp, qi: (qi, hp)),
                   pl.BlockSpec((pair, tq, 1), lambda hp, qi: (hp, qi, 0))],
        out_shape=[jax.ShapeDtypeStruct((T, H * LANES), BF16), jax.ShapeDtypeStruct((H, T, 1), F32)],
        scratch_shapes=chain_scratch * pair, compiler_params=_params(2),
    )(q_all, q_all, kv, kr, kv)


def _attn_bwd(q_all, kv, kr, do, o, lse, cos, sin, H, tq, scale):
    T = q_all.shape[0]
    nq = T // tq
    HV = H * LANES
    pair = 2
    tk2 = pair * tq
    ng = T // tk2
    assert ng * tk2 == T and pair == 2

    def body(qn_ref, qr_ref, kn_ref, kr_ref, v_ref, do_ref, o_ref, lse_ref, cos_ref, sin_ref,
             dqn_ref, dqr_ref, dkn_ref, dkr_ref, dv_ref, dq_sc, dk_sc, dv_sc):
        g = pl.program_id(1)

        @pl.when(g == 0)
        def _():
            dq_sc[...] = jnp.zeros_like(dq_sc)

        key_rows = [slice(c * tq, (c + 1) * tq) for c in range(pair)]

        def block(qi, modes):
            rows = pl.ds(pl.multiple_of(qi * tq, tq), tq)
            q = jnp.concatenate([qn_ref[rows, :], qr_ref[rows, :]], axis=1)
            dov = do_ref[rows, :]
            delta = jnp.sum(dov.astype(F32) * o_ref[rows, :].astype(F32), axis=1, keepdims=True)
            lse_q = lse_ref[0, rows, :]
            dq, dkv = None, []
            for kr_, masked in zip(key_rows, modes):
                if masked is None:
                    dkv.append(None)
                    continue
                k = jnp.concatenate([kn_ref[kr_, :], kr_ref[kr_, :]], axis=1)
                s = _dot(q, k, NT) * scale
                if masked:
                    row = lax.broadcasted_iota(jnp.int32, s.shape, 0)
                    col = lax.broadcasted_iota(jnp.int32, s.shape, 1)
                    s = jnp.where(col <= row, s, MASK_VALUE)
                p = jnp.exp(s - lse_q)
                dv = _dot(p, dov, TN)
                dp = _dot(dov, v_ref[kr_, :], NT)
                ds = (p * (dp - delta) * scale).astype(BF16)
                dkv.append((_dot(ds, q, TN), dv))
                part = _dot(ds, k, NN)
                dq = part if dq is None else dq + part
            return rows, dq, dkv

        rows_a, dq_a, (kv_a0, _) = block(pair * g, (True, None))
        rows_b, dq_b, (kv_b0, kv_b1) = block(pair * g + 1, (False, True))
        dk_sc[key_rows[0], :] = kv_a0[0] + kv_b0[0]
        dv_sc[key_rows[0], :] = kv_a0[1] + kv_b0[1]
        dk_sc[key_rows[1], :] = kv_b1[0]
        dv_sc[key_rows[1], :] = kv_b1[1]
        dq_sc[rows_a, :] += dq_a
        dq_sc[rows_b, :] += dq_b

        def below(trip, carry):
            for qi in (pair * g + pair + 2 * trip, pair * g + pair + 2 * trip + 1):
                rows, dq, dkv = block(qi, (False, False))
                for kr_, (dk, dv) in zip(key_rows, dkv):
                    dk_sc[kr_, :] += dk
                    dv_sc[kr_, :] += dv
                dq_sc[rows, :] += dq
            return carry

        lax.fori_loop(0, (nq - pair * g - pair) // 2, below, 0)
        dkn_ref[...] = dk_sc[:, :LANES].astype(BF16)
        dkr_ref[...] = dk_sc[:, LANES:].astype(BF16)
        dv_ref[...] = dv_sc[...].astype(BF16)

        @pl.when(g == ng - 1)
        def _():
            dqn_ref[...] = dq_sc[:, :LANES].astype(BF16)
            dqr_ref[...] = _rope(dq_sc[:, LANES:], cos_ref[...], sin_ref[...], transpose=True).astype(BF16)

    whole = lambda col: pl.BlockSpec((T, LANES), col)
    tile = lambda col: pl.BlockSpec((tk2, LANES), col)
    return pl.pallas_call(
        body, name="attn_bwd", grid=(H, ng),
        in_specs=[whole(lambda h, g: (0, h)), whole(lambda h, g: (0, H + h)),
                  tile(lambda h, g: (g, h)), tile(lambda h, g: (g, 0)), tile(lambda h, g: (g, H + h)),
                  whole(lambda h, g: (0, h)), whole(lambda h, g: (0, h)),
                  pl.BlockSpec((1, T, 1), lambda h, g: (h, 0, 0)),
                  whole(lambda h, g: (0, 0)), whole(lambda h, g: (0, 0))],
        out_specs=[whole(lambda h, g: (0, h)), whole(lambda h, g: (0, h)),
                   tile(lambda h, g: (g, h)), tile(lambda h, g: (g, h)), tile(lambda h, g: (g, h))],
        out_shape=[jax.ShapeDtypeStruct((T, HV), BF16), jax.ShapeDtypeStruct((T, HV), BF16),
                   jax.ShapeDtypeStruct((T, HV), BF16), jax.ShapeDtypeStruct((T, HV), BF16),
                   jax.ShapeDtypeStruct((T, HV), BF16)],
        scratch_shapes=[pltpu.VMEM((T, 2 * LANES), F32), pltpu.VMEM((tk2, 2 * LANES), F32),
                        pltpu.VMEM((tk2, LANES), F32)],
        compiler_params=_params(2),
    )(q_all, q_all, kv, kr, kv, do, o, lse, cos, sin)


def _key_rope_bwd(dkr_heads, cos, sin, H, tm):
    T, HV = dkr_heads.shape

    def body(dkr_ref, cos_ref, sin_ref, dk_ref):
        dk = dkr_ref[...].astype(F32)
        tot = dk[:, 0:LANES]
        for h in range(1, H):
            tot = tot + dk[:, h * LANES:(h + 1) * LANES]
        dk_ref[...] = _rope(tot, cos_ref[...], sin_ref[...], transpose=True)

    return pl.pallas_call(
        body, name="key_rope_bwd", grid=(T // tm,),
        in_specs=[pl.BlockSpec((tm, HV), lambda i: (i, 0)),
                  pl.BlockSpec((tm, LANES), lambda i: (i, 0)), pl.BlockSpec((tm, LANES), lambda i: (i, 0))],
        out_specs=pl.BlockSpec((tm, LANES), lambda i: (i, 0)),
        out_shape=jax.ShapeDtypeStruct((T, LANES), F32), compiler_params=_params(1),
    )(dkr_heads, cos, sin)


def _place():
    x, y, c = lax.axis_index("x"), lax.axis_index("y"), lax.axis_index("c")
    chips = [(1 - x, y), (x, 1 - y), (1 - x, 1 - y)]
    return x, y, c, chips


def _all_gather_chips(arrs):
    n = len(arrs)
    halves = [a.shape[0] // 2 for a in arrs]
    assert all(h * 2 == a.shape[0] and h % BF16_ROWS == 0 for h, a in zip(halves, arrs))

    def body(*refs):
        w_refs, out_refs = refs[:n], refs[n:2 * n]
        send_sems, recv_sems, local_sems = refs[2 * n:]
        x, y, c, chips = _place()
        sibling = (x, y, 1 - c)
        waits = []
        for w, (w_ref, out_ref, half) in enumerate(zip(w_refs, out_refs, halves)):
            def region(px, py, pc, out_ref=out_ref, half=half):
                return out_ref.at[2 * px + py, pl.ds(pc * half, half), :]

            def copy(k, block, to, src=None, w=w, region=region):
                return pltpu.make_async_remote_copy(
                    src_ref=region(*block) if src is None else src, dst_ref=region(*block),
                    send_sem=send_sems.at[6 * w + k], recv_sem=recv_sems.at[6 * w + k],
                    device_id=to, device_id_type=MESH)

            mine = pltpu.make_async_copy(w_ref, out_ref.at[2 * x + y], local_sems.at[w])
            mine.start()
            my_half = w_ref.at[pl.ds(c * half, half), :]
            first = [copy(j, (x, y, c), (*chip, c), src=my_half) for j, chip in enumerate(chips)]
            for cp in first:
                cp.start()
            waits.append((copy, mine, first))
        for copy, mine, first in waits:
            passed = [copy(3 + j, (*chip, c), sibling) for j, chip in enumerate(chips)]
            for j, chip in enumerate(chips):
                copy(j, (*chip, c), (x, y, c)).wait_recv()
                passed[j].start()
            for j, chip in enumerate(chips):
                copy(3 + j, (*chip, 1 - c), (x, y, c)).wait_recv()
            for cp in first + passed:
                cp.wait_send()
            mine.wait()

    return pl.pallas_call(
        body, name="gather_weights", in_specs=[ANY] * n, out_specs=[ANY] * n,
        out_shape=[jax.ShapeDtypeStruct((N_CHIPS,) + a.shape, a.dtype) for a in arrs],
        scratch_shapes=[pltpu.SemaphoreType.DMA((6 * n,)), pltpu.SemaphoreType.DMA((6 * n,)),
                        pltpu.SemaphoreType.DMA((n,))],
    )(*arrs)


def _swap_cores(name, parts):
    n = len(parts)

    def body(*refs):
        p_refs, r_refs = refs[:n], refs[n:2 * n]
        send_sems, recv_sems = refs[2 * n:]
        x, y, c, _ = _place()
        copies = [pltpu.make_async_remote_copy(
            src_ref=p_refs[w], dst_ref=r_refs[w], send_sem=send_sems.at[w], recv_sem=recv_sems.at[w],
            device_id=(x, y, 1 - c), device_id_type=MESH) for w in range(n)]
        for cp in copies:
            cp.start()
        for cp in copies:
            cp.wait()

    return pl.pallas_call(
        body, name=name, in_specs=[ANY] * n, out_specs=[ANY] * n,
        out_shape=[jax.ShapeDtypeStruct(p.shape, p.dtype) for p in parts],
        scratch_shapes=[pltpu.SemaphoreType.DMA((n,)), pltpu.SemaphoreType.DMA((n,))],
    )(*parts)


def _chip_sum_of_my_half(name, a, core):
    _, rows, cols = a.shape
    half = rows // 2
    tr = _gcd(ADAM_ROWS, half)
    steps = half // tr

    def swap_body(a_ref, r_ref, send_sem, recv_sem):
        x, y, c, _ = _place()
        cp = pltpu.make_async_remote_copy(
            src_ref=a_ref.at[:, pl.ds((1 - c) * half, half), :], dst_ref=r_ref, send_sem=send_sem,
            recv_sem=recv_sem, device_id=(x, y, 1 - c), device_id_type=MESH)
        cp.start()
        cp.wait()

    theirs = pl.pallas_call(
        swap_body, name=name + "_swap", in_specs=[ANY], out_specs=ANY,
        out_shape=jax.ShapeDtypeStruct((1, half, cols), a.dtype),
        scratch_shapes=[pltpu.SemaphoreType.DMA, pltpu.SemaphoreType.DMA],
    )(a)

    def add_body(core_ref, a_ref, r_ref, o_ref):
        o_ref[...] = (a_ref[...].astype(F32) + r_ref[...].astype(F32)).astype(o_ref.dtype)

    return pl.pallas_call(
        add_body, name=name + "_add",
        grid_spec=pltpu.PrefetchScalarGridSpec(
            num_scalar_prefetch=1, grid=(steps,),
            in_specs=[pl.BlockSpec((1, tr, cols), lambda i, c: (0, c[0] * steps + i, 0)),
                      pl.BlockSpec((1, tr, cols), lambda i, c: (0, i, 0))],
            out_specs=pl.BlockSpec((1, tr, cols), lambda i, c: (0, i, 0))),
        out_shape=jax.ShapeDtypeStruct((1, half, cols), a.dtype), compiler_params=_params(1),
    )(core, a, theirs)


HBM = pl.BlockSpec(memory_space=pltpu.HBM)
SEM = pl.BlockSpec(memory_space=pltpu.SEMAPHORE)
EFFECT = pltpu.SideEffectType.DATAFLOW_SIDE_EFFECTING


def _push_copies(a_refs, l_refs, send_sems, recv_sems, by_target):
    x, y, c, chips = _place()
    me = 2 * x + y

    def part(a_ref, q):
        if by_target == 'cols':
            n = a_ref.shape[-1] // N_CHIPS
            return a_ref.at[(slice(None),) * (len(a_ref.shape) - 1) + (pl.ds(pl.multiple_of(q * n, LANES), n),)]
        return a_ref.at[q] if by_target else a_ref

    out = []
    for w, (a_ref, l_ref) in enumerate(zip(a_refs, l_refs)):
        for j, (px, py) in enumerate(chips):
            peer = 2 * px + py
            out.append((
                pltpu.make_async_remote_copy(
                    src_ref=part(a_ref, peer), dst_ref=l_ref.at[me],
                    send_sem=send_sems.at[3 * w + j], recv_sem=recv_sems.at[3 * w + j],
                    device_id=(px, py, c), device_id_type=MESH),
                pltpu.make_async_remote_copy(
                    src_ref=part(a_ref, me), dst_ref=l_ref.at[peer],
                    send_sem=send_sems.at[3 * w + j], recv_sem=recv_sems.at[3 * w + j],
                    device_id=(px, py, c), device_id_type=MESH)))
    return out


def _landing_shape(a, by_target):
    if by_target == 'cols':
        return (N_CHIPS,) + a.shape[:-1] + (a.shape[-1] // N_CHIPS,)
    return (N_CHIPS,) + (a.shape[1:] if by_target else a.shape)


def _push_start(name, arrs, by_target):
    n = len(arrs)
    lands = [lax.empty(_landing_shape(a, by_target), a.dtype) for a in arrs]

    def body(*refs):
        a_refs, l_refs = refs[:n], refs[n:2 * n]
        send_sems, recv_sems = refs[2 * n], refs[2 * n + 1]
        token = refs[-1]
        for send, _ in _push_copies(a_refs, l_refs, send_sems, recv_sems, by_target):
            send.start()
        token[...] = jnp.zeros_like(token)

    res = pl.pallas_call(
        body, name=name,
        out_shape=(pltpu.SemaphoreType.DMA((3 * n,)), pltpu.SemaphoreType.DMA((3 * n,)),
                   *[pltpu.HBM(a.shape, a.dtype) for a in arrs], *[pltpu.HBM(l.shape, l.dtype) for l in lands],
                   jax.ShapeDtypeStruct((8, LANES), F32)),
        in_specs=[HBM] * (2 * n), out_specs=(SEM, SEM, *[HBM] * (2 * n), pl.BlockSpec(memory_space=pltpu.VMEM)),
        input_output_aliases={i: 2 + i for i in range(2 * n)},
        compiler_params=pltpu.CompilerParams(has_side_effects=EFFECT),
    )(*[pltpu.with_memory_space_constraint(a, pltpu.HBM) for a in list(arrs) + lands])
    return res[0], res[1], list(res[2:2 + n]), list(res[2 + n:2 + 2 * n]), res[-1]


def _push_wait(name, send_sems, recv_sems, arrs, lands, after, by_target):
    n = len(arrs)

    def body(*refs):
        a_refs, l_refs = refs[:n], refs[n:2 * n]
        s_sems, r_sems = refs[2 * n], refs[2 * n + 1]
        for send, recv in _push_copies(a_refs, l_refs, s_sems, r_sems, by_target):
            send.wait_send()
            recv.wait_recv()

    res = pl.pallas_call(
        body, name=name,
        out_shape=[pltpu.HBM(a.shape, a.dtype) for a in list(arrs) + list(lands)],
        in_specs=[HBM] * (2 * n) + [SEM, SEM] + [ANY] * len(after), out_specs=[HBM] * (2 * n),
        input_output_aliases={i: i for i in range(2 * n)},
        compiler_params=pltpu.CompilerParams(has_side_effects=EFFECT),
    )(*arrs, *lands, send_sems, recv_sems, *after)
    return list(res[:n]), list(res[n:])


def _all_reduce_small(part):
    def body(p_ref, out_ref, sib_buf, chip_buf, send_sems, recv_sems):
        x, y, c, chips = _place()
        me = 2 * x + y
        swap = pltpu.make_async_remote_copy(
            src_ref=p_ref, dst_ref=sib_buf, send_sem=send_sems.at[0], recv_sem=recv_sems.at[0],
            device_id=(x, y, 1 - c), device_id_type=MESH)
        swap.start()
        swap.wait()
        chip_buf[me] = p_ref[...] + sib_buf[...]
        copies = []
        for j, (px, py) in enumerate(chips):
            cp = pltpu.make_async_remote_copy(
                src_ref=chip_buf.at[me], dst_ref=chip_buf.at[me], send_sem=send_sems.at[1 + j],
                recv_sem=recv_sems.at[1 + j], device_id=(px, py, c), device_id_type=MESH)
            cp.start()
            copies.append(cp)
        for j, (px, py) in enumerate(chips):
            pltpu.make_async_remote_copy(
                src_ref=chip_buf.at[me], dst_ref=chip_buf.at[2 * px + py], send_sem=send_sems.at[1 + j],
                recv_sem=recv_sems.at[1 + j], device_id=(px, py, c), device_id_type=MESH).wait_recv()
        for cp in copies:
            cp.wait_send()
        tot = chip_buf[0]
        for q in range(1, N_CHIPS):
            tot = tot + chip_buf[q]
        out_ref[...] = tot

    vm = pl.BlockSpec(memory_space=pltpu.VMEM)
    return pl.pallas_call(
        body, name="all_reduce_small", in_specs=[vm], out_specs=vm,
        out_shape=jax.ShapeDtypeStruct(part.shape, F32),
        scratch_shapes=[pltpu.VMEM(part.shape, F32), pltpu.VMEM((N_CHIPS,) + part.shape, F32),
                        pltpu.SemaphoreType.DMA((N_CHIPS,)), pltpu.SemaphoreType.DMA((N_CHIPS,))],
    )(part)


def _row_tiles(shape):
    ax = next(d for d, s in enumerate(shape) if s > 1)
    tr = _gcd(ADAM_ROWS, shape[ax])
    block = tuple(tr if d == ax else s for d, s in enumerate(shape))
    return shape[ax] // tr, block, lambda i: tuple(i if d == ax else 0 for d in range(len(shape)))


def _sum_chips(name, landed, sent, chip, by_target):
    shape = landed.shape[1:]
    steps, block, index = _row_tiles(shape)
    if by_target == 'cols':
        own_spec = pl.BlockSpec(block, lambda i, c: index(i)[:-1] + (c[0],))
    else:
        own_spec = pl.BlockSpec((1,) + block, lambda i, c: (c[0],) + index(i))

    def body(chip_ref, l_ref, s_ref, o_ref):
        own = (s_ref[...] if by_target == 'cols' else s_ref[0]).astype(F32)
        tot = None
        for q in range(N_CHIPS):
            term = jnp.where(chip_ref[0] == q, own, l_ref[q].astype(F32))
            tot = term if tot is None else tot + term
        o_ref[...] = tot

    return pl.pallas_call(
        body, name=name,
        grid_spec=pltpu.PrefetchScalarGridSpec(
            num_scalar_prefetch=1, grid=(steps,),
            in_specs=[pl.BlockSpec((N_CHIPS,) + block, lambda i, c: (0,) + index(i)), own_spec],
            out_specs=pl.BlockSpec(block, lambda i, c: index(i))),
        out_shape=jax.ShapeDtypeStruct(shape, F32), compiler_params=_params(1),
    )(chip, landed, sent)


def _adamw_math(g, w, m, v):
    mn = ADAM_B1 * m + (1.0 - ADAM_B1) * g
    vn = ADAM_B2 * v + (1.0 - ADAM_B2) * jnp.square(g)
    m_hat = mn / (1.0 - ADAM_B1 ** ADAM_STEP)
    v_hat = vn / (1.0 - ADAM_B2 ** ADAM_STEP)
    return -ADAM_LR * (m_hat / (jnp.sqrt(v_hat) + ADAM_EPS) + ADAM_WD * w), mn, vn


def _adamw(name, g_parts, w, m, v):
    steps, block, index = _row_tiles(w.shape)
    n = len(g_parts)

    def body(*refs):
        g = refs[0][...]
        for r in refs[1:n]:
            g = g + r[...]
        w_ref, m_ref, v_ref, go_ref, d_ref, mo_ref, vo_ref = refs[n:]
        go_ref[...] = g
        d_ref[...], mo_ref[...], vo_ref[...] = _adamw_math(g, w_ref[...], m_ref[...], v_ref[...])

    spec = pl.BlockSpec(block, index)
    return pl.pallas_call(
        body, name=name, grid=(steps,), in_specs=[spec] * (n + 3), out_specs=[spec] * 4,
        out_shape=[jax.ShapeDtypeStruct(w.shape, F32)] * 4, compiler_params=_params(1),
    )(*g_parts, w, m, v)


def _adamw_row_halves(name, mine, theirs, core, w, m, v):
    _, rows, cols = w.shape
    tr = _gcd(ADAM_ROWS, rows // 2)
    per_half = rows // 2 // tr
    whole = pl.BlockSpec((1, tr, cols), lambda i, c: (0, i, 0))
    part = pl.BlockSpec((1, tr, cols), lambda i, c: (0, i % per_half, 0))

    def body(core_ref, mine_ref, theirs_ref, w_ref, m_ref, v_ref, go_ref, d_ref, mo_ref, vo_ref):
        in_my_half = pl.program_id(0) // per_half == core_ref[0]
        g = jnp.where(in_my_half, mine_ref[...], theirs_ref[...])
        go_ref[...] = g
        d_ref[...], mo_ref[...], vo_ref[...] = _adamw_math(g, w_ref[...], m_ref[...], v_ref[...])

    return pl.pallas_call(
        body, name=name,
        grid_spec=pltpu.PrefetchScalarGridSpec(
            num_scalar_prefetch=1, grid=(2 * per_half,), in_specs=[part, part, whole, whole, whole],
            out_specs=[whole] * 4),
        out_shape=[jax.ShapeDtypeStruct(w.shape, F32)] * 4, compiler_params=_params(1),
    )(core, mine, theirs, w, m, v)


def _adamw_vectors(items):
    n = len(items)

    def body(*refs):
        ins, outs = refs[:4 * n], refs[4 * n:]
        for k in range(n):
            g, w, m, v = (r[...] for r in ins[4 * k:4 * k + 4])
            outs[3 * k][...], outs[3 * k + 1][...], outs[3 * k + 2][...] = _adamw_math(g, w, m, v)

    vm = pl.BlockSpec(memory_space=pltpu.VMEM)
    res = pl.pallas_call(
        body, name="adamw_vectors", in_specs=[vm] * (4 * n), out_specs=[vm] * (3 * n),
        out_shape=[jax.ShapeDtypeStruct(it[1].shape, F32) for it in items for _ in range(3)],
    )(*[a for it in items for a in it])
    return [res[3 * k:3 * k + 3] for k in range(n)]


def _pack_rows(flat, dtype, multiple):
    n = flat.shape[0]
    total = -(-n // multiple) * multiple
    return jnp.pad(flat, (0, total - n)).astype(dtype).reshape(total // LANES, LANES)


def kernel(x, positions, ln_g, ln_b, a_w_in, a_b_in, a_conv_w, a_conv_b, a_norm_g, a_norm_b, a_w_out, a_b_out, kv_w_down, kv_norm_g, kv_w_uk, kv_w_uv, b_w_in, b_q_norm_g, b_w_uq, b_w_out, loss_target, m_ln_g, m_ln_b, m_a_w_in, m_a_b_in, m_a_conv_w, m_a_conv_b, m_a_norm_g, m_a_norm_b, m_a_w_out, m_a_b_out, m_kv_w_down, m_kv_norm_g, m_kv_w_uk, m_kv_w_uv, m_b_w_in, m_b_q_norm_g, m_b_w_uq, m_b_w_out, v_ln_g, v_ln_b, v_a_w_in, v_a_b_in, v_a_conv_w, v_a_conv_b, v_a_norm_g, v_a_norm_b, v_a_w_out, v_a_b_out, v_kv_w_down, v_kv_norm_g, v_kv_w_uk, v_kv_w_uv, v_b_w_in, v_b_q_norm_g, v_b_w_uq, v_b_w_out):
    T, D = x.shape[1], x.shape[2]
    E = N_CHIPS * a_w_out.shape[1]
    RKV = kv_norm_g.shape[0]
    H, DN = kv_w_uk.shape[1], kv_w_uk.shape[2]
    RQ = b_q_norm_g.shape[1]
    HV = N_CHIPS * b_w_out.shape[1]
    assert DN == LANES and kv_w_uv.shape[2] == LANES and HV == H * LANES
    assert kv_w_down.shape[1] == RKV + ROPE_DIM and b_w_uq.shape[3] == DN + ROPE_DIM
    assert ln_g.shape[0] == 2 and a_w_in.shape[0] == 1 and b_w_in.shape[0] == 1
    alpha = (2.0 * ln_g.shape[0]) ** 0.25
    scale = 1.0 / math.sqrt(DN + ROPE_DIM)
    WK = -(-(RKV + LANES) // 256) * 256
    assert WK % RQ == 0
    Z_OFF = WK + RQ
    tmw, tq = min(TM_WIDE, T), min(TQ, T)
    t512, t1024 = _fit(512, T), _fit(1024, T)
    xs = x[0]
    tgt = loss_target[0]
    px, py = lax.axis_index("x"), lax.axis_index("y")
    chip = 2 * px + py

    mats = [a_w_out[0], kv_w_down, kv_w_uk, kv_w_uv, b_w_in[0], b_w_uq[0], b_w_out[0]]
    vecs = [a_b_in[0], a_conv_w[0], a_conv_b[0], a_norm_g[0], a_norm_b[0], a_b_out[0]]
    vec_bits = jnp.concatenate([lax.bitcast_convert_type(w.reshape(-1), BF16).reshape(-1) for w in vecs])
    rest = [w.astype(BF16) for w in mats]
    g_win, gathered = _all_gather_chips(
        [a_w_in[0].astype(BF16), _pack_rows(vec_bits, BF16, 2 * BF16_ROWS * LANES)])
    gathered = gathered.reshape(N_CHIPS, -1)
    gathered, rest = lax.optimization_barrier((gathered, rest))
    rest_sems = _push_start("gather_rest_start", rest, by_target=False)
    off = 0
    fvec = []
    for w in vecs:
        bits = gathered[:, off:off + 2 * w.size].reshape((N_CHIPS,) + w.shape + (2,))
        fvec.append(lax.bitcast_convert_type(bits, F32))
        off += 2 * w.size
    cols = lambda g: jnp.moveaxis(g, 0, -2).reshape(g.shape[1:-1] + (N_CHIPS * g.shape[-1],))
    b_in = cols(fvec[0][:, None, :])
    conv_w = cols(fvec[1])
    conv_b, norm_g, norm_b, b_out = (cols(f[:, None, :]) for f in fvec[2:])
    row = lambda a: a.reshape(1, -1)
    g0, b0, g1, b1 = row(ln_g[0]), row(ln_b[0]), row(ln_g[1]), row(ln_b[1])
    kv_g, q_g = row(kv_norm_g), row(b_q_norm_g[0])
    plain = lambda acc, ins, i, j: [acc]

    b_in = b_in + rest_sems[4][0, 0]
    (proj,) = _row_mm("a_in", [((xs,), None)], g_win, nt=False, tm=t1024, tn=3 * E // N_CHIPS, tk=D,
                      outs=[((T, 3 * E), F32, 'tile')], epi=lambda acc, ins, i, j: [acc + ins[0]],
                      epi_ins=[(b_in, 'col')])
    u1, u4 = _conv_fwd(proj, conv_w, conv_b, norm_g, norm_b, E, tmw)

    rest, landed = _push_wait("gather_rest_wait", *rest_sems[:4], after=[u4], by_target=False)
    g_wout, g_wd, g_uk, g_uv, g_wbin, g_wuq, g_wbout = [
        lax.dynamic_update_slice(l, w[None], (chip,) + (0,) * w.ndim) for w, l in zip(rest, landed)]
    w_out = g_wout.reshape(E, D)
    wd = g_wd.reshape(D, RKV + ROPE_DIM)
    zpad = jnp.zeros((D, ROPE_HALF), BF16)
    wd_p = jnp.concatenate(
        [wd[:, :RKV], wd[:, RKV:RKV + ROPE_HALF], zpad, wd[:, RKV + ROPE_HALF:], zpad,
         jnp.zeros((D, WK - RKV - LANES), BF16)], axis=1)
    w_bin = cols(g_wbin)
    w_z = w_bin[:, RQ:]
    wb_small = jnp.concatenate([wd_p, w_bin[:, :RQ]], axis=1)
    wb_all = jnp.concatenate([wd_p, w_bin], axis=1)
    w_kv = jnp.concatenate([g_uk.reshape(RKV, HV), g_uv.reshape(RKV, HV)], axis=1)
    wuq = g_wuq.reshape(RQ, H, DN + ROPE_DIM)
    zq = jnp.zeros((RQ, H, ROPE_HALF), BF16)
    w_qr = jnp.concatenate([wuq[:, :, DN:DN + ROPE_HALF], zq, wuq[:, :, DN + ROPE_HALF:], zq], axis=2)
    w_q = jnp.concatenate([wuq[:, :, :DN].reshape(RQ, HV), w_qr.reshape(RQ, HV)], axis=1)
    w_bout = g_wbout.reshape(HV, D)

    freqs = ROPE_THETA ** (-jnp.arange(0, ROPE_DIM, 2, dtype=F32) / ROPE_DIM)
    ang = positions[0].astype(F32)[:, None] * freqs
    cs, sn = jnp.cos(ang), jnp.sin(ang)
    ones, zeros = jnp.ones_like(cs), jnp.zeros_like(cs)
    cos_t = jnp.concatenate([cs, ones, cs, ones], axis=1)
    sin_t = jnp.concatenate([-sn, zeros, sn, zeros], axis=1)

    def ln_epi(acc, ins, i, j):
        bias, res, g, b = ins
        xhat, rstd = _ln_stats(alpha * res + acc + bias)
        h = xhat * g + b
        return [h, h, xhat, rstd]

    h1, h1b, xhat1, rstd1 = _row_mm(
        "a_out", [((u4,), None)], w_out, nt=False, tm=t512, tn=D, tk=_fit(2048, E),
        outs=[((T, D), F32, 'tile'), ((T, D), BF16, 'tile'), ((T, D), F32, 'tile'), ((T, 1), F32, 'row')],
        epi=ln_epi, epi_ins=[(b_out, 'col'), (xs, 'tile'), (g0, 'col'), (b0, 'col')])

    tkb = _fit(512, _gcd(WK, RQ, HV))
    def latents_epi(acc, ins, i, j):
        kg, qg, cos_, sin_ = ins
        return [acc, _rms_fwd(acc[:, :RKV], kg), _rope(acc[:, RKV:RKV + LANES], cos_, sin_),
                _rms_fwd(acc[:, WK:WK + RQ], qg)]

    whole_row = lambda a: (a, pl.BlockSpec(a.shape, lambda i, j, k: (0, 0)))
    pb, c_lat, kr, cqn = _row_mm(
        "b_in", [((h1b,), None)], wb_small, nt=False, tm=t512, tn=Z_OFF, tk=_fit(1024, D),
        outs=[((T, Z_OFF), F32, 'tile'), ((T, RKV), BF16, 'row'), ((T, LANES), BF16, 'row'),
              ((T, RQ), BF16, 'row')],
        epi=latents_epi, epi_ins=[whole_row(kv_g), whole_row(q_g), (cos_t, 'row'), (sin_t, 'row')])
    (zb,) = _row_mm("b_in_gate", [((h1b,), None)], w_z, nt=False, tm=t1024, tn=_fit(2048, HV),
                    tk=_fit(1024, D), outs=[((T, HV), BF16, 'tile')], epi=plain)
    (kv,) = _row_mm("kv_up", [((c_lat,), None)], w_kv, nt=False, tm=t1024, tn=_fit(2048, HV),
                    tk=_fit(1024, RKV), outs=[((T, 2 * HV), BF16, 'tile')], epi=plain)
    tnq = _fit(2048, HV)
    half_q = HV // tnq

    def q_epi(acc, ins, i, j):
        return [jnp.where(j >= half_q, _rope(acc, ins[0], ins[1]), acc)]

    (q_all,) = _row_mm("q_up", [((cqn,), None)], w_q, nt=False, tm=t1024, tn=tnq, tk=_fit(1024, RQ),
                       outs=[((T, 2 * HV), BF16, 'tile')], epi=q_epi,
                       epi_ins=[(cos_t, 'row'), (sin_t, 'row')])
    o, lse = _attn_fwd(q_all, kv, kr, H, tq, scale)

    def loss_epi(acc, ins, i, j):
        res, g, b, target = ins
        xhat, rstd = _ln_stats(alpha * res + acc)
        diff = xhat * g + b - target
        dr, dg, db = _ln_bwd(diff / D, xhat, rstd, g)
        return [dr, 0.5 * jnp.sum(diff * diff, keepdims=True) / D, dg, db]

    dr1, loss_part, dg1, db1 = _row_mm(
        "b_out", [((o, zb), _gate)], w_bout, nt=False, tm=t512, tn=D, tk=_fit(2048, HV),
        outs=[((T, D), F32, 'tile'), ((1, 1), F32, 'acc'), ((1, D), F32, 'acc'), ((1, D), F32, 'acc')],
        epi=loss_epi, epi_ins=[(h1, 'tile'), (g1, 'col'), (b1, 'col'), (tgt, 'tile')])

    def gate_bwd_epi(acc, ins, i, j):
        gate, gate_grad = _silu_and_grad(ins[1].astype(F32))
        return [acc * gate, acc * ins[0].astype(F32) * gate_grad]

    do, dz = _row_mm(
        "b_out_bwd", [((dr1,), None)], w_bout, nt=True, tm=t512, tn=_fit(2048, HV), tk=_fit(1024, D),
        outs=[((T, HV), BF16, 'tile'), ((T, HV), BF16, 'tile')], epi=gate_bwd_epi,
        epi_ins=[(o, 'tile'), (zb, 'tile')])
    gw_bout = _tn_mm("dw_b_out", (o, zb), _gate, [((dr1,), None)], tn=_fit(1024, D), tk=t512, out_dtype=BF16)
    dqn, dqr_pre, dkn, dkr_h, dv = _attn_bwd(q_all, kv, kr, do, o, lse, cos_t, sin_t, H, tq, scale)
    dkr_pre = _key_rope_bwd(dkr_h, cos_t, sin_t, H, t512)

    def cq_bwd_epi(acc, ins, i, j):
        dx, dg = _rms_bwd(acc, ins[0], ins[1])
        return [dx, dg]

    dcq, dqg = _row_mm(
        "q_up_bwd", [((dqn,), None), ((dqr_pre,), None)], w_q, nt=True, tm=t1024, tn=RQ, tk=_fit(2048, HV),
        outs=[((T, RQ), BF16, 'tile'), ((1, RQ), F32, 'acc')], epi=cq_bwd_epi,
        epi_ins=[(pb, pl.BlockSpec((t1024, RQ), lambda i, j, k: (i, WK // RQ))), (q_g, 'col')])
    gw_q = _tn_mm("dw_q_up", (cqn,), None, [((dqn,), None), ((dqr_pre,), None)],
                  tn=_fit(2048, HV), tk=t1024, out_dtype=BF16)

    def ckv_bwd_epi(acc, ins, i, j):
        blk, dkr_t, g = ins
        dx, dg = _rms_bwd(acc, blk[:, :RKV], g)
        parts = [dx, dkr_t]
        if WK > RKV + LANES:
            parts.append(jnp.zeros((dx.shape[0], WK - RKV - LANES), F32))
        return [jnp.concatenate(parts, axis=1), dg]

    dckv, dkvg = _row_mm(
        "kv_up_bwd", [((dkn,), None), ((dv,), None)], w_kv, nt=True, tm=t1024, tn=RKV, tk=_fit(2048, HV),
        outs=[((T, WK), BF16, pl.BlockSpec((t1024, WK), lambda i, j, k: (i, 0))), ((1, RKV), F32, 'acc')],
        epi=ckv_bwd_epi,
        epi_ins=[(pb, pl.BlockSpec((t1024, WK), lambda i, j, k: (i, 0))), (dkr_pre, 'row'), (kv_g, 'col')])
    gw_kv = _tn_mm("dw_kv_up", (c_lat,), None, [((dkn,), None), ((dv,), None)],
                   tn=_fit(2048, HV), tk=t1024, out_dtype=BF16)

    def ln1_bwd_epi(acc, ins, i, j):
        dr_up, xhat, rstd, g = ins
        dr, dg, db = _ln_bwd(alpha * dr_up + acc, xhat, rstd, g)
        return [dr, dg, db]

    dp_segs = [((dckv,), None), ((dcq,), None), ((dz,), None)]
    gw_lat = _tn_mm("dw_b_in", (h1b,), None, dp_segs[:2], tn=tkb, tk=t1024, out_dtype=BF16)
    gw_z = _tn_mm("dw_b_in_gate", (h1b,), None, dp_segs[2:], tn=_fit(2048, HV), tk=t1024, out_dtype=BF16)

    shard_cols = lambda g: jnp.moveaxis(g.reshape(g.shape[0], N_CHIPS, -1), 1, 0)
    gq = gw_q.reshape(RQ, 2, H, LANES)
    g_uq = jnp.concatenate(
        [gq[:, 0], gq[:, 1, :, :ROPE_HALF], gq[:, 1, :, 2 * ROPE_HALF:3 * ROPE_HALF]], axis=2)
    g_wd_full = jnp.concatenate(
        [gw_lat[:, :RKV], gw_lat[:, RKV:RKV + ROPE_HALF],
         gw_lat[:, RKV + 2 * ROPE_HALF:RKV + 3 * ROPE_HALF]], axis=1)
    late_names = ["kv_w_down", "kv_w_uk", "kv_w_uv", "b_w_in", "b_w_uq", "b_w_out"]
    late_w = [kv_w_down, kv_w_uk, kv_w_uv, b_w_in, b_w_uq, b_w_out]
    gw_bin = jnp.concatenate([gw_lat[:, WK:], gw_z], axis=1)
    chip_major = [g_wd_full, gw_kv[:, :HV], gw_kv[:, HV:], shard_cols(gw_bin), g_uq, gw_bout]
    late_grads = [g.reshape((N_CHIPS,) + w.shape) for g, w in zip(chip_major, late_w)]
    late_sems = _push_start("scatter_late_start", late_grads, by_target=True)

    dr0, dg0, db0 = _row_mm(
        "b_in_bwd", dp_segs, wb_all, nt=True, tm=t1024, tn=D, tk=tkb,
        outs=[((T, D), F32, 'tile'), ((1, D), F32, 'acc'), ((1, D), F32, 'acc')], epi=ln1_bwd_epi,
        epi_ins=[(dr1, 'tile'), (xhat1, 'tile'), (rstd1, 'row'), (g0 + late_sems[4][0, 0], 'col')])

    def conv_branch_bwd_epi(acc, ins, i, j):
        u1_t, z, g, b = ins
        xhat, rstd = _ln_stats(u1_t)
        u2 = xhat * g + b
        gate, gate_grad = _silu_and_grad(z)
        act, act_grad = _silu_and_grad(u2)
        dz_a = acc * act * gate_grad
        du1, dg, db = _ln_bwd(acc * gate * act_grad, xhat, rstd, g)
        return [du1, dz_a, dg, db]

    du1, dz_a, dng, dnb = _row_mm(
        "a_out_bwd", [((dr0,), None)], w_out, nt=True, tm=tmw, tn=E, tk=_fit(1024, D),
        outs=[((T, E), F32, 'tile'), ((T, E), BF16, 'tile'), ((1, E), F32, 'acc'), ((1, E), F32, 'acc')],
        epi=conv_branch_bwd_epi,
        epi_ins=[(u1, 'tile'), (proj, pl.BlockSpec((tmw, E), lambda i, j, k: (i, 2))), (norm_g, 'col'),
                 (norm_b, 'col')])
    gw_out, dbo = _tn_mm("dw_a_out", (u4,), None, [((dr0,), None)], tn=_fit(1024, D), tk=t1024,
                         out_dtype=BF16, colsum=True)
    mid_sems = _push_start("scatter_mid_start", [gw_out.reshape((N_CHIPS,) + a_w_out.shape)], by_target=True)
    dval, dgate, dcw, dcb = _conv_bwd(du1, proj, conv_w + mid_sems[4][0, 0], E, tmw)
    dproj_segs = [((dval,), None), ((dgate,), None), ((dz_a,), None)]
    gw_in, dbi = _tn_mm("dw_a_in", (xs,), None, dproj_segs, tn=_fit(2048, E), tk=t1024, out_dtype=BF16,
                        colsum=True)

    chip_word = chip.reshape(1).astype(jnp.int32)

    def reduce_and_update(tag, names_, sent, landed, w_, m_, v_, by_target=True):
        sums = [_sum_chips("sum_" + n, l, s, chip_word, by_target) for n, s, l in zip(names_, sent, landed)]
        theirs = _swap_cores("swap_cores_" + tag, sums)
        return {n: _adamw("adamw_" + n, [mine, other], w, m, v)
                for n, mine, other, w, m, v in zip(names_, sums, theirs, w_, m_, v_)}

    late_sent, late_landed = _push_wait("scatter_late_wait", *late_sems[:4], after=[dbi], by_target=True)
    mid_sent, mid_landed = _push_wait("scatter_mid_wait", *mid_sems[:4], after=[dbi], by_target=True)
    core_word = lax.axis_index("c").reshape(1).astype(jnp.int32)
    chip_half = _chip_sum_of_my_half("chip_sum_a_w_in", gw_in.reshape(a_w_in.shape[:-1] + (3 * E,)), core_word)
    early_sems = _push_start("scatter_early_start", [chip_half], by_target='cols')
    (grad_x,) = _row_mm(
        "a_in_bwd", dproj_segs, g_win, nt=True, tm=t512, tn=D, tk=E, b_whole=True,
        outs=[((T, D), F32, 'tile')], epi=lambda acc, ins, i, j: [alpha * ins[0] + acc + ins[1]],
        epi_ins=[(dr0, 'tile'), (jnp.zeros((1, D), F32) + early_sems[4][0, 0], 'col')])
    big_out = reduce_and_update(
        "late", ["a_w_out"] + late_names, mid_sent + late_sent, mid_landed + late_landed,
        [a_w_out] + late_w, [m_a_w_out, m_kv_w_down, m_kv_w_uk, m_kv_w_uv, m_b_w_in, m_b_w_uq, m_b_w_out],
        [v_a_w_out, v_kv_w_down, v_kv_w_uk, v_kv_w_uv, v_b_w_in, v_b_w_uq, v_b_w_out])

    small_full = [jnp.concatenate([dg0, dg1]), jnp.concatenate([db0, db1]), dbi, dcw, dcb, dng, dnb, dbo,
                  dkvg, dqg, loss_part]
    sflat = jnp.concatenate([g.reshape(-1) for g in small_full])
    summed = _all_reduce_small(_pack_rows(sflat, F32, 8 * LANES)).reshape(-1)
    soff = 0
    sgrads = []
    for g in small_full:
        sgrads.append(summed[soff:soff + g.size].reshape(g.shape))
        soff += g.size
    loss = sgrads.pop()[0, 0]
    local_cols = lambda g, n: lax.dynamic_slice_in_dim(g, chip * n, n, axis=g.ndim - 1)
    snames = ["ln_g", "ln_b", "a_b_in", "a_conv_w", "a_conv_b", "a_norm_g", "a_norm_b", "a_b_out",
              "kv_norm_g", "b_q_norm_g"]
    small_w = [ln_g, ln_b, a_b_in, a_conv_w, a_conv_b, a_norm_g, a_norm_b, a_b_out, kv_norm_g, b_q_norm_g]
    small_m = [m_ln_g, m_ln_b, m_a_b_in, m_a_conv_w, m_a_conv_b, m_a_norm_g, m_a_norm_b, m_a_b_out,
               m_kv_norm_g, m_b_q_norm_g]
    small_v = [v_ln_g, v_ln_b, v_a_b_in, v_a_conv_w, v_a_conv_b, v_a_norm_g, v_a_norm_b, v_a_b_out,
               v_kv_norm_g, v_b_q_norm_g]
    sharded = {"a_b_in", "a_conv_w", "a_conv_b", "a_norm_g", "a_norm_b", "a_b_out"}
    local_g = [(local_cols(g, w.shape[-1]) if n in sharded else g).reshape(w.shape)
               for n, g, w in zip(snames, sgrads, small_w)]
    at_least_2d = lambda a: a.reshape((1,) + a.shape) if a.ndim == 1 else a
    sres = _adamw_vectors([tuple(at_least_2d(a) for a in item)
                           for item in zip(local_g, small_w, small_m, small_v)])
    small_out = {n: [g] + [r.reshape(w.shape) for r in res]
                 for n, g, w, res in zip(snames, local_g, small_w, sres)}

    early_sent, early_landed = _push_wait(
        "scatter_early_wait", *early_sems[:4], by_target='cols',
        after=[grad_x, big_out["b_w_out"][1], small_out["b_q_norm_g"][1]])
    my_rows = _sum_chips("sum_a_w_in", early_landed[0], early_sent[0], chip_word, 'cols')
    (sibling_rows,) = _swap_cores("swap_cores_early", [my_rows])
    big_out["a_w_in"] = _adamw_row_halves("adamw_a_w_in", my_rows, sibling_rows, core_word, a_w_in, m_a_w_in,
                                          v_a_w_in)

    order =["ln_g", "ln_b", "a_w_in", "a_b_in", "a_conv_w", "a_conv_b", "a_norm_g", "a_norm_b", "a_w_out",
             "a_b_out", "kv_w_down", "kv_norm_g", "kv_w_uk", "kv_w_uv", "b_w_in", "b_q_norm_g", "b_w_uq",
             "b_w_out"]
    outs = {**big_out, **small_out}
    result = [loss, grad_x[None]]
    for part in range(4):
        result += [outs[n][part] for n in order]
    return tuple(result)
```

```python
import functools
import math

import jax
import jax.numpy as jnp
from jax import lax
from jax.experimental import pallas as pl
from jax.experimental.pallas import tpu as pltpu

F32, BF16 = jnp.float32, jnp.bfloat16
NN = (((1,), (0,)), ((), ()))
NT = (((1,), (1,)), ((), ()))
TN = (((0,), (0,)), ((), ()))
MESH = pl.DeviceIdType.MESH
ANY = pl.BlockSpec(memory_space=pl.ANY)

LANES = 128
BF16_ROWS = 16
VMEM_LIMIT = 56 * 1024 * 1024
N_CHIPS = 4
LN_EPS = 1e-5
RMS_EPS = 1e-6
MASK_VALUE = -1e30
LOG2_E = math.log2(math.e)
ROPE_THETA = 10000.0
ROPE_DIM = 64
ROPE_HALF = ROPE_DIM // 2
ADAM_LR, ADAM_B1, ADAM_B2, ADAM_EPS, ADAM_WD, ADAM_STEP = 0.001, 0.9, 0.999, 1e-08, 0.01, 10

MAX_TILE = 2048
TM_WIDE = 256
TQ = 512
CONV_HALO = 32
CONV_LC = 1024
CONV_SUB = 256
SUBLANES = 8
CONV_RC = 32
ADAM_ROWS = 256


def _dot(a, b, dims):
    return lax.dot_general(a.astype(BF16), b.astype(BF16), dims, preferred_element_type=F32)


def _sig(x):
    return 0.5 * jnp.tanh(0.5 * x) + 0.5


def _params(n_axes):
    return pltpu.CompilerParams(dimension_semantics=("arbitrary",) * n_axes, vmem_limit_bytes=VMEM_LIMIT)


def _gcd(*v):
    return functools.reduce(math.gcd, v)


def _fit(want, dim):
    return math.gcd(min(want, MAX_TILE), dim)


def _row_mm(name, a_segs, b, *, nt, tm, tn, tk, outs, epi, epi_ins=(), b_whole=False):
    M = a_segs[0][0][0].shape[0]
    stacked = b.ndim == 3
    if stacked:
        n_blk = b.shape[2]
        N = b.shape[1] if nt else N_CHIPS * n_blk
        assert (nt and b_whole) or (not nt and tn == n_blk and tk == b.shape[1]), name
    else:
        N = b.shape[0] if nt else b.shape[1]
    nkb = [arrs[0].shape[1] // tk for arrs, _ in a_segs]
    koff = [sum(nkb[:s]) for s in range(len(nkb))]
    ni, nj, nk = M // tm, N // tn, sum(nkb)
    assert M % tm == 0 and N % tn == 0 and all(arrs[0].shape[1] % tk == 0 for arrs, _ in a_segs), name
    assert stacked or (b.shape[1] if nt else b.shape[0]) == nk * tk, name

    def spec_of(shape, kind):
        if isinstance(kind, pl.BlockSpec):
            return kind
        if kind == 'tile':
            return pl.BlockSpec((tm, tn), lambda i, j, k: (i, j))
        if kind == 'row':
            return pl.BlockSpec((tm, shape[1]), lambda i, j, k: (i, 0))
        if kind == 'col':
            return pl.BlockSpec((1, tn), lambda i, j, k: (0, j))
        assert kind == 'acc' and nj == 1, name
        return pl.BlockSpec(shape, lambda i, j, k: (0,) * len(shape))

    in_specs, operands = [], []
    for s, (arrs, _) in enumerate(a_segs):
        for arr in arrs:
            in_specs.append(pl.BlockSpec(
                (tm, tk), lambda i, j, k, s=s: (i, jnp.clip(k - koff[s], 0, nkb[s] - 1))))
            operands.append(arr)
    if b_whole:
        assert nt and nj == 1 and all(n == 1 for n in nkb), name
        in_specs.append(pl.BlockSpec(b.shape, lambda i, j, k: (0,) * b.ndim))
    elif stacked:
        in_specs.append(pl.BlockSpec((1, tk, tn), lambda i, j, k: (j, 0, 0)))
    else:
        in_specs.append(pl.BlockSpec((tn, tk), lambda i, j, k: (j, k)) if nt
                        else pl.BlockSpec((tk, tn), lambda i, j, k: (k, j)))
    operands.append(b)
    for arr, kind in epi_ins:
        in_specs.append(spec_of(arr.shape, kind))
        operands.append(arr)
    out_specs = [spec_of(shape, kind) for shape, _, kind in outs]
    out_shape = [jax.ShapeDtypeStruct(shape, dtype) for shape, dtype, _ in outs]
    n_seg_refs = [len(arrs) for arrs, _ in a_segs]

    def body(*refs):
        pos = 0
        seg_refs = []
        for n in n_seg_refs:
            seg_refs.append(refs[pos:pos + n])
            pos += n
        b_ref = refs[pos]
        e_refs = refs[pos + 1:pos + 1 + len(epi_ins)]
        o_refs = refs[pos + 1 + len(epi_ins):pos + 1 + len(epi_ins) + len(outs)]
        i, j, k = pl.program_id(0), pl.program_id(1), pl.program_id(2)

        def product(fn, rs, s=0):
            a = rs[0][...] if fn is None else fn(*[r[...] for r in rs])
            if b_whole and stacked:
                lo, hi, tot = koff[s] * tk, (koff[s] + 1) * tk, None
                for q in range(lo // n_blk, (hi - 1) // n_blk + 1):
                    c0, c1 = max(lo, q * n_blk), min(hi, (q + 1) * n_blk)
                    part = _dot(a[:, c0 - lo:c1 - lo], b_ref[q, :, c0 - q * n_blk:c1 - q * n_blk], NT)
                    tot = part if tot is None else tot + part
                return tot
            if b_whole:
                return _dot(a, b_ref[:, koff[s] * tk:(koff[s] + 1) * tk], NT)
            return _dot(a, b_ref[0] if stacked else b_ref[...], NT if nt else NN)

        def finish(acc):
            res = epi(acc, [r[...] for r in e_refs], i, j)
            for o_ref, (_, _, kind), r in zip(o_refs, outs, res):
                if isinstance(kind, str) and kind == 'acc':
                    @pl.when(i == 0)
                    def _(o_ref=o_ref, r=r):
                        o_ref[...] = r

                    @pl.when(i > 0)
                    def _(o_ref=o_ref, r=r):
                        o_ref[...] += r
                else:
                    o_ref[...] = r.astype(o_ref.dtype)

        if nk == 1:
            finish(product(a_segs[0][1], seg_refs[0]))
            return
        acc_ref = refs[-1]

        @pl.when(k == 0)
        def _():
            acc_ref[...] = jnp.zeros_like(acc_ref)

        for s, ((_, fn), rs) in enumerate(zip(a_segs, seg_refs)):
            def accumulate(fn=fn, rs=rs, s=s):
                acc_ref[...] += product(fn, rs, s)
            if len(a_segs) == 1:
                accumulate()
            else:
                pl.when(jnp.logical_and(k >= koff[s], k < koff[s] + nkb[s]))(accumulate)

        @pl.when(k == nk - 1)
        def _():
            finish(acc_ref[...])

    return pl.pallas_call(
        body, name=name, grid=(ni, nj, nk), in_specs=in_specs, out_specs=out_specs, out_shape=out_shape,
        scratch_shapes=[] if nk == 1 else [pltpu.VMEM((tm, tn), F32)], compiler_params=_params(3),
    )(*operands)


def _tn_mm(name, a_arrs, a_fn, b_segs, *, tn, tk, out_dtype, shard_major=False, colsum=False):
    T, M = a_arrs[0].shape
    nbj = [arrs[0].shape[1] // tn for arrs, _ in b_segs]
    joff = [sum(nbj[:s]) for s in range(len(nbj))]
    nj, nk = sum(nbj), T // tk
    N = nj * tn
    assert T % tk == 0 and all(arrs[0].shape[1] % tn == 0 for arrs, _ in b_segs), name

    in_specs = [pl.BlockSpec((tk, M), lambda j, k: (k, 0)) for _ in a_arrs]
    operands = list(a_arrs)
    for s, (arrs, _) in enumerate(b_segs):
        for arr in arrs:
            in_specs.append(pl.BlockSpec(
                (tk, tn), lambda j, k, s=s: (k, jnp.clip(j - joff[s], 0, nbj[s] - 1))))
            operands.append(arr)
    if shard_major:
        per = (N // N_CHIPS) // tn
        assert per * tn * N_CHIPS == N, name
        out_shape = [jax.ShapeDtypeStruct((N_CHIPS, M, N // N_CHIPS), out_dtype)]
        out_specs = [pl.BlockSpec((1, M, tn), lambda j, k: (j // per, 0, j % per))]
    else:
        out_shape = [jax.ShapeDtypeStruct((M, N), out_dtype)]
        out_specs = [pl.BlockSpec((M, tn), lambda j, k: (0, j))]
    if colsum:
        out_shape.append(jax.ShapeDtypeStruct((1, N), F32))
        out_specs.append(pl.BlockSpec((1, tn), lambda j, k: (0, j)))
    n_a = len(a_arrs)
    n_seg_refs = [len(arrs) for arrs, _ in b_segs]

    def body(*refs):
        a_refs = refs[:n_a]
        pos = n_a
        seg_refs = []
        for n in n_seg_refs:
            seg_refs.append(refs[pos:pos + n])
            pos += n
        o_ref = refs[pos]
        cs_ref = refs[pos + 1] if colsum else None
        acc_ref = refs[-1]
        j, k = pl.program_id(0), pl.program_id(1)

        @pl.when(k == 0)
        def _():
            acc_ref[...] = jnp.zeros_like(acc_ref)
            if colsum:
                cs_ref[...] = jnp.zeros_like(cs_ref)

        for s, ((_, fn), rs) in enumerate(zip(b_segs, seg_refs)):
            def accumulate(fn=fn, rs=rs):
                a = a_refs[0][...] if a_fn is None else a_fn(*[r[...] for r in a_refs])
                bt = rs[0][...] if fn is None else fn(*[r[...] for r in rs])
                acc_ref[...] += _dot(a, bt, TN)
                if colsum:
                    cs_ref[...] += jnp.sum(bt.astype(F32), axis=0, keepdims=True)
            if len(b_segs) == 1:
                accumulate()
            else:
                pl.when(jnp.logical_and(j >= joff[s], j < joff[s] + nbj[s]))(accumulate)

        @pl.when(k == nk - 1)
        def _():
            if shard_major:
                o_ref[0] = acc_ref[...].astype(o_ref.dtype)
            else:
                o_ref[...] = acc_ref[...].astype(o_ref.dtype)

    res = pl.pallas_call(
        body, name=name, grid=(nj, nk), in_specs=in_specs, out_specs=out_specs, out_shape=out_shape,
        scratch_shapes=[pltpu.VMEM((M, tn), F32)], compiler_params=_params(2),
    )(*operands)
    return res if colsum else res[0]


def _silu(z):
    return z * _sig(z)


def _silu_and_grad(z):
    s = _sig(z)
    return z * s, s * (1.0 + z * (1.0 - s))


def _gate(o, z):
    return o.astype(F32) * _silu(z.astype(F32))


def _ln_stats(r):
    mu = jnp.mean(r, axis=1, keepdims=True)
    xc = r - mu
    var = jnp.mean(xc * xc, axis=1, keepdims=True)
    rstd = lax.rsqrt(var + LN_EPS)
    return xc * rstd, rstd


def _ln_bwd(dy, xhat, rstd, g):
    dxh = dy * g
    m1 = jnp.mean(dxh, axis=1, keepdims=True)
    m2 = jnp.mean(dxh * xhat, axis=1, keepdims=True)
    return (rstd * (dxh - m1 - xhat * m2), jnp.sum(dy * xhat, axis=0, keepdims=True),
            jnp.sum(dy, axis=0, keepdims=True))


def _rms_fwd(x, g):
    rstd = lax.rsqrt(jnp.mean(x * x, axis=1, keepdims=True) + RMS_EPS)
    return x * rstd * g


def _rms_bwd(dy, x, g):
    rstd = lax.rsqrt(jnp.mean(x * x, axis=1, keepdims=True) + RMS_EPS)
    xn = x * rstd
    dxn = dy * g
    return rstd * (dxn - xn * jnp.mean(dxn * xn, axis=1, keepdims=True)), jnp.sum(dy * xn, axis=0, keepdims=True)


def _rope(x, cos, sin, transpose=False):
    parts = []
    for g in range(x.shape[1] // LANES):
        xg = x[:, g * LANES:(g + 1) * LANES]
        if transpose:
            parts.append(xg * cos + pltpu.roll(xg * sin, LANES // 2, 1))
        else:
            parts.append(xg * cos + pltpu.roll(xg, LANES // 2, 1) * sin)
    return parts[0] if len(parts) == 1 else jnp.concatenate(parts, axis=1)


def _shifted_rows(window, rc):
    n = window.shape[0]
    for b in range(SUBLANES):
        rolled = window if b == 0 else pltpu.roll(window, n - b, 0)
        for a8 in range(0, n - rc - b + 1, SUBLANES):
            yield a8 + b, rolled[a8:a8 + rc]


def _conv_fwd(proj, conv_w, conv_b, norm_g, norm_b, E, tm):
    T = proj.shape[0]
    kc = conv_w.shape[0]
    hb, rc = CONV_HALO, min(CONV_RC, tm)
    ni, ratio = T // tm, tm // hb
    sub = min(CONV_SUB, E)
    base = hb - (kc - 1)

    def body(val_ref, gate_ref, z_ref, valh_ref, gateh_ref, w_ref, cb_ref, g_ref, b_ref, u1_ref, u4_ref, ubuf):
        i = pl.program_id(0)
        ubuf[hb:, :] = val_ref[...] * _sig(gate_ref[...])
        halo = valh_ref[...] * _sig(gateh_ref[...])
        ubuf[0:hb, :] = jnp.where(i > 0, halo, 0.0)
        for l0 in range(0, E, sub):
            ls = slice(l0, l0 + sub)
            for r0 in range(0, tm, rc):
                acc = jnp.zeros((rc, sub), F32) + cb_ref[:, ls]
                for off, rows in _shifted_rows(ubuf[r0:r0 + hb + rc, ls], rc):
                    if 0 <= off - base < kc:
                        acc += w_ref[off - base:off - base + 1, ls] * rows
                u1_ref[r0:r0 + rc, ls] = acc
        xhat, _ = _ln_stats(u1_ref[...])
        u4_ref[...] = (_silu(xhat * g_ref[...] + b_ref[...]) * _silu(z_ref[...])).astype(BF16)

    main = lambda col: pl.BlockSpec((tm, E), lambda i: (i, col))
    halo = lambda col: pl.BlockSpec((hb, E), lambda i: (jnp.maximum(i * ratio - 1, 0), col))
    whole = lambda a: pl.BlockSpec(a.shape, lambda i: (0, 0))
    return pl.pallas_call(
        body, name="conv_fwd", grid=(ni,),
        in_specs=[main(0), main(1), main(2), halo(0), halo(1), whole(conv_w), whole(conv_b), whole(norm_g),
                  whole(norm_b)],
        out_specs=[main(0), main(0)],
        out_shape=[jax.ShapeDtypeStruct((T, E), F32), jax.ShapeDtypeStruct((T, E), BF16)],
        scratch_shapes=[pltpu.VMEM((hb + tm, E), F32)], compiler_params=_params(1),
    )(proj, proj, proj, proj, proj, conv_w, conv_b, norm_g, norm_b)


def _conv_bwd(du1, proj, conv_w, E, tm):
    T = du1.shape[0]
    kc = conv_w.shape[0]
    lc, hb, rc = min(CONV_LC, E), CONV_HALO, min(CONV_RC, tm)
    nl, ni, ratio = E // lc, T // tm, tm // hb
    gate_off = E // lc
    last_halo = T // hb - 1

    sub = min(CONV_SUB, lc)
    base = hb - (kc - 1)

    def body(du_ref, dun_ref, val_ref, gate_ref, valh_ref, gateh_ref, w_ref,
             dval_ref, dgate_ref, dw_ref, db_ref, ubuf, dbuf, sbuf, dw_sc):
        i = pl.program_id(1)
        sbuf[...] = _sig(gate_ref[...])
        ubuf[hb:, :] = val_ref[...] * sbuf[...]
        halo = valh_ref[...] * _sig(gateh_ref[...])
        ubuf[0:hb, :] = jnp.where(i > 0, halo, 0.0)
        dbuf[0:tm, :] = du_ref[...]
        dbuf[tm:, :] = jnp.where(i < ni - 1, dun_ref[...], 0.0)

        @pl.when(i == 0)
        def _():
            dw_sc[...] = jnp.zeros_like(dw_sc)
            db_ref[...] = jnp.zeros_like(db_ref)

        db_ref[...] += jnp.sum(du_ref[...], axis=0, keepdims=True)
        for l0 in range(0, lc, sub):
            ls = slice(l0, l0 + sub)
            for r0 in range(0, tm, rc):
                dwin = dbuf[r0:r0 + rc + hb, ls]
                dchunk = dwin[0:rc]
                for off, rows in _shifted_rows(ubuf[r0:r0 + hb + rc, ls], rc):
                    k = off - base
                    if 0 <= k < kc:
                        prod = rows * dchunk
                        part = prod[0:SUBLANES]
                        for s8 in range(SUBLANES, rc, SUBLANES):
                            part = part + prod[s8:s8 + SUBLANES]
                        dw_sc[k, :, ls] += part
                acc = jnp.zeros((rc, sub), F32)
                for off, rows in _shifted_rows(dwin, rc):
                    k = (kc - 1) - off
                    if 0 <= k < kc:
                        acc += w_ref[k:k + 1, ls] * rows
                v, s = val_ref[r0:r0 + rc, ls], sbuf[r0:r0 + rc, ls]
                dval_ref[r0:r0 + rc, ls] = (acc * s).astype(BF16)
                dgate_ref[r0:r0 + rc, ls] = (acc * v * s * (1.0 - s)).astype(BF16)

        @pl.when(i == ni - 1)
        def _():
            for k in range(kc):
                dw_ref[k:k + 1, :] = jnp.sum(dw_sc[k], axis=0, keepdims=True)

    return pl.pallas_call(
        body, name="conv_bwd", grid=(nl, ni),
        in_specs=[
            pl.BlockSpec((tm, lc), lambda l, i: (i, l)),
            pl.BlockSpec((hb, lc), lambda l, i: (jnp.minimum((i + 1) * ratio, last_halo), l)),
            pl.BlockSpec((tm, lc), lambda l, i: (i, l)),
            pl.BlockSpec((tm, lc), lambda l, i: (i, gate_off + l)),
            pl.BlockSpec((hb, lc), lambda l, i: (jnp.maximum(i * ratio - 1, 0), l)),
            pl.BlockSpec((hb, lc), lambda l, i: (jnp.maximum(i * ratio - 1, 0), gate_off + l)),
            pl.BlockSpec((kc, lc), lambda l, i: (0, l)),
        ],
        out_specs=[pl.BlockSpec((tm, lc), lambda l, i: (i, l)), pl.BlockSpec((tm, lc), lambda l, i: (i, l)),
                   pl.BlockSpec((kc, lc), lambda l, i: (0, l)), pl.BlockSpec((1, lc), lambda l, i: (0, l))],
        out_shape=[jax.ShapeDtypeStruct((T, E), BF16), jax.ShapeDtypeStruct((T, E), BF16),
                   jax.ShapeDtypeStruct((kc, E), F32), jax.ShapeDtypeStruct((1, E), F32)],
        scratch_shapes=[pltpu.VMEM((hb + tm, lc), F32), pltpu.VMEM((tm + hb, lc), F32),
                        pltpu.VMEM((tm, lc), F32), pltpu.VMEM((kc, SUBLANES, lc), F32)],
        compiler_params=_params(2),
    )(du1, du1, proj, proj, proj, proj, conv_w)


def _attn_fwd(q_all, kv, kr, H, tq, scale):
    T = q_all.shape[0]
    nq = T // tq
    pair = 2
    W = pair * LANES
    assert H % pair == 0
    hp_n = H // pair

    def body(qn_ref, qr_ref, kn_ref, kr_ref, v_ref, o_ref, lse_ref, *scratch):
        qi = pl.program_id(1)
        chains = [scratch[4 * a:4 * a + 4] for a in range(pair)]
        lanes = [slice(a * LANES, (a + 1) * LANES) for a in range(pair)]
        groups = [slice(c * LANES, (c + 1) * LANES) for c in range(tq // LANES)]

        def fold(x, op):
            r = x[:, groups[0]]
            for gsl in groups[1:]:
                r = op(r, x[:, gsl])
            return r

        for _, m_sc, l_sc, acc_sc in chains:
            m_sc[...] = jnp.full_like(m_sc, MASK_VALUE)
            l_sc[...] = jnp.zeros_like(l_sc)
            acc_sc[...] = jnp.zeros_like(acc_sc)

        def scores(j, masked):
            rows = pl.ds(pl.multiple_of(j * tq, tq), tq)
            krope = kr_ref[rows, :]
            for a, (s_sc, m_sc, _, _) in enumerate(chains):
                q = jnp.concatenate([qn_ref[:, lanes[a]], qr_ref[:, lanes[a]]], axis=1)
                k = jnp.concatenate([kn_ref[rows, lanes[a]], krope], axis=1)
                s = _dot(q, k, NT) * (scale * LOG2_E)
                if masked:
                    row = lax.broadcasted_iota(jnp.int32, s.shape, 0)
                    col = lax.broadcasted_iota(jnp.int32, s.shape, 1)
                    s = jnp.where(col <= row, s, MASK_VALUE)
                s_sc[j] = s
                m_sc[...] = jnp.maximum(m_sc[...], fold(s, jnp.maximum))

        def two_per_trip(fn, count):
            def two(p, carry):
                fn(2 * p)
                fn(2 * p + 1)
                return carry

            lax.fori_loop(0, count // 2, two, 0)

            @pl.when(count % 2 == 1)
            def _():
                fn(count - 1)

        two_per_trip(functools.partial(scores, masked=False), qi)
        scores(qi, True)
        for _, m_sc, _, _ in chains:
            m_sc[...] = jnp.broadcast_to(jnp.max(m_sc[...], axis=1, keepdims=True), m_sc.shape)

        def weigh(j):
            rows = pl.ds(pl.multiple_of(j * tq, tq), tq)
            for a, (s_sc, m_sc, l_sc, acc_sc) in enumerate(chains):
                s, m = s_sc[j], m_sc[...]
                p = jnp.concatenate([jnp.exp2(s[:, gsl] - m) for gsl in groups], axis=1)
                l_sc[...] += fold(p, jnp.add)
                acc_sc[...] += _dot(p, v_ref[rows, lanes[a]], NN)

        two_per_trip(weigh, qi + 1)
        for a, (_, m_sc, l_sc, acc_sc) in enumerate(chains):
            l = jnp.sum(l_sc[...], axis=1, keepdims=True)
            o_ref[:, lanes[a]] = (acc_sc[...] / l).astype(BF16)
            lse_ref[a] = m_sc[:, 0:1] * (1.0 / LOG2_E) + jnp.log(l)

    chain_scratch = [pltpu.VMEM((nq, tq, tq), F32), pltpu.VMEM((tq, LANES), F32), pltpu.VMEM((tq, LANES), F32),
                     pltpu.VMEM((tq, LANES), F32)]
    return pl.pallas_call(
        body, name="attn_fwd", grid=(hp_n, nq),
        in_specs=[pl.BlockSpec((tq, W), lambda hp, qi: (qi, hp)),
                  pl.BlockSpec((tq, W), lambda hp, qi: (qi, hp_n + hp)),
                  pl.BlockSpec((T, W), lambda hp, qi: (0, hp)),
                  pl.BlockSpec((T, LANES), lambda hp, qi: (0, 0)),
                  pl.BlockSpec((T, W), lambda hp, qi: (0, hp_n + hp))],
        out_specs=[pl.BlockSpec((tq, W), lambda hp, qi: (qi, hp)),
                   pl.BlockSpec((pair, tq, 1), lambda hp, qi: (hp, qi, 0))],
        out_shape=[jax.ShapeDtypeStruct((T, H * LANES), BF16), jax.ShapeDtypeStruct((H, T, 1), F32)],
        scratch_shapes=chain_scratch * pair, compiler_params=_params(2),
    )(q_all, q_all, kv, kr, kv)


def _attn_bwd(q_all, kv, kr, do, o, lse, cos, sin, H, tq, scale):
    T = q_all.shape[0]
    nq = T // tq
    HV = H * LANES
    pair = 2
    tk2 = pair * tq
    ng = T // tk2
    assert ng * tk2 == T and pair == 2

    def body(qn_ref, qr_ref, kn_ref, kr_ref, v_ref, do_ref, o_ref, lse_ref, cos_ref, sin_ref,
             dqn_ref, dqr_ref, dkn_ref, dkr_ref, dv_ref, dq_sc, dk_sc, dv_sc):
        g = pl.program_id(1)

        @pl.when(g == 0)
        def _():
            dq_sc[...] = jnp.zeros_like(dq_sc)

        key_rows = [slice(c * tq, (c + 1) * tq) for c in range(pair)]

        def block(qi, modes):
            rows = pl.ds(pl.multiple_of(qi * tq, tq), tq)
            q = jnp.concatenate([qn_ref[rows, :], qr_ref[rows, :]], axis=1)
            dov = do_ref[rows, :]
            delta = jnp.sum(dov.astype(F32) * o_ref[rows, :].astype(F32), axis=1, keepdims=True)
            lse_q = lse_ref[0, rows, :]
            dq, dkv = None, []
            for kr_, masked in zip(key_rows, modes):
                if masked is None:
                    dkv.append(None)
                    continue
                k = jnp.concatenate([kn_ref[kr_, :], kr_ref[kr_, :]], axis=1)
                s = _dot(q, k, NT) * scale
                if masked:
                    row = lax.broadcasted_iota(jnp.int32, s.shape, 0)
                    col = lax.broadcasted_iota(jnp.int32, s.shape, 1)
                    s = jnp.where(col <= row, s, MASK_VALUE)
                p = jnp.exp(s - lse_q)
                dv = _dot(p, dov, TN)
                dp = _dot(dov, v_ref[kr_, :], NT)
                ds = (p * (dp - delta) * scale).astype(BF16)
                dkv.append((_dot(ds, q, TN), dv))
                part = _dot(ds, k, NN)
                dq = part if dq is None else dq + part
            return rows, dq, dkv

        rows_a, dq_a, (kv_a0, _) = block(pair * g, (True, None))
        rows_b, dq_b, (kv_b0, kv_b1) = block(pair * g + 1, (False, True))
        dk_sc[key_rows[0], :] = kv_a0[0] + kv_b0[0]
        dv_sc[key_rows[0], :] = kv_a0[1] + kv_b0[1]
        dk_sc[key_rows[1], :] = kv_b1[0]
        dv_sc[key_rows[1], :] = kv_b1[1]
        dq_sc[rows_a, :] += dq_a
        dq_sc[rows_b, :] += dq_b

        def below(trip, carry):
            for qi in (pair * g + pair + 2 * trip, pair * g + pair + 2 * trip + 1):
                rows, dq, dkv = block(qi, (False, False))
                for kr_, (dk, dv) in zip(key_rows, dkv):
                    dk_sc[kr_, :] += dk
                    dv_sc[kr_, :] += dv
                dq_sc[rows, :] += dq
            return carry

        lax.fori_loop(0, (nq - pair * g - pair) // 2, below, 0)
        dkn_ref[...] = dk_sc[:, :LANES].astype(BF16)
        dkr_ref[...] = dk_sc[:, LANES:].astype(BF16)
        dv_ref[...] = dv_sc[...].astype(BF16)

        @pl.when(g == ng - 1)
        def _():
            dqn_ref[...] = dq_sc[:, :LANES].astype(BF16)
            dqr_ref[...] = _rope(dq_sc[:, LANES:], cos_ref[...], sin_ref[...], transpose=True).astype(BF16)

    whole = lambda col: pl.BlockSpec((T, LANES), col)
    tile = lambda col: pl.BlockSpec((tk2, LANES), col)
    return pl.pallas_call(
        body, name="attn_bwd", grid=(H, ng),
        in_specs=[whole(lambda h, g: (0, h)), whole(lambda h, g: (0, H + h)),
                  tile(lambda h, g: (g, h)), tile(lambda h, g: (g, 0)), tile(lambda h, g: (g, H + h)),
                  whole(lambda h, g: (0, h)), whole(lambda h, g: (0, h)),
                  pl.BlockSpec((1, T, 1), lambda h, g: (h, 0, 0)),
                  whole(lambda h, g: (0, 0)), whole(lambda h, g: (0, 0))],
        out_specs=[whole(lambda h, g: (0, h)), whole(lambda h, g: (0, h)),
                   tile(lambda h, g: (g, h)), tile(lambda h, g: (g, h)), tile(lambda h, g: (g, h))],
        out_shape=[jax.ShapeDtypeStruct((T, HV), BF16), jax.ShapeDtypeStruct((T, HV), BF16),
                   jax.ShapeDtypeStruct((T, HV), BF16), jax.ShapeDtypeStruct((T, HV), BF16),
                   jax.ShapeDtypeStruct((T, HV), BF16)],
        scratch_shapes=[pltpu.VMEM((T, 2 * LANES), F32), pltpu.VMEM((tk2, 2 * LANES), F32),
                        pltpu.VMEM((tk2, LANES), F32)],
        compiler_params=_params(2),
    )(q_all, q_all, kv, kr, kv, do, o, lse, cos, sin)


def _key_rope_bwd(dkr_heads, cos, sin, H, tm):
    T, HV = dkr_heads.shape

    def body(dkr_ref, cos_ref, sin_ref, dk_ref):
        dk = dkr_ref[...].astype(F32)
        tot = dk[:, 0:LANES]
        for h in range(1, H):
            tot = tot + dk[:, h * LANES:(h + 1) * LANES]
        dk_ref[...] = _rope(tot, cos_ref[...], sin_ref[...], transpose=True)

    return pl.pallas_call(
        body, name="key_rope_bwd", grid=(T // tm,),
        in_specs=[pl.BlockSpec((tm, HV), lambda i: (i, 0)),
                  pl.BlockSpec((tm, LANES), lambda i: (i, 0)), pl.BlockSpec((tm, LANES), lambda i: (i, 0))],
        out_specs=pl.BlockSpec((tm, LANES), lambda i: (i, 0)),
        out_shape=jax.ShapeDtypeStruct((T, LANES), F32), compiler_params=_params(1),
    )(dkr_heads, cos, sin)


def _place():
    x, y, c = lax.axis_index("x"), lax.axis_index("y"), lax.axis_index("c")
    chips = [(1 - x, y), (x, 1 - y), (1 - x, 1 - y)]
    return x, y, c, chips


def _all_gather_chips(arrs):
    n = len(arrs)
    halves = [a.shape[0] // 2 for a in arrs]
    assert all(h * 2 == a.shape[0] and h % BF16_ROWS == 0 for h, a in zip(halves, arrs))

    def body(*refs):
        w_refs, out_refs = refs[:n], refs[n:2 * n]
        send_sems, recv_sems, local_sems = refs[2 * n:]
        x, y, c, chips = _place()
        sibling = (x, y, 1 - c)
        waits = []
        for w, (w_ref, out_ref, half) in enumerate(zip(w_refs, out_refs, halves)):
            def region(px, py, pc, out_ref=out_ref, half=half):
                return out_ref.at[2 * px + py, pl.ds(pc * half, half), :]

            def copy(k, block, to, src=None, w=w, region=region):
                return pltpu.make_async_remote_copy(
                    src_ref=region(*block) if src is None else src, dst_ref=region(*block),
                    send_sem=send_sems.at[6 * w + k], recv_sem=recv_sems.at[6 * w + k],
                    device_id=to, device_id_type=MESH)

            mine = pltpu.make_async_copy(w_ref, out_ref.at[2 * x + y], local_sems.at[w])
            mine.start()
            my_half = w_ref.at[pl.ds(c * half, half), :]
            first = [copy(j, (x, y, c), (*chip, c), src=my_half) for j, chip in enumerate(chips)]
            for cp in first:
                cp.start()
            waits.append((copy, mine, first))
        for copy, mine, first in waits:
            passed = [copy(3 + j, (*chip, c), sibling) for j, chip in enumerate(chips)]
            for j, chip in enumerate(chips):
                copy(j, (*chip, c), (x, y, c)).wait_recv()
                passed[j].start()
            for j, chip in enumerate(chips):
                copy(3 + j, (*chip, 1 - c), (x, y, c)).wait_recv()
            for cp in first + passed:
                cp.wait_send()
            mine.wait()

    return pl.pallas_call(
        body, name="gather_weights", in_specs=[ANY] * n, out_specs=[ANY] * n,
        out_shape=[jax.ShapeDtypeStruct((N_CHIPS,) + a.shape, a.dtype) for a in arrs],
        scratch_shapes=[pltpu.SemaphoreType.DMA((6 * n,)), pltpu.SemaphoreType.DMA((6 * n,)),
                        pltpu.SemaphoreType.DMA((n,))],
    )(*arrs)


def _swap_cores(name, parts):
    n = len(parts)

    def body(*refs):
        p_refs, r_refs = refs[:n], refs[n:2 * n]
        send_sems, recv_sems = refs[2 * n:]
        x, y, c, _ = _place()
        copies = [pltpu.make_async_remote_copy(
            src_ref=p_refs[w], dst_ref=r_refs[w], send_sem=send_sems.at[w], recv_sem=recv_sems.at[w],
            device_id=(x, y, 1 - c), device_id_type=MESH) for w in range(n)]
        for cp in copies:
            cp.start()
        for cp in copies:
            cp.wait()

    return pl.pallas_call(
        body, name=name, in_specs=[ANY] * n, out_specs=[ANY] * n,
        out_shape=[jax.ShapeDtypeStruct(p.shape, p.dtype) for p in parts],
        scratch_shapes=[pltpu.SemaphoreType.DMA((n,)), pltpu.SemaphoreType.DMA((n,))],
    )(*parts)


def _chip_sum_of_my_half(name, a, core):
    _, rows, cols = a.shape
    half = rows // 2
    tr = _gcd(ADAM_ROWS, half)
    steps = half // tr

    def swap_body(a_ref, r_ref, send_sem, recv_sem):
        x, y, c, _ = _place()
        cp = pltpu.make_async_remote_copy(
            src_ref=a_ref.at[:, pl.ds((1 - c) * half, half), :], dst_ref=r_ref, send_sem=send_sem,
            recv_sem=recv_sem, device_id=(x, y, 1 - c), device_id_type=MESH)
        cp.start()
        cp.wait()

    theirs = pl.pallas_call(
        swap_body, name=name + "_swap", in_specs=[ANY], out_specs=ANY,
        out_shape=jax.ShapeDtypeStruct((1, half, cols), a.dtype),
        scratch_shapes=[pltpu.SemaphoreType.DMA, pltpu.SemaphoreType.DMA],
    )(a)

    def add_body(core_ref, a_ref, r_ref, o_ref):
        o_ref[...] = (a_ref[...].astype(F32) + r_ref[...].astype(F32)).astype(o_ref.dtype)

    return pl.pallas_call(
        add_body, name=name + "_add",
        grid_spec=pltpu.PrefetchScalarGridSpec(
            num_scalar_prefetch=1, grid=(steps,),
            in_specs=[pl.BlockSpec((1, tr, cols), lambda i, c: (0, c[0] * steps + i, 0)),
                      pl.BlockSpec((1, tr, cols), lambda i, c: (0, i, 0))],
            out_specs=pl.BlockSpec((1, tr, cols), lambda i, c: (0, i, 0))),
        out_shape=jax.ShapeDtypeStruct((1, half, cols), a.dtype), compiler_params=_params(1),
    )(core, a, theirs)


HBM = pl.BlockSpec(memory_space=pltpu.HBM)
SEM = pl.BlockSpec(memory_space=pltpu.SEMAPHORE)
EFFECT = pltpu.SideEffectType.DATAFLOW_SIDE_EFFECTING


def _push_copies(a_refs, l_refs, send_sems, recv_sems, by_target):
    x, y, c, chips = _place()
    me = 2 * x + y

    def part(a_ref, q):
        if by_target == 'cols':
            n = a_ref.shape[-1] // N_CHIPS
            return a_ref.at[(slice(None),) * (len(a_ref.shape) - 1) + (pl.ds(pl.multiple_of(q * n, LANES), n),)]
        return a_ref.at[q] if by_target else a_ref

    out = []
    for w, (a_ref, l_ref) in enumerate(zip(a_refs, l_refs)):
        for j, (px, py) in enumerate(chips):
            peer = 2 * px + py
            out.append((
                pltpu.make_async_remote_copy(
                    src_ref=part(a_ref, peer), dst_ref=l_ref.at[me],
                    send_sem=send_sems.at[3 * w + j], recv_sem=recv_sems.at[3 * w + j],
                    device_id=(px, py, c), device_id_type=MESH),
                pltpu.make_async_remote_copy(
                    src_ref=part(a_ref, me), dst_ref=l_ref.at[peer],
                    send_sem=send_sems.at[3 * w + j], recv_sem=recv_sems.at[3 * w + j],
                    device_id=(px, py, c), device_id_type=MESH)))
    return out


def _landing_shape(a, by_target):
    if by_target == 'cols':
        return (N_CHIPS,) + a.shape[:-1] + (a.shape[-1] // N_CHIPS,)
    return (N_CHIPS,) + (a.shape[1:] if by_target else a.shape)


def _push_start(name, arrs, by_target):
    n = len(arrs)
    lands = [lax.empty(_landing_shape(a, by_target), a.dtype) for a in arrs]

    def body(*refs):
        a_refs, l_refs = refs[:n], refs[n:2 * n]
        send_sems, recv_sems = refs[2 * n], refs[2 * n + 1]
        token = refs[-1]
        for send, _ in _push_copies(a_refs, l_refs, send_sems, recv_sems, by_target):
            send.start()
        token[...] = jnp.zeros_like(token)

    res = pl.pallas_call(
        body, name=name,
        out_shape=(pltpu.SemaphoreType.DMA((3 * n,)), pltpu.SemaphoreType.DMA((3 * n,)),
                   *[pltpu.HBM(a.shape, a.dtype) for a in arrs], *[pltpu.HBM(l.shape, l.dtype) for l in lands],
                   jax.ShapeDtypeStruct((8, LANES), F32)),
        in_specs=[HBM] * (2 * n), out_specs=(SEM, SEM, *[HBM] * (2 * n), pl.BlockSpec(memory_space=pltpu.VMEM)),
        input_output_aliases={i: 2 + i for i in range(2 * n)},
        compiler_params=pltpu.CompilerParams(has_side_effects=EFFECT),
    )(*[pltpu.with_memory_space_constraint(a, pltpu.HBM) for a in list(arrs) + lands])
    return res[0], res[1], list(res[2:2 + n]), list(res[2 + n:2 + 2 * n]), res[-1]


def _push_wait(name, send_sems, recv_sems, arrs, lands, after, by_target):
    n = len(arrs)

    def body(*refs):
        a_refs, l_refs = refs[:n], refs[n:2 * n]
        s_sems, r_sems = refs[2 * n], refs[2 * n + 1]
        for send, recv in _push_copies(a_refs, l_refs, s_sems, r_sems, by_target):
            send.wait_send()
            recv.wait_recv()

    res = pl.pallas_call(
        body, name=name,
        out_shape=[pltpu.HBM(a.shape, a.dtype) for a in list(arrs) + list(lands)],
        in_specs=[HBM] * (2 * n) + [SEM, SEM] + [ANY] * len(after), out_specs=[HBM] * (2 * n),
        input_output_aliases={i: i for i in range(2 * n)},
        compiler_params=pltpu.CompilerParams(has_side_effects=EFFECT),
    )(*arrs, *lands, send_sems, recv_sems, *after)
    return list(res[:n]), list(res[n:])


def _all_reduce_small(part):
    def body(p_ref, out_ref, sib_buf, chip_buf, send_sems, recv_sems):
        x, y, c, chips = _place()
        me = 2 * x + y
        swap = pltpu.make_async_remote_copy(
            src_ref=p_ref, dst_ref=sib_buf, send_sem=send_sems.at[0], recv_sem=recv_sems.at[0],
            device_id=(x, y, 1 - c), device_id_type=MESH)
        swap.start()
        swap.wait()
        chip_buf[me] = p_ref[...] + sib_buf[...]
        copies = []
        for j, (px, py) in enumerate(chips):
            cp = pltpu.make_async_remote_copy(
                src_ref=chip_buf.at[me], dst_ref=chip_buf.at[me], send_sem=send_sems.at[1 + j],
                recv_sem=recv_sems.at[1 + j], device_id=(px, py, c), device_id_type=MESH)
            cp.start()
            copies.append(cp)
        for j, (px, py) in enumerate(chips):
            pltpu.make_async_remote_copy(
                src_ref=chip_buf.at[me], dst_ref=chip_buf.at[2 * px + py], send_sem=send_sems.at[1 + j],
                recv_sem=recv_sems.at[1 + j], device_id=(px, py, c), device_id_type=MESH).wait_recv()
        for cp in copies:
            cp.wait_send()
        tot = chip_buf[0]
        for q in range(1, N_CHIPS):
            tot = tot + chip_buf[q]
        out_ref[...] = tot

    vm = pl.BlockSpec(memory_space=pltpu.VMEM)
    return pl.pallas_call(
        body, name="all_reduce_small", in_specs=[vm], out_specs=vm,
        out_shape=jax.ShapeDtypeStruct(part.shape, F32),
        scratch_shapes=[pltpu.VMEM(part.shape, F32), pltpu.VMEM((N_CHIPS,) + part.shape, F32),
                        pltpu.SemaphoreType.DMA((N_CHIPS,)), pltpu.SemaphoreType.DMA((N_CHIPS,))],
    )(part)


def _row_tiles(shape):
    ax = next(d for d, s in enumerate(shape) if s > 1)
    tr = _gcd(ADAM_ROWS, shape[ax])
    block = tuple(tr if d == ax else s for d, s in enumerate(shape))
    return shape[ax] // tr, block, lambda i: tuple(i if d == ax else 0 for d in range(len(shape)))


def _sum_chips(name, landed, sent, chip, by_target):
    shape = landed.shape[1:]
    steps, block, index = _row_tiles(shape)
    if by_target == 'cols':
        own_spec = pl.BlockSpec(block, lambda i, c: index(i)[:-1] + (c[0],))
    else:
        own_spec = pl.BlockSpec((1,) + block, lambda i, c: (c[0],) + index(i))

    def body(chip_ref, l_ref, s_ref, o_ref):
        own = (s_ref[...] if by_target == 'cols' else s_ref[0]).astype(F32)
        tot = None
        for q in range(N_CHIPS):
            term = jnp.where(chip_ref[0] == q, own, l_ref[q].astype(F32))
            tot = term if tot is None else tot + term
        o_ref[...] = tot

    return pl.pallas_call(
        body, name=name,
        grid_spec=pltpu.PrefetchScalarGridSpec(
            num_scalar_prefetch=1, grid=(steps,),
            in_specs=[pl.BlockSpec((N_CHIPS,) + block, lambda i, c: (0,) + index(i)), own_spec],
            out_specs=pl.BlockSpec(block, lambda i, c: index(i))),
        out_shape=jax.ShapeDtypeStruct(shape, F32), compiler_params=_params(1),
    )(chip, landed, sent)


def _adamw_math(g, w, m, v):
    mn = ADAM_B1 * m + (1.0 - ADAM_B1) * g
    vn = ADAM_B2 * v + (1.0 - ADAM_B2) * jnp.square(g)
    m_hat = mn / (1.0 - ADAM_B1 ** ADAM_STEP)
    v_hat = vn / (1.0 - ADAM_B2 ** ADAM_STEP)
    return -ADAM_LR * (m_hat / (jnp.sqrt(v_hat) + ADAM_EPS) + ADAM_WD * w), mn, vn


def _adamw(name, g_parts, w, m, v):
    steps, block, index = _row_tiles(w.shape)
    n = len(g_parts)

    def body(*refs):
        g = refs[0][...]
        for r in refs[1:n]:
            g = g + r[...]
        w_ref, m_ref, v_ref, go_ref, d_ref, mo_ref, vo_ref = refs[n:]
        go_ref[...] = g
        d_ref[...], mo_ref[...], vo_ref[...] = _adamw_math(g, w_ref[...], m_ref[...], v_ref[...])

    spec = pl.BlockSpec(block, index)
    return pl.pallas_call(
        body, name=name, grid=(steps,), in_specs=[spec] * (n + 3), out_specs=[spec] * 4,
        out_shape=[jax.ShapeDtypeStruct(w.shape, F32)] * 4, compiler_params=_params(1),
    )(*g_parts, w, m, v)


def _adamw_row_halves(name, mine, theirs, core, w, m, v):
    _, rows, cols = w.shape
    tr = _gcd(ADAM_ROWS, rows // 2)
    per_half = rows // 2 // tr
    whole = pl.BlockSpec((1, tr, cols), lambda i, c: (0, i, 0))
    part = pl.BlockSpec((1, tr, cols), lambda i, c: (0, i % per_half, 0))

    def body(core_ref, mine_ref, theirs_ref, w_ref, m_ref, v_ref, go_ref, d_ref, mo_ref, vo_ref):
        in_my_half = pl.program_id(0) // per_half == core_ref[0]
        g = jnp.where(in_my_half, mine_ref[...], theirs_ref[...])
        go_ref[...] = g
        d_ref[...], mo_ref[...], vo_ref[...] = _adamw_math(g, w_ref[...], m_ref[...], v_ref[...])

    return pl.pallas_call(
        body, name=name,
        grid_spec=pltpu.PrefetchScalarGridSpec(
            num_scalar_prefetch=1, grid=(2 * per_half,), in_specs=[part, part, whole, whole, whole],
            out_specs=[whole] * 4),
        out_shape=[jax.ShapeDtypeStruct(w.shape, F32)] * 4, compiler_params=_params(1),
    )(core, mine, theirs, w, m, v)


def _adamw_vectors(items):
    n = len(items)

    def body(*refs):
        ins, outs = refs[:4 * n], refs[4 * n:]
        for k in range(n):
            g, w, m, v = (r[...] for r in ins[4 * k:4 * k + 4])
            outs[3 * k][...], outs[3 * k + 1][...], outs[3 * k + 2][...] = _adamw_math(g, w, m, v)

    vm = pl.BlockSpec(memory_space=pltpu.VMEM)
    res = pl.pallas_call(
        body, name="adamw_vectors", in_specs=[vm] * (4 * n), out_specs=[vm] * (3 * n),
        out_shape=[jax.ShapeDtypeStruct(it[1].shape, F32) for it in items for _ in range(3)],
    )(*[a for it in items for a in it])
    return [res[3 * k:3 * k + 3] for k in range(n)]


def _pack_rows(flat, dtype, multiple):
    n = flat.shape[0]
    total = -(-n // multiple) * multiple
    return jnp.pad(flat, (0, total - n)).astype(dtype).reshape(total // LANES, LANES)


def kernel(x, positions, ln_g, ln_b, a_w_in, a_b_in, a_conv_w, a_conv_b, a_norm_g, a_norm_b, a_w_out, a_b_out, kv_w_down, kv_norm_g, kv_w_uk, kv_w_uv, b_w_in, b_q_norm_g, b_w_uq, b_w_out, loss_target, m_ln_g, m_ln_b, m_a_w_in, m_a_b_in, m_a_conv_w, m_a_conv_b, m_a_norm_g, m_a_norm_b, m_a_w_out, m_a_b_out, m_kv_w_down, m_kv_norm_g, m_kv_w_uk, m_kv_w_uv, m_b_w_in, m_b_q_norm_g, m_b_w_uq, m_b_w_out, v_ln_g, v_ln_b, v_a_w_in, v_a_b_in, v_a_conv_w, v_a_conv_b, v_a_norm_g, v_a_norm_b, v_a_w_out, v_a_b_out, v_kv_w_down, v_kv_norm_g, v_kv_w_uk, v_kv_w_uv, v_b_w_in, v_b_q_norm_g, v_b_w_uq, v_b_w_out):
    T, D = x.shape[1], x.shape[2]
    E = N_CHIPS * a_w_out.shape[1]
    RKV = kv_norm_g.shape[0]
    H, DN = kv_w_uk.shape[1], kv_w_uk.shape[2]
    RQ = b_q_norm_g.shape[1]
    HV = N_CHIPS * b_w_out.shape[1]
    assert DN == LANES and kv_w_uv.shape[2] == LANES and HV == H * LANES
    assert kv_w_down.shape[1] == RKV + ROPE_DIM and b_w_uq.shape[3] == DN + ROPE_DIM
    assert ln_g.shape[0] == 2 and a_w_in.shape[0] == 1 and b_w_in.shape[0] == 1
    alpha = (2.0 * ln_g.shape[0]) ** 0.25
    scale = 1.0 / math.sqrt(DN + ROPE_DIM)
    WK = -(-(RKV + LANES) // 256) * 256
    assert WK % RQ == 0
    Z_OFF = WK + RQ
    tmw, tq = min(TM_WIDE, T), min(TQ, T)
    t512, t1024 = _fit(512, T), _fit(1024, T)
    xs = x[0]
    tgt = loss_target[0]
    px, py = lax.axis_index("x"), lax.axis_index("y")
    chip = 2 * px + py

    mats = [a_w_out[0], kv_w_down, kv_w_uk, kv_w_uv, b_w_in[0], b_w_uq[0], b_w_out[0]]
    vecs = [a_b_in[0], a_conv_w[0], a_conv_b[0], a_norm_g[0], a_norm_b[0], a_b_out[0]]
    vec_bits = jnp.concatenate([lax.bitcast_convert_type(w.reshape(-1), BF16).reshape(-1) for w in vecs])
    rest = [w.astype(BF16) for w in mats]
    g_win, gathered = _all_gather_chips(
        [a_w_in[0].astype(BF16), _pack_rows(vec_bits, BF16, 2 * BF16_ROWS * LANES)])
    gathered = gathered.reshape(N_CHIPS, -1)
    gathered, rest = lax.optimization_barrier((gathered, rest))
    rest_sems = _push_start("gather_rest_start", rest, by_target=False)
    off = 0
    fvec = []
    for w in vecs:
        bits = gathered[:, off:off + 2 * w.size].reshape((N_CHIPS,) + w.shape + (2,))
        fvec.append(lax.bitcast_convert_type(bits, F32))
        off += 2 * w.size
    cols = lambda g: jnp.moveaxis(g, 0, -2).reshape(g.shape[1:-1] + (N_CHIPS * g.shape[-1],))
    b_in = cols(fvec[0][:, None, :])
    conv_w = cols(fvec[1])
    conv_b, norm_g, norm_b, b_out = (cols(f[:, None, :]) for f in fvec[2:])
    row = lambda a: a.reshape(1, -1)
    g0, b0, g1, b1 = row(ln_g[0]), row(ln_b[0]), row(ln_g[1]), row(ln_b[1])
    kv_g, q_g = row(kv_norm_g), row(b_q_norm_g[0])
    plain = lambda acc, ins, i, j: [acc]

    b_in = b_in + rest_sems[4][0, 0]
    (proj,) = _row_mm("a_in", [((xs,), None)], g_win, nt=False, tm=t1024, tn=3 * E // N_CHIPS, tk=D,
                      outs=[((T, 3 * E), F32, 'tile')], epi=lambda acc, ins, i, j: [acc + ins[0]],
                      epi_ins=[(b_in, 'col')])
    u1, u4 = _conv_fwd(proj, conv_w, conv_b, norm_g, norm_b, E, tmw)

    rest, landed = _push_wait("gather_rest_wait", *rest_sems[:4], after=[u4], by_target=False)
    g_wout, g_wd, g_uk, g_uv, g_wbin, g_wuq, g_wbout = [
        lax.dynamic_update_slice(l, w[None], (chip,) + (0,) * w.ndim) for w, l in zip(rest, landed)]
    w_out = g_wout.reshape(E, D)
    wd = g_wd.reshape(D, RKV + ROPE_DIM)
    zpad = jnp.zeros((D, ROPE_HALF), BF16)
    wd_p = jnp.concatenate(
        [wd[:, :RKV], wd[:, RKV:RKV + ROPE_HALF], zpad, wd[:, RKV + ROPE_HALF:], zpad,
         jnp.zeros((D, WK - RKV - LANES), BF16)], axis=1)
    w_bin = cols(g_wbin)
    w_z = w_bin[:, RQ:]
    wb_small = jnp.concatenate([wd_p, w_bin[:, :RQ]], axis=1)
    wb_all = jnp.concatenate([wd_p, w_bin], axis=1)
    w_kv = jnp.concatenate([g_uk.reshape(RKV, HV), g_uv.reshape(RKV, HV)], axis=1)
    wuq = g_wuq.reshape(RQ, H, DN + ROPE_DIM)
    zq = jnp.zeros((RQ, H, ROPE_HALF), BF16)
    w_qr = jnp.concatenate([wuq[:, :, DN:DN + ROPE_HALF], zq, wuq[:, :, DN + ROPE_HALF:], zq], axis=2)
    w_q = jnp.concatenate([wuq[:, :, :DN].reshape(RQ, HV), w_qr.reshape(RQ, HV)], axis=1)
    w_bout = g_wbout.reshape(HV, D)

    freqs = ROPE_THETA ** (-jnp.arange(0, ROPE_DIM, 2, dtype=F32) / ROPE_DIM)
    ang = positions[0].astype(F32)[:, None] * freqs
    cs, sn = jnp.cos(ang), jnp.sin(ang)
    ones, zeros = jnp.ones_like(cs), jnp.zeros_like(cs)
    cos_t = jnp.concatenate([cs, ones, cs, ones], axis=1)
    sin_t = jnp.concatenate([-sn, zeros, sn, zeros], axis=1)

    def ln_epi(acc, ins, i, j):
        bias, res, g, b = ins
        xhat, rstd = _ln_stats(alpha * res + acc + bias)
        h = xhat * g + b
        return [h, h, xhat, rstd]

    h1, h1b, xhat1, rstd1 = _row_mm(
        "a_out", [((u4,), None)], w_out, nt=False, tm=t512, tn=D, tk=_fit(2048, E),
        outs=[((T, D), F32, 'tile'), ((T, D), BF16, 'tile'), ((T, D), F32, 'tile'), ((T, 1), F32, 'row')],
        epi=ln_epi, epi_ins=[(b_out, 'col'), (xs, 'tile'), (g0, 'col'), (b0, 'col')])

    tkb = _fit(512, _gcd(WK, RQ, HV))
    def latents_epi(acc, ins, i, j):
        kg, qg, cos_, sin_ = ins
        return [acc, _rms_fwd(acc[:, :RKV], kg), _rope(acc[:, RKV:RKV + LANES], cos_, sin_),
                _rms_fwd(acc[:, WK:WK + RQ], qg)]

    whole_row = lambda a: (a, pl.BlockSpec(a.shape, lambda i, j, k: (0, 0)))
    pb, c_lat, kr, cqn = _row_mm(
        "b_in", [((h1b,), None)], wb_small, nt=False, tm=t512, tn=Z_OFF, tk=_fit(1024, D),
        outs=[((T, Z_OFF), F32, 'tile'), ((T, RKV), BF16, 'row'), ((T, LANES), BF16, 'row'),
              ((T, RQ), BF16, 'row')],
        epi=latents_epi, epi_ins=[whole_row(kv_g), whole_row(q_g), (cos_t, 'row'), (sin_t, 'row')])
    (zb,) = _row_mm("b_in_gate", [((h1b,), None)], w_z, nt=False, tm=t1024, tn=_fit(2048, HV),
                    tk=_fit(1024, D), outs=[((T, HV), BF16, 'tile')], epi=plain)
    (kv,) = _row_mm("kv_up", [((c_lat,), None)], w_kv, nt=False, tm=t1024, tn=_fit(2048, HV),
                    tk=_fit(1024, RKV), outs=[((T, 2 * HV), BF16, 'tile')], epi=plain)
    tnq = _fit(2048, HV)
    half_q = HV // tnq

    def q_epi(acc, ins, i, j):
        return [jnp.where(j >= half_q, _rope(acc, ins[0], ins[1]), acc)]

    (q_all,) = _row_mm("q_up", [((cqn,), None)], w_q, nt=False, tm=t1024, tn=tnq, tk=_fit(1024, RQ),
                       outs=[((T, 2 * HV), BF16, 'tile')], epi=q_epi,
                       epi_ins=[(cos_t, 'row'), (sin_t, 'row')])
    o, lse = _attn_fwd(q_all, kv, kr, H, tq, scale)

    def loss_epi(acc, ins, i, j):
        res, g, b, target = ins
        xhat, rstd = _ln_stats(alpha * res + acc)
        diff = xhat * g + b - target
        dr, dg, db = _ln_bwd(diff / D, xhat, rstd, g)
        return [dr, 0.5 * jnp.sum(diff * diff, keepdims=True) / D, dg, db]

    dr1, loss_part, dg1, db1 = _row_mm(
        "b_out", [((o, zb), _gate)], w_bout, nt=False, tm=t512, tn=D, tk=_fit(2048, HV),
        outs=[((T, D), F32, 'tile'), ((1, 1), F32, 'acc'), ((1, D), F32, 'acc'), ((1, D), F32, 'acc')],
        epi=loss_epi, epi_ins=[(h1, 'tile'), (g1, 'col'), (b1, 'col'), (tgt, 'tile')])

    def gate_bwd_epi(acc, ins, i, j):
        gate, gate_grad = _silu_and_grad(ins[1].astype(F32))
        return [acc * gate, acc * ins[0].astype(F32) * gate_grad]

    do, dz = _row_mm(
        "b_out_bwd", [((dr1,), None)], w_bout, nt=True, tm=t512, tn=_fit(2048, HV), tk=_fit(1024, D),
        outs=[((T, HV), BF16, 'tile'), ((T, HV), BF16, 'tile')], epi=gate_bwd_epi,
        epi_ins=[(o, 'tile'), (zb, 'tile')])
    gw_bout = _tn_mm("dw_b_out", (o, zb), _gate, [((dr1,), None)], tn=_fit(1024, D), tk=t512, out_dtype=BF16)
    dqn, dqr_pre, dkn, dkr_h, dv = _attn_bwd(q_all, kv, kr, do, o, lse, cos_t, sin_t, H, tq, scale)
    dkr_pre = _key_rope_bwd(dkr_h, cos_t, sin_t, H, t512)

    def cq_bwd_epi(acc, ins, i, j):
        dx, dg = _rms_bwd(acc, ins[0], ins[1])
        return [dx, dg]

    dcq, dqg = _row_mm(
        "q_up_bwd", [((dqn,), None), ((dqr_pre,), None)], w_q, nt=True, tm=t1024, tn=RQ, tk=_fit(2048, HV),
        outs=[((T, RQ), BF16, 'tile'), ((1, RQ), F32, 'acc')], epi=cq_bwd_epi,
        epi_ins=[(pb, pl.BlockSpec((t1024, RQ), lambda i, j, k: (i, WK // RQ))), (q_g, 'col')])
    gw_q = _tn_mm("dw_q_up", (cqn,), None, [((dqn,), None), ((dqr_pre,), None)],
                  tn=_fit(2048, HV), tk=t1024, out_dtype=BF16)

    def ckv_bwd_epi(acc, ins, i, j):
        blk, dkr_t, g = ins
        dx, dg = _rms_bwd(acc, blk[:, :RKV], g)
        parts = [dx, dkr_t]
        if WK > RKV + LANES:
            parts.append(jnp.zeros((dx.shape[0], WK - RKV - LANES), F32))
        return [jnp.concatenate(parts, axis=1), dg]

    dckv, dkvg = _row_mm(
        "kv_up_bwd", [((dkn,), None), ((dv,), None)], w_kv, nt=True, tm=t1024, tn=RKV, tk=_fit(2048, HV),
        outs=[((T, WK), BF16, pl.BlockSpec((t1024, WK), lambda i, j, k: (i, 0))), ((1, RKV), F32, 'acc')],
        epi=ckv_bwd_epi,
        epi_ins=[(pb, pl.BlockSpec((t1024, WK), lambda i, j, k: (i, 0))), (dkr_pre, 'row'), (kv_g, 'col')])
    gw_kv = _tn_mm("dw_kv_up", (c_lat,), None, [((dkn,), None), ((dv,), None)],
                   tn=_fit(2048, HV), tk=t1024, out_dtype=BF16)

    def ln1_bwd_epi(acc, ins, i, j):
        dr_up, xhat, rstd, g = ins
        dr, dg, db = _ln_bwd(alpha * dr_up + acc, xhat, rstd, g)
        return [dr, dg, db]

    dp_segs = [((dckv,), None), ((dcq,), None), ((dz,), None)]
    gw_lat = _tn_mm("dw_b_in", (h1b,), None, dp_segs[:2], tn=tkb, tk=t1024, out_dtype=BF16)
    gw_z = _tn_mm("dw_b_in_gate", (h1b,), None, dp_segs[2:], tn=_fit(2048, HV), tk=t1024, out_dtype=BF16)

    shard_cols = lambda g: jnp.moveaxis(g.reshape(g.shape[0], N_CHIPS, -1), 1, 0)
    gq = gw_q.reshape(RQ, 2, H, LANES)
    g_uq = jnp.concatenate(
        [gq[:, 0], gq[:, 1, :, :ROPE_HALF], gq[:, 1, :, 2 * ROPE_HALF:3 * ROPE_HALF]], axis=2)
    g_wd_full = jnp.concatenate(
        [gw_lat[:, :RKV], gw_lat[:, RKV:RKV + ROPE_HALF],
         gw_lat[:, RKV + 2 * ROPE_HALF:RKV + 3 * ROPE_HALF]], axis=1)
    late_names = ["kv_w_down", "kv_w_uk", "kv_w_uv", "b_w_in", "b_w_uq", "b_w_out"]
    late_w = [kv_w_down, kv_w_uk, kv_w_uv, b_w_in, b_w_uq, b_w_out]
    gw_bin = jnp.concatenate([gw_lat[:, WK:], gw_z], axis=1)
    chip_major = [g_wd_full, gw_kv[:, :HV], gw_kv[:, HV:], shard_cols(gw_bin), g_uq, gw_bout]
    late_grads = [g.reshape((N_CHIPS,) + w.shape) for g, w in zip(chip_major, late_w)]
    late_sems = _push_start("scatter_late_start", late_grads, by_target=True)

    dr0, dg0, db0 = _row_mm(
        "b_in_bwd", dp_segs, wb_all, nt=True, tm=t1024, tn=D, tk=tkb,
        outs=[((T, D), F32, 'tile'), ((1, D), F32, 'acc'), ((1, D), F32, 'acc')], epi=ln1_bwd_epi,
        epi_ins=[(dr1, 'tile'), (xhat1, 'tile'), (rstd1, 'row'), (g0 + late_sems[4][0, 0], 'col')])

    def conv_branch_bwd_epi(acc, ins, i, j):
        u1_t, z, g, b = ins
        xhat, rstd = _ln_stats(u1_t)
        u2 = xhat * g + b
        gate, gate_grad = _silu_and_grad(z)
        act, act_grad = _silu_and_grad(u2)
        dz_a = acc * act * gate_grad
        du1, dg, db = _ln_bwd(acc * gate * act_grad, xhat, rstd, g)
        return [du1, dz_a, dg, db]

    du1, dz_a, dng, dnb = _row_mm(
        "a_out_bwd", [((dr0,), None)], w_out, nt=True, tm=tmw, tn=E, tk=_fit(1024, D),
        outs=[((T, E), F32, 'tile'), ((T, E), BF16, 'tile'), ((1, E), F32, 'acc'), ((1, E), F32, 'acc')],
        epi=conv_branch_bwd_epi,
        epi_ins=[(u1, 'tile'), (proj, pl.BlockSpec((tmw, E), lambda i, j, k: (i, 2))), (norm_g, 'col'),
                 (norm_b, 'col')])
    gw_out, dbo = _tn_mm("dw_a_out", (u4,), None, [((dr0,), None)], tn=_fit(1024, D), tk=t1024,
                         out_dtype=BF16, colsum=True)
    mid_sems = _push_start("scatter_mid_start", [gw_out.reshape((N_CHIPS,) + a_w_out.shape)], by_target=True)
    dval, dgate, dcw, dcb = _conv_bwd(du1, proj, conv_w + mid_sems[4][0, 0], E, tmw)
    dproj_segs = [((dval,), None), ((dgate,), None), ((dz_a,), None)]
    gw_in, dbi = _tn_mm("dw_a_in", (xs,), None, dproj_segs, tn=_fit(2048, E), tk=t1024, out_dtype=BF16,
                        colsum=True)

    chip_word = chip.reshape(1).astype(jnp.int32)

    def reduce_and_update(tag, names_, sent, landed, w_, m_, v_, by_target=True):
        sums = [_sum_chips("sum_" + n, l, s, chip_word, by_target) for n, s, l in zip(names_, sent, landed)]
        theirs = _swap_cores("swap_cores_" + tag, sums)
        return {n: _adamw("adamw_" + n, [mine, other], w, m, v)
                for n, mine, other, w, m, v in zip(names_, sums, theirs, w_, m_, v_)}

    late_sent, late_landed = _push_wait("scatter_late_wait", *late_sems[:4], after=[dbi], by_target=True)
    mid_sent, mid_landed = _push_wait("scatter_mid_wait", *mid_sems[:4], after=[dbi], by_target=True)
    core_word = lax.axis_index("c").reshape(1).astype(jnp.int32)
    chip_half = _chip_sum_of_my_half("chip_sum_a_w_in", gw_in.reshape(a_w_in.shape[:-1] + (3 * E,)), core_word)
    early_sems = _push_start("scatter_early_start", [chip_half], by_target='cols')
    (grad_x,) = _row_mm(
        "a_in_bwd", dproj_segs, g_win, nt=True, tm=t512, tn=D, tk=E, b_whole=True,
        outs=[((T, D), F32, 'tile')], epi=lambda acc, ins, i, j: [alpha * ins[0] + acc + ins[1]],
        epi_ins=[(dr0, 'tile'), (jnp.zeros((1, D), F32) + early_sems[4][0, 0], 'col')])
    big_out = reduce_and_update(
        "late", ["a_w_out"] + late_names, mid_sent + late_sent, mid_landed + late_landed,
        [a_w_out] + late_w, [m_a_w_out, m_kv_w_down, m_kv_w_uk, m_kv_w_uv, m_b_w_in, m_b_w_uq, m_b_w_out],
        [v_a_w_out, v_kv_w_down, v_kv_w_uk, v_kv_w_uv, v_b_w_in, v_b_w_uq, v_b_w_out])

    small_full = [jnp.concatenate([dg0, dg1]), jnp.concatenate([db0, db1]), dbi, dcw, dcb, dng, dnb, dbo,
                  dkvg, dqg, loss_part]
    sflat = jnp.concatenate([g.reshape(-1) for g in small_full])
    summed = _all_reduce_small(_pack_rows(sflat, F32, 8 * LANES)).reshape(-1)
    soff = 0
    sgrads = []
    for g in small_full:
        sgrads.append(summed[soff:soff + g.size].reshape(g.shape))
        soff += g.size
    loss = sgrads.pop()[0, 0]
    local_cols = lambda g, n: lax.dynamic_slice_in_dim(g, chip * n, n, axis=g.ndim - 1)
    snames = ["ln_g", "ln_b", "a_b_in", "a_conv_w", "a_conv_b", "a_norm_g", "a_norm_b", "a_b_out",
              "kv_norm_g", "b_q_norm_g"]
    small_w = [ln_g, ln_b, a_b_in, a_conv_w, a_conv_b, a_norm_g, a_norm_b, a_b_out, kv_norm_g, b_q_norm_g]
    small_m = [m_ln_g, m_ln_b, m_a_b_in, m_a_conv_w, m_a_conv_b, m_a_norm_g, m_a_norm_b, m_a_b_out,
               m_kv_norm_g, m_b_q_norm_g]
    small_v = [v_ln_g, v_ln_b, v_a_b_in, v_a_conv_w, v_a_conv_b, v_a_norm_g, v_a_norm_b, v_a_b_out,
               v_kv_norm_g, v_b_q_norm_g]
    sharded = {"a_b_in", "a_conv_w", "a_conv_b", "a_norm_g", "a_norm_b", "a_b_out"}
    local_g = [(local_cols(g, w.shape[-1]) if n in sharded else g).reshape(w.shape)
               for n, g, w in zip(snames, sgrads, small_w)]
    at_least_2d = lambda a: a.reshape((1,) + a.shape) if a.ndim == 1 else a
    sres = _adamw_vectors([tuple(at_least_2d(a) for a in item)
                           for item in zip(local_g, small_w, small_m, small_v)])
    small_out = {n: [g] + [r.reshape(w.shape) for r in res]
                 for n, g, w, res in zip(snames, local_g, small_w, sres)}

    early_sent, early_landed = _push_wait(
        "scatter_early_wait", *early_sems[:4], by_target='cols',
        after=[grad_x, big_out["b_w_out"][1], small_out["b_q_norm_g"][1]])
    my_rows = _sum_chips("sum_a_w_in", early_landed[0], early_sent[0], chip_word, 'cols')
    (sibling_rows,) = _swap_cores("swap_cores_early", [my_rows])
    big_out["a_w_in"] = _adamw_row_halves("adamw_a_w_in", my_rows, sibling_rows, core_word, a_w_in, m_a_w_in,
                                          v_a_w_in)

    order =["ln_g", "ln_b", "a_w_in", "a_b_in", "a_conv_w", "a_conv_b", "a_norm_g", "a_norm_b", "a_w_out",
             "a_b_out", "kv_w_down", "kv_norm_g", "kv_w_uk", "kv_w_uv", "b_w_in", "b_q_norm_g", "b_w_uq",
             "b_w_out"]
    outs = {**big_out, **small_out}
    result = [loss, grad_x[None]]
    for part in range(4):
        result += [outs[n][part] for n in order]
    return tuple(result)
```
